```python
import math
import jax
import jax.numpy as jnp
from jax import lax
import numpy as np

D_MODEL = 2048
BATCH = 2
SEQ = 4096
DEPTH = 2

CHUNK = 64
EPS = 1e-6
MIX_WIDTH = D_MODEL
GROUP_WIDTH = MIX_WIDTH // 2

RET_HEADS = 4
RET_DK = GROUP_WIDTH // RET_HEADS
RET_DV = GROUP_WIDTH // RET_HEADS
ROPE_BASE = 10000.0
SGU_WINDOW = 128
SGU_GROUPS = 4
SGU_DG = GROUP_WIDTH // SGU_GROUPS
HG_HEADS = 8
HG_DK = GROUP_WIDTH // HG_HEADS
HG_DV = GROUP_WIDTH // HG_HEADS
DSA_HEADS = 8
DSA_DV = GROUP_WIDTH // DSA_HEADS
DSA_Q_RANK = 384
DSA_KV_RANK = 256
IDX_HEADS = 16
IDX_DIM = 64
TOPK_MAX = 256
Q_BLOCK = 128
REL_BUCKETS = 32
REL_MAX_DIST = 256
D_FF = ((-(-8 * D_MODEL // 3)) + 255) // 256 * 256

EVEN_IN = 4 * GROUP_WIDTH + 2 * GROUP_WIDTH
ODD_IN = 4 * GROUP_WIDTH + DSA_Q_RANK + DSA_KV_RANK + IDX_DIM + IDX_HEADS

kernel_name = "hybrid_retention_sgu_hgrn2_dsa_trunk"

F32 = jnp.float32


def rms_norm(x, g=None):
    xf = x.astype(F32)
    y = xf * lax.rsqrt(jnp.mean(xf * xf, axis=-1, keepdims=True) + EPS)
    if g is not None:
        y = y * g.astype(F32)
    return y.astype(x.dtype)


def layer_norm(x, g, b):
    xf = x.astype(F32)
    mu = jnp.mean(xf, axis=-1, keepdims=True)
    var = jnp.mean(jnp.square(xf - mu), axis=-1, keepdims=True)
    return ((xf - mu) * lax.rsqrt(var + EPS) * g.astype(F32) + b.astype(F32)).astype(x.dtype)


def rotary(x, pos):
    d = x.shape[-1]
    inv = ROPE_BASE ** (-jnp.arange(0, d, 2, dtype=F32) / d)
    ang = pos.astype(F32)[:, None] * inv[None, :]
    cos = jnp.cos(ang)[None, :, None, :]
    sin = jnp.sin(ang)[None, :, None, :]
    xf = x.astype(F32)
    x1, x2 = xf[..., : d // 2], xf[..., d // 2:]
    return jnp.concatenate([x1 * cos - x2 * sin, x1 * sin + x2 * cos], axis=-1)


def to_chunks(t, nc):
    b, s, h, d = t.shape
    return t.astype(F32).reshape(b, nc, CHUNK, h, d).transpose(1, 0, 3, 2, 4)


def from_chunks(t):
    nc, b, h, c, d = t.shape
    return t.transpose(1, 0, 3, 2, 4).reshape(b, nc * c, h, d)


def retention(q, k, v):
    b, s, h, dk = q.shape
    nc = s // CHUNK
    log_gamma = jnp.log(1.0 - 2.0 ** (-5.0 - jnp.arange(h, dtype=F32)))
    qc = to_chunks(q, nc)
    kc = to_chunks(k * (dk ** -0.5), nc)
    vc = to_chunks(v, nc)
    pos = jnp.arange(CHUNK, dtype=F32)
    d_intra = jnp.exp(log_gamma[:, None, None] * jnp.abs(pos[:, None] - pos[None, :]))
    scores = jnp.einsum('nbhid,nbhjd->nbhij', qc, kc) * d_intra[None, None]
    intra = jnp.einsum('nbhij,nbhje->nbhie', scores, vc)
    xi = jnp.exp(log_gamma[:, None] * (pos + 1.0))[None, :, :, None]
    zeta = jnp.exp(log_gamma[:, None] * (CHUNK - 1.0 - pos))[None, :, :, None]
    g_chunk = jnp.exp(log_gamma * CHUNK)[None, :, None, None]

    def step(state, inp):
        qi, ki, vi = inp
        cross = jnp.einsum('bhid,bhde->bhie', qi, state) * xi
        state = state * g_chunk + jnp.einsum('bhjd,bhje->bhde', ki * zeta, vi)
        return state, cross

    s0 = jnp.zeros((b, h, dk, v.shape[-1]), F32)
    _, cross = lax.scan(step, s0, (qc, kc, vc))
    return from_chunks(intra + cross)


def spatial_gating(u, v, ln_g, ln_b, w_s, b_s):
    b, s, _ = v.shape
    v = layer_norm(v, ln_g, ln_b).astype(F32)
    ch = jnp.arange(SGU_WINDOW) // CHUNK
    mask = ch[None, :] <= ch[:, None]
    w = jnp.where(mask[None], w_s.astype(F32), 0.0)
    vw = v.reshape(b, s // SGU_WINDOW, SGU_WINDOW, SGU_GROUPS, SGU_DG)
    mixed = jnp.einsum('gij,bnjgc->bnigc', w, vw) + b_s.astype(F32).T[None, None, :, :, None]
    return u.astype(F32) * mixed.reshape(b, s, GROUP_WIDTH)


def hgrn2(q, f_logits, i, lower_bound):
    b, s, h, dk = q.shape
    nc = s // CHUNK
    lb = lower_bound.astype(F32).reshape(h, dk)
    f = lb + (1.0 - lb) * jax.nn.sigmoid(f_logits.astype(F32))
    log_f = jnp.log(f)
    k = 1.0 - f
    qa = jax.nn.silu(q.astype(F32))
    qc, kc, lfc, vc = to_chunks(qa, nc), to_chunks(k, nc), to_chunks(log_f, nc), to_chunks(i, nc)
    causal = jnp.tril(jnp.ones((CHUNK, CHUNK), dtype=bool))

    def step(state, inp):
        qi, ki, lfi, vi = inp
        bcum = jnp.cumsum(lfi, axis=2)
        diff = bcum[:, :, :, None, :] - bcum[:, :, None, :, :]
        decay = jnp.exp(jnp.where(causal[:, :, None], diff, -jnp.inf))
        attn = jnp.einsum('bhtd,bhsd,bhtsd->bhts', qi, ki, decay)
        intra = jnp.einsum('bhts,bhse->bhte', attn, vi)
        cross = jnp.einsum('bhtd,bhde->bhte', qi * jnp.exp(bcum), state)
        blast = bcum[:, :, -1, :]
        state = state * jnp.exp(blast)[..., None] + jnp.einsum(
            'bhsd,bhse->bhde', ki * jnp.exp(blast[:, :, None, :] - bcum), vi)
        return state, intra + cross

    s0 = jnp.zeros((b, h, dk, i.shape[-1]), F32)
    _, out = lax.scan(step, s0, (qc, kc, lfc, vc))
    return from_chunks(out)


def rel_bucket(rel):
    nb = REL_BUCKETS // 2
    max_exact = nb // 2
    ret = jnp.where(rel > 0, nb, 0)
    n = jnp.abs(rel)
    nf = jnp.maximum(n, 1).astype(F32)
    large = max_exact + (jnp.log(nf / max_exact) / math.log(REL_MAX_DIST / max_exact)
                         * (nb - max_exact)).astype(jnp.int32)
    large = jnp.minimum(large, nb - 1)
    return ret + jnp.where(n < max_exact, n, large)


def dsa_attention(c_q, c_kv, k_idx, w_idx, cq_g, ckv_g, w_uq, qn_g, w_qidx, w_uv, rel_bias):
    b, s, _ = c_q.shape
    n_blk = s // Q_BLOCK
    k_sel = min(TOPK_MAX, s // 4)
    cq = rms_norm(c_q, cq_g).astype(F32)
    q = (cq @ w_uq.astype(F32)).reshape(b, s, DSA_HEADS, DSA_KV_RANK)
    q = rms_norm(q, qn_g)
    kv = rms_norm(c_kv, ckv_g).astype(F32)
    q_idx = (cq @ w_qidx.astype(F32)).reshape(b, s, IDX_HEADS, IDX_DIM)
    kix = k_idx.astype(F32)
    w_h = w_idx.astype(F32) * (IDX_HEADS ** -0.5)
    key_chunk = jnp.arange(s, dtype=jnp.int32) // CHUNK

    def blocks(t):
        return t.reshape((b, n_blk, Q_BLOCK) + t.shape[2:]).swapaxes(0, 1)

    def one_block(inp):
        qb, qib, wb, blk = inp
        t = blk * Q_BLOCK + jnp.arange(Q_BLOCK, dtype=jnp.int32)
        sc = jnp.einsum('bqhd,bsd->bqhs', qib, kix) * (IDX_DIM ** -0.5)
        sc = jnp.einsum('bqh,bqhs->bqs', wb, jax.nn.relu(sc))
        admissible = key_chunk[None, :] <= (t // CHUNK)[:, None]
        sc = jnp.where(admissible[None], sc, -jnp.inf)
        _, idx = lax.top_k(sc, k_sel)
        valid = (idx // CHUNK) <= (t // CHUNK)[None, :, None]
        kv_sel = jnp.take_along_axis(kv, idx.reshape(b, -1)[..., None], axis=1)
        kv_sel = kv_sel.reshape(b, Q_BLOCK, k_sel, DSA_KV_RANK)
        logits = jnp.einsum('bqhr,bqkr->bqhk', qb.astype(F32), kv_sel) * (DSA_KV_RANK ** -0.5)
        bias = rel_bias.astype(F32)[rel_bucket(idx - t[None, :, None])]
        logits = logits + bias.transpose(0, 1, 3, 2)
        logits = jnp.where(valid[:, :, None, :], logits, -jnp.inf)
        p = jax.nn.softmax(logits, axis=-1)
        return jnp.einsum('bqhk,bqkr->bqhr', p, kv_sel)

    o = lax.map(one_block, (blocks(q), blocks(q_idx), blocks(w_h), jnp.arange(n_blk, dtype=jnp.int32)))
    o = o.swapaxes(0, 1).reshape(b, s, DSA_HEADS, DSA_KV_RANK)
    return jnp.einsum('bshr,hrd->bshd', o, w_uv.astype(F32)).reshape(b, s, GROUP_WIDTH)


def even_mixer(h, pos, w_in, w_out, sgu_ln_g, sgu_ln_b, sgu_w_s, sgu_b_s):
    b, s, _ = h.shape
    z = h @ w_in
    q, k, v, g, u, vs = jnp.split(z, 6, axis=-1)
    q = rotary(q.reshape(b, s, RET_HEADS, RET_DK), pos)
    k = rotary(k.reshape(b, s, RET_HEADS, RET_DK), pos)
    ret = retention(q, k, v.reshape(b, s, RET_HEADS, RET_DV))
    ret = rms_norm(ret).reshape(b, s, GROUP_WIDTH) * jax.nn.silu(g.astype(F32))
    sgu = spatial_gating(jax.nn.gelu(u, approximate=False), jax.nn.gelu(vs, approximate=False),
                         sgu_ln_g, sgu_ln_b, sgu_w_s, sgu_b_s)
    mixed = jnp.concatenate([ret.astype(h.dtype), sgu.astype(h.dtype)], axis=-1)
    return mixed @ w_out


def odd_mixer(h, lb, w_in, w_out, hgrn_norm_g, cq_g, ckv_g, w_uq, qn_g, w_qidx, w_uv, rel_bias):
    b, s, _ = h.shape
    z = h @ w_in
    offs = [int(o) for o in np.cumsum([GROUP_WIDTH] * 4 + [DSA_Q_RANK, DSA_KV_RANK, IDX_DIM])]
    hq, hf, hi, hg, c_q, c_kv, k_idx, w_idx = jnp.split(z, offs, axis=-1)
    hg_out = hgrn2(hq.reshape(b, s, HG_HEADS, HG_DK), hf.reshape(b, s, HG_HEADS, HG_DK),
                   hi.reshape(b, s, HG_HEADS, HG_DV), lb)
    hg_out = rms_norm(hg_out, hgrn_norm_g.reshape(HG_HEADS, HG_DV)).reshape(b, s, GROUP_WIDTH)
    hg_out = hg_out * jax.nn.silu(hg.astype(F32))
    attn = dsa_attention(c_q, c_kv, k_idx, w_idx, cq_g, ckv_g, w_uq, qn_g, w_qidx, w_uv, rel_bias)
    mixed = jnp.concatenate([hg_out.astype(h.dtype), attn.astype(h.dtype)], axis=-1)
    return mixed @ w_out


def swiglu(h, wg, wu, wd):
    return (jax.nn.silu(h @ wg) * (h @ wu)) @ wd


def setup_inputs(seed: int = 0) -> dict:
    key = jax.random.key(seed)
    ks = iter(jax.random.split(key, 32))
    n_even = (DEPTH + 1) // 2
    n_odd = DEPTH // 2

    def nrm(shape, scale):
        return jax.random.normal(next(ks), shape, F32) * scale

    def gain(shape):
        return 1.0 + 0.01 * jax.random.normal(next(ks), shape, F32)

    return {
        "x": nrm((BATCH, SEQ, D_MODEL), 1.0),
        "ln_mix_g": gain((DEPTH, D_MODEL)),
        "ln_ffn_g": gain((DEPTH, D_MODEL)),
        "w_ffn_gate": nrm((DEPTH, D_MODEL, D_FF), D_MODEL ** -0.5),
        "w_ffn_up": nrm((DEPTH, D_MODEL, D_FF), D_MODEL ** -0.5),
        "w_ffn_down": nrm((DEPTH, D_FF, D_MODEL), D_FF ** -0.5),
        "rel_bias": nrm((REL_BUCKETS, DSA_HEADS), 0.2),
        "ev_w_in": nrm((n_even, D_MODEL, EVEN_IN), D_MODEL ** -0.5),
        "ev_w_out": nrm((n_even, MIX_WIDTH, D_MODEL), MIX_WIDTH ** -0.5),
        "sgu_ln_g": gain((n_even, GROUP_WIDTH)),
        "sgu_ln_b": nrm((n_even, GROUP_WIDTH), 0.01),
        "sgu_w_s": nrm((n_even, SGU_GROUPS, SGU_WINDOW, SGU_WINDOW), SGU_WINDOW ** -0.5),
        "sgu_b_s": gain((n_even, SGU_GROUPS, SGU_WINDOW)),
        "od_w_in": nrm((n_odd, D_MODEL, ODD_IN), D_MODEL ** -0.5),
        "od_w_out": nrm((n_odd, MIX_WIDTH, D_MODEL), MIX_WIDTH ** -0.5),
        "hgrn_lb": nrm((DEPTH, GROUP_WIDTH), 0.1),
        "hgrn_norm_g": gain((n_odd, GROUP_WIDTH)),
        "dsa_cq_g": gain((n_odd, DSA_Q_RANK)),
        "dsa_ckv_g": gain((n_odd, DSA_KV_RANK)),
        "dsa_w_uq": nrm((n_odd, DSA_Q_RANK, DSA_HEADS * DSA_KV_RANK), DSA_Q_RANK ** -0.5),
        "dsa_qnorm_g": gain((n_odd, DSA_KV_RANK)),
        "dsa_w_qidx": nrm((n_odd, DSA_Q_RANK, IDX_HEADS * IDX_DIM), DSA_Q_RANK ** -0.5),
        "dsa_w_uv": nrm((n_odd, DSA_HEADS, DSA_KV_RANK, DSA_DV), DSA_KV_RANK ** -0.5),
    }


def reference(x, ln_mix_g, ln_ffn_g, w_ffn_gate, w_ffn_up, w_ffn_down, rel_bias,
              ev_w_in, ev_w_out, sgu_ln_g, sgu_ln_b, sgu_w_s, sgu_b_s,
              od_w_in, od_w_out, hgrn_lb, hgrn_norm_g, dsa_cq_g, dsa_ckv_g,
              dsa_w_uq, dsa_qnorm_g, dsa_w_qidx, dsa_w_uv):
    s = x.shape[1]
    pos = jnp.arange(s, dtype=jnp.int32)
    lb_soft = jax.nn.softmax(hgrn_lb.astype(F32), axis=0)
    lb_layers = jnp.cumsum(lb_soft, axis=0) - lb_soft[0]
    for layer in range(DEPTH):
        h = rms_norm(x, ln_mix_g[layer])
        j = layer // 2
        if layer % 2 == 0:
            mix = even_mixer(h, pos, ev_w_in[j], ev_w_out[j], sgu_ln_g[j], sgu_ln_b[j],
                             sgu_w_s[j], sgu_b_s[j])
        else:
            mix = odd_mixer(h, lb_layers[layer], od_w_in[j], od_w_out[j], hgrn_norm_g[j],
                            dsa_cq_g[j], dsa_ckv_g[j], dsa_w_uq[j], dsa_qnorm_g[j],
                            dsa_w_qidx[j], dsa_w_uv[j], rel_bias)
        x = x + mix.astype(x.dtype)
        h = rms_norm(x, ln_ffn_g[layer])
        x = x + swiglu(h, w_ffn_gate[layer], w_ffn_up[layer], w_ffn_down[layer]).astype(x.dtype)
    return x
```

```python
import functools
import math

import jax
import jax.numpy as jnp
from jax import lax
from jax.experimental import pallas as pl
from jax.experimental.pallas import tpu as pltpu

F32 = jnp.float32
BF16 = jnp.bfloat16
I32 = jnp.int32

EPS = 1e-6
CHUNK = 64
LANES = 128
ROPE_BASE = 10000.0
RET_HEADS = 4
SGU_WINDOW = 128
SGU_GROUPS = 4
HG_HEADS = 8
DSA_HEADS = 8
DSA_Q_RANK = 384
DSA_KV_RANK = 256
IDX_HEADS = 16
IDX_DIM = 64
TOPK_MAX = 256
Q_BLOCK = 128
KV_TILE = 512
REL_BUCKETS = 32
REL_MAX_DIST = 256
NEG_BIG = -1e30
INT_MIN = -(2 ** 31)

RET_BLOCK = 256
HG_BLOCK = 256
HG_FINE = 8
VMEM_LIMIT = 48 * 1024 * 1024

_NT = (((1,), (1,)), ((), ()))
_TN = (((0,), (0,)), ((), ()))


def _params(semantics):
    return pltpu.CompilerParams(dimension_semantics=semantics, vmem_limit_bytes=VMEM_LIMIT)


def _silu(x):
    return x * jax.nn.sigmoid(x)


def _rms(x):
    return x * lax.rsqrt(jnp.mean(x * x, axis=-1, keepdims=True) + EPS)


def _norm_matmul_kernel(x_ref, g_ref, w_ref, o_ref, h_ref):
    @pl.when(pl.program_id(1) == 0)
    def _():
        h_ref[...] = (_rms(x_ref[...]) * g_ref[...]).astype(BF16)

    o_ref[...] = jnp.dot(h_ref[...], w_ref[...], preferred_element_type=F32)


def _norm_matmul(x, g, w, *, tm, tn):
    n, d = x.shape
    nout = w.shape[1]
    return pl.pallas_call(
        _norm_matmul_kernel,
        grid=(n // tm, nout // tn),
        in_specs=[
            pl.BlockSpec((tm, d), lambda i, j: (i, 0)),
            pl.BlockSpec((1, d), lambda i, j: (0, 0)),
            pl.BlockSpec((d, tn), lambda i, j: (0, j)),
        ],
        out_specs=pl.BlockSpec((tm, tn), lambda i, j: (i, j)),
        out_shape=jax.ShapeDtypeStruct((n, nout), F32),
        scratch_shapes=[pltpu.VMEM((tm, d), BF16)],
        compiler_params=_params(("arbitrary", "arbitrary")),
        name="norm_matmul",
    )(x, g.reshape(1, d), w)


def _outproj_kernel(a1_ref, a2_ref, w1_ref, w2_ref, r_ref, o_ref):
    acc = jnp.dot(a1_ref[...], w1_ref[...], preferred_element_type=F32)
    acc += jnp.dot(a2_ref[...], w2_ref[...], preferred_element_type=F32)
    o_ref[...] = r_ref[...] + acc


def _outproj(a1, a2, w, res, *, tm, tn):
    n, half = a1.shape
    d = w.shape[1]
    return pl.pallas_call(
        _outproj_kernel,
        grid=(n // tm, d // tn),
        in_specs=[
            pl.BlockSpec((tm, half), lambda i, j: (i, 0)),
            pl.BlockSpec((tm, half), lambda i, j: (i, 0)),
            pl.BlockSpec((half, tn), lambda i, j: (0, j)),
            pl.BlockSpec((half, tn), lambda i, j: (1, j)),
            pl.BlockSpec((tm, tn), lambda i, j: (i, j)),
        ],
        out_specs=pl.BlockSpec((tm, tn), lambda i, j: (i, j)),
        out_shape=jax.ShapeDtypeStruct((n, d), F32),
        compiler_params=_params(("arbitrary", "arbitrary")),
        name="outproj",
    )(a1, a2, w, w, res)


def _ffn_kernel(x_ref, g_ref, wg_ref, wu_ref, wd_ref, o_ref, h_ref):
    @pl.when(pl.program_id(1) == 0)
    def _():
        x = x_ref[...]
        h_ref[...] = (_rms(x) * g_ref[...]).astype(BF16)
        o_ref[...] = x

    h = h_ref[...]
    a = jnp.dot(h, wg_ref[...], preferred_element_type=F32)
    u = jnp.dot(h, wu_ref[...], preferred_element_type=F32)
    act = (_silu(a) * u).astype(BF16)
    o_ref[...] += jnp.dot(act, wd_ref[...], preferred_element_type=F32)


def _ffn(x, g, wg, wu, wd, *, tm, tf):
    n, d = x.shape
    dff = wg.shape[1]
    return pl.pallas_call(
        _ffn_kernel,
        grid=(n // tm, dff // tf),
        in_specs=[
            pl.BlockSpec((tm, d), lambda i, f: (i, 0)),
            pl.BlockSpec((1, d), lambda i, f: (0, 0)),
            pl.BlockSpec((d, tf), lambda i, f: (0, f)),
            pl.BlockSpec((d, tf), lambda i, f: (0, f)),
            pl.BlockSpec((tf, d), lambda i, f: (f, 0)),
        ],
        out_specs=pl.BlockSpec((tm, d), lambda i, f: (i, 0)),
        out_shape=jax.ShapeDtypeStruct((n, d), F32),
        scratch_shapes=[pltpu.VMEM((tm, d), BF16)],
        compiler_params=_params(("arbitrary", "arbitrary")),
        name="ffn",
    )(x, g.reshape(1, d), wg, wu, wd)


def _retention_kernel(q_ref, k_ref, v_ref, g_ref, cos_ref, sin_ref, d_ref, xi_ref, zeta_ref,
                      gl_ref, o_ref, state_ref):
    @pl.when(pl.program_id(2) == 0)
    def _():
        state_ref[...] = jnp.zeros_like(state_ref)

    cos = cos_ref[...]
    sin = sin_ref[...]
    half = cos.shape[1]

    def rot(t):
        t1, t2 = t[:, :half], t[:, half:]
        return jnp.concatenate([t1 * cos - t2 * sin, t1 * sin + t2 * cos], axis=1)

    q = rot(q_ref[...])
    k = rot(k_ref[...]) * (q.shape[1] ** -0.5)
    qb = q.astype(BF16)
    vb = v_ref[...].astype(BF16)
    scores = lax.dot_general(qb, k.astype(BF16), _NT, preferred_element_type=F32) * d_ref[0]
    intra = jnp.dot(scores.astype(BF16), vb, preferred_element_type=F32)
    state = state_ref[...]
    cross = jnp.dot(qb, state.astype(BF16), preferred_element_type=F32) * xi_ref[0]
    kz = (k * zeta_ref[0]).astype(BF16)
    state_ref[...] = state * gl_ref[0] + lax.dot_general(kz, vb, _TN, preferred_element_type=F32)
    g = g_ref[...]
    o_ref[...] = (_rms(intra + cross) * _silu(g)).astype(BF16)


def _retention_tables(seq, dk):
    blk = RET_BLOCK
    pos = jnp.arange(seq, dtype=F32)
    inv = ROPE_BASE ** (-jnp.arange(0, dk, 2, dtype=F32) / dk)
    ang = pos[:, None] * inv[None, :]
    log_gamma = jnp.log(1.0 - 2.0 ** (-5.0 - jnp.arange(RET_HEADS, dtype=F32)))
    i = jnp.arange(blk)
    same = (i[:, None] // CHUNK) == (i[None, :] // CHUNK)
    earlier = (i[None, :] // CHUNK) < (i[:, None] // CHUNK)
    diff = (i[:, None] - i[None, :]).astype(F32)
    dist = jnp.where(same, jnp.abs(diff), diff)
    decay = jnp.where((same | earlier)[None], jnp.exp(log_gamma[:, None, None] * dist[None]), 0.0)
    p = jnp.arange(blk, dtype=F32)
    xi = jnp.exp(log_gamma[:, None] * (p + 1.0))[:, :, None]
    zeta = jnp.exp(log_gamma[:, None] * (blk - 1.0 - p))[:, :, None]
    g_blk = jnp.broadcast_to(jnp.exp(log_gamma * blk)[:, None, None], (RET_HEADS, 1, dk))
    return jnp.cos(ang), jnp.sin(ang), decay, xi, zeta, g_blk


def _retention(z, batch, seq):
    n = z.shape[0]
    gw = z.shape[1] // 6
    dk = gw // RET_HEADS
    blk = RET_BLOCK
    nblk = seq // blk
    cos, sin, decay, xi, zeta, g_blk = _retention_tables(seq, dk)

    def zspec(part):
        return pl.BlockSpec((blk, dk), lambda b, h, c: (b * nblk + c, part * RET_HEADS + h))

    return pl.pallas_call(
        _retention_kernel,
        grid=(batch, RET_HEADS, nblk),
        in_specs=[
            zspec(0), zspec(1), zspec(2), zspec(3),
            pl.BlockSpec((blk, dk // 2), lambda b, h, c: (c, 0)),
            pl.BlockSpec((blk, dk // 2), lambda b, h, c: (c, 0)),
            pl.BlockSpec((1, blk, blk), lambda b, h, c: (h, 0, 0)),
            pl.BlockSpec((1, blk, 1), lambda b, h, c: (h, 0, 0)),
            pl.BlockSpec((1, blk, 1), lambda b, h, c: (h, 0, 0)),
            pl.BlockSpec((1, 1, dk), lambda b, h, c: (h, 0, 0)),
        ],
        out_specs=pl.BlockSpec((blk, dk), lambda b, h, c: (b * nblk + c, h)),
        out_shape=jax.ShapeDtypeStruct((n, gw), BF16),
        scratch_shapes=[pltpu.VMEM((dk, dk), F32)],
        compiler_params=_params(("arbitrary", "arbitrary", "arbitrary")),
        name="retention",
    )(z, z, z, z, cos, sin, decay, xi, zeta, g_blk)


def _gelu(x):
    return 0.5 * x * (1.0 + lax.erf(x * math.sqrt(0.5)))


def _sgu_kernel(u_ref, v_ref, lng_ref, lnb_ref, w_ref, b_ref, o_ref):
    rows, width = v_ref.shape
    dg = width // SGU_GROUPS
    v = _gelu(v_ref[...])
    mu = jnp.mean(v, axis=-1, keepdims=True)
    var = jnp.mean(jnp.square(v - mu), axis=-1, keepdims=True)
    vn = ((v - mu) * lax.rsqrt(var + EPS) * lng_ref[...] + lnb_ref[...]).astype(BF16)
    u = _gelu(u_ref[...])
    ri = lax.broadcasted_iota(I32, (SGU_WINDOW, SGU_WINDOW), 0) // CHUNK
    ci = lax.broadcasted_iota(I32, (SGU_WINDOW, SGU_WINDOW), 1) // CHUNK
    allowed = ci <= ri
    for g in range(SGU_GROUPS):
        wg = jnp.where(allowed, w_ref[g], 0.0).astype(BF16)
        bias = b_ref[g]
        for w in range(rows // SGU_WINDOW):
            rs = slice(w * SGU_WINDOW, (w + 1) * SGU_WINDOW)
            cs = slice(g * dg, (g + 1) * dg)
            mixed = jnp.dot(wg, vn[rs, cs], preferred_element_type=F32) + bias
            o_ref[rs, cs] = (u[rs, cs] * mixed).astype(BF16)


def _sgu(z, ln_g, ln_b, w_s, b_s, *, rows):
    n = z.shape[0]
    gw = z.shape[1] // 6
    return pl.pallas_call(
        _sgu_kernel,
        grid=(n // rows,),
        in_specs=[
            pl.BlockSpec((rows, gw), lambda i: (i, 4)),
            pl.BlockSpec((rows, gw), lambda i: (i, 5)),
            pl.BlockSpec((1, gw), lambda i: (0, 0)),
            pl.BlockSpec((1, gw), lambda i: (0, 0)),
            pl.BlockSpec((SGU_GROUPS, SGU_WINDOW, SGU_WINDOW), lambda i: (0, 0, 0)),
            pl.BlockSpec((SGU_GROUPS, SGU_WINDOW, 1), lambda i: (0, 0, 0)),
        ],
        out_specs=pl.BlockSpec((rows, gw), lambda i: (i, 0)),
        out_shape=jax.ShapeDtypeStruct((n, gw), BF16),
        compiler_params=_params(("arbitrary",)),
        name="sgu",
    )(z, z, ln_g.reshape(1, gw), ln_b.reshape(1, gw), w_s, b_s.reshape(SGU_GROUPS, SGU_WINDOW, 1))


def _hgrn_kernel(q_ref, f_ref, i_ref, g_ref, lb_ref, ng_ref, o_ref, st_ref, *, layer):
    rows, dk = q_ref.shape

    @pl.when(pl.program_id(2) == 0)
    def _():
        st_ref[...] = jnp.zeros_like(st_ref)

    lbp = lb_ref[...]
    e = jnp.exp(lbp - jnp.max(lbp, axis=0, keepdims=True))
    soft = e / jnp.sum(e, axis=0, keepdims=True)
    lb = jnp.sum(soft[1:layer + 1], axis=0, keepdims=True)

    f = lb + (1.0 - lb) * jax.nn.sigmoid(f_ref[...])
    lf = jnp.log(f)
    kk = 1.0 - f
    qa = _silu(q_ref[...])

    row = lax.broadcasted_iota(I32, (rows, dk), 0)
    bcum = lf
    sh = 1
    while sh < rows:
        bcum = bcum + jnp.where(row >= sh, pltpu.roll(bcum, sh, axis=0), 0.0)
        sh *= 2

    ti = lax.broadcasted_iota(I32, (rows, rows), 0)
    si = lax.broadcasted_iota(I32, (rows, rows), 1)
    attn = jnp.zeros((rows, rows), F32)
    hs = rows // 2
    while hs >= HG_FINE:
        bs = 2 * hs
        parts = [jnp.broadcast_to(bcum[b * bs + hs - 1:b * bs + hs, :], (bs, dk))
                 for b in range(rows // bs)]
        anchor = parts[0] if len(parts) == 1 else jnp.concatenate(parts, axis=0)
        upper = (row & (bs - 1)) >= hs
        qt = jnp.where(upper, qa * jnp.exp(jnp.minimum(bcum - anchor, 0.0)), 0.0)
        kt = jnp.where(upper, 0.0, kk * jnp.exp(jnp.minimum(anchor - bcum, 0.0)))
        a = lax.dot_general(qt.astype(BF16), kt.astype(BF16), _NT, preferred_element_type=F32)
        if bs < rows:
            a = jnp.where((ti & -bs) == (si & -bs), a, 0.0)
        attn = attn + a
        hs //= 2
    for delta in range(HG_FINE):
        if delta == 0:
            prod = qa * kk
        else:
            k_sh = pltpu.roll(kk, delta, axis=0)
            b_sh = pltpu.roll(bcum, delta, axis=0)
            prod = qa * k_sh * jnp.exp(jnp.minimum(bcum - b_sh, 0.0))
        col = jnp.sum(prod, axis=1, keepdims=True)
        hit = (si == ti - delta) & ((ti & (HG_FINE - 1)) >= delta)
        attn = attn + jnp.where(hit, col, 0.0)

    vb = i_ref[...].astype(BF16)
    intra = jnp.dot(attn.astype(BF16), vb, preferred_element_type=F32)
    st = st_ref[...]
    cross = lax.dot_general((qa * jnp.exp(bcum)).astype(BF16), st.astype(BF16), _NT,
                            preferred_element_type=F32)
    blast = bcum[rows - 1:rows, :]
    kb = (kk * jnp.exp(blast - bcum)).astype(BF16)
    st_ref[...] = st * jnp.exp(blast) + lax.dot_general(vb, kb, _TN, preferred_element_type=F32)
    g = g_ref[...]
    o_ref[...] = (_rms(intra + cross) * ng_ref[...] * _silu(g)).astype(BF16)


def _hgrn(z, batch, seq, lb_raw, norm_g, layer):
    n = z.shape[0]
    gw = z.shape[1] // 4
    dk = gw // HG_HEADS
    blk = HG_BLOCK
    nblk = seq // blk
    depth = lb_raw.shape[0]

    def zspec(part):
        return pl.BlockSpec((blk, dk), lambda b, h, c: (b * nblk + c, part * HG_HEADS + h))

    return pl.pallas_call(
        functools.partial(_hgrn_kernel, layer=layer),
        grid=(batch, HG_HEADS, nblk),
        in_specs=[
            zspec(0), zspec(1), zspec(2), zspec(3),
            pl.BlockSpec((depth, dk), lambda b, h, c: (0, h)),
            pl.BlockSpec((1, dk), lambda b, h, c: (0, h)),
        ],
        out_specs=pl.BlockSpec((blk, dk), lambda b, h, c: (b * nblk + c, h)),
        out_shape=jax.ShapeDtypeStruct((n, gw), BF16),
        scratch_shapes=[pltpu.VMEM((dk, dk), F32)],
        compiler_params=_params(("arbitrary", "arbitrary", "arbitrary")),
        name="hgrn2",
    )(z, z, z, z, lb_raw, norm_g.reshape(1, gw))


def _dsa_prep_kernel(zd_ref, cqg_ref, ckvg_ref, wuq_ref, qng_ref, wqi_ref,
                     q_ref, qi_ref, kv_ref, kix_ref, wh_ref):
    zd = zd_ref[...]
    cq = (_rms(zd[:, :DSA_Q_RANK]) * cqg_ref[...]).astype(BF16)
    qf = jnp.dot(cq, wuq_ref[...], preferred_element_type=F32)
    for h in range(DSA_HEADS):
        cs = slice(h * DSA_KV_RANK, (h + 1) * DSA_KV_RANK)
        q_ref[:, cs] = (_rms(qf[:, cs]) * qng_ref[...] * (DSA_KV_RANK ** -0.5)).astype(BF16)
    qi_ref[...] = (jnp.dot(cq, wqi_ref[...], preferred_element_type=F32) * (IDX_DIM ** -0.5)).astype(BF16)
    c0 = DSA_Q_RANK
    c1 = c0 + DSA_KV_RANK
    kv_ref[...] = (_rms(zd[:, c0:c1]) * ckvg_ref[...]).astype(BF16)
    kix_ref[...] = zd[:, c1:c1 + LANES].astype(BF16)
    wh_ref[...] = zd[:, c1 + LANES:c1 + 2 * LANES] * (IDX_HEADS ** -0.5)


def _dsa_prep(zd, cq_g, ckv_g, w_uq, qn_g, w_qi, *, tm):
    n, wd = zd.shape
    dq = w_uq.shape[1]
    dqi = w_qi.shape[1]
    full = lambda i: (0, 0)
    rows = lambda i: (i, 0)
    return pl.pallas_call(
        _dsa_prep_kernel,
        grid=(n // tm,),
        in_specs=[
            pl.BlockSpec((tm, wd), rows),
            pl.BlockSpec((1, DSA_Q_RANK), full),
            pl.BlockSpec((1, DSA_KV_RANK), full),
            pl.BlockSpec((DSA_Q_RANK, dq), full),
            pl.BlockSpec((1, DSA_KV_RANK), full),
            pl.BlockSpec((DSA_Q_RANK, dqi), full),
        ],
        out_specs=[
            pl.BlockSpec((tm, dq), rows),
            pl.BlockSpec((tm, dqi), rows),
            pl.BlockSpec((tm, DSA_KV_RANK), rows),
            pl.BlockSpec((tm, LANES), rows),
            pl.BlockSpec((tm, LANES), rows),
        ],
        out_shape=[
            jax.ShapeDtypeStruct((n, dq), BF16),
            jax.ShapeDtypeStruct((n, dqi), BF16),
            jax.ShapeDtypeStruct((n, DSA_KV_RANK), BF16),
            jax.ShapeDtypeStruct((n, LANES), BF16),
            jax.ShapeDtypeStruct((n, LANES), F32),
        ],
        compiler_params=_params(("arbitrary",)),
        name="dsa_prep",
    )(zd, cq_g.reshape(1, -1), ckv_g.reshape(1, -1), w_uq, qn_g.reshape(1, -1), w_qi)


def _dsa_select_kernel(qi_ref, wh_ref, kix_ref, m_ref, key_ref, *, ksel, idx_bits):
    qb = pl.program_id(1)
    ntile = qb + 1
    ntile_all = key_ref.shape[0]
    wh = wh_ref[...]
    rowi = lax.broadcasted_iota(I32, (Q_BLOCK, LANES), 0)
    coli = lax.broadcasted_iota(I32, (Q_BLOCK, LANES), 1)
    q_chunk = (qb * Q_BLOCK + rowi) // CHUNK

    def admissible(j):
        return ((j * LANES + coli) // CHUNK) <= q_chunk

    def score_tile(j, carry):
        kt = kix_ref[pl.ds(pl.multiple_of(j * LANES, LANES), LANES), :]
        sc = jnp.zeros((Q_BLOCK, LANES), F32)
        for h in range(IDX_HEADS):
            s = lax.dot_general(qi_ref[:, h * LANES:(h + 1) * LANES], kt, _NT,
                                preferred_element_type=F32)
            sc = sc + wh[:, h:h + 1] * jnp.maximum(s, 0.0)
        bits = pltpu.bitcast(sc, I32)
        key = bits ^ ((bits >> 31) & 0x7FFFFFFF)
        key_ref[j] = jnp.where(admissible(j), key, INT_MIN)
        return carry

    lax.fori_loop(0, ntile, score_tile, 0)

    def lane_count(pred_fn):
        def body(j, acc):
            return acc + pred_fn(j, key_ref[j]).astype(I32)
        acc = lax.fori_loop(0, ntile, body, jnp.zeros((Q_BLOCK, LANES), I32))
        return jnp.sum(acc, axis=1, keepdims=True)

    def value_bit(i, t_u):
        cand_u = t_u | jnp.left_shift(jnp.int32(1), 31 - i)
        cand = cand_u ^ INT_MIN
        cnt = lane_count(lambda j, k: k >= cand)
        return jnp.where(cnt >= ksel, cand_u, t_u)

    t_u = lax.fori_loop(0, 32, value_bit, jnp.zeros((Q_BLOCK, 1), I32))
    thr = t_u ^ INT_MIN
    need = ksel - lane_count(lambda j, k: k > thr)

    def index_bit(i, j_c):
        cand = j_c | jnp.left_shift(jnp.int32(1), idx_bits - 1 - i)
        cnt = lane_count(lambda j, k: (k == thr) & ((j * LANES + coli) < cand))
        return jnp.where(cnt < need, cand, j_c)

    j_c = lax.fori_loop(0, idx_bits, index_bit, jnp.zeros((Q_BLOCK, 1), I32))

    def write_tile(j, carry):
        k = key_ref[j]
        sel = (k > thr) | ((k == thr) & ((j * LANES + coli) <= j_c))
        sel = sel & admissible(j)
        m_ref[0, 0, j] = jnp.where(sel, 0.0, NEG_BIG).astype(BF16)
        return carry

    lax.fori_loop(0, ntile, write_tile, 0)

    def blank_tile(j, carry):
        m_ref[0, 0, j] = jnp.full((Q_BLOCK, LANES), NEG_BIG, BF16)
        return carry

    lax.fori_loop(ntile, ntile_all, blank_tile, 0)


def _dsa_select(qi, wh, kix, batch, seq, ksel):
    nqb = seq // Q_BLOCK
    nkt = seq // LANES
    dqi = qi.shape[1]
    return pl.pallas_call(
        functools.partial(_dsa_select_kernel, ksel=ksel, idx_bits=int(math.log2(seq))),
        grid=(batch, nqb),
        in_specs=[
            pl.BlockSpec((Q_BLOCK, dqi), lambda b, q: (b * nqb + q, 0)),
            pl.BlockSpec((Q_BLOCK, LANES), lambda b, q: (b * nqb + q, 0)),
            pl.BlockSpec((seq, LANES), lambda b, q: (b, 0)),
        ],
        out_specs=pl.BlockSpec((1, 1, nkt, Q_BLOCK, LANES), lambda b, q: (b, q, 0, 0, 0)),
        out_shape=jax.ShapeDtypeStruct((batch, nqb, nkt, Q_BLOCK, LANES), BF16),
        scratch_shapes=[pltpu.VMEM((nkt, Q_BLOCK, LANES), I32)],
        compiler_params=_params(("arbitrary", "arbitrary")),
        name="dsa_select",
    )(qi, wh, kix)


def _rel_bucket(rel):
    nb = REL_BUCKETS // 2
    max_exact = nb // 2
    ret = jnp.where(rel > 0, nb, 0)
    n = jnp.abs(rel)
    nf = jnp.maximum(n, 1).astype(F32)
    large = max_exact + (jnp.log(nf / max_exact) / math.log(REL_MAX_DIST / max_exact)
                         * (nb - max_exact)).astype(I32)
    large = jnp.minimum(large, nb - 1)
    return ret + jnp.where(n < max_exact, n, large)


NEAR_TILES = 3


def _dsa_attn_kernel(q_ref, kv_ref, mask_ref, rb_ref, wuv_ref, o_ref,
                     m_ref, l_ref, acc_ref, corr_ref):
    b, qb, kt = pl.program_id(0), pl.program_id(1), pl.program_id(2)
    nkt = pl.num_programs(2)
    sub = KV_TILE // LANES
    far_bucket = REL_BUCKETS // 2 - 1

    @pl.when((b == 0) & (qb == 0) & (kt == 0))
    def _():
        ti = lax.broadcasted_iota(I32, (Q_BLOCK, LANES), 0)
        si = lax.broadcasted_iota(I32, (Q_BLOCK, LANES), 1)
        for oi in range(NEAR_TILES):
            bucket = _rel_bucket((oi - (NEAR_TILES - 1)) * LANES + si - ti)
            for h in range(DSA_HEADS):
                tbl = jnp.zeros((Q_BLOCK, LANES), F32)
                for bk in range(REL_BUCKETS):
                    tbl = jnp.where(bucket == bk, rb_ref[bk, h], tbl)
                corr_ref[oi, h] = tbl - rb_ref[far_bucket, h]

    @pl.when(kt == 0)
    def _():
        m_ref[...] = jnp.full_like(m_ref, NEG_BIG)
        l_ref[...] = jnp.zeros_like(l_ref)
        acc_ref[...] = jnp.zeros_like(acc_ref)

    @pl.when(kt * KV_TILE < (qb + 1) * Q_BLOCK)
    def _():
        kvt = kv_ref[...]
        for h in range(DSA_HEADS):
            s = lax.dot_general(q_ref[:, h * DSA_KV_RANK:(h + 1) * DSA_KV_RANK], kvt, _NT,
                                preferred_element_type=F32)
            parts = []
            for j in range(sub):
                d = kt * sub + j - qb
                near = (d > -NEAR_TILES) & (d <= 0)
                ni = jnp.clip(d + NEAR_TILES - 1, 0, NEAR_TILES - 1)
                parts.append(mask_ref[0, 0, j].astype(F32) + jnp.where(near, corr_ref[ni, h], 0.0))
            s = s + jnp.concatenate(parts, axis=1)
            m_old = m_ref[h]
            m_new = jnp.maximum(m_old, jnp.max(s, axis=1, keepdims=True))
            alpha = jnp.exp(m_old - m_new)
            p = jnp.exp(s - m_new)
            l_ref[h] = alpha * l_ref[h] + jnp.sum(p, axis=1, keepdims=True)
            acc_ref[h] = alpha * acc_ref[h] + jnp.dot(p.astype(BF16), kvt, preferred_element_type=F32)
            m_ref[h] = m_new

    @pl.when(kt == nkt - 1)
    def _():
        dv = wuv_ref.shape[2]
        for h in range(DSA_HEADS):
            o = (acc_ref[h] / l_ref[h]).astype(BF16)
            o_ref[:, h * dv:(h + 1) * dv] = jnp.dot(o, wuv_ref[h], preferred_element_type=F32).astype(BF16)


def _dsa_attn(q, kv, mask, rel_bias, w_uv, batch, seq):
    n = q.shape[0]
    nqb = seq // Q_BLOCK
    nkt = seq // KV_TILE
    sub = KV_TILE // LANES
    dv = w_uv.shape[2]

    def last_tile(qb):
        return (qb * Q_BLOCK + Q_BLOCK - 1) // KV_TILE

    return pl.pallas_call(
        _dsa_attn_kernel,
        grid=(batch, nqb, nkt),
        in_specs=[
            pl.BlockSpec((Q_BLOCK, q.shape[1]), lambda b, qb, kt: (b * nqb + qb, 0)),
            pl.BlockSpec((KV_TILE, DSA_KV_RANK),
                         lambda b, qb, kt: (b * nkt + jnp.minimum(kt, last_tile(qb)), 0)),
            pl.BlockSpec((1, 1, sub, Q_BLOCK, LANES),
                         lambda b, qb, kt: (b, qb, jnp.minimum(kt, last_tile(qb)), 0, 0)),
            pl.BlockSpec(memory_space=pltpu.SMEM),
            pl.BlockSpec(w_uv.shape, lambda b, qb, kt: (0, 0, 0)),
        ],
        out_specs=pl.BlockSpec((Q_BLOCK, DSA_HEADS * dv), lambda b, qb, kt: (b * nqb + qb, 0)),
        out_shape=jax.ShapeDtypeStruct((n, DSA_HEADS * dv), BF16),
        scratch_shapes=[
            pltpu.VMEM((DSA_HEADS, Q_BLOCK, 1), F32),
            pltpu.VMEM((DSA_HEADS, Q_BLOCK, 1), F32),
            pltpu.VMEM((DSA_HEADS, Q_BLOCK, DSA_KV_RANK), F32),
            pltpu.VMEM((NEAR_TILES, DSA_HEADS, Q_BLOCK, LANES), F32),
        ],
        compiler_params=_params(("arbitrary", "arbitrary", "arbitrary")),
        name="dsa_attn",
    )(q, kv, mask, rel_bias, w_uv)


def _pad_cols(w, width):
    return jnp.pad(w, ((0, 0), (0, width - w.shape[1])))


def kernel(x, ln_mix_g, ln_ffn_g, w_ffn_gate, w_ffn_up, w_ffn_down, rel_bias, ev_w_in, ev_w_out, sgu_ln_g, sgu_ln_b, sgu_w_s, sgu_b_s, od_w_in, od_w_out, hgrn_lb, hgrn_norm_g, dsa_cq_g, dsa_ckv_g, dsa_w_uq, dsa_qnorm_g, dsa_w_qidx, dsa_w_uv):
    batch, seq, d = x.shape
    n = batch * seq
    depth = ln_mix_g.shape[0]
    ksel = min(TOPK_MAX, seq // 4)
    tm = 512
    xf = x.reshape(n, d)
    for layer in range(depth):
        j = layer // 2
        if layer % 2 == 0:
            z = _norm_matmul(xf, ln_mix_g[layer], ev_w_in[j].astype(BF16), tm=tm, tn=512)
            a1 = _retention(z, batch, seq)
            a2 = _sgu(z, sgu_ln_g[j], sgu_ln_b[j], sgu_w_s[j], sgu_b_s[j], rows=256)
            w_out = ev_w_out[j]
        else:
            w_in = od_w_in[j]
            gw = d // 2
            c = 4 * gw
            c_kidx = c + DSA_Q_RANK + DSA_KV_RANK
            w_main = w_in[:, :c].astype(BF16)
            w_dsa = jnp.concatenate([
                w_in[:, c:c_kidx],
                _pad_cols(w_in[:, c_kidx:c_kidx + IDX_DIM], LANES),
                _pad_cols(w_in[:, c_kidx + IDX_DIM:], LANES),
            ], axis=1).astype(BF16)
            z = _norm_matmul(xf, ln_mix_g[layer], w_main, tm=tm, tn=512)
            zd = _norm_matmul(xf, ln_mix_g[layer], w_dsa, tm=tm, tn=w_dsa.shape[1])
            a1 = _hgrn(z, batch, seq, hgrn_lb, hgrn_norm_g[j], layer)
            w_qi = jnp.pad(dsa_w_qidx[j].reshape(DSA_Q_RANK, IDX_HEADS, IDX_DIM),
                           ((0, 0), (0, 0), (0, LANES - IDX_DIM))).reshape(DSA_Q_RANK, IDX_HEADS * LANES)
            q, qi, kv, kix, wh = _dsa_prep(zd, dsa_cq_g[j], dsa_ckv_g[j], dsa_w_uq[j].astype(BF16),
                                           dsa_qnorm_g[j], w_qi.astype(BF16), tm=256)
            mask = _dsa_select(qi, wh, kix, batch, seq, ksel)
            a2 = _dsa_attn(q, kv, mask, rel_bias, dsa_w_uv[j].astype(BF16), batch, seq)
            w_out = od_w_out[j]
        xf = _outproj(a1, a2, w_out.astype(BF16), xf, tm=tm, tn=512)
        xf = _ffn(xf, ln_ffn_g[layer], w_ffn_gate[layer].astype(BF16), w_ffn_up[layer].astype(BF16),
                  w_ffn_down[layer].astype(BF16), tm=tm, tf=512)
    return xf.reshape(batch, seq, d)
```

```python
import functools
import math

import jax
import jax.numpy as jnp
from jax import lax
from jax.experimental import pallas as pl
from jax.experimental.pallas import tpu as pltpu

F32 = jnp.float32
BF16 = jnp.bfloat16
I32 = jnp.int32

EPS = 1e-6
CHUNK = 64
LANES = 128
ROPE_BASE = 10000.0
RET_HEADS = 4
SGU_WINDOW = 128
SGU_GROUPS = 4
HG_HEADS = 8
DSA_HEADS = 8
DSA_Q_RANK = 384
DSA_KV_RANK = 256
IDX_HEADS = 16
IDX_DIM = 64
TOPK_MAX = 256
Q_BLOCK = 128
KV_TILE = 512
SEL_CHUNK = 512
REL_BUCKETS = 32
REL_MAX_DIST = 256
NEG_BIG = -1e30
INT_MIN = -(2 ** 31)

RET_BLOCK = 256
HG_BLOCK = 256
HG_FINE = 8
VMEM_LIMIT = 48 * 1024 * 1024

_NT = (((1,), (1,)), ((), ()))
_TN = (((0,), (0,)), ((), ()))


def _params(semantics):
    return pltpu.CompilerParams(dimension_semantics=semantics, vmem_limit_bytes=VMEM_LIMIT)


def _silu(x):
    return x * jax.nn.sigmoid(x)


def _rms(x):
    return x * lax.rsqrt(jnp.mean(x * x, axis=-1, keepdims=True) + EPS)


def _norm_matmul_kernel(x_ref, g_ref, w_ref, o_ref, h_ref):
    @pl.when(pl.program_id(1) == 0)
    def _():
        h_ref[...] = (_rms(x_ref[...]) * g_ref[...]).astype(BF16)

    o_ref[...] = jnp.dot(h_ref[...], w_ref[...], preferred_element_type=F32)


def _norm_matmul(x, g, w, *, tm, tn):
    n, d = x.shape
    nout = w.shape[1]
    return pl.pallas_call(
        _norm_matmul_kernel,
        grid=(n // tm, nout // tn),
        in_specs=[
            pl.BlockSpec((tm, d), lambda i, j: (i, 0)),
            pl.BlockSpec((1, d), lambda i, j: (0, 0)),
            pl.BlockSpec((d, tn), lambda i, j: (0, j)),
        ],
        out_specs=pl.BlockSpec((tm, tn), lambda i, j: (i, j)),
        out_shape=jax.ShapeDtypeStruct((n, nout), F32),
        scratch_shapes=[pltpu.VMEM((tm, d), BF16)],
        compiler_params=_params(("arbitrary", "arbitrary")),
        name="norm_matmul",
    )(x, g.reshape(1, d), w)


def _outproj_kernel(a1_ref, a2_ref, w1_ref, w2_ref, r_ref, o_ref):
    acc = jnp.dot(a1_ref[...], w1_ref[...], preferred_element_type=F32)
    acc += jnp.dot(a2_ref[...], w2_ref[...], preferred_element_type=F32)
    o_ref[...] = r_ref[...] + acc


def _outproj(a1, a2, w, res, *, tm, tn):
    n, half = a1.shape
    d = w.shape[1]
    return pl.pallas_call(
        _outproj_kernel,
        grid=(n // tm, d // tn),
        in_specs=[
            pl.BlockSpec((tm, half), lambda i, j: (i, 0)),
            pl.BlockSpec((tm, half), lambda i, j: (i, 0)),
            pl.BlockSpec((half, tn), lambda i, j: (0, j)),
            pl.BlockSpec((half, tn), lambda i, j: (1, j)),
            pl.BlockSpec((tm, tn), lambda i, j: (i, j)),
        ],
        out_specs=pl.BlockSpec((tm, tn), lambda i, j: (i, j)),
        out_shape=jax.ShapeDtypeStruct((n, d), F32),
        compiler_params=_params(("arbitrary", "arbitrary")),
        name="outproj",
    )(a1, a2, w, w, res)


def _ffn_kernel(x_ref, g_ref, wg_ref, wu_ref, wd_ref, o_ref, h_ref):
    @pl.when(pl.program_id(1) == 0)
    def _():
        x = x_ref[...]
        h_ref[...] = (_rms(x) * g_ref[...]).astype(BF16)
        o_ref[...] = x

    h = h_ref[...]
    a = jnp.dot(h, wg_ref[...], preferred_element_type=F32)
    u = jnp.dot(h, wu_ref[...], preferred_element_type=F32)
    act = (_silu(a) * u).astype(BF16)
    o_ref[...] += jnp.dot(act, wd_ref[...], preferred_element_type=F32)


def _ffn(x, g, wg, wu, wd, *, tm, tf):
    n, d = x.shape
    dff = wg.shape[1]
    return pl.pallas_call(
        _ffn_kernel,
        grid=(n // tm, dff // tf),
        in_specs=[
            pl.BlockSpec((tm, d), lambda i, f: (i, 0)),
            pl.BlockSpec((1, d), lambda i, f: (0, 0)),
            pl.BlockSpec((d, tf), lambda i, f: (0, f)),
            pl.BlockSpec((d, tf), lambda i, f: (0, f)),
            pl.BlockSpec((tf, d), lambda i, f: (f, 0)),
        ],
        out_specs=pl.BlockSpec((tm, d), lambda i, f: (i, 0)),
        out_shape=jax.ShapeDtypeStruct((n, d), F32),
        scratch_shapes=[pltpu.VMEM((tm, d), BF16)],
        compiler_params=_params(("arbitrary", "arbitrary")),
        name="ffn",
    )(x, g.reshape(1, d), wg, wu, wd)


def _retention_kernel(q_ref, k_ref, v_ref, g_ref, cos_ref, sin_ref, d_ref, xi_ref, zeta_ref,
                      gl_ref, o_ref, state_ref):
    @pl.when(pl.program_id(2) == 0)
    def _():
        state_ref[...] = jnp.zeros_like(state_ref)

    cos = cos_ref[...]
    sin = sin_ref[...]
    half = cos.shape[1]

    def rot(t):
        t1, t2 = t[:, :half], t[:, half:]
        return jnp.concatenate([t1 * cos - t2 * sin, t1 * sin + t2 * cos], axis=1)

    q = rot(q_ref[...])
    k = rot(k_ref[...]) * (q.shape[1] ** -0.5)
    qb = q.astype(BF16)
    vb = v_ref[...].astype(BF16)
    scores = lax.dot_general(qb, k.astype(BF16), _NT, preferred_element_type=F32) * d_ref[0]
    intra = jnp.dot(scores.astype(BF16), vb, preferred_element_type=F32)
    state = state_ref[...]
    cross = jnp.dot(qb, state.astype(BF16), preferred_element_type=F32) * xi_ref[0]
    kz = (k * zeta_ref[0]).astype(BF16)
    state_ref[...] = state * gl_ref[0] + lax.dot_general(kz, vb, _TN, preferred_element_type=F32)
    g = g_ref[...]
    o_ref[...] = (_rms(intra + cross) * _silu(g)).astype(BF16)


def _retention_tables(seq, dk):
    blk = RET_BLOCK
    pos = jnp.arange(seq, dtype=F32)
    inv = ROPE_BASE ** (-jnp.arange(0, dk, 2, dtype=F32) / dk)
    ang = pos[:, None] * inv[None, :]
    log_gamma = jnp.log(1.0 - 2.0 ** (-5.0 - jnp.arange(RET_HEADS, dtype=F32)))
    i = jnp.arange(blk)
    same = (i[:, None] // CHUNK) == (i[None, :] // CHUNK)
    earlier = (i[None, :] // CHUNK) < (i[:, None] // CHUNK)
    diff = (i[:, None] - i[None, :]).astype(F32)
    dist = jnp.where(same, jnp.abs(diff), diff)
    decay = jnp.where((same | earlier)[None], jnp.exp(log_gamma[:, None, None] * dist[None]), 0.0)
    p = jnp.arange(blk, dtype=F32)
    xi = jnp.exp(log_gamma[:, None] * (p + 1.0))[:, :, None]
    zeta = jnp.exp(log_gamma[:, None] * (blk - 1.0 - p))[:, :, None]
    g_blk = jnp.broadcast_to(jnp.exp(log_gamma * blk)[:, None, None], (RET_HEADS, 1, dk))
    return jnp.cos(ang), jnp.sin(ang), decay, xi, zeta, g_blk


def _retention(z, batch, seq):
    n = z.shape[0]
    gw = z.shape[1] // 6
    dk = gw // RET_HEADS
    blk = RET_BLOCK
    nblk = seq // blk
    cos, sin, decay, xi, zeta, g_blk = _retention_tables(seq, dk)

    def zspec(part):
        return pl.BlockSpec((blk, dk), lambda b, h, c: (b * nblk + c, part * RET_HEADS + h))

    return pl.pallas_call(
        _retention_kernel,
        grid=(batch, RET_HEADS, nblk),
        in_specs=[
            zspec(0), zspec(1), zspec(2), zspec(3),
            pl.BlockSpec((blk, dk // 2), lambda b, h, c: (c, 0)),
            pl.BlockSpec((blk, dk // 2), lambda b, h, c: (c, 0)),
            pl.BlockSpec((1, blk, blk), lambda b, h, c: (h, 0, 0)),
            pl.BlockSpec((1, blk, 1), lambda b, h, c: (h, 0, 0)),
            pl.BlockSpec((1, blk, 1), lambda b, h, c: (h, 0, 0)),
            pl.BlockSpec((1, 1, dk), lambda b, h, c: (h, 0, 0)),
        ],
        out_specs=pl.BlockSpec((blk, dk), lambda b, h, c: (b * nblk + c, h)),
        out_shape=jax.ShapeDtypeStruct((n, gw), BF16),
        scratch_shapes=[pltpu.VMEM((dk, dk), F32)],
        compiler_params=_params(("arbitrary", "arbitrary", "arbitrary")),
        name="retention",
    )(z, z, z, z, cos, sin, decay, xi, zeta, g_blk)


def _gelu(x):
    return 0.5 * x * (1.0 + lax.erf(x * math.sqrt(0.5)))


def _sgu_kernel(u_ref, v_ref, lng_ref, lnb_ref, w_ref, b_ref, o_ref):
    rows, width = v_ref.shape
    dg = width // SGU_GROUPS
    v = _gelu(v_ref[...])
    mu = jnp.mean(v, axis=-1, keepdims=True)
    var = jnp.mean(jnp.square(v - mu), axis=-1, keepdims=True)
    vn = ((v - mu) * lax.rsqrt(var + EPS) * lng_ref[...] + lnb_ref[...]).astype(BF16)
    u = _gelu(u_ref[...])
    ri = lax.broadcasted_iota(I32, (SGU_WINDOW, SGU_WINDOW), 0) // CHUNK
    ci = lax.broadcasted_iota(I32, (SGU_WINDOW, SGU_WINDOW), 1) // CHUNK
    allowed = ci <= ri
    for g in range(SGU_GROUPS):
        wg = jnp.where(allowed, w_ref[g], 0.0).astype(BF16)
        bias = b_ref[g]
        for w in range(rows // SGU_WINDOW):
            rs = slice(w * SGU_WINDOW, (w + 1) * SGU_WINDOW)
            cs = slice(g * dg, (g + 1) * dg)
            mixed = jnp.dot(wg, vn[rs, cs], preferred_element_type=F32) + bias
            o_ref[rs, cs] = (u[rs, cs] * mixed).astype(BF16)


def _sgu(z, ln_g, ln_b, w_s, b_s, *, rows):
    n = z.shape[0]
    gw = z.shape[1] // 6
    return pl.pallas_call(
        _sgu_kernel,
        grid=(n // rows,),
        in_specs=[
            pl.BlockSpec((rows, gw), lambda i: (i, 4)),
            pl.BlockSpec((rows, gw), lambda i: (i, 5)),
            pl.BlockSpec((1, gw), lambda i: (0, 0)),
            pl.BlockSpec((1, gw), lambda i: (0, 0)),
            pl.BlockSpec((SGU_GROUPS, SGU_WINDOW, SGU_WINDOW), lambda i: (0, 0, 0)),
            pl.BlockSpec((SGU_GROUPS, SGU_WINDOW, 1), lambda i: (0, 0, 0)),
        ],
        out_specs=pl.BlockSpec((rows, gw), lambda i: (i, 0)),
        out_shape=jax.ShapeDtypeStruct((n, gw), BF16),
        compiler_params=_params(("arbitrary",)),
        name="sgu",
    )(z, z, ln_g.reshape(1, gw), ln_b.reshape(1, gw), w_s, b_s.reshape(SGU_GROUPS, SGU_WINDOW, 1))


def _hgrn_kernel(q_ref, f_ref, i_ref, g_ref, lb_ref, ng_ref, o_ref, st_ref, *, layer):
    rows, dk = q_ref.shape

    @pl.when(pl.program_id(2) == 0)
    def _():
        st_ref[...] = jnp.zeros_like(st_ref)

    lbp = lb_ref[...]
    e = jnp.exp(lbp - jnp.max(lbp, axis=0, keepdims=True))
    soft = e / jnp.sum(e, axis=0, keepdims=True)
    lb = jnp.sum(soft[1:layer + 1], axis=0, keepdims=True)

    f = lb + (1.0 - lb) * jax.nn.sigmoid(f_ref[...])
    lf = jnp.log(f)
    kk = 1.0 - f
    qa = _silu(q_ref[...])

    row = lax.broadcasted_iota(I32, (rows, dk), 0)
    bcum = lf
    sh = 1
    while sh < rows:
        bcum = bcum + jnp.where(row >= sh, pltpu.roll(bcum, sh, axis=0), 0.0)
        sh *= 2

    ti = lax.broadcasted_iota(I32, (rows, rows), 0)
    si = lax.broadcasted_iota(I32, (rows, rows), 1)
    attn = jnp.zeros((rows, rows), F32)
    hs = rows // 2
    while hs >= HG_FINE:
        bs = 2 * hs
        parts = [jnp.broadcast_to(bcum[b * bs + hs - 1:b * bs + hs, :], (bs, dk))
                 for b in range(rows // bs)]
        anchor = parts[0] if len(parts) == 1 else jnp.concatenate(parts, axis=0)
        upper = (row & (bs - 1)) >= hs
        qt = jnp.where(upper, qa * jnp.exp(jnp.minimum(bcum - anchor, 0.0)), 0.0)
        kt = jnp.where(upper, 0.0, kk * jnp.exp(jnp.minimum(anchor - bcum, 0.0)))
        a = lax.dot_general(qt.astype(BF16), kt.astype(BF16), _NT, preferred_element_type=F32)
        if bs < rows:
            a = jnp.where((ti & -bs) == (si & -bs), a, 0.0)
        attn = attn + a
        hs //= 2
    for delta in range(HG_FINE):
        if delta == 0:
            prod = qa * kk
        else:
            k_sh = pltpu.roll(kk, delta, axis=0)
            b_sh = pltpu.roll(bcum, delta, axis=0)
            prod = qa * k_sh * jnp.exp(jnp.minimum(bcum - b_sh, 0.0))
        col = jnp.sum(prod, axis=1, keepdims=True)
        hit = (si == ti - delta) & ((ti & (HG_FINE - 1)) >= delta)
        attn = attn + jnp.where(hit, col, 0.0)

    vb = i_ref[...].astype(BF16)
    intra = jnp.dot(attn.astype(BF16), vb, preferred_element_type=F32)
    st = st_ref[...]
    cross = lax.dot_general((qa * jnp.exp(bcum)).astype(BF16), st.astype(BF16), _NT,
                            preferred_element_type=F32)
    blast = bcum[rows - 1:rows, :]
    kb = (kk * jnp.exp(blast - bcum)).astype(BF16)
    st_ref[...] = st * jnp.exp(blast) + lax.dot_general(vb, kb, _TN, preferred_element_type=F32)
    g = g_ref[...]
    o_ref[...] = (_rms(intra + cross) * ng_ref[...] * _silu(g)).astype(BF16)


def _hgrn(z, batch, seq, lb_raw, norm_g, layer):
    n = z.shape[0]
    gw = z.shape[1] // 4
    dk = gw // HG_HEADS
    blk = HG_BLOCK
    nblk = seq // blk
    depth = lb_raw.shape[0]

    def zspec(part):
        return pl.BlockSpec((blk, dk), lambda b, h, c: (b * nblk + c, part * HG_HEADS + h))

    return pl.pallas_call(
        functools.partial(_hgrn_kernel, layer=layer),
        grid=(batch, HG_HEADS, nblk),
        in_specs=[
            zspec(0), zspec(1), zspec(2), zspec(3),
            pl.BlockSpec((depth, dk), lambda b, h, c: (0, h)),
            pl.BlockSpec((1, dk), lambda b, h, c: (0, h)),
        ],
        out_specs=pl.BlockSpec((blk, dk), lambda b, h, c: (b * nblk + c, h)),
        out_shape=jax.ShapeDtypeStruct((n, gw), BF16),
        scratch_shapes=[pltpu.VMEM((dk, dk), F32)],
        compiler_params=_params(("arbitrary", "arbitrary", "arbitrary")),
        name="hgrn2",
    )(z, z, z, z, lb_raw, norm_g.reshape(1, gw))


def _dsa_prep_kernel(zd_ref, cqg_ref, ckvg_ref, wuq_ref, qng_ref, wqit_ref,
                     q_ref, qit_ref, kv_ref, kix_ref, wht_ref):
    zd = zd_ref[...]
    cq = (_rms(zd[:, :DSA_Q_RANK]) * cqg_ref[...]).astype(BF16)
    qf = jnp.dot(cq, wuq_ref[...], preferred_element_type=F32)
    for i in range(q_ref.shape[0]):
        rs = slice(i * Q_BLOCK, (i + 1) * Q_BLOCK)
        for h in range(DSA_HEADS):
            cs = slice(h * DSA_KV_RANK, (h + 1) * DSA_KV_RANK)
            q_ref[i, h] = (_rms(qf[rs, cs]) * qng_ref[...] * (DSA_KV_RANK ** -0.5)).astype(BF16)
    qit = lax.dot_general(wqit_ref[...], cq, _NT, preferred_element_type=F32)
    qit = (qit * (IDX_DIM ** -0.5)).astype(BF16)
    for i in range(q_ref.shape[0]):
        for h in range(IDX_HEADS):
            c = (i * IDX_HEADS + h) * Q_BLOCK
            qit_ref[:, c:c + Q_BLOCK] = qit[h * LANES:(h + 1) * LANES, i * Q_BLOCK:(i + 1) * Q_BLOCK]
    c0 = DSA_Q_RANK
    c1 = c0 + DSA_KV_RANK
    kv_ref[...] = (_rms(zd[:, c0:c1]) * ckvg_ref[...]).astype(BF16)
    kix_ref[...] = zd[:, c1:c1 + LANES].astype(BF16)
    wht = jnp.transpose(zd[:, c1 + LANES:c1 + 2 * LANES] * (IDX_HEADS ** -0.5))
    wht_ref[...] = wht[:IDX_HEADS, :]


def _dsa_prep(zd, cq_g, ckv_g, w_uq, qn_g, w_qit, *, tm):
    n, wd = zd.shape
    dq = w_uq.shape[1]
    dqi = w_qit.shape[0]
    full = lambda i: (0, 0)
    rows = lambda i: (i, 0)
    cols = lambda i: (0, i)
    return pl.pallas_call(
        _dsa_prep_kernel,
        grid=(n // tm,),
        in_specs=[
            pl.BlockSpec((tm, wd), rows),
            pl.BlockSpec((1, DSA_Q_RANK), full),
            pl.BlockSpec((1, DSA_KV_RANK), full),
            pl.BlockSpec((DSA_Q_RANK, dq), full),
            pl.BlockSpec((1, DSA_KV_RANK), full),
            pl.BlockSpec((dqi, DSA_Q_RANK), full),
        ],
        out_specs=[
            pl.BlockSpec((tm // Q_BLOCK, DSA_HEADS, Q_BLOCK, DSA_KV_RANK), lambda i: (i, 0, 0, 0)),
            pl.BlockSpec((LANES, IDX_HEADS * tm), cols),
            pl.BlockSpec((tm, DSA_KV_RANK), rows),
            pl.BlockSpec((tm, LANES), rows),
            pl.BlockSpec((IDX_HEADS, tm), cols),
        ],
        out_shape=[
            jax.ShapeDtypeStruct((n // Q_BLOCK, DSA_HEADS, Q_BLOCK, DSA_KV_RANK), BF16),
            jax.ShapeDtypeStruct((LANES, IDX_HEADS * n), BF16),
            jax.ShapeDtypeStruct((n, DSA_KV_RANK), BF16),
            jax.ShapeDtypeStruct((n, LANES), BF16),
            jax.ShapeDtypeStruct((IDX_HEADS, n), F32),
        ],
        compiler_params=_params(("arbitrary",)),
        name="dsa_prep",
    )(zd, cq_g.reshape(1, -1), ckv_g.reshape(1, -1), w_uq, qn_g.reshape(1, -1), w_qit)


def _dsa_select_kernel(qit_ref, wht_ref, kix_ref, m_ref, key_ref, jc_ref, *, ksel, idx_bits):
    qb = pl.program_id(1)
    ntile = qb + 1
    ntile_all = m_ref.shape[2]
    rowi = lax.broadcasted_iota(I32, (LANES, Q_BLOCK), 0)
    coli = lax.broadcasted_iota(I32, (LANES, Q_BLOCK), 1)
    q_chunk = (qb * Q_BLOCK + coli) // CHUNK
    pairs = IDX_HEADS // 2
    w_pair = [jnp.concatenate([wht_ref[2 * p:2 * p + 1, :], wht_ref[2 * p + 1:2 * p + 2, :]], axis=1)
              for p in range(pairs)]

    def tile_rows(j):
        return pl.ds(pl.multiple_of(j * LANES, LANES), LANES)

    def admissible(j):
        return ((j * LANES + rowi) // CHUNK) <= q_chunk

    def score_tile(j, carry):
        kt = kix_ref[tile_rows(j), :]
        sc = None
        for p in range(pairs):
            s2 = jnp.dot(kt, qit_ref[:, 2 * p * Q_BLOCK:2 * (p + 1) * Q_BLOCK],
                         preferred_element_type=F32)
            c2 = w_pair[p] * jnp.maximum(s2, 0.0)
            c = c2[:, :Q_BLOCK] + c2[:, Q_BLOCK:]
            sc = c if sc is None else sc + c
        bits = pltpu.bitcast(sc, I32)
        key = bits ^ ((bits >> 31) & 0x7FFFFFFF)
        key_ref[tile_rows(j), :] = jnp.where(admissible(j), key, INT_MIN)
        return carry

    @pl.when(qb == 0)
    def _():
        key_ref[...] = jnp.full(key_ref.shape, INT_MIN, I32)

    lax.fori_loop(0, ntile, score_tile, 0)

    def count(pred_fn):
        def body(jj, acc):
            for t in range(2):
                j = 2 * jj + t
                acc = acc + pred_fn(j, key_ref[tile_rows(j), :]).astype(I32)
            return acc
        acc = lax.fori_loop(0, (ntile + 1) // 2, body, jnp.zeros((LANES, Q_BLOCK), I32))
        return jnp.sum(acc, axis=0, keepdims=True)

    def value_bit(i, c):
        t_u, cnt_t = c
        cand_u = t_u | jnp.left_shift(jnp.int32(1), 31 - i)
        cand = cand_u ^ INT_MIN
        cnt = count(lambda j, k: k >= cand)
        take = cnt >= ksel
        return jnp.where(take, cand_u, t_u), jnp.where(take, cnt, cnt_t)

    cnt0 = (q_chunk[0:1, :] + 1) * CHUNK
    t_u, cnt_t = lax.fori_loop(0, 32, value_bit, (jnp.zeros((1, Q_BLOCK), I32), cnt0))
    thr = t_u ^ INT_MIN
    tied = jnp.max(jnp.where(cnt_t > ksel, 1.0, 0.0))

    jc_ref[...] = jnp.full(jc_ref.shape, 2 ** 31 - 1, I32)

    @pl.when(tied > 0.0)
    def _():
        need = ksel - count(lambda j, k: k > thr)

        def index_bit(i, j_c):
            cand = j_c | jnp.left_shift(jnp.int32(1), idx_bits - 1 - i)
            cnt = count(lambda j, k: (k == thr) & ((j * LANES + rowi) < cand))
            return jnp.where(cnt < need, cand, j_c)

        j_c = lax.fori_loop(0, idx_bits, index_bit, jnp.zeros((1, Q_BLOCK), I32))
        jc_ref[...] = jnp.broadcast_to(j_c, jc_ref.shape)

    j_c = jc_ref[0:1, :]
    eye = (rowi == coli).astype(BF16)

    group = KV_TILE // LANES

    def write_group(g, carry):
        for t in range(group):
            j = g * group + t
            k = key_ref[tile_rows(j), :]
            sel = (k > thr) | ((k == thr) & ((j * LANES + rowi) <= j_c))
            sel = jnp.where(sel & admissible(j), 1.0, 0.0).astype(BF16)
            sel_t = lax.dot_general(eye, sel, _NT, preferred_element_type=F32)
            m_ref[0, 0, j] = ((sel_t - 1.0) * -NEG_BIG).astype(BF16)
        return carry

    ngroup = (ntile + group - 1) // group
    lax.fori_loop(0, ngroup, write_group, 0)

    def blank_tile(j, carry):
        m_ref[0, 0, j] = jnp.full((Q_BLOCK, LANES), NEG_BIG, BF16)
        return carry

    lax.fori_loop(ngroup * group, ntile_all, blank_tile, 0)


def _dsa_select(qit, wht, kix, batch, seq, ksel):
    nqb = seq // Q_BLOCK
    nkt = seq // LANES
    return pl.pallas_call(
        functools.partial(_dsa_select_kernel, ksel=ksel, idx_bits=int(math.log2(seq))),
        grid=(batch, nqb),
        in_specs=[
            pl.BlockSpec((LANES, IDX_HEADS * Q_BLOCK), lambda b, q: (0, b * nqb + q)),
            pl.BlockSpec((IDX_HEADS, Q_BLOCK), lambda b, q: (0, b * nqb + q)),
            pl.BlockSpec((seq, LANES), lambda b, q: (b, 0)),
        ],
        out_specs=pl.BlockSpec((1, 1, nkt, Q_BLOCK, LANES), lambda b, q: (b, q, 0, 0, 0)),
        out_shape=jax.ShapeDtypeStruct((batch, nqb, nkt, Q_BLOCK, LANES), BF16),
        scratch_shapes=[pltpu.VMEM((seq, Q_BLOCK), I32), pltpu.VMEM((8, Q_BLOCK), I32)],
        compiler_params=_params(("arbitrary", "arbitrary")),
        name="dsa_select",
    )(qit, wht, kix)


def _rel_bucket(rel):
    nb = REL_BUCKETS // 2
    max_exact = nb // 2
    ret = jnp.where(rel > 0, nb, 0)
    n = jnp.abs(rel)
    nf = jnp.maximum(n, 1).astype(F32)
    large = max_exact + (jnp.log(nf / max_exact) / math.log(REL_MAX_DIST / max_exact)
                         * (nb - max_exact)).astype(I32)
    large = jnp.minimum(large, nb - 1)
    return ret + jnp.where(n < max_exact, n, large)


NEAR_TILES = 3


def _dsa_attn_kernel(q_ref, kv_ref, mask_ref, rb_ref, wuv_ref, o_ref,
                     m_ref, l_ref, acc_ref, corr_ref, s_ref, p_ref):
    b, qb, kt = pl.program_id(0), pl.program_id(1), pl.program_id(2)
    nkt = pl.num_programs(2)
    sub = KV_TILE // LANES
    far_bucket = REL_BUCKETS // 2 - 1

    def head_rows(h):
        return slice(h * Q_BLOCK, (h + 1) * Q_BLOCK)

    @pl.when((b == 0) & (qb == 0) & (kt == 0))
    def _():
        ti = lax.broadcasted_iota(I32, (Q_BLOCK, LANES), 0)
        si = lax.broadcasted_iota(I32, (Q_BLOCK, LANES), 1)
        for oi in range(NEAR_TILES):
            bucket = _rel_bucket((oi - (NEAR_TILES - 1)) * LANES + si - ti)
            for h in range(DSA_HEADS):
                tbl = jnp.zeros((Q_BLOCK, LANES), F32)
                for bk in range(REL_BUCKETS):
                    tbl = jnp.where(bucket == bk, rb_ref[bk, h], tbl)
                corr_ref[oi, head_rows(h), :] = tbl - rb_ref[far_bucket, h]

    @pl.when(kt == 0)
    def _():
        m_ref[...] = jnp.full_like(m_ref, NEG_BIG)
        l_ref[...] = jnp.zeros_like(l_ref)
        acc_ref[...] = jnp.zeros_like(acc_ref)

    @pl.when(kt * KV_TILE < (qb + 1) * Q_BLOCK)
    def _():
        kvt = kv_ref[...]
        q_all = q_ref[0].reshape(DSA_HEADS * Q_BLOCK, DSA_KV_RANK)
        s_ref[...] = lax.dot_general(q_all, kvt, _NT, preferred_element_type=F32)
        for j in range(sub):
            d = kt * sub + j - qb

            @pl.when((d > -NEAR_TILES) & (d <= 0))
            def _(j=j, d=d):
                s_ref[:, j * LANES:(j + 1) * LANES] += corr_ref[d + NEAR_TILES - 1]

        madd = jnp.concatenate([mask_ref[0, 0, j].astype(F32) for j in range(sub)], axis=1)
        for h in range(DSA_HEADS):
            rs = head_rows(h)
            s = s_ref[rs, :] + madd
            m_old = m_ref[rs, :]
            m_new = jnp.maximum(m_old, jnp.max(s, axis=1, keepdims=True))
            alpha = jnp.exp(m_old - m_new)
            p = jnp.exp(s - m_new)
            l_ref[rs, :] = alpha * l_ref[rs, :] + jnp.sum(p, axis=1, keepdims=True)
            acc_ref[rs, :] = alpha * acc_ref[rs, :]
            p_ref[rs, :] = p.astype(BF16)
            m_ref[rs, :] = m_new
        acc_ref[...] += jnp.dot(p_ref[...], kvt, preferred_element_type=F32)

    @pl.when(kt == nkt - 1)
    def _():
        dv = wuv_ref.shape[2]
        for h in range(DSA_HEADS):
            rs = head_rows(h)
            o = (acc_ref[rs, :] / l_ref[rs, :]).astype(BF16)
            o_ref[:, h * dv:(h + 1) * dv] = jnp.dot(o, wuv_ref[h], preferred_element_type=F32).astype(BF16)


def _dsa_attn(q, kv, mask, rel_bias, w_uv, batch, seq):
    n = kv.shape[0]
    nqb = seq // Q_BLOCK
    nkt = seq // KV_TILE
    sub = KV_TILE // LANES
    dv = w_uv.shape[2]
    rows = DSA_HEADS * Q_BLOCK

    def last_tile(qb):
        return (qb * Q_BLOCK + Q_BLOCK - 1) // KV_TILE

    return pl.pallas_call(
        _dsa_attn_kernel,
        grid=(batch, nqb, nkt),
        in_specs=[
            pl.BlockSpec((1, DSA_HEADS, Q_BLOCK, DSA_KV_RANK), lambda b, qb, kt: (b * nqb + qb, 0, 0, 0)),
            pl.BlockSpec((KV_TILE, DSA_KV_RANK),
                         lambda b, qb, kt: (b * nkt + jnp.minimum(kt, last_tile(qb)), 0)),
            pl.BlockSpec((1, 1, sub, Q_BLOCK, LANES),
                         lambda b, qb, kt: (b, qb, jnp.minimum(kt, last_tile(qb)), 0, 0)),
            pl.BlockSpec(memory_space=pltpu.SMEM),
            pl.BlockSpec(w_uv.shape, lambda b, qb, kt: (0, 0, 0)),
        ],
        out_specs=pl.BlockSpec((Q_BLOCK, DSA_HEADS * dv), lambda b, qb, kt: (b * nqb + qb, 0)),
        out_shape=jax.ShapeDtypeStruct((n, DSA_HEADS * dv), BF16),
        scratch_shapes=[
            pltpu.VMEM((rows, 1), F32),
            pltpu.VMEM((rows, 1), F32),
            pltpu.VMEM((rows, DSA_KV_RANK), F32),
            pltpu.VMEM((NEAR_TILES, rows, LANES), F32),
            pltpu.VMEM((rows, KV_TILE), F32),
            pltpu.VMEM((rows, KV_TILE), BF16),
        ],
        compiler_params=_params(("arbitrary", "arbitrary", "arbitrary")),
        name="dsa_attn",
    )(q, kv, mask, rel_bias, w_uv)


def _pad_cols(w, width):
    return jnp.pad(w, ((0, 0), (0, width - w.shape[1])))


def kernel(x, ln_mix_g, ln_ffn_g, w_ffn_gate, w_ffn_up, w_ffn_down, rel_bias, ev_w_in, ev_w_out, sgu_ln_g, sgu_ln_b, sgu_w_s, sgu_b_s, od_w_in, od_w_out, hgrn_lb, hgrn_norm_g, dsa_cq_g, dsa_ckv_g, dsa_w_uq, dsa_qnorm_g, dsa_w_qidx, dsa_w_uv):
    batch, seq, d = x.shape
    n = batch * seq
    depth = ln_mix_g.shape[0]
    ksel = min(TOPK_MAX, seq // 4)
    tm = 512
    xf = x.reshape(n, d)
    for layer in range(depth):
        j = layer // 2
        if layer % 2 == 0:
            z = _norm_matmul(xf, ln_mix_g[layer], ev_w_in[j].astype(BF16), tm=tm, tn=512)
            a1 = _retention(z, batch, seq)
            a2 = _sgu(z, sgu_ln_g[j], sgu_ln_b[j], sgu_w_s[j], sgu_b_s[j], rows=256)
            w_out = ev_w_out[j]
        else:
            w_in = od_w_in[j]
            gw = d // 2
            c = 4 * gw
            c_kidx = c + DSA_Q_RANK + DSA_KV_RANK
            w_main = w_in[:, :c].astype(BF16)
            w_dsa = jnp.concatenate([
                w_in[:, c:c_kidx],
                _pad_cols(w_in[:, c_kidx:c_kidx + IDX_DIM], LANES),
                _pad_cols(w_in[:, c_kidx + IDX_DIM:], LANES),
            ], axis=1).astype(BF16)
            z = _norm_matmul(xf, ln_mix_g[layer], w_main, tm=tm, tn=512)
            zd = _norm_matmul(xf, ln_mix_g[layer], w_dsa, tm=tm, tn=w_dsa.shape[1])
            a1 = _hgrn(z, batch, seq, hgrn_lb, hgrn_norm_g[j], layer)
            w_qit = jnp.pad(dsa_w_qidx[j].T.reshape(IDX_HEADS, IDX_DIM, DSA_Q_RANK),
                            ((0, 0), (0, LANES - IDX_DIM), (0, 0))).reshape(IDX_HEADS * LANES, DSA_Q_RANK)
            q, qit, kv, kix, wht = _dsa_prep(zd, dsa_cq_g[j], dsa_ckv_g[j], dsa_w_uq[j].astype(BF16),
                                             dsa_qnorm_g[j], w_qit.astype(BF16), tm=256)
            mask = _dsa_select(qit, wht, kix, batch, seq, ksel)
            a2 = _dsa_attn(q, kv, mask, rel_bias, dsa_w_uv[j].astype(BF16), batch, seq)
            w_out = od_w_out[j]
        xf = _outproj(a1, a2, w_out.astype(BF16), xf, tm=tm, tn=512)
        xf = _ffn(xf, ln_ffn_g[layer], w_ffn_gate[layer].astype(BF16), w_ffn_up[layer].astype(BF16),
                  w_ffn_down[layer].astype(BF16), tm=tm, tf=512)
    return xf.reshape(batch, seq, d)
```

```python
import functools
import math

import jax
import jax.numpy as jnp
from jax import lax
from jax.experimental import pallas as pl
from jax.experimental.pallas import tpu as pltpu

F32 = jnp.float32
BF16 = jnp.bfloat16
I32 = jnp.int32

EPS = 1e-6
CHUNK = 64
LANES = 128
ROPE_BASE = 10000.0
RET_HEADS = 4
SGU_WINDOW = 128
SGU_GROUPS = 4
HG_HEADS = 8
DSA_HEADS = 8
DSA_Q_RANK = 384
DSA_KV_RANK = 256
IDX_HEADS = 16
IDX_DIM = 64
TOPK_MAX = 256
Q_BLOCK = 128
KV_TILE = 512
LOG2E = math.log2(math.e)
REL_BUCKETS = 32
REL_MAX_DIST = 256
NEG_BIG = -1e30
INT_MIN = -(2 ** 31)

RET_BLOCK = 256
HG_BLOCK = 256
HG_FINE = 8
VMEM_LIMIT = 48 * 1024 * 1024
PROJ_ROWS = 1024
PROJ_COLS = 512
OUT_COLS = 1024
FFN_ROWS = 512
FFN_COLS = 512

_NT = (((1,), (1,)), ((), ()))
_TN = (((0,), (0,)), ((), ()))


def _params(semantics):
    return pltpu.CompilerParams(dimension_semantics=semantics, vmem_limit_bytes=VMEM_LIMIT)


def _silu(x):
    return x * jax.nn.sigmoid(x)


def _rms(x):
    return x * lax.rsqrt(jnp.mean(x * x, axis=-1, keepdims=True) + EPS)


def _norm_matmul_kernel(x_ref, g_ref, w_ref, o_ref, h_ref):
    @pl.when(pl.program_id(1) == 0)
    def _():
        h_ref[...] = (_rms(x_ref[...]) * g_ref[...]).astype(BF16)

    o_ref[...] = jnp.dot(h_ref[...], w_ref[...], preferred_element_type=F32)


def _norm_matmul(x, g, w, *, tm, tn):
    n, d = x.shape
    nout = w.shape[1]
    return pl.pallas_call(
        _norm_matmul_kernel,
        grid=(n // tm, nout // tn),
        in_specs=[
            pl.BlockSpec((tm, d), lambda i, j: (i, 0)),
            pl.BlockSpec((1, d), lambda i, j: (0, 0)),
            pl.BlockSpec((d, tn), lambda i, j: (0, j)),
        ],
        out_specs=pl.BlockSpec((tm, tn), lambda i, j: (i, j)),
        out_shape=jax.ShapeDtypeStruct((n, nout), F32),
        scratch_shapes=[pltpu.VMEM((tm, d), BF16)],
        compiler_params=_params(("arbitrary", "arbitrary")),
        name="norm_matmul",
    )(x, g.reshape(1, d), w)


def _outproj_kernel(a1_ref, a2_ref, w1_ref, w2_ref, r_ref, o_ref):
    acc = jnp.dot(a1_ref[...], w1_ref[...], preferred_element_type=F32)
    acc += jnp.dot(a2_ref[...], w2_ref[...], preferred_element_type=F32)
    o_ref[...] = r_ref[...] + acc


def _outproj(a1, a2, w, res, *, tm, tn):
    n, half = a1.shape
    d = w.shape[1]
    return pl.pallas_call(
        _outproj_kernel,
        grid=(n // tm, d // tn),
        in_specs=[
            pl.BlockSpec((tm, half), lambda i, j: (i, 0)),
            pl.BlockSpec((tm, half), lambda i, j: (i, 0)),
            pl.BlockSpec((half, tn), lambda i, j: (0, j)),
            pl.BlockSpec((half, tn), lambda i, j: (1, j)),
            pl.BlockSpec((tm, tn), lambda i, j: (i, j)),
        ],
        out_specs=pl.BlockSpec((tm, tn), lambda i, j: (i, j)),
        out_shape=jax.ShapeDtypeStruct((n, d), F32),
        compiler_params=_params(("arbitrary", "arbitrary")),
        name="outproj",
    )(a1, a2, w, w, res)


def _ffn_kernel(x_ref, g_ref, wg_ref, wu_ref, wd_ref, o_ref, h_ref):
    @pl.when(pl.program_id(1) == 0)
    def _():
        x = x_ref[...]
        h_ref[...] = (_rms(x) * g_ref[...]).astype(BF16)
        o_ref[...] = x

    h = h_ref[...]
    a = jnp.dot(h, wg_ref[...], preferred_element_type=F32)
    u = jnp.dot(h, wu_ref[...], preferred_element_type=F32)
    act = (_silu(a) * u).astype(BF16)
    o_ref[...] += jnp.dot(act, wd_ref[...], preferred_element_type=F32)


def _ffn(x, g, wg, wu, wd, *, tm, tf):
    n, d = x.shape
    dff = wg.shape[1]
    return pl.pallas_call(
        _ffn_kernel,
        grid=(n // tm, dff // tf),
        in_specs=[
            pl.BlockSpec((tm, d), lambda i, f: (i, 0)),
            pl.BlockSpec((1, d), lambda i, f: (0, 0)),
            pl.BlockSpec((d, tf), lambda i, f: (0, f)),
            pl.BlockSpec((d, tf), lambda i, f: (0, f)),
            pl.BlockSpec((tf, d), lambda i, f: (f, 0)),
        ],
        out_specs=pl.BlockSpec((tm, d), lambda i, f: (i, 0)),
        out_shape=jax.ShapeDtypeStruct((n, d), F32),
        scratch_shapes=[pltpu.VMEM((tm, d), BF16)],
        compiler_params=_params(("arbitrary", "arbitrary")),
        name="ffn",
    )(x, g.reshape(1, d), wg, wu, wd)


def _retention_kernel(q_ref, k_ref, v_ref, g_ref, cos_ref, sin_ref, d_ref, xi_ref, zeta_ref,
                      gl_ref, o_ref, state_ref):
    @pl.when(pl.program_id(2) == 0)
    def _():
        state_ref[...] = jnp.zeros_like(state_ref)

    cos = cos_ref[...]
    sin = sin_ref[...]
    half = cos.shape[1]

    def rot(t):
        t1, t2 = t[:, :half], t[:, half:]
        return jnp.concatenate([t1 * cos - t2 * sin, t1 * sin + t2 * cos], axis=1)

    q = rot(q_ref[...])
    k = rot(k_ref[...]) * (q.shape[1] ** -0.5)
    qb = q.astype(BF16)
    vb = v_ref[...].astype(BF16)
    scores = lax.dot_general(qb, k.astype(BF16), _NT, preferred_element_type=F32) * d_ref[0]
    intra = jnp.dot(scores.astype(BF16), vb, preferred_element_type=F32)
    state = state_ref[...]
    cross = jnp.dot(qb, state.astype(BF16), preferred_element_type=F32) * xi_ref[0]
    kz = (k * zeta_ref[0]).astype(BF16)
    state_ref[...] = state * gl_ref[0] + lax.dot_general(kz, vb, _TN, preferred_element_type=F32)
    g = g_ref[...]
    o_ref[...] = (_rms(intra + cross) * _silu(g)).astype(BF16)


def _retention_tables(seq, dk):
    blk = RET_BLOCK
    pos = jnp.arange(seq, dtype=F32)
    inv = ROPE_BASE ** (-jnp.arange(0, dk, 2, dtype=F32) / dk)
    ang = pos[:, None] * inv[None, :]
    log_gamma = jnp.log(1.0 - 2.0 ** (-5.0 - jnp.arange(RET_HEADS, dtype=F32)))
    i = jnp.arange(blk)
    same = (i[:, None] // CHUNK) == (i[None, :] // CHUNK)
    earlier = (i[None, :] // CHUNK) < (i[:, None] // CHUNK)
    diff = (i[:, None] - i[None, :]).astype(F32)
    dist = jnp.where(same, jnp.abs(diff), diff)
    decay = jnp.where((same | earlier)[None], jnp.exp(log_gamma[:, None, None] * dist[None]), 0.0)
    p = jnp.arange(blk, dtype=F32)
    xi = jnp.exp(log_gamma[:, None] * (p + 1.0))[:, :, None]
    zeta = jnp.exp(log_gamma[:, None] * (blk - 1.0 - p))[:, :, None]
    g_blk = jnp.broadcast_to(jnp.exp(log_gamma * blk)[:, None, None], (RET_HEADS, 1, dk))
    return jnp.cos(ang), jnp.sin(ang), decay, xi, zeta, g_blk


def _retention(z, batch, seq):
    n = z.shape[0]
    gw = z.shape[1] // 6
    dk = gw // RET_HEADS
    blk = RET_BLOCK
    nblk = seq // blk
    cos, sin, decay, xi, zeta, g_blk = _retention_tables(seq, dk)

    def zspec(part):
        return pl.BlockSpec((blk, dk), lambda b, h, c: (b * nblk + c, part * RET_HEADS + h))

    return pl.pallas_call(
        _retention_kernel,
        grid=(batch, RET_HEADS, nblk),
        in_specs=[
            zspec(0), zspec(1), zspec(2), zspec(3),
            pl.BlockSpec((blk, dk // 2), lambda b, h, c: (c, 0)),
            pl.BlockSpec((blk, dk // 2), lambda b, h, c: (c, 0)),
            pl.BlockSpec((1, blk, blk), lambda b, h, c: (h, 0, 0)),
            pl.BlockSpec((1, blk, 1), lambda b, h, c: (h, 0, 0)),
            pl.BlockSpec((1, blk, 1), lambda b, h, c: (h, 0, 0)),
            pl.BlockSpec((1, 1, dk), lambda b, h, c: (h, 0, 0)),
        ],
        out_specs=pl.BlockSpec((blk, dk), lambda b, h, c: (b * nblk + c, h)),
        out_shape=jax.ShapeDtypeStruct((n, gw), BF16),
        scratch_shapes=[pltpu.VMEM((dk, dk), F32)],
        compiler_params=_params(("arbitrary", "arbitrary", "arbitrary")),
        name="retention",
    )(z, z, z, z, cos, sin, decay, xi, zeta, g_blk)


def _gelu(x):
    return 0.5 * x * (1.0 + lax.erf(x * math.sqrt(0.5)))


def _sgu_kernel(u_ref, v_ref, lng_ref, lnb_ref, w_ref, b_ref, o_ref):
    rows, width = v_ref.shape
    dg = width // SGU_GROUPS
    v = _gelu(v_ref[...])
    mu = jnp.mean(v, axis=-1, keepdims=True)
    var = jnp.mean(jnp.square(v - mu), axis=-1, keepdims=True)
    vn = ((v - mu) * lax.rsqrt(var + EPS) * lng_ref[...] + lnb_ref[...]).astype(BF16)
    u = _gelu(u_ref[...])
    ri = lax.broadcasted_iota(I32, (SGU_WINDOW, SGU_WINDOW), 0) // CHUNK
    ci = lax.broadcasted_iota(I32, (SGU_WINDOW, SGU_WINDOW), 1) // CHUNK
    allowed = ci <= ri
    for g in range(SGU_GROUPS):
        wg = jnp.where(allowed, w_ref[g], 0.0).astype(BF16)
        bias = b_ref[g]
        for w in range(rows // SGU_WINDOW):
            rs = slice(w * SGU_WINDOW, (w + 1) * SGU_WINDOW)
            cs = slice(g * dg, (g + 1) * dg)
            mixed = jnp.dot(wg, vn[rs, cs], preferred_element_type=F32) + bias
            o_ref[rs, cs] = (u[rs, cs] * mixed).astype(BF16)


def _sgu(z, ln_g, ln_b, w_s, b_s, *, rows):
    n = z.shape[0]
    gw = z.shape[1] // 6
    return pl.pallas_call(
        _sgu_kernel,
        grid=(n // rows,),
        in_specs=[
            pl.BlockSpec((rows, gw), lambda i: (i, 4)),
            pl.BlockSpec((rows, gw), lambda i: (i, 5)),
            pl.BlockSpec((1, gw), lambda i: (0, 0)),
            pl.BlockSpec((1, gw), lambda i: (0, 0)),
            pl.BlockSpec((SGU_GROUPS, SGU_WINDOW, SGU_WINDOW), lambda i: (0, 0, 0)),
            pl.BlockSpec((SGU_GROUPS, SGU_WINDOW, 1), lambda i: (0, 0, 0)),
        ],
        out_specs=pl.BlockSpec((rows, gw), lambda i: (i, 0)),
        out_shape=jax.ShapeDtypeStruct((n, gw), BF16),
        compiler_params=_params(("arbitrary",)),
        name="sgu",
    )(z, z, ln_g.reshape(1, gw), ln_b.reshape(1, gw), w_s, b_s.reshape(SGU_GROUPS, SGU_WINDOW, 1))


def _hgrn_kernel(q_ref, f_ref, i_ref, g_ref, lb_ref, ng_ref, o_ref, st_ref, *, layer):
    rows, dk = q_ref.shape

    @pl.when(pl.program_id(2) == 0)
    def _():
        st_ref[...] = jnp.zeros_like(st_ref)

    lbp = lb_ref[...]
    e = jnp.exp(lbp - jnp.max(lbp, axis=0, keepdims=True))
    soft = e / jnp.sum(e, axis=0, keepdims=True)
    lb = jnp.sum(soft[1:layer + 1], axis=0, keepdims=True)

    f = lb + (1.0 - lb) * jax.nn.sigmoid(f_ref[...])
    lf = jnp.log(f)
    kk = 1.0 - f
    qa = _silu(q_ref[...])

    row = lax.broadcasted_iota(I32, (rows, dk), 0)
    bcum = lf
    sh = 1
    while sh < rows:
        bcum = bcum + jnp.where(row >= sh, pltpu.roll(bcum, sh, axis=0), 0.0)
        sh *= 2

    ti = lax.broadcasted_iota(I32, (rows, rows), 0)
    si = lax.broadcasted_iota(I32, (rows, rows), 1)
    attn = jnp.zeros((rows, rows), F32)
    hs = rows // 2
    while hs >= HG_FINE:
        bs = 2 * hs
        parts = [jnp.broadcast_to(bcum[b * bs + hs - 1:b * bs + hs, :], (bs, dk))
                 for b in range(rows // bs)]
        anchor = parts[0] if len(parts) == 1 else jnp.concatenate(parts, axis=0)
        upper = (row & (bs - 1)) >= hs
        qt = jnp.where(upper, qa * jnp.exp(jnp.minimum(bcum - anchor, 0.0)), 0.0)
        kt = jnp.where(upper, 0.0, kk * jnp.exp(jnp.minimum(anchor - bcum, 0.0)))
        a = lax.dot_general(qt.astype(BF16), kt.astype(BF16), _NT, preferred_element_type=F32)
        if bs < rows:
            a = jnp.where((ti & -bs) == (si & -bs), a, 0.0)
        attn = attn + a
        hs //= 2
    for delta in range(HG_FINE):
        if delta == 0:
            prod = qa * kk
        else:
            k_sh = pltpu.roll(kk, delta, axis=0)
            b_sh = pltpu.roll(bcum, delta, axis=0)
            prod = qa * k_sh * jnp.exp(jnp.minimum(bcum - b_sh, 0.0))
        col = jnp.sum(prod, axis=1, keepdims=True)
        hit = (si == ti - delta) & ((ti & (HG_FINE - 1)) >= delta)
        attn = attn + jnp.where(hit, col, 0.0)

    vb = i_ref[...].astype(BF16)
    intra = jnp.dot(attn.astype(BF16), vb, preferred_element_type=F32)
    st = st_ref[...]
    cross = lax.dot_general((qa * jnp.exp(bcum)).astype(BF16), st.astype(BF16), _NT,
                            preferred_element_type=F32)
    blast = bcum[rows - 1:rows, :]
    kb = (kk * jnp.exp(blast - bcum)).astype(BF16)
    st_ref[...] = st * jnp.exp(blast) + lax.dot_general(vb, kb, _TN, preferred_element_type=F32)
    g = g_ref[...]
    o_ref[...] = (_rms(intra + cross) * ng_ref[...] * _silu(g)).astype(BF16)


def _hgrn(z, batch, seq, lb_raw, norm_g, layer):
    n = z.shape[0]
    gw = z.shape[1] // 4
    dk = gw // HG_HEADS
    blk = HG_BLOCK
    nblk = seq // blk
    depth = lb_raw.shape[0]

    def zspec(part):
        return pl.BlockSpec((blk, dk), lambda b, h, c: (b * nblk + c, part * HG_HEADS + h))

    return pl.pallas_call(
        functools.partial(_hgrn_kernel, layer=layer),
        grid=(batch, HG_HEADS, nblk),
        in_specs=[
            zspec(0), zspec(1), zspec(2), zspec(3),
            pl.BlockSpec((depth, dk), lambda b, h, c: (0, h)),
            pl.BlockSpec((1, dk), lambda b, h, c: (0, h)),
        ],
        out_specs=pl.BlockSpec((blk, dk), lambda b, h, c: (b * nblk + c, h)),
        out_shape=jax.ShapeDtypeStruct((n, gw), BF16),
        scratch_shapes=[pltpu.VMEM((dk, dk), F32)],
        compiler_params=_params(("arbitrary", "arbitrary", "arbitrary")),
        name="hgrn2",
    )(z, z, z, z, lb_raw, norm_g.reshape(1, gw))


def _dsa_prep_kernel(zd_ref, cqg_ref, ckvg_ref, wuq_ref, qng_ref, wqit_ref,
                     q_ref, qit_ref, kv_ref, kix_ref, wht_ref):
    zd = zd_ref[...]
    cq = (_rms(zd[:, :DSA_Q_RANK]) * cqg_ref[...]).astype(BF16)
    qf = jnp.dot(cq, wuq_ref[...], preferred_element_type=F32)
    for i in range(q_ref.shape[0]):
        rs = slice(i * Q_BLOCK, (i + 1) * Q_BLOCK)
        for h in range(DSA_HEADS):
            cs = slice(h * DSA_KV_RANK, (h + 1) * DSA_KV_RANK)
            q_ref[i, h] = (_rms(qf[rs, cs]) * qng_ref[...] * (DSA_KV_RANK ** -0.5 * LOG2E)).astype(BF16)
    qit = lax.dot_general(wqit_ref[...], cq, _NT, preferred_element_type=F32)
    qit = (qit * (IDX_DIM ** -0.5)).astype(BF16)
    for i in range(q_ref.shape[0]):
        for h in range(IDX_HEADS):
            c = (i * IDX_HEADS + h) * Q_BLOCK
            qit_ref[:, c:c + Q_BLOCK] = qit[h * LANES:(h + 1) * LANES, i * Q_BLOCK:(i + 1) * Q_BLOCK]
    c0 = DSA_Q_RANK
    c1 = c0 + DSA_KV_RANK
    kv_ref[...] = (_rms(zd[:, c0:c1]) * ckvg_ref[...]).astype(BF16)
    kix_ref[...] = zd[:, c1:c1 + LANES].astype(BF16)
    wht = jnp.transpose(zd[:, c1 + LANES:c1 + 2 * LANES] * (IDX_HEADS ** -0.5))
    wht_ref[...] = wht[:IDX_HEADS, :]


def _dsa_prep(zd, cq_g, ckv_g, w_uq, qn_g, w_qit, *, tm):
    n, wd = zd.shape
    dq = w_uq.shape[1]
    dqi = w_qit.shape[0]
    full = lambda i: (0, 0)
    rows = lambda i: (i, 0)
    cols = lambda i: (0, i)
    return pl.pallas_call(
        _dsa_prep_kernel,
        grid=(n // tm,),
        in_specs=[
            pl.BlockSpec((tm, wd), rows),
            pl.BlockSpec((1, DSA_Q_RANK), full),
            pl.BlockSpec((1, DSA_KV_RANK), full),
            pl.BlockSpec((DSA_Q_RANK, dq), full),
            pl.BlockSpec((1, DSA_KV_RANK), full),
            pl.BlockSpec((dqi, DSA_Q_RANK), full),
        ],
        out_specs=[
            pl.BlockSpec((tm // Q_BLOCK, DSA_HEADS, Q_BLOCK, DSA_KV_RANK), lambda i: (i, 0, 0, 0)),
            pl.BlockSpec((LANES, IDX_HEADS * tm), cols),
            pl.BlockSpec((tm, DSA_KV_RANK), rows),
            pl.BlockSpec((tm, LANES), rows),
            pl.BlockSpec((IDX_HEADS, tm), cols),
        ],
        out_shape=[
            jax.ShapeDtypeStruct((n // Q_BLOCK, DSA_HEADS, Q_BLOCK, DSA_KV_RANK), BF16),
            jax.ShapeDtypeStruct((LANES, IDX_HEADS * n), BF16),
            jax.ShapeDtypeStruct((n, DSA_KV_RANK), BF16),
            jax.ShapeDtypeStruct((n, LANES), BF16),
            jax.ShapeDtypeStruct((IDX_HEADS, n), F32),
        ],
        compiler_params=_params(("arbitrary",)),
        name="dsa_prep",
    )(zd, cq_g.reshape(1, -1), ckv_g.reshape(1, -1), w_uq, qn_g.reshape(1, -1), w_qit)


def _dsa_select_kernel(qit_ref, wht_ref, kix_ref, m_ref, key_ref, jc_ref, *, ksel, idx_bits):
    qb = pl.program_id(1)
    ntile = qb + 1
    ntile_all = m_ref.shape[2]
    rowi = lax.broadcasted_iota(I32, (LANES, Q_BLOCK), 0)
    coli = lax.broadcasted_iota(I32, (LANES, Q_BLOCK), 1)
    q_chunk = (qb * Q_BLOCK + coli) // CHUNK
    pairs = IDX_HEADS // 2
    w_pair = [jnp.concatenate([wht_ref[2 * p:2 * p + 1, :], wht_ref[2 * p + 1:2 * p + 2, :]], axis=1)
              for p in range(pairs)]

    def tile_rows(j):
        return pl.ds(pl.multiple_of(j * LANES, LANES), LANES)

    def admissible(j):
        return ((j * LANES + rowi) // CHUNK) <= q_chunk

    def score_tile(j, carry):
        kt = kix_ref[tile_rows(j), :]
        sc = None
        for p in range(pairs):
            s2 = jnp.dot(kt, qit_ref[:, 2 * p * Q_BLOCK:2 * (p + 1) * Q_BLOCK],
                         preferred_element_type=F32)
            c2 = w_pair[p] * jnp.maximum(s2, 0.0)
            c = c2[:, :Q_BLOCK] + c2[:, Q_BLOCK:]
            sc = c if sc is None else sc + c
        bits = pltpu.bitcast(sc, I32)
        key = bits ^ ((bits >> 31) & 0x7FFFFFFF)
        key_ref[tile_rows(j), :] = jnp.where(admissible(j), key, INT_MIN)
        return carry

    @pl.when(qb == 0)
    def _():
        key_ref[...] = jnp.full(key_ref.shape, INT_MIN, I32)

    lax.fori_loop(0, ntile, score_tile, 0)

    def count(pred_fn):
        def body(jj, acc):
            for t in range(2):
                j = 2 * jj + t
                acc = acc + pred_fn(j, key_ref[tile_rows(j), :]).astype(I32)
            return acc
        acc = lax.fori_loop(0, (ntile + 1) // 2, body, jnp.zeros((LANES, Q_BLOCK), I32))
        return jnp.sum(acc, axis=0, keepdims=True)

    def value_bit(i, c):
        t_u, cnt_t = c
        cand_u = t_u | jnp.left_shift(jnp.int32(1), 31 - i)
        cand = cand_u ^ INT_MIN
        cnt = count(lambda j, k: k >= cand)
        take = cnt >= ksel
        return jnp.where(take, cand_u, t_u), jnp.where(take, cnt, cnt_t)

    cnt0 = (q_chunk[0:1, :] + 1) * CHUNK
    t_u, cnt_t = lax.fori_loop(0, 32, value_bit, (jnp.zeros((1, Q_BLOCK), I32), cnt0))
    thr = t_u ^ INT_MIN
    tied = jnp.max(jnp.where(cnt_t > ksel, 1.0, 0.0))

    jc_ref[...] = jnp.full(jc_ref.shape, 2 ** 31 - 1, I32)

    @pl.when(tied > 0.0)
    def _():
        need = ksel - count(lambda j, k: k > thr)

        def index_bit(i, j_c):
            cand = j_c | jnp.left_shift(jnp.int32(1), idx_bits - 1 - i)
            cnt = count(lambda j, k: (k == thr) & ((j * LANES + rowi) < cand))
            return jnp.where(cnt < need, cand, j_c)

        j_c = lax.fori_loop(0, idx_bits, index_bit, jnp.zeros((1, Q_BLOCK), I32))
        jc_ref[...] = jnp.broadcast_to(j_c, jc_ref.shape)

    j_c = jc_ref[0:1, :]
    eye = (rowi == coli).astype(BF16)

    group = KV_TILE // LANES

    def write_group(g, carry):
        for t in range(group):
            j = g * group + t
            k = key_ref[tile_rows(j), :]
            sel = (k > thr) | ((k == thr) & ((j * LANES + rowi) <= j_c))
            sel = jnp.where(sel & admissible(j), 1.0, 0.0).astype(BF16)
            sel_t = lax.dot_general(eye, sel, _NT, preferred_element_type=F32)
            m_ref[0, 0, j] = ((sel_t - 1.0) * -NEG_BIG).astype(BF16)
        return carry

    ngroup = (ntile + group - 1) // group
    lax.fori_loop(0, ngroup, write_group, 0)

    def blank_tile(j, carry):
        m_ref[0, 0, j] = jnp.full((Q_BLOCK, LANES), NEG_BIG, BF16)
        return carry

    lax.fori_loop(ngroup * group, ntile_all, blank_tile, 0)


def _dsa_select(qit, wht, kix, batch, seq, ksel):
    nqb = seq // Q_BLOCK
    nkt = seq // LANES
    return pl.pallas_call(
        functools.partial(_dsa_select_kernel, ksel=ksel, idx_bits=int(math.log2(seq))),
        grid=(batch, nqb),
        in_specs=[
            pl.BlockSpec((LANES, IDX_HEADS * Q_BLOCK), lambda b, q: (0, b * nqb + q)),
            pl.BlockSpec((IDX_HEADS, Q_BLOCK), lambda b, q: (0, b * nqb + q)),
            pl.BlockSpec((seq, LANES), lambda b, q: (b, 0)),
        ],
        out_specs=pl.BlockSpec((1, 1, nkt, Q_BLOCK, LANES), lambda b, q: (b, q, 0, 0, 0)),
        out_shape=jax.ShapeDtypeStruct((batch, nqb, nkt, Q_BLOCK, LANES), BF16),
        scratch_shapes=[pltpu.VMEM((seq, Q_BLOCK), I32), pltpu.VMEM((8, Q_BLOCK), I32)],
        compiler_params=_params(("arbitrary", "arbitrary")),
        name="dsa_select",
    )(qit, wht, kix)


def _rel_bucket(rel):
    nb = REL_BUCKETS // 2
    max_exact = nb // 2
    ret = jnp.where(rel > 0, nb, 0)
    n = jnp.abs(rel)
    nf = jnp.maximum(n, 1).astype(F32)
    large = max_exact + (jnp.log(nf / max_exact) / math.log(REL_MAX_DIST / max_exact)
                         * (nb - max_exact)).astype(I32)
    large = jnp.minimum(large, nb - 1)
    return ret + jnp.where(n < max_exact, n, large)


NEAR_TILES = 3


def _dsa_attn_kernel(q_ref, kv_ref, mask_ref, rb_ref, wuv_ref, o_ref,
                     m_ref, l_ref, acc_ref, corr_ref, s_ref, p_ref):
    b, qb, kt = pl.program_id(0), pl.program_id(1), pl.program_id(2)
    nkt = pl.num_programs(2)
    sub = KV_TILE // LANES
    far_bucket = REL_BUCKETS // 2 - 1

    def head_rows(h):
        return slice(h * Q_BLOCK, (h + 1) * Q_BLOCK)

    @pl.when((b == 0) & (qb == 0) & (kt == 0))
    def _():
        ti = lax.broadcasted_iota(I32, (Q_BLOCK, LANES), 0)
        si = lax.broadcasted_iota(I32, (Q_BLOCK, LANES), 1)
        for oi in range(NEAR_TILES):
            bucket = _rel_bucket((oi - (NEAR_TILES - 1)) * LANES + si - ti)
            for h in range(DSA_HEADS):
                tbl = jnp.zeros((Q_BLOCK, LANES), F32)
                for bk in range(REL_BUCKETS):
                    tbl = jnp.where(bucket == bk, rb_ref[bk, h], tbl)
                corr_ref[oi, head_rows(h), :] = (tbl - rb_ref[far_bucket, h]) * LOG2E

    @pl.when(kt == 0)
    def _():
        m_ref[...] = jnp.full_like(m_ref, NEG_BIG)
        l_ref[...] = jnp.zeros_like(l_ref)
        acc_ref[...] = jnp.zeros_like(acc_ref)

    @pl.when(kt * KV_TILE < (qb + 1) * Q_BLOCK)
    def _():
        kvt = kv_ref[...]
        q_all = q_ref[0].reshape(DSA_HEADS * Q_BLOCK, DSA_KV_RANK)
        half = DSA_HEADS * Q_BLOCK // 2
        for part in range(2):
            rs = slice(part * half, (part + 1) * half)
            s_ref[rs, :] = lax.dot_general(q_all[rs], kvt, _NT, preferred_element_type=F32)
        for j in range(sub):
            d = kt * sub + j - qb

            @pl.when((d > -NEAR_TILES) & (d <= 0))
            def _(j=j, d=d):
                s_ref[:, j * LANES:(j + 1) * LANES] += corr_ref[d + NEAR_TILES - 1]

        madd = jnp.concatenate([mask_ref[0, 0, j].astype(F32) for j in range(sub)], axis=1)
        for h in range(DSA_HEADS):
            rs = head_rows(h)
            s = s_ref[rs, :] + madd
            m_old = m_ref[rs, :]
            m_new = jnp.maximum(m_old, jnp.max(s, axis=1, keepdims=True))
            alpha = jnp.exp2(m_old - m_new)
            p = jnp.exp2(s - m_new)
            l_ref[rs, :] = alpha * l_ref[rs, :] + jnp.sum(p, axis=1, keepdims=True)
            acc_ref[rs, :] = alpha * acc_ref[rs, :]
            p_ref[rs, :] = p.astype(BF16)
            m_ref[rs, :] = m_new
        half = DSA_HEADS * Q_BLOCK // 2
        for part in range(2):
            rs = slice(part * half, (part + 1) * half)
            acc_ref[rs, :] += jnp.dot(p_ref[rs, :], kvt, preferred_element_type=F32)

    @pl.when(kt == nkt - 1)
    def _():
        dv = wuv_ref.shape[2]
        for h in range(DSA_HEADS):
            rs = head_rows(h)
            o = (acc_ref[rs, :] / l_ref[rs, :]).astype(BF16)
            o_ref[:, h * dv:(h + 1) * dv] = jnp.dot(o, wuv_ref[h], preferred_element_type=F32).astype(BF16)


def _dsa_attn(q, kv, mask, rel_bias, w_uv, batch, seq):
    n = kv.shape[0]
    nqb = seq // Q_BLOCK
    nkt = seq // KV_TILE
    sub = KV_TILE // LANES
    dv = w_uv.shape[2]
    rows = DSA_HEADS * Q_BLOCK

    def last_tile(qb):
        return (qb * Q_BLOCK + Q_BLOCK - 1) // KV_TILE

    return pl.pallas_call(
        _dsa_attn_kernel,
        grid=(batch, nqb, nkt),
        in_specs=[
            pl.BlockSpec((1, DSA_HEADS, Q_BLOCK, DSA_KV_RANK), lambda b, qb, kt: (b * nqb + qb, 0, 0, 0)),
            pl.BlockSpec((KV_TILE, DSA_KV_RANK),
                         lambda b, qb, kt: (b * nkt + jnp.minimum(kt, last_tile(qb)), 0)),
            pl.BlockSpec((1, 1, sub, Q_BLOCK, LANES),
                         lambda b, qb, kt: (b, qb, jnp.minimum(kt, last_tile(qb)), 0, 0)),
            pl.BlockSpec(memory_space=pltpu.SMEM),
            pl.BlockSpec(w_uv.shape, lambda b, qb, kt: (0, 0, 0)),
        ],
        out_specs=pl.BlockSpec((Q_BLOCK, DSA_HEADS * dv), lambda b, qb, kt: (b * nqb + qb, 0)),
        out_shape=jax.ShapeDtypeStruct((n, DSA_HEADS * dv), BF16),
        scratch_shapes=[
            pltpu.VMEM((rows, 1), F32),
            pltpu.VMEM((rows, 1), F32),
            pltpu.VMEM((rows, DSA_KV_RANK), F32),
            pltpu.VMEM((NEAR_TILES, rows, LANES), F32),
            pltpu.VMEM((rows, KV_TILE), F32),
            pltpu.VMEM((rows, KV_TILE), BF16),
        ],
        compiler_params=_params(("arbitrary", "arbitrary", "arbitrary")),
        name="dsa_attn",
    )(q, kv, mask, rel_bias, w_uv)


def _pad_cols(w, width):
    return jnp.pad(w, ((0, 0), (0, width - w.shape[1])))


def kernel(x, ln_mix_g, ln_ffn_g, w_ffn_gate, w_ffn_up, w_ffn_down, rel_bias, ev_w_in, ev_w_out, sgu_ln_g, sgu_ln_b, sgu_w_s, sgu_b_s, od_w_in, od_w_out, hgrn_lb, hgrn_norm_g, dsa_cq_g, dsa_ckv_g, dsa_w_uq, dsa_qnorm_g, dsa_w_qidx, dsa_w_uv):
    batch, seq, d = x.shape
    n = batch * seq
    depth = ln_mix_g.shape[0]
    ksel = min(TOPK_MAX, seq // 4)
    tm = min(PROJ_ROWS, n)
    xf = x.reshape(n, d)
    for layer in range(depth):
        j = layer // 2
        if layer % 2 == 0:
            z = _norm_matmul(xf, ln_mix_g[layer], ev_w_in[j].astype(BF16), tm=tm, tn=PROJ_COLS)
            a1 = _retention(z, batch, seq)
            a2 = _sgu(z, sgu_ln_g[j], sgu_ln_b[j], sgu_w_s[j], sgu_b_s[j], rows=256)
            w_out = ev_w_out[j]
        else:
            w_in = od_w_in[j]
            gw = d // 2
            c = 4 * gw
            c_kidx = c + DSA_Q_RANK + DSA_KV_RANK
            w_main = w_in[:, :c].astype(BF16)
            w_dsa = jnp.concatenate([
                w_in[:, c:c_kidx],
                _pad_cols(w_in[:, c_kidx:c_kidx + IDX_DIM], LANES),
                _pad_cols(w_in[:, c_kidx + IDX_DIM:], LANES),
            ], axis=1).astype(BF16)
            z = _norm_matmul(xf, ln_mix_g[layer], w_main, tm=tm, tn=PROJ_COLS)
            zd = _norm_matmul(xf, ln_mix_g[layer], w_dsa, tm=tm, tn=w_dsa.shape[1])
            a1 = _hgrn(z, batch, seq, hgrn_lb, hgrn_norm_g[j], layer)
            w_qit = jnp.pad(dsa_w_qidx[j].T.reshape(IDX_HEADS, IDX_DIM, DSA_Q_RANK),
                            ((0, 0), (0, LANES - IDX_DIM), (0, 0))).reshape(IDX_HEADS * LANES, DSA_Q_RANK)
            q, qit, kv, kix, wht = _dsa_prep(zd, dsa_cq_g[j], dsa_ckv_g[j], dsa_w_uq[j].astype(BF16),
                                             dsa_qnorm_g[j], w_qit.astype(BF16), tm=256)
            mask = _dsa_select(qit, wht, kix, batch, seq, ksel)
            a2 = _dsa_attn(q, kv, mask, rel_bias, dsa_w_uv[j].astype(BF16), batch, seq)
            w_out = od_w_out[j]
        xf = _outproj(a1, a2, w_out.astype(BF16), xf, tm=tm, tn=OUT_COLS)
        xf = _ffn(xf, ln_ffn_g[layer], w_ffn_gate[layer].astype(BF16), w_ffn_up[layer].astype(BF16),
                  w_ffn_down[layer].astype(BF16), tm=min(FFN_ROWS, n), tf=FFN_COLS)
    return xf.reshape(batch, seq, d)
```

```python
import functools
import math

import jax
import jax.numpy as jnp
from jax import lax
from jax.experimental import pallas as pl
from jax.experimental.pallas import tpu as pltpu

F32 = jnp.float32
BF16 = jnp.bfloat16
I32 = jnp.int32

EPS = 1e-6
CHUNK = 64
LANES = 128
ROPE_BASE = 10000.0
RET_HEADS = 4
SGU_WINDOW = 128
SGU_GROUPS = 4
HG_HEADS = 8
DSA_HEADS = 8
DSA_Q_RANK = 384
DSA_KV_RANK = 256
IDX_HEADS = 16
IDX_DIM = 64
TOPK_MAX = 256
Q_BLOCK = 128
KV_TILE = 512
LOG2E = math.log2(math.e)
REL_BUCKETS = 32
REL_MAX_DIST = 256
NEG_BIG = -1e30
INT_MIN = -(2 ** 31)

RET_BLOCK = 256
HG_BLOCK = 256
HG_FINE = 8
VMEM_LIMIT = 48 * 1024 * 1024
PROJ_ROWS = 1024
PROJ_COLS = 512
OUT_COLS = 1024
FFN_ROWS = 512
FFN_COLS = 512

_NT = (((1,), (1,)), ((), ()))
_TN = (((0,), (0,)), ((), ()))


def _params(semantics):
    return pltpu.CompilerParams(dimension_semantics=semantics, vmem_limit_bytes=VMEM_LIMIT)


def _silu(x):
    return x * jax.nn.sigmoid(x)


def _rms(x):
    return x * lax.rsqrt(jnp.mean(x * x, axis=-1, keepdims=True) + EPS)


def _norm_matmul_kernel(x_ref, g_ref, w_ref, o_ref, h_ref):
    @pl.when(pl.program_id(1) == 0)
    def _():
        h_ref[...] = (_rms(x_ref[...]) * g_ref[...]).astype(BF16)

    o_ref[...] = jnp.dot(h_ref[...], w_ref[...], preferred_element_type=F32)


def _norm_matmul(x, g, w, *, tm, tn):
    n, d = x.shape
    nout = w.shape[1]
    return pl.pallas_call(
        _norm_matmul_kernel,
        grid=(n // tm, nout // tn),
        in_specs=[
            pl.BlockSpec((tm, d), lambda i, j: (i, 0)),
            pl.BlockSpec((1, d), lambda i, j: (0, 0)),
            pl.BlockSpec((d, tn), lambda i, j: (0, j)),
        ],
        out_specs=pl.BlockSpec((tm, tn), lambda i, j: (i, j)),
        out_shape=jax.ShapeDtypeStruct((n, nout), F32),
        scratch_shapes=[pltpu.VMEM((tm, d), BF16)],
        compiler_params=_params(("arbitrary", "arbitrary")),
        name="norm_matmul",
    )(x, g.reshape(1, d), w)


def _outproj_kernel(a1_ref, a2_ref, w1_ref, w2_ref, r_ref, o_ref):
    acc = jnp.dot(a1_ref[...], w1_ref[...], preferred_element_type=F32)
    acc += jnp.dot(a2_ref[...], w2_ref[...], preferred_element_type=F32)
    o_ref[...] = r_ref[...] + acc


def _outproj(a1, a2, w, res, *, tm, tn):
    n, half = a1.shape
    d = w.shape[1]
    return pl.pallas_call(
        _outproj_kernel,
        grid=(n // tm, d // tn),
        in_specs=[
            pl.BlockSpec((tm, half), lambda i, j: (i, 0)),
            pl.BlockSpec((tm, half), lambda i, j: (i, 0)),
            pl.BlockSpec((half, tn), lambda i, j: (0, j)),
            pl.BlockSpec((half, tn), lambda i, j: (1, j)),
            pl.BlockSpec((tm, tn), lambda i, j: (i, j)),
        ],
        out_specs=pl.BlockSpec((tm, tn), lambda i, j: (i, j)),
        out_shape=jax.ShapeDtypeStruct((n, d), F32),
        compiler_params=_params(("arbitrary", "arbitrary")),
        name="outproj",
    )(a1, a2, w, w, res)


def _ffn_kernel(x_ref, g_ref, wg_ref, wu_ref, wd_ref, o_ref, h_ref):
    @pl.when(pl.program_id(1) == 0)
    def _():
        x = x_ref[...]
        h_ref[...] = (_rms(x) * g_ref[...]).astype(BF16)
        o_ref[...] = x

    h = h_ref[...]
    a = jnp.dot(h, wg_ref[...], preferred_element_type=F32)
    u = jnp.dot(h, wu_ref[...], preferred_element_type=F32)
    act = (_silu(a) * u).astype(BF16)
    o_ref[...] += jnp.dot(act, wd_ref[...], preferred_element_type=F32)


def _ffn(x, g, wg, wu, wd, *, tm, tf):
    n, d = x.shape
    dff = wg.shape[1]
    return pl.pallas_call(
        _ffn_kernel,
        grid=(n // tm, dff // tf),
        in_specs=[
            pl.BlockSpec((tm, d), lambda i, f: (i, 0)),
            pl.BlockSpec((1, d), lambda i, f: (0, 0)),
            pl.BlockSpec((d, tf), lambda i, f: (0, f)),
            pl.BlockSpec((d, tf), lambda i, f: (0, f)),
            pl.BlockSpec((tf, d), lambda i, f: (f, 0)),
        ],
        out_specs=pl.BlockSpec((tm, d), lambda i, f: (i, 0)),
        out_shape=jax.ShapeDtypeStruct((n, d), F32),
        scratch_shapes=[pltpu.VMEM((tm, d), BF16)],
        compiler_params=_params(("arbitrary", "arbitrary")),
        name="ffn",
    )(x, g.reshape(1, d), wg, wu, wd)


def _retention_kernel(q_ref, k_ref, v_ref, g_ref, cos_ref, sin_ref, d_ref, xi_ref, zeta_ref,
                      gl_ref, o_ref, state_ref):
    @pl.when(pl.program_id(2) == 0)
    def _():
        state_ref[...] = jnp.zeros_like(state_ref)

    cos = cos_ref[...]
    sin = sin_ref[...]
    half = cos.shape[1]

    def rot(t):
        t1, t2 = t[:, :half], t[:, half:]
        return jnp.concatenate([t1 * cos - t2 * sin, t1 * sin + t2 * cos], axis=1)

    q = rot(q_ref[...])
    k = rot(k_ref[...]) * (q.shape[1] ** -0.5)
    qb = q.astype(BF16)
    vb = v_ref[...].astype(BF16)
    scores = lax.dot_general(qb, k.astype(BF16), _NT, preferred_element_type=F32) * d_ref[0]
    intra = jnp.dot(scores.astype(BF16), vb, preferred_element_type=F32)
    state = state_ref[...]
    cross = jnp.dot(qb, state.astype(BF16), preferred_element_type=F32) * xi_ref[0]
    kz = (k * zeta_ref[0]).astype(BF16)
    state_ref[...] = state * gl_ref[0] + lax.dot_general(kz, vb, _TN, preferred_element_type=F32)
    g = g_ref[...]
    o_ref[...] = (_rms(intra + cross) * _silu(g)).astype(BF16)


def _retention_tables(seq, dk):
    blk = RET_BLOCK
    pos = jnp.arange(seq, dtype=F32)
    inv = ROPE_BASE ** (-jnp.arange(0, dk, 2, dtype=F32) / dk)
    ang = pos[:, None] * inv[None, :]
    log_gamma = jnp.log(1.0 - 2.0 ** (-5.0 - jnp.arange(RET_HEADS, dtype=F32)))
    i = jnp.arange(blk)
    same = (i[:, None] // CHUNK) == (i[None, :] // CHUNK)
    earlier = (i[None, :] // CHUNK) < (i[:, None] // CHUNK)
    diff = (i[:, None] - i[None, :]).astype(F32)
    dist = jnp.where(same, jnp.abs(diff), diff)
    decay = jnp.where((same | earlier)[None], jnp.exp(log_gamma[:, None, None] * dist[None]), 0.0)
    p = jnp.arange(blk, dtype=F32)
    xi = jnp.exp(log_gamma[:, None] * (p + 1.0))[:, :, None]
    zeta = jnp.exp(log_gamma[:, None] * (blk - 1.0 - p))[:, :, None]
    g_blk = jnp.broadcast_to(jnp.exp(log_gamma * blk)[:, None, None], (RET_HEADS, 1, dk))
    return jnp.cos(ang), jnp.sin(ang), decay, xi, zeta, g_blk


def _retention(z, batch, seq):
    n = z.shape[0]
    gw = z.shape[1] // 6
    dk = gw // RET_HEADS
    blk = RET_BLOCK
    nblk = seq // blk
    cos, sin, decay, xi, zeta, g_blk = _retention_tables(seq, dk)

    def zspec(part):
        return pl.BlockSpec((blk, dk), lambda b, h, c: (b * nblk + c, part * RET_HEADS + h))

    return pl.pallas_call(
        _retention_kernel,
        grid=(batch, RET_HEADS, nblk),
        in_specs=[
            zspec(0), zspec(1), zspec(2), zspec(3),
            pl.BlockSpec((blk, dk // 2), lambda b, h, c: (c, 0)),
            pl.BlockSpec((blk, dk // 2), lambda b, h, c: (c, 0)),
            pl.BlockSpec((1, blk, blk), lambda b, h, c: (h, 0, 0)),
            pl.BlockSpec((1, blk, 1), lambda b, h, c: (h, 0, 0)),
            pl.BlockSpec((1, blk, 1), lambda b, h, c: (h, 0, 0)),
            pl.BlockSpec((1, 1, dk), lambda b, h, c: (h, 0, 0)),
        ],
        out_specs=pl.BlockSpec((blk, dk), lambda b, h, c: (b * nblk + c, h)),
        out_shape=jax.ShapeDtypeStruct((n, gw), BF16),
        scratch_shapes=[pltpu.VMEM((dk, dk), F32)],
        compiler_params=_params(("arbitrary", "arbitrary", "arbitrary")),
        name="retention",
    )(z, z, z, z, cos, sin, decay, xi, zeta, g_blk)


def _gelu(x):
    return 0.5 * x * (1.0 + lax.erf(x * math.sqrt(0.5)))


def _sgu_kernel(u_ref, v_ref, lng_ref, lnb_ref, w_ref, b_ref, o_ref):
    rows, width = v_ref.shape
    dg = width // SGU_GROUPS
    v = _gelu(v_ref[...])
    mu = jnp.mean(v, axis=-1, keepdims=True)
    var = jnp.mean(jnp.square(v - mu), axis=-1, keepdims=True)
    vn = ((v - mu) * lax.rsqrt(var + EPS) * lng_ref[...] + lnb_ref[...]).astype(BF16)
    u = _gelu(u_ref[...])
    ri = lax.broadcasted_iota(I32, (SGU_WINDOW, SGU_WINDOW), 0) // CHUNK
    ci = lax.broadcasted_iota(I32, (SGU_WINDOW, SGU_WINDOW), 1) // CHUNK
    allowed = ci <= ri
    for g in range(SGU_GROUPS):
        wg = jnp.where(allowed, w_ref[g], 0.0).astype(BF16)
        bias = b_ref[g]
        for w in range(rows // SGU_WINDOW):
            rs = slice(w * SGU_WINDOW, (w + 1) * SGU_WINDOW)
            cs = slice(g * dg, (g + 1) * dg)
            mixed = jnp.dot(wg, vn[rs, cs], preferred_element_type=F32) + bias
            o_ref[rs, cs] = (u[rs, cs] * mixed).astype(BF16)


def _sgu(z, ln_g, ln_b, w_s, b_s, *, rows):
    n = z.shape[0]
    gw = z.shape[1] // 6
    return pl.pallas_call(
        _sgu_kernel,
        grid=(n // rows,),
        in_specs=[
            pl.BlockSpec((rows, gw), lambda i: (i, 4)),
            pl.BlockSpec((rows, gw), lambda i: (i, 5)),
            pl.BlockSpec((1, gw), lambda i: (0, 0)),
            pl.BlockSpec((1, gw), lambda i: (0, 0)),
            pl.BlockSpec((SGU_GROUPS, SGU_WINDOW, SGU_WINDOW), lambda i: (0, 0, 0)),
            pl.BlockSpec((SGU_GROUPS, SGU_WINDOW, 1), lambda i: (0, 0, 0)),
        ],
        out_specs=pl.BlockSpec((rows, gw), lambda i: (i, 0)),
        out_shape=jax.ShapeDtypeStruct((n, gw), BF16),
        compiler_params=_params(("arbitrary",)),
        name="sgu",
    )(z, z, ln_g.reshape(1, gw), ln_b.reshape(1, gw), w_s, b_s.reshape(SGU_GROUPS, SGU_WINDOW, 1))


def _hgrn_kernel(q_ref, f_ref, i_ref, g_ref, lb_ref, ng_ref, tri_ref, o_ref,
                 st_ref, ksh_ref, bsh_ref, vsh_ref, *, layer):
    rows, dk = q_ref.shape

    @pl.when(pl.program_id(2) == 0)
    def _():
        st_ref[...] = jnp.zeros_like(st_ref)

    lbp = lb_ref[...]
    e = jnp.exp(lbp - jnp.max(lbp, axis=0, keepdims=True))
    soft = e / jnp.sum(e, axis=0, keepdims=True)
    lb = jnp.sum(soft[1:layer + 1], axis=0, keepdims=True)

    f = lb + (1.0 - lb) * jax.nn.sigmoid(f_ref[...])
    lf = jnp.log(f)
    kk = 1.0 - f
    qa = _silu(q_ref[...])

    tri = tri_ref[...]
    bcum = None
    rest = lf
    for _ in range(3):
        term = rest.astype(BF16)
        part = jnp.dot(tri, term, preferred_element_type=F32)
        bcum = part if bcum is None else bcum + part
        rest = rest - term.astype(F32)

    row = lax.broadcasted_iota(I32, (rows, dk), 0)
    ti = lax.broadcasted_iota(I32, (rows, rows), 0)
    si = lax.broadcasted_iota(I32, (rows, rows), 1)
    attn = jnp.zeros((rows, rows), F32)
    hs = rows // 2
    while hs >= HG_FINE:
        bs = 2 * hs
        parts = [jnp.broadcast_to(bcum[b * bs + hs - 1:b * bs + hs, :], (bs, dk))
                 for b in range(rows // bs)]
        anchor = parts[0] if len(parts) == 1 else jnp.concatenate(parts, axis=0)
        upper = (row & (bs - 1)) >= hs
        fac = jnp.exp(-jnp.abs(bcum - anchor))
        qt = jnp.where(upper, qa * fac, 0.0)
        kt = jnp.where(upper, 0.0, kk * fac)
        a = lax.dot_general(qt.astype(BF16), kt.astype(BF16), _NT, preferred_element_type=F32)
        if bs < rows:
            a = jnp.where((ti & -bs) == (si & -bs), a, 0.0)
        attn = attn + a
        hs //= 2

    v = i_ref[...]
    vb = v.astype(BF16)
    near = qa * kk
    intra = jnp.sum(near, axis=1, keepdims=True) * v
    pad = jnp.zeros((HG_FINE, dk), F32)
    for buf, val in ((ksh_ref, kk), (bsh_ref, bcum), (vsh_ref, v)):
        buf[0:HG_FINE, :] = pad
        buf[HG_FINE:, :] = val
    for delta in range(1, HG_FINE):
        back = slice(HG_FINE - delta, HG_FINE - delta + rows)
        prod = qa * ksh_ref[back, :] * jnp.exp(jnp.minimum(bcum - bsh_ref[back, :], 0.0))
        prod = jnp.where((row & (HG_FINE - 1)) >= delta, prod, 0.0)
        intra = intra + jnp.sum(prod, axis=1, keepdims=True) * vsh_ref[back, :]
    intra = intra + jnp.dot(attn.astype(BF16), vb, preferred_element_type=F32)
    st = st_ref[...]
    cross = lax.dot_general((qa * jnp.exp(bcum)).astype(BF16), st.astype(BF16), _NT,
                            preferred_element_type=F32)
    blast = bcum[rows - 1:rows, :]
    kb = (kk * jnp.exp(blast - bcum)).astype(BF16)
    st_ref[...] = st * jnp.exp(blast) + lax.dot_general(vb, kb, _TN, preferred_element_type=F32)
    g = g_ref[...]
    o_ref[...] = (_rms(intra + cross) * ng_ref[...] * _silu(g)).astype(BF16)


def _hgrn(z, batch, seq, lb_raw, norm_g, layer):
    n = z.shape[0]
    gw = z.shape[1] // 4
    dk = gw // HG_HEADS
    blk = HG_BLOCK
    nblk = seq // blk
    depth = lb_raw.shape[0]

    def zspec(part):
        return pl.BlockSpec((blk, dk), lambda b, h, c: (b * nblk + c, part * HG_HEADS + h))

    return pl.pallas_call(
        functools.partial(_hgrn_kernel, layer=layer),
        grid=(batch, HG_HEADS, nblk),
        in_specs=[
            zspec(0), zspec(1), zspec(2), zspec(3),
            pl.BlockSpec((depth, dk), lambda b, h, c: (0, h)),
            pl.BlockSpec((1, dk), lambda b, h, c: (0, h)),
            pl.BlockSpec((blk, blk), lambda b, h, c: (0, 0)),
        ],
        out_specs=pl.BlockSpec((blk, dk), lambda b, h, c: (b * nblk + c, h)),
        out_shape=jax.ShapeDtypeStruct((n, gw), BF16),
        scratch_shapes=[pltpu.VMEM((dk, dk), F32)] + [pltpu.VMEM((blk + HG_FINE, dk), F32)] * 3,
        compiler_params=_params(("arbitrary", "arbitrary", "arbitrary")),
        name="hgrn2",
    )(z, z, z, z, lb_raw, norm_g.reshape(1, gw), jnp.tril(jnp.ones((blk, blk), BF16)))


def _dsa_prep_kernel(zd_ref, cqg_ref, ckvg_ref, wuq_ref, qng_ref, wqit_ref,
                     q_ref, qit_ref, kv_ref, kix_ref, wht_ref):
    zd = zd_ref[...]
    cq = (_rms(zd[:, :DSA_Q_RANK]) * cqg_ref[...]).astype(BF16)
    qf = jnp.dot(cq, wuq_ref[...], preferred_element_type=F32)
    for i in range(q_ref.shape[0]):
        rs = slice(i * Q_BLOCK, (i + 1) * Q_BLOCK)
        for h in range(DSA_HEADS):
            cs = slice(h * DSA_KV_RANK, (h + 1) * DSA_KV_RANK)
            q_ref[i, h] = (_rms(qf[rs, cs]) * qng_ref[...] * (DSA_KV_RANK ** -0.5 * LOG2E)).astype(BF16)
    qit = lax.dot_general(wqit_ref[...], cq, _NT, preferred_element_type=F32)
    qit = (qit * (IDX_DIM ** -0.5)).astype(BF16)
    for i in range(q_ref.shape[0]):
        for h in range(IDX_HEADS):
            c = (i * IDX_HEADS + h) * Q_BLOCK
            qit_ref[:, c:c + Q_BLOCK] = qit[h * LANES:(h + 1) * LANES, i * Q_BLOCK:(i + 1) * Q_BLOCK]
    c0 = DSA_Q_RANK
    c1 = c0 + DSA_KV_RANK
    kv_ref[...] = (_rms(zd[:, c0:c1]) * ckvg_ref[...]).astype(BF16)
    kix_ref[...] = zd[:, c1:c1 + LANES].astype(BF16)
    wht = jnp.transpose(zd[:, c1 + LANES:c1 + 2 * LANES] * (IDX_HEADS ** -0.5))
    wht_ref[...] = wht[:IDX_HEADS, :]


def _dsa_prep(zd, cq_g, ckv_g, w_uq, qn_g, w_qit, *, tm):
    n, wd = zd.shape
    dq = w_uq.shape[1]
    dqi = w_qit.shape[0]
    full = lambda i: (0, 0)
    rows = lambda i: (i, 0)
    cols = lambda i: (0, i)
    return pl.pallas_call(
        _dsa_prep_kernel,
        grid=(n // tm,),
        in_specs=[
            pl.BlockSpec((tm, wd), rows),
            pl.BlockSpec((1, DSA_Q_RANK), full),
            pl.BlockSpec((1, DSA_KV_RANK), full),
            pl.BlockSpec((DSA_Q_RANK, dq), full),
            pl.BlockSpec((1, DSA_KV_RANK), full),
            pl.BlockSpec((dqi, DSA_Q_RANK), full),
        ],
        out_specs=[
            pl.BlockSpec((tm // Q_BLOCK, DSA_HEADS, Q_BLOCK, DSA_KV_RANK), lambda i: (i, 0, 0, 0)),
            pl.BlockSpec((LANES, IDX_HEADS * tm), cols),
            pl.BlockSpec((tm, DSA_KV_RANK), rows),
            pl.BlockSpec((tm, LANES), rows),
            pl.BlockSpec((IDX_HEADS, tm), cols),
        ],
        out_shape=[
            jax.ShapeDtypeStruct((n // Q_BLOCK, DSA_HEADS, Q_BLOCK, DSA_KV_RANK), BF16),
            jax.ShapeDtypeStruct((LANES, IDX_HEADS * n), BF16),
            jax.ShapeDtypeStruct((n, DSA_KV_RANK), BF16),
            jax.ShapeDtypeStruct((n, LANES), BF16),
            jax.ShapeDtypeStruct((IDX_HEADS, n), F32),
        ],
        compiler_params=_params(("arbitrary",)),
        name="dsa_prep",
    )(zd, cq_g.reshape(1, -1), ckv_g.reshape(1, -1), w_uq, qn_g.reshape(1, -1), w_qit)


def _dsa_select_kernel(qit_ref, wht_ref, kix_ref, m_ref, key_ref, jc_ref, *, ksel, idx_bits):
    qb = pl.program_id(1)
    ntile = qb + 1
    ntile_all = m_ref.shape[2]
    rowi = lax.broadcasted_iota(I32, (LANES, Q_BLOCK), 0)
    coli = lax.broadcasted_iota(I32, (LANES, Q_BLOCK), 1)
    q_chunk = (qb * Q_BLOCK + coli) // CHUNK
    pairs = IDX_HEADS // 2
    w_pair = [jnp.concatenate([wht_ref[2 * p:2 * p + 1, :], wht_ref[2 * p + 1:2 * p + 2, :]], axis=1)
              for p in range(pairs)]

    def tile_rows(j):
        return pl.ds(pl.multiple_of(j * LANES, LANES), LANES)

    def admissible(j):
        return ((j * LANES + rowi) // CHUNK) <= q_chunk

    def score_tile(jj, carry):
        for t in range(2):
            j = 2 * jj + t
            kt = kix_ref[tile_rows(j), :]
            sc = None
            for p in range(pairs):
                s2 = jnp.dot(kt, qit_ref[:, 2 * p * Q_BLOCK:2 * (p + 1) * Q_BLOCK],
                             preferred_element_type=F32)
                c2 = w_pair[p] * jnp.maximum(s2, 0.0)
                c = c2[:, :Q_BLOCK] + c2[:, Q_BLOCK:]
                sc = c if sc is None else sc + c
            bits = pltpu.bitcast(sc, I32)
            key = bits ^ ((bits >> 31) & 0x7FFFFFFF)
            key_ref[tile_rows(j), :] = jnp.where(admissible(j), key, INT_MIN)
        return carry

    @pl.when(qb == 0)
    def _():
        key_ref[...] = jnp.full(key_ref.shape, INT_MIN, I32)

    lax.fori_loop(0, (ntile + 1) // 2, score_tile, 0)

    def count(pred_fn):
        def body(jj, acc):
            for t in range(2):
                j = 2 * jj + t
                acc = acc + pred_fn(j, key_ref[tile_rows(j), :]).astype(I32)
            return acc
        acc = lax.fori_loop(0, (ntile + 1) // 2, body, jnp.zeros((LANES, Q_BLOCK), I32))
        return jnp.sum(acc, axis=0, keepdims=True)

    def value_bit(i, c):
        t_u, cnt_t = c
        cand_u = t_u | jnp.left_shift(jnp.int32(1), 31 - i)
        cand = cand_u ^ INT_MIN
        cnt = count(lambda j, k: k >= cand)
        take = cnt >= ksel
        return jnp.where(take, cand_u, t_u), jnp.where(take, cnt, cnt_t)

    cnt0 = (q_chunk[0:1, :] + 1) * CHUNK
    t_u, cnt_t = lax.fori_loop(0, 32, value_bit, (jnp.zeros((1, Q_BLOCK), I32), cnt0))
    thr = t_u ^ INT_MIN
    tied = jnp.max(jnp.where(cnt_t > ksel, 1.0, 0.0))

    jc_ref[...] = jnp.full(jc_ref.shape, 2 ** 31 - 1, I32)

    @pl.when(tied > 0.0)
    def _():
        need = ksel - count(lambda j, k: k > thr)

        def index_bit(i, j_c):
            cand = j_c | jnp.left_shift(jnp.int32(1), idx_bits - 1 - i)
            cnt = count(lambda j, k: (k == thr) & ((j * LANES + rowi) < cand))
            return jnp.where(cnt < need, cand, j_c)

        j_c = lax.fori_loop(0, idx_bits, index_bit, jnp.zeros((1, Q_BLOCK), I32))
        jc_ref[...] = jnp.broadcast_to(j_c, jc_ref.shape)

    j_c = jc_ref[0:1, :]
    eye = (rowi == coli).astype(BF16)

    group = KV_TILE // LANES

    def write_group(g, carry):
        for t in range(group):
            j = g * group + t
            k = key_ref[tile_rows(j), :]
            sel = (k > thr) | ((k == thr) & ((j * LANES + rowi) <= j_c))
            sel = jnp.where(sel & admissible(j), 1.0, 0.0).astype(BF16)
            sel_t = lax.dot_general(eye, sel, _NT, preferred_element_type=F32)
            m_ref[0, 0, j] = ((sel_t - 1.0) * -NEG_BIG).astype(BF16)
        return carry

    ngroup = (ntile + group - 1) // group
    lax.fori_loop(0, ngroup, write_group, 0)

    def blank_tile(j, carry):
        m_ref[0, 0, j] = jnp.full((Q_BLOCK, LANES), NEG_BIG, BF16)
        return carry

    lax.fori_loop(ngroup * group, ntile_all, blank_tile, 0)


def _dsa_select(qit, wht, kix, batch, seq, ksel):
    nqb = seq // Q_BLOCK
    nkt = seq // LANES
    return pl.pallas_call(
        functools.partial(_dsa_select_kernel, ksel=ksel, idx_bits=int(math.log2(seq))),
        grid=(batch, nqb),
        in_specs=[
            pl.BlockSpec((LANES, IDX_HEADS * Q_BLOCK), lambda b, q: (0, b * nqb + q)),
            pl.BlockSpec((IDX_HEADS, Q_BLOCK), lambda b, q: (0, b * nqb + q)),
            pl.BlockSpec((seq, LANES), lambda b, q: (b, 0)),
        ],
        out_specs=pl.BlockSpec((1, 1, nkt, Q_BLOCK, LANES), lambda b, q: (b, q, 0, 0, 0)),
        out_shape=jax.ShapeDtypeStruct((batch, nqb, nkt, Q_BLOCK, LANES), BF16),
        scratch_shapes=[pltpu.VMEM((seq, Q_BLOCK), I32), pltpu.VMEM((8, Q_BLOCK), I32)],
        compiler_params=_params(("arbitrary", "arbitrary")),
        name="dsa_select",
    )(qit, wht, kix)


def _rel_bucket(rel):
    nb = REL_BUCKETS // 2
    max_exact = nb // 2
    ret = jnp.where(rel > 0, nb, 0)
    n = jnp.abs(rel)
    nf = jnp.maximum(n, 1).astype(F32)
    large = max_exact + (jnp.log(nf / max_exact) / math.log(REL_MAX_DIST / max_exact)
                         * (nb - max_exact)).astype(I32)
    large = jnp.minimum(large, nb - 1)
    return ret + jnp.where(n < max_exact, n, large)


NEAR_TILES = 3


def _dsa_attn_kernel(q_ref, kv_ref, mask_ref, rb_ref, wuv_ref, o_ref,
                     m_ref, l_ref, alpha_ref, acc_ref, corr_ref, s_ref, p_ref):
    b, qb = pl.program_id(0), pl.program_id(1)
    sub = KV_TILE // LANES
    far_bucket = REL_BUCKETS // 2 - 1
    half = DSA_HEADS * Q_BLOCK // 2

    def head_rows(h):
        return slice(h * Q_BLOCK, (h + 1) * Q_BLOCK)

    @pl.when((b == 0) & (qb == 0))
    def _():
        ti = lax.broadcasted_iota(I32, (Q_BLOCK, LANES), 0)
        si = lax.broadcasted_iota(I32, (Q_BLOCK, LANES), 1)
        for oi in range(NEAR_TILES):
            bucket = _rel_bucket((oi - (NEAR_TILES - 1)) * LANES + si - ti)
            for h in range(DSA_HEADS):
                tbl = jnp.zeros((Q_BLOCK, LANES), F32)
                for bk in range(REL_BUCKETS):
                    tbl = jnp.where(bucket == bk, rb_ref[bk, h], tbl)
                corr_ref[oi, head_rows(h), :] = (tbl - rb_ref[far_bucket, h]) * LOG2E

    m_ref[...] = jnp.full_like(m_ref, NEG_BIG)
    l_ref[...] = jnp.zeros_like(l_ref)
    acc_ref[...] = jnp.zeros_like(acc_ref)
    q_all = q_ref[0].reshape(DSA_HEADS * Q_BLOCK, DSA_KV_RANK)

    def key_step(kt, carry):
        kvt = kv_ref[pl.ds(pl.multiple_of(kt * KV_TILE, KV_TILE), KV_TILE), :]
        for part in range(2):
            rs = slice(part * half, (part + 1) * half)
            s_ref[rs, :] = lax.dot_general(q_all[rs], kvt, _NT, preferred_element_type=F32)
        for j in range(sub):
            d = kt * sub + j - qb

            @pl.when((d > -NEAR_TILES) & (d <= 0))
            def _(j=j, d=d):
                s_ref[:, j * LANES:(j + 1) * LANES] += corr_ref[d + NEAR_TILES - 1]

        madd = jnp.concatenate([mask_ref[0, 0, kt * sub + j].astype(F32) for j in range(sub)], axis=1)
        for h in range(DSA_HEADS):
            rs = head_rows(h)
            s = s_ref[rs, :] + madd
            m_old = m_ref[rs, :]
            m_new = jnp.maximum(m_old, jnp.max(s, axis=1, keepdims=True))
            alpha = jnp.exp2(m_old - m_new)
            p = jnp.exp2(s - m_new)
            l_ref[rs, :] = alpha * l_ref[rs, :] + jnp.sum(p, axis=1, keepdims=True)
            alpha_ref[rs, :] = alpha
            p_ref[rs, :] = p.astype(BF16)
            m_ref[rs, :] = m_new
        for part in range(2):
            rs = slice(part * half, (part + 1) * half)
            pv = jnp.dot(p_ref[rs, :], kvt, preferred_element_type=F32)
            acc_ref[rs, :] = alpha_ref[rs, :] * acc_ref[rs, :] + pv
        return carry

    lax.fori_loop(0, qb // sub + 1, key_step, 0)

    dv = wuv_ref.shape[2]
    for h in range(DSA_HEADS):
        rs = head_rows(h)
        o = (acc_ref[rs, :] / l_ref[rs, :]).astype(BF16)
        o_ref[:, h * dv:(h + 1) * dv] = jnp.dot(o, wuv_ref[h], preferred_element_type=F32).astype(BF16)


def _dsa_attn(q, kv, mask, rel_bias, w_uv, batch, seq):
    n = kv.shape[0]
    nqb = seq // Q_BLOCK
    dv = w_uv.shape[2]
    rows = DSA_HEADS * Q_BLOCK
    return pl.pallas_call(
        _dsa_attn_kernel,
        grid=(batch, nqb),
        in_specs=[
            pl.BlockSpec((1, DSA_HEADS, Q_BLOCK, DSA_KV_RANK), lambda b, qb: (b * nqb + qb, 0, 0, 0)),
            pl.BlockSpec((seq, DSA_KV_RANK), lambda b, qb: (b, 0)),
            pl.BlockSpec((1, 1) + mask.shape[2:], lambda b, qb: (b, qb, 0, 0, 0)),
            pl.BlockSpec(memory_space=pltpu.SMEM),
            pl.BlockSpec(w_uv.shape, lambda b, qb: (0, 0, 0)),
        ],
        out_specs=pl.BlockSpec((Q_BLOCK, DSA_HEADS * dv), lambda b, qb: (b * nqb + qb, 0)),
        out_shape=jax.ShapeDtypeStruct((n, DSA_HEADS * dv), BF16),
        scratch_shapes=[
            pltpu.VMEM((rows, 1), F32),
            pltpu.VMEM((rows, 1), F32),
            pltpu.VMEM((rows, 1), F32),
            pltpu.VMEM((rows, DSA_KV_RANK), F32),
            pltpu.VMEM((NEAR_TILES, rows, LANES), F32),
            pltpu.VMEM((rows, KV_TILE), F32),
            pltpu.VMEM((rows, KV_TILE), BF16),
        ],
        compiler_params=_params(("arbitrary", "arbitrary")),
        name="dsa_attn",
    )(q, kv, mask, rel_bias, w_uv)


def _pad_cols(w, width):
    return jnp.pad(w, ((0, 0), (0, width - w.shape[1])))


def kernel(x, ln_mix_g, ln_ffn_g, w_ffn_gate, w_ffn_up, w_ffn_down, rel_bias, ev_w_in, ev_w_out, sgu_ln_g, sgu_ln_b, sgu_w_s, sgu_b_s, od_w_in, od_w_out, hgrn_lb, hgrn_norm_g, dsa_cq_g, dsa_ckv_g, dsa_w_uq, dsa_qnorm_g, dsa_w_qidx, dsa_w_uv):
    batch, seq, d = x.shape
    n = batch * seq
    depth = ln_mix_g.shape[0]
    ksel = min(TOPK_MAX, seq // 4)
    tm = min(PROJ_ROWS, n)
    xf = x.reshape(n, d)
    for layer in range(depth):
        j = layer // 2
        if layer % 2 == 0:
            z = _norm_matmul(xf, ln_mix_g[layer], ev_w_in[j].astype(BF16), tm=tm, tn=PROJ_COLS)
            a1 = _retention(z, batch, seq)
            a2 = _sgu(z, sgu_ln_g[j], sgu_ln_b[j], sgu_w_s[j], sgu_b_s[j], rows=256)
            w_out = ev_w_out[j]
        else:
            w_in = od_w_in[j]
            gw = d // 2
            c = 4 * gw
            c_kidx = c + DSA_Q_RANK + DSA_KV_RANK
            w_main = w_in[:, :c].astype(BF16)
            w_dsa = jnp.concatenate([
                w_in[:, c:c_kidx],
                _pad_cols(w_in[:, c_kidx:c_kidx + IDX_DIM], LANES),
                _pad_cols(w_in[:, c_kidx + IDX_DIM:], LANES),
            ], axis=1).astype(BF16)
            z = _norm_matmul(xf, ln_mix_g[layer], w_main, tm=tm, tn=PROJ_COLS)
            zd = _norm_matmul(xf, ln_mix_g[layer], w_dsa, tm=tm, tn=w_dsa.shape[1])
            a1 = _hgrn(z, batch, seq, hgrn_lb, hgrn_norm_g[j], layer)
            w_qit = jnp.pad(dsa_w_qidx[j].T.reshape(IDX_HEADS, IDX_DIM, DSA_Q_RANK),
                            ((0, 0), (0, LANES - IDX_DIM), (0, 0))).reshape(IDX_HEADS * LANES, DSA_Q_RANK)
            q, qit, kv, kix, wht = _dsa_prep(zd, dsa_cq_g[j], dsa_ckv_g[j], dsa_w_uq[j].astype(BF16),
                                             dsa_qnorm_g[j], w_qit.astype(BF16), tm=256)
            mask = _dsa_select(qit, wht, kix, batch, seq, ksel)
            a2 = _dsa_attn(q, kv, mask, rel_bias, dsa_w_uv[j].astype(BF16), batch, seq)
            w_out = od_w_out[j]
        xf = _outproj(a1, a2, w_out.astype(BF16), xf, tm=tm, tn=OUT_COLS)
        xf = _ffn(xf, ln_ffn_g[layer], w_ffn_gate[layer].astype(BF16), w_ffn_up[layer].astype(BF16),
                  w_ffn_down[layer].astype(BF16), tm=min(FFN_ROWS, n), tf=FFN_COLS)
    return xf.reshape(batch, seq, d)
```

```python
import functools
import math

import jax
import jax.numpy as jnp
from jax import lax
from jax.experimental import pallas as pl
from jax.experimental.pallas import tpu as pltpu

F32 = jnp.float32
BF16 = jnp.bfloat16
I32 = jnp.int32

EPS = 1e-6
CHUNK = 64
LANES = 128
ROPE_BASE = 10000.0
RET_HEADS = 4
SGU_WINDOW = 128
SGU_GROUPS = 4
HG_HEADS = 8
DSA_HEADS = 8
DSA_Q_RANK = 384
DSA_KV_RANK = 256
IDX_HEADS = 16
IDX_DIM = 64
TOPK_MAX = 256
Q_BLOCK = 128
KV_TILE = 512
SM_ROWS = 64
LOG2E = math.log2(math.e)
REL_BUCKETS = 32
REL_MAX_DIST = 256
NEG_BIG = -1e30
INT_MIN = -(2 ** 31)

RET_BLOCK = 256
RET_HEADS_PER_STEP = 2
HG_BLOCK = 256
HG_FINE = 8
VMEM_LIMIT = 48 * 1024 * 1024
PROJ_ROWS = 1024
PROJ_COLS = 512
OUT_COLS = 1024
FFN_ROWS = 512
FFN_COLS = 512

_NT = (((1,), (1,)), ((), ()))
_TN = (((0,), (0,)), ((), ()))


def _params(semantics):
    return pltpu.CompilerParams(dimension_semantics=semantics, vmem_limit_bytes=VMEM_LIMIT)


def _silu(x):
    return x * jax.nn.sigmoid(x)


def _rms(x):
    return x * lax.rsqrt(jnp.mean(x * x, axis=-1, keepdims=True) + EPS)


def _norm_matmul_kernel(x_ref, g_ref, w_ref, o_ref, h_ref):
    @pl.when(pl.program_id(1) == 0)
    def _():
        h_ref[...] = (_rms(x_ref[...]) * g_ref[...]).astype(BF16)

    o_ref[...] = jnp.dot(h_ref[...], w_ref[...], preferred_element_type=F32)


def _norm_matmul(x, g, w, *, tm, tn):
    n, d = x.shape
    nout = w.shape[1]
    return pl.pallas_call(
        _norm_matmul_kernel,
        grid=(n // tm, nout // tn),
        in_specs=[
            pl.BlockSpec((tm, d), lambda i, j: (i, 0)),
            pl.BlockSpec((1, d), lambda i, j: (0, 0)),
            pl.BlockSpec((d, tn), lambda i, j: (0, j)),
        ],
        out_specs=pl.BlockSpec((tm, tn), lambda i, j: (i, j)),
        out_shape=jax.ShapeDtypeStruct((n, nout), F32),
        scratch_shapes=[pltpu.VMEM((tm, d), BF16)],
        compiler_params=_params(("arbitrary", "arbitrary")),
        name="norm_matmul",
    )(x, g.reshape(1, d), w)


def _outproj_kernel(a1_ref, a2_ref, w1_ref, w2_ref, r_ref, o_ref):
    acc = jnp.dot(a1_ref[...], w1_ref[...], preferred_element_type=F32)
    acc += jnp.dot(a2_ref[...], w2_ref[...], preferred_element_type=F32)
    o_ref[...] = r_ref[...] + acc


def _outproj(a1, a2, w, res, *, tm, tn):
    n, half = a1.shape
    d = w.shape[1]
    return pl.pallas_call(
        _outproj_kernel,
        grid=(n // tm, d // tn),
        in_specs=[
            pl.BlockSpec((tm, half), lambda i, j: (i, 0)),
            pl.BlockSpec((tm, half), lambda i, j: (i, 0)),
            pl.BlockSpec((half, tn), lambda i, j: (0, j)),
            pl.BlockSpec((half, tn), lambda i, j: (1, j)),
            pl.BlockSpec((tm, tn), lambda i, j: (i, j)),
        ],
        out_specs=pl.BlockSpec((tm, tn), lambda i, j: (i, j)),
        out_shape=jax.ShapeDtypeStruct((n, d), F32),
        compiler_params=_params(("arbitrary", "arbitrary")),
        name="outproj",
    )(a1, a2, w, w, res)


def _ffn_kernel(x_ref, g_ref, wg_ref, wu_ref, wd_ref, o_ref, h_ref):
    @pl.when(pl.program_id(1) == 0)
    def _():
        x = x_ref[...]
        h_ref[...] = (_rms(x) * g_ref[...]).astype(BF16)
        o_ref[...] = x

    h = h_ref[...]
    a = jnp.dot(h, wg_ref[...], preferred_element_type=F32)
    u = jnp.dot(h, wu_ref[...], preferred_element_type=F32)
    act = (_silu(a) * u).astype(BF16)
    o_ref[...] += jnp.dot(act, wd_ref[...], preferred_element_type=F32)


def _ffn(x, g, wg, wu, wd, *, tm, tf):
    n, d = x.shape
    dff = wg.shape[1]
    return pl.pallas_call(
        _ffn_kernel,
        grid=(n // tm, dff // tf),
        in_specs=[
            pl.BlockSpec((tm, d), lambda i, f: (i, 0)),
            pl.BlockSpec((1, d), lambda i, f: (0, 0)),
            pl.BlockSpec((d, tf), lambda i, f: (0, f)),
            pl.BlockSpec((d, tf), lambda i, f: (0, f)),
            pl.BlockSpec((tf, d), lambda i, f: (f, 0)),
        ],
        out_specs=pl.BlockSpec((tm, d), lambda i, f: (i, 0)),
        out_shape=jax.ShapeDtypeStruct((n, d), F32),
        scratch_shapes=[pltpu.VMEM((tm, d), BF16)],
        compiler_params=_params(("arbitrary", "arbitrary")),
        name="ffn",
    )(x, g.reshape(1, d), wg, wu, wd)


def _retention_kernel(q_ref, k_ref, v_ref, g_ref, cos_ref, sin_ref, d_ref, xi_ref, zeta_ref,
                      gl_ref, o_ref, state_ref):
    @pl.when(pl.program_id(2) == 0)
    def _():
        state_ref[...] = jnp.zeros_like(state_ref)

    cos = cos_ref[...]
    sin = sin_ref[...]
    half = cos.shape[1]
    dk = 2 * half

    def rot(t):
        t1, t2 = t[:, :half], t[:, half:]
        return jnp.concatenate([t1 * cos - t2 * sin, t1 * sin + t2 * cos], axis=1)

    for i in range(RET_HEADS_PER_STEP):
        cs = slice(i * dk, (i + 1) * dk)
        q = rot(q_ref[:, cs])
        k = rot(k_ref[:, cs]) * (dk ** -0.5)
        qb = q.astype(BF16)
        vb = v_ref[:, cs].astype(BF16)
        scores = lax.dot_general(qb, k.astype(BF16), _NT, preferred_element_type=F32) * d_ref[i]
        intra = jnp.dot(scores.astype(BF16), vb, preferred_element_type=F32)
        state = state_ref[i]
        cross = jnp.dot(qb, state.astype(BF16), preferred_element_type=F32) * xi_ref[i]
        kz = (k * zeta_ref[i]).astype(BF16)
        state_ref[i] = state * gl_ref[i] + lax.dot_general(kz, vb, _TN, preferred_element_type=F32)
        o_ref[:, cs] = (_rms(intra + cross) * _silu(g_ref[:, cs])).astype(BF16)


def _retention_tables(seq, dk):
    blk = RET_BLOCK
    pos = jnp.arange(seq, dtype=F32)
    inv = ROPE_BASE ** (-jnp.arange(0, dk, 2, dtype=F32) / dk)
    ang = pos[:, None] * inv[None, :]
    log_gamma = jnp.log(1.0 - 2.0 ** (-5.0 - jnp.arange(RET_HEADS, dtype=F32)))
    i = jnp.arange(blk)
    same = (i[:, None] // CHUNK) == (i[None, :] // CHUNK)
    earlier = (i[None, :] // CHUNK) < (i[:, None] // CHUNK)
    diff = (i[:, None] - i[None, :]).astype(F32)
    dist = jnp.where(same, jnp.abs(diff), diff)
    decay = jnp.where((same | earlier)[None], jnp.exp(log_gamma[:, None, None] * dist[None]), 0.0)
    p = jnp.arange(blk, dtype=F32)
    wide = (RET_HEADS, blk, dk)
    xi = jnp.broadcast_to(jnp.exp(log_gamma[:, None] * (p + 1.0))[:, :, None], wide)
    zeta = jnp.broadcast_to(jnp.exp(log_gamma[:, None] * (blk - 1.0 - p))[:, :, None], wide)
    g_blk = jnp.broadcast_to(jnp.exp(log_gamma * blk)[:, None, None], (RET_HEADS, 1, dk))
    return jnp.cos(ang), jnp.sin(ang), decay, xi, zeta, g_blk


def _retention(z, batch, seq):
    n = z.shape[0]
    gw = z.shape[1] // 6
    dk = gw // RET_HEADS
    blk = RET_BLOCK
    nblk = seq // blk
    cos, sin, decay, xi, zeta, g_blk = _retention_tables(seq, dk)

    hp = RET_HEADS_PER_STEP
    groups = RET_HEADS // hp

    def zspec(part):
        return pl.BlockSpec((blk, hp * dk), lambda b, h, c: (b * nblk + c, part * groups + h))

    def hspec(rows, cols):
        return pl.BlockSpec((hp, rows, cols), lambda b, h, c: (h, 0, 0))

    return pl.pallas_call(
        _retention_kernel,
        grid=(batch, groups, nblk),
        in_specs=[
            zspec(0), zspec(1), zspec(2), zspec(3),
            pl.BlockSpec((blk, dk // 2), lambda b, h, c: (c, 0)),
            pl.BlockSpec((blk, dk // 2), lambda b, h, c: (c, 0)),
            hspec(blk, blk), hspec(blk, dk), hspec(blk, dk), hspec(1, dk),
        ],
        out_specs=pl.BlockSpec((blk, hp * dk), lambda b, h, c: (b * nblk + c, h)),
        out_shape=jax.ShapeDtypeStruct((n, gw), BF16),
        scratch_shapes=[pltpu.VMEM((hp, dk, dk), F32)],
        compiler_params=_params(("arbitrary", "arbitrary", "arbitrary")),
        name="retention",
    )(z, z, z, z, cos, sin, decay, xi, zeta, g_blk)


def _gelu(x):
    return 0.5 * x * (1.0 + lax.erf(x * math.sqrt(0.5)))


def _sgu_kernel(u_ref, v_ref, lng_ref, lnb_ref, w_ref, b_ref, o_ref):
    rows, width = v_ref.shape
    dg = width // SGU_GROUPS
    v = _gelu(v_ref[...])
    mu = jnp.mean(v, axis=-1, keepdims=True)
    var = jnp.mean(jnp.square(v - mu), axis=-1, keepdims=True)
    vn = ((v - mu) * lax.rsqrt(var + EPS) * lng_ref[...] + lnb_ref[...]).astype(BF16)
    u = _gelu(u_ref[...])
    ri = lax.broadcasted_iota(I32, (SGU_WINDOW, SGU_WINDOW), 0) // CHUNK
    ci = lax.broadcasted_iota(I32, (SGU_WINDOW, SGU_WINDOW), 1) // CHUNK
    allowed = ci <= ri
    for g in range(SGU_GROUPS):
        wg = jnp.where(allowed, w_ref[g], 0.0).astype(BF16)
        bias = b_ref[g]
        for w in range(rows // SGU_WINDOW):
            rs = slice(w * SGU_WINDOW, (w + 1) * SGU_WINDOW)
            cs = slice(g * dg, (g + 1) * dg)
            mixed = jnp.dot(wg, vn[rs, cs], preferred_element_type=F32) + bias
            o_ref[rs, cs] = (u[rs, cs] * mixed).astype(BF16)


def _sgu(z, ln_g, ln_b, w_s, b_s, *, rows):
    n = z.shape[0]
    gw = z.shape[1] // 6
    return pl.pallas_call(
        _sgu_kernel,
        grid=(n // rows,),
        in_specs=[
            pl.BlockSpec((rows, gw), lambda i: (i, 4)),
            pl.BlockSpec((rows, gw), lambda i: (i, 5)),
            pl.BlockSpec((1, gw), lambda i: (0, 0)),
            pl.BlockSpec((1, gw), lambda i: (0, 0)),
            pl.BlockSpec((SGU_GROUPS, SGU_WINDOW, SGU_WINDOW), lambda i: (0, 0, 0)),
            pl.BlockSpec((SGU_GROUPS, SGU_WINDOW, 1), lambda i: (0, 0, 0)),
        ],
        out_specs=pl.BlockSpec((rows, gw), lambda i: (i, 0)),
        out_shape=jax.ShapeDtypeStruct((n, gw), BF16),
        compiler_params=_params(("arbitrary",)),
        name="sgu",
    )(z, z, ln_g.reshape(1, gw), ln_b.reshape(1, gw), w_s, b_s.reshape(SGU_GROUPS, SGU_WINDOW, 1))


def _hgrn_kernel(q_ref, f_ref, i_ref, g_ref, lb_ref, ng_ref, tri_ref, o_ref,
                 st_ref, ksh_ref, bsh_ref, vsh_ref, *, layer):
    rows, dk = q_ref.shape

    @pl.when(pl.program_id(2) == 0)
    def _():
        st_ref[...] = jnp.zeros_like(st_ref)

    lbp = lb_ref[...]
    e = jnp.exp(lbp - jnp.max(lbp, axis=0, keepdims=True))
    soft = e / jnp.sum(e, axis=0, keepdims=True)
    lb = jnp.sum(soft[1:layer + 1], axis=0, keepdims=True)

    f = lb + (1.0 - lb) * jax.nn.sigmoid(f_ref[...])
    lf = jnp.log(f)
    kk = 1.0 - f
    qa = _silu(q_ref[...])

    tri = tri_ref[...]
    bcum = None
    rest = lf
    for _ in range(3):
        term = rest.astype(BF16)
        part = jnp.dot(tri, term, preferred_element_type=F32)
        bcum = part if bcum is None else bcum + part
        rest = rest - term.astype(F32)

    row = lax.broadcasted_iota(I32, (rows, dk), 0)
    ti = lax.broadcasted_iota(I32, (rows, rows), 0)
    si = lax.broadcasted_iota(I32, (rows, rows), 1)
    attn = jnp.zeros((rows, rows), F32)
    hs = rows // 2
    while hs >= HG_FINE:
        bs = 2 * hs
        parts = [jnp.broadcast_to(bcum[b * bs + hs - 1:b * bs + hs, :], (bs, dk))
                 for b in range(rows // bs)]
        anchor = parts[0] if len(parts) == 1 else jnp.concatenate(parts, axis=0)
        upper = (row & (bs - 1)) >= hs
        fac = jnp.exp(-jnp.abs(bcum - anchor))
        qt = jnp.where(upper, qa * fac, 0.0)
        kt = jnp.where(upper, 0.0, kk * fac)
        a = lax.dot_general(qt.astype(BF16), kt.astype(BF16), _NT, preferred_element_type=F32)
        if bs < rows:
            a = jnp.where((ti & -bs) == (si & -bs), a, 0.0)
        attn = attn + a
        hs //= 2

    v = i_ref[...]
    vb = v.astype(BF16)
    near = qa * kk
    intra = jnp.sum(near, axis=1, keepdims=True) * v
    pad = jnp.zeros((HG_FINE, dk), F32)
    for buf, val in ((ksh_ref, kk), (bsh_ref, bcum), (vsh_ref, v)):
        buf[0:HG_FINE, :] = pad
        buf[HG_FINE:, :] = val
    for delta in range(1, HG_FINE):
        back = slice(HG_FINE - delta, HG_FINE - delta + rows)
        prod = qa * ksh_ref[back, :] * jnp.exp(jnp.minimum(bcum - bsh_ref[back, :], 0.0))
        prod = jnp.where((row & (HG_FINE - 1)) >= delta, prod, 0.0)
        intra = intra + jnp.sum(prod, axis=1, keepdims=True) * vsh_ref[back, :]
    intra = intra + jnp.dot(attn.astype(BF16), vb, preferred_element_type=F32)
    st = st_ref[...]
    cross = lax.dot_general((qa * jnp.exp(bcum)).astype(BF16), st.astype(BF16), _NT,
                            preferred_element_type=F32)
    blast = bcum[rows - 1:rows, :]
    kb = (kk * jnp.exp(blast - bcum)).astype(BF16)
    st_ref[...] = st * jnp.exp(blast) + lax.dot_general(vb, kb, _TN, preferred_element_type=F32)
    g = g_ref[...]
    o_ref[...] = (_rms(intra + cross) * ng_ref[...] * _silu(g)).astype(BF16)


def _hgrn(z, batch, seq, lb_raw, norm_g, layer):
    n = z.shape[0]
    gw = z.shape[1] // 4
    dk = gw // HG_HEADS
    blk = HG_BLOCK
    nblk = seq // blk
    depth = lb_raw.shape[0]

    def zspec(part):
        return pl.BlockSpec((blk, dk), lambda b, h, c: (b * nblk + c, part * HG_HEADS + h))

    return pl.pallas_call(
        functools.partial(_hgrn_kernel, layer=layer),
        grid=(batch, HG_HEADS, nblk),
        in_specs=[
            zspec(0), zspec(1), zspec(2), zspec(3),
            pl.BlockSpec((depth, dk), lambda b, h, c: (0, h)),
            pl.BlockSpec((1, dk), lambda b, h, c: (0, h)),
            pl.BlockSpec((blk, blk), lambda b, h, c: (0, 0)),
        ],
        out_specs=pl.BlockSpec((blk, dk), lambda b, h, c: (b * nblk + c, h)),
        out_shape=jax.ShapeDtypeStruct((n, gw), BF16),
        scratch_shapes=[pltpu.VMEM((dk, dk), F32)] + [pltpu.VMEM((blk + HG_FINE, dk), F32)] * 3,
        compiler_params=_params(("arbitrary", "arbitrary", "arbitrary")),
        name="hgrn2",
    )(z, z, z, z, lb_raw, norm_g.reshape(1, gw), jnp.tril(jnp.ones((blk, blk), BF16)))


def _dsa_prep_kernel(zd_ref, cqg_ref, ckvg_ref, wuq_ref, qng_ref, wqit_ref,
                     q_ref, qit_ref, kv_ref, kix_ref, wht_ref):
    zd = zd_ref[...]
    cq = (_rms(zd[:, :DSA_Q_RANK]) * cqg_ref[...]).astype(BF16)
    qf = jnp.dot(cq, wuq_ref[...], preferred_element_type=F32)
    for i in range(q_ref.shape[0]):
        rs = slice(i * Q_BLOCK, (i + 1) * Q_BLOCK)
        for h in range(DSA_HEADS):
            cs = slice(h * DSA_KV_RANK, (h + 1) * DSA_KV_RANK)
            q_ref[i, h] = (_rms(qf[rs, cs]) * qng_ref[...] * (DSA_KV_RANK ** -0.5 * LOG2E)).astype(BF16)
    qit = lax.dot_general(wqit_ref[...], cq, _NT, preferred_element_type=F32)
    qit = (qit * (IDX_DIM ** -0.5)).astype(BF16)
    for i in range(q_ref.shape[0]):
        for h in range(IDX_HEADS):
            c = (i * IDX_HEADS + h) * Q_BLOCK
            qit_ref[:, c:c + Q_BLOCK] = qit[h * LANES:(h + 1) * LANES, i * Q_BLOCK:(i + 1) * Q_BLOCK]
    c0 = DSA_Q_RANK
    c1 = c0 + DSA_KV_RANK
    kv_ref[...] = (_rms(zd[:, c0:c1]) * ckvg_ref[...]).astype(BF16)
    kix_ref[...] = zd[:, c1:c1 + LANES].astype(BF16)
    wht = jnp.transpose(zd[:, c1 + LANES:c1 + 2 * LANES] * (IDX_HEADS ** -0.5))
    wht_ref[...] = wht[:IDX_HEADS, :]


def _dsa_prep(zd, cq_g, ckv_g, w_uq, qn_g, w_qit, *, tm):
    n, wd = zd.shape
    dq = w_uq.shape[1]
    dqi = w_qit.shape[0]
    full = lambda i: (0, 0)
    rows = lambda i: (i, 0)
    cols = lambda i: (0, i)
    return pl.pallas_call(
        _dsa_prep_kernel,
        grid=(n // tm,),
        in_specs=[
            pl.BlockSpec((tm, wd), rows),
            pl.BlockSpec((1, DSA_Q_RANK), full),
            pl.BlockSpec((1, DSA_KV_RANK), full),
            pl.BlockSpec((DSA_Q_RANK, dq), full),
            pl.BlockSpec((1, DSA_KV_RANK), full),
            pl.BlockSpec((dqi, DSA_Q_RANK), full),
        ],
        out_specs=[
            pl.BlockSpec((tm // Q_BLOCK, DSA_HEADS, Q_BLOCK, DSA_KV_RANK), lambda i: (i, 0, 0, 0)),
            pl.BlockSpec((LANES, IDX_HEADS * tm), cols),
            pl.BlockSpec((tm, DSA_KV_RANK), rows),
            pl.BlockSpec((tm, LANES), rows),
            pl.BlockSpec((IDX_HEADS, tm), cols),
        ],
        out_shape=[
            jax.ShapeDtypeStruct((n // Q_BLOCK, DSA_HEADS, Q_BLOCK, DSA_KV_RANK), BF16),
            jax.ShapeDtypeStruct((LANES, IDX_HEADS * n), BF16),
            jax.ShapeDtypeStruct((n, DSA_KV_RANK), BF16),
            jax.ShapeDtypeStruct((n, LANES), BF16),
            jax.ShapeDtypeStruct((IDX_HEADS, n), F32),
        ],
        compiler_params=_params(("arbitrary",)),
        name="dsa_prep",
    )(zd, cq_g.reshape(1, -1), ckv_g.reshape(1, -1), w_uq, qn_g.reshape(1, -1), w_qit)


def _dsa_select_kernel(qit_ref, wht_ref, kix_ref, m_ref, key_ref, jc_ref, *, ksel, idx_bits):
    qb = pl.program_id(1)
    ntile = qb + 1
    ntile_all = m_ref.shape[2]
    rowi = lax.broadcasted_iota(I32, (LANES, Q_BLOCK), 0)
    coli = lax.broadcasted_iota(I32, (LANES, Q_BLOCK), 1)
    q_chunk = (qb * Q_BLOCK + coli) // CHUNK
    pairs = IDX_HEADS // 2
    w_pair = [jnp.concatenate([wht_ref[2 * p:2 * p + 1, :], wht_ref[2 * p + 1:2 * p + 2, :]], axis=1)
              for p in range(pairs)]

    def tile_rows(j):
        return pl.ds(pl.multiple_of(j * LANES, LANES), LANES)

    def admissible(j):
        return ((j * LANES + rowi) // CHUNK) <= q_chunk

    def score_tile(jj, carry):
        for t in range(2):
            j = 2 * jj + t
            kt = kix_ref[tile_rows(j), :]
            sc = None
            for p in range(pairs):
                s2 = jnp.dot(kt, qit_ref[:, 2 * p * Q_BLOCK:2 * (p + 1) * Q_BLOCK],
                             preferred_element_type=F32)
                c2 = w_pair[p] * jnp.maximum(s2, 0.0)
                c = c2[:, :Q_BLOCK] + c2[:, Q_BLOCK:]
                sc = c if sc is None else sc + c
            bits = pltpu.bitcast(sc, I32)
            key = bits ^ ((bits >> 31) & 0x7FFFFFFF)
            key_ref[tile_rows(j), :] = jnp.where(admissible(j), key, INT_MIN)
        return carry

    @pl.when(qb == 0)
    def _():
        key_ref[...] = jnp.full(key_ref.shape, INT_MIN, I32)

    lax.fori_loop(0, (ntile + 1) // 2, score_tile, 0)

    def count(pred_fn):
        def body(jj, acc):
            for t in range(2):
                j = 2 * jj + t
                acc = acc + pred_fn(j, key_ref[tile_rows(j), :]).astype(I32)
            return acc
        acc = lax.fori_loop(0, (ntile + 1) // 2, body, jnp.zeros((LANES, Q_BLOCK), I32))
        return jnp.sum(acc, axis=0, keepdims=True)

    def value_bit(i, c):
        t_u, cnt_t = c
        cand_u = t_u | jnp.left_shift(jnp.int32(1), 31 - i)
        cand = cand_u ^ INT_MIN
        cnt = count(lambda j, k: k >= cand)
        take = cnt >= ksel
        return jnp.where(take, cand_u, t_u), jnp.where(take, cnt, cnt_t)

    cnt0 = (q_chunk[0:1, :] + 1) * CHUNK
    t_u, cnt_t = lax.fori_loop(0, 32, value_bit, (jnp.zeros((1, Q_BLOCK), I32), cnt0))
    thr = t_u ^ INT_MIN
    tied = jnp.max(jnp.where(cnt_t > ksel, 1.0, 0.0))

    jc_ref[...] = jnp.full(jc_ref.shape, 2 ** 31 - 1, I32)

    @pl.when(tied > 0.0)
    def _():
        need = ksel - count(lambda j, k: k > thr)

        def index_bit(i, j_c):
            cand = j_c | jnp.left_shift(jnp.int32(1), idx_bits - 1 - i)
            cnt = count(lambda j, k: (k == thr) & ((j * LANES + rowi) < cand))
            return jnp.where(cnt < need, cand, j_c)

        j_c = lax.fori_loop(0, idx_bits, index_bit, jnp.zeros((1, Q_BLOCK), I32))
        jc_ref[...] = jnp.broadcast_to(j_c, jc_ref.shape)

    j_c = jc_ref[0:1, :]
    eye = (rowi == coli).astype(BF16)

    group = KV_TILE // LANES

    def write_group(g, carry):
        for t in range(group):
            j = g * group + t
            k = key_ref[tile_rows(j), :]
            sel = (k > thr) | ((k == thr) & ((j * LANES + rowi) <= j_c))
            sel = jnp.where(sel & admissible(j), 1.0, 0.0).astype(BF16)
            sel_t = lax.dot_general(eye, sel, _NT, preferred_element_type=F32)
            m_ref[0, 0, j] = ((sel_t - 1.0) * -NEG_BIG).astype(BF16)
        return carry

    ngroup = (ntile + group - 1) // group
    lax.fori_loop(0, ngroup, write_group, 0)

    def blank_tile(j, carry):
        m_ref[0, 0, j] = jnp.full((Q_BLOCK, LANES), NEG_BIG, BF16)
        return carry

    lax.fori_loop(ngroup * group, ntile_all, blank_tile, 0)


def _dsa_select(qit, wht, kix, batch, seq, ksel):
    nqb = seq // Q_BLOCK
    nkt = seq // LANES
    return pl.pallas_call(
        functools.partial(_dsa_select_kernel, ksel=ksel, idx_bits=int(math.log2(seq))),
        grid=(batch, nqb),
        in_specs=[
            pl.BlockSpec((LANES, IDX_HEADS * Q_BLOCK), lambda b, q: (0, b * nqb + q)),
            pl.BlockSpec((IDX_HEADS, Q_BLOCK), lambda b, q: (0, b * nqb + q)),
            pl.BlockSpec((seq, LANES), lambda b, q: (b, 0)),
        ],
        out_specs=pl.BlockSpec((1, 1, nkt, Q_BLOCK, LANES), lambda b, q: (b, q, 0, 0, 0)),
        out_shape=jax.ShapeDtypeStruct((batch, nqb, nkt, Q_BLOCK, LANES), BF16),
        scratch_shapes=[pltpu.VMEM((seq, Q_BLOCK), I32), pltpu.VMEM((8, Q_BLOCK), I32)],
        compiler_params=_params(("arbitrary", "arbitrary")),
        name="dsa_select",
    )(qit, wht, kix)


def _rel_bucket(rel):
    nb = REL_BUCKETS // 2
    max_exact = nb // 2
    ret = jnp.where(rel > 0, nb, 0)
    n = jnp.abs(rel)
    nf = jnp.maximum(n, 1).astype(F32)
    large = max_exact + (jnp.log(nf / max_exact) / math.log(REL_MAX_DIST / max_exact)
                         * (nb - max_exact)).astype(I32)
    large = jnp.minimum(large, nb - 1)
    return ret + jnp.where(n < max_exact, n, large)


NEAR_TILES = 3


def _dsa_attn_kernel(q_ref, kv_ref, mask_ref, rb_ref, wuv_ref, o_ref,
                     m_ref, l_ref, alpha_ref, acc_ref, corr_ref, s_ref, p_ref, madd_ref):
    b, qb = pl.program_id(0), pl.program_id(1)
    sub = KV_TILE // LANES
    far_bucket = REL_BUCKETS // 2 - 1
    half = DSA_HEADS * Q_BLOCK // 2

    def head_rows(h):
        return slice(h * Q_BLOCK, (h + 1) * Q_BLOCK)

    @pl.when((b == 0) & (qb == 0))
    def _():
        ti = lax.broadcasted_iota(I32, (Q_BLOCK, LANES), 0)
        si = lax.broadcasted_iota(I32, (Q_BLOCK, LANES), 1)
        for oi in range(NEAR_TILES):
            bucket = _rel_bucket((oi - (NEAR_TILES - 1)) * LANES + si - ti)
            for h in range(DSA_HEADS):
                tbl = jnp.zeros((Q_BLOCK, LANES), F32)
                for bk in range(REL_BUCKETS):
                    tbl = jnp.where(bucket == bk, rb_ref[bk, h], tbl)
                corr_ref[oi, head_rows(h), :] = (tbl - rb_ref[far_bucket, h]) * LOG2E

    m_ref[...] = jnp.full_like(m_ref, NEG_BIG)
    l_ref[...] = jnp.zeros_like(l_ref)
    acc_ref[...] = jnp.zeros_like(acc_ref)
    q_all = q_ref[0].reshape(DSA_HEADS * Q_BLOCK, DSA_KV_RANK)

    def key_step(kt, carry):
        kvt = kv_ref[pl.ds(pl.multiple_of(kt * KV_TILE, KV_TILE), KV_TILE), :]
        for part in range(2):
            rs = slice(part * half, (part + 1) * half)
            s_ref[rs, :] = lax.dot_general(q_all[rs], kvt, _NT, preferred_element_type=F32)
        for j in range(sub):
            d = kt * sub + j - qb

            @pl.when((d > -NEAR_TILES) & (d <= 0))
            def _(j=j, d=d):
                s_ref[:, j * LANES:(j + 1) * LANES] += corr_ref[d + NEAR_TILES - 1]

        for j in range(sub):
            madd_ref[:, j * LANES:(j + 1) * LANES] = mask_ref[0, 0, kt * sub + j].astype(F32)
        groups_per_head = Q_BLOCK // SM_ROWS
        for g in range(DSA_HEADS * groups_per_head):
            rs = slice(g * SM_ROWS, (g + 1) * SM_ROWS)
            qg = g % groups_per_head
            s = s_ref[rs, :] + madd_ref[qg * SM_ROWS:(qg + 1) * SM_ROWS, :]
            m_old = m_ref[rs, :]
            m_new = jnp.maximum(m_old, jnp.max(s, axis=1, keepdims=True))
            alpha = jnp.exp2(m_old - m_new)
            p = jnp.exp2(s - jnp.tile(m_new, (1, sub)))
            l_ref[rs, :] = alpha * l_ref[rs, :] + jnp.sum(p, axis=1, keepdims=True)
            alpha_ref[rs, :] = alpha
            p_ref[rs, :] = p.astype(BF16)
            m_ref[rs, :] = m_new
        for part in range(2):
            rs = slice(part * half, (part + 1) * half)
            pv = jnp.dot(p_ref[rs, :], kvt, preferred_element_type=F32)
            acc_ref[rs, :] = jnp.tile(alpha_ref[rs, :], (1, DSA_KV_RANK // LANES)) * acc_ref[rs, :] + pv
        return carry

    lax.fori_loop(0, qb // sub + 1, key_step, 0)

    dv = wuv_ref.shape[2]
    for h in range(DSA_HEADS):
        rs = head_rows(h)
        o = (acc_ref[rs, :] / jnp.tile(l_ref[rs, :], (1, DSA_KV_RANK // LANES))).astype(BF16)
        o_ref[:, h * dv:(h + 1) * dv] = jnp.dot(o, wuv_ref[h], preferred_element_type=F32).astype(BF16)


def _dsa_attn(q, kv, mask, rel_bias, w_uv, batch, seq):
    n = kv.shape[0]
    nqb = seq // Q_BLOCK
    dv = w_uv.shape[2]
    rows = DSA_HEADS * Q_BLOCK
    return pl.pallas_call(
        _dsa_attn_kernel,
        grid=(batch, nqb),
        in_specs=[
            pl.BlockSpec((1, DSA_HEADS, Q_BLOCK, DSA_KV_RANK), lambda b, qb: (b * nqb + qb, 0, 0, 0)),
            pl.BlockSpec((seq, DSA_KV_RANK), lambda b, qb: (b, 0)),
            pl.BlockSpec((1, 1) + mask.shape[2:], lambda b, qb: (b, qb, 0, 0, 0)),
            pl.BlockSpec(memory_space=pltpu.SMEM),
            pl.BlockSpec(w_uv.shape, lambda b, qb: (0, 0, 0)),
        ],
        out_specs=pl.BlockSpec((Q_BLOCK, DSA_HEADS * dv), lambda b, qb: (b * nqb + qb, 0)),
        out_shape=jax.ShapeDtypeStruct((n, DSA_HEADS * dv), BF16),
        scratch_shapes=[
            pltpu.VMEM((rows, LANES), F32),
            pltpu.VMEM((rows, LANES), F32),
            pltpu.VMEM((rows, LANES), F32),
            pltpu.VMEM((rows, DSA_KV_RANK), F32),
            pltpu.VMEM((NEAR_TILES, rows, LANES), F32),
            pltpu.VMEM((rows, KV_TILE), F32),
            pltpu.VMEM((rows, KV_TILE), BF16),
            pltpu.VMEM((Q_BLOCK, KV_TILE), F32),
        ],
        compiler_params=_params(("arbitrary", "arbitrary")),
        name="dsa_attn",
    )(q, kv, mask, rel_bias, w_uv)


def _pad_cols(w, width):
    return jnp.pad(w, ((0, 0), (0, width - w.shape[1])))


def kernel(x, ln_mix_g, ln_ffn_g, w_ffn_gate, w_ffn_up, w_ffn_down, rel_bias, ev_w_in, ev_w_out, sgu_ln_g, sgu_ln_b, sgu_w_s, sgu_b_s, od_w_in, od_w_out, hgrn_lb, hgrn_norm_g, dsa_cq_g, dsa_ckv_g, dsa_w_uq, dsa_qnorm_g, dsa_w_qidx, dsa_w_uv):
    batch, seq, d = x.shape
    n = batch * seq
    depth = ln_mix_g.shape[0]
    ksel = min(TOPK_MAX, seq // 4)
    tm = min(PROJ_ROWS, n)
    xf = x.reshape(n, d)
    for layer in range(depth):
        j = layer // 2
        if layer % 2 == 0:
            z = _norm_matmul(xf, ln_mix_g[layer], ev_w_in[j].astype(BF16), tm=tm, tn=PROJ_COLS)
            a1 = _retention(z, batch, seq)
            a2 = _sgu(z, sgu_ln_g[j], sgu_ln_b[j], sgu_w_s[j], sgu_b_s[j], rows=256)
            w_out = ev_w_out[j]
        else:
            w_in = od_w_in[j]
            gw = d // 2
            c = 4 * gw
            c_kidx = c + DSA_Q_RANK + DSA_KV_RANK
            w_main = w_in[:, :c].astype(BF16)
            w_dsa = jnp.concatenate([
                w_in[:, c:c_kidx],
                _pad_cols(w_in[:, c_kidx:c_kidx + IDX_DIM], LANES),
                _pad_cols(w_in[:, c_kidx + IDX_DIM:], LANES),
            ], axis=1).astype(BF16)
            z = _norm_matmul(xf, ln_mix_g[layer], w_main, tm=tm, tn=PROJ_COLS)
            zd = _norm_matmul(xf, ln_mix_g[layer], w_dsa, tm=tm, tn=w_dsa.shape[1])
            a1 = _hgrn(z, batch, seq, hgrn_lb, hgrn_norm_g[j], layer)
            w_qit = jnp.pad(dsa_w_qidx[j].T.reshape(IDX_HEADS, IDX_DIM, DSA_Q_RANK),
                            ((0, 0), (0, LANES - IDX_DIM), (0, 0))).reshape(IDX_HEADS * LANES, DSA_Q_RANK)
            q, qit, kv, kix, wht = _dsa_prep(zd, dsa_cq_g[j], dsa_ckv_g[j], dsa_w_uq[j].astype(BF16),
                                             dsa_qnorm_g[j], w_qit.astype(BF16), tm=256)
            mask = _dsa_select(qit, wht, kix, batch, seq, ksel)
            a2 = _dsa_attn(q, kv, mask, rel_bias, dsa_w_uv[j].astype(BF16), batch, seq)
            w_out = od_w_out[j]
        xf = _outproj(a1, a2, w_out.astype(BF16), xf, tm=tm, tn=OUT_COLS)
        xf = _ffn(xf, ln_ffn_g[layer], w_ffn_gate[layer].astype(BF16), w_ffn_up[layer].astype(BF16),
                  w_ffn_down[layer].astype(BF16), tm=min(FFN_ROWS, n), tf=FFN_COLS)
    return xf.reshape(batch, seq, d)
```

```python
import functools
import math

import jax
import jax.numpy as jnp
from jax import lax
from jax.experimental import pallas as pl
from jax.experimental.pallas import tpu as pltpu

F32 = jnp.float32
BF16 = jnp.bfloat16
I32 = jnp.int32

EPS = 1e-6
CHUNK = 64
LANES = 128
ROPE_BASE = 10000.0
RET_HEADS = 4
SGU_WINDOW = 128
SGU_GROUPS = 4
HG_HEADS = 8
DSA_HEADS = 8
DSA_Q_RANK = 384
DSA_KV_RANK = 256
IDX_HEADS = 16
IDX_DIM = 64
TOPK_MAX = 256
Q_BLOCK = 128
KV_TILE = 512
SM_ROWS = 64
LOG2E = math.log2(math.e)
REL_BUCKETS = 32
REL_MAX_DIST = 256
NEG_BIG = -1e30
INT_MIN = -(2 ** 31)

RET_BLOCK = 256
RET_HEADS_PER_STEP = 4
HG_BLOCK = 256
HG_FINE = 8
VMEM_LIMIT = 48 * 1024 * 1024
PROJ_ROWS = 1024
PROJ_COLS = 512
OUT_COLS = 1024
FFN_ROWS = 512
FFN_COLS = 512

_NT = (((1,), (1,)), ((), ()))
_TN = (((0,), (0,)), ((), ()))


def _params(semantics):
    return pltpu.CompilerParams(dimension_semantics=semantics, vmem_limit_bytes=VMEM_LIMIT)


def _silu(x):
    return x * jax.nn.sigmoid(x)


def _rms(x):
    return x * lax.rsqrt(jnp.mean(x * x, axis=-1, keepdims=True) + EPS)


def _norm_matmul_kernel(x_ref, g_ref, w_ref, o_ref, h_ref):
    @pl.when(pl.program_id(1) == 0)
    def _():
        h_ref[...] = (_rms(x_ref[...]) * g_ref[...]).astype(BF16)

    o_ref[...] = jnp.dot(h_ref[...], w_ref[...], preferred_element_type=F32)


def _norm_matmul(x, g, w, layer, nout, *, tm, tn):
    n, d = x.shape
    return pl.pallas_call(
        _norm_matmul_kernel,
        grid=(n // tm, nout // tn),
        in_specs=[
            pl.BlockSpec((tm, d), lambda i, j: (i, 0)),
            pl.BlockSpec((1, d), lambda i, j: (0, 0)),
            pl.BlockSpec((None, d, tn), lambda i, j: (layer, 0, j)),
        ],
        out_specs=pl.BlockSpec((tm, tn), lambda i, j: (i, j)),
        out_shape=jax.ShapeDtypeStruct((n, nout), F32),
        scratch_shapes=[pltpu.VMEM((tm, d), BF16)],
        compiler_params=_params(("arbitrary", "arbitrary")),
        name="norm_matmul",
    )(x, g.reshape(1, d), w)


def _outproj_kernel(a1_ref, a2_ref, w1_ref, w2_ref, r_ref, o_ref):
    acc = jnp.dot(a1_ref[...], w1_ref[...], preferred_element_type=F32)
    acc += jnp.dot(a2_ref[...], w2_ref[...], preferred_element_type=F32)
    o_ref[...] = r_ref[...] + acc


def _outproj(a1, a2, w, res, *, tm, tn):
    n, half = a1.shape
    d = w.shape[1]
    return pl.pallas_call(
        _outproj_kernel,
        grid=(n // tm, d // tn),
        in_specs=[
            pl.BlockSpec((tm, half), lambda i, j: (i, 0)),
            pl.BlockSpec((tm, half), lambda i, j: (i, 0)),
            pl.BlockSpec((half, tn), lambda i, j: (0, j)),
            pl.BlockSpec((half, tn), lambda i, j: (1, j)),
            pl.BlockSpec((tm, tn), lambda i, j: (i, j)),
        ],
        out_specs=pl.BlockSpec((tm, tn), lambda i, j: (i, j)),
        out_shape=jax.ShapeDtypeStruct((n, d), F32),
        compiler_params=_params(("arbitrary", "arbitrary")),
        name="outproj",
    )(a1, a2, w, w, res)


def _ffn_kernel(x_ref, g_ref, wg_ref, wu_ref, wd_ref, o_ref, h_ref):
    @pl.when(pl.program_id(1) == 0)
    def _():
        x = x_ref[...]
        h_ref[...] = (_rms(x) * g_ref[...]).astype(BF16)
        o_ref[...] = x

    h = h_ref[...]
    a = jnp.dot(h, wg_ref[...], preferred_element_type=F32)
    u = jnp.dot(h, wu_ref[...], preferred_element_type=F32)
    act = (_silu(a) * u).astype(BF16)
    o_ref[...] += jnp.dot(act, wd_ref[...], preferred_element_type=F32)


def _ffn(x, g, wg, wu, wd, layer, *, tm, tf):
    n, d = x.shape
    dff = wg.shape[2]
    return pl.pallas_call(
        _ffn_kernel,
        grid=(n // tm, dff // tf),
        in_specs=[
            pl.BlockSpec((tm, d), lambda i, f: (i, 0)),
            pl.BlockSpec((1, d), lambda i, f: (0, 0)),
            pl.BlockSpec((None, d, tf), lambda i, f: (layer, 0, f)),
            pl.BlockSpec((None, d, tf), lambda i, f: (layer, 0, f)),
            pl.BlockSpec((None, tf, d), lambda i, f: (layer, f, 0)),
        ],
        out_specs=pl.BlockSpec((tm, d), lambda i, f: (i, 0)),
        out_shape=jax.ShapeDtypeStruct((n, d), F32),
        scratch_shapes=[pltpu.VMEM((tm, d), BF16)],
        compiler_params=_params(("arbitrary", "arbitrary")),
        name="ffn",
    )(x, g.reshape(1, d), wg, wu, wd)


def _retention_kernel(q_ref, k_ref, v_ref, g_ref, cos_ref, sin_ref, d_ref, xi_ref, zeta_ref,
                      gl_ref, o_ref, state_ref):
    @pl.when(pl.program_id(2) == 0)
    def _():
        state_ref[...] = jnp.zeros_like(state_ref)

    cos = cos_ref[...]
    sin = sin_ref[...]
    half = cos.shape[1]
    dk = 2 * half

    def rot(t):
        t1, t2 = t[:, :half], t[:, half:]
        return jnp.concatenate([t1 * cos - t2 * sin, t1 * sin + t2 * cos], axis=1)

    for i in range(RET_HEADS_PER_STEP):
        cs = slice(i * dk, (i + 1) * dk)
        q = rot(q_ref[:, cs])
        k = rot(k_ref[:, cs]) * (dk ** -0.5)
        qb = q.astype(BF16)
        vb = v_ref[:, cs].astype(BF16)
        scores = lax.dot_general(qb, k.astype(BF16), _NT, preferred_element_type=F32) * d_ref[i]
        intra = jnp.dot(scores.astype(BF16), vb, preferred_element_type=F32)
        state = state_ref[i]
        cross = jnp.dot(qb, state.astype(BF16), preferred_element_type=F32) * xi_ref[i]
        kz = (k * zeta_ref[i]).astype(BF16)
        state_ref[i] = state * gl_ref[i] + lax.dot_general(kz, vb, _TN, preferred_element_type=F32)
        o_ref[:, cs] = (_rms(intra + cross) * _silu(g_ref[:, cs])).astype(BF16)


def _retention_tables(seq, dk):
    blk = RET_BLOCK
    pos = jnp.arange(seq, dtype=F32)
    inv = ROPE_BASE ** (-jnp.arange(0, dk, 2, dtype=F32) / dk)
    ang = pos[:, None] * inv[None, :]
    log_gamma = jnp.log(1.0 - 2.0 ** (-5.0 - jnp.arange(RET_HEADS, dtype=F32)))
    i = jnp.arange(blk)
    same = (i[:, None] // CHUNK) == (i[None, :] // CHUNK)
    earlier = (i[None, :] // CHUNK) < (i[:, None] // CHUNK)
    diff = (i[:, None] - i[None, :]).astype(F32)
    dist = jnp.where(same, jnp.abs(diff), diff)
    decay = jnp.where((same | earlier)[None], jnp.exp(log_gamma[:, None, None] * dist[None]), 0.0)
    p = jnp.arange(blk, dtype=F32)
    wide = (RET_HEADS, blk, dk)
    xi = jnp.broadcast_to(jnp.exp(log_gamma[:, None] * (p + 1.0))[:, :, None], wide)
    zeta = jnp.broadcast_to(jnp.exp(log_gamma[:, None] * (blk - 1.0 - p))[:, :, None], wide)
    g_blk = jnp.broadcast_to(jnp.exp(log_gamma * blk)[:, None, None], (RET_HEADS, 1, dk))
    return jnp.cos(ang), jnp.sin(ang), decay, xi, zeta, g_blk


def _retention(z, batch, seq):
    n = z.shape[0]
    gw = z.shape[1] // 6
    dk = gw // RET_HEADS
    blk = RET_BLOCK
    nblk = seq // blk
    cos, sin, decay, xi, zeta, g_blk = _retention_tables(seq, dk)

    hp = RET_HEADS_PER_STEP
    groups = RET_HEADS // hp

    def zspec(part):
        return pl.BlockSpec((blk, hp * dk), lambda b, h, c: (b * nblk + c, part * groups + h))

    def hspec(rows, cols):
        return pl.BlockSpec((hp, rows, cols), lambda b, h, c: (h, 0, 0))

    return pl.pallas_call(
        _retention_kernel,
        grid=(batch, groups, nblk),
        in_specs=[
            zspec(0), zspec(1), zspec(2), zspec(3),
            pl.BlockSpec((blk, dk // 2), lambda b, h, c: (c, 0)),
            pl.BlockSpec((blk, dk // 2), lambda b, h, c: (c, 0)),
            hspec(blk, blk), hspec(blk, dk), hspec(blk, dk), hspec(1, dk),
        ],
        out_specs=pl.BlockSpec((blk, hp * dk), lambda b, h, c: (b * nblk + c, h)),
        out_shape=jax.ShapeDtypeStruct((n, gw), BF16),
        scratch_shapes=[pltpu.VMEM((hp, dk, dk), F32)],
        compiler_params=_params(("arbitrary", "arbitrary", "arbitrary")),
        name="retention",
    )(z, z, z, z, cos, sin, decay, xi, zeta, g_blk)


def _gelu(x):
    return 0.5 * x * (1.0 + lax.erf(x * math.sqrt(0.5)))


def _sgu_kernel(u_ref, v_ref, lng_ref, lnb_ref, w_ref, b_ref, o_ref):
    rows, width = v_ref.shape
    dg = width // SGU_GROUPS
    v = _gelu(v_ref[...])
    mu = jnp.mean(v, axis=-1, keepdims=True)
    var = jnp.mean(jnp.square(v - mu), axis=-1, keepdims=True)
    vn = ((v - mu) * lax.rsqrt(var + EPS) * lng_ref[...] + lnb_ref[...]).astype(BF16)
    u = _gelu(u_ref[...])
    ri = lax.broadcasted_iota(I32, (SGU_WINDOW, SGU_WINDOW), 0) // CHUNK
    ci = lax.broadcasted_iota(I32, (SGU_WINDOW, SGU_WINDOW), 1) // CHUNK
    allowed = ci <= ri
    for g in range(SGU_GROUPS):
        wg = jnp.where(allowed, w_ref[g], 0.0).astype(BF16)
        bias = b_ref[g]
        for w in range(rows // SGU_WINDOW):
            rs = slice(w * SGU_WINDOW, (w + 1) * SGU_WINDOW)
            cs = slice(g * dg, (g + 1) * dg)
            mixed = jnp.dot(wg, vn[rs, cs], preferred_element_type=F32) + bias
            o_ref[rs, cs] = (u[rs, cs] * mixed).astype(BF16)


def _sgu(z, ln_g, ln_b, w_s, b_s, *, rows):
    n = z.shape[0]
    gw = z.shape[1] // 6
    return pl.pallas_call(
        _sgu_kernel,
        grid=(n // rows,),
        in_specs=[
            pl.BlockSpec((rows, gw), lambda i: (i, 4)),
            pl.BlockSpec((rows, gw), lambda i: (i, 5)),
            pl.BlockSpec((1, gw), lambda i: (0, 0)),
            pl.BlockSpec((1, gw), lambda i: (0, 0)),
            pl.BlockSpec((SGU_GROUPS, SGU_WINDOW, SGU_WINDOW), lambda i: (0, 0, 0)),
            pl.BlockSpec((SGU_GROUPS, SGU_WINDOW, 1), lambda i: (0, 0, 0)),
        ],
        out_specs=pl.BlockSpec((rows, gw), lambda i: (i, 0)),
        out_shape=jax.ShapeDtypeStruct((n, gw), BF16),
        compiler_params=_params(("arbitrary",)),
        name="sgu",
    )(z, z, ln_g.reshape(1, gw), ln_b.reshape(1, gw), w_s, b_s.reshape(SGU_GROUPS, SGU_WINDOW, 1))


def _hgrn_kernel(q_ref, f_ref, i_ref, g_ref, lb_ref, ng_ref, tri_ref, o_ref,
                 st_ref, ksh_ref, bsh_ref, vsh_ref, *, layer):
    rows, dk = q_ref.shape

    @pl.when(pl.program_id(2) == 0)
    def _():
        st_ref[...] = jnp.zeros_like(st_ref)

    lbp = lb_ref[...]
    e = jnp.exp(lbp - jnp.max(lbp, axis=0, keepdims=True))
    soft = e / jnp.sum(e, axis=0, keepdims=True)
    lb = jnp.sum(soft[1:layer + 1], axis=0, keepdims=True)

    f = lb + (1.0 - lb) * jax.nn.sigmoid(f_ref[...])
    lf = jnp.log(f)
    kk = 1.0 - f
    qa = _silu(q_ref[...])

    tri = tri_ref[...]
    bcum = None
    rest = lf
    for _ in range(3):
        term = rest.astype(BF16)
        part = jnp.dot(tri, term, preferred_element_type=F32)
        bcum = part if bcum is None else bcum + part
        rest = rest - term.astype(F32)

    row = lax.broadcasted_iota(I32, (rows, dk), 0)
    ti = lax.broadcasted_iota(I32, (rows, rows), 0)
    si = lax.broadcasted_iota(I32, (rows, rows), 1)
    attn = jnp.zeros((rows, rows), F32)
    hs = rows // 2
    while hs >= HG_FINE:
        bs = 2 * hs
        parts = [jnp.broadcast_to(bcum[b * bs + hs - 1:b * bs + hs, :], (bs, dk))
                 for b in range(rows // bs)]
        anchor = parts[0] if len(parts) == 1 else jnp.concatenate(parts, axis=0)
        upper = (row & (bs - 1)) >= hs
        fac = jnp.exp(-jnp.abs(bcum - anchor))
        qt = jnp.where(upper, qa * fac, 0.0)
        kt = jnp.where(upper, 0.0, kk * fac)
        a = lax.dot_general(qt.astype(BF16), kt.astype(BF16), _NT, preferred_element_type=F32)
        if bs < rows:
            a = jnp.where((ti & -bs) == (si & -bs), a, 0.0)
        attn = attn + a
        hs //= 2

    v = i_ref[...]
    vb = v.astype(BF16)
    near = qa * kk
    intra = jnp.sum(near, axis=1, keepdims=True) * v
    pad = jnp.zeros((HG_FINE, dk), F32)
    for buf, val in ((ksh_ref, kk), (bsh_ref, bcum), (vsh_ref, v)):
        buf[0:HG_FINE, :] = pad
        buf[HG_FINE:, :] = val
    for delta in range(1, HG_FINE):
        back = slice(HG_FINE - delta, HG_FINE - delta + rows)
        prod = qa * ksh_ref[back, :] * jnp.exp(jnp.minimum(bcum - bsh_ref[back, :], 0.0))
        prod = jnp.where((row & (HG_FINE - 1)) >= delta, prod, 0.0)
        intra = intra + jnp.sum(prod, axis=1, keepdims=True) * vsh_ref[back, :]
    intra = intra + jnp.dot(attn.astype(BF16), vb, preferred_element_type=F32)
    st = st_ref[...]
    cross = lax.dot_general((qa * jnp.exp(bcum)).astype(BF16), st.astype(BF16), _NT,
                            preferred_element_type=F32)
    blast = bcum[rows - 1:rows, :]
    kb = (kk * jnp.exp(blast - bcum)).astype(BF16)
    st_ref[...] = st * jnp.exp(blast) + lax.dot_general(vb, kb, _TN, preferred_element_type=F32)
    g = g_ref[...]
    o_ref[...] = (_rms(intra + cross) * ng_ref[...] * _silu(g)).astype(BF16)


def _hgrn(z, batch, seq, lb_raw, norm_g, layer):
    n = z.shape[0]
    gw = z.shape[1] // 4
    dk = gw // HG_HEADS
    blk = HG_BLOCK
    nblk = seq // blk
    depth = lb_raw.shape[0]

    def zspec(part):
        return pl.BlockSpec((blk, dk), lambda b, h, c: (b * nblk + c, part * HG_HEADS + h))

    return pl.pallas_call(
        functools.partial(_hgrn_kernel, layer=layer),
        grid=(batch, HG_HEADS, nblk),
        in_specs=[
            zspec(0), zspec(1), zspec(2), zspec(3),
            pl.BlockSpec((depth, dk), lambda b, h, c: (0, h)),
            pl.BlockSpec((1, dk), lambda b, h, c: (0, h)),
            pl.BlockSpec((blk, blk), lambda b, h, c: (0, 0)),
        ],
        out_specs=pl.BlockSpec((blk, dk), lambda b, h, c: (b * nblk + c, h)),
        out_shape=jax.ShapeDtypeStruct((n, gw), BF16),
        scratch_shapes=[pltpu.VMEM((dk, dk), F32)] + [pltpu.VMEM((blk + HG_FINE, dk), F32)] * 3,
        compiler_params=_params(("arbitrary", "arbitrary", "arbitrary")),
        name="hgrn2",
    )(z, z, z, z, lb_raw, norm_g.reshape(1, gw), jnp.tril(jnp.ones((blk, blk), BF16)))


def _dsa_prep_kernel(zd_ref, cqg_ref, ckvg_ref, wuq_ref, qng_ref, wqit_ref,
                     q_ref, qit_ref, kv_ref, kix_ref, wht_ref):
    zd = zd_ref[...]
    cq = (_rms(zd[:, :DSA_Q_RANK]) * cqg_ref[...]).astype(BF16)
    qf = jnp.dot(cq, wuq_ref[...], preferred_element_type=F32)
    for i in range(q_ref.shape[0]):
        rs = slice(i * Q_BLOCK, (i + 1) * Q_BLOCK)
        for h in range(DSA_HEADS):
            cs = slice(h * DSA_KV_RANK, (h + 1) * DSA_KV_RANK)
            q_ref[i, h] = (_rms(qf[rs, cs]) * qng_ref[...] * (DSA_KV_RANK ** -0.5 * LOG2E)).astype(BF16)
    qit = lax.dot_general(wqit_ref[...], cq, _NT, preferred_element_type=F32)
    qit = (qit * (IDX_DIM ** -0.5)).astype(BF16)
    for i in range(q_ref.shape[0]):
        for h in range(IDX_HEADS):
            c = (i * IDX_HEADS + h) * Q_BLOCK
            qit_ref[:, c:c + Q_BLOCK] = qit[h * LANES:(h + 1) * LANES, i * Q_BLOCK:(i + 1) * Q_BLOCK]
    c0 = DSA_Q_RANK
    c1 = c0 + DSA_KV_RANK
    kv_ref[...] = (_rms(zd[:, c0:c1]) * ckvg_ref[...]).astype(BF16)
    kix_ref[...] = zd[:, c1:c1 + LANES].astype(BF16)
    wht = jnp.transpose(zd[:, c1 + LANES:c1 + 2 * LANES] * (IDX_HEADS ** -0.5))
    wht_ref[...] = wht[:IDX_HEADS, :]


def _dsa_prep(zd, cq_g, ckv_g, w_uq, qn_g, w_qit, *, tm):
    n, wd = zd.shape
    dq = w_uq.shape[1]
    dqi = w_qit.shape[0]
    full = lambda i: (0, 0)
    rows = lambda i: (i, 0)
    cols = lambda i: (0, i)
    return pl.pallas_call(
        _dsa_prep_kernel,
        grid=(n // tm,),
        in_specs=[
            pl.BlockSpec((tm, wd), rows),
            pl.BlockSpec((1, DSA_Q_RANK), full),
            pl.BlockSpec((1, DSA_KV_RANK), full),
            pl.BlockSpec((DSA_Q_RANK, dq), full),
            pl.BlockSpec((1, DSA_KV_RANK), full),
            pl.BlockSpec((dqi, DSA_Q_RANK), full),
        ],
        out_specs=[
            pl.BlockSpec((tm // Q_BLOCK, DSA_HEADS, Q_BLOCK, DSA_KV_RANK), lambda i: (i, 0, 0, 0)),
            pl.BlockSpec((LANES, IDX_HEADS * tm), cols),
            pl.BlockSpec((tm, DSA_KV_RANK), rows),
            pl.BlockSpec((tm, LANES), rows),
            pl.BlockSpec((IDX_HEADS, tm), cols),
        ],
        out_shape=[
            jax.ShapeDtypeStruct((n // Q_BLOCK, DSA_HEADS, Q_BLOCK, DSA_KV_RANK), BF16),
            jax.ShapeDtypeStruct((LANES, IDX_HEADS * n), BF16),
            jax.ShapeDtypeStruct((n, DSA_KV_RANK), BF16),
            jax.ShapeDtypeStruct((n, LANES), BF16),
            jax.ShapeDtypeStruct((IDX_HEADS, n), F32),
        ],
        compiler_params=_params(("arbitrary",)),
        name="dsa_prep",
    )(zd, cq_g.reshape(1, -1), ckv_g.reshape(1, -1), w_uq, qn_g.reshape(1, -1), w_qit)


def _dsa_select_kernel(qit_ref, wht_ref, kix_ref, m_ref, key_ref, jc_ref, *, ksel, idx_bits):
    qb = pl.program_id(1)
    ntile = qb + 1
    ntile_all = m_ref.shape[2]
    rowi = lax.broadcasted_iota(I32, (LANES, Q_BLOCK), 0)
    coli = lax.broadcasted_iota(I32, (LANES, Q_BLOCK), 1)
    q_chunk = (qb * Q_BLOCK + coli) // CHUNK
    pairs = IDX_HEADS // 2
    w_pair = [jnp.concatenate([wht_ref[2 * p:2 * p + 1, :], wht_ref[2 * p + 1:2 * p + 2, :]], axis=1)
              for p in range(pairs)]

    def tile_rows(j):
        return pl.ds(pl.multiple_of(j * LANES, LANES), LANES)

    def admissible(j):
        return ((j * LANES + rowi) // CHUNK) <= q_chunk

    def score_tile(jj, carry):
        for t in range(2):
            j = 2 * jj + t
            kt = kix_ref[tile_rows(j), :]
            sc = None
            for p in range(pairs):
                s2 = jnp.dot(kt, qit_ref[:, 2 * p * Q_BLOCK:2 * (p + 1) * Q_BLOCK],
                             preferred_element_type=F32)
                c2 = w_pair[p] * jnp.maximum(s2, 0.0)
                c = c2[:, :Q_BLOCK] + c2[:, Q_BLOCK:]
                sc = c if sc is None else sc + c
            bits = pltpu.bitcast(sc, I32)
            key = bits ^ ((bits >> 31) & 0x7FFFFFFF)
            key_ref[tile_rows(j), :] = jnp.where(admissible(j), key, INT_MIN)
        return carry

    @pl.when(qb == 0)
    def _():
        key_ref[...] = jnp.full(key_ref.shape, INT_MIN, I32)

    lax.fori_loop(0, (ntile + 1) // 2, score_tile, 0)

    def count(pred_fn):
        def body(jj, acc):
            for t in range(2):
                j = 2 * jj + t
                acc = acc + pred_fn(j, key_ref[tile_rows(j), :]).astype(I32)
            return acc
        acc = lax.fori_loop(0, (ntile + 1) // 2, body, jnp.zeros((LANES, Q_BLOCK), I32))
        return jnp.sum(acc, axis=0, keepdims=True)

    def value_bit(i, c):
        t_u, cnt_t = c
        cand_u = t_u | jnp.left_shift(jnp.int32(1), 31 - i)
        cand = cand_u ^ INT_MIN
        cnt = count(lambda j, k: k >= cand)
        take = cnt >= ksel
        return jnp.where(take, cand_u, t_u), jnp.where(take, cnt, cnt_t)

    cnt0 = (q_chunk[0:1, :] + 1) * CHUNK
    t_u, cnt_t = lax.fori_loop(0, 32, value_bit, (jnp.zeros((1, Q_BLOCK), I32), cnt0))
    thr = t_u ^ INT_MIN
    tied = jnp.max(jnp.where(cnt_t > ksel, 1.0, 0.0))

    jc_ref[...] = jnp.full(jc_ref.shape, 2 ** 31 - 1, I32)

    @pl.when(tied > 0.0)
    def _():
        need = ksel - count(lambda j, k: k > thr)

        def index_bit(i, j_c):
            cand = j_c | jnp.left_shift(jnp.int32(1), idx_bits - 1 - i)
            cnt = count(lambda j, k: (k == thr) & ((j * LANES + rowi) < cand))
            return jnp.where(cnt < need, cand, j_c)

        j_c = lax.fori_loop(0, idx_bits, index_bit, jnp.zeros((1, Q_BLOCK), I32))
        jc_ref[...] = jnp.broadcast_to(j_c, jc_ref.shape)

    j_c = jc_ref[0:1, :]
    eye = (rowi == coli).astype(BF16)

    group = KV_TILE // LANES

    def write_group(g, carry):
        for t in range(group):
            j = g * group + t
            k = key_ref[tile_rows(j), :]
            sel = (k > thr) | ((k == thr) & ((j * LANES + rowi) <= j_c))
            sel = jnp.where(sel & admissible(j), 1.0, 0.0).astype(BF16)
            sel_t = lax.dot_general(eye, sel, _NT, preferred_element_type=F32)
            m_ref[0, 0, j] = ((sel_t - 1.0) * -NEG_BIG).astype(BF16)
        return carry

    ngroup = (ntile + group - 1) // group
    lax.fori_loop(0, ngroup, write_group, 0)

    def blank_tile(j, carry):
        m_ref[0, 0, j] = jnp.full((Q_BLOCK, LANES), NEG_BIG, BF16)
        return carry

    lax.fori_loop(ngroup * group, ntile_all, blank_tile, 0)


def _dsa_select(qit, wht, kix, batch, seq, ksel):
    nqb = seq // Q_BLOCK
    nkt = seq // LANES
    return pl.pallas_call(
        functools.partial(_dsa_select_kernel, ksel=ksel, idx_bits=int(math.log2(seq))),
        grid=(batch, nqb),
        in_specs=[
            pl.BlockSpec((LANES, IDX_HEADS * Q_BLOCK), lambda b, q: (0, b * nqb + q)),
            pl.BlockSpec((IDX_HEADS, Q_BLOCK), lambda b, q: (0, b * nqb + q)),
            pl.BlockSpec((seq, LANES), lambda b, q: (b, 0)),
        ],
        out_specs=pl.BlockSpec((1, 1, nkt, Q_BLOCK, LANES), lambda b, q: (b, q, 0, 0, 0)),
        out_shape=jax.ShapeDtypeStruct((batch, nqb, nkt, Q_BLOCK, LANES), BF16),
        scratch_shapes=[pltpu.VMEM((seq, Q_BLOCK), I32), pltpu.VMEM((8, Q_BLOCK), I32)],
        compiler_params=_params(("arbitrary", "arbitrary")),
        name="dsa_select",
    )(qit, wht, kix)


def _rel_bucket(rel):
    nb = REL_BUCKETS // 2
    max_exact = nb // 2
    ret = jnp.where(rel > 0, nb, 0)
    n = jnp.abs(rel)
    nf = jnp.maximum(n, 1).astype(F32)
    large = max_exact + (jnp.log(nf / max_exact) / math.log(REL_MAX_DIST / max_exact)
                         * (nb - max_exact)).astype(I32)
    large = jnp.minimum(large, nb - 1)
    return ret + jnp.where(n < max_exact, n, large)


NEAR_TILES = 3


def _dsa_attn_kernel(q_ref, kv_ref, mask_ref, rb_ref, wuv_ref, o_ref,
                     m_ref, l_ref, alpha_ref, acc_ref, corr_ref, s_ref, p_ref, madd_ref):
    b, qb = pl.program_id(0), pl.program_id(1)
    sub = KV_TILE // LANES
    far_bucket = REL_BUCKETS // 2 - 1
    half = DSA_HEADS * Q_BLOCK // 2

    def head_rows(h):
        return slice(h * Q_BLOCK, (h + 1) * Q_BLOCK)

    @pl.when((b == 0) & (qb == 0))
    def _():
        ti = lax.broadcasted_iota(I32, (Q_BLOCK, LANES), 0)
        si = lax.broadcasted_iota(I32, (Q_BLOCK, LANES), 1)
        for oi in range(NEAR_TILES):
            bucket = _rel_bucket((oi - (NEAR_TILES - 1)) * LANES + si - ti)
            for h in range(DSA_HEADS):
                tbl = jnp.zeros((Q_BLOCK, LANES), F32)
                for bk in range(REL_BUCKETS):
                    tbl = jnp.where(bucket == bk, rb_ref[bk, h], tbl)
                corr_ref[oi, head_rows(h), :] = (tbl - rb_ref[far_bucket, h]) * LOG2E

    m_ref[...] = jnp.full_like(m_ref, NEG_BIG)
    l_ref[...] = jnp.zeros_like(l_ref)
    acc_ref[...] = jnp.zeros_like(acc_ref)
    q_all = q_ref[0].reshape(DSA_HEADS * Q_BLOCK, DSA_KV_RANK)

    def key_step(kt, carry):
        kvt = kv_ref[pl.ds(pl.multiple_of(kt * KV_TILE, KV_TILE), KV_TILE), :]
        for part in range(2):
            rs = slice(part * half, (part + 1) * half)
            s_ref[rs, :] = lax.dot_general(q_all[rs], kvt, _NT, preferred_element_type=F32)
        for j in range(sub):
            d = kt * sub + j - qb

            @pl.when((d > -NEAR_TILES) & (d <= 0))
            def _(j=j, d=d):
                s_ref[:, j * LANES:(j + 1) * LANES] += corr_ref[d + NEAR_TILES - 1]

        for j in range(sub):
            madd_ref[:, j * LANES:(j + 1) * LANES] = mask_ref[0, 0, kt * sub + j].astype(F32)
        groups_per_head = Q_BLOCK // SM_ROWS
        for g in range(DSA_HEADS * groups_per_head):
            rs = slice(g * SM_ROWS, (g + 1) * SM_ROWS)
            qg = g % groups_per_head
            s = s_ref[rs, :] + madd_ref[qg * SM_ROWS:(qg + 1) * SM_ROWS, :]
            m_old = m_ref[rs, :]
            m_new = jnp.maximum(m_old, jnp.max(s, axis=1, keepdims=True))
            alpha = jnp.exp2(m_old - m_new)
            p = jnp.exp2(s - jnp.tile(m_new, (1, sub)))
            l_ref[rs, :] = alpha * l_ref[rs, :] + jnp.sum(p, axis=1, keepdims=True)
            alpha_ref[rs, :] = alpha
            p_ref[rs, :] = p.astype(BF16)
            m_ref[rs, :] = m_new
        for part in range(2):
            rs = slice(part * half, (part + 1) * half)
            pv = jnp.dot(p_ref[rs, :], kvt, preferred_element_type=F32)
            acc_ref[rs, :] = jnp.tile(alpha_ref[rs, :], (1, DSA_KV_RANK // LANES)) * acc_ref[rs, :] + pv
        return carry

    lax.fori_loop(0, qb // sub + 1, key_step, 0)

    dv = wuv_ref.shape[2]
    for h in range(DSA_HEADS):
        rs = head_rows(h)
        o = (acc_ref[rs, :] / jnp.tile(l_ref[rs, :], (1, DSA_KV_RANK // LANES))).astype(BF16)
        o_ref[:, h * dv:(h + 1) * dv] = jnp.dot(o, wuv_ref[h], preferred_element_type=F32).astype(BF16)


def _dsa_attn(q, kv, mask, rel_bias, w_uv, batch, seq):
    n = kv.shape[0]
    nqb = seq // Q_BLOCK
    dv = w_uv.shape[2]
    rows = DSA_HEADS * Q_BLOCK
    return pl.pallas_call(
        _dsa_attn_kernel,
        grid=(batch, nqb),
        in_specs=[
            pl.BlockSpec((1, DSA_HEADS, Q_BLOCK, DSA_KV_RANK), lambda b, qb: (b * nqb + qb, 0, 0, 0)),
            pl.BlockSpec((seq, DSA_KV_RANK), lambda b, qb: (b, 0)),
            pl.BlockSpec((1, 1) + mask.shape[2:], lambda b, qb: (b, qb, 0, 0, 0)),
            pl.BlockSpec(memory_space=pltpu.SMEM),
            pl.BlockSpec(w_uv.shape, lambda b, qb: (0, 0, 0)),
        ],
        out_specs=pl.BlockSpec((Q_BLOCK, DSA_HEADS * dv), lambda b, qb: (b * nqb + qb, 0)),
        out_shape=jax.ShapeDtypeStruct((n, DSA_HEADS * dv), BF16),
        scratch_shapes=[
            pltpu.VMEM((rows, LANES), F32),
            pltpu.VMEM((rows, LANES), F32),
            pltpu.VMEM((rows, LANES), F32),
            pltpu.VMEM((rows, DSA_KV_RANK), F32),
            pltpu.VMEM((NEAR_TILES, rows, LANES), F32),
            pltpu.VMEM((rows, KV_TILE), F32),
            pltpu.VMEM((rows, KV_TILE), BF16),
            pltpu.VMEM((Q_BLOCK, KV_TILE), F32),
        ],
        compiler_params=_params(("arbitrary", "arbitrary")),
        name="dsa_attn",
    )(q, kv, mask, rel_bias, w_uv)


def _pad_cols(w, width):
    return jnp.pad(w, ((0, 0), (0, width - w.shape[1])))


def kernel(x, ln_mix_g, ln_ffn_g, w_ffn_gate, w_ffn_up, w_ffn_down, rel_bias, ev_w_in, ev_w_out, sgu_ln_g, sgu_ln_b, sgu_w_s, sgu_b_s, od_w_in, od_w_out, hgrn_lb, hgrn_norm_g, dsa_cq_g, dsa_ckv_g, dsa_w_uq, dsa_qnorm_g, dsa_w_qidx, dsa_w_uv):
    batch, seq, d = x.shape
    n = batch * seq
    depth = ln_mix_g.shape[0]
    ksel = min(TOPK_MAX, seq // 4)
    tm = min(PROJ_ROWS, n)
    xf = x.reshape(n, d)
    wg_all, wu_all, wd_all = (w.astype(BF16) for w in (w_ffn_gate, w_ffn_up, w_ffn_down))
    ev_in_all, od_in_all = ev_w_in.astype(BF16), od_w_in.astype(BF16)
    for layer in range(depth):
        j = layer // 2
        if layer % 2 == 0:
            z = _norm_matmul(xf, ln_mix_g[layer], ev_in_all, j, ev_in_all.shape[2], tm=tm, tn=PROJ_COLS)
            a1 = _retention(z, batch, seq)
            a2 = _sgu(z, sgu_ln_g[j], sgu_ln_b[j], sgu_w_s[j], sgu_b_s[j], rows=256)
            w_out = ev_w_out[j]
        else:
            w_in = od_w_in[j]
            gw = d // 2
            c = 4 * gw
            c_kidx = c + DSA_Q_RANK + DSA_KV_RANK
            w_dsa = jnp.concatenate([
                w_in[:, c:c_kidx],
                _pad_cols(w_in[:, c_kidx:c_kidx + IDX_DIM], LANES),
                _pad_cols(w_in[:, c_kidx + IDX_DIM:], LANES),
            ], axis=1).astype(BF16)[None]
            z = _norm_matmul(xf, ln_mix_g[layer], od_in_all, j, c, tm=tm, tn=PROJ_COLS)
            zd = _norm_matmul(xf, ln_mix_g[layer], w_dsa, 0, w_dsa.shape[2], tm=tm, tn=w_dsa.shape[2])
            a1 = _hgrn(z, batch, seq, hgrn_lb, hgrn_norm_g[j], layer)
            w_qit = jnp.pad(dsa_w_qidx[j].T.reshape(IDX_HEADS, IDX_DIM, DSA_Q_RANK),
                            ((0, 0), (0, LANES - IDX_DIM), (0, 0))).reshape(IDX_HEADS * LANES, DSA_Q_RANK)
            q, qit, kv, kix, wht = _dsa_prep(zd, dsa_cq_g[j], dsa_ckv_g[j], dsa_w_uq[j].astype(BF16),
                                             dsa_qnorm_g[j], w_qit.astype(BF16), tm=256)
            mask = _dsa_select(qit, wht, kix, batch, seq, ksel)
            a2 = _dsa_attn(q, kv, mask, rel_bias, dsa_w_uv[j].astype(BF16), batch, seq)
            w_out = od_w_out[j]
        xf = _outproj(a1, a2, w_out.astype(BF16), xf, tm=tm, tn=OUT_COLS)
        xf = _ffn(xf, ln_ffn_g[layer], wg_all, wu_all, wd_all, layer, tm=min(FFN_ROWS, n), tf=FFN_COLS)
    return xf.reshape(batch, seq, d)
```

```python
import functools
import math

import jax
import jax.numpy as jnp
from jax import lax
from jax.experimental import pallas as pl
from jax.experimental.pallas import tpu as pltpu

F32 = jnp.float32
BF16 = jnp.bfloat16
I32 = jnp.int32
I16 = jnp.int16

EPS = 1e-6
CHUNK = 64
LANES = 128
ROPE_BASE = 10000.0
RET_HEADS = 4
SGU_WINDOW = 128
SGU_GROUPS = 4
HG_HEADS = 8
DSA_HEADS = 8
DSA_Q_RANK = 384
DSA_KV_RANK = 256
IDX_HEADS = 16
IDX_DIM = 64
TOPK_MAX = 256
Q_BLOCK = 128
KV_TILE = 512
SM_ROWS = 64
LOG2E = math.log2(math.e)
REL_BUCKETS = 32
REL_MAX_DIST = 256
NEG_BIG = -1e30
INT_MIN = -(2 ** 31)
HALF_OFFSET = 2 ** 15
SCAN_TILES = 4

RET_BLOCK = 256
RET_HEADS_PER_STEP = 4
HG_BLOCK = 256
HG_FINE = 8
VMEM_LIMIT = 48 * 1024 * 1024
PROJ_ROWS = 1024
PROJ_COLS = 512
OUT_COLS = 1024
FFN_ROWS = 512
FFN_COLS = 512

_NT = (((1,), (1,)), ((), ()))
_TN = (((0,), (0,)), ((), ()))


def _params(semantics):
    return pltpu.CompilerParams(dimension_semantics=semantics, vmem_limit_bytes=VMEM_LIMIT)


def _silu(x):
    return x * jax.nn.sigmoid(x)


def _rms(x):
    return x * lax.rsqrt(jnp.mean(x * x, axis=-1, keepdims=True) + EPS)


def _norm_matmul_kernel(x_ref, g_ref, w_ref, o_ref, h_ref):
    @pl.when(pl.program_id(1) == 0)
    def _():
        h_ref[...] = (_rms(x_ref[...]) * g_ref[...]).astype(BF16)

    o_ref[...] = jnp.dot(h_ref[...], w_ref[...], preferred_element_type=F32)


def _norm_matmul(x, g, w, layer, nout, *, tm, tn):
    n, d = x.shape
    return pl.pallas_call(
        _norm_matmul_kernel,
        grid=(n // tm, nout // tn),
        in_specs=[
            pl.BlockSpec((tm, d), lambda i, j: (i, 0)),
            pl.BlockSpec((1, d), lambda i, j: (0, 0)),
            pl.BlockSpec((None, d, tn), lambda i, j: (layer, 0, j)),
        ],
        out_specs=pl.BlockSpec((tm, tn), lambda i, j: (i, j)),
        out_shape=jax.ShapeDtypeStruct((n, nout), F32),
        scratch_shapes=[pltpu.VMEM((tm, d), BF16)],
        compiler_params=_params(("arbitrary", "arbitrary")),
        name="norm_matmul",
    )(x, g.reshape(1, d), w)


def _outproj_kernel(a1_ref, a2_ref, w1_ref, w2_ref, r_ref, o_ref):
    acc = jnp.dot(a1_ref[...], w1_ref[...], preferred_element_type=F32)
    acc += jnp.dot(a2_ref[...], w2_ref[...], preferred_element_type=F32)
    o_ref[...] = r_ref[...] + acc


def _outproj(a1, a2, w, res, *, tm, tn):
    n, half = a1.shape
    d = w.shape[1]
    return pl.pallas_call(
        _outproj_kernel,
        grid=(n // tm, d // tn),
        in_specs=[
            pl.BlockSpec((tm, half), lambda i, j: (i, 0)),
            pl.BlockSpec((tm, half), lambda i, j: (i, 0)),
            pl.BlockSpec((half, tn), lambda i, j: (0, j)),
            pl.BlockSpec((half, tn), lambda i, j: (1, j)),
            pl.BlockSpec((tm, tn), lambda i, j: (i, j)),
        ],
        out_specs=pl.BlockSpec((tm, tn), lambda i, j: (i, j)),
        out_shape=jax.ShapeDtypeStruct((n, d), F32),
        compiler_params=_params(("arbitrary", "arbitrary")),
        name="outproj",
    )(a1, a2, w, w, res)


def _ffn_kernel(x_ref, g_ref, wg_ref, wu_ref, wd_ref, o_ref, h_ref):
    @pl.when(pl.program_id(1) == 0)
    def _():
        x = x_ref[...]
        h_ref[...] = (_rms(x) * g_ref[...]).astype(BF16)
        o_ref[...] = x

    h = h_ref[...]
    a = jnp.dot(h, wg_ref[...], preferred_element_type=F32)
    u = jnp.dot(h, wu_ref[...], preferred_element_type=F32)
    act = (_silu(a) * u).astype(BF16)
    o_ref[...] += jnp.dot(act, wd_ref[...], preferred_element_type=F32)


def _ffn(x, g, wg, wu, wd, layer, *, tm, tf):
    n, d = x.shape
    dff = wg.shape[2]
    return pl.pallas_call(
        _ffn_kernel,
        grid=(n // tm, dff // tf),
        in_specs=[
            pl.BlockSpec((tm, d), lambda i, f: (i, 0)),
            pl.BlockSpec((1, d), lambda i, f: (0, 0)),
            pl.BlockSpec((None, d, tf), lambda i, f: (layer, 0, f)),
            pl.BlockSpec((None, d, tf), lambda i, f: (layer, 0, f)),
            pl.BlockSpec((None, tf, d), lambda i, f: (layer, f, 0)),
        ],
        out_specs=pl.BlockSpec((tm, d), lambda i, f: (i, 0)),
        out_shape=jax.ShapeDtypeStruct((n, d), F32),
        scratch_shapes=[pltpu.VMEM((tm, d), BF16)],
        compiler_params=_params(("arbitrary", "arbitrary")),
        name="ffn",
    )(x, g.reshape(1, d), wg, wu, wd)


def _retention_kernel(q_ref, k_ref, v_ref, g_ref, cos_ref, sin_ref, d_ref, xi_ref, zeta_ref,
                      gl_ref, o_ref, state_ref):
    @pl.when(pl.program_id(2) == 0)
    def _():
        state_ref[...] = jnp.zeros_like(state_ref)

    cos = cos_ref[...]
    sin = sin_ref[...]
    half = cos.shape[1]
    dk = 2 * half

    def rot(t):
        t1, t2 = t[:, :half], t[:, half:]
        return jnp.concatenate([t1 * cos - t2 * sin, t1 * sin + t2 * cos], axis=1)

    for i in range(RET_HEADS_PER_STEP):
        cs = slice(i * dk, (i + 1) * dk)
        q = rot(q_ref[:, cs])
        k = rot(k_ref[:, cs]) * (dk ** -0.5)
        qb = q.astype(BF16)
        vb = v_ref[:, cs].astype(BF16)
        scores = lax.dot_general(qb, k.astype(BF16), _NT, preferred_element_type=F32) * d_ref[i]
        intra = jnp.dot(scores.astype(BF16), vb, preferred_element_type=F32)
        state = state_ref[i]
        cross = jnp.dot(qb, state.astype(BF16), preferred_element_type=F32) * xi_ref[i]
        kz = (k * zeta_ref[i]).astype(BF16)
        state_ref[i] = state * gl_ref[i] + lax.dot_general(kz, vb, _TN, preferred_element_type=F32)
        o_ref[:, cs] = (_rms(intra + cross) * _silu(g_ref[:, cs])).astype(BF16)


def _retention_tables(seq, dk):
    blk = RET_BLOCK
    pos = jnp.arange(seq, dtype=F32)
    inv = ROPE_BASE ** (-jnp.arange(0, dk, 2, dtype=F32) / dk)
    ang = pos[:, None] * inv[None, :]
    log_gamma = jnp.log(1.0 - 2.0 ** (-5.0 - jnp.arange(RET_HEADS, dtype=F32)))
    i = jnp.arange(blk)
    same = (i[:, None] // CHUNK) == (i[None, :] // CHUNK)
    earlier = (i[None, :] // CHUNK) < (i[:, None] // CHUNK)
    diff = (i[:, None] - i[None, :]).astype(F32)
    dist = jnp.where(same, jnp.abs(diff), diff)
    decay = jnp.where((same | earlier)[None], jnp.exp(log_gamma[:, None, None] * dist[None]), 0.0)
    p = jnp.arange(blk, dtype=F32)
    wide = (RET_HEADS, blk, dk)
    xi = jnp.broadcast_to(jnp.exp(log_gamma[:, None] * (p + 1.0))[:, :, None], wide)
    zeta = jnp.broadcast_to(jnp.exp(log_gamma[:, None] * (blk - 1.0 - p))[:, :, None], wide)
    g_blk = jnp.broadcast_to(jnp.exp(log_gamma * blk)[:, None, None], (RET_HEADS, 1, dk))
    return jnp.cos(ang), jnp.sin(ang), decay, xi, zeta, g_blk


def _retention(z, batch, seq):
    n = z.shape[0]
    gw = z.shape[1] // 6
    dk = gw // RET_HEADS
    blk = RET_BLOCK
    nblk = seq // blk
    cos, sin, decay, xi, zeta, g_blk = _retention_tables(seq, dk)

    hp = RET_HEADS_PER_STEP
    groups = RET_HEADS // hp

    def zspec(part):
        return pl.BlockSpec((blk, hp * dk), lambda b, h, c: (b * nblk + c, part * groups + h))

    def hspec(rows, cols):
        return pl.BlockSpec((hp, rows, cols), lambda b, h, c: (h, 0, 0))

    return pl.pallas_call(
        _retention_kernel,
        grid=(batch, groups, nblk),
        in_specs=[
            zspec(0), zspec(1), zspec(2), zspec(3),
            pl.BlockSpec((blk, dk // 2), lambda b, h, c: (c, 0)),
            pl.BlockSpec((blk, dk // 2), lambda b, h, c: (c, 0)),
            hspec(blk, blk), hspec(blk, dk), hspec(blk, dk), hspec(1, dk),
        ],
        out_specs=pl.BlockSpec((blk, hp * dk), lambda b, h, c: (b * nblk + c, h)),
        out_shape=jax.ShapeDtypeStruct((n, gw), BF16),
        scratch_shapes=[pltpu.VMEM((hp, dk, dk), F32)],
        compiler_params=_params(("arbitrary", "arbitrary", "arbitrary")),
        name="retention",
    )(z, z, z, z, cos, sin, decay, xi, zeta, g_blk)


def _gelu(x):
    return 0.5 * x * (1.0 + lax.erf(x * math.sqrt(0.5)))


def _sgu_kernel(u_ref, v_ref, lng_ref, lnb_ref, w_ref, b_ref, o_ref):
    rows, width = v_ref.shape
    dg = width // SGU_GROUPS
    v = _gelu(v_ref[...])
    mu = jnp.mean(v, axis=-1, keepdims=True)
    var = jnp.mean(jnp.square(v - mu), axis=-1, keepdims=True)
    vn = ((v - mu) * lax.rsqrt(var + EPS) * lng_ref[...] + lnb_ref[...]).astype(BF16)
    u = _gelu(u_ref[...])
    ri = lax.broadcasted_iota(I32, (SGU_WINDOW, SGU_WINDOW), 0) // CHUNK
    ci = lax.broadcasted_iota(I32, (SGU_WINDOW, SGU_WINDOW), 1) // CHUNK
    allowed = ci <= ri
    for g in range(SGU_GROUPS):
        wg = jnp.where(allowed, w_ref[g], 0.0).astype(BF16)
        bias = b_ref[g]
        for w in range(rows // SGU_WINDOW):
            rs = slice(w * SGU_WINDOW, (w + 1) * SGU_WINDOW)
            cs = slice(g * dg, (g + 1) * dg)
            mixed = jnp.dot(wg, vn[rs, cs], preferred_element_type=F32) + bias
            o_ref[rs, cs] = (u[rs, cs] * mixed).astype(BF16)


def _sgu(z, ln_g, ln_b, w_s, b_s, *, rows):
    n = z.shape[0]
    gw = z.shape[1] // 6
    return pl.pallas_call(
        _sgu_kernel,
        grid=(n // rows,),
        in_specs=[
            pl.BlockSpec((rows, gw), lambda i: (i, 4)),
            pl.BlockSpec((rows, gw), lambda i: (i, 5)),
            pl.BlockSpec((1, gw), lambda i: (0, 0)),
            pl.BlockSpec((1, gw), lambda i: (0, 0)),
            pl.BlockSpec((SGU_GROUPS, SGU_WINDOW, SGU_WINDOW), lambda i: (0, 0, 0)),
            pl.BlockSpec((SGU_GROUPS, SGU_WINDOW, 1), lambda i: (0, 0, 0)),
        ],
        out_specs=pl.BlockSpec((rows, gw), lambda i: (i, 0)),
        out_shape=jax.ShapeDtypeStruct((n, gw), BF16),
        compiler_params=_params(("arbitrary",)),
        name="sgu",
    )(z, z, ln_g.reshape(1, gw), ln_b.reshape(1, gw), w_s, b_s.reshape(SGU_GROUPS, SGU_WINDOW, 1))


def _hgrn_kernel(q_ref, f_ref, i_ref, g_ref, lb_ref, ng_ref, tri_ref, o_ref,
                 st_ref, ksh_ref, bsh_ref, vsh_ref, *, layer):
    rows, dk = q_ref.shape

    @pl.when(pl.program_id(2) == 0)
    def _():
        st_ref[...] = jnp.zeros_like(st_ref)

    lbp = lb_ref[...]
    e = jnp.exp(lbp - jnp.max(lbp, axis=0, keepdims=True))
    soft = e / jnp.sum(e, axis=0, keepdims=True)
    lb = jnp.sum(soft[1:layer + 1], axis=0, keepdims=True)

    f = lb + (1.0 - lb) * jax.nn.sigmoid(f_ref[...])
    lf = jnp.log(f)
    kk = 1.0 - f
    qa = _silu(q_ref[...])

    tri = tri_ref[...]
    bcum = None
    rest = lf
    for _ in range(3):
        term = rest.astype(BF16)
        part = jnp.dot(tri, term, preferred_element_type=F32)
        bcum = part if bcum is None else bcum + part
        rest = rest - term.astype(F32)

    row = lax.broadcasted_iota(I32, (rows, dk), 0)
    ti = lax.broadcasted_iota(I32, (rows, rows), 0)
    si = lax.broadcasted_iota(I32, (rows, rows), 1)
    attn = jnp.zeros((rows, rows), F32)
    hs = rows // 2
    while hs >= HG_FINE:
        bs = 2 * hs
        parts = [jnp.broadcast_to(bcum[b * bs + hs - 1:b * bs + hs, :], (bs, dk))
                 for b in range(rows // bs)]
        anchor = parts[0] if len(parts) == 1 else jnp.concatenate(parts, axis=0)
        upper = (row & (bs - 1)) >= hs
        fac = jnp.exp(-jnp.abs(bcum - anchor))
        qt = jnp.where(upper, qa * fac, 0.0)
        kt = jnp.where(upper, 0.0, kk * fac)
        a = lax.dot_general(qt.astype(BF16), kt.astype(BF16), _NT, preferred_element_type=F32)
        if bs < rows:
            a = jnp.where((ti & -bs) == (si & -bs), a, 0.0)
        attn = attn + a
        hs //= 2

    v = i_ref[...]
    vb = v.astype(BF16)
    near = qa * kk
    intra = jnp.sum(near, axis=1, keepdims=True) * v
    pad = jnp.zeros((HG_FINE, dk), F32)
    for buf, val in ((ksh_ref, kk), (bsh_ref, bcum), (vsh_ref, v)):
        buf[0:HG_FINE, :] = pad
        buf[HG_FINE:, :] = val
    for delta in range(1, HG_FINE):
        back = slice(HG_FINE - delta, HG_FINE - delta + rows)
        prod = qa * ksh_ref[back, :] * jnp.exp(jnp.minimum(bcum - bsh_ref[back, :], 0.0))
        prod = jnp.where((row & (HG_FINE - 1)) >= delta, prod, 0.0)
        intra = intra + jnp.sum(prod, axis=1, keepdims=True) * vsh_ref[back, :]
    intra = intra + jnp.dot(attn.astype(BF16), vb, preferred_element_type=F32)
    st = st_ref[...]
    cross = lax.dot_general((qa * jnp.exp(bcum)).astype(BF16), st.astype(BF16), _NT,
                            preferred_element_type=F32)
    blast = bcum[rows - 1:rows, :]
    kb = (kk * jnp.exp(blast - bcum)).astype(BF16)
    st_ref[...] = st * jnp.exp(blast) + lax.dot_general(vb, kb, _TN, preferred_element_type=F32)
    g = g_ref[...]
    o_ref[...] = (_rms(intra + cross) * ng_ref[...] * _silu(g)).astype(BF16)


def _hgrn(z, batch, seq, lb_raw, norm_g, layer):
    n = z.shape[0]
    gw = z.shape[1] // 4
    dk = gw // HG_HEADS
    blk = HG_BLOCK
    nblk = seq // blk
    depth = lb_raw.shape[0]

    def zspec(part):
        return pl.BlockSpec((blk, dk), lambda b, h, c: (b * nblk + c, part * HG_HEADS + h))

    return pl.pallas_call(
        functools.partial(_hgrn_kernel, layer=layer),
        grid=(batch, HG_HEADS, nblk),
        in_specs=[
            zspec(0), zspec(1), zspec(2), zspec(3),
            pl.BlockSpec((depth, dk), lambda b, h, c: (0, h)),
            pl.BlockSpec((1, dk), lambda b, h, c: (0, h)),
            pl.BlockSpec((blk, blk), lambda b, h, c: (0, 0)),
        ],
        out_specs=pl.BlockSpec((blk, dk), lambda b, h, c: (b * nblk + c, h)),
        out_shape=jax.ShapeDtypeStruct((n, gw), BF16),
        scratch_shapes=[pltpu.VMEM((dk, dk), F32)] + [pltpu.VMEM((blk + HG_FINE, dk), F32)] * 3,
        compiler_params=_params(("arbitrary", "arbitrary", "arbitrary")),
        name="hgrn2",
    )(z, z, z, z, lb_raw, norm_g.reshape(1, gw), jnp.tril(jnp.ones((blk, blk), BF16)))


def _dsa_prep_kernel(zd_ref, cqg_ref, ckvg_ref, wuq_ref, qng_ref, wqit_ref,
                     q_ref, qit_ref, kv_ref, kix_ref, wht_ref):
    zd = zd_ref[...]
    cq = (_rms(zd[:, :DSA_Q_RANK]) * cqg_ref[...]).astype(BF16)
    qf = jnp.dot(cq, wuq_ref[...], preferred_element_type=F32)
    for i in range(q_ref.shape[0]):
        rs = slice(i * Q_BLOCK, (i + 1) * Q_BLOCK)
        for h in range(DSA_HEADS):
            cs = slice(h * DSA_KV_RANK, (h + 1) * DSA_KV_RANK)
            q_ref[i, h] = (_rms(qf[rs, cs]) * qng_ref[...] * (DSA_KV_RANK ** -0.5 * LOG2E)).astype(BF16)
    qit = lax.dot_general(wqit_ref[...], cq, _NT, preferred_element_type=F32)
    qit = (qit * (IDX_DIM ** -0.5)).astype(BF16)
    for i in range(q_ref.shape[0]):
        for h in range(IDX_HEADS):
            c = (i * IDX_HEADS + h) * Q_BLOCK
            qit_ref[:, c:c + Q_BLOCK] = qit[h * LANES:(h + 1) * LANES, i * Q_BLOCK:(i + 1) * Q_BLOCK]
    c0 = DSA_Q_RANK
    c1 = c0 + DSA_KV_RANK
    kv_ref[...] = (_rms(zd[:, c0:c1]) * ckvg_ref[...]).astype(BF16)
    kix_ref[...] = zd[:, c1:c1 + LANES].astype(BF16)
    wht = jnp.transpose(zd[:, c1 + LANES:c1 + 2 * LANES] * (IDX_HEADS ** -0.5))
    wht_ref[...] = wht[:IDX_HEADS, :]


def _dsa_prep(zd, cq_g, ckv_g, w_uq, qn_g, w_qit, *, tm):
    n, wd = zd.shape
    dq = w_uq.shape[1]
    dqi = w_qit.shape[0]
    full = lambda i: (0, 0)
    rows = lambda i: (i, 0)
    cols = lambda i: (0, i)
    return pl.pallas_call(
        _dsa_prep_kernel,
        grid=(n // tm,),
        in_specs=[
            pl.BlockSpec((tm, wd), rows),
            pl.BlockSpec((1, DSA_Q_RANK), full),
            pl.BlockSpec((1, DSA_KV_RANK), full),
            pl.BlockSpec((DSA_Q_RANK, dq), full),
            pl.BlockSpec((1, DSA_KV_RANK), full),
            pl.BlockSpec((dqi, DSA_Q_RANK), full),
        ],
        out_specs=[
            pl.BlockSpec((tm // Q_BLOCK, DSA_HEADS, Q_BLOCK, DSA_KV_RANK), lambda i: (i, 0, 0, 0)),
            pl.BlockSpec((LANES, IDX_HEADS * tm), cols),
            pl.BlockSpec((tm, DSA_KV_RANK), rows),
            pl.BlockSpec((tm, LANES), rows),
            pl.BlockSpec((IDX_HEADS, tm), cols),
        ],
        out_shape=[
            jax.ShapeDtypeStruct((n // Q_BLOCK, DSA_HEADS, Q_BLOCK, DSA_KV_RANK), BF16),
            jax.ShapeDtypeStruct((LANES, IDX_HEADS * n), BF16),
            jax.ShapeDtypeStruct((n, DSA_KV_RANK), BF16),
            jax.ShapeDtypeStruct((n, LANES), BF16),
            jax.ShapeDtypeStruct((IDX_HEADS, n), F32),
        ],
        compiler_params=_params(("arbitrary",)),
        name="dsa_prep",
    )(zd, cq_g.reshape(1, -1), ckv_g.reshape(1, -1), w_uq, qn_g.reshape(1, -1), w_qit)


def _dsa_select_kernel(qit_ref, wht_ref, kix_ref, m_ref, key_ref, jc_ref, hi_ref, lo_ref, lob_ref,
                       *, ksel, idx_bits):
    qb = pl.program_id(1)
    ntile = qb + 1
    ntile_all = m_ref.shape[2]
    rowi = lax.broadcasted_iota(I32, (LANES, Q_BLOCK), 0)
    coli = lax.broadcasted_iota(I32, (LANES, Q_BLOCK), 1)
    q_chunk = (qb * Q_BLOCK + coli) // CHUNK
    pairs = IDX_HEADS // 2
    w_pair = [jnp.concatenate([wht_ref[2 * p:2 * p + 1, :], wht_ref[2 * p + 1:2 * p + 2, :]], axis=1)
              for p in range(pairs)]

    def tile_rows(j):
        return pl.ds(pl.multiple_of(j * LANES, LANES), LANES)

    def admissible(j):
        return ((j * LANES + rowi) // CHUNK) <= q_chunk

    def score_tile(jj, carry):
        for t in range(2):
            j = 2 * jj + t
            kt = kix_ref[tile_rows(j), :]
            sc = None
            for p in range(pairs):
                s2 = jnp.dot(kt, qit_ref[:, 2 * p * Q_BLOCK:2 * (p + 1) * Q_BLOCK],
                             preferred_element_type=F32)
                c2 = w_pair[p] * jnp.maximum(s2, 0.0)
                c = c2[:, :Q_BLOCK] + c2[:, Q_BLOCK:]
                sc = c if sc is None else sc + c
            bits = pltpu.bitcast(sc, I32)
            key = bits ^ ((bits >> 31) & 0x7FFFFFFF)
            key = jnp.where(admissible(j), key, INT_MIN)
            key_ref[tile_rows(j), :] = key
            hi_ref[tile_rows(j), :] = (key >> 16).astype(I16)
            lo_ref[tile_rows(j), :] = ((key & 0xFFFF) - HALF_OFFSET).astype(I16)
        return carry

    @pl.when(qb == 0)
    def _():
        key_ref[...] = jnp.full(key_ref.shape, INT_MIN, I32)
        hi_ref[...] = jnp.full(hi_ref.shape, -HALF_OFFSET, I16)
        lo_ref[...] = jnp.full(lo_ref.shape, -HALF_OFFSET, I16)

    lax.fori_loop(0, (ntile + 1) // 2, score_tile, 0)
    nquad = (ntile + SCAN_TILES - 1) // SCAN_TILES

    def count16(ref, pred_fn):
        def body(jj, acc):
            for t in range(SCAN_TILES):
                hit = pred_fn(ref[tile_rows(SCAN_TILES * jj + t), :])
                acc = acc + jnp.where(hit, jnp.int16(1), jnp.int16(0))
            return acc
        acc = lax.fori_loop(0, nquad, body, jnp.zeros((LANES, Q_BLOCK), I16))
        return jnp.sum(acc.astype(I32), axis=0, keepdims=True)

    def search16(ref, want, cnt_start):
        def bit(i, c):
            v_u, cnt_v = c
            cand_u = v_u | jnp.left_shift(jnp.int32(1), 15 - i)
            cand = (cand_u - HALF_OFFSET).astype(I16)
            cnt = count16(ref, lambda v: v >= cand)
            take = cnt >= want
            return jnp.where(take, cand_u, v_u), jnp.where(take, cnt, cnt_v)
        return lax.fori_loop(0, 16, bit, (jnp.zeros((1, Q_BLOCK), I32), cnt_start))

    def count(pred_fn):
        def body(jj, acc):
            for t in range(2):
                j = 2 * jj + t
                acc = acc + pred_fn(j, key_ref[tile_rows(j), :]).astype(I32)
            return acc
        acc = lax.fori_loop(0, (ntile + 1) // 2, body, jnp.zeros((LANES, Q_BLOCK), I32))
        return jnp.sum(acc, axis=0, keepdims=True)

    cnt0 = (q_chunk[0:1, :] + 1) * CHUNK
    p_u, cnt_p = search16(hi_ref, ksel, cnt0)
    p16 = (p_u - HALF_OFFSET).astype(I16)
    cnt_above = count16(hi_ref, lambda v: v > p16)

    def boundary_tile(jj, carry):
        for t in range(SCAN_TILES):
            rows = tile_rows(SCAN_TILES * jj + t)
            lob_ref[rows, :] = jnp.where(hi_ref[rows, :] == p16, lo_ref[rows, :], jnp.int16(-HALF_OFFSET))
        return carry

    lax.fori_loop(0, nquad, boundary_tile, 0)
    q_u, cnt_q = search16(lob_ref, ksel - cnt_above, cnt_p - cnt_above)
    thr = jnp.left_shift(p_u - HALF_OFFSET, 16) | q_u
    cnt_t = cnt_above + cnt_q
    tied = jnp.max(jnp.where(cnt_t > ksel, 1.0, 0.0))

    jc_ref[...] = jnp.full(jc_ref.shape, 2 ** 31 - 1, I32)

    @pl.when(tied > 0.0)
    def _():
        need = ksel - count(lambda j, k: k > thr)

        def index_bit(i, j_c):
            cand = j_c | jnp.left_shift(jnp.int32(1), idx_bits - 1 - i)
            cnt = count(lambda j, k: (k == thr) & ((j * LANES + rowi) < cand))
            return jnp.where(cnt < need, cand, j_c)

        j_c = lax.fori_loop(0, idx_bits, index_bit, jnp.zeros((1, Q_BLOCK), I32))
        jc_ref[...] = jnp.broadcast_to(j_c, jc_ref.shape)

    j_c = jc_ref[0:1, :]
    eye = (rowi == coli).astype(BF16)

    group = KV_TILE // LANES

    def write_group(g, carry):
        for t in range(group):
            j = g * group + t
            k = key_ref[tile_rows(j), :]
            sel = (k > thr) | ((k == thr) & ((j * LANES + rowi) <= j_c))
            sel = jnp.where(sel & admissible(j), 1.0, 0.0).astype(BF16)
            sel_t = lax.dot_general(eye, sel, _NT, preferred_element_type=F32)
            m_ref[0, 0, j] = ((sel_t - 1.0) * -NEG_BIG).astype(BF16)
        return carry

    ngroup = (ntile + group - 1) // group
    lax.fori_loop(0, ngroup, write_group, 0)

    def blank_tile(j, carry):
        m_ref[0, 0, j] = jnp.full((Q_BLOCK, LANES), NEG_BIG, BF16)
        return carry

    lax.fori_loop(ngroup * group, ntile_all, blank_tile, 0)


def _dsa_select(qit, wht, kix, batch, seq, ksel):
    nqb = seq // Q_BLOCK
    nkt = seq // LANES
    return pl.pallas_call(
        functools.partial(_dsa_select_kernel, ksel=ksel, idx_bits=int(math.log2(seq))),
        grid=(batch, nqb),
        in_specs=[
            pl.BlockSpec((LANES, IDX_HEADS * Q_BLOCK), lambda b, q: (0, b * nqb + q)),
            pl.BlockSpec((IDX_HEADS, Q_BLOCK), lambda b, q: (0, b * nqb + q)),
            pl.BlockSpec((seq, LANES), lambda b, q: (b, 0)),
        ],
        out_specs=pl.BlockSpec((1, 1, nkt, Q_BLOCK, LANES), lambda b, q: (b, q, 0, 0, 0)),
        out_shape=jax.ShapeDtypeStruct((batch, nqb, nkt, Q_BLOCK, LANES), BF16),
        scratch_shapes=[pltpu.VMEM((seq, Q_BLOCK), I32), pltpu.VMEM((8, Q_BLOCK), I32)]
        + [pltpu.VMEM((seq, Q_BLOCK), I16)] * 3,
        compiler_params=_params(("arbitrary", "arbitrary")),
        name="dsa_select",
    )(qit, wht, kix)


def _rel_bucket(rel):
    nb = REL_BUCKETS // 2
    max_exact = nb // 2
    ret = jnp.where(rel > 0, nb, 0)
    n = jnp.abs(rel)
    nf = jnp.maximum(n, 1).astype(F32)
    large = max_exact + (jnp.log(nf / max_exact) / math.log(REL_MAX_DIST / max_exact)
                         * (nb - max_exact)).astype(I32)
    large = jnp.minimum(large, nb - 1)
    return ret + jnp.where(n < max_exact, n, large)


NEAR_TILES = 3


def _dsa_attn_kernel(q_ref, kv_ref, mask_ref, rb_ref, wuv_ref, o_ref,
                     m_ref, l_ref, alpha_ref, acc_ref, corr_ref, s_ref, p_ref, madd_ref):
    b, qb = pl.program_id(0), pl.program_id(1)
    sub = KV_TILE // LANES
    far_bucket = REL_BUCKETS // 2 - 1
    half = DSA_HEADS * Q_BLOCK // 2

    def head_rows(h):
        return slice(h * Q_BLOCK, (h + 1) * Q_BLOCK)

    @pl.when((b == 0) & (qb == 0))
    def _():
        ti = lax.broadcasted_iota(I32, (Q_BLOCK, LANES), 0)
        si = lax.broadcasted_iota(I32, (Q_BLOCK, LANES), 1)
        for oi in range(NEAR_TILES):
            bucket = _rel_bucket((oi - (NEAR_TILES - 1)) * LANES + si - ti)
            for h in range(DSA_HEADS):
                tbl = jnp.zeros((Q_BLOCK, LANES), F32)
                for bk in range(REL_BUCKETS):
                    tbl = jnp.where(bucket == bk, rb_ref[bk, h], tbl)
                corr_ref[oi, head_rows(h), :] = (tbl - rb_ref[far_bucket, h]) * LOG2E

    m_ref[...] = jnp.full_like(m_ref, NEG_BIG)
    l_ref[...] = jnp.zeros_like(l_ref)
    acc_ref[...] = jnp.zeros_like(acc_ref)
    q_all = q_ref[0].reshape(DSA_HEADS * Q_BLOCK, DSA_KV_RANK)

    def key_step(kt, carry):
        kvt = kv_ref[pl.ds(pl.multiple_of(kt * KV_TILE, KV_TILE), KV_TILE), :]
        for part in range(2):
            rs = slice(part * half, (part + 1) * half)
            s_ref[rs, :] = lax.dot_general(q_all[rs], kvt, _NT, preferred_element_type=F32)
        for j in range(sub):
            d = kt * sub + j - qb

            @pl.when((d > -NEAR_TILES) & (d <= 0))
            def _(j=j, d=d):
                s_ref[:, j * LANES:(j + 1) * LANES] += corr_ref[d + NEAR_TILES - 1]

        for j in range(sub):
            madd_ref[:, j * LANES:(j + 1) * LANES] = mask_ref[0, 0, kt * sub + j].astype(F32)
        groups_per_head = Q_BLOCK // SM_ROWS
        for g in range(DSA_HEADS * groups_per_head):
            rs = slice(g * SM_ROWS, (g + 1) * SM_ROWS)
            qg = g % groups_per_head
            s = s_ref[rs, :] + madd_ref[qg * SM_ROWS:(qg + 1) * SM_ROWS, :]
            m_old = m_ref[rs, :]
            m_new = jnp.maximum(m_old, jnp.max(s, axis=1, keepdims=True))
            alpha = jnp.exp2(m_old - m_new)
            p = jnp.exp2(s - jnp.tile(m_new, (1, sub)))
            l_ref[rs, :] = alpha * l_ref[rs, :] + jnp.sum(p, axis=1, keepdims=True)
            alpha_ref[rs, :] = alpha
            p_ref[rs, :] = p.astype(BF16)
            m_ref[rs, :] = m_new
        for part in range(2):
            rs = slice(part * half, (part + 1) * half)
            pv = jnp.dot(p_ref[rs, :], kvt, preferred_element_type=F32)
            acc_ref[rs, :] = jnp.tile(alpha_ref[rs, :], (1, DSA_KV_RANK // LANES)) * acc_ref[rs, :] + pv
        return carry

    lax.fori_loop(0, qb // sub + 1, key_step, 0)

    dv = wuv_ref.shape[2]
    for h in range(DSA_HEADS):
        rs = head_rows(h)
        o = (acc_ref[rs, :] / jnp.tile(l_ref[rs, :], (1, DSA_KV_RANK // LANES))).astype(BF16)
        o_ref[:, h * dv:(h + 1) * dv] = jnp.dot(o, wuv_ref[h], preferred_element_type=F32).astype(BF16)


def _dsa_attn(q, kv, mask, rel_bias, w_uv, batch, seq):
    n = kv.shape[0]
    nqb = seq // Q_BLOCK
    dv = w_uv.shape[2]
    rows = DSA_HEADS * Q_BLOCK
    return pl.pallas_call(
        _dsa_attn_kernel,
        grid=(batch, nqb),
        in_specs=[
            pl.BlockSpec((1, DSA_HEADS, Q_BLOCK, DSA_KV_RANK), lambda b, qb: (b * nqb + qb, 0, 0, 0)),
            pl.BlockSpec((seq, DSA_KV_RANK), lambda b, qb: (b, 0)),
            pl.BlockSpec((1, 1) + mask.shape[2:], lambda b, qb: (b, qb, 0, 0, 0)),
            pl.BlockSpec(memory_space=pltpu.SMEM),
            pl.BlockSpec(w_uv.shape, lambda b, qb: (0, 0, 0)),
        ],
        out_specs=pl.BlockSpec((Q_BLOCK, DSA_HEADS * dv), lambda b, qb: (b * nqb + qb, 0)),
        out_shape=jax.ShapeDtypeStruct((n, DSA_HEADS * dv), BF16),
        scratch_shapes=[
            pltpu.VMEM((rows, LANES), F32),
            pltpu.VMEM((rows, LANES), F32),
            pltpu.VMEM((rows, LANES), F32),
            pltpu.VMEM((rows, DSA_KV_RANK), F32),
            pltpu.VMEM((NEAR_TILES, rows, LANES), F32),
            pltpu.VMEM((rows, KV_TILE), F32),
            pltpu.VMEM((rows, KV_TILE), BF16),
            pltpu.VMEM((Q_BLOCK, KV_TILE), F32),
        ],
        compiler_params=_params(("arbitrary", "arbitrary")),
        name="dsa_attn",
    )(q, kv, mask, rel_bias, w_uv)


def _pad_cols(w, width):
    return jnp.pad(w, ((0, 0), (0, width - w.shape[1])))


def kernel(x, ln_mix_g, ln_ffn_g, w_ffn_gate, w_ffn_up, w_ffn_down, rel_bias, ev_w_in, ev_w_out, sgu_ln_g, sgu_ln_b, sgu_w_s, sgu_b_s, od_w_in, od_w_out, hgrn_lb, hgrn_norm_g, dsa_cq_g, dsa_ckv_g, dsa_w_uq, dsa_qnorm_g, dsa_w_qidx, dsa_w_uv):
    batch, seq, d = x.shape
    n = batch * seq
    depth = ln_mix_g.shape[0]
    ksel = min(TOPK_MAX, seq // 4)
    tm = min(PROJ_ROWS, n)
    xf = x.reshape(n, d)
    wg_all, wu_all, wd_all = (w.astype(BF16) for w in (w_ffn_gate, w_ffn_up, w_ffn_down))
    ev_in_all, od_in_all = ev_w_in.astype(BF16), od_w_in.astype(BF16)
    for layer in range(depth):
        j = layer // 2
        if layer % 2 == 0:
            z = _norm_matmul(xf, ln_mix_g[layer], ev_in_all, j, ev_in_all.shape[2], tm=tm, tn=PROJ_COLS)
            a1 = _retention(z, batch, seq)
            a2 = _sgu(z, sgu_ln_g[j], sgu_ln_b[j], sgu_w_s[j], sgu_b_s[j], rows=256)
            w_out = ev_w_out[j]
        else:
            w_in = od_w_in[j]
            gw = d // 2
            c = 4 * gw
            c_kidx = c + DSA_Q_RANK + DSA_KV_RANK
            w_dsa = jnp.concatenate([
                w_in[:, c:c_kidx],
                _pad_cols(w_in[:, c_kidx:c_kidx + IDX_DIM], LANES),
                _pad_cols(w_in[:, c_kidx + IDX_DIM:], LANES),
            ], axis=1).astype(BF16)[None]
            z = _norm_matmul(xf, ln_mix_g[layer], od_in_all, j, c, tm=tm, tn=PROJ_COLS)
            zd = _norm_matmul(xf, ln_mix_g[layer], w_dsa, 0, w_dsa.shape[2], tm=tm, tn=w_dsa.shape[2])
            a1 = _hgrn(z, batch, seq, hgrn_lb, hgrn_norm_g[j], layer)
            w_qit = jnp.pad(dsa_w_qidx[j].T.reshape(IDX_HEADS, IDX_DIM, DSA_Q_RANK),
                            ((0, 0), (0, LANES - IDX_DIM), (0, 0))).reshape(IDX_HEADS * LANES, DSA_Q_RANK)
            q, qit, kv, kix, wht = _dsa_prep(zd, dsa_cq_g[j], dsa_ckv_g[j], dsa_w_uq[j].astype(BF16),
                                             dsa_qnorm_g[j], w_qit.astype(BF16), tm=256)
            mask = _dsa_select(qit, wht, kix, batch, seq, ksel)
            a2 = _dsa_attn(q, kv, mask, rel_bias, dsa_w_uv[j].astype(BF16), batch, seq)
            w_out = od_w_out[j]
        xf = _outproj(a1, a2, w_out.astype(BF16), xf, tm=tm, tn=OUT_COLS)
        xf = _ffn(xf, ln_ffn_g[layer], wg_all, wu_all, wd_all, layer, tm=min(FFN_ROWS, n), tf=FFN_COLS)
    return xf.reshape(batch, seq, d)
```

```python
import functools
import math

import jax
import jax.numpy as jnp
from jax import lax
from jax.experimental import pallas as pl
from jax.experimental.pallas import tpu as pltpu

F32 = jnp.float32
BF16 = jnp.bfloat16
I32 = jnp.int32

EPS = 1e-6
CHUNK = 64
LANES = 128
ROPE_BASE = 10000.0
RET_HEADS = 4
SGU_WINDOW = 128
SGU_GROUPS = 4
HG_HEADS = 8
DSA_HEADS = 8
DSA_Q_RANK = 384
DSA_KV_RANK = 256
IDX_HEADS = 16
IDX_DIM = 64
TOPK_MAX = 256
Q_BLOCK = 128
KV_TILE = 512
SM_ROWS = 64
LOG2E = math.log2(math.e)
REL_BUCKETS = 32
REL_MAX_DIST = 256
NEG_BIG = -1e30
INT_MIN = -(2 ** 31)
SCAN_TILES = 2

RET_BLOCK = 256
RET_HEADS_PER_STEP = 4
HG_BLOCK = 256
HG_FINE = 8
VMEM_LIMIT = 48 * 1024 * 1024
PROJ_ROWS = 1024
PROJ_COLS = 1024
OUT_COLS = 1024
FFN_ROWS = 1024
FFN_VMEM_LIMIT = 58 * 1024 * 1024
FFN_COLS = 512

_NT = (((1,), (1,)), ((), ()))
_TN = (((0,), (0,)), ((), ()))


def _params(semantics, vmem_limit=VMEM_LIMIT):
    return pltpu.CompilerParams(dimension_semantics=semantics, vmem_limit_bytes=vmem_limit)


def _silu(x):
    return x * jax.nn.sigmoid(x)


def _rms(x):
    return x * lax.rsqrt(jnp.mean(x * x, axis=-1, keepdims=True) + EPS)


def _norm_matmul_kernel(x_ref, g_ref, w_ref, o_ref, h_ref):
    @pl.when(pl.program_id(1) == 0)
    def _():
        h_ref[...] = (_rms(x_ref[...]) * g_ref[...]).astype(BF16)

    o_ref[...] = jnp.dot(h_ref[...], w_ref[...], preferred_element_type=F32).astype(o_ref.dtype)


def _norm_matmul(x, g, w, layer, nout, *, tm, tn, out_dtype=F32):
    n, d = x.shape
    return pl.pallas_call(
        _norm_matmul_kernel,
        grid=(n // tm, nout // tn),
        in_specs=[
            pl.BlockSpec((tm, d), lambda i, j: (i, 0)),
            pl.BlockSpec((1, d), lambda i, j: (0, 0)),
            pl.BlockSpec((None, d, tn), lambda i, j: (layer, 0, j)),
        ],
        out_specs=pl.BlockSpec((tm, tn), lambda i, j: (i, j)),
        out_shape=jax.ShapeDtypeStruct((n, nout), out_dtype),
        scratch_shapes=[pltpu.VMEM((tm, d), BF16)],
        compiler_params=_params(("arbitrary", "arbitrary")),
        name="norm_matmul",
    )(x, g.reshape(1, d), w)


def _outproj_kernel(a1_ref, a2_ref, w1_ref, w2_ref, r_ref, o_ref):
    acc = jnp.dot(a1_ref[...], w1_ref[...], preferred_element_type=F32)
    acc += jnp.dot(a2_ref[...], w2_ref[...], preferred_element_type=F32)
    o_ref[...] = r_ref[...] + acc


def _outproj(a1, a2, w, res, *, tm, tn):
    n, half = a1.shape
    d = w.shape[1]
    return pl.pallas_call(
        _outproj_kernel,
        grid=(n // tm, d // tn),
        in_specs=[
            pl.BlockSpec((tm, half), lambda i, j: (i, 0)),
            pl.BlockSpec((tm, half), lambda i, j: (i, 0)),
            pl.BlockSpec((half, tn), lambda i, j: (0, j)),
            pl.BlockSpec((half, tn), lambda i, j: (1, j)),
            pl.BlockSpec((tm, tn), lambda i, j: (i, j)),
        ],
        out_specs=pl.BlockSpec((tm, tn), lambda i, j: (i, j)),
        out_shape=jax.ShapeDtypeStruct((n, d), F32),
        compiler_params=_params(("arbitrary", "arbitrary")),
        name="outproj",
    )(a1, a2, w, w, res)


def _ffn_kernel(x_ref, g_ref, wg_ref, wu_ref, wd_ref, o_ref, h_ref):
    @pl.when(pl.program_id(1) == 0)
    def _():
        x = x_ref[...]
        h_ref[...] = (_rms(x) * g_ref[...]).astype(BF16)
        o_ref[...] = x

    h = h_ref[...]
    a = jnp.dot(h, wg_ref[...], preferred_element_type=F32)
    u = jnp.dot(h, wu_ref[...], preferred_element_type=F32)
    act = (_silu(a) * u).astype(BF16)
    o_ref[...] += jnp.dot(act, wd_ref[...], preferred_element_type=F32)


def _ffn(x, g, wg, wu, wd, layer, *, tm, tf):
    n, d = x.shape
    dff = wg.shape[2]
    return pl.pallas_call(
        _ffn_kernel,
        grid=(n // tm, dff // tf),
        in_specs=[
            pl.BlockSpec((tm, d), lambda i, f: (i, 0)),
            pl.BlockSpec((1, d), lambda i, f: (0, 0)),
            pl.BlockSpec((None, d, tf), lambda i, f: (layer, 0, f)),
            pl.BlockSpec((None, d, tf), lambda i, f: (layer, 0, f)),
            pl.BlockSpec((None, tf, d), lambda i, f: (layer, f, 0)),
        ],
        out_specs=pl.BlockSpec((tm, d), lambda i, f: (i, 0)),
        out_shape=jax.ShapeDtypeStruct((n, d), F32),
        scratch_shapes=[pltpu.VMEM((tm, d), BF16)],
        compiler_params=_params(("arbitrary", "arbitrary"), FFN_VMEM_LIMIT),
        name="ffn",
    )(x, g.reshape(1, d), wg, wu, wd)


def _retention_kernel(q_ref, k_ref, v_ref, g_ref, cos_ref, sin_ref, d_ref, xi_ref, zeta_ref,
                      gl_ref, o_ref, state_ref):
    @pl.when(pl.program_id(2) == 0)
    def _():
        state_ref[...] = jnp.zeros_like(state_ref)

    cos = cos_ref[...]
    sin = sin_ref[...]
    half = cos.shape[1]
    dk = 2 * half

    def rot(t):
        t1, t2 = t[:, :half], t[:, half:]
        return jnp.concatenate([t1 * cos - t2 * sin, t1 * sin + t2 * cos], axis=1)

    for i in range(RET_HEADS_PER_STEP):
        cs = slice(i * dk, (i + 1) * dk)
        q = rot(q_ref[:, cs].astype(F32))
        k = rot(k_ref[:, cs].astype(F32)) * (dk ** -0.5)
        qb = q.astype(BF16)
        vb = v_ref[:, cs].astype(BF16)
        scores = lax.dot_general(qb, k.astype(BF16), _NT, preferred_element_type=F32) * d_ref[i]
        intra = jnp.dot(scores.astype(BF16), vb, preferred_element_type=F32)
        state = state_ref[i]
        cross = jnp.dot(qb, state.astype(BF16), preferred_element_type=F32) * xi_ref[i]
        kz = (k * zeta_ref[i]).astype(BF16)
        state_ref[i] = state * gl_ref[i] + lax.dot_general(kz, vb, _TN, preferred_element_type=F32)
        o_ref[:, cs] = (_rms(intra + cross) * _silu(g_ref[:, cs].astype(F32))).astype(BF16)


def _retention_tables(seq, dk):
    blk = RET_BLOCK
    pos = jnp.arange(seq, dtype=F32)
    inv = ROPE_BASE ** (-jnp.arange(0, dk, 2, dtype=F32) / dk)
    ang = pos[:, None] * inv[None, :]
    log_gamma = jnp.log(1.0 - 2.0 ** (-5.0 - jnp.arange(RET_HEADS, dtype=F32)))
    i = jnp.arange(blk)
    same = (i[:, None] // CHUNK) == (i[None, :] // CHUNK)
    earlier = (i[None, :] // CHUNK) < (i[:, None] // CHUNK)
    diff = (i[:, None] - i[None, :]).astype(F32)
    dist = jnp.where(same, jnp.abs(diff), diff)
    decay = jnp.where((same | earlier)[None], jnp.exp(log_gamma[:, None, None] * dist[None]), 0.0)
    p = jnp.arange(blk, dtype=F32)
    wide = (RET_HEADS, blk, dk)
    xi = jnp.broadcast_to(jnp.exp(log_gamma[:, None] * (p + 1.0))[:, :, None], wide)
    zeta = jnp.broadcast_to(jnp.exp(log_gamma[:, None] * (blk - 1.0 - p))[:, :, None], wide)
    g_blk = jnp.broadcast_to(jnp.exp(log_gamma * blk)[:, None, None], (RET_HEADS, 1, dk))
    return jnp.cos(ang), jnp.sin(ang), decay, xi, zeta, g_blk


def _retention(z, batch, seq):
    n = z.shape[0]
    gw = z.shape[1] // 6
    dk = gw // RET_HEADS
    blk = RET_BLOCK
    nblk = seq // blk
    cos, sin, decay, xi, zeta, g_blk = _retention_tables(seq, dk)

    hp = RET_HEADS_PER_STEP
    groups = RET_HEADS // hp

    def zspec(part):
        return pl.BlockSpec((blk, hp * dk), lambda b, h, c: (b * nblk + c, part * groups + h))

    def hspec(rows, cols):
        return pl.BlockSpec((hp, rows, cols), lambda b, h, c: (h, 0, 0))

    return pl.pallas_call(
        _retention_kernel,
        grid=(batch, groups, nblk),
        in_specs=[
            zspec(0), zspec(1), zspec(2), zspec(3),
            pl.BlockSpec((blk, dk // 2), lambda b, h, c: (c, 0)),
            pl.BlockSpec((blk, dk // 2), lambda b, h, c: (c, 0)),
            hspec(blk, blk), hspec(blk, dk), hspec(blk, dk), hspec(1, dk),
        ],
        out_specs=pl.BlockSpec((blk, hp * dk), lambda b, h, c: (b * nblk + c, h)),
        out_shape=jax.ShapeDtypeStruct((n, gw), BF16),
        scratch_shapes=[pltpu.VMEM((hp, dk, dk), F32)],
        compiler_params=_params(("arbitrary", "arbitrary", "arbitrary")),
        name="retention",
    )(z, z, z, z, cos, sin, decay, xi, zeta, g_blk)


def _gelu(x):
    return 0.5 * x * (1.0 + lax.erf(x * math.sqrt(0.5)))


def _sgu_kernel(u_ref, v_ref, lng_ref, lnb_ref, w_ref, b_ref, o_ref):
    rows, width = v_ref.shape
    dg = width // SGU_GROUPS
    v = _gelu(v_ref[...].astype(F32))
    mu = jnp.mean(v, axis=-1, keepdims=True)
    var = jnp.mean(jnp.square(v - mu), axis=-1, keepdims=True)
    vn = ((v - mu) * lax.rsqrt(var + EPS) * lng_ref[...] + lnb_ref[...]).astype(BF16)
    u = _gelu(u_ref[...].astype(F32))
    ri = lax.broadcasted_iota(I32, (SGU_WINDOW, SGU_WINDOW), 0) // CHUNK
    ci = lax.broadcasted_iota(I32, (SGU_WINDOW, SGU_WINDOW), 1) // CHUNK
    allowed = ci <= ri
    for g in range(SGU_GROUPS):
        wg = jnp.where(allowed, w_ref[g], 0.0).astype(BF16)
        bias = b_ref[g]
        for w in range(rows // SGU_WINDOW):
            rs = slice(w * SGU_WINDOW, (w + 1) * SGU_WINDOW)
            cs = slice(g * dg, (g + 1) * dg)
            mixed = jnp.dot(wg, vn[rs, cs], preferred_element_type=F32) + bias
            o_ref[rs, cs] = (u[rs, cs] * mixed).astype(BF16)


def _sgu(z, ln_g, ln_b, w_s, b_s, *, rows):
    n = z.shape[0]
    gw = z.shape[1] // 6
    return pl.pallas_call(
        _sgu_kernel,
        grid=(n // rows,),
        in_specs=[
            pl.BlockSpec((rows, gw), lambda i: (i, 4)),
            pl.BlockSpec((rows, gw), lambda i: (i, 5)),
            pl.BlockSpec((1, gw), lambda i: (0, 0)),
            pl.BlockSpec((1, gw), lambda i: (0, 0)),
            pl.BlockSpec((SGU_GROUPS, SGU_WINDOW, SGU_WINDOW), lambda i: (0, 0, 0)),
            pl.BlockSpec((SGU_GROUPS, SGU_WINDOW, 1), lambda i: (0, 0, 0)),
        ],
        out_specs=pl.BlockSpec((rows, gw), lambda i: (i, 0)),
        out_shape=jax.ShapeDtypeStruct((n, gw), BF16),
        compiler_params=_params(("arbitrary",)),
        name="sgu",
    )(z, z, ln_g.reshape(1, gw), ln_b.reshape(1, gw), w_s, b_s.reshape(SGU_GROUPS, SGU_WINDOW, 1))


def _hgrn_kernel(q_ref, f_ref, i_ref, g_ref, lb_ref, ng_ref, tri_ref, o_ref,
                 st_ref, ksh_ref, bsh_ref, vsh_ref, *, layer):
    rows, dk = q_ref.shape

    @pl.when(pl.program_id(2) == 0)
    def _():
        st_ref[...] = jnp.zeros_like(st_ref)

    lbp = lb_ref[...]
    e = jnp.exp(lbp - jnp.max(lbp, axis=0, keepdims=True))
    soft = e / jnp.sum(e, axis=0, keepdims=True)
    lb = jnp.sum(soft[1:layer + 1], axis=0, keepdims=True)

    f = lb + (1.0 - lb) * jax.nn.sigmoid(f_ref[...].astype(F32))
    lf = jnp.log(f)
    kk = 1.0 - f
    qa = _silu(q_ref[...].astype(F32))

    tri = tri_ref[...]
    bcum = None
    rest = lf
    for _ in range(3):
        term = rest.astype(BF16)
        part = jnp.dot(tri, term, preferred_element_type=F32)
        bcum = part if bcum is None else bcum + part
        rest = rest - term.astype(F32)

    row = lax.broadcasted_iota(I32, (rows, dk), 0)
    ti = lax.broadcasted_iota(I32, (rows, rows), 0)
    si = lax.broadcasted_iota(I32, (rows, rows), 1)
    attn = jnp.zeros((rows, rows), F32)
    hs = rows // 2
    while hs >= HG_FINE:
        bs = 2 * hs
        parts = [jnp.broadcast_to(bcum[b * bs + hs - 1:b * bs + hs, :], (bs, dk))
                 for b in range(rows // bs)]
        anchor = parts[0] if len(parts) == 1 else jnp.concatenate(parts, axis=0)
        upper = (row & (bs - 1)) >= hs
        fac = jnp.exp(-jnp.abs(bcum - anchor))
        qt = jnp.where(upper, qa * fac, 0.0)
        kt = jnp.where(upper, 0.0, kk * fac)
        a = lax.dot_general(qt.astype(BF16), kt.astype(BF16), _NT, preferred_element_type=F32)
        if bs < rows:
            a = jnp.where((ti & -bs) == (si & -bs), a, 0.0)
        attn = attn + a
        hs //= 2

    v = i_ref[...].astype(F32)
    vb = v.astype(BF16)
    near = qa * kk
    intra = jnp.sum(near, axis=1, keepdims=True) * v
    pad = jnp.zeros((HG_FINE, dk), F32)
    for buf, val in ((ksh_ref, kk), (bsh_ref, bcum), (vsh_ref, v)):
        buf[0:HG_FINE, :] = pad
        buf[HG_FINE:, :] = val
    for delta in range(1, HG_FINE):
        back = slice(HG_FINE - delta, HG_FINE - delta + rows)
        prod = qa * ksh_ref[back, :] * jnp.exp(jnp.minimum(bcum - bsh_ref[back, :], 0.0))
        prod = jnp.where((row & (HG_FINE - 1)) >= delta, prod, 0.0)
        intra = intra + jnp.sum(prod, axis=1, keepdims=True) * vsh_ref[back, :]
    intra = intra + jnp.dot(attn.astype(BF16), vb, preferred_element_type=F32)
    st = st_ref[...]
    cross = lax.dot_general((qa * jnp.exp(bcum)).astype(BF16), st.astype(BF16), _NT,
                            preferred_element_type=F32)
    blast = bcum[rows - 1:rows, :]
    kb = (kk * jnp.exp(blast - bcum)).astype(BF16)
    st_ref[...] = st * jnp.exp(blast) + lax.dot_general(vb, kb, _TN, preferred_element_type=F32)
    g = g_ref[...].astype(F32)
    o_ref[...] = (_rms(intra + cross) * ng_ref[...] * _silu(g)).astype(BF16)


def _hgrn(z, batch, seq, lb_raw, norm_g, layer):
    n = z.shape[0]
    gw = z.shape[1] // 4
    dk = gw // HG_HEADS
    blk = HG_BLOCK
    nblk = seq // blk
    depth = lb_raw.shape[0]

    def zspec(part):
        return pl.BlockSpec((blk, dk), lambda b, h, c: (b * nblk + c, part * HG_HEADS + h))

    return pl.pallas_call(
        functools.partial(_hgrn_kernel, layer=layer),
        grid=(batch, HG_HEADS, nblk),
        in_specs=[
            zspec(0), zspec(1), zspec(2), zspec(3),
            pl.BlockSpec((depth, dk), lambda b, h, c: (0, h)),
            pl.BlockSpec((1, dk), lambda b, h, c: (0, h)),
            pl.BlockSpec((blk, blk), lambda b, h, c: (0, 0)),
        ],
        out_specs=pl.BlockSpec((blk, dk), lambda b, h, c: (b * nblk + c, h)),
        out_shape=jax.ShapeDtypeStruct((n, gw), BF16),
        scratch_shapes=[pltpu.VMEM((dk, dk), F32)] + [pltpu.VMEM((blk + HG_FINE, dk), F32)] * 3,
        compiler_params=_params(("arbitrary", "arbitrary", "arbitrary")),
        name="hgrn2",
    )(z, z, z, z, lb_raw, norm_g.reshape(1, gw), jnp.tril(jnp.ones((blk, blk), BF16)))


def _dsa_prep_kernel(zd_ref, cqg_ref, ckvg_ref, wuq_ref, qng_ref, wqit_ref,
                     q_ref, qit_ref, kv_ref, kix_ref, wht_ref):
    zd = zd_ref[...]
    cq = (_rms(zd[:, :DSA_Q_RANK]) * cqg_ref[...]).astype(BF16)
    qf = jnp.dot(cq, wuq_ref[...], preferred_element_type=F32)
    for i in range(q_ref.shape[0]):
        rs = slice(i * Q_BLOCK, (i + 1) * Q_BLOCK)
        for h in range(DSA_HEADS):
            cs = slice(h * DSA_KV_RANK, (h + 1) * DSA_KV_RANK)
            q_ref[i, h] = (_rms(qf[rs, cs]) * qng_ref[...] * (DSA_KV_RANK ** -0.5 * LOG2E)).astype(BF16)
    qit = lax.dot_general(wqit_ref[...], cq, _NT, preferred_element_type=F32)
    qit = (qit * (IDX_DIM ** -0.5)).astype(BF16)
    for i in range(q_ref.shape[0]):
        for h in range(IDX_HEADS):
            c = (i * IDX_HEADS + h) * Q_BLOCK
            qit_ref[:, c:c + Q_BLOCK] = qit[h * LANES:(h + 1) * LANES, i * Q_BLOCK:(i + 1) * Q_BLOCK]
    c0 = DSA_Q_RANK
    c1 = c0 + DSA_KV_RANK
    kv_ref[...] = (_rms(zd[:, c0:c1]) * ckvg_ref[...]).astype(BF16)
    kix_ref[...] = zd[:, c1:c1 + LANES].astype(BF16)
    wht = jnp.transpose(zd[:, c1 + LANES:c1 + 2 * LANES] * (IDX_HEADS ** -0.5))
    wht_ref[...] = wht[:IDX_HEADS, :]


def _dsa_prep(zd, cq_g, ckv_g, w_uq, qn_g, w_qit, *, tm):
    n, wd = zd.shape
    dq = w_uq.shape[1]
    dqi = w_qit.shape[0]
    full = lambda i: (0, 0)
    rows = lambda i: (i, 0)
    cols = lambda i: (0, i)
    return pl.pallas_call(
        _dsa_prep_kernel,
        grid=(n // tm,),
        in_specs=[
            pl.BlockSpec((tm, wd), rows),
            pl.BlockSpec((1, DSA_Q_RANK), full),
            pl.BlockSpec((1, DSA_KV_RANK), full),
            pl.BlockSpec((DSA_Q_RANK, dq), full),
            pl.BlockSpec((1, DSA_KV_RANK), full),
            pl.BlockSpec((dqi, DSA_Q_RANK), full),
        ],
        out_specs=[
            pl.BlockSpec((tm // Q_BLOCK, DSA_HEADS, Q_BLOCK, DSA_KV_RANK), lambda i: (i, 0, 0, 0)),
            pl.BlockSpec((LANES, IDX_HEADS * tm), cols),
            pl.BlockSpec((tm, DSA_KV_RANK), rows),
            pl.BlockSpec((tm, LANES), rows),
            pl.BlockSpec((IDX_HEADS, tm), cols),
        ],
        out_shape=[
            jax.ShapeDtypeStruct((n // Q_BLOCK, DSA_HEADS, Q_BLOCK, DSA_KV_RANK), BF16),
            jax.ShapeDtypeStruct((LANES, IDX_HEADS * n), BF16),
            jax.ShapeDtypeStruct((n, DSA_KV_RANK), BF16),
            jax.ShapeDtypeStruct((n, LANES), BF16),
            jax.ShapeDtypeStruct((IDX_HEADS, n), F32),
        ],
        compiler_params=_params(("arbitrary",)),
        name="dsa_prep",
    )(zd, cq_g.reshape(1, -1), ckv_g.reshape(1, -1), w_uq, qn_g.reshape(1, -1), w_qit)


def _dsa_select_kernel(qit_ref, wht_ref, kix_ref, m_ref, key_ref, jc_ref, *, ksel, idx_bits):
    qb = pl.program_id(1)
    ntile = qb + 1
    ntile_all = m_ref.shape[2]
    rowi = lax.broadcasted_iota(I32, (LANES, Q_BLOCK), 0)
    coli = lax.broadcasted_iota(I32, (LANES, Q_BLOCK), 1)
    q_chunk = (qb * Q_BLOCK + coli) // CHUNK
    pairs = IDX_HEADS // 2
    w_pair = [jnp.concatenate([wht_ref[2 * p:2 * p + 1, :], wht_ref[2 * p + 1:2 * p + 2, :]], axis=1)
              for p in range(pairs)]

    def tile_rows(j):
        return pl.ds(pl.multiple_of(j * LANES, LANES), LANES)

    def admissible(j):
        return ((j * LANES + rowi) // CHUNK) <= q_chunk

    def score_tile(jj, carry):
        for t in range(2):
            j = 2 * jj + t
            kt = kix_ref[tile_rows(j), :]
            sc = None
            for p in range(pairs):
                s2 = jnp.dot(kt, qit_ref[:, 2 * p * Q_BLOCK:2 * (p + 1) * Q_BLOCK],
                             preferred_element_type=F32)
                c2 = w_pair[p] * jnp.maximum(s2, 0.0)
                c = c2[:, :Q_BLOCK] + c2[:, Q_BLOCK:]
                sc = c if sc is None else sc + c
            bits = pltpu.bitcast(sc, I32)
            key = bits ^ ((bits >> 31) & 0x7FFFFFFF)
            key_ref[tile_rows(j), :] = jnp.where(admissible(j), key, INT_MIN)
        return carry

    @pl.when(qb == 0)
    def _():
        key_ref[...] = jnp.full(key_ref.shape, INT_MIN, I32)

    lax.fori_loop(0, (ntile + 1) // 2, score_tile, 0)

    def count(pred_fn):
        def body(jj, acc):
            for t in range(SCAN_TILES):
                j = SCAN_TILES * jj + t
                acc = acc + pred_fn(j, key_ref[tile_rows(j), :]).astype(I32)
            return acc
        trips = (ntile + SCAN_TILES - 1) // SCAN_TILES
        acc = lax.fori_loop(0, trips, body, jnp.zeros((LANES, Q_BLOCK), I32))
        return jnp.sum(acc, axis=0, keepdims=True)

    def value_bit(i, c):
        t_u, cnt_t = c
        cand_u = t_u | jnp.left_shift(jnp.int32(1), 31 - i)
        cand = cand_u ^ INT_MIN
        cnt = count(lambda j, k: k >= cand)
        take = cnt >= ksel
        return jnp.where(take, cand_u, t_u), jnp.where(take, cnt, cnt_t)

    cnt0 = (q_chunk[0:1, :] + 1) * CHUNK
    t_u, cnt_t = lax.fori_loop(0, 32, value_bit, (jnp.zeros((1, Q_BLOCK), I32), cnt0))
    thr = t_u ^ INT_MIN
    tied = jnp.max(jnp.where(cnt_t > ksel, 1.0, 0.0))

    jc_ref[...] = jnp.full(jc_ref.shape, 2 ** 31 - 1, I32)

    @pl.when(tied > 0.0)
    def _():
        need = ksel - count(lambda j, k: k > thr)

        def index_bit(i, j_c):
            cand = j_c | jnp.left_shift(jnp.int32(1), idx_bits - 1 - i)
            cnt = count(lambda j, k: (k == thr) & ((j * LANES + rowi) < cand))
            return jnp.where(cnt < need, cand, j_c)

        j_c = lax.fori_loop(0, idx_bits, index_bit, jnp.zeros((1, Q_BLOCK), I32))
        jc_ref[...] = jnp.broadcast_to(j_c, jc_ref.shape)

    j_c = jc_ref[0:1, :]
    eye = (rowi == coli).astype(BF16)

    group = KV_TILE // LANES

    def write_group(g, carry):
        for t in range(group):
            j = g * group + t
            k = key_ref[tile_rows(j), :]
            sel = (k > thr) | ((k == thr) & ((j * LANES + rowi) <= j_c))
            sel = jnp.where(sel & admissible(j), 1.0, 0.0).astype(BF16)
            sel_t = lax.dot_general(eye, sel, _NT, preferred_element_type=F32)
            m_ref[0, 0, j] = ((sel_t - 1.0) * -NEG_BIG).astype(BF16)
        return carry

    ngroup = (ntile + group - 1) // group
    lax.fori_loop(0, ngroup, write_group, 0)

    def blank_tile(j, carry):
        m_ref[0, 0, j] = jnp.full((Q_BLOCK, LANES), NEG_BIG, BF16)
        return carry

    lax.fori_loop(ngroup * group, ntile_all, blank_tile, 0)


def _dsa_select(qit, wht, kix, batch, seq, ksel):
    nqb = seq // Q_BLOCK
    nkt = seq // LANES
    return pl.pallas_call(
        functools.partial(_dsa_select_kernel, ksel=ksel, idx_bits=int(math.log2(seq))),
        grid=(batch, nqb),
        in_specs=[
            pl.BlockSpec((LANES, IDX_HEADS * Q_BLOCK), lambda b, q: (0, b * nqb + q)),
            pl.BlockSpec((IDX_HEADS, Q_BLOCK), lambda b, q: (0, b * nqb + q)),
            pl.BlockSpec((seq, LANES), lambda b, q: (b, 0)),
        ],
        out_specs=pl.BlockSpec((1, 1, nkt, Q_BLOCK, LANES), lambda b, q: (b, q, 0, 0, 0)),
        out_shape=jax.ShapeDtypeStruct((batch, nqb, nkt, Q_BLOCK, LANES), BF16),
        scratch_shapes=[pltpu.VMEM((seq, Q_BLOCK), I32), pltpu.VMEM((8, Q_BLOCK), I32)],
        compiler_params=_params(("arbitrary", "arbitrary")),
        name="dsa_select",
    )(qit, wht, kix)


def _rel_bucket(rel):
    nb = REL_BUCKETS // 2
    max_exact = nb // 2
    ret = jnp.where(rel > 0, nb, 0)
    n = jnp.abs(rel)
    nf = jnp.maximum(n, 1).astype(F32)
    large = max_exact + (jnp.log(nf / max_exact) / math.log(REL_MAX_DIST / max_exact)
                         * (nb - max_exact)).astype(I32)
    large = jnp.minimum(large, nb - 1)
    return ret + jnp.where(n < max_exact, n, large)


NEAR_TILES = 3


def _dsa_attn_kernel(q_ref, kv_ref, mask_ref, rb_ref, wuv_ref, o_ref,
                     m_ref, l_ref, alpha_ref, acc_ref, corr_ref, s_ref, p_ref, madd_ref):
    b, qb = pl.program_id(0), pl.program_id(1)
    sub = KV_TILE // LANES
    far_bucket = REL_BUCKETS // 2 - 1
    half = DSA_HEADS * Q_BLOCK // 2

    def head_rows(h):
        return slice(h * Q_BLOCK, (h + 1) * Q_BLOCK)

    @pl.when((b == 0) & (qb == 0))
    def _():
        ti = lax.broadcasted_iota(I32, (Q_BLOCK, LANES), 0)
        si = lax.broadcasted_iota(I32, (Q_BLOCK, LANES), 1)
        for oi in range(NEAR_TILES):
            bucket = _rel_bucket((oi - (NEAR_TILES - 1)) * LANES + si - ti)
            for h in range(DSA_HEADS):
                tbl = jnp.zeros((Q_BLOCK, LANES), F32)
                for bk in range(REL_BUCKETS):
                    tbl = jnp.where(bucket == bk, rb_ref[bk, h], tbl)
                corr_ref[oi, head_rows(h), :] = (tbl - rb_ref[far_bucket, h]) * LOG2E

    m_ref[...] = jnp.full_like(m_ref, NEG_BIG)
    l_ref[...] = jnp.zeros_like(l_ref)
    acc_ref[...] = jnp.zeros_like(acc_ref)
    q_all = q_ref[0].reshape(DSA_HEADS * Q_BLOCK, DSA_KV_RANK)

    def key_step(kt, carry):
        kvt = kv_ref[pl.ds(pl.multiple_of(kt * KV_TILE, KV_TILE), KV_TILE), :]
        for part in range(2):
            rs = slice(part * half, (part + 1) * half)
            s_ref[rs, :] = lax.dot_general(q_all[rs], kvt, _NT, preferred_element_type=F32)
        for j in range(sub):
            d = kt * sub + j - qb

            @pl.when((d > -NEAR_TILES) & (d <= 0))
            def _(j=j, d=d):
                s_ref[:, j * LANES:(j + 1) * LANES] += corr_ref[d + NEAR_TILES - 1]

        for j in range(sub):
            madd_ref[:, j * LANES:(j + 1) * LANES] = mask_ref[0, 0, kt * sub + j].astype(F32)
        groups_per_head = Q_BLOCK // SM_ROWS
        for g in range(DSA_HEADS * groups_per_head):
            rs = slice(g * SM_ROWS, (g + 1) * SM_ROWS)
            qg = g % groups_per_head
            s = s_ref[rs, :] + madd_ref[qg * SM_ROWS:(qg + 1) * SM_ROWS, :]
            m_old = m_ref[rs, :]
            m_new = jnp.maximum(m_old, jnp.max(s, axis=1, keepdims=True))
            alpha = jnp.exp2(m_old - m_new)
            p = jnp.exp2(s - jnp.tile(m_new, (1, sub)))
            l_ref[rs, :] = alpha * l_ref[rs, :] + jnp.sum(p, axis=1, keepdims=True)
            alpha_ref[rs, :] = alpha
            p_ref[rs, :] = p.astype(BF16)
            m_ref[rs, :] = m_new
        for part in range(2):
            rs = slice(part * half, (part + 1) * half)
            pv = jnp.dot(p_ref[rs, :], kvt, preferred_element_type=F32)
            acc_ref[rs, :] = jnp.tile(alpha_ref[rs, :], (1, DSA_KV_RANK // LANES)) * acc_ref[rs, :] + pv
        return carry

    lax.fori_loop(0, qb // sub + 1, key_step, 0)

    dv = wuv_ref.shape[2]
    for h in range(DSA_HEADS):
        rs = head_rows(h)
        o = (acc_ref[rs, :] / jnp.tile(l_ref[rs, :], (1, DSA_KV_RANK // LANES))).astype(BF16)
        o_ref[:, h * dv:(h + 1) * dv] = jnp.dot(o, wuv_ref[h], preferred_element_type=F32).astype(BF16)


def _dsa_attn(q, kv, mask, rel_bias, w_uv, batch, seq):
    n = kv.shape[0]
    nqb = seq // Q_BLOCK
    dv = w_uv.shape[2]
    rows = DSA_HEADS * Q_BLOCK
    return pl.pallas_call(
        _dsa_attn_kernel,
        grid=(batch, nqb),
        in_specs=[
            pl.BlockSpec((1, DSA_HEADS, Q_BLOCK, DSA_KV_RANK), lambda b, qb: (b * nqb + qb, 0, 0, 0)),
            pl.BlockSpec((seq, DSA_KV_RANK), lambda b, qb: (b, 0)),
            pl.BlockSpec((1, 1) + mask.shape[2:], lambda b, qb: (b, qb, 0, 0, 0)),
            pl.BlockSpec(memory_space=pltpu.SMEM),
            pl.BlockSpec(w_uv.shape, lambda b, qb: (0, 0, 0)),
        ],
        out_specs=pl.BlockSpec((Q_BLOCK, DSA_HEADS * dv), lambda b, qb: (b * nqb + qb, 0)),
        out_shape=jax.ShapeDtypeStruct((n, DSA_HEADS * dv), BF16),
        scratch_shapes=[
            pltpu.VMEM((rows, LANES), F32),
            pltpu.VMEM((rows, LANES), F32),
            pltpu.VMEM((rows, LANES), F32),
            pltpu.VMEM((rows, DSA_KV_RANK), F32),
            pltpu.VMEM((NEAR_TILES, rows, LANES), F32),
            pltpu.VMEM((rows, KV_TILE), F32),
            pltpu.VMEM((rows, KV_TILE), BF16),
            pltpu.VMEM((Q_BLOCK, KV_TILE), F32),
        ],
        compiler_params=_params(("arbitrary", "arbitrary")),
        name="dsa_attn",
    )(q, kv, mask, rel_bias, w_uv)


def _pad_cols(w, width):
    return jnp.pad(w, ((0, 0), (0, width - w.shape[1])))


def kernel(x, ln_mix_g, ln_ffn_g, w_ffn_gate, w_ffn_up, w_ffn_down, rel_bias, ev_w_in, ev_w_out, sgu_ln_g, sgu_ln_b, sgu_w_s, sgu_b_s, od_w_in, od_w_out, hgrn_lb, hgrn_norm_g, dsa_cq_g, dsa_ckv_g, dsa_w_uq, dsa_qnorm_g, dsa_w_qidx, dsa_w_uv):
    batch, seq, d = x.shape
    n = batch * seq
    depth = ln_mix_g.shape[0]
    ksel = min(TOPK_MAX, seq // 4)
    tm = min(PROJ_ROWS, n)
    xf = x.reshape(n, d)
    wg_all, wu_all, wd_all = (w.astype(BF16) for w in (w_ffn_gate, w_ffn_up, w_ffn_down))
    ev_in_all, od_in_all = ev_w_in.astype(BF16), od_w_in.astype(BF16)
    for layer in range(depth):
        j = layer // 2
        if layer % 2 == 0:
            z = _norm_matmul(xf, ln_mix_g[layer], ev_in_all, j, ev_in_all.shape[2], tm=tm, tn=PROJ_COLS, out_dtype=BF16)
            a1 = _retention(z, batch, seq)
            a2 = _sgu(z, sgu_ln_g[j], sgu_ln_b[j], sgu_w_s[j], sgu_b_s[j], rows=256)
            w_out = ev_w_out[j]
        else:
            w_in = od_w_in[j]
            gw = d // 2
            c = 4 * gw
            c_kidx = c + DSA_Q_RANK + DSA_KV_RANK
            w_dsa = jnp.concatenate([
                w_in[:, c:c_kidx],
                _pad_cols(w_in[:, c_kidx:c_kidx + IDX_DIM], LANES),
                _pad_cols(w_in[:, c_kidx + IDX_DIM:], LANES),
            ], axis=1).astype(BF16)[None]
            z = _norm_matmul(xf, ln_mix_g[layer], od_in_all, j, c, tm=tm, tn=PROJ_COLS, out_dtype=BF16)
            zd = _norm_matmul(xf, ln_mix_g[layer], w_dsa, 0, w_dsa.shape[2], tm=tm, tn=w_dsa.shape[2])
            a1 = _hgrn(z, batch, seq, hgrn_lb, hgrn_norm_g[j], layer)
            w_qit = jnp.pad(dsa_w_qidx[j].T.reshape(IDX_HEADS, IDX_DIM, DSA_Q_RANK),
                            ((0, 0), (0, LANES - IDX_DIM), (0, 0))).reshape(IDX_HEADS * LANES, DSA_Q_RANK)
            q, qit, kv, kix, wht = _dsa_prep(zd, dsa_cq_g[j], dsa_ckv_g[j], dsa_w_uq[j].astype(BF16),
                                             dsa_qnorm_g[j], w_qit.astype(BF16), tm=256)
            mask = _dsa_select(qit, wht, kix, batch, seq, ksel)
            a2 = _dsa_attn(q, kv, mask, rel_bias, dsa_w_uv[j].astype(BF16), batch, seq)
            w_out = od_w_out[j]
        xf = _outproj(a1, a2, w_out.astype(BF16), xf, tm=tm, tn=OUT_COLS)
        xf = _ffn(xf, ln_ffn_g[layer], wg_all, wu_all, wd_all, layer, tm=min(FFN_ROWS, n), tf=FFN_COLS)
    return xf.reshape(batch, seq, d)
```

```python
import functools
import math

import jax
import jax.numpy as jnp
from jax import lax
from jax.experimental import pallas as pl
from jax.experimental.pallas import tpu as pltpu

F32 = jnp.float32
BF16 = jnp.bfloat16
I32 = jnp.int32

EPS = 1e-6
CHUNK = 64
LANES = 128
ROPE_BASE = 10000.0
RET_HEADS = 4
SGU_WINDOW = 128
SGU_GROUPS = 4
HG_HEADS = 8
DSA_HEADS = 8
DSA_Q_RANK = 384
DSA_KV_RANK = 256
IDX_HEADS = 16
IDX_DIM = 64
TOPK_MAX = 256
Q_BLOCK = 128
KV_TILE = 512
SM_ROWS = 64
LOG2E = math.log2(math.e)
REL_BUCKETS = 32
REL_MAX_DIST = 256
NEG_BIG = -1e30
INT_MIN = -(2 ** 31)
SCAN_TILES = 2

RET_BLOCK = 256
RET_HEADS_PER_STEP = 4
HG_BLOCK = 256
HG_FINE = 4
VMEM_LIMIT = 48 * 1024 * 1024
PROJ_ROWS = 1024
PROJ_COLS = 1024
OUT_ROWS = 512
OUT_COLS = 2048
FFN_ROWS = 1024
FFN_VMEM_LIMIT = 58 * 1024 * 1024
SIDE_VMEM_LIMIT = 56 * 1024 * 1024
FFN_COLS = 512

_NT = (((1,), (1,)), ((), ()))
_TN = (((0,), (0,)), ((), ()))


def _params(semantics, vmem_limit=VMEM_LIMIT):
    return pltpu.CompilerParams(dimension_semantics=semantics, vmem_limit_bytes=vmem_limit)


def _silu(x):
    return x * jax.nn.sigmoid(x)


def _rms(x):
    return x * lax.rsqrt(jnp.mean(x * x, axis=-1, keepdims=True) + EPS)


def _norm_matmul_kernel(x_ref, g_ref, w_ref, o_ref, h_ref):
    @pl.when(pl.program_id(1) == 0)
    def _():
        h_ref[...] = (_rms(x_ref[...]) * g_ref[...]).astype(BF16)

    o_ref[...] = jnp.dot(h_ref[...], w_ref[...], preferred_element_type=F32).astype(o_ref.dtype)


def _norm_matmul(x, g, w, layer, nout, *, tm, tn, out_dtype=F32):
    n, d = x.shape
    return pl.pallas_call(
        _norm_matmul_kernel,
        grid=(n // tm, nout // tn),
        in_specs=[
            pl.BlockSpec((tm, d), lambda i, j: (i, 0)),
            pl.BlockSpec((1, d), lambda i, j: (0, 0)),
            pl.BlockSpec((None, d, tn), lambda i, j: (layer, 0, j)),
        ],
        out_specs=pl.BlockSpec((tm, tn), lambda i, j: (i, j)),
        out_shape=jax.ShapeDtypeStruct((n, nout), out_dtype),
        scratch_shapes=[pltpu.VMEM((tm, d), BF16)],
        compiler_params=_params(("arbitrary", "arbitrary")),
        name="norm_matmul",
    )(x, g.reshape(1, d), w)


def _norm_matmul_side_kernel(x_ref, g_ref, w_ref, ws_ref, o_ref, os_ref, h_ref):
    j = pl.program_id(1)
    last = pl.num_programs(1) - 1

    @pl.when(j == 0)
    def _():
        h_ref[...] = (_rms(x_ref[...]) * g_ref[...]).astype(BF16)

    @pl.when(j < last)
    def _():
        o_ref[...] = jnp.dot(h_ref[...], w_ref[...], preferred_element_type=F32).astype(o_ref.dtype)

    @pl.when(j == last)
    def _():
        os_ref[...] = jnp.dot(h_ref[...], ws_ref[...], preferred_element_type=F32)


def _norm_matmul_side(x, g, w, layer, nout, w_side, *, tm, tn, out_dtype):
    n, d = x.shape
    nmain = nout // tn
    ns = w_side.shape[1]
    return pl.pallas_call(
        _norm_matmul_side_kernel,
        grid=(n // tm, nmain + 1),
        in_specs=[
            pl.BlockSpec((tm, d), lambda i, j: (i, 0)),
            pl.BlockSpec((1, d), lambda i, j: (0, 0)),
            pl.BlockSpec((None, d, tn), lambda i, j: (layer, 0, jnp.minimum(j, nmain - 1))),
            pl.BlockSpec((d, ns), lambda i, j: (0, 0)),
        ],
        out_specs=[
            pl.BlockSpec((tm, tn), lambda i, j: (i, jnp.minimum(j, nmain - 1))),
            pl.BlockSpec((tm, ns), lambda i, j: (i, 0)),
        ],
        out_shape=[jax.ShapeDtypeStruct((n, nout), out_dtype), jax.ShapeDtypeStruct((n, ns), F32)],
        scratch_shapes=[pltpu.VMEM((tm, d), BF16)],
        compiler_params=_params(("arbitrary", "arbitrary"), SIDE_VMEM_LIMIT),
        name="norm_matmul_side",
    )(x, g.reshape(1, d), w, w_side)


def _outproj_kernel(a1_ref, a2_ref, w1_ref, w2_ref, r_ref, o_ref):
    acc = jnp.dot(a1_ref[...], w1_ref[...], preferred_element_type=F32)
    acc += jnp.dot(a2_ref[...], w2_ref[...], preferred_element_type=F32)
    o_ref[...] = r_ref[...] + acc


def _outproj(a1, a2, w, res, *, tm, tn):
    n, half = a1.shape
    d = w.shape[1]
    return pl.pallas_call(
        _outproj_kernel,
        grid=(n // tm, d // tn),
        in_specs=[
            pl.BlockSpec((tm, half), lambda i, j: (i, 0)),
            pl.BlockSpec((tm, half), lambda i, j: (i, 0)),
            pl.BlockSpec((half, tn), lambda i, j: (0, j)),
            pl.BlockSpec((half, tn), lambda i, j: (1, j)),
            pl.BlockSpec((tm, tn), lambda i, j: (i, j)),
        ],
        out_specs=pl.BlockSpec((tm, tn), lambda i, j: (i, j)),
        out_shape=jax.ShapeDtypeStruct((n, d), F32),
        compiler_params=_params(("arbitrary", "arbitrary")),
        name="outproj",
    )(a1, a2, w, w, res)


def _ffn_kernel(x_ref, g_ref, wg_ref, wu_ref, wd_ref, o_ref, h_ref):
    @pl.when(pl.program_id(1) == 0)
    def _():
        x = x_ref[...]
        h_ref[...] = (_rms(x) * g_ref[...]).astype(BF16)
        o_ref[...] = x

    h = h_ref[...]
    a = jnp.dot(h, wg_ref[...], preferred_element_type=F32)
    u = jnp.dot(h, wu_ref[...], preferred_element_type=F32)
    act = (_silu(a) * u).astype(BF16)
    o_ref[...] += jnp.dot(act, wd_ref[...], preferred_element_type=F32)


def _ffn(x, g, wg, wu, wd, layer, *, tm, tf):
    n, d = x.shape
    dff = wg.shape[2]
    return pl.pallas_call(
        _ffn_kernel,
        grid=(n // tm, dff // tf),
        in_specs=[
            pl.BlockSpec((tm, d), lambda i, f: (i, 0)),
            pl.BlockSpec((1, d), lambda i, f: (0, 0)),
            pl.BlockSpec((None, d, tf), lambda i, f: (layer, 0, f)),
            pl.BlockSpec((None, d, tf), lambda i, f: (layer, 0, f)),
            pl.BlockSpec((None, tf, d), lambda i, f: (layer, f, 0)),
        ],
        out_specs=pl.BlockSpec((tm, d), lambda i, f: (i, 0)),
        out_shape=jax.ShapeDtypeStruct((n, d), F32),
        scratch_shapes=[pltpu.VMEM((tm, d), BF16)],
        compiler_params=_params(("arbitrary", "arbitrary"), FFN_VMEM_LIMIT),
        name="ffn",
    )(x, g.reshape(1, d), wg, wu, wd)


def _retention_kernel(q_ref, k_ref, v_ref, g_ref, cos_ref, sin_ref, d_ref, xi_ref, zeta_ref,
                      gl_ref, o_ref, state_ref):
    @pl.when(pl.program_id(2) == 0)
    def _():
        state_ref[...] = jnp.zeros_like(state_ref)

    cos = cos_ref[...]
    sin = sin_ref[...]
    half = cos.shape[1]
    dk = 2 * half

    def rot(t):
        t1, t2 = t[:, :half], t[:, half:]
        return jnp.concatenate([t1 * cos - t2 * sin, t1 * sin + t2 * cos], axis=1)

    for i in range(RET_HEADS_PER_STEP):
        cs = slice(i * dk, (i + 1) * dk)
        q = rot(q_ref[:, cs].astype(F32))
        k = rot(k_ref[:, cs].astype(F32)) * (dk ** -0.5)
        qb = q.astype(BF16)
        vb = v_ref[:, cs].astype(BF16)
        scores = lax.dot_general(qb, k.astype(BF16), _NT, preferred_element_type=F32) * d_ref[i]
        intra = jnp.dot(scores.astype(BF16), vb, preferred_element_type=F32)
        state = state_ref[i]
        cross = jnp.dot(qb, state.astype(BF16), preferred_element_type=F32) * xi_ref[i]
        kz = (k * zeta_ref[i]).astype(BF16)
        state_ref[i] = state * gl_ref[i] + lax.dot_general(kz, vb, _TN, preferred_element_type=F32)
        o_ref[:, cs] = (_rms(intra + cross) * _silu(g_ref[:, cs].astype(F32))).astype(BF16)


def _retention_tables(seq, dk):
    blk = RET_BLOCK
    pos = jnp.arange(seq, dtype=F32)
    inv = ROPE_BASE ** (-jnp.arange(0, dk, 2, dtype=F32) / dk)
    ang = pos[:, None] * inv[None, :]
    log_gamma = jnp.log(1.0 - 2.0 ** (-5.0 - jnp.arange(RET_HEADS, dtype=F32)))
    i = jnp.arange(blk)
    same = (i[:, None] // CHUNK) == (i[None, :] // CHUNK)
    earlier = (i[None, :] // CHUNK) < (i[:, None] // CHUNK)
    diff = (i[:, None] - i[None, :]).astype(F32)
    dist = jnp.where(same, jnp.abs(diff), diff)
    decay = jnp.where((same | earlier)[None], jnp.exp(log_gamma[:, None, None] * dist[None]), 0.0)
    p = jnp.arange(blk, dtype=F32)
    wide = (RET_HEADS, blk, dk)
    xi = jnp.broadcast_to(jnp.exp(log_gamma[:, None] * (p + 1.0))[:, :, None], wide)
    zeta = jnp.broadcast_to(jnp.exp(log_gamma[:, None] * (blk - 1.0 - p))[:, :, None], wide)
    g_blk = jnp.broadcast_to(jnp.exp(log_gamma * blk)[:, None, None], (RET_HEADS, 1, dk))
    return jnp.cos(ang), jnp.sin(ang), decay, xi, zeta, g_blk


def _retention(z, batch, seq):
    n = z.shape[0]
    gw = z.shape[1] // 6
    dk = gw // RET_HEADS
    blk = RET_BLOCK
    nblk = seq // blk
    cos, sin, decay, xi, zeta, g_blk = _retention_tables(seq, dk)

    hp = RET_HEADS_PER_STEP
    groups = RET_HEADS // hp

    def zspec(part):
        return pl.BlockSpec((blk, hp * dk), lambda b, h, c: (b * nblk + c, part * groups + h))

    def hspec(rows, cols):
        return pl.BlockSpec((hp, rows, cols), lambda b, h, c: (h, 0, 0))

    return pl.pallas_call(
        _retention_kernel,
        grid=(batch, groups, nblk),
        in_specs=[
            zspec(0), zspec(1), zspec(2), zspec(3),
            pl.BlockSpec((blk, dk // 2), lambda b, h, c: (c, 0)),
            pl.BlockSpec((blk, dk // 2), lambda b, h, c: (c, 0)),
            hspec(blk, blk), hspec(blk, dk), hspec(blk, dk), hspec(1, dk),
        ],
        out_specs=pl.BlockSpec((blk, hp * dk), lambda b, h, c: (b * nblk + c, h)),
        out_shape=jax.ShapeDtypeStruct((n, gw), BF16),
        scratch_shapes=[pltpu.VMEM((hp, dk, dk), F32)],
        compiler_params=_params(("arbitrary", "arbitrary", "arbitrary")),
        name="retention",
    )(z, z, z, z, cos, sin, decay, xi, zeta, g_blk)


def _gelu(x):
    return 0.5 * x * (1.0 + lax.erf(x * math.sqrt(0.5)))


def _sgu_kernel(u_ref, v_ref, lng_ref, lnb_ref, w_ref, b_ref, o_ref):
    rows, width = v_ref.shape
    dg = width // SGU_GROUPS
    v = _gelu(v_ref[...].astype(F32))
    mu = jnp.mean(v, axis=-1, keepdims=True)
    var = jnp.mean(jnp.square(v - mu), axis=-1, keepdims=True)
    vn = ((v - mu) * lax.rsqrt(var + EPS) * lng_ref[...] + lnb_ref[...]).astype(BF16)
    u = _gelu(u_ref[...].astype(F32))
    ri = lax.broadcasted_iota(I32, (SGU_WINDOW, SGU_WINDOW), 0) // CHUNK
    ci = lax.broadcasted_iota(I32, (SGU_WINDOW, SGU_WINDOW), 1) // CHUNK
    allowed = ci <= ri
    for g in range(SGU_GROUPS):
        wg = jnp.where(allowed, w_ref[g], 0.0).astype(BF16)
        bias = b_ref[g]
        for w in range(rows // SGU_WINDOW):
            rs = slice(w * SGU_WINDOW, (w + 1) * SGU_WINDOW)
            cs = slice(g * dg, (g + 1) * dg)
            mixed = jnp.dot(wg, vn[rs, cs], preferred_element_type=F32) + bias
            o_ref[rs, cs] = (u[rs, cs] * mixed).astype(BF16)


def _sgu(z, ln_g, ln_b, w_s, b_s, *, rows):
    n = z.shape[0]
    gw = z.shape[1] // 6
    return pl.pallas_call(
        _sgu_kernel,
        grid=(n // rows,),
        in_specs=[
            pl.BlockSpec((rows, gw), lambda i: (i, 4)),
            pl.BlockSpec((rows, gw), lambda i: (i, 5)),
            pl.BlockSpec((1, gw), lambda i: (0, 0)),
            pl.BlockSpec((1, gw), lambda i: (0, 0)),
            pl.BlockSpec((SGU_GROUPS, SGU_WINDOW, SGU_WINDOW), lambda i: (0, 0, 0)),
            pl.BlockSpec((SGU_GROUPS, SGU_WINDOW, 1), lambda i: (0, 0, 0)),
        ],
        out_specs=pl.BlockSpec((rows, gw), lambda i: (i, 0)),
        out_shape=jax.ShapeDtypeStruct((n, gw), BF16),
        compiler_params=_params(("arbitrary",)),
        name="sgu",
    )(z, z, ln_g.reshape(1, gw), ln_b.reshape(1, gw), w_s, b_s.reshape(SGU_GROUPS, SGU_WINDOW, 1))


def _hgrn_kernel(q_ref, f_ref, i_ref, g_ref, lb_ref, ng_ref, tri_ref, o_ref,
                 st_ref, ksh_ref, bsh_ref, vsh_ref, *, layer):
    rows, dk = q_ref.shape

    @pl.when(pl.program_id(2) == 0)
    def _():
        st_ref[...] = jnp.zeros_like(st_ref)

    lbp = lb_ref[...]
    e = jnp.exp(lbp - jnp.max(lbp, axis=0, keepdims=True))
    soft = e / jnp.sum(e, axis=0, keepdims=True)
    lb = jnp.sum(soft[1:layer + 1], axis=0, keepdims=True)

    f = lb + (1.0 - lb) * jax.nn.sigmoid(f_ref[...].astype(F32))
    lf = jnp.log(f)
    kk = 1.0 - f
    qa = _silu(q_ref[...].astype(F32))

    tri = tri_ref[...]
    bcum = None
    rest = lf
    for _ in range(3):
        term = rest.astype(BF16)
        part = jnp.dot(tri, term, preferred_element_type=F32)
        bcum = part if bcum is None else bcum + part
        rest = rest - term.astype(F32)

    row = lax.broadcasted_iota(I32, (rows, dk), 0)
    ti = lax.broadcasted_iota(I32, (rows, rows), 0)
    si = lax.broadcasted_iota(I32, (rows, rows), 1)
    attn = jnp.zeros((rows, rows), F32)
    hs = rows // 2
    while hs >= HG_FINE:
        bs = 2 * hs
        parts = [jnp.broadcast_to(bcum[b * bs + hs - 1:b * bs + hs, :], (bs, dk))
                 for b in range(rows // bs)]
        anchor = parts[0] if len(parts) == 1 else jnp.concatenate(parts, axis=0)
        upper = (row & (bs - 1)) >= hs
        fac = jnp.exp(-jnp.abs(bcum - anchor))
        qt = jnp.where(upper, qa * fac, 0.0)
        kt = jnp.where(upper, 0.0, kk * fac)
        a = lax.dot_general(qt.astype(BF16), kt.astype(BF16), _NT, preferred_element_type=F32)
        if bs < rows:
            a = jnp.where((ti & -bs) == (si & -bs), a, 0.0)
        attn = attn + a
        hs //= 2

    v = i_ref[...].astype(F32)
    vb = v.astype(BF16)
    near = qa * kk
    intra = jnp.sum(near, axis=1, keepdims=True) * v
    pad = jnp.zeros((HG_FINE, dk), F32)
    for buf, val in ((ksh_ref, kk), (bsh_ref, bcum), (vsh_ref, v)):
        buf[0:HG_FINE, :] = pad
        buf[HG_FINE:, :] = val
    for delta in range(1, HG_FINE):
        back = slice(HG_FINE - delta, HG_FINE - delta + rows)
        prod = qa * ksh_ref[back, :] * jnp.exp(jnp.minimum(bcum - bsh_ref[back, :], 0.0))
        prod = jnp.where((row & (HG_FINE - 1)) >= delta, prod, 0.0)
        intra = intra + jnp.sum(prod, axis=1, keepdims=True) * vsh_ref[back, :]
    intra = intra + jnp.dot(attn.astype(BF16), vb, preferred_element_type=F32)
    st = st_ref[...]
    cross = lax.dot_general((qa * jnp.exp(bcum)).astype(BF16), st.astype(BF16), _NT,
                            preferred_element_type=F32)
    blast = bcum[rows - 1:rows, :]
    kb = (kk * jnp.exp(blast - bcum)).astype(BF16)
    st_ref[...] = st * jnp.exp(blast) + lax.dot_general(vb, kb, _TN, preferred_element_type=F32)
    g = g_ref[...].astype(F32)
    o_ref[...] = (_rms(intra + cross) * ng_ref[...] * _silu(g)).astype(BF16)


def _hgrn(z, batch, seq, lb_raw, norm_g, layer):
    n = z.shape[0]
    gw = z.shape[1] // 4
    dk = gw // HG_HEADS
    blk = HG_BLOCK
    nblk = seq // blk
    depth = lb_raw.shape[0]

    def zspec(part):
        return pl.BlockSpec((blk, dk), lambda b, h, c: (b * nblk + c, part * HG_HEADS + h))

    return pl.pallas_call(
        functools.partial(_hgrn_kernel, layer=layer),
        grid=(batch, HG_HEADS, nblk),
        in_specs=[
            zspec(0), zspec(1), zspec(2), zspec(3),
            pl.BlockSpec((depth, dk), lambda b, h, c: (0, h)),
            pl.BlockSpec((1, dk), lambda b, h, c: (0, h)),
            pl.BlockSpec((blk, blk), lambda b, h, c: (0, 0)),
        ],
        out_specs=pl.BlockSpec((blk, dk), lambda b, h, c: (b * nblk + c, h)),
        out_shape=jax.ShapeDtypeStruct((n, gw), BF16),
        scratch_shapes=[pltpu.VMEM((dk, dk), F32)] + [pltpu.VMEM((blk + HG_FINE, dk), F32)] * 3,
        compiler_params=_params(("arbitrary", "arbitrary", "arbitrary")),
        name="hgrn2",
    )(z, z, z, z, lb_raw, norm_g.reshape(1, gw), jnp.tril(jnp.ones((blk, blk), BF16)))


def _dsa_prep_kernel(zd_ref, cqg_ref, ckvg_ref, wuq_ref, qng_ref, wqit_ref,
                     q_ref, qit_ref, kv_ref, kix_ref, wht_ref):
    zd = zd_ref[...]
    cq = (_rms(zd[:, :DSA_Q_RANK]) * cqg_ref[...]).astype(BF16)
    qf = jnp.dot(cq, wuq_ref[...], preferred_element_type=F32)
    for i in range(q_ref.shape[0]):
        rs = slice(i * Q_BLOCK, (i + 1) * Q_BLOCK)
        for h in range(DSA_HEADS):
            cs = slice(h * DSA_KV_RANK, (h + 1) * DSA_KV_RANK)
            q_ref[i, h] = (_rms(qf[rs, cs]) * qng_ref[...] * (DSA_KV_RANK ** -0.5 * LOG2E)).astype(BF16)
    qit = lax.dot_general(wqit_ref[...], cq, _NT, preferred_element_type=F32)
    qit = (qit * (IDX_DIM ** -0.5)).astype(BF16)
    for i in range(q_ref.shape[0]):
        for h in range(IDX_HEADS):
            c = (i * IDX_HEADS + h) * Q_BLOCK
            qit_ref[:, c:c + Q_BLOCK] = qit[h * LANES:(h + 1) * LANES, i * Q_BLOCK:(i + 1) * Q_BLOCK]
    c0 = DSA_Q_RANK
    c1 = c0 + DSA_KV_RANK
    kv_ref[...] = (_rms(zd[:, c0:c1]) * ckvg_ref[...]).astype(BF16)
    kix_ref[...] = zd[:, c1:c1 + LANES].astype(BF16)
    wht = jnp.transpose(zd[:, c1 + LANES:c1 + 2 * LANES] * (IDX_HEADS ** -0.5))
    wht_ref[...] = wht[:IDX_HEADS, :]


def _dsa_prep(zd, cq_g, ckv_g, w_uq, qn_g, w_qit, *, tm):
    n, wd = zd.shape
    dq = w_uq.shape[1]
    dqi = w_qit.shape[0]
    full = lambda i: (0, 0)
    rows = lambda i: (i, 0)
    cols = lambda i: (0, i)
    return pl.pallas_call(
        _dsa_prep_kernel,
        grid=(n // tm,),
        in_specs=[
            pl.BlockSpec((tm, wd), rows),
            pl.BlockSpec((1, DSA_Q_RANK), full),
            pl.BlockSpec((1, DSA_KV_RANK), full),
            pl.BlockSpec((DSA_Q_RANK, dq), full),
            pl.BlockSpec((1, DSA_KV_RANK), full),
            pl.BlockSpec((dqi, DSA_Q_RANK), full),
        ],
        out_specs=[
            pl.BlockSpec((tm // Q_BLOCK, DSA_HEADS, Q_BLOCK, DSA_KV_RANK), lambda i: (i, 0, 0, 0)),
            pl.BlockSpec((LANES, IDX_HEADS * tm), cols),
            pl.BlockSpec((tm, DSA_KV_RANK), rows),
            pl.BlockSpec((tm, LANES), rows),
            pl.BlockSpec((IDX_HEADS, tm), cols),
        ],
        out_shape=[
            jax.ShapeDtypeStruct((n // Q_BLOCK, DSA_HEADS, Q_BLOCK, DSA_KV_RANK), BF16),
            jax.ShapeDtypeStruct((LANES, IDX_HEADS * n), BF16),
            jax.ShapeDtypeStruct((n, DSA_KV_RANK), BF16),
            jax.ShapeDtypeStruct((n, LANES), BF16),
            jax.ShapeDtypeStruct((IDX_HEADS, n), F32),
        ],
        compiler_params=_params(("arbitrary",)),
        name="dsa_prep",
    )(zd, cq_g.reshape(1, -1), ckv_g.reshape(1, -1), w_uq, qn_g.reshape(1, -1), w_qit)


def _dsa_select_kernel(qit_ref, wht_ref, kix_ref, m_ref, key_ref, jc_ref, *, ksel, idx_bits):
    qb = pl.program_id(1)
    ntile = qb + 1
    ntile_all = m_ref.shape[2]
    rowi = lax.broadcasted_iota(I32, (LANES, Q_BLOCK), 0)
    coli = lax.broadcasted_iota(I32, (LANES, Q_BLOCK), 1)
    q_chunk = (qb * Q_BLOCK + coli) // CHUNK
    pairs = IDX_HEADS // 2
    w_pair = [jnp.concatenate([wht_ref[2 * p:2 * p + 1, :], wht_ref[2 * p + 1:2 * p + 2, :]], axis=1)
              for p in range(pairs)]

    def tile_rows(j):
        return pl.ds(pl.multiple_of(j * LANES, LANES), LANES)

    def admissible(j):
        return ((j * LANES + rowi) // CHUNK) <= q_chunk

    def score_tile(jj, carry):
        for t in range(2):
            j = 2 * jj + t
            kt = kix_ref[tile_rows(j), :]
            sc = None
            for p in range(pairs):
                s2 = jnp.dot(kt, qit_ref[:, 2 * p * Q_BLOCK:2 * (p + 1) * Q_BLOCK],
                             preferred_element_type=F32)
                c2 = w_pair[p] * jnp.maximum(s2, 0.0)
                c = c2[:, :Q_BLOCK] + c2[:, Q_BLOCK:]
                sc = c if sc is None else sc + c
            bits = pltpu.bitcast(sc, I32)
            key = bits ^ ((bits >> 31) & 0x7FFFFFFF)
            key_ref[tile_rows(j), :] = jnp.where(admissible(j), key, INT_MIN)
        return carry

    @pl.when(qb == 0)
    def _():
        key_ref[...] = jnp.full(key_ref.shape, INT_MIN, I32)

    lax.fori_loop(0, (ntile + 1) // 2, score_tile, 0)

    def count(pred_fn):
        def body(jj, acc):
            for t in range(SCAN_TILES):
                j = SCAN_TILES * jj + t
                acc = acc + pred_fn(j, key_ref[tile_rows(j), :]).astype(I32)
            return acc
        trips = (ntile + SCAN_TILES - 1) // SCAN_TILES
        acc = lax.fori_loop(0, trips, body, jnp.zeros((LANES, Q_BLOCK), I32))
        return jnp.sum(acc, axis=0, keepdims=True)

    def value_bit(i, c):
        t_u, cnt_t = c
        cand_u = t_u | jnp.left_shift(jnp.int32(1), 31 - i)
        cand = cand_u ^ INT_MIN
        cnt = count(lambda j, k: k >= cand)
        take = cnt >= ksel
        return jnp.where(take, cand_u, t_u), jnp.where(take, cnt, cnt_t)

    cnt0 = (q_chunk[0:1, :] + 1) * CHUNK
    t_u, cnt_t = lax.fori_loop(0, 32, value_bit, (jnp.zeros((1, Q_BLOCK), I32), cnt0))
    thr = t_u ^ INT_MIN
    tied = jnp.max(jnp.where(cnt_t > ksel, 1.0, 0.0))

    jc_ref[...] = jnp.full(jc_ref.shape, 2 ** 31 - 1, I32)

    @pl.when(tied > 0.0)
    def _():
        need = ksel - count(lambda j, k: k > thr)

        def index_bit(i, j_c):
            cand = j_c | jnp.left_shift(jnp.int32(1), idx_bits - 1 - i)
            cnt = count(lambda j, k: (k == thr) & ((j * LANES + rowi) < cand))
            return jnp.where(cnt < need, cand, j_c)

        j_c = lax.fori_loop(0, idx_bits, index_bit, jnp.zeros((1, Q_BLOCK), I32))
        jc_ref[...] = jnp.broadcast_to(j_c, jc_ref.shape)

    j_c = jc_ref[0:1, :]
    eye = (rowi == coli).astype(BF16)

    group = KV_TILE // LANES

    def write_group(g, carry):
        for t in range(group):
            j = g * group + t
            k = key_ref[tile_rows(j), :]
            sel = (k > thr) | ((k == thr) & ((j * LANES + rowi) <= j_c))
            sel = jnp.where(sel & admissible(j), 1.0, 0.0).astype(BF16)
            sel_t = lax.dot_general(eye, sel, _NT, preferred_element_type=F32)
            m_ref[0, 0, j] = ((sel_t - 1.0) * -NEG_BIG).astype(BF16)
        return carry

    ngroup = (ntile + group - 1) // group
    lax.fori_loop(0, ngroup, write_group, 0)

    def blank_tile(j, carry):
        m_ref[0, 0, j] = jnp.full((Q_BLOCK, LANES), NEG_BIG, BF16)
        return carry

    lax.fori_loop(ngroup * group, ntile_all, blank_tile, 0)


def _dsa_select(qit, wht, kix, batch, seq, ksel):
    nqb = seq // Q_BLOCK
    nkt = seq // LANES
    return pl.pallas_call(
        functools.partial(_dsa_select_kernel, ksel=ksel, idx_bits=int(math.log2(seq))),
        grid=(batch, nqb),
        in_specs=[
            pl.BlockSpec((LANES, IDX_HEADS * Q_BLOCK), lambda b, q: (0, b * nqb + q)),
            pl.BlockSpec((IDX_HEADS, Q_BLOCK), lambda b, q: (0, b * nqb + q)),
            pl.BlockSpec((seq, LANES), lambda b, q: (b, 0)),
        ],
        out_specs=pl.BlockSpec((1, 1, nkt, Q_BLOCK, LANES), lambda b, q: (b, q, 0, 0, 0)),
        out_shape=jax.ShapeDtypeStruct((batch, nqb, nkt, Q_BLOCK, LANES), BF16),
        scratch_shapes=[pltpu.VMEM((seq, Q_BLOCK), I32), pltpu.VMEM((8, Q_BLOCK), I32)],
        compiler_params=_params(("arbitrary", "arbitrary")),
        name="dsa_select",
    )(qit, wht, kix)


def _rel_bucket(rel):
    nb = REL_BUCKETS // 2
    max_exact = nb // 2
    ret = jnp.where(rel > 0, nb, 0)
    n = jnp.abs(rel)
    nf = jnp.maximum(n, 1).astype(F32)
    large = max_exact + (jnp.log(nf / max_exact) / math.log(REL_MAX_DIST / max_exact)
                         * (nb - max_exact)).astype(I32)
    large = jnp.minimum(large, nb - 1)
    return ret + jnp.where(n < max_exact, n, large)


NEAR_TILES = 3


def _dsa_attn_kernel(q_ref, kv_ref, mask_ref, rb_ref, wuv_ref, o_ref,
                     m_ref, l_ref, alpha_ref, acc_ref, corr_ref, s_ref, p_ref, madd_ref):
    b, qb = pl.program_id(0), pl.program_id(1)
    sub = KV_TILE // LANES
    far_bucket = REL_BUCKETS // 2 - 1
    half = DSA_HEADS * Q_BLOCK // 2

    def head_rows(h):
        return slice(h * Q_BLOCK, (h + 1) * Q_BLOCK)

    @pl.when((b == 0) & (qb == 0))
    def _():
        ti = lax.broadcasted_iota(I32, (Q_BLOCK, LANES), 0)
        si = lax.broadcasted_iota(I32, (Q_BLOCK, LANES), 1)
        for oi in range(NEAR_TILES):
            bucket = _rel_bucket((oi - (NEAR_TILES - 1)) * LANES + si - ti)
            for h in range(DSA_HEADS):
                tbl = jnp.zeros((Q_BLOCK, LANES), F32)
                for bk in range(REL_BUCKETS):
                    tbl = jnp.where(bucket == bk, rb_ref[bk, h], tbl)
                corr_ref[oi, head_rows(h), :] = (tbl - rb_ref[far_bucket, h]) * LOG2E

    m_ref[...] = jnp.full_like(m_ref, NEG_BIG)
    l_ref[...] = jnp.zeros_like(l_ref)
    acc_ref[...] = jnp.zeros_like(acc_ref)
    q_all = q_ref[0].reshape(DSA_HEADS * Q_BLOCK, DSA_KV_RANK)

    def key_step(kt, carry):
        kvt = kv_ref[pl.ds(pl.multiple_of(kt * KV_TILE, KV_TILE), KV_TILE), :]
        for part in range(2):
            rs = slice(part * half, (part + 1) * half)
            s_ref[rs, :] = lax.dot_general(q_all[rs], kvt, _NT, preferred_element_type=F32)
        for j in range(sub):
            d = kt * sub + j - qb

            @pl.when((d > -NEAR_TILES) & (d <= 0))
            def _(j=j, d=d):
                s_ref[:, j * LANES:(j + 1) * LANES] += corr_ref[d + NEAR_TILES - 1]

        for j in range(sub):
            madd_ref[:, j * LANES:(j + 1) * LANES] = mask_ref[0, 0, kt * sub + j].astype(F32)
        groups_per_head = Q_BLOCK // SM_ROWS
        for g in range(DSA_HEADS * groups_per_head):
            rs = slice(g * SM_ROWS, (g + 1) * SM_ROWS)
            qg = g % groups_per_head
            s = s_ref[rs, :] + madd_ref[qg * SM_ROWS:(qg + 1) * SM_ROWS, :]
            m_old = m_ref[rs, :]
            m_new = jnp.maximum(m_old, jnp.max(s, axis=1, keepdims=True))
            alpha = jnp.exp2(m_old - m_new)
            p = jnp.exp2(s - jnp.tile(m_new, (1, sub)))
            l_ref[rs, :] = alpha * l_ref[rs, :] + jnp.sum(p, axis=1, keepdims=True)
            alpha_ref[rs, :] = alpha
            p_ref[rs, :] = p.astype(BF16)
            m_ref[rs, :] = m_new
        for part in range(2):
            rs = slice(part * half, (part + 1) * half)
            pv = jnp.dot(p_ref[rs, :], kvt, preferred_element_type=F32)
            acc_ref[rs, :] = jnp.tile(alpha_ref[rs, :], (1, DSA_KV_RANK // LANES)) * acc_ref[rs, :] + pv
        return carry

    lax.fori_loop(0, qb // sub + 1, key_step, 0)

    dv = wuv_ref.shape[2]
    for h in range(DSA_HEADS):
        rs = head_rows(h)
        o = (acc_ref[rs, :] / jnp.tile(l_ref[rs, :], (1, DSA_KV_RANK // LANES))).astype(BF16)
        o_ref[:, h * dv:(h + 1) * dv] = jnp.dot(o, wuv_ref[h], preferred_element_type=F32).astype(BF16)


def _dsa_attn(q, kv, mask, rel_bias, w_uv, batch, seq):
    n = kv.shape[0]
    nqb = seq // Q_BLOCK
    dv = w_uv.shape[2]
    rows = DSA_HEADS * Q_BLOCK
    return pl.pallas_call(
        _dsa_attn_kernel,
        grid=(batch, nqb),
        in_specs=[
            pl.BlockSpec((1, DSA_HEADS, Q_BLOCK, DSA_KV_RANK), lambda b, qb: (b * nqb + qb, 0, 0, 0)),
            pl.BlockSpec((seq, DSA_KV_RANK), lambda b, qb: (b, 0)),
            pl.BlockSpec((1, 1) + mask.shape[2:], lambda b, qb: (b, qb, 0, 0, 0)),
            pl.BlockSpec(memory_space=pltpu.SMEM),
            pl.BlockSpec(w_uv.shape, lambda b, qb: (0, 0, 0)),
        ],
        out_specs=pl.BlockSpec((Q_BLOCK, DSA_HEADS * dv), lambda b, qb: (b * nqb + qb, 0)),
        out_shape=jax.ShapeDtypeStruct((n, DSA_HEADS * dv), BF16),
        scratch_shapes=[
            pltpu.VMEM((rows, LANES), F32),
            pltpu.VMEM((rows, LANES), F32),
            pltpu.VMEM((rows, LANES), F32),
            pltpu.VMEM((rows, DSA_KV_RANK), F32),
            pltpu.VMEM((NEAR_TILES, rows, LANES), F32),
            pltpu.VMEM((rows, KV_TILE), F32),
            pltpu.VMEM((rows, KV_TILE), BF16),
            pltpu.VMEM((Q_BLOCK, KV_TILE), F32),
        ],
        compiler_params=_params(("arbitrary", "arbitrary")),
        name="dsa_attn",
    )(q, kv, mask, rel_bias, w_uv)


def _pad_cols(w, width):
    return jnp.pad(w, ((0, 0), (0, width - w.shape[1])))


def kernel(x, ln_mix_g, ln_ffn_g, w_ffn_gate, w_ffn_up, w_ffn_down, rel_bias, ev_w_in, ev_w_out, sgu_ln_g, sgu_ln_b, sgu_w_s, sgu_b_s, od_w_in, od_w_out, hgrn_lb, hgrn_norm_g, dsa_cq_g, dsa_ckv_g, dsa_w_uq, dsa_qnorm_g, dsa_w_qidx, dsa_w_uv):
    batch, seq, d = x.shape
    n = batch * seq
    depth = ln_mix_g.shape[0]
    ksel = min(TOPK_MAX, seq // 4)
    tm = min(PROJ_ROWS, n)
    xf = x.reshape(n, d)
    wg_all, wu_all, wd_all = (w.astype(BF16) for w in (w_ffn_gate, w_ffn_up, w_ffn_down))
    ev_in_all, od_in_all = ev_w_in.astype(BF16), od_w_in.astype(BF16)
    for layer in range(depth):
        j = layer // 2
        if layer % 2 == 0:
            z = _norm_matmul(xf, ln_mix_g[layer], ev_in_all, j, ev_in_all.shape[2], tm=tm, tn=PROJ_COLS, out_dtype=BF16)
            a1 = _retention(z, batch, seq)
            a2 = _sgu(z, sgu_ln_g[j], sgu_ln_b[j], sgu_w_s[j], sgu_b_s[j], rows=256)
            w_out = ev_w_out[j]
        else:
            w_in = od_w_in[j]
            gw = d // 2
            c = 4 * gw
            c_kidx = c + DSA_Q_RANK + DSA_KV_RANK
            w_dsa = jnp.concatenate([
                w_in[:, c:c_kidx],
                _pad_cols(w_in[:, c_kidx:c_kidx + IDX_DIM], LANES),
                _pad_cols(w_in[:, c_kidx + IDX_DIM:], LANES),
            ], axis=1).astype(BF16)
            z, zd = _norm_matmul_side(xf, ln_mix_g[layer], od_in_all, j, c, w_dsa,
                                      tm=tm, tn=min(PROJ_COLS, c), out_dtype=BF16)
            a1 = _hgrn(z, batch, seq, hgrn_lb, hgrn_norm_g[j], layer)
            w_qit = jnp.pad(dsa_w_qidx[j].T.reshape(IDX_HEADS, IDX_DIM, DSA_Q_RANK),
                            ((0, 0), (0, LANES - IDX_DIM), (0, 0))).reshape(IDX_HEADS * LANES, DSA_Q_RANK)
            q, qit, kv, kix, wht = _dsa_prep(zd, dsa_cq_g[j], dsa_ckv_g[j], dsa_w_uq[j].astype(BF16),
                                             dsa_qnorm_g[j], w_qit.astype(BF16), tm=256)
            mask = _dsa_select(qit, wht, kix, batch, seq, ksel)
            a2 = _dsa_attn(q, kv, mask, rel_bias, dsa_w_uv[j].astype(BF16), batch, seq)
            w_out = od_w_out[j]
        xf = _outproj(a1, a2, w_out.astype(BF16), xf, tm=min(OUT_ROWS, n), tn=OUT_COLS)
        xf = _ffn(xf, ln_ffn_g[layer], wg_all, wu_all, wd_all, layer, tm=min(FFN_ROWS, n), tf=FFN_COLS)
    return xf.reshape(batch, seq, d)
```

```python
import functools
import math

import jax
import jax.numpy as jnp
from jax import lax
from jax.experimental import pallas as pl
from jax.experimental.pallas import tpu as pltpu

F32 = jnp.float32
BF16 = jnp.bfloat16
I32 = jnp.int32

EPS = 1e-6
CHUNK = 64
LANES = 128
ROPE_BASE = 10000.0
RET_HEADS = 4
SGU_WINDOW = 128
SGU_GROUPS = 4
HG_HEADS = 8
DSA_HEADS = 8
DSA_Q_RANK = 384
DSA_KV_RANK = 256
IDX_HEADS = 16
IDX_DIM = 64
TOPK_MAX = 256
Q_BLOCK = 128
KV_TILE = 512
SM_ROWS = 64
LOG2E = math.log2(math.e)
REL_BUCKETS = 32
REL_MAX_DIST = 256
NEG_BIG = -1e30
INT_MIN = -(2 ** 31)
SCAN_TILES = 2

RET_BLOCK = 256
RET_HEADS_PER_STEP = 4
HG_BLOCK = 256
HG_FINE = 4
VMEM_LIMIT = 48 * 1024 * 1024
PROJ_ROWS = 1024
PROJ_COLS = 1024
OUT_ROWS = 512
OUT_COLS = 2048
FFN_ROWS = 1024
FFN_VMEM_LIMIT = 58 * 1024 * 1024
SIDE_VMEM_LIMIT = 56 * 1024 * 1024
FFN_COLS = 512

_NT = (((1,), (1,)), ((), ()))
_TN = (((0,), (0,)), ((), ()))


def _params(semantics, vmem_limit=VMEM_LIMIT):
    return pltpu.CompilerParams(dimension_semantics=semantics, vmem_limit_bytes=vmem_limit)


def _silu(x):
    return x * jax.nn.sigmoid(x)


def _rms(x):
    return x * lax.rsqrt(jnp.mean(x * x, axis=-1, keepdims=True) + EPS)


def _norm_matmul_kernel(x_ref, g_ref, w_ref, o_ref, h_ref):
    @pl.when(pl.program_id(1) == 0)
    def _():
        h_ref[...] = (_rms(x_ref[...]) * g_ref[...]).astype(BF16)

    o_ref[...] = jnp.dot(h_ref[...], w_ref[...], preferred_element_type=F32).astype(o_ref.dtype)


def _norm_matmul(x, g, w, layer, nout, *, tm, tn, out_dtype=F32):
    n, d = x.shape
    return pl.pallas_call(
        _norm_matmul_kernel,
        grid=(n // tm, nout // tn),
        in_specs=[
            pl.BlockSpec((tm, d), lambda i, j: (i, 0)),
            pl.BlockSpec((1, d), lambda i, j: (0, 0)),
            pl.BlockSpec((None, d, tn), lambda i, j: (layer, 0, j)),
        ],
        out_specs=pl.BlockSpec((tm, tn), lambda i, j: (i, j)),
        out_shape=jax.ShapeDtypeStruct((n, nout), out_dtype),
        scratch_shapes=[pltpu.VMEM((tm, d), BF16)],
        compiler_params=_params(("arbitrary", "arbitrary")),
        name="norm_matmul",
    )(x, g.reshape(1, d), w)


def _outproj_kernel(a1_ref, a2_ref, w1_ref, w2_ref, r_ref, o_ref):
    acc = jnp.dot(a1_ref[...], w1_ref[...], preferred_element_type=F32)
    acc += jnp.dot(a2_ref[...], w2_ref[...], preferred_element_type=F32)
    o_ref[...] = r_ref[...] + acc


def _outproj(a1, a2, w, res, *, tm, tn):
    n, half = a1.shape
    d = w.shape[1]
    return pl.pallas_call(
        _outproj_kernel,
        grid=(n // tm, d // tn),
        in_specs=[
            pl.BlockSpec((tm, half), lambda i, j: (i, 0)),
            pl.BlockSpec((tm, half), lambda i, j: (i, 0)),
            pl.BlockSpec((half, tn), lambda i, j: (0, j)),
            pl.BlockSpec((half, tn), lambda i, j: (1, j)),
            pl.BlockSpec((tm, tn), lambda i, j: (i, j)),
        ],
        out_specs=pl.BlockSpec((tm, tn), lambda i, j: (i, j)),
        out_shape=jax.ShapeDtypeStruct((n, d), F32),
        compiler_params=_params(("arbitrary", "arbitrary")),
        name="outproj",
    )(a1, a2, w, w, res)


def _ffn_kernel(x_ref, g_ref, wg_ref, wu_ref, wd_ref, o_ref, h_ref):
    @pl.when(pl.program_id(1) == 0)
    def _():
        x = x_ref[...]
        h_ref[...] = (_rms(x) * g_ref[...]).astype(BF16)
        o_ref[...] = x

    h = h_ref[...]
    a = jnp.dot(h, wg_ref[...], preferred_element_type=F32)
    u = jnp.dot(h, wu_ref[...], preferred_element_type=F32)
    act = (_silu(a) * u).astype(BF16)
    o_ref[...] += jnp.dot(act, wd_ref[...], preferred_element_type=F32)


def _ffn(x, g, wg, wu, wd, layer, *, tm, tf):
    n, d = x.shape
    dff = wg.shape[2]
    return pl.pallas_call(
        _ffn_kernel,
        grid=(n // tm, dff // tf),
        in_specs=[
            pl.BlockSpec((tm, d), lambda i, f: (i, 0)),
            pl.BlockSpec((1, d), lambda i, f: (0, 0)),
            pl.BlockSpec((None, d, tf), lambda i, f: (layer, 0, f)),
            pl.BlockSpec((None, d, tf), lambda i, f: (layer, 0, f)),
            pl.BlockSpec((None, tf, d), lambda i, f: (layer, f, 0)),
        ],
        out_specs=pl.BlockSpec((tm, d), lambda i, f: (i, 0)),
        out_shape=jax.ShapeDtypeStruct((n, d), F32),
        scratch_shapes=[pltpu.VMEM((tm, d), BF16)],
        compiler_params=_params(("arbitrary", "arbitrary"), FFN_VMEM_LIMIT),
        name="ffn",
    )(x, g.reshape(1, d), wg, wu, wd)


def _retention_kernel(q_ref, k_ref, v_ref, g_ref, cos_ref, sin_ref, d_ref, xi_ref, zeta_ref,
                      gl_ref, o_ref, state_ref):
    @pl.when(pl.program_id(2) == 0)
    def _():
        state_ref[...] = jnp.zeros_like(state_ref)

    cos = cos_ref[...]
    sin = sin_ref[...]
    half = cos.shape[1]
    dk = 2 * half

    def rot(t):
        t1, t2 = t[:, :half], t[:, half:]
        return jnp.concatenate([t1 * cos - t2 * sin, t1 * sin + t2 * cos], axis=1)

    for i in range(RET_HEADS_PER_STEP):
        cs = slice(i * dk, (i + 1) * dk)
        q = rot(q_ref[:, cs].astype(F32))
        k = rot(k_ref[:, cs].astype(F32)) * (dk ** -0.5)
        qb = q.astype(BF16)
        vb = v_ref[:, cs].astype(BF16)
        scores = lax.dot_general(qb, k.astype(BF16), _NT, preferred_element_type=F32) * d_ref[i]
        intra = jnp.dot(scores.astype(BF16), vb, preferred_element_type=F32)
        state = state_ref[i]
        cross = jnp.dot(qb, state.astype(BF16), preferred_element_type=F32) * xi_ref[i]
        kz = (k * zeta_ref[i]).astype(BF16)
        state_ref[i] = state * gl_ref[i] + lax.dot_general(kz, vb, _TN, preferred_element_type=F32)
        o_ref[:, cs] = (_rms(intra + cross) * _silu(g_ref[:, cs].astype(F32))).astype(BF16)


def _retention_tables(seq, dk):
    blk = RET_BLOCK
    pos = jnp.arange(seq, dtype=F32)
    inv = ROPE_BASE ** (-jnp.arange(0, dk, 2, dtype=F32) / dk)
    ang = pos[:, None] * inv[None, :]
    log_gamma = jnp.log(1.0 - 2.0 ** (-5.0 - jnp.arange(RET_HEADS, dtype=F32)))
    i = jnp.arange(blk)
    same = (i[:, None] // CHUNK) == (i[None, :] // CHUNK)
    earlier = (i[None, :] // CHUNK) < (i[:, None] // CHUNK)
    diff = (i[:, None] - i[None, :]).astype(F32)
    dist = jnp.where(same, jnp.abs(diff), diff)
    decay = jnp.where((same | earlier)[None], jnp.exp(log_gamma[:, None, None] * dist[None]), 0.0)
    p = jnp.arange(blk, dtype=F32)
    wide = (RET_HEADS, blk, dk)
    xi = jnp.broadcast_to(jnp.exp(log_gamma[:, None] * (p + 1.0))[:, :, None], wide)
    zeta = jnp.broadcast_to(jnp.exp(log_gamma[:, None] * (blk - 1.0 - p))[:, :, None], wide)
    g_blk = jnp.broadcast_to(jnp.exp(log_gamma * blk)[:, None, None], (RET_HEADS, 1, dk))
    return jnp.cos(ang), jnp.sin(ang), decay, xi, zeta, g_blk


def _retention(z, batch, seq):
    n = z.shape[0]
    gw = z.shape[1] // 6
    dk = gw // RET_HEADS
    blk = RET_BLOCK
    nblk = seq // blk
    cos, sin, decay, xi, zeta, g_blk = _retention_tables(seq, dk)

    hp = RET_HEADS_PER_STEP
    groups = RET_HEADS // hp

    def zspec(part):
        return pl.BlockSpec((blk, hp * dk), lambda b, h, c: (b * nblk + c, part * groups + h))

    def hspec(rows, cols):
        return pl.BlockSpec((hp, rows, cols), lambda b, h, c: (h, 0, 0))

    return pl.pallas_call(
        _retention_kernel,
        grid=(batch, groups, nblk),
        in_specs=[
            zspec(0), zspec(1), zspec(2), zspec(3),
            pl.BlockSpec((blk, dk // 2), lambda b, h, c: (c, 0)),
            pl.BlockSpec((blk, dk // 2), lambda b, h, c: (c, 0)),
            hspec(blk, blk), hspec(blk, dk), hspec(blk, dk), hspec(1, dk),
        ],
        out_specs=pl.BlockSpec((blk, hp * dk), lambda b, h, c: (b * nblk + c, h)),
        out_shape=jax.ShapeDtypeStruct((n, gw), BF16),
        scratch_shapes=[pltpu.VMEM((hp, dk, dk), F32)],
        compiler_params=_params(("arbitrary", "arbitrary", "arbitrary")),
        name="retention",
    )(z, z, z, z, cos, sin, decay, xi, zeta, g_blk)


def _gelu(x):
    return 0.5 * x * (1.0 + lax.erf(x * math.sqrt(0.5)))


def _sgu_kernel(u_ref, v_ref, lng_ref, lnb_ref, w_ref, b_ref, o_ref):
    rows, width = v_ref.shape
    dg = width // SGU_GROUPS
    v = _gelu(v_ref[...].astype(F32))
    mu = jnp.mean(v, axis=-1, keepdims=True)
    var = jnp.mean(jnp.square(v - mu), axis=-1, keepdims=True)
    vn = ((v - mu) * lax.rsqrt(var + EPS) * lng_ref[...] + lnb_ref[...]).astype(BF16)
    u = _gelu(u_ref[...].astype(F32))
    ri = lax.broadcasted_iota(I32, (SGU_WINDOW, SGU_WINDOW), 0) // CHUNK
    ci = lax.broadcasted_iota(I32, (SGU_WINDOW, SGU_WINDOW), 1) // CHUNK
    allowed = ci <= ri
    for g in range(SGU_GROUPS):
        wg = jnp.where(allowed, w_ref[g], 0.0).astype(BF16)
        bias = b_ref[g]
        for w in range(rows // SGU_WINDOW):
            rs = slice(w * SGU_WINDOW, (w + 1) * SGU_WINDOW)
            cs = slice(g * dg, (g + 1) * dg)
            mixed = jnp.dot(wg, vn[rs, cs], preferred_element_type=F32) + bias
            o_ref[rs, cs] = (u[rs, cs] * mixed).astype(BF16)


def _sgu(z, ln_g, ln_b, w_s, b_s, *, rows):
    n = z.shape[0]
    gw = z.shape[1] // 6
    return pl.pallas_call(
        _sgu_kernel,
        grid=(n // rows,),
        in_specs=[
            pl.BlockSpec((rows, gw), lambda i: (i, 4)),
            pl.BlockSpec((rows, gw), lambda i: (i, 5)),
            pl.BlockSpec((1, gw), lambda i: (0, 0)),
            pl.BlockSpec((1, gw), lambda i: (0, 0)),
            pl.BlockSpec((SGU_GROUPS, SGU_WINDOW, SGU_WINDOW), lambda i: (0, 0, 0)),
            pl.BlockSpec((SGU_GROUPS, SGU_WINDOW, 1), lambda i: (0, 0, 0)),
        ],
        out_specs=pl.BlockSpec((rows, gw), lambda i: (i, 0)),
        out_shape=jax.ShapeDtypeStruct((n, gw), BF16),
        compiler_params=_params(("arbitrary",)),
        name="sgu",
    )(z, z, ln_g.reshape(1, gw), ln_b.reshape(1, gw), w_s, b_s.reshape(SGU_GROUPS, SGU_WINDOW, 1))


def _hgrn_head(q, f_logits, v, g, lb, ng, tri, st_ref, sh_ref):
    rows, dk = q.shape
    f = lb + (1.0 - lb) * jax.nn.sigmoid(f_logits)
    lf = jnp.log(f)
    kk = 1.0 - f
    qa = _silu(q)

    bcum = None
    rest = lf
    for _ in range(3):
        term = rest.astype(BF16)
        part = jnp.dot(tri, term, preferred_element_type=F32)
        bcum = part if bcum is None else bcum + part
        rest = rest - term.astype(F32)

    row = lax.broadcasted_iota(I32, (rows, dk), 0)
    ti = lax.broadcasted_iota(I32, (rows, rows), 0)
    si = lax.broadcasted_iota(I32, (rows, rows), 1)
    attn = jnp.zeros((rows, rows), F32)
    hs = rows // 2
    while hs >= HG_FINE:
        bs = 2 * hs
        parts = [jnp.broadcast_to(bcum[b * bs + hs - 1:b * bs + hs, :], (bs, dk))
                 for b in range(rows // bs)]
        anchor = parts[0] if len(parts) == 1 else jnp.concatenate(parts, axis=0)
        upper = (row & (bs - 1)) >= hs
        fac = jnp.exp(-jnp.abs(bcum - anchor))
        qt = jnp.where(upper, qa * fac, 0.0)
        kt = jnp.where(upper, 0.0, kk * fac)
        a = lax.dot_general(qt.astype(BF16), kt.astype(BF16), _NT, preferred_element_type=F32)
        if bs < rows:
            a = jnp.where((ti & -bs) == (si & -bs), a, 0.0)
        attn = attn + a
        hs //= 2

    vb = v.astype(BF16)
    near = qa * kk
    intra = jnp.sum(near, axis=1, keepdims=True) * v
    pad = jnp.zeros((HG_FINE, dk), F32)
    for idx, val in enumerate((kk, bcum, v)):
        sh_ref[idx, 0:HG_FINE, :] = pad
        sh_ref[idx, HG_FINE:, :] = val
    for delta in range(1, HG_FINE):
        back = slice(HG_FINE - delta, HG_FINE - delta + rows)
        prod = qa * sh_ref[0, back, :] * jnp.exp(jnp.minimum(bcum - sh_ref[1, back, :], 0.0))
        prod = jnp.where((row & (HG_FINE - 1)) >= delta, prod, 0.0)
        intra = intra + jnp.sum(prod, axis=1, keepdims=True) * sh_ref[2, back, :]
    intra = intra + jnp.dot(attn.astype(BF16), vb, preferred_element_type=F32)
    st = st_ref[...]
    cross = lax.dot_general((qa * jnp.exp(bcum)).astype(BF16), st.astype(BF16), _NT,
                            preferred_element_type=F32)
    blast = bcum[rows - 1:rows, :]
    kb = (kk * jnp.exp(blast - bcum)).astype(BF16)
    st_ref[...] = st * jnp.exp(blast) + lax.dot_general(vb, kb, _TN, preferred_element_type=F32)
    return _rms(intra + cross) * ng * _silu(g)


def _hgrn_lower_bound(lb_raw, layer):
    e = jnp.exp(lb_raw - jnp.max(lb_raw, axis=0, keepdims=True))
    soft = e / jnp.sum(e, axis=0, keepdims=True)
    return jnp.sum(soft[1:layer + 1], axis=0, keepdims=True)


def _inproj_hgrn_kernel(x_ref, g_ref, w_ref, ws_ref, lb_ref, ng_ref, tri_ref, o_ref, zd_ref,
                        z_ref, st_ref, sh_ref, *, layer):
    c = pl.program_id(1)
    gw = o_ref.shape[1]
    dk = gw // HG_HEADS

    @pl.when((pl.program_id(0) == 0) & (c == 0))
    def _():
        z_ref[...] = jnp.zeros_like(z_ref)

    @pl.when(c <= 1)
    def _():
        st_ref[...] = jnp.zeros_like(st_ref)

    h = (_rms(x_ref[...]) * g_ref[...]).astype(BF16)
    z_ref[c % 2] = jnp.dot(h, w_ref[...], preferred_element_type=F32)
    zd_ref[...] = jnp.dot(h, ws_ref[...], preferred_element_type=F32)

    prev = (c + 1) % 2
    lb = _hgrn_lower_bound(lb_ref[...], layer)
    tri = tri_ref[...]
    for hd in range(HG_HEADS):
        cols = [slice(part * gw + hd * dk, part * gw + (hd + 1) * dk) for part in range(4)]
        hs = slice(hd * dk, (hd + 1) * dk)
        out = _hgrn_head(z_ref[prev, :, cols[0]], z_ref[prev, :, cols[1]], z_ref[prev, :, cols[2]],
                         z_ref[prev, :, cols[3]], lb[:, hs], ng_ref[:, hs], tri, st_ref.at[hd], sh_ref.at[hd])
        o_ref[:, hs] = out.astype(BF16)


def _inproj_hgrn(x, g, w, layer_idx, w_side, lb_raw, norm_g, batch, seq, layer):
    n, d = x.shape
    gw = norm_g.shape[0]
    dk = gw // HG_HEADS
    blk = HG_BLOCK
    nblk = seq // blk
    ns = w_side.shape[1]
    depth = lb_raw.shape[0]
    const = lambda b, c: (0, 0)
    return pl.pallas_call(
        functools.partial(_inproj_hgrn_kernel, layer=layer),
        grid=(batch, nblk + 1),
        in_specs=[
            pl.BlockSpec((blk, d), lambda b, c: (b * nblk + jnp.minimum(c, nblk - 1), 0)),
            pl.BlockSpec((1, d), const),
            pl.BlockSpec((None, d, 4 * gw), lambda b, c: (layer_idx, 0, 0), pipeline_mode=pl.Buffered(1)),
            pl.BlockSpec((d, ns), const, pipeline_mode=pl.Buffered(1)),
            pl.BlockSpec((depth, gw), const),
            pl.BlockSpec((1, gw), const),
            pl.BlockSpec((blk, blk), const),
        ],
        out_specs=[
            pl.BlockSpec((blk, gw), lambda b, c: (b * nblk + jnp.maximum(c - 1, 0), 0)),
            pl.BlockSpec((blk, ns), lambda b, c: (b * nblk + jnp.minimum(c, nblk - 1), 0)),
        ],
        out_shape=[jax.ShapeDtypeStruct((n, gw), BF16), jax.ShapeDtypeStruct((n, ns), F32)],
        scratch_shapes=[
            pltpu.VMEM((2, blk, 4 * gw), F32),
            pltpu.VMEM((HG_HEADS, dk, dk), F32),
            pltpu.VMEM((HG_HEADS, 3, blk + HG_FINE, dk), F32),
        ],
        compiler_params=_params(("arbitrary", "arbitrary"), SIDE_VMEM_LIMIT),
        name="inproj_hgrn",
    )(x, g.reshape(1, d), w, w_side, lb_raw, norm_g.reshape(1, gw), jnp.tril(jnp.ones((blk, blk), BF16)))


def _dsa_prep_kernel(zd_ref, cqg_ref, ckvg_ref, wuq_ref, qng_ref, wqit_ref,
                     q_ref, qit_ref, kv_ref, kix_ref, wht_ref):
    zd = zd_ref[...]
    cq = (_rms(zd[:, :DSA_Q_RANK]) * cqg_ref[...]).astype(BF16)
    qf = jnp.dot(cq, wuq_ref[...], preferred_element_type=F32)
    for i in range(q_ref.shape[0]):
        rs = slice(i * Q_BLOCK, (i + 1) * Q_BLOCK)
        for h in range(DSA_HEADS):
            cs = slice(h * DSA_KV_RANK, (h + 1) * DSA_KV_RANK)
            q_ref[i, h] = (_rms(qf[rs, cs]) * qng_ref[...] * (DSA_KV_RANK ** -0.5 * LOG2E)).astype(BF16)
    qit = lax.dot_general(wqit_ref[...], cq, _NT, preferred_element_type=F32)
    qit = (qit * (IDX_DIM ** -0.5)).astype(BF16)
    for i in range(q_ref.shape[0]):
        for h in range(IDX_HEADS):
            c = (i * IDX_HEADS + h) * Q_BLOCK
            qit_ref[:, c:c + Q_BLOCK] = qit[h * LANES:(h + 1) * LANES, i * Q_BLOCK:(i + 1) * Q_BLOCK]
    c0 = DSA_Q_RANK
    c1 = c0 + DSA_KV_RANK
    kv_ref[...] = (_rms(zd[:, c0:c1]) * ckvg_ref[...]).astype(BF16)
    kix_ref[...] = zd[:, c1:c1 + LANES].astype(BF16)
    wht = jnp.transpose(zd[:, c1 + LANES:c1 + 2 * LANES] * (IDX_HEADS ** -0.5))
    wht_ref[...] = wht[:IDX_HEADS, :]


def _dsa_prep(zd, cq_g, ckv_g, w_uq, qn_g, w_qit, *, tm):
    n, wd = zd.shape
    dq = w_uq.shape[1]
    dqi = w_qit.shape[0]
    full = lambda i: (0, 0)
    rows = lambda i: (i, 0)
    cols = lambda i: (0, i)
    return pl.pallas_call(
        _dsa_prep_kernel,
        grid=(n // tm,),
        in_specs=[
            pl.BlockSpec((tm, wd), rows),
            pl.BlockSpec((1, DSA_Q_RANK), full),
            pl.BlockSpec((1, DSA_KV_RANK), full),
            pl.BlockSpec((DSA_Q_RANK, dq), full),
            pl.BlockSpec((1, DSA_KV_RANK), full),
            pl.BlockSpec((dqi, DSA_Q_RANK), full),
        ],
        out_specs=[
            pl.BlockSpec((tm // Q_BLOCK, DSA_HEADS, Q_BLOCK, DSA_KV_RANK), lambda i: (i, 0, 0, 0)),
            pl.BlockSpec((LANES, IDX_HEADS * tm), cols),
            pl.BlockSpec((tm, DSA_KV_RANK), rows),
            pl.BlockSpec((tm, LANES), rows),
            pl.BlockSpec((IDX_HEADS, tm), cols),
        ],
        out_shape=[
            jax.ShapeDtypeStruct((n // Q_BLOCK, DSA_HEADS, Q_BLOCK, DSA_KV_RANK), BF16),
            jax.ShapeDtypeStruct((LANES, IDX_HEADS * n), BF16),
            jax.ShapeDtypeStruct((n, DSA_KV_RANK), BF16),
            jax.ShapeDtypeStruct((n, LANES), BF16),
            jax.ShapeDtypeStruct((IDX_HEADS, n), F32),
        ],
        compiler_params=_params(("arbitrary",)),
        name="dsa_prep",
    )(zd, cq_g.reshape(1, -1), ckv_g.reshape(1, -1), w_uq, qn_g.reshape(1, -1), w_qit)


def _dsa_select_kernel(qit_ref, wht_ref, kix_ref, m_ref, key_ref, jc_ref, *, ksel, idx_bits):
    qb = pl.program_id(1)
    ntile = qb + 1
    ntile_all = m_ref.shape[2]
    rowi = lax.broadcasted_iota(I32, (LANES, Q_BLOCK), 0)
    coli = lax.broadcasted_iota(I32, (LANES, Q_BLOCK), 1)
    q_chunk = (qb * Q_BLOCK + coli) // CHUNK
    pairs = IDX_HEADS // 2
    w_pair = [jnp.concatenate([wht_ref[2 * p:2 * p + 1, :], wht_ref[2 * p + 1:2 * p + 2, :]], axis=1)
              for p in range(pairs)]

    def tile_rows(j):
        return pl.ds(pl.multiple_of(j * LANES, LANES), LANES)

    def admissible(j):
        return ((j * LANES + rowi) // CHUNK) <= q_chunk

    def score_tile(jj, carry):
        for t in range(2):
            j = 2 * jj + t
            kt = kix_ref[tile_rows(j), :]
            sc = None
            for p in range(pairs):
                s2 = jnp.dot(kt, qit_ref[:, 2 * p * Q_BLOCK:2 * (p + 1) * Q_BLOCK],
                             preferred_element_type=F32)
                c2 = w_pair[p] * jnp.maximum(s2, 0.0)
                c = c2[:, :Q_BLOCK] + c2[:, Q_BLOCK:]
                sc = c if sc is None else sc + c
            bits = pltpu.bitcast(sc, I32)
            key = bits ^ ((bits >> 31) & 0x7FFFFFFF)
            key_ref[tile_rows(j), :] = jnp.where(admissible(j), key, INT_MIN)
        return carry

    @pl.when(qb == 0)
    def _():
        key_ref[...] = jnp.full(key_ref.shape, INT_MIN, I32)

    lax.fori_loop(0, (ntile + 1) // 2, score_tile, 0)

    def count(pred_fn):
        def body(jj, acc):
            for t in range(SCAN_TILES):
                j = SCAN_TILES * jj + t
                acc = acc + pred_fn(j, key_ref[tile_rows(j), :]).astype(I32)
            return acc
        trips = (ntile + SCAN_TILES - 1) // SCAN_TILES
        acc = lax.fori_loop(0, trips, body, jnp.zeros((LANES, Q_BLOCK), I32))
        return jnp.sum(acc, axis=0, keepdims=True)

    def value_bit(i, c):
        t_u, cnt_t = c
        cand_u = t_u | jnp.left_shift(jnp.int32(1), 31 - i)
        cand = cand_u ^ INT_MIN
        cnt = count(lambda j, k: k >= cand)
        take = cnt >= ksel
        return jnp.where(take, cand_u, t_u), jnp.where(take, cnt, cnt_t)

    cnt0 = (q_chunk[0:1, :] + 1) * CHUNK
    t_u, cnt_t = lax.fori_loop(0, 32, value_bit, (jnp.zeros((1, Q_BLOCK), I32), cnt0))
    thr = t_u ^ INT_MIN
    tied = jnp.max(jnp.where(cnt_t > ksel, 1.0, 0.0))

    jc_ref[...] = jnp.full(jc_ref.shape, 2 ** 31 - 1, I32)

    @pl.when(tied > 0.0)
    def _():
        need = ksel - count(lambda j, k: k > thr)

        def index_bit(i, j_c):
            cand = j_c | jnp.left_shift(jnp.int32(1), idx_bits - 1 - i)
            cnt = count(lambda j, k: (k == thr) & ((j * LANES + rowi) < cand))
            return jnp.where(cnt < need, cand, j_c)

        j_c = lax.fori_loop(0, idx_bits, index_bit, jnp.zeros((1, Q_BLOCK), I32))
        jc_ref[...] = jnp.broadcast_to(j_c, jc_ref.shape)

    j_c = jc_ref[0:1, :]
    eye = (rowi == coli).astype(BF16)

    group = KV_TILE // LANES

    def write_group(g, carry):
        for t in range(group):
            j = g * group + t
            k = key_ref[tile_rows(j), :]
            sel = (k > thr) | ((k == thr) & ((j * LANES + rowi) <= j_c))
            sel = jnp.where(sel & admissible(j), 1.0, 0.0).astype(BF16)
            sel_t = lax.dot_general(eye, sel, _NT, preferred_element_type=F32)
            m_ref[0, 0, j] = ((sel_t - 1.0) * -NEG_BIG).astype(BF16)
        return carry

    ngroup = (ntile + group - 1) // group
    lax.fori_loop(0, ngroup, write_group, 0)

    def blank_tile(j, carry):
        m_ref[0, 0, j] = jnp.full((Q_BLOCK, LANES), NEG_BIG, BF16)
        return carry

    lax.fori_loop(ngroup * group, ntile_all, blank_tile, 0)


def _dsa_select(qit, wht, kix, batch, seq, ksel):
    nqb = seq // Q_BLOCK
    nkt = seq // LANES
    return pl.pallas_call(
        functools.partial(_dsa_select_kernel, ksel=ksel, idx_bits=int(math.log2(seq))),
        grid=(batch, nqb),
        in_specs=[
            pl.BlockSpec((LANES, IDX_HEADS * Q_BLOCK), lambda b, q: (0, b * nqb + q)),
            pl.BlockSpec((IDX_HEADS, Q_BLOCK), lambda b, q: (0, b * nqb + q)),
            pl.BlockSpec((seq, LANES), lambda b, q: (b, 0)),
        ],
        out_specs=pl.BlockSpec((1, 1, nkt, Q_BLOCK, LANES), lambda b, q: (b, q, 0, 0, 0)),
        out_shape=jax.ShapeDtypeStruct((batch, nqb, nkt, Q_BLOCK, LANES), BF16),
        scratch_shapes=[pltpu.VMEM((seq, Q_BLOCK), I32), pltpu.VMEM((8, Q_BLOCK), I32)],
        compiler_params=_params(("arbitrary", "arbitrary")),
        name="dsa_select",
    )(qit, wht, kix)


def _rel_bucket(rel):
    nb = REL_BUCKETS // 2
    max_exact = nb // 2
    ret = jnp.where(rel > 0, nb, 0)
    n = jnp.abs(rel)
    nf = jnp.maximum(n, 1).astype(F32)
    large = max_exact + (jnp.log(nf / max_exact) / math.log(REL_MAX_DIST / max_exact)
                         * (nb - max_exact)).astype(I32)
    large = jnp.minimum(large, nb - 1)
    return ret + jnp.where(n < max_exact, n, large)


NEAR_TILES = 3


def _dsa_attn_kernel(q_ref, kv_ref, mask_ref, rb_ref, wuv_ref, o_ref,
                     m_ref, l_ref, alpha_ref, acc_ref, corr_ref, s_ref, p_ref, madd_ref):
    b, qb = pl.program_id(0), pl.program_id(1)
    sub = KV_TILE // LANES
    far_bucket = REL_BUCKETS // 2 - 1
    half = DSA_HEADS * Q_BLOCK // 2

    def head_rows(h):
        return slice(h * Q_BLOCK, (h + 1) * Q_BLOCK)

    @pl.when((b == 0) & (qb == 0))
    def _():
        ti = lax.broadcasted_iota(I32, (Q_BLOCK, LANES), 0)
        si = lax.broadcasted_iota(I32, (Q_BLOCK, LANES), 1)
        for oi in range(NEAR_TILES):
            bucket = _rel_bucket((oi - (NEAR_TILES - 1)) * LANES + si - ti)
            for h in range(DSA_HEADS):
                tbl = jnp.zeros((Q_BLOCK, LANES), F32)
                for bk in range(REL_BUCKETS):
                    tbl = jnp.where(bucket == bk, rb_ref[bk, h], tbl)
                corr_ref[oi, head_rows(h), :] = (tbl - rb_ref[far_bucket, h]) * LOG2E

    m_ref[...] = jnp.full_like(m_ref, NEG_BIG)
    l_ref[...] = jnp.zeros_like(l_ref)
    acc_ref[...] = jnp.zeros_like(acc_ref)
    q_all = q_ref[0].reshape(DSA_HEADS * Q_BLOCK, DSA_KV_RANK)

    def key_step(kt, carry):
        kvt = kv_ref[pl.ds(pl.multiple_of(kt * KV_TILE, KV_TILE), KV_TILE), :]
        for part in range(2):
            rs = slice(part * half, (part + 1) * half)
            s_ref[rs, :] = lax.dot_general(q_all[rs], kvt, _NT, preferred_element_type=F32)
        for j in range(sub):
            d = kt * sub + j - qb

            @pl.when((d > -NEAR_TILES) & (d <= 0))
            def _(j=j, d=d):
                s_ref[:, j * LANES:(j + 1) * LANES] += corr_ref[d + NEAR_TILES - 1]

        for j in range(sub):
            madd_ref[:, j * LANES:(j + 1) * LANES] = mask_ref[0, 0, kt * sub + j].astype(F32)
        groups_per_head = Q_BLOCK // SM_ROWS
        for g in range(DSA_HEADS * groups_per_head):
            rs = slice(g * SM_ROWS, (g + 1) * SM_ROWS)
            qg = g % groups_per_head
            s = s_ref[rs, :] + madd_ref[qg * SM_ROWS:(qg + 1) * SM_ROWS, :]
            m_old = m_ref[rs, :]
            m_new = jnp.maximum(m_old, jnp.max(s, axis=1, keepdims=True))
            alpha = jnp.exp2(m_old - m_new)
            p = jnp.exp2(s - jnp.tile(m_new, (1, sub)))
            l_ref[rs, :] = alpha * l_ref[rs, :] + jnp.sum(p, axis=1, keepdims=True)
            alpha_ref[rs, :] = alpha
            p_ref[rs, :] = p.astype(BF16)
            m_ref[rs, :] = m_new
        for part in range(2):
            rs = slice(part * half, (part + 1) * half)
            pv = jnp.dot(p_ref[rs, :], kvt, preferred_element_type=F32)
            acc_ref[rs, :] = jnp.tile(alpha_ref[rs, :], (1, DSA_KV_RANK // LANES)) * acc_ref[rs, :] + pv
        return carry

    lax.fori_loop(0, qb // sub + 1, key_step, 0)

    dv = wuv_ref.shape[2]
    for h in range(DSA_HEADS):
        rs = head_rows(h)
        o = (acc_ref[rs, :] / jnp.tile(l_ref[rs, :], (1, DSA_KV_RANK // LANES))).astype(BF16)
        o_ref[:, h * dv:(h + 1) * dv] = jnp.dot(o, wuv_ref[h], preferred_element_type=F32).astype(BF16)


def _dsa_attn(q, kv, mask, rel_bias, w_uv, batch, seq):
    n = kv.shape[0]
    nqb = seq // Q_BLOCK
    dv = w_uv.shape[2]
    rows = DSA_HEADS * Q_BLOCK
    return pl.pallas_call(
        _dsa_attn_kernel,
        grid=(batch, nqb),
        in_specs=[
            pl.BlockSpec((1, DSA_HEADS, Q_BLOCK, DSA_KV_RANK), lambda b, qb: (b * nqb + qb, 0, 0, 0)),
            pl.BlockSpec((seq, DSA_KV_RANK), lambda b, qb: (b, 0)),
            pl.BlockSpec((1, 1) + mask.shape[2:], lambda b, qb: (b, qb, 0, 0, 0)),
            pl.BlockSpec(memory_space=pltpu.SMEM),
            pl.BlockSpec(w_uv.shape, lambda b, qb: (0, 0, 0)),
        ],
        out_specs=pl.BlockSpec((Q_BLOCK, DSA_HEADS * dv), lambda b, qb: (b * nqb + qb, 0)),
        out_shape=jax.ShapeDtypeStruct((n, DSA_HEADS * dv), BF16),
        scratch_shapes=[
            pltpu.VMEM((rows, LANES), F32),
            pltpu.VMEM((rows, LANES), F32),
            pltpu.VMEM((rows, LANES), F32),
            pltpu.VMEM((rows, DSA_KV_RANK), F32),
            pltpu.VMEM((NEAR_TILES, rows, LANES), F32),
            pltpu.VMEM((rows, KV_TILE), F32),
            pltpu.VMEM((rows, KV_TILE), BF16),
            pltpu.VMEM((Q_BLOCK, KV_TILE), F32),
        ],
        compiler_params=_params(("arbitrary", "arbitrary")),
        name="dsa_attn",
    )(q, kv, mask, rel_bias, w_uv)


def _pad_cols(w, width):
    return jnp.pad(w, ((0, 0), (0, width - w.shape[1])))


def kernel(x, ln_mix_g, ln_ffn_g, w_ffn_gate, w_ffn_up, w_ffn_down, rel_bias, ev_w_in, ev_w_out, sgu_ln_g, sgu_ln_b, sgu_w_s, sgu_b_s, od_w_in, od_w_out, hgrn_lb, hgrn_norm_g, dsa_cq_g, dsa_ckv_g, dsa_w_uq, dsa_qnorm_g, dsa_w_qidx, dsa_w_uv):
    batch, seq, d = x.shape
    n = batch * seq
    depth = ln_mix_g.shape[0]
    ksel = min(TOPK_MAX, seq // 4)
    tm = min(PROJ_ROWS, n)
    xf = x.reshape(n, d)
    wg_all, wu_all, wd_all = (w.astype(BF16) for w in (w_ffn_gate, w_ffn_up, w_ffn_down))
    ev_in_all, od_in_all = ev_w_in.astype(BF16), od_w_in.astype(BF16)
    for layer in range(depth):
        j = layer // 2
        if layer % 2 == 0:
            z = _norm_matmul(xf, ln_mix_g[layer], ev_in_all, j, ev_in_all.shape[2], tm=tm, tn=PROJ_COLS, out_dtype=BF16)
            a1 = _retention(z, batch, seq)
            a2 = _sgu(z, sgu_ln_g[j], sgu_ln_b[j], sgu_w_s[j], sgu_b_s[j], rows=256)
            w_out = ev_w_out[j]
        else:
            w_in = od_w_in[j]
            gw = d // 2
            c = 4 * gw
            c_kidx = c + DSA_Q_RANK + DSA_KV_RANK
            w_dsa = jnp.concatenate([
                w_in[:, c:c_kidx],
                _pad_cols(w_in[:, c_kidx:c_kidx + IDX_DIM], LANES),
                _pad_cols(w_in[:, c_kidx + IDX_DIM:], LANES),
            ], axis=1).astype(BF16)
            a1, zd = _inproj_hgrn(xf, ln_mix_g[layer], od_in_all, j, w_dsa, hgrn_lb, hgrn_norm_g[j],
                                  batch, seq, layer)
            w_qit = jnp.pad(dsa_w_qidx[j].T.reshape(IDX_HEADS, IDX_DIM, DSA_Q_RANK),
                            ((0, 0), (0, LANES - IDX_DIM), (0, 0))).reshape(IDX_HEADS * LANES, DSA_Q_RANK)
            q, qit, kv, kix, wht = _dsa_prep(zd, dsa_cq_g[j], dsa_ckv_g[j], dsa_w_uq[j].astype(BF16),
                                             dsa_qnorm_g[j], w_qit.astype(BF16), tm=256)
            mask = _dsa_select(qit, wht, kix, batch, seq, ksel)
            a2 = _dsa_attn(q, kv, mask, rel_bias, dsa_w_uv[j].astype(BF16), batch, seq)
            w_out = od_w_out[j]
        xf = _outproj(a1, a2, w_out.astype(BF16), xf, tm=min(OUT_ROWS, n), tn=OUT_COLS)
        xf = _ffn(xf, ln_ffn_g[layer], wg_all, wu_all, wd_all, layer, tm=min(FFN_ROWS, n), tf=FFN_COLS)
    return xf.reshape(batch, seq, d)
```

```python
import functools
import math

import jax
import jax.numpy as jnp
from jax import lax
from jax.experimental import pallas as pl
from jax.experimental.pallas import tpu as pltpu

F32 = jnp.float32
BF16 = jnp.bfloat16
I32 = jnp.int32

EPS = 1e-6
CHUNK = 64
LANES = 128
ROPE_BASE = 10000.0
RET_HEADS = 4
SGU_WINDOW = 128
SGU_GROUPS = 4
HG_HEADS = 8
DSA_HEADS = 8
DSA_Q_RANK = 384
DSA_KV_RANK = 256
IDX_HEADS = 16
IDX_DIM = 64
TOPK_MAX = 256
Q_BLOCK = 128
KV_TILE = 512
SM_ROWS = 64
LOG2E = math.log2(math.e)
REL_BUCKETS = 32
REL_MAX_DIST = 256
NEG_BIG = -1e30
INT_MIN = -(2 ** 31)
SCAN_TILES = 2
PLANE_GROUP = 32

RET_BLOCK = 256
RET_HEADS_PER_STEP = 4
HG_BLOCK = 256
HG_FINE = 4
VMEM_LIMIT = 48 * 1024 * 1024
PROJ_ROWS = 1024
PROJ_COLS = 1024
OUT_ROWS = 512
OUT_COLS = 2048
FFN_ROWS = 1024
FFN_VMEM_LIMIT = 58 * 1024 * 1024
SIDE_VMEM_LIMIT = 56 * 1024 * 1024
FFN_COLS = 512

_NT = (((1,), (1,)), ((), ()))
_TN = (((0,), (0,)), ((), ()))


def _params(semantics, vmem_limit=VMEM_LIMIT):
    return pltpu.CompilerParams(dimension_semantics=semantics, vmem_limit_bytes=vmem_limit)


def _silu(x):
    return x * jax.nn.sigmoid(x)


def _rms(x):
    return x * lax.rsqrt(jnp.mean(x * x, axis=-1, keepdims=True) + EPS)


def _norm_matmul_kernel(x_ref, g_ref, w_ref, o_ref, h_ref):
    @pl.when(pl.program_id(1) == 0)
    def _():
        h_ref[...] = (_rms(x_ref[...]) * g_ref[...]).astype(BF16)

    o_ref[...] = jnp.dot(h_ref[...], w_ref[...], preferred_element_type=F32).astype(o_ref.dtype)


def _norm_matmul(x, g, w, layer, nout, *, tm, tn, out_dtype=F32):
    n, d = x.shape
    return pl.pallas_call(
        _norm_matmul_kernel,
        grid=(n // tm, nout // tn),
        in_specs=[
            pl.BlockSpec((tm, d), lambda i, j: (i, 0)),
            pl.BlockSpec((1, d), lambda i, j: (0, 0)),
            pl.BlockSpec((None, d, tn), lambda i, j: (layer, 0, j)),
        ],
        out_specs=pl.BlockSpec((tm, tn), lambda i, j: (i, j)),
        out_shape=jax.ShapeDtypeStruct((n, nout), out_dtype),
        scratch_shapes=[pltpu.VMEM((tm, d), BF16)],
        compiler_params=_params(("arbitrary", "arbitrary")),
        name="norm_matmul",
    )(x, g.reshape(1, d), w)


def _outproj_kernel(a1_ref, a2_ref, w1_ref, w2_ref, r_ref, o_ref):
    acc = jnp.dot(a1_ref[...], w1_ref[...], preferred_element_type=F32)
    acc += jnp.dot(a2_ref[...], w2_ref[...], preferred_element_type=F32)
    o_ref[...] = r_ref[...] + acc


def _outproj(a1, a2, w, res, *, tm, tn):
    n, half = a1.shape
    d = w.shape[1]
    return pl.pallas_call(
        _outproj_kernel,
        grid=(n // tm, d // tn),
        in_specs=[
            pl.BlockSpec((tm, half), lambda i, j: (i, 0)),
            pl.BlockSpec((tm, half), lambda i, j: (i, 0)),
            pl.BlockSpec((half, tn), lambda i, j: (0, j)),
            pl.BlockSpec((half, tn), lambda i, j: (1, j)),
            pl.BlockSpec((tm, tn), lambda i, j: (i, j)),
        ],
        out_specs=pl.BlockSpec((tm, tn), lambda i, j: (i, j)),
        out_shape=jax.ShapeDtypeStruct((n, d), F32),
        compiler_params=_params(("arbitrary", "arbitrary")),
        name="outproj",
    )(a1, a2, w, w, res)


def _ffn_kernel(x_ref, g_ref, wg_ref, wu_ref, wd_ref, o_ref, h_ref):
    @pl.when(pl.program_id(1) == 0)
    def _():
        x = x_ref[...]
        h_ref[...] = (_rms(x) * g_ref[...]).astype(BF16)
        o_ref[...] = x

    h = h_ref[...]
    a = jnp.dot(h, wg_ref[...], preferred_element_type=F32)
    u = jnp.dot(h, wu_ref[...], preferred_element_type=F32)
    act = (_silu(a) * u).astype(BF16)
    o_ref[...] += jnp.dot(act, wd_ref[...], preferred_element_type=F32)


def _ffn(x, g, wg, wu, wd, layer, *, tm, tf):
    n, d = x.shape
    dff = wg.shape[2]
    return pl.pallas_call(
        _ffn_kernel,
        grid=(n // tm, dff // tf),
        in_specs=[
            pl.BlockSpec((tm, d), lambda i, f: (i, 0)),
            pl.BlockSpec((1, d), lambda i, f: (0, 0)),
            pl.BlockSpec((None, d, tf), lambda i, f: (layer, 0, f)),
            pl.BlockSpec((None, d, tf), lambda i, f: (layer, 0, f)),
            pl.BlockSpec((None, tf, d), lambda i, f: (layer, f, 0)),
        ],
        out_specs=pl.BlockSpec((tm, d), lambda i, f: (i, 0)),
        out_shape=jax.ShapeDtypeStruct((n, d), F32),
        scratch_shapes=[pltpu.VMEM((tm, d), BF16)],
        compiler_params=_params(("arbitrary", "arbitrary"), FFN_VMEM_LIMIT),
        name="ffn",
    )(x, g.reshape(1, d), wg, wu, wd)


def _retention_kernel(q_ref, k_ref, v_ref, g_ref, cos_ref, sin_ref, d_ref, xi_ref, zeta_ref,
                      gl_ref, o_ref, state_ref):
    @pl.when(pl.program_id(2) == 0)
    def _():
        state_ref[...] = jnp.zeros_like(state_ref)

    cos = cos_ref[...]
    sin = sin_ref[...]
    half = cos.shape[1]
    dk = 2 * half

    def rot(t):
        t1, t2 = t[:, :half], t[:, half:]
        return jnp.concatenate([t1 * cos - t2 * sin, t1 * sin + t2 * cos], axis=1)

    for i in range(RET_HEADS_PER_STEP):
        cs = slice(i * dk, (i + 1) * dk)
        q = rot(q_ref[:, cs].astype(F32))
        k = rot(k_ref[:, cs].astype(F32)) * (dk ** -0.5)
        qb = q.astype(BF16)
        vb = v_ref[:, cs].astype(BF16)
        scores = lax.dot_general(qb, k.astype(BF16), _NT, preferred_element_type=F32) * d_ref[i]
        intra = jnp.dot(scores.astype(BF16), vb, preferred_element_type=F32)
        state = state_ref[i]
        cross = jnp.dot(qb, state.astype(BF16), preferred_element_type=F32) * xi_ref[i]
        kz = (k * zeta_ref[i]).astype(BF16)
        state_ref[i] = state * gl_ref[i] + lax.dot_general(kz, vb, _TN, preferred_element_type=F32)
        o_ref[:, cs] = (_rms(intra + cross) * _silu(g_ref[:, cs].astype(F32))).astype(BF16)


def _retention_tables(seq, dk):
    blk = RET_BLOCK
    pos = jnp.arange(seq, dtype=F32)
    inv = ROPE_BASE ** (-jnp.arange(0, dk, 2, dtype=F32) / dk)
    ang = pos[:, None] * inv[None, :]
    log_gamma = jnp.log(1.0 - 2.0 ** (-5.0 - jnp.arange(RET_HEADS, dtype=F32)))
    i = jnp.arange(blk)
    same = (i[:, None] // CHUNK) == (i[None, :] // CHUNK)
    earlier = (i[None, :] // CHUNK) < (i[:, None] // CHUNK)
    diff = (i[:, None] - i[None, :]).astype(F32)
    dist = jnp.where(same, jnp.abs(diff), diff)
    decay = jnp.where((same | earlier)[None], jnp.exp(log_gamma[:, None, None] * dist[None]), 0.0)
    p = jnp.arange(blk, dtype=F32)
    wide = (RET_HEADS, blk, dk)
    xi = jnp.broadcast_to(jnp.exp(log_gamma[:, None] * (p + 1.0))[:, :, None], wide)
    zeta = jnp.broadcast_to(jnp.exp(log_gamma[:, None] * (blk - 1.0 - p))[:, :, None], wide)
    g_blk = jnp.broadcast_to(jnp.exp(log_gamma * blk)[:, None, None], (RET_HEADS, 1, dk))
    return jnp.cos(ang), jnp.sin(ang), decay, xi, zeta, g_blk


def _retention(z, batch, seq):
    n = z.shape[0]
    gw = z.shape[1] // 6
    dk = gw // RET_HEADS
    blk = RET_BLOCK
    nblk = seq // blk
    cos, sin, decay, xi, zeta, g_blk = _retention_tables(seq, dk)

    hp = RET_HEADS_PER_STEP
    groups = RET_HEADS // hp

    def zspec(part):
        return pl.BlockSpec((blk, hp * dk), lambda b, h, c: (b * nblk + c, part * groups + h))

    def hspec(rows, cols):
        return pl.BlockSpec((hp, rows, cols), lambda b, h, c: (h, 0, 0))

    return pl.pallas_call(
        _retention_kernel,
        grid=(batch, groups, nblk),
        in_specs=[
            zspec(0), zspec(1), zspec(2), zspec(3),
            pl.BlockSpec((blk, dk // 2), lambda b, h, c: (c, 0)),
            pl.BlockSpec((blk, dk // 2), lambda b, h, c: (c, 0)),
            hspec(blk, blk), hspec(blk, dk), hspec(blk, dk), hspec(1, dk),
        ],
        out_specs=pl.BlockSpec((blk, hp * dk), lambda b, h, c: (b * nblk + c, h)),
        out_shape=jax.ShapeDtypeStruct((n, gw), BF16),
        scratch_shapes=[pltpu.VMEM((hp, dk, dk), F32)],
        compiler_params=_params(("arbitrary", "arbitrary", "arbitrary")),
        name="retention",
    )(z, z, z, z, cos, sin, decay, xi, zeta, g_blk)


def _gelu(x):
    return 0.5 * x * (1.0 + lax.erf(x * math.sqrt(0.5)))


def _sgu_kernel(u_ref, v_ref, lng_ref, lnb_ref, w_ref, b_ref, o_ref):
    rows, width = v_ref.shape
    dg = width // SGU_GROUPS
    v = _gelu(v_ref[...].astype(F32))
    mu = jnp.mean(v, axis=-1, keepdims=True)
    var = jnp.mean(jnp.square(v - mu), axis=-1, keepdims=True)
    vn = ((v - mu) * lax.rsqrt(var + EPS) * lng_ref[...] + lnb_ref[...]).astype(BF16)
    u = _gelu(u_ref[...].astype(F32))
    ri = lax.broadcasted_iota(I32, (SGU_WINDOW, SGU_WINDOW), 0) // CHUNK
    ci = lax.broadcasted_iota(I32, (SGU_WINDOW, SGU_WINDOW), 1) // CHUNK
    allowed = ci <= ri
    for g in range(SGU_GROUPS):
        wg = jnp.where(allowed, w_ref[g], 0.0).astype(BF16)
        bias = b_ref[g]
        for w in range(rows // SGU_WINDOW):
            rs = slice(w * SGU_WINDOW, (w + 1) * SGU_WINDOW)
            cs = slice(g * dg, (g + 1) * dg)
            mixed = jnp.dot(wg, vn[rs, cs], preferred_element_type=F32) + bias
            o_ref[rs, cs] = (u[rs, cs] * mixed).astype(BF16)


def _sgu(z, ln_g, ln_b, w_s, b_s, *, rows):
    n = z.shape[0]
    gw = z.shape[1] // 6
    return pl.pallas_call(
        _sgu_kernel,
        grid=(n // rows,),
        in_specs=[
            pl.BlockSpec((rows, gw), lambda i: (i, 4)),
            pl.BlockSpec((rows, gw), lambda i: (i, 5)),
            pl.BlockSpec((1, gw), lambda i: (0, 0)),
            pl.BlockSpec((1, gw), lambda i: (0, 0)),
            pl.BlockSpec((SGU_GROUPS, SGU_WINDOW, SGU_WINDOW), lambda i: (0, 0, 0)),
            pl.BlockSpec((SGU_GROUPS, SGU_WINDOW, 1), lambda i: (0, 0, 0)),
        ],
        out_specs=pl.BlockSpec((rows, gw), lambda i: (i, 0)),
        out_shape=jax.ShapeDtypeStruct((n, gw), BF16),
        compiler_params=_params(("arbitrary",)),
        name="sgu",
    )(z, z, ln_g.reshape(1, gw), ln_b.reshape(1, gw), w_s, b_s.reshape(SGU_GROUPS, SGU_WINDOW, 1))


def _hgrn_head(q, f_logits, v, g, lb, ng, tri, st_ref, sh_ref):
    rows, dk = q.shape
    f = lb + (1.0 - lb) * jax.nn.sigmoid(f_logits)
    lf = jnp.log(f)
    kk = 1.0 - f
    qa = _silu(q)

    bcum = None
    rest = lf
    for _ in range(3):
        term = rest.astype(BF16)
        part = jnp.dot(tri, term, preferred_element_type=F32)
        bcum = part if bcum is None else bcum + part
        rest = rest - term.astype(F32)

    row = lax.broadcasted_iota(I32, (rows, dk), 0)
    ti = lax.broadcasted_iota(I32, (rows, rows), 0)
    si = lax.broadcasted_iota(I32, (rows, rows), 1)
    attn = jnp.zeros((rows, rows), F32)
    hs = rows // 2
    while hs >= HG_FINE:
        bs = 2 * hs
        parts = [jnp.broadcast_to(bcum[b * bs + hs - 1:b * bs + hs, :], (bs, dk))
                 for b in range(rows // bs)]
        anchor = parts[0] if len(parts) == 1 else jnp.concatenate(parts, axis=0)
        upper = (row & (bs - 1)) >= hs
        fac = jnp.exp(-jnp.abs(bcum - anchor))
        qt = jnp.where(upper, qa * fac, 0.0)
        kt = jnp.where(upper, 0.0, kk * fac)
        a = lax.dot_general(qt.astype(BF16), kt.astype(BF16), _NT, preferred_element_type=F32)
        if bs < rows:
            a = jnp.where((ti & -bs) == (si & -bs), a, 0.0)
        attn = attn + a
        hs //= 2

    vb = v.astype(BF16)
    near = qa * kk
    intra = jnp.sum(near, axis=1, keepdims=True) * v
    pad = jnp.zeros((HG_FINE, dk), F32)
    for idx, val in enumerate((kk, bcum, v)):
        sh_ref[idx, 0:HG_FINE, :] = pad
        sh_ref[idx, HG_FINE:, :] = val
    for delta in range(1, HG_FINE):
        back = slice(HG_FINE - delta, HG_FINE - delta + rows)
        prod = qa * sh_ref[0, back, :] * jnp.exp(jnp.minimum(bcum - sh_ref[1, back, :], 0.0))
        prod = jnp.where((row & (HG_FINE - 1)) >= delta, prod, 0.0)
        intra = intra + jnp.sum(prod, axis=1, keepdims=True) * sh_ref[2, back, :]
    intra = intra + jnp.dot(attn.astype(BF16), vb, preferred_element_type=F32)
    st = st_ref[...]
    cross = lax.dot_general((qa * jnp.exp(bcum)).astype(BF16), st.astype(BF16), _NT,
                            preferred_element_type=F32)
    blast = bcum[rows - 1:rows, :]
    kb = (kk * jnp.exp(blast - bcum)).astype(BF16)
    st_ref[...] = st * jnp.exp(blast) + lax.dot_general(vb, kb, _TN, preferred_element_type=F32)
    return _rms(intra + cross) * ng * _silu(g)


def _hgrn_lower_bound(lb_raw, layer):
    e = jnp.exp(lb_raw - jnp.max(lb_raw, axis=0, keepdims=True))
    soft = e / jnp.sum(e, axis=0, keepdims=True)
    return jnp.sum(soft[1:layer + 1], axis=0, keepdims=True)


def _inproj_hgrn_kernel(x_ref, g_ref, w_ref, ws_ref, lb_ref, ng_ref, tri_ref, o_ref, zd_ref,
                        z_ref, st_ref, sh_ref, *, layer):
    c = pl.program_id(1)
    gw = o_ref.shape[1]
    dk = gw // HG_HEADS

    @pl.when((pl.program_id(0) == 0) & (c == 0))
    def _():
        z_ref[...] = jnp.zeros_like(z_ref)

    @pl.when(c <= 1)
    def _():
        st_ref[...] = jnp.zeros_like(st_ref)

    h = (_rms(x_ref[...]) * g_ref[...]).astype(BF16)
    z_ref[c % 2] = jnp.dot(h, w_ref[...], preferred_element_type=F32)
    zd_ref[...] = jnp.dot(h, ws_ref[...], preferred_element_type=F32)

    prev = (c + 1) % 2
    lb = _hgrn_lower_bound(lb_ref[...], layer)
    tri = tri_ref[...]
    for hd in range(HG_HEADS):
        cols = [slice(part * gw + hd * dk, part * gw + (hd + 1) * dk) for part in range(4)]
        hs = slice(hd * dk, (hd + 1) * dk)
        out = _hgrn_head(z_ref[prev, :, cols[0]], z_ref[prev, :, cols[1]], z_ref[prev, :, cols[2]],
                         z_ref[prev, :, cols[3]], lb[:, hs], ng_ref[:, hs], tri, st_ref.at[hd], sh_ref.at[hd])
        o_ref[:, hs] = out.astype(BF16)


def _inproj_hgrn(x, g, w, layer_idx, w_side, lb_raw, norm_g, batch, seq, layer):
    n, d = x.shape
    gw = norm_g.shape[0]
    dk = gw // HG_HEADS
    blk = HG_BLOCK
    nblk = seq // blk
    ns = w_side.shape[1]
    depth = lb_raw.shape[0]
    const = lambda b, c: (0, 0)
    return pl.pallas_call(
        functools.partial(_inproj_hgrn_kernel, layer=layer),
        grid=(batch, nblk + 1),
        in_specs=[
            pl.BlockSpec((blk, d), lambda b, c: (b * nblk + jnp.minimum(c, nblk - 1), 0)),
            pl.BlockSpec((1, d), const),
            pl.BlockSpec((None, d, 4 * gw), lambda b, c: (layer_idx, 0, 0), pipeline_mode=pl.Buffered(1)),
            pl.BlockSpec((d, ns), const, pipeline_mode=pl.Buffered(1)),
            pl.BlockSpec((depth, gw), const),
            pl.BlockSpec((1, gw), const),
            pl.BlockSpec((blk, blk), const),
        ],
        out_specs=[
            pl.BlockSpec((blk, gw), lambda b, c: (b * nblk + jnp.maximum(c - 1, 0), 0)),
            pl.BlockSpec((blk, ns), lambda b, c: (b * nblk + jnp.minimum(c, nblk - 1), 0)),
        ],
        out_shape=[jax.ShapeDtypeStruct((n, gw), BF16), jax.ShapeDtypeStruct((n, ns), F32)],
        scratch_shapes=[
            pltpu.VMEM((2, blk, 4 * gw), F32),
            pltpu.VMEM((HG_HEADS, dk, dk), F32),
            pltpu.VMEM((HG_HEADS, 3, blk + HG_FINE, dk), F32),
        ],
        compiler_params=_params(("arbitrary", "arbitrary"), SIDE_VMEM_LIMIT),
        name="inproj_hgrn",
    )(x, g.reshape(1, d), w, w_side, lb_raw, norm_g.reshape(1, gw), jnp.tril(jnp.ones((blk, blk), BF16)))


def _dsa_prep_kernel(zd_ref, cqg_ref, ckvg_ref, wuq_ref, qng_ref, wqit_ref,
                     q_ref, qit_ref, kv_ref, kix_ref, wht_ref):
    zd = zd_ref[...]
    cq = (_rms(zd[:, :DSA_Q_RANK]) * cqg_ref[...]).astype(BF16)
    qf = jnp.dot(cq, wuq_ref[...], preferred_element_type=F32)
    for i in range(q_ref.shape[0]):
        rs = slice(i * Q_BLOCK, (i + 1) * Q_BLOCK)
        for h in range(DSA_HEADS):
            cs = slice(h * DSA_KV_RANK, (h + 1) * DSA_KV_RANK)
            q_ref[i, h] = (_rms(qf[rs, cs]) * qng_ref[...] * (DSA_KV_RANK ** -0.5 * LOG2E)).astype(BF16)
    qit = lax.dot_general(wqit_ref[...], cq, _NT, preferred_element_type=F32)
    qit = (qit * (IDX_DIM ** -0.5)).astype(BF16)
    for i in range(q_ref.shape[0]):
        for h in range(IDX_HEADS):
            c = (i * IDX_HEADS + h) * Q_BLOCK
            qit_ref[:, c:c + Q_BLOCK] = qit[h * LANES:(h + 1) * LANES, i * Q_BLOCK:(i + 1) * Q_BLOCK]
    c0 = DSA_Q_RANK
    c1 = c0 + DSA_KV_RANK
    kv_ref[...] = (_rms(zd[:, c0:c1]) * ckvg_ref[...]).astype(BF16)
    kix_ref[...] = zd[:, c1:c1 + LANES].astype(BF16)
    wht = jnp.transpose(zd[:, c1 + LANES:c1 + 2 * LANES] * (IDX_HEADS ** -0.5))
    wht_ref[...] = wht[:IDX_HEADS, :]


def _dsa_prep(zd, cq_g, ckv_g, w_uq, qn_g, w_qit, *, tm):
    n, wd = zd.shape
    dq = w_uq.shape[1]
    dqi = w_qit.shape[0]
    full = lambda i: (0, 0)
    rows = lambda i: (i, 0)
    cols = lambda i: (0, i)
    return pl.pallas_call(
        _dsa_prep_kernel,
        grid=(n // tm,),
        in_specs=[
            pl.BlockSpec((tm, wd), rows),
            pl.BlockSpec((1, DSA_Q_RANK), full),
            pl.BlockSpec((1, DSA_KV_RANK), full),
            pl.BlockSpec((DSA_Q_RANK, dq), full),
            pl.BlockSpec((1, DSA_KV_RANK), full),
            pl.BlockSpec((dqi, DSA_Q_RANK), full),
        ],
        out_specs=[
            pl.BlockSpec((tm // Q_BLOCK, DSA_HEADS, Q_BLOCK, DSA_KV_RANK), lambda i: (i, 0, 0, 0)),
            pl.BlockSpec((LANES, IDX_HEADS * tm), cols),
            pl.BlockSpec((tm, DSA_KV_RANK), rows),
            pl.BlockSpec((tm, LANES), rows),
            pl.BlockSpec((IDX_HEADS, tm), cols),
        ],
        out_shape=[
            jax.ShapeDtypeStruct((n // Q_BLOCK, DSA_HEADS, Q_BLOCK, DSA_KV_RANK), BF16),
            jax.ShapeDtypeStruct((LANES, IDX_HEADS * n), BF16),
            jax.ShapeDtypeStruct((n, DSA_KV_RANK), BF16),
            jax.ShapeDtypeStruct((n, LANES), BF16),
            jax.ShapeDtypeStruct((IDX_HEADS, n), F32),
        ],
        compiler_params=_params(("arbitrary",)),
        name="dsa_prep",
    )(zd, cq_g.reshape(1, -1), ckv_g.reshape(1, -1), w_uq, qn_g.reshape(1, -1), w_qit)


def _bit_planes(key):
    rows = key.shape[0]
    v = [key[8 * k:8 * (k + 1), :] for k in range(rows // 8)]
    sub = lax.broadcasted_iota(I32, v[0].shape, 0)

    def swap(lo, hi, j, m):
        return (lo & ~m) | ((hi >> j) & m), (hi & m) | ((lo << j) & ~m)

    for g in range(0, len(v), 4):
        for a, b in ((0, 2), (1, 3)):
            v[g + a], v[g + b] = swap(v[g + a], v[g + b], 16, 0x0000FFFF)
        for a, b in ((0, 1), (2, 3)):
            v[g + a], v[g + b] = swap(v[g + a], v[g + b], 8, 0x00FF00FF)
    for j, m in ((4, 0x0F0F0F0F), (2, 0x33333333), (1, 0x55555555)):
        high = (sub & j) != 0
        keep = jnp.where(high, m, ~m)
        for k in range(len(v)):
            down, up = pltpu.roll(v[k], j, axis=0), pltpu.roll(v[k], 8 - j, axis=0)
            moved = jnp.where(high, down << j, up >> j)
            v[k] = (v[k] & keep) | (moved & ~keep)
    return jnp.concatenate(v, axis=0)


def _dsa_select_kernel(qit_ref, wht_ref, kix_ref, m_ref, key_ref, jc_ref, plane_ref, *, ksel, idx_bits):
    qb = pl.program_id(1)
    ntile = qb + 1
    ntile_all = m_ref.shape[2]
    rowi = lax.broadcasted_iota(I32, (LANES, Q_BLOCK), 0)
    coli = lax.broadcasted_iota(I32, (LANES, Q_BLOCK), 1)
    q_chunk = (qb * Q_BLOCK + coli) // CHUNK
    pairs = IDX_HEADS // 2
    w_pair = [jnp.concatenate([wht_ref[2 * p:2 * p + 1, :], wht_ref[2 * p + 1:2 * p + 2, :]], axis=1)
              for p in range(pairs)]

    def tile_rows(j):
        return pl.ds(pl.multiple_of(j * LANES, LANES), LANES)

    def admissible(j):
        return ((j * LANES + rowi) // CHUNK) <= q_chunk

    def score_tile(jj, carry):
        for t in range(2):
            j = 2 * jj + t
            kt = kix_ref[tile_rows(j), :]
            sc = None
            for p in range(pairs):
                s2 = jnp.dot(kt, qit_ref[:, 2 * p * Q_BLOCK:2 * (p + 1) * Q_BLOCK],
                             preferred_element_type=F32)
                c2 = w_pair[p] * jnp.maximum(s2, 0.0)
                c = c2[:, :Q_BLOCK] + c2[:, Q_BLOCK:]
                sc = c if sc is None else sc + c
            bits = pltpu.bitcast(sc, I32)
            key = bits ^ ((bits >> 31) & 0x7FFFFFFF)
            key = jnp.where(admissible(j), key, INT_MIN)
            key_ref[tile_rows(j), :] = key
            plane_ref[tile_rows(j), :] = _bit_planes(key)
        return carry

    @pl.when(qb == 0)
    def _():
        key_ref[...] = jnp.full(key_ref.shape, INT_MIN, I32)
        group_row = lax.broadcasted_iota(I32, plane_ref.shape, 0) & (PLANE_GROUP - 1)
        plane_ref[...] = jnp.where(group_row == 0, -1, 0)

    lax.fori_loop(0, (ntile + 1) // 2, score_tile, 0)

    def count(pred_fn):
        def body(jj, acc):
            for t in range(SCAN_TILES):
                j = SCAN_TILES * jj + t
                acc = acc + pred_fn(j, key_ref[tile_rows(j), :]).astype(I32)
            return acc
        trips = (ntile + SCAN_TILES - 1) // SCAN_TILES
        acc = lax.fori_loop(0, trips, body, jnp.zeros((LANES, Q_BLOCK), I32))
        return jnp.sum(acc, axis=0, keepdims=True)

    nword = plane_ref.shape[0] // (8 * PLANE_GROUP)

    def radix_bit(i, c):
        alive, above, t_u = c
        flip = jnp.where(i == 0, -1, 0)
        ones = [a & (plane_ref[pl.ds(v * 8 * PLANE_GROUP + i, 8, stride=PLANE_GROUP), :] ^ flip)
                for v, a in enumerate(alive)]
        acc = lax.population_count(ones[0])
        for o in ones[1:]:
            acc = acc + lax.population_count(o)
        cnt1 = jnp.sum(acc, axis=0, keepdims=True)
        take = (above + cnt1) >= ksel
        alive = tuple(jnp.where(take, o, a ^ o) for a, o in zip(alive, ones))
        above = jnp.where(take, above, above + cnt1)
        t_u = t_u | jnp.where(take, jnp.left_shift(jnp.int32(1), 31 - i), 0)
        return alive, above, t_u

    start = (tuple(jnp.full((8, Q_BLOCK), -1, I32) for _ in range(nword)),
             jnp.zeros((1, Q_BLOCK), I32), jnp.zeros((1, Q_BLOCK), I32))
    alive, above, t_u = lax.fori_loop(0, 32, radix_bit, start)
    thr = t_u ^ INT_MIN
    ties = lax.population_count(alive[0])
    for a in alive[1:]:
        ties = ties + lax.population_count(a)
    cnt_t = above + jnp.sum(ties, axis=0, keepdims=True)
    tied = jnp.max(jnp.where((cnt_t > ksel) & (thr > INT_MIN), 1.0, 0.0))

    jc_ref[...] = jnp.full(jc_ref.shape, 2 ** 31 - 1, I32)

    @pl.when(tied > 0.0)
    def _():
        need = ksel - count(lambda j, k: k > thr)

        def index_bit(i, j_c):
            cand = j_c | jnp.left_shift(jnp.int32(1), idx_bits - 1 - i)
            cnt = count(lambda j, k: (k == thr) & ((j * LANES + rowi) < cand))
            return jnp.where(cnt < need, cand, j_c)

        j_c = lax.fori_loop(0, idx_bits, index_bit, jnp.zeros((1, Q_BLOCK), I32))
        jc_ref[...] = jnp.broadcast_to(j_c, jc_ref.shape)

    j_c = jc_ref[0:1, :]
    eye = (rowi == coli).astype(BF16)

    group = KV_TILE // LANES

    def write_group(g, carry):
        for t in range(group):
            j = g * group + t
            k = key_ref[tile_rows(j), :]
            sel = (k > thr) | ((k == thr) & ((j * LANES + rowi) <= j_c))
            sel = jnp.where(sel & admissible(j), 1.0, 0.0).astype(BF16)
            sel_t = lax.dot_general(eye, sel, _NT, preferred_element_type=F32)
            m_ref[0, 0, j] = ((sel_t - 1.0) * -NEG_BIG).astype(BF16)
        return carry

    ngroup = (ntile + group - 1) // group
    lax.fori_loop(0, ngroup, write_group, 0)

    def blank_tile(j, carry):
        m_ref[0, 0, j] = jnp.full((Q_BLOCK, LANES), NEG_BIG, BF16)
        return carry

    lax.fori_loop(ngroup * group, ntile_all, blank_tile, 0)


def _dsa_select(qit, wht, kix, batch, seq, ksel):
    nqb = seq // Q_BLOCK
    nkt = seq // LANES
    return pl.pallas_call(
        functools.partial(_dsa_select_kernel, ksel=ksel, idx_bits=int(math.log2(seq))),
        grid=(batch, nqb),
        in_specs=[
            pl.BlockSpec((LANES, IDX_HEADS * Q_BLOCK), lambda b, q: (0, b * nqb + q)),
            pl.BlockSpec((IDX_HEADS, Q_BLOCK), lambda b, q: (0, b * nqb + q)),
            pl.BlockSpec((seq, LANES), lambda b, q: (b, 0)),
        ],
        out_specs=pl.BlockSpec((1, 1, nkt, Q_BLOCK, LANES), lambda b, q: (b, q, 0, 0, 0)),
        out_shape=jax.ShapeDtypeStruct((batch, nqb, nkt, Q_BLOCK, LANES), BF16),
        scratch_shapes=[pltpu.VMEM((seq, Q_BLOCK), I32), pltpu.VMEM((8, Q_BLOCK), I32),
                        pltpu.VMEM((seq, Q_BLOCK), I32)],
        compiler_params=_params(("arbitrary", "arbitrary")),
        name="dsa_select",
    )(qit, wht, kix)


def _rel_bucket(rel):
    nb = REL_BUCKETS // 2
    max_exact = nb // 2
    ret = jnp.where(rel > 0, nb, 0)
    n = jnp.abs(rel)
    nf = jnp.maximum(n, 1).astype(F32)
    large = max_exact + (jnp.log(nf / max_exact) / math.log(REL_MAX_DIST / max_exact)
                         * (nb - max_exact)).astype(I32)
    large = jnp.minimum(large, nb - 1)
    return ret + jnp.where(n < max_exact, n, large)


NEAR_TILES = 3


def _dsa_attn_kernel(q_ref, kv_ref, mask_ref, rb_ref, wuv_ref, o_ref,
                     m_ref, l_ref, alpha_ref, acc_ref, corr_ref, s_ref, p_ref, madd_ref):
    b, qb = pl.program_id(0), pl.program_id(1)
    sub = KV_TILE // LANES
    far_bucket = REL_BUCKETS // 2 - 1
    half = DSA_HEADS * Q_BLOCK // 2

    def head_rows(h):
        return slice(h * Q_BLOCK, (h + 1) * Q_BLOCK)

    @pl.when((b == 0) & (qb == 0))
    def _():
        ti = lax.broadcasted_iota(I32, (Q_BLOCK, LANES), 0)
        si = lax.broadcasted_iota(I32, (Q_BLOCK, LANES), 1)
        for oi in range(NEAR_TILES):
            bucket = _rel_bucket((oi - (NEAR_TILES - 1)) * LANES + si - ti)
            for h in range(DSA_HEADS):
                tbl = jnp.zeros((Q_BLOCK, LANES), F32)
                for bk in range(REL_BUCKETS):
                    tbl = jnp.where(bucket == bk, rb_ref[bk, h], tbl)
                corr_ref[oi, head_rows(h), :] = (tbl - rb_ref[far_bucket, h]) * LOG2E

    m_ref[...] = jnp.full_like(m_ref, NEG_BIG)
    l_ref[...] = jnp.zeros_like(l_ref)
    acc_ref[...] = jnp.zeros_like(acc_ref)
    q_all = q_ref[0].reshape(DSA_HEADS * Q_BLOCK, DSA_KV_RANK)

    def key_step(kt, carry):
        kvt = kv_ref[pl.ds(pl.multiple_of(kt * KV_TILE, KV_TILE), KV_TILE), :]
        for part in range(2):
            rs = slice(part * half, (part + 1) * half)
            s_ref[rs, :] = lax.dot_general(q_all[rs], kvt, _NT, preferred_element_type=F32)
        for j in range(sub):
            d = kt * sub + j - qb

            @pl.when((d > -NEAR_TILES) & (d <= 0))
            def _(j=j, d=d):
                s_ref[:, j * LANES:(j + 1) * LANES] += corr_ref[d + NEAR_TILES - 1]

        for j in range(sub):
            madd_ref[:, j * LANES:(j + 1) * LANES] = mask_ref[0, 0, kt * sub + j].astype(F32)
        groups_per_head = Q_BLOCK // SM_ROWS
        for g in range(DSA_HEADS * groups_per_head):
            rs = slice(g * SM_ROWS, (g + 1) * SM_ROWS)
            qg = g % groups_per_head
            s = s_ref[rs, :] + madd_ref[qg * SM_ROWS:(qg + 1) * SM_ROWS, :]
            m_old = m_ref[rs, :]
            m_new = jnp.maximum(m_old, jnp.max(s, axis=1, keepdims=True))
            alpha = jnp.exp2(m_old - m_new)
            p = jnp.exp2(s - jnp.tile(m_new, (1, sub)))
            l_ref[rs, :] = alpha * l_ref[rs, :] + jnp.sum(p, axis=1, keepdims=True)
            alpha_ref[rs, :] = alpha
            p_ref[rs, :] = p.astype(BF16)
            m_ref[rs, :] = m_new
        for part in range(2):
            rs = slice(part * half, (part + 1) * half)
            pv = jnp.dot(p_ref[rs, :], kvt, preferred_element_type=F32)
            acc_ref[rs, :] = jnp.tile(alpha_ref[rs, :], (1, DSA_KV_RANK // LANES)) * acc_ref[rs, :] + pv
        return carry

    lax.fori_loop(0, qb // sub + 1, key_step, 0)

    dv = wuv_ref.shape[2]
    for h in range(DSA_HEADS):
        rs = head_rows(h)
        o = (acc_ref[rs, :] / jnp.tile(l_ref[rs, :], (1, DSA_KV_RANK // LANES))).astype(BF16)
        o_ref[:, h * dv:(h + 1) * dv] = jnp.dot(o, wuv_ref[h], preferred_element_type=F32).astype(BF16)


def _dsa_attn(q, kv, mask, rel_bias, w_uv, batch, seq):
    n = kv.shape[0]
    nqb = seq // Q_BLOCK
    dv = w_uv.shape[2]
    rows = DSA_HEADS * Q_BLOCK
    return pl.pallas_call(
        _dsa_attn_kernel,
        grid=(batch, nqb),
        in_specs=[
            pl.BlockSpec((1, DSA_HEADS, Q_BLOCK, DSA_KV_RANK), lambda b, qb: (b * nqb + qb, 0, 0, 0)),
            pl.BlockSpec((seq, DSA_KV_RANK), lambda b, qb: (b, 0)),
            pl.BlockSpec((1, 1) + mask.shape[2:], lambda b, qb: (b, qb, 0, 0, 0)),
            pl.BlockSpec(memory_space=pltpu.SMEM),
            pl.BlockSpec(w_uv.shape, lambda b, qb: (0, 0, 0)),
        ],
        out_specs=pl.BlockSpec((Q_BLOCK, DSA_HEADS * dv), lambda b, qb: (b * nqb + qb, 0)),
        out_shape=jax.ShapeDtypeStruct((n, DSA_HEADS * dv), BF16),
        scratch_shapes=[
            pltpu.VMEM((rows, LANES), F32),
            pltpu.VMEM((rows, LANES), F32),
            pltpu.VMEM((rows, LANES), F32),
            pltpu.VMEM((rows, DSA_KV_RANK), F32),
            pltpu.VMEM((NEAR_TILES, rows, LANES), F32),
            pltpu.VMEM((rows, KV_TILE), F32),
            pltpu.VMEM((rows, KV_TILE), BF16),
            pltpu.VMEM((Q_BLOCK, KV_TILE), F32),
        ],
        compiler_params=_params(("arbitrary", "arbitrary")),
        name="dsa_attn",
    )(q, kv, mask, rel_bias, w_uv)


def _pad_cols(w, width):
    return jnp.pad(w, ((0, 0), (0, width - w.shape[1])))


def kernel(x, ln_mix_g, ln_ffn_g, w_ffn_gate, w_ffn_up, w_ffn_down, rel_bias, ev_w_in, ev_w_out, sgu_ln_g, sgu_ln_b, sgu_w_s, sgu_b_s, od_w_in, od_w_out, hgrn_lb, hgrn_norm_g, dsa_cq_g, dsa_ckv_g, dsa_w_uq, dsa_qnorm_g, dsa_w_qidx, dsa_w_uv):
    batch, seq, d = x.shape
    n = batch * seq
    depth = ln_mix_g.shape[0]
    ksel = min(TOPK_MAX, seq // 4)
    tm = min(PROJ_ROWS, n)
    xf = x.reshape(n, d)
    wg_all, wu_all, wd_all = (w.astype(BF16) for w in (w_ffn_gate, w_ffn_up, w_ffn_down))
    ev_in_all, od_in_all = ev_w_in.astype(BF16), od_w_in.astype(BF16)
    for layer in range(depth):
        j = layer // 2
        if layer % 2 == 0:
            z = _norm_matmul(xf, ln_mix_g[layer], ev_in_all, j, ev_in_all.shape[2], tm=tm, tn=PROJ_COLS, out_dtype=BF16)
            a1 = _retention(z, batch, seq)
            a2 = _sgu(z, sgu_ln_g[j], sgu_ln_b[j], sgu_w_s[j], sgu_b_s[j], rows=256)
            w_out = ev_w_out[j]
        else:
            w_in = od_w_in[j]
            gw = d // 2
            c = 4 * gw
            c_kidx = c + DSA_Q_RANK + DSA_KV_RANK
            w_dsa = jnp.concatenate([
                w_in[:, c:c_kidx],
                _pad_cols(w_in[:, c_kidx:c_kidx + IDX_DIM], LANES),
                _pad_cols(w_in[:, c_kidx + IDX_DIM:], LANES),
            ], axis=1).astype(BF16)
            a1, zd = _inproj_hgrn(xf, ln_mix_g[layer], od_in_all, j, w_dsa, hgrn_lb, hgrn_norm_g[j],
                                  batch, seq, layer)
            w_qit = jnp.pad(dsa_w_qidx[j].T.reshape(IDX_HEADS, IDX_DIM, DSA_Q_RANK),
                            ((0, 0), (0, LANES - IDX_DIM), (0, 0))).reshape(IDX_HEADS * LANES, DSA_Q_RANK)
            q, qit, kv, kix, wht = _dsa_prep(zd, dsa_cq_g[j], dsa_ckv_g[j], dsa_w_uq[j].astype(BF16),
                                             dsa_qnorm_g[j], w_qit.astype(BF16), tm=256)
            mask = _dsa_select(qit, wht, kix, batch, seq, ksel)
            a2 = _dsa_attn(q, kv, mask, rel_bias, dsa_w_uv[j].astype(BF16), batch, seq)
            w_out = od_w_out[j]
        xf = _outproj(a1, a2, w_out.astype(BF16), xf, tm=min(OUT_ROWS, n), tn=OUT_COLS)
        xf = _ffn(xf, ln_ffn_g[layer], wg_all, wu_all, wd_all, layer, tm=min(FFN_ROWS, n), tf=FFN_COLS)
    return xf.reshape(batch, seq, d)
```

```python
import functools
import math

import jax
import jax.numpy as jnp
from jax import lax
from jax.experimental import pallas as pl
from jax.experimental.pallas import tpu as pltpu

F32 = jnp.float32
BF16 = jnp.bfloat16
I32 = jnp.int32

EPS = 1e-6
CHUNK = 64
LANES = 128
ROPE_BASE = 10000.0
RET_HEADS = 4
SGU_WINDOW = 128
SGU_GROUPS = 4
HG_HEADS = 8
DSA_HEADS = 8
DSA_Q_RANK = 384
DSA_KV_RANK = 256
IDX_HEADS = 16
IDX_DIM = 64
TOPK_MAX = 256
Q_BLOCK = 128
KV_TILE = 512
SM_ROWS = 64
LOG2E = math.log2(math.e)
REL_BUCKETS = 32
REL_MAX_DIST = 256
NEG_BIG = -1e30
INT_MIN = -(2 ** 31)
SCAN_TILES = 2
PLANE_GROUP = 32

RET_BLOCK = 256
RET_HEADS_PER_STEP = 4
HG_BLOCK = 256
HG_FINE = 4
VMEM_LIMIT = 48 * 1024 * 1024
PROJ_ROWS = 1024
PROJ_COLS = 1024
OUT_ROWS = 512
OUT_COLS = 2048
FFN_ROWS = 1024
FFN_VMEM_LIMIT = 58 * 1024 * 1024
SIDE_VMEM_LIMIT = 56 * 1024 * 1024
FFN_COLS = 512

_NT = (((1,), (1,)), ((), ()))
_TN = (((0,), (0,)), ((), ()))


def _params(semantics, vmem_limit=VMEM_LIMIT):
    return pltpu.CompilerParams(dimension_semantics=semantics, vmem_limit_bytes=vmem_limit)


def _silu(x):
    return x * jax.nn.sigmoid(x)


def _rms(x):
    return x * lax.rsqrt(jnp.mean(x * x, axis=-1, keepdims=True) + EPS)


def _norm_matmul_kernel(x_ref, g_ref, w_ref, o_ref, h_ref):
    @pl.when(pl.program_id(1) == 0)
    def _():
        h_ref[...] = (_rms(x_ref[...]) * g_ref[...]).astype(BF16)

    o_ref[...] = jnp.dot(h_ref[...], w_ref[...], preferred_element_type=F32).astype(o_ref.dtype)


def _norm_matmul(x, g, w, layer, nout, *, tm, tn, out_dtype=F32):
    n, d = x.shape
    return pl.pallas_call(
        _norm_matmul_kernel,
        grid=(n // tm, nout // tn),
        in_specs=[
            pl.BlockSpec((tm, d), lambda i, j: (i, 0)),
            pl.BlockSpec((1, d), lambda i, j: (0, 0)),
            pl.BlockSpec((None, d, tn), lambda i, j: (layer, 0, j)),
        ],
        out_specs=pl.BlockSpec((tm, tn), lambda i, j: (i, j)),
        out_shape=jax.ShapeDtypeStruct((n, nout), out_dtype),
        scratch_shapes=[pltpu.VMEM((tm, d), BF16)],
        compiler_params=_params(("arbitrary", "arbitrary")),
        name="norm_matmul",
    )(x, g.reshape(1, d), w)


def _outproj_kernel(a1_ref, a2_ref, w1_ref, w2_ref, r_ref, o_ref):
    acc = jnp.dot(a1_ref[...], w1_ref[...], preferred_element_type=F32)
    acc += jnp.dot(a2_ref[...], w2_ref[...], preferred_element_type=F32)
    o_ref[...] = r_ref[...] + acc


def _outproj(a1, a2, w, res, *, tm, tn):
    n, half = a1.shape
    d = w.shape[1]
    return pl.pallas_call(
        _outproj_kernel,
        grid=(n // tm, d // tn),
        in_specs=[
            pl.BlockSpec((tm, half), lambda i, j: (i, 0)),
            pl.BlockSpec((tm, half), lambda i, j: (i, 0)),
            pl.BlockSpec((half, tn), lambda i, j: (0, j)),
            pl.BlockSpec((half, tn), lambda i, j: (1, j)),
            pl.BlockSpec((tm, tn), lambda i, j: (i, j)),
        ],
        out_specs=pl.BlockSpec((tm, tn), lambda i, j: (i, j)),
        out_shape=jax.ShapeDtypeStruct((n, d), F32),
        compiler_params=_params(("arbitrary", "arbitrary")),
        name="outproj",
    )(a1, a2, w, w, res)


def _ffn_kernel(x_ref, g_ref, wg_ref, wu_ref, wd_ref, o_ref, h_ref):
    @pl.when(pl.program_id(1) == 0)
    def _():
        x = x_ref[...]
        h_ref[...] = (_rms(x) * g_ref[...]).astype(BF16)
        o_ref[...] = x

    h = h_ref[...]
    a = jnp.dot(h, wg_ref[...], preferred_element_type=F32)
    u = jnp.dot(h, wu_ref[...], preferred_element_type=F32)
    act = (_silu(a) * u).astype(BF16)
    o_ref[...] += jnp.dot(act, wd_ref[...], preferred_element_type=F32)


def _ffn(x, g, wg, wu, wd, layer, *, tm, tf):
    n, d = x.shape
    dff = wg.shape[2]
    return pl.pallas_call(
        _ffn_kernel,
        grid=(n // tm, dff // tf),
        in_specs=[
            pl.BlockSpec((tm, d), lambda i, f: (i, 0)),
            pl.BlockSpec((1, d), lambda i, f: (0, 0)),
            pl.BlockSpec((None, d, tf), lambda i, f: (layer, 0, f)),
            pl.BlockSpec((None, d, tf), lambda i, f: (layer, 0, f)),
            pl.BlockSpec((None, tf, d), lambda i, f: (layer, f, 0)),
        ],
        out_specs=pl.BlockSpec((tm, d), lambda i, f: (i, 0)),
        out_shape=jax.ShapeDtypeStruct((n, d), F32),
        scratch_shapes=[pltpu.VMEM((tm, d), BF16)],
        compiler_params=_params(("arbitrary", "arbitrary"), FFN_VMEM_LIMIT),
        name="ffn",
    )(x, g.reshape(1, d), wg, wu, wd)


def _retention_kernel(q_ref, k_ref, v_ref, g_ref, cos_ref, sin_ref, d_ref, xi_ref, zeta_ref,
                      gl_ref, o_ref, state_ref):
    @pl.when(pl.program_id(2) == 0)
    def _():
        state_ref[...] = jnp.zeros_like(state_ref)

    cos = cos_ref[...]
    sin = sin_ref[...]
    half = cos.shape[1]
    dk = 2 * half

    def rot(t):
        t1, t2 = t[:, :half], t[:, half:]
        return jnp.concatenate([t1 * cos - t2 * sin, t1 * sin + t2 * cos], axis=1)

    for i in range(RET_HEADS_PER_STEP):
        cs = slice(i * dk, (i + 1) * dk)
        q = rot(q_ref[:, cs].astype(F32))
        k = rot(k_ref[:, cs].astype(F32)) * (dk ** -0.5)
        qb = q.astype(BF16)
        vb = v_ref[:, cs].astype(BF16)
        scores = lax.dot_general(qb, k.astype(BF16), _NT, preferred_element_type=F32) * d_ref[i]
        intra = jnp.dot(scores.astype(BF16), vb, preferred_element_type=F32)
        state = state_ref[i]
        cross = jnp.dot(qb, state.astype(BF16), preferred_element_type=F32) * xi_ref[i]
        kz = (k * zeta_ref[i]).astype(BF16)
        state_ref[i] = state * gl_ref[i] + lax.dot_general(kz, vb, _TN, preferred_element_type=F32)
        o_ref[:, cs] = (_rms(intra + cross) * _silu(g_ref[:, cs].astype(F32))).astype(BF16)


def _retention_tables(seq, dk):
    blk = RET_BLOCK
    pos = jnp.arange(seq, dtype=F32)
    inv = ROPE_BASE ** (-jnp.arange(0, dk, 2, dtype=F32) / dk)
    ang = pos[:, None] * inv[None, :]
    log_gamma = jnp.log(1.0 - 2.0 ** (-5.0 - jnp.arange(RET_HEADS, dtype=F32)))
    i = jnp.arange(blk)
    same = (i[:, None] // CHUNK) == (i[None, :] // CHUNK)
    earlier = (i[None, :] // CHUNK) < (i[:, None] // CHUNK)
    diff = (i[:, None] - i[None, :]).astype(F32)
    dist = jnp.where(same, jnp.abs(diff), diff)
    decay = jnp.where((same | earlier)[None], jnp.exp(log_gamma[:, None, None] * dist[None]), 0.0)
    p = jnp.arange(blk, dtype=F32)
    wide = (RET_HEADS, blk, dk)
    xi = jnp.broadcast_to(jnp.exp(log_gamma[:, None] * (p + 1.0))[:, :, None], wide)
    zeta = jnp.broadcast_to(jnp.exp(log_gamma[:, None] * (blk - 1.0 - p))[:, :, None], wide)
    g_blk = jnp.broadcast_to(jnp.exp(log_gamma * blk)[:, None, None], (RET_HEADS, 1, dk))
    return jnp.cos(ang), jnp.sin(ang), decay, xi, zeta, g_blk


def _retention(z, batch, seq):
    n = z.shape[0]
    gw = z.shape[1] // 6
    dk = gw // RET_HEADS
    blk = RET_BLOCK
    nblk = seq // blk
    cos, sin, decay, xi, zeta, g_blk = _retention_tables(seq, dk)

    hp = RET_HEADS_PER_STEP
    groups = RET_HEADS // hp

    def zspec(part):
        return pl.BlockSpec((blk, hp * dk), lambda b, h, c: (b * nblk + c, part * groups + h))

    def hspec(rows, cols):
        return pl.BlockSpec((hp, rows, cols), lambda b, h, c: (h, 0, 0))

    return pl.pallas_call(
        _retention_kernel,
        grid=(batch, groups, nblk),
        in_specs=[
            zspec(0), zspec(1), zspec(2), zspec(3),
            pl.BlockSpec((blk, dk // 2), lambda b, h, c: (c, 0)),
            pl.BlockSpec((blk, dk // 2), lambda b, h, c: (c, 0)),
            hspec(blk, blk), hspec(blk, dk), hspec(blk, dk), hspec(1, dk),
        ],
        out_specs=pl.BlockSpec((blk, hp * dk), lambda b, h, c: (b * nblk + c, h)),
        out_shape=jax.ShapeDtypeStruct((n, gw), BF16),
        scratch_shapes=[pltpu.VMEM((hp, dk, dk), F32)],
        compiler_params=_params(("arbitrary", "arbitrary", "arbitrary")),
        name="retention",
    )(z, z, z, z, cos, sin, decay, xi, zeta, g_blk)


def _gelu(x):
    return 0.5 * x * (1.0 + lax.erf(x * math.sqrt(0.5)))


def _sgu_kernel(u_ref, v_ref, lng_ref, lnb_ref, w_ref, b_ref, o_ref):
    rows, width = v_ref.shape
    dg = width // SGU_GROUPS
    v = _gelu(v_ref[...].astype(F32))
    mu = jnp.mean(v, axis=-1, keepdims=True)
    var = jnp.mean(jnp.square(v - mu), axis=-1, keepdims=True)
    vn = ((v - mu) * lax.rsqrt(var + EPS) * lng_ref[...] + lnb_ref[...]).astype(BF16)
    u = _gelu(u_ref[...].astype(F32))
    ri = lax.broadcasted_iota(I32, (SGU_WINDOW, SGU_WINDOW), 0) // CHUNK
    ci = lax.broadcasted_iota(I32, (SGU_WINDOW, SGU_WINDOW), 1) // CHUNK
    allowed = ci <= ri
    for g in range(SGU_GROUPS):
        wg = jnp.where(allowed, w_ref[g], 0.0).astype(BF16)
        bias = b_ref[g]
        for w in range(rows // SGU_WINDOW):
            rs = slice(w * SGU_WINDOW, (w + 1) * SGU_WINDOW)
            cs = slice(g * dg, (g + 1) * dg)
            mixed = jnp.dot(wg, vn[rs, cs], preferred_element_type=F32) + bias
            o_ref[rs, cs] = (u[rs, cs] * mixed).astype(BF16)


def _sgu(z, ln_g, ln_b, w_s, b_s, *, rows):
    n = z.shape[0]
    gw = z.shape[1] // 6
    return pl.pallas_call(
        _sgu_kernel,
        grid=(n // rows,),
        in_specs=[
            pl.BlockSpec((rows, gw), lambda i: (i, 4)),
            pl.BlockSpec((rows, gw), lambda i: (i, 5)),
            pl.BlockSpec((1, gw), lambda i: (0, 0)),
            pl.BlockSpec((1, gw), lambda i: (0, 0)),
            pl.BlockSpec((SGU_GROUPS, SGU_WINDOW, SGU_WINDOW), lambda i: (0, 0, 0)),
            pl.BlockSpec((SGU_GROUPS, SGU_WINDOW, 1), lambda i: (0, 0, 0)),
        ],
        out_specs=pl.BlockSpec((rows, gw), lambda i: (i, 0)),
        out_shape=jax.ShapeDtypeStruct((n, gw), BF16),
        compiler_params=_params(("arbitrary",)),
        name="sgu",
    )(z, z, ln_g.reshape(1, gw), ln_b.reshape(1, gw), w_s, b_s.reshape(SGU_GROUPS, SGU_WINDOW, 1))


def _hgrn_head(q, f_logits, v, g, lb, ng, tri, st_ref, sh_ref):
    rows, dk = q.shape
    f = lb + (1.0 - lb) * jax.nn.sigmoid(f_logits)
    lf = jnp.log(f)
    kk = 1.0 - f
    qa = _silu(q)

    bcum = None
    rest = lf
    for _ in range(3):
        term = rest.astype(BF16)
        part = jnp.dot(tri, term, preferred_element_type=F32)
        bcum = part if bcum is None else bcum + part
        rest = rest - term.astype(F32)

    row = lax.broadcasted_iota(I32, (rows, dk), 0)
    ti = lax.broadcasted_iota(I32, (rows, rows), 0)
    si = lax.broadcasted_iota(I32, (rows, rows), 1)
    attn = jnp.zeros((rows, rows), F32)
    hs = rows // 2
    while hs >= HG_FINE:
        bs = 2 * hs
        parts = [jnp.broadcast_to(bcum[b * bs + hs - 1:b * bs + hs, :], (bs, dk))
                 for b in range(rows // bs)]
        anchor = parts[0] if len(parts) == 1 else jnp.concatenate(parts, axis=0)
        upper = (row & (bs - 1)) >= hs
        fac = jnp.exp(-jnp.abs(bcum - anchor))
        qt = jnp.where(upper, qa * fac, 0.0)
        kt = jnp.where(upper, 0.0, kk * fac)
        a = lax.dot_general(qt.astype(BF16), kt.astype(BF16), _NT, preferred_element_type=F32)
        if bs < rows:
            a = jnp.where((ti & -bs) == (si & -bs), a, 0.0)
        attn = attn + a
        hs //= 2

    vb = v.astype(BF16)
    near = qa * kk
    intra = jnp.sum(near, axis=1, keepdims=True) * v
    pad = jnp.zeros((HG_FINE, dk), F32)
    for idx, val in enumerate((kk, bcum, v)):
        sh_ref[idx, 0:HG_FINE, :] = pad
        sh_ref[idx, HG_FINE:, :] = val
    for delta in range(1, HG_FINE):
        back = slice(HG_FINE - delta, HG_FINE - delta + rows)
        prod = qa * sh_ref[0, back, :] * jnp.exp(jnp.minimum(bcum - sh_ref[1, back, :], 0.0))
        prod = jnp.where((row & (HG_FINE - 1)) >= delta, prod, 0.0)
        intra = intra + jnp.sum(prod, axis=1, keepdims=True) * sh_ref[2, back, :]
    intra = intra + jnp.dot(attn.astype(BF16), vb, preferred_element_type=F32)
    st = st_ref[...]
    cross = lax.dot_general((qa * jnp.exp(bcum)).astype(BF16), st.astype(BF16), _NT,
                            preferred_element_type=F32)
    blast = bcum[rows - 1:rows, :]
    kb = (kk * jnp.exp(blast - bcum)).astype(BF16)
    st_ref[...] = st * jnp.exp(blast) + lax.dot_general(vb, kb, _TN, preferred_element_type=F32)
    return _rms(intra + cross) * ng * _silu(g)


def _hgrn_lower_bound(lb_raw, layer):
    e = jnp.exp(lb_raw - jnp.max(lb_raw, axis=0, keepdims=True))
    soft = e / jnp.sum(e, axis=0, keepdims=True)
    return jnp.sum(soft[1:layer + 1], axis=0, keepdims=True)


def _inproj_hgrn_kernel(x_ref, g_ref, w_ref, ws_ref, lb_ref, ng_ref, tri_ref, o_ref, zd_ref,
                        z_ref, st_ref, sh_ref, *, layer):
    c = pl.program_id(1)
    gw = o_ref.shape[1]
    dk = gw // HG_HEADS

    @pl.when((pl.program_id(0) == 0) & (c == 0))
    def _():
        z_ref[...] = jnp.zeros_like(z_ref)

    @pl.when(c <= 1)
    def _():
        st_ref[...] = jnp.zeros_like(st_ref)

    h = (_rms(x_ref[...]) * g_ref[...]).astype(BF16)
    z_ref[c % 2] = jnp.dot(h, w_ref[...], preferred_element_type=F32)
    zd_ref[...] = jnp.dot(h, ws_ref[...], preferred_element_type=F32)

    prev = (c + 1) % 2
    lb = _hgrn_lower_bound(lb_ref[...], layer)
    tri = tri_ref[...]
    for hd in range(HG_HEADS):
        cols = [slice(part * gw + hd * dk, part * gw + (hd + 1) * dk) for part in range(4)]
        hs = slice(hd * dk, (hd + 1) * dk)
        out = _hgrn_head(z_ref[prev, :, cols[0]], z_ref[prev, :, cols[1]], z_ref[prev, :, cols[2]],
                         z_ref[prev, :, cols[3]], lb[:, hs], ng_ref[:, hs], tri, st_ref.at[hd], sh_ref.at[hd])
        o_ref[:, hs] = out.astype(BF16)


def _inproj_hgrn(x, g, w, layer_idx, w_side, lb_raw, norm_g, batch, seq, layer):
    n, d = x.shape
    gw = norm_g.shape[0]
    dk = gw // HG_HEADS
    blk = HG_BLOCK
    nblk = seq // blk
    ns = w_side.shape[1]
    depth = lb_raw.shape[0]
    const = lambda b, c: (0, 0)
    return pl.pallas_call(
        functools.partial(_inproj_hgrn_kernel, layer=layer),
        grid=(batch, nblk + 1),
        in_specs=[
            pl.BlockSpec((blk, d), lambda b, c: (b * nblk + jnp.minimum(c, nblk - 1), 0)),
            pl.BlockSpec((1, d), const),
            pl.BlockSpec((None, d, 4 * gw), lambda b, c: (layer_idx, 0, 0), pipeline_mode=pl.Buffered(1)),
            pl.BlockSpec((d, ns), const, pipeline_mode=pl.Buffered(1)),
            pl.BlockSpec((depth, gw), const),
            pl.BlockSpec((1, gw), const),
            pl.BlockSpec((blk, blk), const),
        ],
        out_specs=[
            pl.BlockSpec((blk, gw), lambda b, c: (b * nblk + jnp.maximum(c - 1, 0), 0)),
            pl.BlockSpec((blk, ns), lambda b, c: (b * nblk + jnp.minimum(c, nblk - 1), 0)),
        ],
        out_shape=[jax.ShapeDtypeStruct((n, gw), BF16), jax.ShapeDtypeStruct((n, ns), F32)],
        scratch_shapes=[
            pltpu.VMEM((2, blk, 4 * gw), F32),
            pltpu.VMEM((HG_HEADS, dk, dk), F32),
            pltpu.VMEM((HG_HEADS, 3, blk + HG_FINE, dk), F32),
        ],
        compiler_params=_params(("arbitrary", "arbitrary"), SIDE_VMEM_LIMIT),
        name="inproj_hgrn",
    )(x, g.reshape(1, d), w, w_side, lb_raw, norm_g.reshape(1, gw), jnp.tril(jnp.ones((blk, blk), BF16)))


def _dsa_prep_kernel(zd_ref, cqg_ref, ckvg_ref, wuq_ref, qng_ref, wqit_ref,
                     q_ref, qit_ref, kv_ref, kix_ref, wht_ref):
    zd = zd_ref[...]
    cq = (_rms(zd[:, :DSA_Q_RANK]) * cqg_ref[...]).astype(BF16)
    qf = jnp.dot(cq, wuq_ref[...], preferred_element_type=F32)
    for i in range(q_ref.shape[0]):
        rs = slice(i * Q_BLOCK, (i + 1) * Q_BLOCK)
        for h in range(DSA_HEADS):
            cs = slice(h * DSA_KV_RANK, (h + 1) * DSA_KV_RANK)
            q_ref[i, h] = (_rms(qf[rs, cs]) * qng_ref[...] * (DSA_KV_RANK ** -0.5 * LOG2E)).astype(BF16)
    qit = lax.dot_general(wqit_ref[...], cq, _NT, preferred_element_type=F32)
    qit = (qit * (IDX_DIM ** -0.5)).astype(BF16)
    for i in range(q_ref.shape[0]):
        for h in range(IDX_HEADS):
            c = (i * IDX_HEADS + h) * Q_BLOCK
            qit_ref[:, c:c + Q_BLOCK] = qit[h * LANES:(h + 1) * LANES, i * Q_BLOCK:(i + 1) * Q_BLOCK]
    c0 = DSA_Q_RANK
    c1 = c0 + DSA_KV_RANK
    kv_ref[...] = (_rms(zd[:, c0:c1]) * ckvg_ref[...]).astype(BF16)
    kix_ref[...] = zd[:, c1:c1 + LANES].astype(BF16)
    wht = jnp.transpose(zd[:, c1 + LANES:c1 + 2 * LANES] * (IDX_HEADS ** -0.5))
    wht_ref[...] = wht[:IDX_HEADS, :]


def _dsa_prep(zd, cq_g, ckv_g, w_uq, qn_g, w_qit, *, tm):
    n, wd = zd.shape
    dq = w_uq.shape[1]
    dqi = w_qit.shape[0]
    full = lambda i: (0, 0)
    rows = lambda i: (i, 0)
    cols = lambda i: (0, i)
    return pl.pallas_call(
        _dsa_prep_kernel,
        grid=(n // tm,),
        in_specs=[
            pl.BlockSpec((tm, wd), rows),
            pl.BlockSpec((1, DSA_Q_RANK), full),
            pl.BlockSpec((1, DSA_KV_RANK), full),
            pl.BlockSpec((DSA_Q_RANK, dq), full),
            pl.BlockSpec((1, DSA_KV_RANK), full),
            pl.BlockSpec((dqi, DSA_Q_RANK), full),
        ],
        out_specs=[
            pl.BlockSpec((tm // Q_BLOCK, DSA_HEADS, Q_BLOCK, DSA_KV_RANK), lambda i: (i, 0, 0, 0)),
            pl.BlockSpec((LANES, IDX_HEADS * tm), cols),
            pl.BlockSpec((tm, DSA_KV_RANK), rows),
            pl.BlockSpec((tm, LANES), rows),
            pl.BlockSpec((IDX_HEADS, tm), cols),
        ],
        out_shape=[
            jax.ShapeDtypeStruct((n // Q_BLOCK, DSA_HEADS, Q_BLOCK, DSA_KV_RANK), BF16),
            jax.ShapeDtypeStruct((LANES, IDX_HEADS * n), BF16),
            jax.ShapeDtypeStruct((n, DSA_KV_RANK), BF16),
            jax.ShapeDtypeStruct((n, LANES), BF16),
            jax.ShapeDtypeStruct((IDX_HEADS, n), F32),
        ],
        compiler_params=_params(("arbitrary",)),
        name="dsa_prep",
    )(zd, cq_g.reshape(1, -1), ckv_g.reshape(1, -1), w_uq, qn_g.reshape(1, -1), w_qit)


def _bit_planes(key):
    rows = key.shape[0]
    v = [key[8 * k:8 * (k + 1), :] for k in range(rows // 8)]
    sub = lax.broadcasted_iota(I32, v[0].shape, 0)

    def swap(lo, hi, j, m):
        return (lo & ~m) | ((hi >> j) & m), (hi & m) | ((lo << j) & ~m)

    for g in range(0, len(v), 4):
        for a, b in ((0, 2), (1, 3)):
            v[g + a], v[g + b] = swap(v[g + a], v[g + b], 16, 0x0000FFFF)
        for a, b in ((0, 1), (2, 3)):
            v[g + a], v[g + b] = swap(v[g + a], v[g + b], 8, 0x00FF00FF)
    for j, m in ((4, 0x0F0F0F0F), (2, 0x33333333), (1, 0x55555555)):
        high = (sub & j) != 0
        keep = jnp.where(high, m, ~m)
        for k in range(len(v)):
            down, up = pltpu.roll(v[k], j, axis=0), pltpu.roll(v[k], 8 - j, axis=0)
            moved = jnp.where(high, down << j, up >> j)
            v[k] = (v[k] & keep) | (moved & ~keep)
    return jnp.concatenate(v, axis=0)


def _dsa_select_kernel(qit_ref, wht_ref, kix_ref, m_ref, key_ref, jc_ref, plane_ref, *, ksel, idx_bits):
    qb = pl.program_id(1)
    ntile = qb + 1
    ntile_all = m_ref.shape[2]
    rowi = lax.broadcasted_iota(I32, (LANES, Q_BLOCK), 0)
    coli = lax.broadcasted_iota(I32, (LANES, Q_BLOCK), 1)
    q_chunk = (qb * Q_BLOCK + coli) // CHUNK
    pairs = IDX_HEADS // 2
    w_pair = [jnp.concatenate([wht_ref[2 * p:2 * p + 1, :], wht_ref[2 * p + 1:2 * p + 2, :]], axis=1)
              for p in range(pairs)]

    def tile_rows(j):
        return pl.ds(pl.multiple_of(j * LANES, LANES), LANES)

    def admissible(j):
        return ((j * LANES + rowi) // CHUNK) <= q_chunk

    def score_tile(jj, carry):
        for t in range(2):
            j = 2 * jj + t
            kt = kix_ref[tile_rows(j), :]
            sc = None
            for p in range(pairs):
                s2 = jnp.dot(kt, qit_ref[:, 2 * p * Q_BLOCK:2 * (p + 1) * Q_BLOCK],
                             preferred_element_type=F32)
                c2 = w_pair[p] * jnp.maximum(s2, 0.0)
                c = c2[:, :Q_BLOCK] + c2[:, Q_BLOCK:]
                sc = c if sc is None else sc + c
            bits = pltpu.bitcast(sc, I32)
            key = bits ^ ((bits >> 31) & 0x7FFFFFFF)
            key = jnp.where(admissible(j), key, INT_MIN)
            key_ref[tile_rows(j), :] = key
            plane_ref[tile_rows(j), :] = _bit_planes(key)
        return carry

    @pl.when(qb == 0)
    def _():
        key_ref[...] = jnp.full(key_ref.shape, INT_MIN, I32)
        group_row = lax.broadcasted_iota(I32, plane_ref.shape, 0) & (PLANE_GROUP - 1)
        plane_ref[...] = jnp.where(group_row == 0, -1, 0)

    lax.fori_loop(0, (ntile + 1) // 2, score_tile, 0)

    def count(pred_fn):
        def body(jj, acc):
            for t in range(SCAN_TILES):
                j = SCAN_TILES * jj + t
                acc = acc + pred_fn(j, key_ref[tile_rows(j), :]).astype(I32)
            return acc
        trips = (ntile + SCAN_TILES - 1) // SCAN_TILES
        acc = lax.fori_loop(0, trips, body, jnp.zeros((LANES, Q_BLOCK), I32))
        return jnp.sum(acc, axis=0, keepdims=True)

    nword = plane_ref.shape[0] // (8 * PLANE_GROUP)

    def plane(v, i):
        return plane_ref[pl.ds(v * 8 * PLANE_GROUP + i, 8, stride=PLANE_GROUP), :]

    def total(words):
        acc = lax.population_count(words[0])
        for w in words[1:]:
            acc = acc + lax.population_count(w)
        return jnp.sum(acc, axis=0, keepdims=True)

    def radix_pair(t, c):
        alive, above, t_u = c
        i = 2 * t
        flip = jnp.where(t == 0, -1, 0)
        set1 = [a & (plane(v, i) ^ flip) for v, a in enumerate(alive)]
        clr1 = [a ^ s for a, s in zip(alive, set1)]
        p2 = [plane(v, i + 1) for v in range(nword)]
        set1_set2 = [s & p for s, p in zip(set1, p2)]
        clr1_set2 = [s & p for s, p in zip(clr1, p2)]
        n1, n11, n01 = total(set1), total(set1_set2), total(clr1_set2)
        take1 = (above + n1) >= ksel
        above = jnp.where(take1, above, above + n1)
        n2 = jnp.where(take1, n11, n01)
        take2 = (above + n2) >= ksel
        above = jnp.where(take2, above, above + n2)
        alive = tuple(
            jnp.where(take1, jnp.where(take2, ss, s ^ ss), jnp.where(take2, cs, c0 ^ cs))
            for s, c0, ss, cs in zip(set1, clr1, set1_set2, clr1_set2))
        t_u = (t_u | jnp.where(take1, jnp.left_shift(jnp.int32(1), 31 - i), 0)
               | jnp.where(take2, jnp.left_shift(jnp.int32(1), 30 - i), 0))
        return alive, above, t_u

    start = (tuple(jnp.full((8, Q_BLOCK), -1, I32) for _ in range(nword)),
             jnp.zeros((1, Q_BLOCK), I32), jnp.zeros((1, Q_BLOCK), I32))
    alive, above, t_u = lax.fori_loop(0, 16, radix_pair, start)
    thr = t_u ^ INT_MIN
    ties = lax.population_count(alive[0])
    for a in alive[1:]:
        ties = ties + lax.population_count(a)
    cnt_t = above + jnp.sum(ties, axis=0, keepdims=True)
    tied = jnp.max(jnp.where((cnt_t > ksel) & (thr > INT_MIN), 1.0, 0.0))

    jc_ref[...] = jnp.full(jc_ref.shape, 2 ** 31 - 1, I32)

    @pl.when(tied > 0.0)
    def _():
        need = ksel - count(lambda j, k: k > thr)

        def index_bit(i, j_c):
            cand = j_c | jnp.left_shift(jnp.int32(1), idx_bits - 1 - i)
            cnt = count(lambda j, k: (k == thr) & ((j * LANES + rowi) < cand))
            return jnp.where(cnt < need, cand, j_c)

        j_c = lax.fori_loop(0, idx_bits, index_bit, jnp.zeros((1, Q_BLOCK), I32))
        jc_ref[...] = jnp.broadcast_to(j_c, jc_ref.shape)

    j_c = jc_ref[0:1, :]
    eye = (rowi == coli).astype(BF16)

    group = KV_TILE // LANES

    def write_group(g, carry):
        for t in range(group):
            j = g * group + t
            k = key_ref[tile_rows(j), :]
            sel = (k > thr) | ((k == thr) & ((j * LANES + rowi) <= j_c))
            sel = jnp.where(sel & admissible(j), 1.0, 0.0).astype(BF16)
            sel_t = lax.dot_general(eye, sel, _NT, preferred_element_type=F32)
            m_ref[0, 0, j] = ((sel_t - 1.0) * -NEG_BIG).astype(BF16)
        return carry

    ngroup = (ntile + group - 1) // group
    lax.fori_loop(0, ngroup, write_group, 0)

    def blank_tile(j, carry):
        m_ref[0, 0, j] = jnp.full((Q_BLOCK, LANES), NEG_BIG, BF16)
        return carry

    lax.fori_loop(ngroup * group, ntile_all, blank_tile, 0)


def _dsa_select(qit, wht, kix, batch, seq, ksel):
    nqb = seq // Q_BLOCK
    nkt = seq // LANES
    return pl.pallas_call(
        functools.partial(_dsa_select_kernel, ksel=ksel, idx_bits=int(math.log2(seq))),
        grid=(batch, nqb),
        in_specs=[
            pl.BlockSpec((LANES, IDX_HEADS * Q_BLOCK), lambda b, q: (0, b * nqb + q)),
            pl.BlockSpec((IDX_HEADS, Q_BLOCK), lambda b, q: (0, b * nqb + q)),
            pl.BlockSpec((seq, LANES), lambda b, q: (b, 0)),
        ],
        out_specs=pl.BlockSpec((1, 1, nkt, Q_BLOCK, LANES), lambda b, q: (b, q, 0, 0, 0)),
        out_shape=jax.ShapeDtypeStruct((batch, nqb, nkt, Q_BLOCK, LANES), BF16),
        scratch_shapes=[pltpu.VMEM((seq, Q_BLOCK), I32), pltpu.VMEM((8, Q_BLOCK), I32),
                        pltpu.VMEM((seq, Q_BLOCK), I32)],
        compiler_params=_params(("arbitrary", "arbitrary")),
        name="dsa_select",
    )(qit, wht, kix)


def _rel_bucket(rel):
    nb = REL_BUCKETS // 2
    max_exact = nb // 2
    ret = jnp.where(rel > 0, nb, 0)
    n = jnp.abs(rel)
    nf = jnp.maximum(n, 1).astype(F32)
    large = max_exact + (jnp.log(nf / max_exact) / math.log(REL_MAX_DIST / max_exact)
                         * (nb - max_exact)).astype(I32)
    large = jnp.minimum(large, nb - 1)
    return ret + jnp.where(n < max_exact, n, large)


NEAR_TILES = 3


def _dsa_attn_kernel(q_ref, kv_ref, mask_ref, rb_ref, wuv_ref, o_ref,
                     m_ref, l_ref, alpha_ref, acc_ref, corr_ref, s_ref, p_ref, madd_ref):
    b, qb = pl.program_id(0), pl.program_id(1)
    sub = KV_TILE // LANES
    far_bucket = REL_BUCKETS // 2 - 1
    half = DSA_HEADS * Q_BLOCK // 2

    def head_rows(h):
        return slice(h * Q_BLOCK, (h + 1) * Q_BLOCK)

    @pl.when((b == 0) & (qb == 0))
    def _():
        ti = lax.broadcasted_iota(I32, (Q_BLOCK, LANES), 0)
        si = lax.broadcasted_iota(I32, (Q_BLOCK, LANES), 1)
        for oi in range(NEAR_TILES):
            bucket = _rel_bucket((oi - (NEAR_TILES - 1)) * LANES + si - ti)
            for h in range(DSA_HEADS):
                tbl = jnp.zeros((Q_BLOCK, LANES), F32)
                for bk in range(REL_BUCKETS):
                    tbl = jnp.where(bucket == bk, rb_ref[bk, h], tbl)
                corr_ref[oi, head_rows(h), :] = (tbl - rb_ref[far_bucket, h]) * LOG2E

    m_ref[...] = jnp.full_like(m_ref, NEG_BIG)
    l_ref[...] = jnp.zeros_like(l_ref)
    acc_ref[...] = jnp.zeros_like(acc_ref)
    q_all = q_ref[0].reshape(DSA_HEADS * Q_BLOCK, DSA_KV_RANK)

    def key_step(kt, carry):
        kvt = kv_ref[pl.ds(pl.multiple_of(kt * KV_TILE, KV_TILE), KV_TILE), :]
        for part in range(2):
            rs = slice(part * half, (part + 1) * half)
            s_ref[rs, :] = lax.dot_general(q_all[rs], kvt, _NT, preferred_element_type=F32)
        for j in range(sub):
            d = kt * sub + j - qb

            @pl.when((d > -NEAR_TILES) & (d <= 0))
            def _(j=j, d=d):
                s_ref[:, j * LANES:(j + 1) * LANES] += corr_ref[d + NEAR_TILES - 1]

        for j in range(sub):
            madd_ref[:, j * LANES:(j + 1) * LANES] = mask_ref[0, 0, kt * sub + j].astype(F32)
        groups_per_head = Q_BLOCK // SM_ROWS
        for g in range(DSA_HEADS * groups_per_head):
            rs = slice(g * SM_ROWS, (g + 1) * SM_ROWS)
            qg = g % groups_per_head
            s = s_ref[rs, :] + madd_ref[qg * SM_ROWS:(qg + 1) * SM_ROWS, :]
            m_old = m_ref[rs, :]
            m_new = jnp.maximum(m_old, jnp.max(s, axis=1, keepdims=True))
            alpha = jnp.exp2(m_old - m_new)
            p = jnp.exp2(s - jnp.tile(m_new, (1, sub)))
            l_ref[rs, :] = alpha * l_ref[rs, :] + jnp.sum(p, axis=1, keepdims=True)
            alpha_ref[rs, :] = alpha
            p_ref[rs, :] = p.astype(BF16)
            m_ref[rs, :] = m_new
        for part in range(2):
            rs = slice(part * half, (part + 1) * half)
            pv = jnp.dot(p_ref[rs, :], kvt, preferred_element_type=F32)
            acc_ref[rs, :] = jnp.tile(alpha_ref[rs, :], (1, DSA_KV_RANK // LANES)) * acc_ref[rs, :] + pv
        return carry

    lax.fori_loop(0, qb // sub + 1, key_step, 0)

    dv = wuv_ref.shape[2]
    for h in range(DSA_HEADS):
        rs = head_rows(h)
        o = (acc_ref[rs, :] / jnp.tile(l_ref[rs, :], (1, DSA_KV_RANK // LANES))).astype(BF16)
        o_ref[:, h * dv:(h + 1) * dv] = jnp.dot(o, wuv_ref[h], preferred_element_type=F32).astype(BF16)


def _dsa_attn(q, kv, mask, rel_bias, w_uv, batch, seq):
    n = kv.shape[0]
    nqb = seq // Q_BLOCK
    dv = w_uv.shape[2]
    rows = DSA_HEADS * Q_BLOCK
    return pl.pallas_call(
        _dsa_attn_kernel,
        grid=(batch, nqb),
        in_specs=[
            pl.BlockSpec((1, DSA_HEADS, Q_BLOCK, DSA_KV_RANK), lambda b, qb: (b * nqb + qb, 0, 0, 0)),
            pl.BlockSpec((seq, DSA_KV_RANK), lambda b, qb: (b, 0)),
            pl.BlockSpec((1, 1) + mask.shape[2:], lambda b, qb: (b, qb, 0, 0, 0)),
            pl.BlockSpec(memory_space=pltpu.SMEM),
            pl.BlockSpec(w_uv.shape, lambda b, qb: (0, 0, 0)),
        ],
        out_specs=pl.BlockSpec((Q_BLOCK, DSA_HEADS * dv), lambda b, qb: (b * nqb + qb, 0)),
        out_shape=jax.ShapeDtypeStruct((n, DSA_HEADS * dv), BF16),
        scratch_shapes=[
            pltpu.VMEM((rows, LANES), F32),
            pltpu.VMEM((rows, LANES), F32),
            pltpu.VMEM((rows, LANES), F32),
            pltpu.VMEM((rows, DSA_KV_RANK), F32),
            pltpu.VMEM((NEAR_TILES, rows, LANES), F32),
            pltpu.VMEM((rows, KV_TILE), F32),
            pltpu.VMEM((rows, KV_TILE), BF16),
            pltpu.VMEM((Q_BLOCK, KV_TILE), F32),
        ],
        compiler_params=_params(("arbitrary", "arbitrary")),
        name="dsa_attn",
    )(q, kv, mask, rel_bias, w_uv)


def _pad_cols(w, width):
    return jnp.pad(w, ((0, 0), (0, width - w.shape[1])))


def kernel(x, ln_mix_g, ln_ffn_g, w_ffn_gate, w_ffn_up, w_ffn_down, rel_bias, ev_w_in, ev_w_out, sgu_ln_g, sgu_ln_b, sgu_w_s, sgu_b_s, od_w_in, od_w_out, hgrn_lb, hgrn_norm_g, dsa_cq_g, dsa_ckv_g, dsa_w_uq, dsa_qnorm_g, dsa_w_qidx, dsa_w_uv):
    batch, seq, d = x.shape
    n = batch * seq
    depth = ln_mix_g.shape[0]
    ksel = min(TOPK_MAX, seq // 4)
    tm = min(PROJ_ROWS, n)
    xf = x.reshape(n, d)
    wg_all, wu_all, wd_all = (w.astype(BF16) for w in (w_ffn_gate, w_ffn_up, w_ffn_down))
    ev_in_all, od_in_all = ev_w_in.astype(BF16), od_w_in.astype(BF16)
    for layer in range(depth):
        j = layer // 2
        if layer % 2 == 0:
            z = _norm_matmul(xf, ln_mix_g[layer], ev_in_all, j, ev_in_all.shape[2], tm=tm, tn=PROJ_COLS, out_dtype=BF16)
            a1 = _retention(z, batch, seq)
            a2 = _sgu(z, sgu_ln_g[j], sgu_ln_b[j], sgu_w_s[j], sgu_b_s[j], rows=256)
            w_out = ev_w_out[j]
        else:
            w_in = od_w_in[j]
            gw = d // 2
            c = 4 * gw
            c_kidx = c + DSA_Q_RANK + DSA_KV_RANK
            w_dsa = jnp.concatenate([
                w_in[:, c:c_kidx],
                _pad_cols(w_in[:, c_kidx:c_kidx + IDX_DIM], LANES),
                _pad_cols(w_in[:, c_kidx + IDX_DIM:], LANES),
            ], axis=1).astype(BF16)
            a1, zd = _inproj_hgrn(xf, ln_mix_g[layer], od_in_all, j, w_dsa, hgrn_lb, hgrn_norm_g[j],
                                  batch, seq, layer)
            w_qit = jnp.pad(dsa_w_qidx[j].T.reshape(IDX_HEADS, IDX_DIM, DSA_Q_RANK),
                            ((0, 0), (0, LANES - IDX_DIM), (0, 0))).reshape(IDX_HEADS * LANES, DSA_Q_RANK)
            q, qit, kv, kix, wht = _dsa_prep(zd, dsa_cq_g[j], dsa_ckv_g[j], dsa_w_uq[j].astype(BF16),
                                             dsa_qnorm_g[j], w_qit.astype(BF16), tm=256)
            mask = _dsa_select(qit, wht, kix, batch, seq, ksel)
            a2 = _dsa_attn(q, kv, mask, rel_bias, dsa_w_uv[j].astype(BF16), batch, seq)
            w_out = od_w_out[j]
        xf = _outproj(a1, a2, w_out.astype(BF16), xf, tm=min(OUT_ROWS, n), tn=OUT_COLS)
        xf = _ffn(xf, ln_ffn_g[layer], wg_all, wu_all, wd_all, layer, tm=min(FFN_ROWS, n), tf=FFN_COLS)
    return xf.reshape(batch, seq, d)
```

```python
import functools
import math

import jax
import jax.numpy as jnp
from jax import lax
from jax.experimental import pallas as pl
from jax.experimental.pallas import tpu as pltpu

F32 = jnp.float32
BF16 = jnp.bfloat16
I32 = jnp.int32

EPS = 1e-6
CHUNK = 64
LANES = 128
ROPE_BASE = 10000.0
RET_HEADS = 4
SGU_WINDOW = 128
SGU_GROUPS = 4
HG_HEADS = 8
DSA_HEADS = 8
DSA_Q_RANK = 384
DSA_KV_RANK = 256
IDX_HEADS = 16
IDX_DIM = 64
TOPK_MAX = 256
Q_BLOCK = 128
KV_TILE = 512
SM_ROWS = 64
LOG2E = math.log2(math.e)
REL_BUCKETS = 32
REL_MAX_DIST = 256
NEG_BIG = -1e30
INT_MIN = -(2 ** 31)
SCAN_TILES = 2
PLANE_GROUP = 32

RET_BLOCK = 256
RET_HEADS_PER_STEP = 4
HG_BLOCK = 256
HG_FINE = 4
VMEM_LIMIT = 48 * 1024 * 1024
PROJ_ROWS = 1024
PROJ_COLS = 1024
OUT_ROWS = 512
OUT_COLS = 2048
FFN_ROWS = 1024
FFN_VMEM_LIMIT = 58 * 1024 * 1024
SIDE_VMEM_LIMIT = 56 * 1024 * 1024
FFN_COLS = 512

_NT = (((1,), (1,)), ((), ()))
_TN = (((0,), (0,)), ((), ()))


def _params(semantics, vmem_limit=VMEM_LIMIT):
    return pltpu.CompilerParams(dimension_semantics=semantics, vmem_limit_bytes=vmem_limit)


def _silu(x):
    return x * jax.nn.sigmoid(x)


def _rms(x):
    return x * lax.rsqrt(jnp.mean(x * x, axis=-1, keepdims=True) + EPS)


def _norm_matmul_kernel(x_ref, g_ref, w_ref, o_ref, h_ref):
    @pl.when(pl.program_id(1) == 0)
    def _():
        h_ref[...] = (_rms(x_ref[...]) * g_ref[...]).astype(BF16)

    o_ref[...] = jnp.dot(h_ref[...], w_ref[...], preferred_element_type=F32).astype(o_ref.dtype)


def _norm_matmul(x, g, w, layer, nout, *, tm, tn, out_dtype=F32):
    n, d = x.shape
    return pl.pallas_call(
        _norm_matmul_kernel,
        grid=(n // tm, nout // tn),
        in_specs=[
            pl.BlockSpec((tm, d), lambda i, j: (i, 0)),
            pl.BlockSpec((1, d), lambda i, j: (0, 0)),
            pl.BlockSpec((None, d, tn), lambda i, j: (layer, 0, j)),
        ],
        out_specs=pl.BlockSpec((tm, tn), lambda i, j: (i, j)),
        out_shape=jax.ShapeDtypeStruct((n, nout), out_dtype),
        scratch_shapes=[pltpu.VMEM((tm, d), BF16)],
        compiler_params=_params(("arbitrary", "arbitrary")),
        name="norm_matmul",
    )(x, g.reshape(1, d), w)


def _outproj_kernel(a1_ref, a2_ref, w1_ref, w2_ref, r_ref, o_ref):
    acc = jnp.dot(a1_ref[...], w1_ref[...], preferred_element_type=F32)
    acc += jnp.dot(a2_ref[...], w2_ref[...], preferred_element_type=F32)
    o_ref[...] = r_ref[...] + acc


def _outproj(a1, a2, w, res, *, tm, tn):
    n, half = a1.shape
    d = w.shape[1]
    return pl.pallas_call(
        _outproj_kernel,
        grid=(n // tm, d // tn),
        in_specs=[
            pl.BlockSpec((tm, half), lambda i, j: (i, 0)),
            pl.BlockSpec((tm, half), lambda i, j: (i, 0)),
            pl.BlockSpec((half, tn), lambda i, j: (0, j)),
            pl.BlockSpec((half, tn), lambda i, j: (1, j)),
            pl.BlockSpec((tm, tn), lambda i, j: (i, j)),
        ],
        out_specs=pl.BlockSpec((tm, tn), lambda i, j: (i, j)),
        out_shape=jax.ShapeDtypeStruct((n, d), F32),
        compiler_params=_params(("arbitrary", "arbitrary")),
        name="outproj",
    )(a1, a2, w, w, res)


def _ffn_kernel(x_ref, g_ref, wg_ref, wu_ref, wd_ref, o_ref, h_ref):
    @pl.when(pl.program_id(1) == 0)
    def _():
        x = x_ref[...]
        h_ref[...] = (_rms(x) * g_ref[...]).astype(BF16)
        o_ref[...] = x

    h = h_ref[...]
    a = jnp.dot(h, wg_ref[...], preferred_element_type=F32)
    u = jnp.dot(h, wu_ref[...], preferred_element_type=F32)
    act = (_silu(a) * u).astype(BF16)
    o_ref[...] += jnp.dot(act, wd_ref[...], preferred_element_type=F32)


def _ffn(x, g, wg, wu, wd, layer, *, tm, tf):
    n, d = x.shape
    dff = wg.shape[2]
    return pl.pallas_call(
        _ffn_kernel,
        grid=(n // tm, dff // tf),
        in_specs=[
            pl.BlockSpec((tm, d), lambda i, f: (i, 0)),
            pl.BlockSpec((1, d), lambda i, f: (0, 0)),
            pl.BlockSpec((None, d, tf), lambda i, f: (layer, 0, f)),
            pl.BlockSpec((None, d, tf), lambda i, f: (layer, 0, f)),
            pl.BlockSpec((None, tf, d), lambda i, f: (layer, f, 0)),
        ],
        out_specs=pl.BlockSpec((tm, d), lambda i, f: (i, 0)),
        out_shape=jax.ShapeDtypeStruct((n, d), F32),
        scratch_shapes=[pltpu.VMEM((tm, d), BF16)],
        compiler_params=_params(("arbitrary", "arbitrary"), FFN_VMEM_LIMIT),
        name="ffn",
    )(x, g.reshape(1, d), wg, wu, wd)


def _retention_kernel(q_ref, k_ref, v_ref, g_ref, cos_ref, sin_ref, d_ref, xi_ref, zeta_ref,
                      gl_ref, o_ref, state_ref):
    @pl.when(pl.program_id(2) == 0)
    def _():
        state_ref[...] = jnp.zeros_like(state_ref)

    cos = cos_ref[...]
    sin = sin_ref[...]
    half = cos.shape[1]
    dk = 2 * half

    def rot(t):
        t1, t2 = t[:, :half], t[:, half:]
        return jnp.concatenate([t1 * cos - t2 * sin, t1 * sin + t2 * cos], axis=1)

    for i in range(RET_HEADS_PER_STEP):
        cs = slice(i * dk, (i + 1) * dk)
        q = rot(q_ref[:, cs].astype(F32))
        k = rot(k_ref[:, cs].astype(F32)) * (dk ** -0.5)
        qb = q.astype(BF16)
        vb = v_ref[:, cs].astype(BF16)
        scores = lax.dot_general(qb, k.astype(BF16), _NT, preferred_element_type=F32) * d_ref[i]
        intra = jnp.dot(scores.astype(BF16), vb, preferred_element_type=F32)
        state = state_ref[i]
        cross = jnp.dot(qb, state.astype(BF16), preferred_element_type=F32) * xi_ref[i]
        kz = (k * zeta_ref[i]).astype(BF16)
        state_ref[i] = state * gl_ref[i] + lax.dot_general(kz, vb, _TN, preferred_element_type=F32)
        o_ref[:, cs] = (_rms(intra + cross) * _silu(g_ref[:, cs].astype(F32))).astype(BF16)


def _retention_tables(seq, dk):
    blk = RET_BLOCK
    pos = jnp.arange(seq, dtype=F32)
    inv = ROPE_BASE ** (-jnp.arange(0, dk, 2, dtype=F32) / dk)
    ang = pos[:, None] * inv[None, :]
    log_gamma = jnp.log(1.0 - 2.0 ** (-5.0 - jnp.arange(RET_HEADS, dtype=F32)))
    i = jnp.arange(blk)
    same = (i[:, None] // CHUNK) == (i[None, :] // CHUNK)
    earlier = (i[None, :] // CHUNK) < (i[:, None] // CHUNK)
    diff = (i[:, None] - i[None, :]).astype(F32)
    dist = jnp.where(same, jnp.abs(diff), diff)
    decay = jnp.where((same | earlier)[None], jnp.exp(log_gamma[:, None, None] * dist[None]), 0.0)
    p = jnp.arange(blk, dtype=F32)
    wide = (RET_HEADS, blk, dk)
    xi = jnp.broadcast_to(jnp.exp(log_gamma[:, None] * (p + 1.0))[:, :, None], wide)
    zeta = jnp.broadcast_to(jnp.exp(log_gamma[:, None] * (blk - 1.0 - p))[:, :, None], wide)
    g_blk = jnp.broadcast_to(jnp.exp(log_gamma * blk)[:, None, None], (RET_HEADS, 1, dk))
    return jnp.cos(ang), jnp.sin(ang), decay, xi, zeta, g_blk


def _retention(z, batch, seq):
    n = z.shape[0]
    gw = z.shape[1] // 6
    dk = gw // RET_HEADS
    blk = RET_BLOCK
    nblk = seq // blk
    cos, sin, decay, xi, zeta, g_blk = _retention_tables(seq, dk)

    hp = RET_HEADS_PER_STEP
    groups = RET_HEADS // hp

    def zspec(part):
        return pl.BlockSpec((blk, hp * dk), lambda b, h, c: (b * nblk + c, part * groups + h))

    def hspec(rows, cols):
        return pl.BlockSpec((hp, rows, cols), lambda b, h, c: (h, 0, 0))

    return pl.pallas_call(
        _retention_kernel,
        grid=(batch, groups, nblk),
        in_specs=[
            zspec(0), zspec(1), zspec(2), zspec(3),
            pl.BlockSpec((blk, dk // 2), lambda b, h, c: (c, 0)),
            pl.BlockSpec((blk, dk // 2), lambda b, h, c: (c, 0)),
            hspec(blk, blk), hspec(blk, dk), hspec(blk, dk), hspec(1, dk),
        ],
        out_specs=pl.BlockSpec((blk, hp * dk), lambda b, h, c: (b * nblk + c, h)),
        out_shape=jax.ShapeDtypeStruct((n, gw), BF16),
        scratch_shapes=[pltpu.VMEM((hp, dk, dk), F32)],
        compiler_params=_params(("arbitrary", "arbitrary", "arbitrary")),
        name="retention",
    )(z, z, z, z, cos, sin, decay, xi, zeta, g_blk)


def _gelu(x):
    return 0.5 * x * (1.0 + lax.erf(x * math.sqrt(0.5)))


def _sgu_kernel(u_ref, v_ref, lng_ref, lnb_ref, w_ref, b_ref, o_ref):
    rows, width = v_ref.shape
    dg = width // SGU_GROUPS
    v = _gelu(v_ref[...].astype(F32))
    mu = jnp.mean(v, axis=-1, keepdims=True)
    var = jnp.mean(jnp.square(v - mu), axis=-1, keepdims=True)
    vn = ((v - mu) * lax.rsqrt(var + EPS) * lng_ref[...] + lnb_ref[...]).astype(BF16)
    u = _gelu(u_ref[...].astype(F32))
    ri = lax.broadcasted_iota(I32, (SGU_WINDOW, SGU_WINDOW), 0) // CHUNK
    ci = lax.broadcasted_iota(I32, (SGU_WINDOW, SGU_WINDOW), 1) // CHUNK
    allowed = ci <= ri
    for g in range(SGU_GROUPS):
        wg = jnp.where(allowed, w_ref[g], 0.0).astype(BF16)
        bias = b_ref[g]
        for w in range(rows // SGU_WINDOW):
            rs = slice(w * SGU_WINDOW, (w + 1) * SGU_WINDOW)
            cs = slice(g * dg, (g + 1) * dg)
            mixed = jnp.dot(wg, vn[rs, cs], preferred_element_type=F32) + bias
            o_ref[rs, cs] = (u[rs, cs] * mixed).astype(BF16)


def _sgu(z, ln_g, ln_b, w_s, b_s, *, rows):
    n = z.shape[0]
    gw = z.shape[1] // 6
    return pl.pallas_call(
        _sgu_kernel,
        grid=(n // rows,),
        in_specs=[
            pl.BlockSpec((rows, gw), lambda i: (i, 4)),
            pl.BlockSpec((rows, gw), lambda i: (i, 5)),
            pl.BlockSpec((1, gw), lambda i: (0, 0)),
            pl.BlockSpec((1, gw), lambda i: (0, 0)),
            pl.BlockSpec((SGU_GROUPS, SGU_WINDOW, SGU_WINDOW), lambda i: (0, 0, 0)),
            pl.BlockSpec((SGU_GROUPS, SGU_WINDOW, 1), lambda i: (0, 0, 0)),
        ],
        out_specs=pl.BlockSpec((rows, gw), lambda i: (i, 0)),
        out_shape=jax.ShapeDtypeStruct((n, gw), BF16),
        compiler_params=_params(("arbitrary",)),
        name="sgu",
    )(z, z, ln_g.reshape(1, gw), ln_b.reshape(1, gw), w_s, b_s.reshape(SGU_GROUPS, SGU_WINDOW, 1))


def _hgrn_head(q, f_logits, v, g, lb, ng, tri, st_ref, sh_ref):
    rows, dk = q.shape
    f = lb + (1.0 - lb) * jax.nn.sigmoid(f_logits)
    lf = jnp.log(f)
    kk = 1.0 - f
    qa = _silu(q)

    bcum = None
    rest = lf
    for _ in range(3):
        term = rest.astype(BF16)
        part = jnp.dot(tri, term, preferred_element_type=F32)
        bcum = part if bcum is None else bcum + part
        rest = rest - term.astype(F32)

    row = lax.broadcasted_iota(I32, (rows, dk), 0)
    ti = lax.broadcasted_iota(I32, (rows, rows), 0)
    si = lax.broadcasted_iota(I32, (rows, rows), 1)
    attn = jnp.zeros((rows, rows), F32)
    hs = rows // 2
    while hs >= HG_FINE:
        bs = 2 * hs
        parts = [jnp.broadcast_to(bcum[b * bs + hs - 1:b * bs + hs, :], (bs, dk))
                 for b in range(rows // bs)]
        anchor = parts[0] if len(parts) == 1 else jnp.concatenate(parts, axis=0)
        upper = (row & (bs - 1)) >= hs
        fac = jnp.exp(-jnp.abs(bcum - anchor))
        qt = jnp.where(upper, qa * fac, 0.0)
        kt = jnp.where(upper, 0.0, kk * fac)
        a = lax.dot_general(qt.astype(BF16), kt.astype(BF16), _NT, preferred_element_type=F32)
        if bs < rows:
            a = jnp.where((ti & -bs) == (si & -bs), a, 0.0)
        attn = attn + a
        hs //= 2

    vb = v.astype(BF16)
    near = qa * kk
    intra = jnp.sum(near, axis=1, keepdims=True) * v
    pad = jnp.zeros((HG_FINE, dk), F32)
    for idx, val in enumerate((kk, bcum, v)):
        sh_ref[idx, 0:HG_FINE, :] = pad
        sh_ref[idx, HG_FINE:, :] = val
    for delta in range(1, HG_FINE):
        back = slice(HG_FINE - delta, HG_FINE - delta + rows)
        prod = qa * sh_ref[0, back, :] * jnp.exp(jnp.minimum(bcum - sh_ref[1, back, :], 0.0))
        prod = jnp.where((row & (HG_FINE - 1)) >= delta, prod, 0.0)
        intra = intra + jnp.sum(prod, axis=1, keepdims=True) * sh_ref[2, back, :]
    intra = intra + jnp.dot(attn.astype(BF16), vb, preferred_element_type=F32)
    st = st_ref[...]
    cross = lax.dot_general((qa * jnp.exp(bcum)).astype(BF16), st.astype(BF16), _NT,
                            preferred_element_type=F32)
    blast = bcum[rows - 1:rows, :]
    kb = (kk * jnp.exp(blast - bcum)).astype(BF16)
    st_ref[...] = st * jnp.exp(blast) + lax.dot_general(vb, kb, _TN, preferred_element_type=F32)
    return _rms(intra + cross) * ng * _silu(g)


def _hgrn_lower_bound(lb_raw, layer):
    e = jnp.exp(lb_raw - jnp.max(lb_raw, axis=0, keepdims=True))
    soft = e / jnp.sum(e, axis=0, keepdims=True)
    return jnp.sum(soft[1:layer + 1], axis=0, keepdims=True)


def _inproj_hgrn_kernel(x_ref, g_ref, w_ref, ws_ref, lb_ref, ng_ref, tri_ref, o_ref, zd_ref,
                        z_ref, st_ref, sh_ref, *, layer):
    c = pl.program_id(1)
    gw = o_ref.shape[1]
    dk = gw // HG_HEADS

    @pl.when((pl.program_id(0) == 0) & (c == 0))
    def _():
        z_ref[...] = jnp.zeros_like(z_ref)

    @pl.when(c <= 1)
    def _():
        st_ref[...] = jnp.zeros_like(st_ref)

    h = (_rms(x_ref[...]) * g_ref[...]).astype(BF16)
    z_ref[c % 2] = jnp.dot(h, w_ref[...], preferred_element_type=F32)
    zd_ref[...] = jnp.dot(h, ws_ref[...], preferred_element_type=F32)

    prev = (c + 1) % 2
    lb = _hgrn_lower_bound(lb_ref[...], layer)
    tri = tri_ref[...]
    for hd in range(HG_HEADS):
        cols = [slice(part * gw + hd * dk, part * gw + (hd + 1) * dk) for part in range(4)]
        hs = slice(hd * dk, (hd + 1) * dk)
        out = _hgrn_head(z_ref[prev, :, cols[0]], z_ref[prev, :, cols[1]], z_ref[prev, :, cols[2]],
                         z_ref[prev, :, cols[3]], lb[:, hs], ng_ref[:, hs], tri, st_ref.at[hd], sh_ref.at[hd])
        o_ref[:, hs] = out.astype(BF16)


def _inproj_hgrn(x, g, w, layer_idx, w_side, lb_raw, norm_g, batch, seq, layer):
    n, d = x.shape
    gw = norm_g.shape[0]
    dk = gw // HG_HEADS
    blk = HG_BLOCK
    nblk = seq // blk
    ns = w_side.shape[1]
    depth = lb_raw.shape[0]
    const = lambda b, c: (0, 0)
    return pl.pallas_call(
        functools.partial(_inproj_hgrn_kernel, layer=layer),
        grid=(batch, nblk + 1),
        in_specs=[
            pl.BlockSpec((blk, d), lambda b, c: (b * nblk + jnp.minimum(c, nblk - 1), 0)),
            pl.BlockSpec((1, d), const),
            pl.BlockSpec((None, d, 4 * gw), lambda b, c: (layer_idx, 0, 0), pipeline_mode=pl.Buffered(1)),
            pl.BlockSpec((d, ns), const, pipeline_mode=pl.Buffered(1)),
            pl.BlockSpec((depth, gw), const),
            pl.BlockSpec((1, gw), const),
            pl.BlockSpec((blk, blk), const),
        ],
        out_specs=[
            pl.BlockSpec((blk, gw), lambda b, c: (b * nblk + jnp.maximum(c - 1, 0), 0)),
            pl.BlockSpec((blk, ns), lambda b, c: (b * nblk + jnp.minimum(c, nblk - 1), 0)),
        ],
        out_shape=[jax.ShapeDtypeStruct((n, gw), BF16), jax.ShapeDtypeStruct((n, ns), F32)],
        scratch_shapes=[
            pltpu.VMEM((2, blk, 4 * gw), F32),
            pltpu.VMEM((HG_HEADS, dk, dk), F32),
            pltpu.VMEM((HG_HEADS, 3, blk + HG_FINE, dk), F32),
        ],
        compiler_params=_params(("arbitrary", "arbitrary"), SIDE_VMEM_LIMIT),
        name="inproj_hgrn",
    )(x, g.reshape(1, d), w, w_side, lb_raw, norm_g.reshape(1, gw), jnp.tril(jnp.ones((blk, blk), BF16)))


def _dsa_prep_kernel(zd_ref, cqg_ref, ckvg_ref, wuq_ref, qng_ref, wqit_ref,
                     q_ref, qit_ref, kv_ref, kix_ref, wht_ref):
    zd = zd_ref[...]
    cq = (_rms(zd[:, :DSA_Q_RANK]) * cqg_ref[...]).astype(BF16)
    qf = jnp.dot(cq, wuq_ref[...], preferred_element_type=F32)
    for i in range(q_ref.shape[0]):
        rs = slice(i * Q_BLOCK, (i + 1) * Q_BLOCK)
        for h in range(DSA_HEADS):
            cs = slice(h * DSA_KV_RANK, (h + 1) * DSA_KV_RANK)
            q_ref[i, h] = (_rms(qf[rs, cs]) * qng_ref[...] * (DSA_KV_RANK ** -0.5 * LOG2E)).astype(BF16)
    qit = lax.dot_general(wqit_ref[...], cq, _NT, preferred_element_type=F32)
    qit = (qit * (IDX_DIM ** -0.5)).astype(BF16)
    for i in range(q_ref.shape[0]):
        for h in range(IDX_HEADS):
            c = (i * IDX_HEADS + h) * Q_BLOCK
            qit_ref[:, c:c + Q_BLOCK] = qit[h * LANES:(h + 1) * LANES, i * Q_BLOCK:(i + 1) * Q_BLOCK]
    c0 = DSA_Q_RANK
    c1 = c0 + DSA_KV_RANK
    kv_ref[...] = (_rms(zd[:, c0:c1]) * ckvg_ref[...]).astype(BF16)
    kix_ref[...] = zd[:, c1:c1 + LANES].astype(BF16)
    wht = jnp.transpose(zd[:, c1 + LANES:c1 + 2 * LANES] * (IDX_HEADS ** -0.5))
    wht_ref[...] = wht[:IDX_HEADS, :]


def _dsa_prep(zd, cq_g, ckv_g, w_uq, qn_g, w_qit, *, tm):
    n, wd = zd.shape
    dq = w_uq.shape[1]
    dqi = w_qit.shape[0]
    full = lambda i: (0, 0)
    rows = lambda i: (i, 0)
    cols = lambda i: (0, i)
    return pl.pallas_call(
        _dsa_prep_kernel,
        grid=(n // tm,),
        in_specs=[
            pl.BlockSpec((tm, wd), rows),
            pl.BlockSpec((1, DSA_Q_RANK), full),
            pl.BlockSpec((1, DSA_KV_RANK), full),
            pl.BlockSpec((DSA_Q_RANK, dq), full),
            pl.BlockSpec((1, DSA_KV_RANK), full),
            pl.BlockSpec((dqi, DSA_Q_RANK), full),
        ],
        out_specs=[
            pl.BlockSpec((tm // Q_BLOCK, DSA_HEADS, Q_BLOCK, DSA_KV_RANK), lambda i: (i, 0, 0, 0)),
            pl.BlockSpec((LANES, IDX_HEADS * tm), cols),
            pl.BlockSpec((tm, DSA_KV_RANK), rows),
            pl.BlockSpec((tm, LANES), rows),
            pl.BlockSpec((IDX_HEADS, tm), cols),
        ],
        out_shape=[
            jax.ShapeDtypeStruct((n // Q_BLOCK, DSA_HEADS, Q_BLOCK, DSA_KV_RANK), BF16),
            jax.ShapeDtypeStruct((LANES, IDX_HEADS * n), BF16),
            jax.ShapeDtypeStruct((n, DSA_KV_RANK), BF16),
            jax.ShapeDtypeStruct((n, LANES), BF16),
            jax.ShapeDtypeStruct((IDX_HEADS, n), F32),
        ],
        compiler_params=_params(("arbitrary",)),
        name="dsa_prep",
    )(zd, cq_g.reshape(1, -1), ckv_g.reshape(1, -1), w_uq, qn_g.reshape(1, -1), w_qit)


def _bit_planes(key):
    rows = key.shape[0]
    v = [key[8 * k:8 * (k + 1), :] for k in range(rows // 8)]
    sub = lax.broadcasted_iota(I32, v[0].shape, 0)

    def swap(lo, hi, j, m):
        return (lo & ~m) | ((hi >> j) & m), (hi & m) | ((lo << j) & ~m)

    for g in range(0, len(v), 4):
        for a, b in ((0, 2), (1, 3)):
            v[g + a], v[g + b] = swap(v[g + a], v[g + b], 16, 0x0000FFFF)
        for a, b in ((0, 1), (2, 3)):
            v[g + a], v[g + b] = swap(v[g + a], v[g + b], 8, 0x00FF00FF)
    for j, m in ((4, 0x0F0F0F0F), (2, 0x33333333), (1, 0x55555555)):
        high = (sub & j) != 0
        keep = jnp.where(high, m, ~m)
        for k in range(len(v)):
            down, up = pltpu.roll(v[k], j, axis=0), pltpu.roll(v[k], 8 - j, axis=0)
            moved = jnp.where(high, down << j, up >> j)
            v[k] = (v[k] & keep) | (moved & ~keep)
    return jnp.concatenate(v, axis=0)


def _dsa_select_kernel(qit_ref, wht_ref, kix_ref, wg_in, wu_in, wd_in, m_ref, wg_out, wu_out, wd_out,
                       key_ref, jc_ref, plane_ref, *, ksel, idx_bits):
    for src, dst in ((wg_in, wg_out), (wu_in, wu_out), (wd_in, wd_out)):
        dst[...] = src[...].astype(BF16)

    qb = pl.program_id(1)
    ntile = qb + 1
    ntile_all = m_ref.shape[2]
    rowi = lax.broadcasted_iota(I32, (LANES, Q_BLOCK), 0)
    coli = lax.broadcasted_iota(I32, (LANES, Q_BLOCK), 1)
    q_chunk = (qb * Q_BLOCK + coli) // CHUNK
    pairs = IDX_HEADS // 2
    w_pair = [jnp.concatenate([wht_ref[2 * p:2 * p + 1, :], wht_ref[2 * p + 1:2 * p + 2, :]], axis=1)
              for p in range(pairs)]

    def tile_rows(j):
        return pl.ds(pl.multiple_of(j * LANES, LANES), LANES)

    def admissible(j):
        return ((j * LANES + rowi) // CHUNK) <= q_chunk

    def score_tile(jj, carry):
        for t in range(2):
            j = 2 * jj + t
            kt = kix_ref[tile_rows(j), :]
            sc = None
            for p in range(pairs):
                s2 = jnp.dot(kt, qit_ref[:, 2 * p * Q_BLOCK:2 * (p + 1) * Q_BLOCK],
                             preferred_element_type=F32)
                c2 = w_pair[p] * jnp.maximum(s2, 0.0)
                c = c2[:, :Q_BLOCK] + c2[:, Q_BLOCK:]
                sc = c if sc is None else sc + c
            bits = pltpu.bitcast(sc, I32)
            key = bits ^ ((bits >> 31) & 0x7FFFFFFF)
            key = jnp.where(admissible(j), key, INT_MIN)
            key_ref[tile_rows(j), :] = key
            plane_ref[tile_rows(j), :] = _bit_planes(key)
        return carry

    @pl.when(qb == 0)
    def _():
        key_ref[...] = jnp.full(key_ref.shape, INT_MIN, I32)
        group_row = lax.broadcasted_iota(I32, plane_ref.shape, 0) & (PLANE_GROUP - 1)
        plane_ref[...] = jnp.where(group_row == 0, -1, 0)

    lax.fori_loop(0, (ntile + 1) // 2, score_tile, 0)

    def count(pred_fn):
        def body(jj, acc):
            for t in range(SCAN_TILES):
                j = SCAN_TILES * jj + t
                acc = acc + pred_fn(j, key_ref[tile_rows(j), :]).astype(I32)
            return acc
        trips = (ntile + SCAN_TILES - 1) // SCAN_TILES
        acc = lax.fori_loop(0, trips, body, jnp.zeros((LANES, Q_BLOCK), I32))
        return jnp.sum(acc, axis=0, keepdims=True)

    nword = plane_ref.shape[0] // (8 * PLANE_GROUP)

    def plane(v, i):
        return plane_ref[pl.ds(v * 8 * PLANE_GROUP + i, 8, stride=PLANE_GROUP), :]

    def total(words):
        acc = lax.population_count(words[0])
        for w in words[1:]:
            acc = acc + lax.population_count(w)
        return jnp.sum(acc, axis=0, keepdims=True)

    def radix_pair(t, c):
        alive, above, t_u = c
        i = 2 * t
        flip = jnp.where(t == 0, -1, 0)
        set1 = [a & (plane(v, i) ^ flip) for v, a in enumerate(alive)]
        clr1 = [a ^ s for a, s in zip(alive, set1)]
        p2 = [plane(v, i + 1) for v in range(nword)]
        set1_set2 = [s & p for s, p in zip(set1, p2)]
        clr1_set2 = [s & p for s, p in zip(clr1, p2)]
        n1, n11, n01 = total(set1), total(set1_set2), total(clr1_set2)
        take1 = (above + n1) >= ksel
        above = jnp.where(take1, above, above + n1)
        n2 = jnp.where(take1, n11, n01)
        take2 = (above + n2) >= ksel
        above = jnp.where(take2, above, above + n2)
        alive = tuple(
            jnp.where(take1, jnp.where(take2, ss, s ^ ss), jnp.where(take2, cs, c0 ^ cs))
            for s, c0, ss, cs in zip(set1, clr1, set1_set2, clr1_set2))
        t_u = (t_u | jnp.where(take1, jnp.left_shift(jnp.int32(1), 31 - i), 0)
               | jnp.where(take2, jnp.left_shift(jnp.int32(1), 30 - i), 0))
        return alive, above, t_u

    start = (tuple(jnp.full((8, Q_BLOCK), -1, I32) for _ in range(nword)),
             jnp.zeros((1, Q_BLOCK), I32), jnp.zeros((1, Q_BLOCK), I32))
    alive, above, t_u = lax.fori_loop(0, 16, radix_pair, start)
    thr = t_u ^ INT_MIN
    ties = lax.population_count(alive[0])
    for a in alive[1:]:
        ties = ties + lax.population_count(a)
    cnt_t = above + jnp.sum(ties, axis=0, keepdims=True)
    tied = jnp.max(jnp.where((cnt_t > ksel) & (thr > INT_MIN), 1.0, 0.0))

    jc_ref[...] = jnp.full(jc_ref.shape, 2 ** 31 - 1, I32)

    @pl.when(tied > 0.0)
    def _():
        need = ksel - count(lambda j, k: k > thr)

        def index_bit(i, j_c):
            cand = j_c | jnp.left_shift(jnp.int32(1), idx_bits - 1 - i)
            cnt = count(lambda j, k: (k == thr) & ((j * LANES + rowi) < cand))
            return jnp.where(cnt < need, cand, j_c)

        j_c = lax.fori_loop(0, idx_bits, index_bit, jnp.zeros((1, Q_BLOCK), I32))
        jc_ref[...] = jnp.broadcast_to(j_c, jc_ref.shape)

    j_c = jc_ref[0:1, :]
    eye = (rowi == coli).astype(BF16)

    group = KV_TILE // LANES

    def write_group(g, carry):
        for t in range(group):
            j = g * group + t
            k = key_ref[tile_rows(j), :]
            sel = (k > thr) | ((k == thr) & ((j * LANES + rowi) <= j_c))
            sel = jnp.where(sel & admissible(j), 1.0, 0.0).astype(BF16)
            sel_t = lax.dot_general(eye, sel, _NT, preferred_element_type=F32)
            m_ref[0, 0, j] = ((sel_t - 1.0) * -NEG_BIG).astype(BF16)
        return carry

    ngroup = (ntile + group - 1) // group
    lax.fori_loop(0, ngroup, write_group, 0)

    def blank_tile(j, carry):
        m_ref[0, 0, j] = jnp.full((Q_BLOCK, LANES), NEG_BIG, BF16)
        return carry

    lax.fori_loop(ngroup * group, ntile_all, blank_tile, 0)


def _cast_blocks(rows, steps):
    units = rows // 16
    blocks = max(k for k in range(1, min(units, steps) + 1) if units % k == 0)
    return rows // blocks, blocks


def _dsa_select(qit, wht, kix, ffn_weights, layer, batch, seq, ksel):
    nqb = seq // Q_BLOCK
    nkt = seq // LANES
    steps = batch * nqb

    def cast_specs(w):
        rows, blocks = _cast_blocks(w.shape[1], steps)
        index = lambda b, q: (jnp.minimum(b * nqb + q, blocks - 1), 0)
        return (pl.BlockSpec((None, rows, w.shape[2]), lambda b, q: (layer,) + index(b, q)),
                pl.BlockSpec((rows, w.shape[2]), index),
                jax.ShapeDtypeStruct(w.shape[1:], BF16))

    w_in, w_out, w_shape = zip(*(cast_specs(w) for w in ffn_weights))
    return pl.pallas_call(
        functools.partial(_dsa_select_kernel, ksel=ksel, idx_bits=int(math.log2(seq))),
        grid=(batch, nqb),
        in_specs=[
            pl.BlockSpec((LANES, IDX_HEADS * Q_BLOCK), lambda b, q: (0, b * nqb + q)),
            pl.BlockSpec((IDX_HEADS, Q_BLOCK), lambda b, q: (0, b * nqb + q)),
            pl.BlockSpec((seq, LANES), lambda b, q: (b, 0)),
            *w_in,
        ],
        out_specs=[pl.BlockSpec((1, 1, nkt, Q_BLOCK, LANES), lambda b, q: (b, q, 0, 0, 0)), *w_out],
        out_shape=[jax.ShapeDtypeStruct((batch, nqb, nkt, Q_BLOCK, LANES), BF16), *w_shape],
        scratch_shapes=[pltpu.VMEM((seq, Q_BLOCK), I32), pltpu.VMEM((8, Q_BLOCK), I32),
                        pltpu.VMEM((seq, Q_BLOCK), I32)],
        compiler_params=_params(("arbitrary", "arbitrary")),
        name="dsa_select",
    )(qit, wht, kix, *ffn_weights)


def _rel_bucket(rel):
    nb = REL_BUCKETS // 2
    max_exact = nb // 2
    ret = jnp.where(rel > 0, nb, 0)
    n = jnp.abs(rel)
    nf = jnp.maximum(n, 1).astype(F32)
    large = max_exact + (jnp.log(nf / max_exact) / math.log(REL_MAX_DIST / max_exact)
                         * (nb - max_exact)).astype(I32)
    large = jnp.minimum(large, nb - 1)
    return ret + jnp.where(n < max_exact, n, large)


NEAR_TILES = 3


def _dsa_attn_kernel(q_ref, kv_ref, mask_ref, rb_ref, wuv_ref, o_ref,
                     m_ref, l_ref, alpha_ref, acc_ref, corr_ref, s_ref, p_ref, madd_ref):
    b, qb = pl.program_id(0), pl.program_id(1)
    sub = KV_TILE // LANES
    far_bucket = REL_BUCKETS // 2 - 1
    half = DSA_HEADS * Q_BLOCK // 2

    def head_rows(h):
        return slice(h * Q_BLOCK, (h + 1) * Q_BLOCK)

    @pl.when((b == 0) & (qb == 0))
    def _():
        ti = lax.broadcasted_iota(I32, (Q_BLOCK, LANES), 0)
        si = lax.broadcasted_iota(I32, (Q_BLOCK, LANES), 1)
        for oi in range(NEAR_TILES):
            bucket = _rel_bucket((oi - (NEAR_TILES - 1)) * LANES + si - ti)
            for h in range(DSA_HEADS):
                tbl = jnp.zeros((Q_BLOCK, LANES), F32)
                for bk in range(REL_BUCKETS):
                    tbl = jnp.where(bucket == bk, rb_ref[bk, h], tbl)
                corr_ref[oi, head_rows(h), :] = (tbl - rb_ref[far_bucket, h]) * LOG2E

    m_ref[...] = jnp.full_like(m_ref, NEG_BIG)
    l_ref[...] = jnp.zeros_like(l_ref)
    acc_ref[...] = jnp.zeros_like(acc_ref)
    q_all = q_ref[0].reshape(DSA_HEADS * Q_BLOCK, DSA_KV_RANK)

    def key_step(kt, carry):
        kvt = kv_ref[pl.ds(pl.multiple_of(kt * KV_TILE, KV_TILE), KV_TILE), :]
        for part in range(2):
            rs = slice(part * half, (part + 1) * half)
            s_ref[rs, :] = lax.dot_general(q_all[rs], kvt, _NT, preferred_element_type=F32)
        for j in range(sub):
            d = kt * sub + j - qb

            @pl.when((d > -NEAR_TILES) & (d <= 0))
            def _(j=j, d=d):
                s_ref[:, j * LANES:(j + 1) * LANES] += corr_ref[d + NEAR_TILES - 1]

        for j in range(sub):
            madd_ref[:, j * LANES:(j + 1) * LANES] = mask_ref[0, 0, kt * sub + j].astype(F32)
        groups_per_head = Q_BLOCK // SM_ROWS
        for g in range(DSA_HEADS * groups_per_head):
            rs = slice(g * SM_ROWS, (g + 1) * SM_ROWS)
            qg = g % groups_per_head
            s = s_ref[rs, :] + madd_ref[qg * SM_ROWS:(qg + 1) * SM_ROWS, :]
            m_old = m_ref[rs, :]
            m_new = jnp.maximum(m_old, jnp.max(s, axis=1, keepdims=True))
            alpha = jnp.exp2(m_old - m_new)
            p = jnp.exp2(s - jnp.tile(m_new, (1, sub)))
            l_ref[rs, :] = alpha * l_ref[rs, :] + jnp.sum(p, axis=1, keepdims=True)
            alpha_ref[rs, :] = alpha
            p_ref[rs, :] = p.astype(BF16)
            m_ref[rs, :] = m_new
        for part in range(2):
            rs = slice(part * half, (part + 1) * half)
            pv = jnp.dot(p_ref[rs, :], kvt, preferred_element_type=F32)
            acc_ref[rs, :] = jnp.tile(alpha_ref[rs, :], (1, DSA_KV_RANK // LANES)) * acc_ref[rs, :] + pv
        return carry

    lax.fori_loop(0, qb // sub + 1, key_step, 0)

    dv = wuv_ref.shape[2]
    for h in range(DSA_HEADS):
        rs = head_rows(h)
        o = (acc_ref[rs, :] / jnp.tile(l_ref[rs, :], (1, DSA_KV_RANK // LANES))).astype(BF16)
        o_ref[:, h * dv:(h + 1) * dv] = jnp.dot(o, wuv_ref[h], preferred_element_type=F32).astype(BF16)


def _dsa_attn(q, kv, mask, rel_bias, w_uv, batch, seq):
    n = kv.shape[0]
    nqb = seq // Q_BLOCK
    dv = w_uv.shape[2]
    rows = DSA_HEADS * Q_BLOCK
    return pl.pallas_call(
        _dsa_attn_kernel,
        grid=(batch, nqb),
        in_specs=[
            pl.BlockSpec((1, DSA_HEADS, Q_BLOCK, DSA_KV_RANK), lambda b, qb: (b * nqb + qb, 0, 0, 0)),
            pl.BlockSpec((seq, DSA_KV_RANK), lambda b, qb: (b, 0)),
            pl.BlockSpec((1, 1) + mask.shape[2:], lambda b, qb: (b, qb, 0, 0, 0)),
            pl.BlockSpec(memory_space=pltpu.SMEM),
            pl.BlockSpec(w_uv.shape, lambda b, qb: (0, 0, 0)),
        ],
        out_specs=pl.BlockSpec((Q_BLOCK, DSA_HEADS * dv), lambda b, qb: (b * nqb + qb, 0)),
        out_shape=jax.ShapeDtypeStruct((n, DSA_HEADS * dv), BF16),
        scratch_shapes=[
            pltpu.VMEM((rows, LANES), F32),
            pltpu.VMEM((rows, LANES), F32),
            pltpu.VMEM((rows, LANES), F32),
            pltpu.VMEM((rows, DSA_KV_RANK), F32),
            pltpu.VMEM((NEAR_TILES, rows, LANES), F32),
            pltpu.VMEM((rows, KV_TILE), F32),
            pltpu.VMEM((rows, KV_TILE), BF16),
            pltpu.VMEM((Q_BLOCK, KV_TILE), F32),
        ],
        compiler_params=_params(("arbitrary", "arbitrary")),
        name="dsa_attn",
    )(q, kv, mask, rel_bias, w_uv)


def _pad_cols(w, width):
    return jnp.pad(w, ((0, 0), (0, width - w.shape[1])))


def kernel(x, ln_mix_g, ln_ffn_g, w_ffn_gate, w_ffn_up, w_ffn_down, rel_bias, ev_w_in, ev_w_out, sgu_ln_g, sgu_ln_b, sgu_w_s, sgu_b_s, od_w_in, od_w_out, hgrn_lb, hgrn_norm_g, dsa_cq_g, dsa_ckv_g, dsa_w_uq, dsa_qnorm_g, dsa_w_qidx, dsa_w_uv):
    batch, seq, d = x.shape
    n = batch * seq
    depth = ln_mix_g.shape[0]
    ksel = min(TOPK_MAX, seq // 4)
    tm = min(PROJ_ROWS, n)
    xf = x.reshape(n, d)
    ffn_f32 = (w_ffn_gate, w_ffn_up, w_ffn_down)
    ev_in_all, od_in_all = ev_w_in.astype(BF16), od_w_in.astype(BF16)
    for layer in range(depth):
        j = layer // 2
        if layer % 2 == 0:
            z = _norm_matmul(xf, ln_mix_g[layer], ev_in_all, j, ev_in_all.shape[2], tm=tm, tn=PROJ_COLS, out_dtype=BF16)
            a1 = _retention(z, batch, seq)
            a2 = _sgu(z, sgu_ln_g[j], sgu_ln_b[j], sgu_w_s[j], sgu_b_s[j], rows=256)
            w_out = ev_w_out[j]
            ffn_w = [w[layer].astype(BF16) for w in ffn_f32]
        else:
            w_in = od_w_in[j]
            gw = d // 2
            c = 4 * gw
            c_kidx = c + DSA_Q_RANK + DSA_KV_RANK
            w_dsa = jnp.concatenate([
                w_in[:, c:c_kidx],
                _pad_cols(w_in[:, c_kidx:c_kidx + IDX_DIM], LANES),
                _pad_cols(w_in[:, c_kidx + IDX_DIM:], LANES),
            ], axis=1).astype(BF16)
            a1, zd = _inproj_hgrn(xf, ln_mix_g[layer], od_in_all, j, w_dsa, hgrn_lb, hgrn_norm_g[j],
                                  batch, seq, layer)
            w_qit = jnp.pad(dsa_w_qidx[j].T.reshape(IDX_HEADS, IDX_DIM, DSA_Q_RANK),
                            ((0, 0), (0, LANES - IDX_DIM), (0, 0))).reshape(IDX_HEADS * LANES, DSA_Q_RANK)
            q, qit, kv, kix, wht = _dsa_prep(zd, dsa_cq_g[j], dsa_ckv_g[j], dsa_w_uq[j].astype(BF16),
                                             dsa_qnorm_g[j], w_qit.astype(BF16), tm=256)
            mask, *ffn_w = _dsa_select(qit, wht, kix, ffn_f32, layer, batch, seq, ksel)
            a2 = _dsa_attn(q, kv, mask, rel_bias, dsa_w_uv[j].astype(BF16), batch, seq)
            w_out = od_w_out[j]
        xf = _outproj(a1, a2, w_out.astype(BF16), xf, tm=min(OUT_ROWS, n), tn=OUT_COLS)
        xf = _ffn(xf, ln_ffn_g[layer], *(w[None] for w in ffn_w), 0, tm=min(FFN_ROWS, n), tf=FFN_COLS)
    return xf.reshape(batch, seq, d)
```

```python
import functools
import math

import jax
import jax.numpy as jnp
from jax import lax
from jax.experimental import pallas as pl
from jax.experimental.pallas import tpu as pltpu

F32 = jnp.float32
BF16 = jnp.bfloat16
I32 = jnp.int32

EPS = 1e-6
CHUNK = 64
LANES = 128
ROPE_BASE = 10000.0
RET_HEADS = 4
SGU_WINDOW = 128
SGU_GROUPS = 4
HG_HEADS = 8
DSA_HEADS = 8
DSA_Q_RANK = 384
DSA_KV_RANK = 256
IDX_HEADS = 16
IDX_DIM = 64
TOPK_MAX = 256
Q_BLOCK = 128
KV_TILE = 512
SM_ROWS = 64
LOG2E = math.log2(math.e)
REL_BUCKETS = 32
REL_MAX_DIST = 256
NEG_BIG = -1e30
INT_MIN = -(2 ** 31)
SCAN_TILES = 2
PLANE_GROUP = 32

RET_BLOCK = 256
RET_HEADS_PER_STEP = 4
HG_BLOCK = 256
HG_FINE = 4
VMEM_LIMIT = 48 * 1024 * 1024
PROJ_ROWS = 1024
PROJ_COLS = 1024
OUT_ROWS = 512
OUT_COLS = 2048
FFN_ROWS = 1024
FFN_VMEM_LIMIT = 58 * 1024 * 1024
SIDE_VMEM_LIMIT = 56 * 1024 * 1024
FFN_COLS = 512

_NT = (((1,), (1,)), ((), ()))
_TN = (((0,), (0,)), ((), ()))


def _params(semantics, vmem_limit=VMEM_LIMIT):
    return pltpu.CompilerParams(dimension_semantics=semantics, vmem_limit_bytes=vmem_limit)


def _silu(x):
    return x * jax.nn.sigmoid(x)


def _rms(x):
    return x * lax.rsqrt(jnp.mean(x * x, axis=-1, keepdims=True) + EPS)


def _cast_blocks(rows, steps):
    units = rows // 16
    blocks = max(k for k in range(1, min(units, steps) + 1) if units % k == 0)
    return rows // blocks, blocks


def _cast_specs(weights, step_of, steps):
    ins, outs, shapes = [], [], []
    for w, layer in weights:
        rows, blocks = _cast_blocks(w.shape[1], steps)

        def index(*ids, blocks=blocks):
            return (jnp.minimum(step_of(*ids), blocks - 1), 0)

        ins.append(pl.BlockSpec((None, rows, w.shape[2]), lambda *ids, layer=layer, index=index: (layer,) + index(*ids)))
        outs.append(pl.BlockSpec((rows, w.shape[2]), index))
        shapes.append(jax.ShapeDtypeStruct(w.shape[1:], BF16))
    return ins, outs, shapes


def _norm_matmul_kernel(x_ref, g_ref, w_ref, *rest):
    ncast = (len(rest) - 2) // 2
    o_ref, h_ref = rest[ncast], rest[-1]
    for src, dst in zip(rest[:ncast], rest[ncast + 1:-1]):
        dst[...] = src[...].astype(BF16)

    @pl.when(pl.program_id(1) == 0)
    def _():
        h_ref[...] = (_rms(x_ref[...]) * g_ref[...]).astype(BF16)

    o_ref[...] = jnp.dot(h_ref[...], w_ref[...], preferred_element_type=F32).astype(o_ref.dtype)


def _norm_matmul(x, g, w, layer, nout, *, tm, tn, out_dtype=F32, cast=()):
    n, d = x.shape
    ncols = nout // tn
    c_in, c_out, c_shape = _cast_specs(cast, lambda i, j: i * ncols + j, (n // tm) * ncols)
    return pl.pallas_call(
        _norm_matmul_kernel,
        grid=(n // tm, ncols),
        in_specs=[
            pl.BlockSpec((tm, d), lambda i, j: (i, 0)),
            pl.BlockSpec((1, d), lambda i, j: (0, 0)),
            pl.BlockSpec((None, d, tn), lambda i, j: (layer, 0, j)),
            *c_in,
        ],
        out_specs=[pl.BlockSpec((tm, tn), lambda i, j: (i, j)), *c_out],
        out_shape=[jax.ShapeDtypeStruct((n, nout), out_dtype), *c_shape],
        scratch_shapes=[pltpu.VMEM((tm, d), BF16)],
        compiler_params=_params(("arbitrary", "arbitrary"), SIDE_VMEM_LIMIT),
        name="norm_matmul",
    )(x, g.reshape(1, d), w, *(w_c for w_c, _ in cast))


def _outproj_kernel(a1_ref, a2_ref, w1_ref, w2_ref, r_ref, o_ref):
    acc = jnp.dot(a1_ref[...], w1_ref[...], preferred_element_type=F32)
    acc += jnp.dot(a2_ref[...], w2_ref[...], preferred_element_type=F32)
    o_ref[...] = r_ref[...] + acc


def _outproj(a1, a2, w, res, *, tm, tn):
    n, half = a1.shape
    d = w.shape[1]
    return pl.pallas_call(
        _outproj_kernel,
        grid=(n // tm, d // tn),
        in_specs=[
            pl.BlockSpec((tm, half), lambda i, j: (i, 0)),
            pl.BlockSpec((tm, half), lambda i, j: (i, 0)),
            pl.BlockSpec((half, tn), lambda i, j: (0, j)),
            pl.BlockSpec((half, tn), lambda i, j: (1, j)),
            pl.BlockSpec((tm, tn), lambda i, j: (i, j)),
        ],
        out_specs=pl.BlockSpec((tm, tn), lambda i, j: (i, j)),
        out_shape=jax.ShapeDtypeStruct((n, d), F32),
        compiler_params=_params(("arbitrary", "arbitrary")),
        name="outproj",
    )(a1, a2, w, w, res)


def _ffn_kernel(x_ref, g_ref, wg_ref, wu_ref, wd_ref, o_ref, h_ref):
    @pl.when(pl.program_id(1) == 0)
    def _():
        x = x_ref[...]
        h_ref[...] = (_rms(x) * g_ref[...]).astype(BF16)
        o_ref[...] = x

    h = h_ref[...]
    a = jnp.dot(h, wg_ref[...], preferred_element_type=F32)
    u = jnp.dot(h, wu_ref[...], preferred_element_type=F32)
    act = (_silu(a) * u).astype(BF16)
    o_ref[...] += jnp.dot(act, wd_ref[...], preferred_element_type=F32)


def _ffn(x, g, wg, wu, wd, layer, *, tm, tf):
    n, d = x.shape
    dff = wg.shape[2]
    return pl.pallas_call(
        _ffn_kernel,
        grid=(n // tm, dff // tf),
        in_specs=[
            pl.BlockSpec((tm, d), lambda i, f: (i, 0)),
            pl.BlockSpec((1, d), lambda i, f: (0, 0)),
            pl.BlockSpec((None, d, tf), lambda i, f: (layer, 0, f)),
            pl.BlockSpec((None, d, tf), lambda i, f: (layer, 0, f)),
            pl.BlockSpec((None, tf, d), lambda i, f: (layer, f, 0)),
        ],
        out_specs=pl.BlockSpec((tm, d), lambda i, f: (i, 0)),
        out_shape=jax.ShapeDtypeStruct((n, d), F32),
        scratch_shapes=[pltpu.VMEM((tm, d), BF16)],
        compiler_params=_params(("arbitrary", "arbitrary"), FFN_VMEM_LIMIT),
        name="ffn",
    )(x, g.reshape(1, d), wg, wu, wd)


def _retention_kernel(q_ref, k_ref, v_ref, g_ref, cos_ref, sin_ref, d_ref, xi_ref, zeta_ref,
                      gl_ref, o_ref, state_ref):
    @pl.when(pl.program_id(2) == 0)
    def _():
        state_ref[...] = jnp.zeros_like(state_ref)

    cos = cos_ref[...]
    sin = sin_ref[...]
    half = cos.shape[1]
    dk = 2 * half

    def rot(t):
        t1, t2 = t[:, :half], t[:, half:]
        return jnp.concatenate([t1 * cos - t2 * sin, t1 * sin + t2 * cos], axis=1)

    for i in range(RET_HEADS_PER_STEP):
        cs = slice(i * dk, (i + 1) * dk)
        q = rot(q_ref[:, cs].astype(F32))
        k = rot(k_ref[:, cs].astype(F32)) * (dk ** -0.5)
        qb = q.astype(BF16)
        vb = v_ref[:, cs].astype(BF16)
        scores = lax.dot_general(qb, k.astype(BF16), _NT, preferred_element_type=F32) * d_ref[i]
        intra = jnp.dot(scores.astype(BF16), vb, preferred_element_type=F32)
        state = state_ref[i]
        cross = jnp.dot(qb, state.astype(BF16), preferred_element_type=F32) * xi_ref[i]
        kz = (k * zeta_ref[i]).astype(BF16)
        state_ref[i] = state * gl_ref[i] + lax.dot_general(kz, vb, _TN, preferred_element_type=F32)
        o_ref[:, cs] = (_rms(intra + cross) * _silu(g_ref[:, cs].astype(F32))).astype(BF16)


def _retention_tables(seq, dk):
    blk = RET_BLOCK
    pos = jnp.arange(seq, dtype=F32)
    inv = ROPE_BASE ** (-jnp.arange(0, dk, 2, dtype=F32) / dk)
    ang = pos[:, None] * inv[None, :]
    log_gamma = jnp.log(1.0 - 2.0 ** (-5.0 - jnp.arange(RET_HEADS, dtype=F32)))
    i = jnp.arange(blk)
    same = (i[:, None] // CHUNK) == (i[None, :] // CHUNK)
    earlier = (i[None, :] // CHUNK) < (i[:, None] // CHUNK)
    diff = (i[:, None] - i[None, :]).astype(F32)
    dist = jnp.where(same, jnp.abs(diff), diff)
    decay = jnp.where((same | earlier)[None], jnp.exp(log_gamma[:, None, None] * dist[None]), 0.0)
    p = jnp.arange(blk, dtype=F32)
    wide = (RET_HEADS, blk, dk)
    xi = jnp.broadcast_to(jnp.exp(log_gamma[:, None] * (p + 1.0))[:, :, None], wide)
    zeta = jnp.broadcast_to(jnp.exp(log_gamma[:, None] * (blk - 1.0 - p))[:, :, None], wide)
    g_blk = jnp.broadcast_to(jnp.exp(log_gamma * blk)[:, None, None], (RET_HEADS, 1, dk))
    return jnp.cos(ang), jnp.sin(ang), decay, xi, zeta, g_blk


def _retention(z, batch, seq):
    n = z.shape[0]
    gw = z.shape[1] // 6
    dk = gw // RET_HEADS
    blk = RET_BLOCK
    nblk = seq // blk
    cos, sin, decay, xi, zeta, g_blk = _retention_tables(seq, dk)

    hp = RET_HEADS_PER_STEP
    groups = RET_HEADS // hp

    def zspec(part):
        return pl.BlockSpec((blk, hp * dk), lambda b, h, c: (b * nblk + c, part * groups + h))

    def hspec(rows, cols):
        return pl.BlockSpec((hp, rows, cols), lambda b, h, c: (h, 0, 0))

    return pl.pallas_call(
        _retention_kernel,
        grid=(batch, groups, nblk),
        in_specs=[
            zspec(0), zspec(1), zspec(2), zspec(3),
            pl.BlockSpec((blk, dk // 2), lambda b, h, c: (c, 0)),
            pl.BlockSpec((blk, dk // 2), lambda b, h, c: (c, 0)),
            hspec(blk, blk), hspec(blk, dk), hspec(blk, dk), hspec(1, dk),
        ],
        out_specs=pl.BlockSpec((blk, hp * dk), lambda b, h, c: (b * nblk + c, h)),
        out_shape=jax.ShapeDtypeStruct((n, gw), BF16),
        scratch_shapes=[pltpu.VMEM((hp, dk, dk), F32)],
        compiler_params=_params(("arbitrary", "arbitrary", "arbitrary")),
        name="retention",
    )(z, z, z, z, cos, sin, decay, xi, zeta, g_blk)


def _gelu(x):
    return 0.5 * x * (1.0 + lax.erf(x * math.sqrt(0.5)))


def _sgu_kernel(u_ref, v_ref, lng_ref, lnb_ref, w_ref, b_ref, o_ref):
    rows, width = v_ref.shape
    dg = width // SGU_GROUPS
    v = _gelu(v_ref[...].astype(F32))
    mu = jnp.mean(v, axis=-1, keepdims=True)
    var = jnp.mean(jnp.square(v - mu), axis=-1, keepdims=True)
    vn = ((v - mu) * lax.rsqrt(var + EPS) * lng_ref[...] + lnb_ref[...]).astype(BF16)
    u = _gelu(u_ref[...].astype(F32))
    ri = lax.broadcasted_iota(I32, (SGU_WINDOW, SGU_WINDOW), 0) // CHUNK
    ci = lax.broadcasted_iota(I32, (SGU_WINDOW, SGU_WINDOW), 1) // CHUNK
    allowed = ci <= ri
    for g in range(SGU_GROUPS):
        wg = jnp.where(allowed, w_ref[g], 0.0).astype(BF16)
        bias = b_ref[g]
        for w in range(rows // SGU_WINDOW):
            rs = slice(w * SGU_WINDOW, (w + 1) * SGU_WINDOW)
            cs = slice(g * dg, (g + 1) * dg)
            mixed = jnp.dot(wg, vn[rs, cs], preferred_element_type=F32) + bias
            o_ref[rs, cs] = (u[rs, cs] * mixed).astype(BF16)


def _sgu(z, ln_g, ln_b, w_s, b_s, *, rows):
    n = z.shape[0]
    gw = z.shape[1] // 6
    return pl.pallas_call(
        _sgu_kernel,
        grid=(n // rows,),
        in_specs=[
            pl.BlockSpec((rows, gw), lambda i: (i, 4)),
            pl.BlockSpec((rows, gw), lambda i: (i, 5)),
            pl.BlockSpec((1, gw), lambda i: (0, 0)),
            pl.BlockSpec((1, gw), lambda i: (0, 0)),
            pl.BlockSpec((SGU_GROUPS, SGU_WINDOW, SGU_WINDOW), lambda i: (0, 0, 0)),
            pl.BlockSpec((SGU_GROUPS, SGU_WINDOW, 1), lambda i: (0, 0, 0)),
        ],
        out_specs=pl.BlockSpec((rows, gw), lambda i: (i, 0)),
        out_shape=jax.ShapeDtypeStruct((n, gw), BF16),
        compiler_params=_params(("arbitrary",)),
        name="sgu",
    )(z, z, ln_g.reshape(1, gw), ln_b.reshape(1, gw), w_s, b_s.reshape(SGU_GROUPS, SGU_WINDOW, 1))


def _hgrn_head(q, f_logits, v, g, lb, ng, tri, st_ref, sh_ref):
    rows, dk = q.shape
    f = lb + (1.0 - lb) * jax.nn.sigmoid(f_logits)
    lf = jnp.log(f)
    kk = 1.0 - f
    qa = _silu(q)

    bcum = None
    rest = lf
    for _ in range(3):
        term = rest.astype(BF16)
        part = jnp.dot(tri, term, preferred_element_type=F32)
        bcum = part if bcum is None else bcum + part
        rest = rest - term.astype(F32)

    row = lax.broadcasted_iota(I32, (rows, dk), 0)
    ti = lax.broadcasted_iota(I32, (rows, rows), 0)
    si = lax.broadcasted_iota(I32, (rows, rows), 1)
    attn = jnp.zeros((rows, rows), F32)
    hs = rows // 2
    while hs >= HG_FINE:
        bs = 2 * hs
        parts = [jnp.broadcast_to(bcum[b * bs + hs - 1:b * bs + hs, :], (bs, dk))
                 for b in range(rows // bs)]
        anchor = parts[0] if len(parts) == 1 else jnp.concatenate(parts, axis=0)
        upper = (row & (bs - 1)) >= hs
        fac = jnp.exp(-jnp.abs(bcum - anchor))
        qt = jnp.where(upper, qa * fac, 0.0)
        kt = jnp.where(upper, 0.0, kk * fac)
        a = lax.dot_general(qt.astype(BF16), kt.astype(BF16), _NT, preferred_element_type=F32)
        if bs < rows:
            a = jnp.where((ti & -bs) == (si & -bs), a, 0.0)
        attn = attn + a
        hs //= 2

    vb = v.astype(BF16)
    near = qa * kk
    intra = jnp.sum(near, axis=1, keepdims=True) * v
    pad = jnp.zeros((HG_FINE, dk), F32)
    for idx, val in enumerate((kk, bcum, v)):
        sh_ref[idx, 0:HG_FINE, :] = pad
        sh_ref[idx, HG_FINE:, :] = val
    for delta in range(1, HG_FINE):
        back = slice(HG_FINE - delta, HG_FINE - delta + rows)
        prod = qa * sh_ref[0, back, :] * jnp.exp(jnp.minimum(bcum - sh_ref[1, back, :], 0.0))
        prod = jnp.where((row & (HG_FINE - 1)) >= delta, prod, 0.0)
        intra = intra + jnp.sum(prod, axis=1, keepdims=True) * sh_ref[2, back, :]
    intra = intra + jnp.dot(attn.astype(BF16), vb, preferred_element_type=F32)
    st = st_ref[...]
    cross = lax.dot_general((qa * jnp.exp(bcum)).astype(BF16), st.astype(BF16), _NT,
                            preferred_element_type=F32)
    blast = bcum[rows - 1:rows, :]
    kb = (kk * jnp.exp(blast - bcum)).astype(BF16)
    st_ref[...] = st * jnp.exp(blast) + lax.dot_general(vb, kb, _TN, preferred_element_type=F32)
    return _rms(intra + cross) * ng * _silu(g)


def _hgrn_lower_bound(lb_raw, layer):
    e = jnp.exp(lb_raw - jnp.max(lb_raw, axis=0, keepdims=True))
    soft = e / jnp.sum(e, axis=0, keepdims=True)
    return jnp.sum(soft[1:layer + 1], axis=0, keepdims=True)


def _inproj_hgrn_kernel(x_ref, g_ref, w_ref, ws_ref, lb_ref, ng_ref, tri_ref, o_ref, zd_ref,
                        z_ref, st_ref, sh_ref, *, layer):
    c = pl.program_id(1)
    gw = o_ref.shape[1]
    dk = gw // HG_HEADS

    @pl.when((pl.program_id(0) == 0) & (c == 0))
    def _():
        z_ref[...] = jnp.zeros_like(z_ref)

    @pl.when(c <= 1)
    def _():
        st_ref[...] = jnp.zeros_like(st_ref)

    h = (_rms(x_ref[...]) * g_ref[...]).astype(BF16)
    z_ref[c % 2] = jnp.dot(h, w_ref[...], preferred_element_type=F32)
    zd_ref[...] = jnp.dot(h, ws_ref[...], preferred_element_type=F32)

    prev = (c + 1) % 2
    lb = _hgrn_lower_bound(lb_ref[...], layer)
    tri = tri_ref[...]
    for hd in range(HG_HEADS):
        cols = [slice(part * gw + hd * dk, part * gw + (hd + 1) * dk) for part in range(4)]
        hs = slice(hd * dk, (hd + 1) * dk)
        out = _hgrn_head(z_ref[prev, :, cols[0]], z_ref[prev, :, cols[1]], z_ref[prev, :, cols[2]],
                         z_ref[prev, :, cols[3]], lb[:, hs], ng_ref[:, hs], tri, st_ref.at[hd], sh_ref.at[hd])
        o_ref[:, hs] = out.astype(BF16)


def _inproj_hgrn(x, g, w, layer_idx, w_side, lb_raw, norm_g, batch, seq, layer):
    n, d = x.shape
    gw = norm_g.shape[0]
    dk = gw // HG_HEADS
    blk = HG_BLOCK
    nblk = seq // blk
    ns = w_side.shape[1]
    depth = lb_raw.shape[0]
    const = lambda b, c: (0, 0)
    return pl.pallas_call(
        functools.partial(_inproj_hgrn_kernel, layer=layer),
        grid=(batch, nblk + 1),
        in_specs=[
            pl.BlockSpec((blk, d), lambda b, c: (b * nblk + jnp.minimum(c, nblk - 1), 0)),
            pl.BlockSpec((1, d), const),
            pl.BlockSpec((None, d, 4 * gw), lambda b, c: (layer_idx, 0, 0), pipeline_mode=pl.Buffered(1)),
            pl.BlockSpec((d, ns), const, pipeline_mode=pl.Buffered(1)),
            pl.BlockSpec((depth, gw), const),
            pl.BlockSpec((1, gw), const),
            pl.BlockSpec((blk, blk), const),
        ],
        out_specs=[
            pl.BlockSpec((blk, gw), lambda b, c: (b * nblk + jnp.maximum(c - 1, 0), 0)),
            pl.BlockSpec((blk, ns), lambda b, c: (b * nblk + jnp.minimum(c, nblk - 1), 0)),
        ],
        out_shape=[jax.ShapeDtypeStruct((n, gw), BF16), jax.ShapeDtypeStruct((n, ns), F32)],
        scratch_shapes=[
            pltpu.VMEM((2, blk, 4 * gw), F32),
            pltpu.VMEM((HG_HEADS, dk, dk), F32),
            pltpu.VMEM((HG_HEADS, 3, blk + HG_FINE, dk), F32),
        ],
        compiler_params=_params(("arbitrary", "arbitrary"), SIDE_VMEM_LIMIT),
        name="inproj_hgrn",
    )(x, g.reshape(1, d), w, w_side, lb_raw, norm_g.reshape(1, gw), jnp.tril(jnp.ones((blk, blk), BF16)))


def _dsa_prep_kernel(zd_ref, cqg_ref, ckvg_ref, wuq_ref, qng_ref, wqit_ref,
                     q_ref, qit_ref, kv_ref, kix_ref, wht_ref):
    zd = zd_ref[...]
    cq = (_rms(zd[:, :DSA_Q_RANK]) * cqg_ref[...]).astype(BF16)
    qf = jnp.dot(cq, wuq_ref[...], preferred_element_type=F32)
    for i in range(q_ref.shape[0]):
        rs = slice(i * Q_BLOCK, (i + 1) * Q_BLOCK)
        for h in range(DSA_HEADS):
            cs = slice(h * DSA_KV_RANK, (h + 1) * DSA_KV_RANK)
            q_ref[i, h] = (_rms(qf[rs, cs]) * qng_ref[...] * (DSA_KV_RANK ** -0.5 * LOG2E)).astype(BF16)
    qit = lax.dot_general(wqit_ref[...], cq, _NT, preferred_element_type=F32)
    qit = (qit * (IDX_DIM ** -0.5)).astype(BF16)
    for i in range(q_ref.shape[0]):
        for h in range(IDX_HEADS):
            c = (i * IDX_HEADS + h) * Q_BLOCK
            qit_ref[:, c:c + Q_BLOCK] = qit[h * LANES:(h + 1) * LANES, i * Q_BLOCK:(i + 1) * Q_BLOCK]
    c0 = DSA_Q_RANK
    c1 = c0 + DSA_KV_RANK
    kv_ref[...] = (_rms(zd[:, c0:c1]) * ckvg_ref[...]).astype(BF16)
    kix_ref[...] = zd[:, c1:c1 + LANES].astype(BF16)
    wht = jnp.transpose(zd[:, c1 + LANES:c1 + 2 * LANES] * (IDX_HEADS ** -0.5))
    wht_ref[...] = wht[:IDX_HEADS, :]


def _dsa_prep(zd, cq_g, ckv_g, w_uq, qn_g, w_qit, *, tm):
    n, wd = zd.shape
    dq = w_uq.shape[1]
    dqi = w_qit.shape[0]
    full = lambda i: (0, 0)
    rows = lambda i: (i, 0)
    cols = lambda i: (0, i)
    return pl.pallas_call(
        _dsa_prep_kernel,
        grid=(n // tm,),
        in_specs=[
            pl.BlockSpec((tm, wd), rows),
            pl.BlockSpec((1, DSA_Q_RANK), full),
            pl.BlockSpec((1, DSA_KV_RANK), full),
            pl.BlockSpec((DSA_Q_RANK, dq), full),
            pl.BlockSpec((1, DSA_KV_RANK), full),
            pl.BlockSpec((dqi, DSA_Q_RANK), full),
        ],
        out_specs=[
            pl.BlockSpec((tm // Q_BLOCK, DSA_HEADS, Q_BLOCK, DSA_KV_RANK), lambda i: (i, 0, 0, 0)),
            pl.BlockSpec((LANES, IDX_HEADS * tm), cols),
            pl.BlockSpec((tm, DSA_KV_RANK), rows),
            pl.BlockSpec((tm, LANES), rows),
            pl.BlockSpec((IDX_HEADS, tm), cols),
        ],
        out_shape=[
            jax.ShapeDtypeStruct((n // Q_BLOCK, DSA_HEADS, Q_BLOCK, DSA_KV_RANK), BF16),
            jax.ShapeDtypeStruct((LANES, IDX_HEADS * n), BF16),
            jax.ShapeDtypeStruct((n, DSA_KV_RANK), BF16),
            jax.ShapeDtypeStruct((n, LANES), BF16),
            jax.ShapeDtypeStruct((IDX_HEADS, n), F32),
        ],
        compiler_params=_params(("arbitrary",)),
        name="dsa_prep",
    )(zd, cq_g.reshape(1, -1), ckv_g.reshape(1, -1), w_uq, qn_g.reshape(1, -1), w_qit)


def _bit_planes(key):
    rows = key.shape[0]
    v = [key[8 * k:8 * (k + 1), :] for k in range(rows // 8)]
    sub = lax.broadcasted_iota(I32, v[0].shape, 0)

    def swap(lo, hi, j, m):
        return (lo & ~m) | ((hi >> j) & m), (hi & m) | ((lo << j) & ~m)

    for g in range(0, len(v), 4):
        for a, b in ((0, 2), (1, 3)):
            v[g + a], v[g + b] = swap(v[g + a], v[g + b], 16, 0x0000FFFF)
        for a, b in ((0, 1), (2, 3)):
            v[g + a], v[g + b] = swap(v[g + a], v[g + b], 8, 0x00FF00FF)
    for j, m in ((4, 0x0F0F0F0F), (2, 0x33333333), (1, 0x55555555)):
        high = (sub & j) != 0
        keep = jnp.where(high, m, ~m)
        for k in range(len(v)):
            down, up = pltpu.roll(v[k], j, axis=0), pltpu.roll(v[k], 8 - j, axis=0)
            moved = jnp.where(high, down << j, up >> j)
            v[k] = (v[k] & keep) | (moved & ~keep)
    return jnp.concatenate(v, axis=0)


def _dsa_select_kernel(qit_ref, wht_ref, kix_ref, wg_in, wu_in, wd_in, m_ref, wg_out, wu_out, wd_out,
                       key_ref, jc_ref, plane_ref, *, ksel, idx_bits):
    for src, dst in ((wg_in, wg_out), (wu_in, wu_out), (wd_in, wd_out)):
        dst[...] = src[...].astype(BF16)

    qb = pl.program_id(1)
    ntile = qb + 1
    ntile_all = m_ref.shape[2]
    rowi = lax.broadcasted_iota(I32, (LANES, Q_BLOCK), 0)
    coli = lax.broadcasted_iota(I32, (LANES, Q_BLOCK), 1)
    q_chunk = (qb * Q_BLOCK + coli) // CHUNK
    pairs = IDX_HEADS // 2
    w_pair = [jnp.concatenate([wht_ref[2 * p:2 * p + 1, :], wht_ref[2 * p + 1:2 * p + 2, :]], axis=1)
              for p in range(pairs)]

    def tile_rows(j):
        return pl.ds(pl.multiple_of(j * LANES, LANES), LANES)

    def admissible(j):
        return ((j * LANES + rowi) // CHUNK) <= q_chunk

    def score_tile(jj, carry):
        for t in range(2):
            j = 2 * jj + t
            kt = kix_ref[tile_rows(j), :]
            sc = None
            for p in range(pairs):
                s2 = jnp.dot(kt, qit_ref[:, 2 * p * Q_BLOCK:2 * (p + 1) * Q_BLOCK],
                             preferred_element_type=F32)
                c2 = w_pair[p] * jnp.maximum(s2, 0.0)
                c = c2[:, :Q_BLOCK] + c2[:, Q_BLOCK:]
                sc = c if sc is None else sc + c
            bits = pltpu.bitcast(sc, I32)
            key = bits ^ ((bits >> 31) & 0x7FFFFFFF)
            key = jnp.where(admissible(j), key, INT_MIN)
            key_ref[tile_rows(j), :] = key
            plane_ref[tile_rows(j), :] = _bit_planes(key)
        return carry

    @pl.when(qb == 0)
    def _():
        key_ref[...] = jnp.full(key_ref.shape, INT_MIN, I32)
        group_row = lax.broadcasted_iota(I32, plane_ref.shape, 0) & (PLANE_GROUP - 1)
        plane_ref[...] = jnp.where(group_row == 0, -1, 0)

    lax.fori_loop(0, (ntile + 1) // 2, score_tile, 0)

    def count(pred_fn):
        def body(jj, acc):
            for t in range(SCAN_TILES):
                j = SCAN_TILES * jj + t
                acc = acc + pred_fn(j, key_ref[tile_rows(j), :]).astype(I32)
            return acc
        trips = (ntile + SCAN_TILES - 1) // SCAN_TILES
        acc = lax.fori_loop(0, trips, body, jnp.zeros((LANES, Q_BLOCK), I32))
        return jnp.sum(acc, axis=0, keepdims=True)

    nword = plane_ref.shape[0] // (8 * PLANE_GROUP)

    def plane(v, i):
        return plane_ref[pl.ds(v * 8 * PLANE_GROUP + i, 8, stride=PLANE_GROUP), :]

    def total(words):
        acc = lax.population_count(words[0])
        for w in words[1:]:
            acc = acc + lax.population_count(w)
        return jnp.sum(acc, axis=0, keepdims=True)

    def radix_pair(t, c):
        alive, above, t_u = c
        i = 2 * t
        flip = jnp.where(t == 0, -1, 0)
        set1 = [a & (plane(v, i) ^ flip) for v, a in enumerate(alive)]
        clr1 = [a ^ s for a, s in zip(alive, set1)]
        p2 = [plane(v, i + 1) for v in range(nword)]
        set1_set2 = [s & p for s, p in zip(set1, p2)]
        clr1_set2 = [s & p for s, p in zip(clr1, p2)]
        n1, n11, n01 = total(set1), total(set1_set2), total(clr1_set2)
        take1 = (above + n1) >= ksel
        above = jnp.where(take1, above, above + n1)
        n2 = jnp.where(take1, n11, n01)
        take2 = (above + n2) >= ksel
        above = jnp.where(take2, above, above + n2)
        alive = tuple(
            jnp.where(take1, jnp.where(take2, ss, s ^ ss), jnp.where(take2, cs, c0 ^ cs))
            for s, c0, ss, cs in zip(set1, clr1, set1_set2, clr1_set2))
        t_u = (t_u | jnp.where(take1, jnp.left_shift(jnp.int32(1), 31 - i), 0)
               | jnp.where(take2, jnp.left_shift(jnp.int32(1), 30 - i), 0))
        return alive, above, t_u

    start = (tuple(jnp.full((8, Q_BLOCK), -1, I32) for _ in range(nword)),
             jnp.zeros((1, Q_BLOCK), I32), jnp.zeros((1, Q_BLOCK), I32))
    alive, above, t_u = lax.fori_loop(0, 16, radix_pair, start)
    thr = t_u ^ INT_MIN
    ties = lax.population_count(alive[0])
    for a in alive[1:]:
        ties = ties + lax.population_count(a)
    cnt_t = above + jnp.sum(ties, axis=0, keepdims=True)
    tied = jnp.max(jnp.where((cnt_t > ksel) & (thr > INT_MIN), 1.0, 0.0))

    jc_ref[...] = jnp.full(jc_ref.shape, 2 ** 31 - 1, I32)

    @pl.when(tied > 0.0)
    def _():
        need = ksel - count(lambda j, k: k > thr)

        def index_bit(i, j_c):
            cand = j_c | jnp.left_shift(jnp.int32(1), idx_bits - 1 - i)
            cnt = count(lambda j, k: (k == thr) & ((j * LANES + rowi) < cand))
            return jnp.where(cnt < need, cand, j_c)

        j_c = lax.fori_loop(0, idx_bits, index_bit, jnp.zeros((1, Q_BLOCK), I32))
        jc_ref[...] = jnp.broadcast_to(j_c, jc_ref.shape)

    j_c = jc_ref[0:1, :]
    eye = (rowi == coli).astype(BF16)

    group = KV_TILE // LANES

    def write_group(g, carry):
        for t in range(group):
            j = g * group + t
            k = key_ref[tile_rows(j), :]
            sel = (k > thr) | ((k == thr) & ((j * LANES + rowi) <= j_c))
            sel = jnp.where(sel & admissible(j), 1.0, 0.0).astype(BF16)
            sel_t = lax.dot_general(eye, sel, _NT, preferred_element_type=F32)
            m_ref[0, 0, j] = ((sel_t - 1.0) * -NEG_BIG).astype(BF16)
        return carry

    ngroup = (ntile + group - 1) // group
    lax.fori_loop(0, ngroup, write_group, 0)

    def blank_tile(j, carry):
        m_ref[0, 0, j] = jnp.full((Q_BLOCK, LANES), NEG_BIG, BF16)
        return carry

    lax.fori_loop(ngroup * group, ntile_all, blank_tile, 0)


def _dsa_select(qit, wht, kix, ffn_weights, layer, batch, seq, ksel):
    nqb = seq // Q_BLOCK
    nkt = seq // LANES
    w_in, w_out, w_shape = _cast_specs([(w, layer) for w in ffn_weights], lambda b, q: b * nqb + q,
                                       batch * nqb)
    return pl.pallas_call(
        functools.partial(_dsa_select_kernel, ksel=ksel, idx_bits=int(math.log2(seq))),
        grid=(batch, nqb),
        in_specs=[
            pl.BlockSpec((LANES, IDX_HEADS * Q_BLOCK), lambda b, q: (0, b * nqb + q)),
            pl.BlockSpec((IDX_HEADS, Q_BLOCK), lambda b, q: (0, b * nqb + q)),
            pl.BlockSpec((seq, LANES), lambda b, q: (b, 0)),
            *w_in,
        ],
        out_specs=[pl.BlockSpec((1, 1, nkt, Q_BLOCK, LANES), lambda b, q: (b, q, 0, 0, 0)), *w_out],
        out_shape=[jax.ShapeDtypeStruct((batch, nqb, nkt, Q_BLOCK, LANES), BF16), *w_shape],
        scratch_shapes=[pltpu.VMEM((seq, Q_BLOCK), I32), pltpu.VMEM((8, Q_BLOCK), I32),
                        pltpu.VMEM((seq, Q_BLOCK), I32)],
        compiler_params=_params(("arbitrary", "arbitrary")),
        name="dsa_select",
    )(qit, wht, kix, *ffn_weights)


def _rel_bucket(rel):
    nb = REL_BUCKETS // 2
    max_exact = nb // 2
    ret = jnp.where(rel > 0, nb, 0)
    n = jnp.abs(rel)
    nf = jnp.maximum(n, 1).astype(F32)
    large = max_exact + (jnp.log(nf / max_exact) / math.log(REL_MAX_DIST / max_exact)
                         * (nb - max_exact)).astype(I32)
    large = jnp.minimum(large, nb - 1)
    return ret + jnp.where(n < max_exact, n, large)


NEAR_TILES = 3


def _dsa_attn_kernel(q_ref, kv_ref, mask_ref, rb_ref, wuv_ref, o_ref,
                     m_ref, l_ref, alpha_ref, acc_ref, corr_ref, s_ref, p_ref, madd_ref):
    b, qb = pl.program_id(0), pl.program_id(1)
    sub = KV_TILE // LANES
    far_bucket = REL_BUCKETS // 2 - 1
    half = DSA_HEADS * Q_BLOCK // 2

    def head_rows(h):
        return slice(h * Q_BLOCK, (h + 1) * Q_BLOCK)

    @pl.when((b == 0) & (qb == 0))
    def _():
        ti = lax.broadcasted_iota(I32, (Q_BLOCK, LANES), 0)
        si = lax.broadcasted_iota(I32, (Q_BLOCK, LANES), 1)
        for oi in range(NEAR_TILES):
            bucket = _rel_bucket((oi - (NEAR_TILES - 1)) * LANES + si - ti)
            for h in range(DSA_HEADS):
                tbl = jnp.zeros((Q_BLOCK, LANES), F32)
                for bk in range(REL_BUCKETS):
                    tbl = jnp.where(bucket == bk, rb_ref[bk, h], tbl)
                corr_ref[oi, head_rows(h), :] = (tbl - rb_ref[far_bucket, h]) * LOG2E

    m_ref[...] = jnp.full_like(m_ref, NEG_BIG)
    l_ref[...] = jnp.zeros_like(l_ref)
    acc_ref[...] = jnp.zeros_like(acc_ref)
    q_all = q_ref[0].reshape(DSA_HEADS * Q_BLOCK, DSA_KV_RANK)

    def key_step(kt, carry):
        kvt = kv_ref[pl.ds(pl.multiple_of(kt * KV_TILE, KV_TILE), KV_TILE), :]
        for part in range(2):
            rs = slice(part * half, (part + 1) * half)
            s_ref[rs, :] = lax.dot_general(q_all[rs], kvt, _NT, preferred_element_type=F32)
        for j in range(sub):
            d = kt * sub + j - qb

            @pl.when((d > -NEAR_TILES) & (d <= 0))
            def _(j=j, d=d):
                s_ref[:, j * LANES:(j + 1) * LANES] += corr_ref[d + NEAR_TILES - 1]

        for j in range(sub):
            madd_ref[:, j * LANES:(j + 1) * LANES] = mask_ref[0, 0, kt * sub + j].astype(F32)
        groups_per_head = Q_BLOCK // SM_ROWS
        for g in range(DSA_HEADS * groups_per_head):
            rs = slice(g * SM_ROWS, (g + 1) * SM_ROWS)
            qg = g % groups_per_head
            s = s_ref[rs, :] + madd_ref[qg * SM_ROWS:(qg + 1) * SM_ROWS, :]
            m_old = m_ref[rs, :]
            m_new = jnp.maximum(m_old, jnp.max(s, axis=1, keepdims=True))
            alpha = jnp.exp2(m_old - m_new)
            p = jnp.exp2(s - jnp.tile(m_new, (1, sub)))
            l_ref[rs, :] = alpha * l_ref[rs, :] + jnp.sum(p, axis=1, keepdims=True)
            alpha_ref[rs, :] = alpha
            p_ref[rs, :] = p.astype(BF16)
            m_ref[rs, :] = m_new
        for part in range(2):
            rs = slice(part * half, (part + 1) * half)
            pv = jnp.dot(p_ref[rs, :], kvt, preferred_element_type=F32)
            acc_ref[rs, :] = jnp.tile(alpha_ref[rs, :], (1, DSA_KV_RANK // LANES)) * acc_ref[rs, :] + pv
        return carry

    lax.fori_loop(0, qb // sub + 1, key_step, 0)

    dv = wuv_ref.shape[2]
    for h in range(DSA_HEADS):
        rs = head_rows(h)
        o = (acc_ref[rs, :] / jnp.tile(l_ref[rs, :], (1, DSA_KV_RANK // LANES))).astype(BF16)
        o_ref[:, h * dv:(h + 1) * dv] = jnp.dot(o, wuv_ref[h], preferred_element_type=F32).astype(BF16)


def _dsa_attn(q, kv, mask, rel_bias, w_uv, batch, seq):
    n = kv.shape[0]
    nqb = seq // Q_BLOCK
    dv = w_uv.shape[2]
    rows = DSA_HEADS * Q_BLOCK
    return pl.pallas_call(
        _dsa_attn_kernel,
        grid=(batch, nqb),
        in_specs=[
            pl.BlockSpec((1, DSA_HEADS, Q_BLOCK, DSA_KV_RANK), lambda b, qb: (b * nqb + qb, 0, 0, 0)),
            pl.BlockSpec((seq, DSA_KV_RANK), lambda b, qb: (b, 0)),
            pl.BlockSpec((1, 1) + mask.shape[2:], lambda b, qb: (b, qb, 0, 0, 0)),
            pl.BlockSpec(memory_space=pltpu.SMEM),
            pl.BlockSpec(w_uv.shape, lambda b, qb: (0, 0, 0)),
        ],
        out_specs=pl.BlockSpec((Q_BLOCK, DSA_HEADS * dv), lambda b, qb: (b * nqb + qb, 0)),
        out_shape=jax.ShapeDtypeStruct((n, DSA_HEADS * dv), BF16),
        scratch_shapes=[
            pltpu.VMEM((rows, LANES), F32),
            pltpu.VMEM((rows, LANES), F32),
            pltpu.VMEM((rows, LANES), F32),
            pltpu.VMEM((rows, DSA_KV_RANK), F32),
            pltpu.VMEM((NEAR_TILES, rows, LANES), F32),
            pltpu.VMEM((rows, KV_TILE), F32),
            pltpu.VMEM((rows, KV_TILE), BF16),
            pltpu.VMEM((Q_BLOCK, KV_TILE), F32),
        ],
        compiler_params=_params(("arbitrary", "arbitrary")),
        name="dsa_attn",
    )(q, kv, mask, rel_bias, w_uv)


def _pad_cols(w, width):
    return jnp.pad(w, ((0, 0), (0, width - w.shape[1])))


def kernel(x, ln_mix_g, ln_ffn_g, w_ffn_gate, w_ffn_up, w_ffn_down, rel_bias, ev_w_in, ev_w_out, sgu_ln_g, sgu_ln_b, sgu_w_s, sgu_b_s, od_w_in, od_w_out, hgrn_lb, hgrn_norm_g, dsa_cq_g, dsa_ckv_g, dsa_w_uq, dsa_qnorm_g, dsa_w_qidx, dsa_w_uv):
    batch, seq, d = x.shape
    n = batch * seq
    depth = ln_mix_g.shape[0]
    ksel = min(TOPK_MAX, seq // 4)
    tm = min(PROJ_ROWS, n)
    xf = x.reshape(n, d)
    ffn_f32 = (w_ffn_gate, w_ffn_up, w_ffn_down)
    ev_in_all = ev_w_in.astype(BF16)
    for layer in range(depth):
        j = layer // 2
        if layer % 2 == 0:
            cast = [(w, layer) for w in ffn_f32] + ([(od_w_in, j)] if layer + 1 < depth else [])
            z, *cast = _norm_matmul(xf, ln_mix_g[layer], ev_in_all, j, ev_in_all.shape[2], tm=tm,
                                    tn=PROJ_COLS, out_dtype=BF16, cast=cast)
            ffn_w, od_in = cast[:3], cast[3:]
            a1 = _retention(z, batch, seq)
            a2 = _sgu(z, sgu_ln_g[j], sgu_ln_b[j], sgu_w_s[j], sgu_b_s[j], rows=256)
            w_out = ev_w_out[j]
        else:
            w_in = od_w_in[j]
            gw = d // 2
            c = 4 * gw
            c_kidx = c + DSA_Q_RANK + DSA_KV_RANK
            w_dsa = jnp.concatenate([
                w_in[:, c:c_kidx],
                _pad_cols(w_in[:, c_kidx:c_kidx + IDX_DIM], LANES),
                _pad_cols(w_in[:, c_kidx + IDX_DIM:], LANES),
            ], axis=1).astype(BF16)
            a1, zd = _inproj_hgrn(xf, ln_mix_g[layer], od_in[0][None], 0, w_dsa, hgrn_lb, hgrn_norm_g[j],
                                  batch, seq, layer)
            w_qit = jnp.pad(dsa_w_qidx[j].T.reshape(IDX_HEADS, IDX_DIM, DSA_Q_RANK),
                            ((0, 0), (0, LANES - IDX_DIM), (0, 0))).reshape(IDX_HEADS * LANES, DSA_Q_RANK)
            q, qit, kv, kix, wht = _dsa_prep(zd, dsa_cq_g[j], dsa_ckv_g[j], dsa_w_uq[j].astype(BF16),
                                             dsa_qnorm_g[j], w_qit.astype(BF16), tm=256)
            mask, *ffn_w = _dsa_select(qit, wht, kix, ffn_f32, layer, batch, seq, ksel)
            a2 = _dsa_attn(q, kv, mask, rel_bias, dsa_w_uv[j].astype(BF16), batch, seq)
            w_out = od_w_out[j]
        xf = _outproj(a1, a2, w_out.astype(BF16), xf, tm=min(OUT_ROWS, n), tn=OUT_COLS)
        xf = _ffn(xf, ln_ffn_g[layer], *(w[None] for w in ffn_w), 0, tm=min(FFN_ROWS, n), tf=FFN_COLS)
    return xf.reshape(batch, seq, d)
```

```python
import functools
import math

import jax
import jax.numpy as jnp
from jax import lax
from jax.experimental import pallas as pl
from jax.experimental.pallas import tpu as pltpu

F32 = jnp.float32
BF16 = jnp.bfloat16
I32 = jnp.int32

EPS = 1e-6
CHUNK = 64
LANES = 128
ROPE_BASE = 10000.0
RET_HEADS = 4
SGU_WINDOW = 128
SGU_GROUPS = 4
HG_HEADS = 8
DSA_HEADS = 8
DSA_Q_RANK = 384
DSA_KV_RANK = 256
IDX_HEADS = 16
IDX_DIM = 64
TOPK_MAX = 256
Q_BLOCK = 128
KV_TILE = 512
SM_ROWS = 64
LOG2E = math.log2(math.e)
REL_BUCKETS = 32
REL_MAX_DIST = 256
NEG_BIG = -1e30
INT_MIN = -(2 ** 31)
SCAN_TILES = 2
PLANE_GROUP = 32

RET_BLOCK = 256
RET_HEADS_PER_STEP = 4
HG_BLOCK = 256
HG_FINE = 4
VMEM_LIMIT = 48 * 1024 * 1024
PROJ_ROWS = 1024
PROJ_COLS = 1024
OUT_ROWS = 512
OUT_COLS = 2048
FFN_ROWS = 1024
FFN_VMEM_LIMIT = 58 * 1024 * 1024
SIDE_VMEM_LIMIT = 56 * 1024 * 1024
FFN_COLS = 512

_NT = (((1,), (1,)), ((), ()))
_TN = (((0,), (0,)), ((), ()))


def _params(semantics, vmem_limit=VMEM_LIMIT):
    return pltpu.CompilerParams(dimension_semantics=semantics, vmem_limit_bytes=vmem_limit)


def _silu(x):
    return x * jax.nn.sigmoid(x)


def _rms(x):
    return x * lax.rsqrt(jnp.mean(x * x, axis=-1, keepdims=True) + EPS)


def _cast_blocks(rows, steps):
    units = rows // 16
    blocks = max(k for k in range(1, min(units, steps) + 1) if units % k == 0)
    return rows // blocks, blocks


def _cast_specs(weights, step_of, steps):
    ins, outs, shapes = [], [], []
    for w, layer in weights:
        rows, blocks = _cast_blocks(w.shape[1], steps)

        def index(*ids, blocks=blocks):
            return (jnp.minimum(step_of(*ids), blocks - 1), 0)

        ins.append(pl.BlockSpec((None, rows, w.shape[2]), lambda *ids, layer=layer, index=index: (layer,) + index(*ids)))
        outs.append(pl.BlockSpec((rows, w.shape[2]), index))
        shapes.append(jax.ShapeDtypeStruct(w.shape[1:], BF16))
    return ins, outs, shapes


def _norm_matmul_kernel(x_ref, g_ref, w_ref, *rest):
    ncast = (len(rest) - 2) // 2
    o_ref, h_ref = rest[ncast], rest[-1]
    for src, dst in zip(rest[:ncast], rest[ncast + 1:-1]):
        dst[...] = src[...].astype(BF16)

    @pl.when(pl.program_id(1) == 0)
    def _():
        h_ref[...] = (_rms(x_ref[...]) * g_ref[...]).astype(BF16)

    o_ref[...] = jnp.dot(h_ref[...], w_ref[...], preferred_element_type=F32).astype(o_ref.dtype)


def _norm_matmul(x, g, w, layer, nout, *, tm, tn, out_dtype=F32, cast=()):
    n, d = x.shape
    ncols = nout // tn
    c_in, c_out, c_shape = _cast_specs(cast, lambda i, j: i * ncols + j, (n // tm) * ncols)
    return pl.pallas_call(
        _norm_matmul_kernel,
        grid=(n // tm, ncols),
        in_specs=[
            pl.BlockSpec((tm, d), lambda i, j: (i, 0)),
            pl.BlockSpec((1, d), lambda i, j: (0, 0)),
            pl.BlockSpec((None, d, tn), lambda i, j: (layer, 0, j)),
            *c_in,
        ],
        out_specs=[pl.BlockSpec((tm, tn), lambda i, j: (i, j)), *c_out],
        out_shape=[jax.ShapeDtypeStruct((n, nout), out_dtype), *c_shape],
        scratch_shapes=[pltpu.VMEM((tm, d), BF16)],
        compiler_params=_params(("arbitrary", "arbitrary"), SIDE_VMEM_LIMIT),
        name="norm_matmul",
    )(x, g.reshape(1, d), w, *(w_c for w_c, _ in cast))


def _outproj_kernel(a1_ref, a2_ref, w1_ref, w2_ref, r_ref, o_ref):
    acc = jnp.dot(a1_ref[...], w1_ref[...], preferred_element_type=F32)
    acc += jnp.dot(a2_ref[...], w2_ref[...], preferred_element_type=F32)
    o_ref[...] = r_ref[...] + acc


def _outproj(a1, a2, w, res, *, tm, tn):
    n, half = a1.shape
    d = w.shape[1]
    return pl.pallas_call(
        _outproj_kernel,
        grid=(n // tm, d // tn),
        in_specs=[
            pl.BlockSpec((tm, half), lambda i, j: (i, 0)),
            pl.BlockSpec((tm, half), lambda i, j: (i, 0)),
            pl.BlockSpec((half, tn), lambda i, j: (0, j)),
            pl.BlockSpec((half, tn), lambda i, j: (1, j)),
            pl.BlockSpec((tm, tn), lambda i, j: (i, j)),
        ],
        out_specs=pl.BlockSpec((tm, tn), lambda i, j: (i, j)),
        out_shape=jax.ShapeDtypeStruct((n, d), F32),
        compiler_params=_params(("arbitrary", "arbitrary")),
        name="outproj",
    )(a1, a2, w, w, res)


def _ffn_kernel(x_ref, g_ref, wg_ref, wu_ref, wd_ref, o_ref, h_ref):
    @pl.when(pl.program_id(1) == 0)
    def _():
        x = x_ref[...]
        h_ref[...] = (_rms(x) * g_ref[...]).astype(BF16)
        o_ref[...] = x

    h = h_ref[...]
    a = jnp.dot(h, wg_ref[...], preferred_element_type=F32)
    u = jnp.dot(h, wu_ref[...], preferred_element_type=F32)
    act = (_silu(a) * u).astype(BF16)
    o_ref[...] += jnp.dot(act, wd_ref[...], preferred_element_type=F32)


def _ffn(x, g, wg, wu, wd, layer, *, tm, tf):
    n, d = x.shape
    dff = wg.shape[2]
    return pl.pallas_call(
        _ffn_kernel,
        grid=(n // tm, dff // tf),
        in_specs=[
            pl.BlockSpec((tm, d), lambda i, f: (i, 0)),
            pl.BlockSpec((1, d), lambda i, f: (0, 0)),
            pl.BlockSpec((None, d, tf), lambda i, f: (layer, 0, f)),
            pl.BlockSpec((None, d, tf), lambda i, f: (layer, 0, f)),
            pl.BlockSpec((None, tf, d), lambda i, f: (layer, f, 0)),
        ],
        out_specs=pl.BlockSpec((tm, d), lambda i, f: (i, 0)),
        out_shape=jax.ShapeDtypeStruct((n, d), F32),
        scratch_shapes=[pltpu.VMEM((tm, d), BF16)],
        compiler_params=_params(("arbitrary", "arbitrary"), FFN_VMEM_LIMIT),
        name="ffn",
    )(x, g.reshape(1, d), wg, wu, wd)


def _retention_kernel(q_ref, k_ref, v_ref, g_ref, cos_ref, sin_ref, d_ref, xi_ref, zeta_ref,
                      gl_ref, o_ref, state_ref):
    @pl.when(pl.program_id(2) == 0)
    def _():
        state_ref[...] = jnp.zeros_like(state_ref)

    cos = cos_ref[...]
    sin = sin_ref[...]
    half = cos.shape[1]
    dk = 2 * half

    def rot(t):
        t1, t2 = t[:, :half], t[:, half:]
        return jnp.concatenate([t1 * cos - t2 * sin, t1 * sin + t2 * cos], axis=1)

    for i in range(RET_HEADS_PER_STEP):
        cs = slice(i * dk, (i + 1) * dk)
        q = rot(q_ref[:, cs].astype(F32))
        k = rot(k_ref[:, cs].astype(F32)) * (dk ** -0.5)
        qb = q.astype(BF16)
        vb = v_ref[:, cs].astype(BF16)
        scores = lax.dot_general(qb, k.astype(BF16), _NT, preferred_element_type=F32) * d_ref[i]
        intra = jnp.dot(scores.astype(BF16), vb, preferred_element_type=F32)
        state = state_ref[i]
        cross = jnp.dot(qb, state.astype(BF16), preferred_element_type=F32) * xi_ref[i]
        kz = (k * zeta_ref[i]).astype(BF16)
        state_ref[i] = state * gl_ref[i] + lax.dot_general(kz, vb, _TN, preferred_element_type=F32)
        o_ref[:, cs] = (_rms(intra + cross) * _silu(g_ref[:, cs].astype(F32))).astype(BF16)


def _retention_tables(seq, dk):
    blk = RET_BLOCK
    pos = jnp.arange(seq, dtype=F32)
    inv = ROPE_BASE ** (-jnp.arange(0, dk, 2, dtype=F32) / dk)
    ang = pos[:, None] * inv[None, :]
    log_gamma = jnp.log(1.0 - 2.0 ** (-5.0 - jnp.arange(RET_HEADS, dtype=F32)))
    i = jnp.arange(blk)
    same = (i[:, None] // CHUNK) == (i[None, :] // CHUNK)
    earlier = (i[None, :] // CHUNK) < (i[:, None] // CHUNK)
    diff = (i[:, None] - i[None, :]).astype(F32)
    dist = jnp.where(same, jnp.abs(diff), diff)
    decay = jnp.where((same | earlier)[None], jnp.exp(log_gamma[:, None, None] * dist[None]), 0.0)
    p = jnp.arange(blk, dtype=F32)
    wide = (RET_HEADS, blk, dk)
    xi = jnp.broadcast_to(jnp.exp(log_gamma[:, None] * (p + 1.0))[:, :, None], wide)
    zeta = jnp.broadcast_to(jnp.exp(log_gamma[:, None] * (blk - 1.0 - p))[:, :, None], wide)
    g_blk = jnp.broadcast_to(jnp.exp(log_gamma * blk)[:, None, None], (RET_HEADS, 1, dk))
    return jnp.cos(ang), jnp.sin(ang), decay, xi, zeta, g_blk


def _retention(z, batch, seq):
    n = z.shape[0]
    gw = z.shape[1] // 6
    dk = gw // RET_HEADS
    blk = RET_BLOCK
    nblk = seq // blk
    cos, sin, decay, xi, zeta, g_blk = _retention_tables(seq, dk)

    hp = RET_HEADS_PER_STEP
    groups = RET_HEADS // hp

    def zspec(part):
        return pl.BlockSpec((blk, hp * dk), lambda b, h, c: (b * nblk + c, part * groups + h))

    def hspec(rows, cols):
        return pl.BlockSpec((hp, rows, cols), lambda b, h, c: (h, 0, 0))

    return pl.pallas_call(
        _retention_kernel,
        grid=(batch, groups, nblk),
        in_specs=[
            zspec(0), zspec(1), zspec(2), zspec(3),
            pl.BlockSpec((blk, dk // 2), lambda b, h, c: (c, 0)),
            pl.BlockSpec((blk, dk // 2), lambda b, h, c: (c, 0)),
            hspec(blk, blk), hspec(blk, dk), hspec(blk, dk), hspec(1, dk),
        ],
        out_specs=pl.BlockSpec((blk, hp * dk), lambda b, h, c: (b * nblk + c, h)),
        out_shape=jax.ShapeDtypeStruct((n, gw), BF16),
        scratch_shapes=[pltpu.VMEM((hp, dk, dk), F32)],
        compiler_params=_params(("arbitrary", "arbitrary", "arbitrary")),
        name="retention",
    )(z, z, z, z, cos, sin, decay, xi, zeta, g_blk)


def _gelu(x):
    return 0.5 * x * (1.0 + lax.erf(x * math.sqrt(0.5)))


def _sgu_kernel(u_ref, v_ref, lng_ref, lnb_ref, w_ref, b_ref, o_ref):
    rows, width = v_ref.shape
    dg = width // SGU_GROUPS
    v = _gelu(v_ref[...].astype(F32))
    mu = jnp.mean(v, axis=-1, keepdims=True)
    var = jnp.mean(jnp.square(v - mu), axis=-1, keepdims=True)
    vn = ((v - mu) * lax.rsqrt(var + EPS) * lng_ref[...] + lnb_ref[...]).astype(BF16)
    u = _gelu(u_ref[...].astype(F32))
    ri = lax.broadcasted_iota(I32, (SGU_WINDOW, SGU_WINDOW), 0) // CHUNK
    ci = lax.broadcasted_iota(I32, (SGU_WINDOW, SGU_WINDOW), 1) // CHUNK
    allowed = ci <= ri
    for g in range(SGU_GROUPS):
        wg = jnp.where(allowed, w_ref[g], 0.0).astype(BF16)
        bias = b_ref[g]
        for w in range(rows // SGU_WINDOW):
            rs = slice(w * SGU_WINDOW, (w + 1) * SGU_WINDOW)
            cs = slice(g * dg, (g + 1) * dg)
            mixed = jnp.dot(wg, vn[rs, cs], preferred_element_type=F32) + bias
            o_ref[rs, cs] = (u[rs, cs] * mixed).astype(BF16)


def _sgu(z, ln_g, ln_b, w_s, b_s, *, rows):
    n = z.shape[0]
    gw = z.shape[1] // 6
    return pl.pallas_call(
        _sgu_kernel,
        grid=(n // rows,),
        in_specs=[
            pl.BlockSpec((rows, gw), lambda i: (i, 4)),
            pl.BlockSpec((rows, gw), lambda i: (i, 5)),
            pl.BlockSpec((1, gw), lambda i: (0, 0)),
            pl.BlockSpec((1, gw), lambda i: (0, 0)),
            pl.BlockSpec((SGU_GROUPS, SGU_WINDOW, SGU_WINDOW), lambda i: (0, 0, 0)),
            pl.BlockSpec((SGU_GROUPS, SGU_WINDOW, 1), lambda i: (0, 0, 0)),
        ],
        out_specs=pl.BlockSpec((rows, gw), lambda i: (i, 0)),
        out_shape=jax.ShapeDtypeStruct((n, gw), BF16),
        compiler_params=_params(("arbitrary",)),
        name="sgu",
    )(z, z, ln_g.reshape(1, gw), ln_b.reshape(1, gw), w_s, b_s.reshape(SGU_GROUPS, SGU_WINDOW, 1))


def _hgrn_head(q, f_logits, v, g, lb, ng, tri, st_ref, sh_ref):
    rows, dk = q.shape
    f = lb + (1.0 - lb) * jax.nn.sigmoid(f_logits)
    lf = jnp.log(f)
    kk = 1.0 - f
    qa = _silu(q)

    bcum = None
    rest = lf
    for _ in range(3):
        term = rest.astype(BF16)
        part = jnp.dot(tri, term, preferred_element_type=F32)
        bcum = part if bcum is None else bcum + part
        rest = rest - term.astype(F32)

    row = lax.broadcasted_iota(I32, (rows, dk), 0)
    ti = lax.broadcasted_iota(I32, (rows, rows), 0)
    si = lax.broadcasted_iota(I32, (rows, rows), 1)
    attn = jnp.zeros((rows, rows), F32)
    hs = rows // 2
    while hs >= HG_FINE:
        bs = 2 * hs
        parts = [jnp.broadcast_to(bcum[b * bs + hs - 1:b * bs + hs, :], (bs, dk))
                 for b in range(rows // bs)]
        anchor = parts[0] if len(parts) == 1 else jnp.concatenate(parts, axis=0)
        upper = (row & (bs - 1)) >= hs
        fac = jnp.exp(-jnp.abs(bcum - anchor))
        qt = jnp.where(upper, qa * fac, 0.0)
        kt = jnp.where(upper, 0.0, kk * fac)
        a = lax.dot_general(qt.astype(BF16), kt.astype(BF16), _NT, preferred_element_type=F32)
        if bs < rows:
            a = jnp.where((ti & -bs) == (si & -bs), a, 0.0)
        attn = attn + a
        hs //= 2

    vb = v.astype(BF16)
    near = qa * kk
    intra = jnp.sum(near, axis=1, keepdims=True) * v
    pad = jnp.zeros((HG_FINE, dk), F32)
    for idx, val in enumerate((kk, bcum, v)):
        sh_ref[idx, 0:HG_FINE, :] = pad
        sh_ref[idx, HG_FINE:, :] = val
    for delta in range(1, HG_FINE):
        back = slice(HG_FINE - delta, HG_FINE - delta + rows)
        prod = qa * sh_ref[0, back, :] * jnp.exp(jnp.minimum(bcum - sh_ref[1, back, :], 0.0))
        prod = jnp.where((row & (HG_FINE - 1)) >= delta, prod, 0.0)
        intra = intra + jnp.sum(prod, axis=1, keepdims=True) * sh_ref[2, back, :]
    intra = intra + jnp.dot(attn.astype(BF16), vb, preferred_element_type=F32)
    st = st_ref[...]
    cross = lax.dot_general((qa * jnp.exp(bcum)).astype(BF16), st.astype(BF16), _NT,
                            preferred_element_type=F32)
    blast = bcum[rows - 1:rows, :]
    kb = (kk * jnp.exp(blast - bcum)).astype(BF16)
    st_ref[...] = st * jnp.exp(blast) + lax.dot_general(vb, kb, _TN, preferred_element_type=F32)
    return _rms(intra + cross) * ng * _silu(g)


def _hgrn_lower_bound(lb_raw, layer):
    e = jnp.exp(lb_raw - jnp.max(lb_raw, axis=0, keepdims=True))
    soft = e / jnp.sum(e, axis=0, keepdims=True)
    return jnp.sum(soft[1:layer + 1], axis=0, keepdims=True)


def _inproj_hgrn_kernel(x_ref, g_ref, w_ref, ws_ref, lb_ref, ng_ref, tri_ref, o_ref, zd_ref,
                        z_ref, st_ref, sh_ref, *, layer):
    c = pl.program_id(1)
    gw = o_ref.shape[1]
    dk = gw // HG_HEADS

    @pl.when((pl.program_id(0) == 0) & (c == 0))
    def _():
        z_ref[...] = jnp.zeros_like(z_ref)

    @pl.when(c <= 1)
    def _():
        st_ref[...] = jnp.zeros_like(st_ref)

    h = (_rms(x_ref[...]) * g_ref[...]).astype(BF16)
    z_ref[c % 2] = jnp.dot(h, w_ref[...], preferred_element_type=F32)
    zd_ref[...] = jnp.dot(h, ws_ref[...], preferred_element_type=F32)

    prev = (c + 1) % 2
    lb = _hgrn_lower_bound(lb_ref[...], layer)
    tri = tri_ref[...]
    for hd in range(HG_HEADS):
        cols = [slice(part * gw + hd * dk, part * gw + (hd + 1) * dk) for part in range(4)]
        hs = slice(hd * dk, (hd + 1) * dk)
        out = _hgrn_head(z_ref[prev, :, cols[0]], z_ref[prev, :, cols[1]], z_ref[prev, :, cols[2]],
                         z_ref[prev, :, cols[3]], lb[:, hs], ng_ref[:, hs], tri, st_ref.at[hd], sh_ref.at[hd])
        o_ref[:, hs] = out.astype(BF16)


def _inproj_hgrn(x, g, w, layer_idx, w_side, lb_raw, norm_g, batch, seq, layer):
    n, d = x.shape
    gw = norm_g.shape[0]
    dk = gw // HG_HEADS
    blk = HG_BLOCK
    nblk = seq // blk
    ns = w_side.shape[1]
    depth = lb_raw.shape[0]
    const = lambda b, c: (0, 0)
    return pl.pallas_call(
        functools.partial(_inproj_hgrn_kernel, layer=layer),
        grid=(batch, nblk + 1),
        in_specs=[
            pl.BlockSpec((blk, d), lambda b, c: (b * nblk + jnp.minimum(c, nblk - 1), 0)),
            pl.BlockSpec((1, d), const),
            pl.BlockSpec((None, d, 4 * gw), lambda b, c: (layer_idx, 0, 0), pipeline_mode=pl.Buffered(1)),
            pl.BlockSpec((d, ns), const, pipeline_mode=pl.Buffered(1)),
            pl.BlockSpec((depth, gw), const),
            pl.BlockSpec((1, gw), const),
            pl.BlockSpec((blk, blk), const),
        ],
        out_specs=[
            pl.BlockSpec((blk, gw), lambda b, c: (b * nblk + jnp.maximum(c - 1, 0), 0)),
            pl.BlockSpec((blk, ns), lambda b, c: (b * nblk + jnp.minimum(c, nblk - 1), 0)),
        ],
        out_shape=[jax.ShapeDtypeStruct((n, gw), BF16), jax.ShapeDtypeStruct((n, ns), F32)],
        scratch_shapes=[
            pltpu.VMEM((2, blk, 4 * gw), F32),
            pltpu.VMEM((HG_HEADS, dk, dk), F32),
            pltpu.VMEM((HG_HEADS, 3, blk + HG_FINE, dk), F32),
        ],
        compiler_params=_params(("arbitrary", "arbitrary"), SIDE_VMEM_LIMIT),
        name="inproj_hgrn",
    )(x, g.reshape(1, d), w, w_side, lb_raw, norm_g.reshape(1, gw), jnp.tril(jnp.ones((blk, blk), BF16)))


def _dsa_prep_kernel(zd_ref, cqg_ref, ckvg_ref, wuq_ref, qng_ref, wqit_ref,
                     q_ref, qit_ref, kv_ref, kix_ref, wht_ref):
    zd = zd_ref[...]
    cq = (_rms(zd[:, :DSA_Q_RANK]) * cqg_ref[...]).astype(BF16)
    qf = jnp.dot(cq, wuq_ref[...], preferred_element_type=F32)
    for i in range(q_ref.shape[0]):
        rs = slice(i * Q_BLOCK, (i + 1) * Q_BLOCK)
        for h in range(DSA_HEADS):
            cs = slice(h * DSA_KV_RANK, (h + 1) * DSA_KV_RANK)
            q_ref[i, h] = (_rms(qf[rs, cs]) * qng_ref[...] * (DSA_KV_RANK ** -0.5 * LOG2E)).astype(BF16)
    qit = lax.dot_general(wqit_ref[...], cq, _NT, preferred_element_type=F32)
    qit = (qit * (IDX_DIM ** -0.5)).astype(BF16)
    for i in range(q_ref.shape[0]):
        for h in range(IDX_HEADS):
            c = (i * IDX_HEADS + h) * Q_BLOCK
            qit_ref[:, c:c + Q_BLOCK] = qit[h * LANES:(h + 1) * LANES, i * Q_BLOCK:(i + 1) * Q_BLOCK]
    c0 = DSA_Q_RANK
    c1 = c0 + DSA_KV_RANK
    kv_ref[...] = (_rms(zd[:, c0:c1]) * ckvg_ref[...]).astype(BF16)
    kix_ref[...] = zd[:, c1:c1 + LANES].astype(BF16)
    wht = jnp.transpose(zd[:, c1 + LANES:c1 + 2 * LANES] * (IDX_HEADS ** -0.5))
    wht_ref[...] = wht[:IDX_HEADS, :]


def _dsa_prep(zd, cq_g, ckv_g, w_uq, qn_g, w_qit, *, tm):
    n, wd = zd.shape
    dq = w_uq.shape[1]
    dqi = w_qit.shape[0]
    full = lambda i: (0, 0)
    rows = lambda i: (i, 0)
    cols = lambda i: (0, i)
    return pl.pallas_call(
        _dsa_prep_kernel,
        grid=(n // tm,),
        in_specs=[
            pl.BlockSpec((tm, wd), rows),
            pl.BlockSpec((1, DSA_Q_RANK), full),
            pl.BlockSpec((1, DSA_KV_RANK), full),
            pl.BlockSpec((DSA_Q_RANK, dq), full),
            pl.BlockSpec((1, DSA_KV_RANK), full),
            pl.BlockSpec((dqi, DSA_Q_RANK), full),
        ],
        out_specs=[
            pl.BlockSpec((tm // Q_BLOCK, DSA_HEADS, Q_BLOCK, DSA_KV_RANK), lambda i: (i, 0, 0, 0)),
            pl.BlockSpec((LANES, IDX_HEADS * tm), cols),
            pl.BlockSpec((tm, DSA_KV_RANK), rows),
            pl.BlockSpec((tm, LANES), rows),
            pl.BlockSpec((IDX_HEADS, tm), cols),
        ],
        out_shape=[
            jax.ShapeDtypeStruct((n // Q_BLOCK, DSA_HEADS, Q_BLOCK, DSA_KV_RANK), BF16),
            jax.ShapeDtypeStruct((LANES, IDX_HEADS * n), BF16),
            jax.ShapeDtypeStruct((n, DSA_KV_RANK), BF16),
            jax.ShapeDtypeStruct((n, LANES), BF16),
            jax.ShapeDtypeStruct((IDX_HEADS, n), F32),
        ],
        compiler_params=_params(("arbitrary",)),
        name="dsa_prep",
    )(zd, cq_g.reshape(1, -1), ckv_g.reshape(1, -1), w_uq, qn_g.reshape(1, -1), w_qit)


def _bit_planes(key):
    rows = key.shape[0]
    v = [key[8 * k:8 * (k + 1), :] for k in range(rows // 8)]
    sub = lax.broadcasted_iota(I32, v[0].shape, 0)

    def swap(lo, hi, j, m):
        return (lo & ~m) | ((hi >> j) & m), (hi & m) | ((lo << j) & ~m)

    for g in range(0, len(v), 4):
        for a, b in ((0, 2), (1, 3)):
            v[g + a], v[g + b] = swap(v[g + a], v[g + b], 16, 0x0000FFFF)
        for a, b in ((0, 1), (2, 3)):
            v[g + a], v[g + b] = swap(v[g + a], v[g + b], 8, 0x00FF00FF)
    for j, m in ((4, 0x0F0F0F0F), (2, 0x33333333), (1, 0x55555555)):
        high = (sub & j) != 0
        keep = jnp.where(high, m, ~m)
        for k in range(len(v)):
            down, up = pltpu.roll(v[k], j, axis=0), pltpu.roll(v[k], 8 - j, axis=0)
            moved = jnp.where(high, down << j, up >> j)
            v[k] = (v[k] & keep) | (moved & ~keep)
    return jnp.concatenate(v, axis=0)


def _dsa_select_kernel(qit_ref, wht_ref, kix_ref, wg_in, wu_in, wd_in, m_ref, wg_out, wu_out, wd_out,
                       key_ref, jc_ref, plane_ref, *, ksel, idx_bits):
    for src, dst in ((wg_in, wg_out), (wu_in, wu_out), (wd_in, wd_out)):
        dst[...] = src[...].astype(BF16)

    qb = pl.program_id(1)
    ntile = qb + 1
    ntile_all = m_ref.shape[2]
    rowi = lax.broadcasted_iota(I32, (LANES, Q_BLOCK), 0)
    coli = lax.broadcasted_iota(I32, (LANES, Q_BLOCK), 1)
    q_chunk = (qb * Q_BLOCK + coli) // CHUNK
    pairs = IDX_HEADS // 2
    w_pair = [jnp.concatenate([wht_ref[2 * p:2 * p + 1, :], wht_ref[2 * p + 1:2 * p + 2, :]], axis=1)
              for p in range(pairs)]

    def tile_rows(j):
        return pl.ds(pl.multiple_of(j * LANES, LANES), LANES)

    def admissible(j):
        return ((j * LANES + rowi) // CHUNK) <= q_chunk

    def score_tile(jj, carry):
        for t in range(2):
            j = 2 * jj + t
            kt = kix_ref[tile_rows(j), :]
            sc = None
            for p in range(pairs):
                s2 = jnp.dot(kt, qit_ref[:, 2 * p * Q_BLOCK:2 * (p + 1) * Q_BLOCK],
                             preferred_element_type=F32)
                c2 = w_pair[p] * jnp.maximum(s2, 0.0)
                c = c2[:, :Q_BLOCK] + c2[:, Q_BLOCK:]
                sc = c if sc is None else sc + c
            bits = pltpu.bitcast(sc, I32)
            key = bits ^ ((bits >> 31) & 0x7FFFFFFF)
            key = jnp.where(admissible(j), key, INT_MIN)
            key_ref[tile_rows(j), :] = key
            plane_ref[tile_rows(j), :] = _bit_planes(key)
        return carry

    @pl.when(qb == 0)
    def _():
        key_ref[...] = jnp.full(key_ref.shape, INT_MIN, I32)
        group_row = lax.broadcasted_iota(I32, plane_ref.shape, 0) & (PLANE_GROUP - 1)
        plane_ref[...] = jnp.where(group_row == 0, -1, 0)

    lax.fori_loop(0, (ntile + 1) // 2, score_tile, 0)

    def count(pred_fn):
        def body(jj, acc):
            for t in range(SCAN_TILES):
                j = SCAN_TILES * jj + t
                acc = acc + pred_fn(j, key_ref[tile_rows(j), :]).astype(I32)
            return acc
        trips = (ntile + SCAN_TILES - 1) // SCAN_TILES
        acc = lax.fori_loop(0, trips, body, jnp.zeros((LANES, Q_BLOCK), I32))
        return jnp.sum(acc, axis=0, keepdims=True)

    nword = plane_ref.shape[0] // (8 * PLANE_GROUP)

    def plane(v, i):
        return plane_ref[pl.ds(v * 8 * PLANE_GROUP + i, 8, stride=PLANE_GROUP), :]

    def total(words):
        acc = lax.population_count(words[0])
        for w in words[1:]:
            acc = acc + lax.population_count(w)
        return jnp.sum(acc, axis=0, keepdims=True)

    def radix_pair(t, c):
        alive, above, t_u = c
        i = 2 * t
        flip = jnp.where(t == 0, -1, 0)
        set1 = [a & (plane(v, i) ^ flip) for v, a in enumerate(alive)]
        clr1 = [a ^ s for a, s in zip(alive, set1)]
        p2 = [plane(v, i + 1) for v in range(nword)]
        set1_set2 = [s & p for s, p in zip(set1, p2)]
        clr1_set2 = [s & p for s, p in zip(clr1, p2)]
        n1, n11, n01 = total(set1), total(set1_set2), total(clr1_set2)
        take1 = (above + n1) >= ksel
        above = jnp.where(take1, above, above + n1)
        n2 = jnp.where(take1, n11, n01)
        take2 = (above + n2) >= ksel
        above = jnp.where(take2, above, above + n2)
        alive = tuple(
            jnp.where(take1, jnp.where(take2, ss, s ^ ss), jnp.where(take2, cs, c0 ^ cs))
            for s, c0, ss, cs in zip(set1, clr1, set1_set2, clr1_set2))
        t_u = (t_u | jnp.where(take1, jnp.left_shift(jnp.int32(1), 31 - i), 0)
               | jnp.where(take2, jnp.left_shift(jnp.int32(1), 30 - i), 0))
        return alive, above, t_u

    start = (tuple(jnp.full((8, Q_BLOCK), -1, I32) for _ in range(nword)),
             jnp.zeros((1, Q_BLOCK), I32), jnp.zeros((1, Q_BLOCK), I32))
    alive, above, t_u = lax.fori_loop(0, 16, radix_pair, start)
    thr = t_u ^ INT_MIN
    ties = lax.population_count(alive[0])
    for a in alive[1:]:
        ties = ties + lax.population_count(a)
    cnt_t = above + jnp.sum(ties, axis=0, keepdims=True)
    tied = jnp.max(jnp.where((cnt_t > ksel) & (thr > INT_MIN), 1.0, 0.0))

    jc_ref[...] = jnp.full(jc_ref.shape, 2 ** 31 - 1, I32)

    @pl.when(tied > 0.0)
    def _():
        need = ksel - count(lambda j, k: k > thr)

        def index_bit(i, j_c):
            cand = j_c | jnp.left_shift(jnp.int32(1), idx_bits - 1 - i)
            cnt = count(lambda j, k: (k == thr) & ((j * LANES + rowi) < cand))
            return jnp.where(cnt < need, cand, j_c)

        j_c = lax.fori_loop(0, idx_bits, index_bit, jnp.zeros((1, Q_BLOCK), I32))
        jc_ref[...] = jnp.broadcast_to(j_c, jc_ref.shape)

    j_c = jc_ref[0:1, :]
    eye = (rowi == coli).astype(BF16)

    group = KV_TILE // LANES

    def write_group(g, carry):
        for t in range(group):
            j = g * group + t
            k = key_ref[tile_rows(j), :]
            sel = (k > thr) | ((k == thr) & ((j * LANES + rowi) <= j_c))
            sel = jnp.where(sel & admissible(j), 1.0, 0.0).astype(BF16)
            sel_t = lax.dot_general(eye, sel, _NT, preferred_element_type=F32)
            m_ref[0, 0, j] = ((sel_t - 1.0) * -NEG_BIG).astype(BF16)
        return carry

    ngroup = (ntile + group - 1) // group
    lax.fori_loop(0, ngroup, write_group, 0)

    def blank_tile(j, carry):
        m_ref[0, 0, j] = jnp.full((Q_BLOCK, LANES), NEG_BIG, BF16)
        return carry

    lax.fori_loop(ngroup * group, ntile_all, blank_tile, 0)


def _dsa_select(qit, wht, kix, ffn_weights, layer, batch, seq, ksel):
    nqb = seq // Q_BLOCK
    nkt = seq // LANES
    w_in, w_out, w_shape = _cast_specs([(w, layer) for w in ffn_weights], lambda b, q: b * nqb + q,
                                       batch * nqb)
    return pl.pallas_call(
        functools.partial(_dsa_select_kernel, ksel=ksel, idx_bits=int(math.log2(seq))),
        grid=(batch, nqb),
        in_specs=[
            pl.BlockSpec((LANES, IDX_HEADS * Q_BLOCK), lambda b, q: (0, b * nqb + q)),
            pl.BlockSpec((IDX_HEADS, Q_BLOCK), lambda b, q: (0, b * nqb + q)),
            pl.BlockSpec((seq, LANES), lambda b, q: (b, 0)),
            *w_in,
        ],
        out_specs=[pl.BlockSpec((1, 1, nkt, Q_BLOCK, LANES), lambda b, q: (b, q, 0, 0, 0)), *w_out],
        out_shape=[jax.ShapeDtypeStruct((batch, nqb, nkt, Q_BLOCK, LANES), BF16), *w_shape],
        scratch_shapes=[pltpu.VMEM((seq, Q_BLOCK), I32), pltpu.VMEM((8, Q_BLOCK), I32),
                        pltpu.VMEM((seq, Q_BLOCK), I32)],
        compiler_params=_params(("arbitrary", "arbitrary")),
        name="dsa_select",
    )(qit, wht, kix, *ffn_weights)


def _rel_bucket(rel):
    nb = REL_BUCKETS // 2
    max_exact = nb // 2
    ret = jnp.where(rel > 0, nb, 0)
    n = jnp.abs(rel)
    nf = jnp.maximum(n, 1).astype(F32)
    large = max_exact + (jnp.log(nf / max_exact) / math.log(REL_MAX_DIST / max_exact)
                         * (nb - max_exact)).astype(I32)
    large = jnp.minimum(large, nb - 1)
    return ret + jnp.where(n < max_exact, n, large)


NEAR_TILES = 3


def _dsa_attn_kernel(q_ref, kv_ref, mask_ref, rb_ref, wuv_ref, o_ref,
                     m_ref, l_ref, alpha_ref, acc_ref, corr_ref, s_ref, p_ref, madd_ref):
    b, qb = pl.program_id(0), pl.program_id(1)
    sub = KV_TILE // LANES
    far_bucket = REL_BUCKETS // 2 - 1
    half = DSA_HEADS * Q_BLOCK // 2

    def head_rows(h):
        return slice(h * Q_BLOCK, (h + 1) * Q_BLOCK)

    @pl.when((b == 0) & (qb == 0))
    def _():
        ti = lax.broadcasted_iota(I32, (Q_BLOCK, LANES), 0)
        si = lax.broadcasted_iota(I32, (Q_BLOCK, LANES), 1)
        for oi in range(NEAR_TILES):
            bucket = _rel_bucket((oi - (NEAR_TILES - 1)) * LANES + si - ti)
            for h in range(DSA_HEADS):
                tbl = jnp.zeros((Q_BLOCK, LANES), F32)
                for bk in range(REL_BUCKETS):
                    tbl = jnp.where(bucket == bk, rb_ref[bk, h], tbl)
                corr_ref[oi, head_rows(h), :] = (tbl - rb_ref[far_bucket, h]) * LOG2E

    m_ref[...] = jnp.full_like(m_ref, NEG_BIG)
    l_ref[...] = jnp.zeros_like(l_ref)
    acc_ref[...] = jnp.zeros_like(acc_ref)
    q_all = q_ref[0].reshape(DSA_HEADS * Q_BLOCK, DSA_KV_RANK)

    def key_step(kt, carry):
        kvt = kv_ref[pl.ds(pl.multiple_of(kt * KV_TILE, KV_TILE), KV_TILE), :]
        for part in range(2):
            rs = slice(part * half, (part + 1) * half)
            s_ref[rs, :] = lax.dot_general(q_all[rs], kvt, _NT, preferred_element_type=F32)
        for j in range(sub):
            d = kt * sub + j - qb

            @pl.when((d > -NEAR_TILES) & (d <= 0))
            def _(j=j, d=d):
                s_ref[:, j * LANES:(j + 1) * LANES] += corr_ref[d + NEAR_TILES - 1]

        for j in range(sub):
            madd_ref[:, j * LANES:(j + 1) * LANES] = mask_ref[0, 0, kt * sub + j].astype(F32)
        groups_per_head = Q_BLOCK // SM_ROWS
        for g in range(DSA_HEADS * groups_per_head):
            rs = slice(g * SM_ROWS, (g + 1) * SM_ROWS)
            qg = g % groups_per_head
            s = s_ref[rs, :] + madd_ref[qg * SM_ROWS:(qg + 1) * SM_ROWS, :]
            m_old = m_ref[rs, :]
            m_new = jnp.maximum(m_old, jnp.max(s, axis=1, keepdims=True))
            alpha = jnp.exp2(m_old - m_new)
            p = jnp.exp2(s - jnp.tile(m_new, (1, sub)))
            l_ref[rs, :] = alpha * l_ref[rs, :] + jnp.sum(p, axis=1, keepdims=True)
            alpha_ref[rs, :] = alpha
            p_ref[rs, :] = p.astype(BF16)
            m_ref[rs, :] = m_new
        for part in range(2):
            rs = slice(part * half, (part + 1) * half)
            pv = jnp.dot(p_ref[rs, :], kvt, preferred_element_type=F32)
            acc_ref[rs, :] = jnp.tile(alpha_ref[rs, :], (1, DSA_KV_RANK // LANES)) * acc_ref[rs, :] + pv
        return carry

    lax.fori_loop(0, qb // sub + 1, key_step, 0)

    dv = wuv_ref.shape[2]
    for h in range(DSA_HEADS):
        rs = head_rows(h)
        o = (acc_ref[rs, :] / jnp.tile(l_ref[rs, :], (1, DSA_KV_RANK // LANES))).astype(BF16)
        o_ref[:, h * dv:(h + 1) * dv] = jnp.dot(o, wuv_ref[h], preferred_element_type=F32).astype(BF16)


def _dsa_attn(q, kv, mask, rel_bias, w_uv, batch, seq):
    n = kv.shape[0]
    nqb = seq // Q_BLOCK
    dv = w_uv.shape[2]
    rows = DSA_HEADS * Q_BLOCK
    return pl.pallas_call(
        _dsa_attn_kernel,
        grid=(batch, nqb),
        in_specs=[
            pl.BlockSpec((1, DSA_HEADS, Q_BLOCK, DSA_KV_RANK), lambda b, qb: (b * nqb + qb, 0, 0, 0)),
            pl.BlockSpec((seq, DSA_KV_RANK), lambda b, qb: (b, 0)),
            pl.BlockSpec((1, 1) + mask.shape[2:], lambda b, qb: (b, qb, 0, 0, 0)),
            pl.BlockSpec(memory_space=pltpu.SMEM),
            pl.BlockSpec(w_uv.shape, lambda b, qb: (0, 0, 0)),
        ],
        out_specs=pl.BlockSpec((Q_BLOCK, DSA_HEADS * dv), lambda b, qb: (b * nqb + qb, 0)),
        out_shape=jax.ShapeDtypeStruct((n, DSA_HEADS * dv), BF16),
        scratch_shapes=[
            pltpu.VMEM((rows, LANES), F32),
            pltpu.VMEM((rows, LANES), F32),
            pltpu.VMEM((rows, LANES), F32),
            pltpu.VMEM((rows, DSA_KV_RANK), F32),
            pltpu.VMEM((NEAR_TILES, rows, LANES), F32),
            pltpu.VMEM((rows, KV_TILE), F32),
            pltpu.VMEM((rows, KV_TILE), BF16),
            pltpu.VMEM((Q_BLOCK, KV_TILE), F32),
        ],
        compiler_params=_params(("arbitrary", "arbitrary")),
        name="dsa_attn",
    )(q, kv, mask, rel_bias, w_uv)


def _pad_cols(w, width):
    return jnp.pad(w, ((0, 0), (0, width - w.shape[1])))


def kernel(x, ln_mix_g, ln_ffn_g, w_ffn_gate, w_ffn_up, w_ffn_down, rel_bias, ev_w_in, ev_w_out, sgu_ln_g, sgu_ln_b, sgu_w_s, sgu_b_s, od_w_in, od_w_out, hgrn_lb, hgrn_norm_g, dsa_cq_g, dsa_ckv_g, dsa_w_uq, dsa_qnorm_g, dsa_w_qidx, dsa_w_uv):
    batch, seq, d = x.shape
    n = batch * seq
    depth = ln_mix_g.shape[0]
    ksel = min(TOPK_MAX, seq // 4)
    tm = min(PROJ_ROWS, n)
    xf = x.reshape(n, d)
    ffn_f32 = (w_ffn_gate, w_ffn_up, w_ffn_down)
    ev_in_all, od_in_all = ev_w_in.astype(BF16), od_w_in.astype(BF16)
    for layer in range(depth):
        j = layer // 2
        if layer % 2 == 0:
            z, *ffn_w = _norm_matmul(xf, ln_mix_g[layer], ev_in_all, j, ev_in_all.shape[2], tm=tm,
                                     tn=PROJ_COLS, out_dtype=BF16, cast=[(w, layer) for w in ffn_f32])
            a1 = _retention(z, batch, seq)
            a2 = _sgu(z, sgu_ln_g[j], sgu_ln_b[j], sgu_w_s[j], sgu_b_s[j], rows=256)
            w_out = ev_w_out[j]
        else:
            w_in = od_w_in[j]
            gw = d // 2
            c = 4 * gw
            c_kidx = c + DSA_Q_RANK + DSA_KV_RANK
            w_dsa = jnp.concatenate([
                w_in[:, c:c_kidx],
                _pad_cols(w_in[:, c_kidx:c_kidx + IDX_DIM], LANES),
                _pad_cols(w_in[:, c_kidx + IDX_DIM:], LANES),
            ], axis=1).astype(BF16)
            a1, zd = _inproj_hgrn(xf, ln_mix_g[layer], od_in_all, j, w_dsa, hgrn_lb, hgrn_norm_g[j],
                                  batch, seq, layer)
            w_qit = jnp.pad(dsa_w_qidx[j].T.reshape(IDX_HEADS, IDX_DIM, DSA_Q_RANK),
                            ((0, 0), (0, LANES - IDX_DIM), (0, 0))).reshape(IDX_HEADS * LANES, DSA_Q_RANK)
            q, qit, kv, kix, wht = _dsa_prep(zd, dsa_cq_g[j], dsa_ckv_g[j], dsa_w_uq[j].astype(BF16),
                                             dsa_qnorm_g[j], w_qit.astype(BF16), tm=256)
            mask, *ffn_w = _dsa_select(qit, wht, kix, ffn_f32, layer, batch, seq, ksel)
            a2 = _dsa_attn(q, kv, mask, rel_bias, dsa_w_uv[j].astype(BF16), batch, seq)
            w_out = od_w_out[j]
        xf = _outproj(a1, a2, w_out.astype(BF16), xf, tm=min(OUT_ROWS, n), tn=OUT_COLS)
        xf = _ffn(xf, ln_ffn_g[layer], *(w[None] for w in ffn_w), 0, tm=min(FFN_ROWS, n), tf=FFN_COLS)
    return xf.reshape(batch, seq, d)
```

```python
import functools
import math

import jax
import jax.numpy as jnp
from jax import lax
from jax.experimental import pallas as pl
from jax.experimental.pallas import tpu as pltpu

F32 = jnp.float32
BF16 = jnp.bfloat16
I32 = jnp.int32

EPS = 1e-6
CHUNK = 64
LANES = 128
ROPE_BASE = 10000.0
RET_HEADS = 4
SGU_WINDOW = 128
SGU_GROUPS = 4
HG_HEADS = 8
DSA_HEADS = 8
DSA_Q_RANK = 384
DSA_KV_RANK = 256
IDX_HEADS = 16
IDX_DIM = 64
TOPK_MAX = 256
Q_BLOCK = 128
KV_TILE = 512
SM_ROWS = 64
LOG2E = math.log2(math.e)
REL_BUCKETS = 32
REL_MAX_DIST = 256
NEG_BIG = -1e30
INT_MIN = -(2 ** 31)
SCAN_TILES = 2
PLANE_ROWS = 256

RET_BLOCK = 256
RET_HEADS_PER_STEP = 4
HG_BLOCK = 256
HG_FINE = 4
VMEM_LIMIT = 48 * 1024 * 1024
PROJ_ROWS = 1024
PROJ_COLS = 1024
OUT_ROWS = 512
OUT_COLS = 2048
FFN_ROWS = 1024
FFN_VMEM_LIMIT = 58 * 1024 * 1024
SIDE_VMEM_LIMIT = 56 * 1024 * 1024
FFN_COLS = 512

_NT = (((1,), (1,)), ((), ()))
_TN = (((0,), (0,)), ((), ()))


def _params(semantics, vmem_limit=VMEM_LIMIT):
    return pltpu.CompilerParams(dimension_semantics=semantics, vmem_limit_bytes=vmem_limit)


def _silu(x):
    return x * jax.nn.sigmoid(x)


def _rms(x):
    return x * lax.rsqrt(jnp.mean(x * x, axis=-1, keepdims=True) + EPS)


def _cast_blocks(rows, steps):
    units = rows // 16
    blocks = max(k for k in range(1, min(units, steps) + 1) if units % k == 0)
    return rows // blocks, blocks


def _cast_specs(weights, step_of, steps):
    ins, outs, shapes = [], [], []
    for w, layer in weights:
        rows, blocks = _cast_blocks(w.shape[1], steps)

        def index(*ids, blocks=blocks):
            return (jnp.minimum(step_of(*ids), blocks - 1), 0)

        ins.append(pl.BlockSpec((None, rows, w.shape[2]), lambda *ids, layer=layer, index=index: (layer,) + index(*ids)))
        outs.append(pl.BlockSpec((rows, w.shape[2]), index))
        shapes.append(jax.ShapeDtypeStruct(w.shape[1:], BF16))
    return ins, outs, shapes


def _norm_matmul_kernel(x_ref, g_ref, w_ref, *rest):
    ncast = (len(rest) - 2) // 2
    o_ref, h_ref = rest[ncast], rest[-1]
    for src, dst in zip(rest[:ncast], rest[ncast + 1:-1]):
        dst[...] = src[...].astype(BF16)

    @pl.when(pl.program_id(1) == 0)
    def _():
        h_ref[...] = (_rms(x_ref[...]) * g_ref[...]).astype(BF16)

    o_ref[...] = jnp.dot(h_ref[...], w_ref[...], preferred_element_type=F32).astype(o_ref.dtype)


def _norm_matmul(x, g, w, layer, nout, *, tm, tn, out_dtype=F32, cast=()):
    n, d = x.shape
    ncols = nout // tn
    c_in, c_out, c_shape = _cast_specs(cast, lambda i, j: i * ncols + j, (n // tm) * ncols)
    return pl.pallas_call(
        _norm_matmul_kernel,
        grid=(n // tm, ncols),
        in_specs=[
            pl.BlockSpec((tm, d), lambda i, j: (i, 0)),
            pl.BlockSpec((1, d), lambda i, j: (0, 0)),
            pl.BlockSpec((None, d, tn), lambda i, j: (layer, 0, j)),
            *c_in,
        ],
        out_specs=[pl.BlockSpec((tm, tn), lambda i, j: (i, j)), *c_out],
        out_shape=[jax.ShapeDtypeStruct((n, nout), out_dtype), *c_shape],
        scratch_shapes=[pltpu.VMEM((tm, d), BF16)],
        compiler_params=_params(("arbitrary", "arbitrary"), SIDE_VMEM_LIMIT),
        name="norm_matmul",
    )(x, g.reshape(1, d), w, *(w_c for w_c, _ in cast))


def _outproj_kernel(a1_ref, a2_ref, w1_ref, w2_ref, r_ref, o_ref):
    acc = jnp.dot(a1_ref[...], w1_ref[...], preferred_element_type=F32)
    acc += jnp.dot(a2_ref[...], w2_ref[...], preferred_element_type=F32)
    o_ref[...] = r_ref[...] + acc


def _outproj(a1, a2, w, res, *, tm, tn):
    n, half = a1.shape
    d = w.shape[1]
    return pl.pallas_call(
        _outproj_kernel,
        grid=(n // tm, d // tn),
        in_specs=[
            pl.BlockSpec((tm, half), lambda i, j: (i, 0)),
            pl.BlockSpec((tm, half), lambda i, j: (i, 0)),
            pl.BlockSpec((half, tn), lambda i, j: (0, j)),
            pl.BlockSpec((half, tn), lambda i, j: (1, j)),
            pl.BlockSpec((tm, tn), lambda i, j: (i, j)),
        ],
        out_specs=pl.BlockSpec((tm, tn), lambda i, j: (i, j)),
        out_shape=jax.ShapeDtypeStruct((n, d), F32),
        compiler_params=_params(("arbitrary", "arbitrary")),
        name="outproj",
    )(a1, a2, w, w, res)


def _ffn_kernel(x_ref, g_ref, wg_ref, wu_ref, wd_ref, o_ref, h_ref):
    @pl.when(pl.program_id(1) == 0)
    def _():
        x = x_ref[...]
        h_ref[...] = (_rms(x) * g_ref[...]).astype(BF16)
        o_ref[...] = x

    h = h_ref[...]
    a = jnp.dot(h, wg_ref[...], preferred_element_type=F32)
    u = jnp.dot(h, wu_ref[...], preferred_element_type=F32)
    act = (_silu(a) * u).astype(BF16)
    o_ref[...] += jnp.dot(act, wd_ref[...], preferred_element_type=F32)


def _ffn(x, g, wg, wu, wd, layer, *, tm, tf):
    n, d = x.shape
    dff = wg.shape[2]
    return pl.pallas_call(
        _ffn_kernel,
        grid=(n // tm, dff // tf),
        in_specs=[
            pl.BlockSpec((tm, d), lambda i, f: (i, 0)),
            pl.BlockSpec((1, d), lambda i, f: (0, 0)),
            pl.BlockSpec((None, d, tf), lambda i, f: (layer, 0, f)),
            pl.BlockSpec((None, d, tf), lambda i, f: (layer, 0, f)),
            pl.BlockSpec((None, tf, d), lambda i, f: (layer, f, 0)),
        ],
        out_specs=pl.BlockSpec((tm, d), lambda i, f: (i, 0)),
        out_shape=jax.ShapeDtypeStruct((n, d), F32),
        scratch_shapes=[pltpu.VMEM((tm, d), BF16)],
        compiler_params=_params(("arbitrary", "arbitrary"), FFN_VMEM_LIMIT),
        name="ffn",
    )(x, g.reshape(1, d), wg, wu, wd)


def _retention_kernel(q_ref, k_ref, v_ref, g_ref, cos_ref, sin_ref, d_ref, xi_ref, zeta_ref,
                      gl_ref, o_ref, state_ref):
    @pl.when(pl.program_id(2) == 0)
    def _():
        state_ref[...] = jnp.zeros_like(state_ref)

    cos = cos_ref[...]
    sin = sin_ref[...]
    half = cos.shape[1]
    dk = 2 * half

    def rot(t):
        t1, t2 = t[:, :half], t[:, half:]
        return jnp.concatenate([t1 * cos - t2 * sin, t1 * sin + t2 * cos], axis=1)

    for i in range(RET_HEADS_PER_STEP):
        cs = slice(i * dk, (i + 1) * dk)
        q = rot(q_ref[:, cs].astype(F32))
        k = rot(k_ref[:, cs].astype(F32)) * (dk ** -0.5)
        qb = q.astype(BF16)
        vb = v_ref[:, cs].astype(BF16)
        scores = lax.dot_general(qb, k.astype(BF16), _NT, preferred_element_type=F32) * d_ref[i]
        intra = jnp.dot(scores.astype(BF16), vb, preferred_element_type=F32)
        state = state_ref[i]
        cross = jnp.dot(qb, state.astype(BF16), preferred_element_type=F32) * xi_ref[i]
        kz = (k * zeta_ref[i]).astype(BF16)
        state_ref[i] = state * gl_ref[i] + lax.dot_general(kz, vb, _TN, preferred_element_type=F32)
        o_ref[:, cs] = (_rms(intra + cross) * _silu(g_ref[:, cs].astype(F32))).astype(BF16)


def _retention_tables(seq, dk):
    blk = RET_BLOCK
    pos = jnp.arange(seq, dtype=F32)
    inv = ROPE_BASE ** (-jnp.arange(0, dk, 2, dtype=F32) / dk)
    ang = pos[:, None] * inv[None, :]
    log_gamma = jnp.log(1.0 - 2.0 ** (-5.0 - jnp.arange(RET_HEADS, dtype=F32)))
    i = jnp.arange(blk)
    same = (i[:, None] // CHUNK) == (i[None, :] // CHUNK)
    earlier = (i[None, :] // CHUNK) < (i[:, None] // CHUNK)
    diff = (i[:, None] - i[None, :]).astype(F32)
    dist = jnp.where(same, jnp.abs(diff), diff)
    decay = jnp.where((same | earlier)[None], jnp.exp(log_gamma[:, None, None] * dist[None]), 0.0)
    p = jnp.arange(blk, dtype=F32)
    wide = (RET_HEADS, blk, dk)
    xi = jnp.broadcast_to(jnp.exp(log_gamma[:, None] * (p + 1.0))[:, :, None], wide)
    zeta = jnp.broadcast_to(jnp.exp(log_gamma[:, None] * (blk - 1.0 - p))[:, :, None], wide)
    g_blk = jnp.broadcast_to(jnp.exp(log_gamma * blk)[:, None, None], (RET_HEADS, 1, dk))
    return jnp.cos(ang), jnp.sin(ang), decay, xi, zeta, g_blk


def _retention(z, batch, seq):
    n = z.shape[0]
    gw = z.shape[1] // 6
    dk = gw // RET_HEADS
    blk = RET_BLOCK
    nblk = seq // blk
    cos, sin, decay, xi, zeta, g_blk = _retention_tables(seq, dk)

    hp = RET_HEADS_PER_STEP
    groups = RET_HEADS // hp

    def zspec(part):
        return pl.BlockSpec((blk, hp * dk), lambda b, h, c: (b * nblk + c, part * groups + h))

    def hspec(rows, cols):
        return pl.BlockSpec((hp, rows, cols), lambda b, h, c: (h, 0, 0))

    return pl.pallas_call(
        _retention_kernel,
        grid=(batch, groups, nblk),
        in_specs=[
            zspec(0), zspec(1), zspec(2), zspec(3),
            pl.BlockSpec((blk, dk // 2), lambda b, h, c: (c, 0)),
            pl.BlockSpec((blk, dk // 2), lambda b, h, c: (c, 0)),
            hspec(blk, blk), hspec(blk, dk), hspec(blk, dk), hspec(1, dk),
        ],
        out_specs=pl.BlockSpec((blk, hp * dk), lambda b, h, c: (b * nblk + c, h)),
        out_shape=jax.ShapeDtypeStruct((n, gw), BF16),
        scratch_shapes=[pltpu.VMEM((hp, dk, dk), F32)],
        compiler_params=_params(("arbitrary", "arbitrary", "arbitrary")),
        name="retention",
    )(z, z, z, z, cos, sin, decay, xi, zeta, g_blk)


def _gelu(x):
    return 0.5 * x * (1.0 + lax.erf(x * math.sqrt(0.5)))


def _sgu_kernel(u_ref, v_ref, lng_ref, lnb_ref, w_ref, b_ref, o_ref):
    rows, width = v_ref.shape
    dg = width // SGU_GROUPS
    v = _gelu(v_ref[...].astype(F32))
    mu = jnp.mean(v, axis=-1, keepdims=True)
    var = jnp.mean(jnp.square(v - mu), axis=-1, keepdims=True)
    vn = ((v - mu) * lax.rsqrt(var + EPS) * lng_ref[...] + lnb_ref[...]).astype(BF16)
    u = _gelu(u_ref[...].astype(F32))
    ri = lax.broadcasted_iota(I32, (SGU_WINDOW, SGU_WINDOW), 0) // CHUNK
    ci = lax.broadcasted_iota(I32, (SGU_WINDOW, SGU_WINDOW), 1) // CHUNK
    allowed = ci <= ri
    for g in range(SGU_GROUPS):
        wg = jnp.where(allowed, w_ref[g], 0.0).astype(BF16)
        bias = b_ref[g]
        for w in range(rows // SGU_WINDOW):
            rs = slice(w * SGU_WINDOW, (w + 1) * SGU_WINDOW)
            cs = slice(g * dg, (g + 1) * dg)
            mixed = jnp.dot(wg, vn[rs, cs], preferred_element_type=F32) + bias
            o_ref[rs, cs] = (u[rs, cs] * mixed).astype(BF16)


def _sgu(z, ln_g, ln_b, w_s, b_s, *, rows):
    n = z.shape[0]
    gw = z.shape[1] // 6
    return pl.pallas_call(
        _sgu_kernel,
        grid=(n // rows,),
        in_specs=[
            pl.BlockSpec((rows, gw), lambda i: (i, 4)),
            pl.BlockSpec((rows, gw), lambda i: (i, 5)),
            pl.BlockSpec((1, gw), lambda i: (0, 0)),
            pl.BlockSpec((1, gw), lambda i: (0, 0)),
            pl.BlockSpec((SGU_GROUPS, SGU_WINDOW, SGU_WINDOW), lambda i: (0, 0, 0)),
            pl.BlockSpec((SGU_GROUPS, SGU_WINDOW, 1), lambda i: (0, 0, 0)),
        ],
        out_specs=pl.BlockSpec((rows, gw), lambda i: (i, 0)),
        out_shape=jax.ShapeDtypeStruct((n, gw), BF16),
        compiler_params=_params(("arbitrary",)),
        name="sgu",
    )(z, z, ln_g.reshape(1, gw), ln_b.reshape(1, gw), w_s, b_s.reshape(SGU_GROUPS, SGU_WINDOW, 1))


def _hgrn_head(q, f_logits, v, g, lb, ng, tri, st_ref, sh_ref):
    rows, dk = q.shape
    f = lb + (1.0 - lb) * jax.nn.sigmoid(f_logits)
    lf = jnp.log(f)
    kk = 1.0 - f
    qa = _silu(q)

    bcum = None
    rest = lf
    for _ in range(3):
        term = rest.astype(BF16)
        part = jnp.dot(tri, term, preferred_element_type=F32)
        bcum = part if bcum is None else bcum + part
        rest = rest - term.astype(F32)

    row = lax.broadcasted_iota(I32, (rows, dk), 0)
    ti = lax.broadcasted_iota(I32, (rows, rows), 0)
    si = lax.broadcasted_iota(I32, (rows, rows), 1)
    attn = jnp.zeros((rows, rows), F32)
    hs = rows // 2
    while hs >= HG_FINE:
        bs = 2 * hs
        parts = [jnp.broadcast_to(bcum[b * bs + hs - 1:b * bs + hs, :], (bs, dk))
                 for b in range(rows // bs)]
        anchor = parts[0] if len(parts) == 1 else jnp.concatenate(parts, axis=0)
        upper = (row & (bs - 1)) >= hs
        fac = jnp.exp(-jnp.abs(bcum - anchor))
        qt = jnp.where(upper, qa * fac, 0.0)
        kt = jnp.where(upper, 0.0, kk * fac)
        a = lax.dot_general(qt.astype(BF16), kt.astype(BF16), _NT, preferred_element_type=F32)
        if bs < rows:
            a = jnp.where((ti & -bs) == (si & -bs), a, 0.0)
        attn = attn + a
        hs //= 2

    vb = v.astype(BF16)
    near = qa * kk
    intra = jnp.sum(near, axis=1, keepdims=True) * v
    pad = jnp.zeros((HG_FINE, dk), F32)
    for idx, val in enumerate((kk, bcum, v)):
        sh_ref[idx, 0:HG_FINE, :] = pad
        sh_ref[idx, HG_FINE:, :] = val
    for delta in range(1, HG_FINE):
        back = slice(HG_FINE - delta, HG_FINE - delta + rows)
        prod = qa * sh_ref[0, back, :] * jnp.exp(jnp.minimum(bcum - sh_ref[1, back, :], 0.0))
        prod = jnp.where((row & (HG_FINE - 1)) >= delta, prod, 0.0)
        intra = intra + jnp.sum(prod, axis=1, keepdims=True) * sh_ref[2, back, :]
    intra = intra + jnp.dot(attn.astype(BF16), vb, preferred_element_type=F32)
    st = st_ref[...]
    cross = lax.dot_general((qa * jnp.exp(bcum)).astype(BF16), st.astype(BF16), _NT,
                            preferred_element_type=F32)
    blast = bcum[rows - 1:rows, :]
    kb = (kk * jnp.exp(blast - bcum)).astype(BF16)
    st_ref[...] = st * jnp.exp(blast) + lax.dot_general(vb, kb, _TN, preferred_element_type=F32)
    return _rms(intra + cross) * ng * _silu(g)


def _hgrn_lower_bound(lb_raw, layer):
    e = jnp.exp(lb_raw - jnp.max(lb_raw, axis=0, keepdims=True))
    soft = e / jnp.sum(e, axis=0, keepdims=True)
    return jnp.sum(soft[1:layer + 1], axis=0, keepdims=True)


def _inproj_hgrn_kernel(x_ref, g_ref, w_ref, ws_ref, lb_ref, ng_ref, tri_ref, o_ref, zd_ref,
                        z_ref, st_ref, sh_ref, *, layer):
    c = pl.program_id(1)
    gw = o_ref.shape[1]
    dk = gw // HG_HEADS

    @pl.when((pl.program_id(0) == 0) & (c == 0))
    def _():
        z_ref[...] = jnp.zeros_like(z_ref)

    @pl.when(c <= 1)
    def _():
        st_ref[...] = jnp.zeros_like(st_ref)

    h = (_rms(x_ref[...]) * g_ref[...]).astype(BF16)
    z_ref[c % 2] = jnp.dot(h, w_ref[...], preferred_element_type=F32)
    zd_ref[...] = jnp.dot(h, ws_ref[...], preferred_element_type=F32)

    prev = (c + 1) % 2
    lb = _hgrn_lower_bound(lb_ref[...], layer)
    tri = tri_ref[...]
    for hd in range(HG_HEADS):
        cols = [slice(part * gw + hd * dk, part * gw + (hd + 1) * dk) for part in range(4)]
        hs = slice(hd * dk, (hd + 1) * dk)
        out = _hgrn_head(z_ref[prev, :, cols[0]], z_ref[prev, :, cols[1]], z_ref[prev, :, cols[2]],
                         z_ref[prev, :, cols[3]], lb[:, hs], ng_ref[:, hs], tri, st_ref.at[hd], sh_ref.at[hd])
        o_ref[:, hs] = out.astype(BF16)


def _inproj_hgrn(x, g, w, layer_idx, w_side, lb_raw, norm_g, batch, seq, layer):
    n, d = x.shape
    gw = norm_g.shape[0]
    dk = gw // HG_HEADS
    blk = HG_BLOCK
    nblk = seq // blk
    ns = w_side.shape[1]
    depth = lb_raw.shape[0]
    const = lambda b, c: (0, 0)
    return pl.pallas_call(
        functools.partial(_inproj_hgrn_kernel, layer=layer),
        grid=(batch, nblk + 1),
        in_specs=[
            pl.BlockSpec((blk, d), lambda b, c: (b * nblk + jnp.minimum(c, nblk - 1), 0)),
            pl.BlockSpec((1, d), const),
            pl.BlockSpec((None, d, 4 * gw), lambda b, c: (layer_idx, 0, 0), pipeline_mode=pl.Buffered(1)),
            pl.BlockSpec((d, ns), const, pipeline_mode=pl.Buffered(1)),
            pl.BlockSpec((depth, gw), const),
            pl.BlockSpec((1, gw), const),
            pl.BlockSpec((blk, blk), const),
        ],
        out_specs=[
            pl.BlockSpec((blk, gw), lambda b, c: (b * nblk + jnp.maximum(c - 1, 0), 0)),
            pl.BlockSpec((blk, ns), lambda b, c: (b * nblk + jnp.minimum(c, nblk - 1), 0)),
        ],
        out_shape=[jax.ShapeDtypeStruct((n, gw), BF16), jax.ShapeDtypeStruct((n, ns), F32)],
        scratch_shapes=[
            pltpu.VMEM((2, blk, 4 * gw), F32),
            pltpu.VMEM((HG_HEADS, dk, dk), F32),
            pltpu.VMEM((HG_HEADS, 3, blk + HG_FINE, dk), F32),
        ],
        compiler_params=_params(("arbitrary", "arbitrary"), SIDE_VMEM_LIMIT),
        name="inproj_hgrn",
    )(x, g.reshape(1, d), w, w_side, lb_raw, norm_g.reshape(1, gw), jnp.tril(jnp.ones((blk, blk), BF16)))


def _dsa_prep_kernel(zd_ref, cqg_ref, ckvg_ref, wuq_ref, qng_ref, wqit_ref,
                     q_ref, qit_ref, kv_ref, kix_ref, wht_ref):
    zd = zd_ref[...]
    cq = (_rms(zd[:, :DSA_Q_RANK]) * cqg_ref[...]).astype(BF16)
    qf = jnp.dot(cq, wuq_ref[...], preferred_element_type=F32)
    for i in range(q_ref.shape[0]):
        rs = slice(i * Q_BLOCK, (i + 1) * Q_BLOCK)
        for h in range(DSA_HEADS):
            cs = slice(h * DSA_KV_RANK, (h + 1) * DSA_KV_RANK)
            q_ref[i, h] = (_rms(qf[rs, cs]) * qng_ref[...] * (DSA_KV_RANK ** -0.5 * LOG2E)).astype(BF16)
    qit = lax.dot_general(wqit_ref[...], cq, _NT, preferred_element_type=F32)
    qit = (qit * (IDX_DIM ** -0.5)).astype(BF16)
    for i in range(q_ref.shape[0]):
        for h in range(IDX_HEADS):
            c = (i * IDX_HEADS + h) * Q_BLOCK
            qit_ref[:, c:c + Q_BLOCK] = qit[h * LANES:(h + 1) * LANES, i * Q_BLOCK:(i + 1) * Q_BLOCK]
    c0 = DSA_Q_RANK
    c1 = c0 + DSA_KV_RANK
    kv_ref[...] = (_rms(zd[:, c0:c1]) * ckvg_ref[...]).astype(BF16)
    kix_ref[...] = zd[:, c1:c1 + LANES].astype(BF16)
    wht = jnp.transpose(zd[:, c1 + LANES:c1 + 2 * LANES] * (IDX_HEADS ** -0.5))
    wht_ref[...] = wht[:IDX_HEADS, :]


def _dsa_prep(zd, cq_g, ckv_g, w_uq, qn_g, w_qit, *, tm):
    n, wd = zd.shape
    dq = w_uq.shape[1]
    dqi = w_qit.shape[0]
    full = lambda i: (0, 0)
    rows = lambda i: (i, 0)
    cols = lambda i: (0, i)
    return pl.pallas_call(
        _dsa_prep_kernel,
        grid=(n // tm,),
        in_specs=[
            pl.BlockSpec((tm, wd), rows),
            pl.BlockSpec((1, DSA_Q_RANK), full),
            pl.BlockSpec((1, DSA_KV_RANK), full),
            pl.BlockSpec((DSA_Q_RANK, dq), full),
            pl.BlockSpec((1, DSA_KV_RANK), full),
            pl.BlockSpec((dqi, DSA_Q_RANK), full),
        ],
        out_specs=[
            pl.BlockSpec((tm // Q_BLOCK, DSA_HEADS, Q_BLOCK, DSA_KV_RANK), lambda i: (i, 0, 0, 0)),
            pl.BlockSpec((LANES, IDX_HEADS * tm), cols),
            pl.BlockSpec((tm, DSA_KV_RANK), rows),
            pl.BlockSpec((tm, LANES), rows),
            pl.BlockSpec((IDX_HEADS, tm), cols),
        ],
        out_shape=[
            jax.ShapeDtypeStruct((n // Q_BLOCK, DSA_HEADS, Q_BLOCK, DSA_KV_RANK), BF16),
            jax.ShapeDtypeStruct((LANES, IDX_HEADS * n), BF16),
            jax.ShapeDtypeStruct((n, DSA_KV_RANK), BF16),
            jax.ShapeDtypeStruct((n, LANES), BF16),
            jax.ShapeDtypeStruct((IDX_HEADS, n), F32),
        ],
        compiler_params=_params(("arbitrary",)),
        name="dsa_prep",
    )(zd, cq_g.reshape(1, -1), ckv_g.reshape(1, -1), w_uq, qn_g.reshape(1, -1), w_qit)


def _bit_planes(v):
    v = list(v)

    def swap(lo, hi, j, m):
        return (lo & ~m) | ((hi >> j) & m), (hi & m) | ((lo << j) & ~m)

    for j, m in ((16, 0x0000FFFF), (8, 0x00FF00FF), (4, 0x0F0F0F0F), (2, 0x33333333), (1, 0x55555555)):
        for k in range(len(v)):
            if k & j == 0:
                v[k], v[k + j] = swap(v[k], v[k + j], j, m)
    return v


def _dsa_select_kernel(qit_ref, wht_ref, kix_ref, wg_in, wu_in, wd_in, m_ref, wg_out, wu_out, wd_out,
                       key_ref, jc_ref, plane_ref, *, ksel, idx_bits):
    for src, dst in ((wg_in, wg_out), (wu_in, wu_out), (wd_in, wd_out)):
        dst[...] = src[...].astype(BF16)

    qb = pl.program_id(1)
    ntile = qb + 1
    ntile_all = m_ref.shape[2]
    rowi = lax.broadcasted_iota(I32, (LANES, Q_BLOCK), 0)
    coli = lax.broadcasted_iota(I32, (LANES, Q_BLOCK), 1)
    q_chunk = (qb * Q_BLOCK + coli) // CHUNK
    pairs = IDX_HEADS // 2
    w_pair = [jnp.concatenate([wht_ref[2 * p:2 * p + 1, :], wht_ref[2 * p + 1:2 * p + 2, :]], axis=1)
              for p in range(pairs)]

    def tile_rows(j):
        return pl.ds(pl.multiple_of(j * LANES, LANES), LANES)

    def admissible(j):
        return ((j * LANES + rowi) // CHUNK) <= q_chunk

    def score_tile(jj, carry):
        words = []
        for t in range(2):
            j = 2 * jj + t
            kt = kix_ref[tile_rows(j), :]
            sc = None
            for p in range(pairs):
                s2 = jnp.dot(kt, qit_ref[:, 2 * p * Q_BLOCK:2 * (p + 1) * Q_BLOCK],
                             preferred_element_type=F32)
                c2 = w_pair[p] * jnp.maximum(s2, 0.0)
                c = c2[:, :Q_BLOCK] + c2[:, Q_BLOCK:]
                sc = c if sc is None else sc + c
            bits = pltpu.bitcast(sc, I32)
            key = bits ^ ((bits >> 31) & 0x7FFFFFFF)
            key = jnp.where(admissible(j), key, INT_MIN)
            key_ref[tile_rows(j), :] = key
            words += [key[8 * k:8 * (k + 1), :] for k in range(LANES // 8)]
        for i, p in enumerate(_bit_planes(words)):
            plane_ref[pl.ds(pl.multiple_of(jj * PLANE_ROWS + 8 * i, 8), 8), :] = p
        return carry

    @pl.when(qb == 0)
    def _():
        key_ref[...] = jnp.full(key_ref.shape, INT_MIN, I32)
        plane_row = lax.broadcasted_iota(I32, plane_ref.shape, 0) & (PLANE_ROWS - 1)
        plane_ref[...] = jnp.where(plane_row < 8, -1, 0)

    lax.fori_loop(0, (ntile + 1) // 2, score_tile, 0)

    def count(pred_fn):
        def body(jj, acc):
            for t in range(SCAN_TILES):
                j = SCAN_TILES * jj + t
                acc = acc + pred_fn(j, key_ref[tile_rows(j), :]).astype(I32)
            return acc
        trips = (ntile + SCAN_TILES - 1) // SCAN_TILES
        acc = lax.fori_loop(0, trips, body, jnp.zeros((LANES, Q_BLOCK), I32))
        return jnp.sum(acc, axis=0, keepdims=True)

    nword = plane_ref.shape[0] // PLANE_ROWS

    def plane(v, i):
        return plane_ref[pl.ds(pl.multiple_of(v * PLANE_ROWS + 8 * i, 8), 8), :]

    def total(words):
        acc = lax.population_count(words[0])
        for w in words[1:]:
            acc = acc + lax.population_count(w)
        return jnp.sum(acc, axis=0, keepdims=True)

    def radix_pair(t, c):
        alive, above, t_u = c
        i = 2 * t
        flip = jnp.where(t == 0, -1, 0)
        set1 = [a & (plane(v, i) ^ flip) for v, a in enumerate(alive)]
        clr1 = [a ^ s for a, s in zip(alive, set1)]
        p2 = [plane(v, i + 1) for v in range(nword)]
        set1_set2 = [s & p for s, p in zip(set1, p2)]
        clr1_set2 = [s & p for s, p in zip(clr1, p2)]
        n1, n11, n01 = total(set1), total(set1_set2), total(clr1_set2)
        take1 = (above + n1) >= ksel
        above = jnp.where(take1, above, above + n1)
        n2 = jnp.where(take1, n11, n01)
        take2 = (above + n2) >= ksel
        above = jnp.where(take2, above, above + n2)
        alive = tuple(
            jnp.where(take1, jnp.where(take2, ss, s ^ ss), jnp.where(take2, cs, c0 ^ cs))
            for s, c0, ss, cs in zip(set1, clr1, set1_set2, clr1_set2))
        t_u = (t_u | jnp.where(take1, jnp.left_shift(jnp.int32(1), 31 - i), 0)
               | jnp.where(take2, jnp.left_shift(jnp.int32(1), 30 - i), 0))
        return alive, above, t_u

    start = (tuple(jnp.full((8, Q_BLOCK), -1, I32) for _ in range(nword)),
             jnp.zeros((1, Q_BLOCK), I32), jnp.zeros((1, Q_BLOCK), I32))
    alive, above, t_u = lax.fori_loop(0, 16, radix_pair, start)
    thr = t_u ^ INT_MIN
    ties = lax.population_count(alive[0])
    for a in alive[1:]:
        ties = ties + lax.population_count(a)
    cnt_t = above + jnp.sum(ties, axis=0, keepdims=True)
    tied = jnp.max(jnp.where((cnt_t > ksel) & (thr > INT_MIN), 1.0, 0.0))

    jc_ref[...] = jnp.full(jc_ref.shape, 2 ** 31 - 1, I32)

    @pl.when(tied > 0.0)
    def _():
        need = ksel - count(lambda j, k: k > thr)

        def index_bit(i, j_c):
            cand = j_c | jnp.left_shift(jnp.int32(1), idx_bits - 1 - i)
            cnt = count(lambda j, k: (k == thr) & ((j * LANES + rowi) < cand))
            return jnp.where(cnt < need, cand, j_c)

        j_c = lax.fori_loop(0, idx_bits, index_bit, jnp.zeros((1, Q_BLOCK), I32))
        jc_ref[...] = jnp.broadcast_to(j_c, jc_ref.shape)

    j_c = jc_ref[0:1, :]
    eye = (rowi == coli).astype(BF16)

    group = KV_TILE // LANES

    def write_group(g, carry):
        for t in range(group):
            j = g * group + t
            k = key_ref[tile_rows(j), :]
            sel = (k > thr) | ((k == thr) & ((j * LANES + rowi) <= j_c))
            sel = jnp.where(sel & admissible(j), 1.0, 0.0).astype(BF16)
            sel_t = lax.dot_general(eye, sel, _NT, preferred_element_type=F32)
            m_ref[0, 0, j] = ((sel_t - 1.0) * -NEG_BIG).astype(BF16)
        return carry

    ngroup = (ntile + group - 1) // group
    lax.fori_loop(0, ngroup, write_group, 0)

    def blank_tile(j, carry):
        m_ref[0, 0, j] = jnp.full((Q_BLOCK, LANES), NEG_BIG, BF16)
        return carry

    lax.fori_loop(ngroup * group, ntile_all, blank_tile, 0)


def _dsa_select(qit, wht, kix, ffn_weights, layer, batch, seq, ksel):
    nqb = seq // Q_BLOCK
    nkt = seq // LANES
    w_in, w_out, w_shape = _cast_specs([(w, layer) for w in ffn_weights], lambda b, q: b * nqb + q,
                                       batch * nqb)
    return pl.pallas_call(
        functools.partial(_dsa_select_kernel, ksel=ksel, idx_bits=int(math.log2(seq))),
        grid=(batch, nqb),
        in_specs=[
            pl.BlockSpec((LANES, IDX_HEADS * Q_BLOCK), lambda b, q: (0, b * nqb + q)),
            pl.BlockSpec((IDX_HEADS, Q_BLOCK), lambda b, q: (0, b * nqb + q)),
            pl.BlockSpec((seq, LANES), lambda b, q: (b, 0)),
            *w_in,
        ],
        out_specs=[pl.BlockSpec((1, 1, nkt, Q_BLOCK, LANES), lambda b, q: (b, q, 0, 0, 0)), *w_out],
        out_shape=[jax.ShapeDtypeStruct((batch, nqb, nkt, Q_BLOCK, LANES), BF16), *w_shape],
        scratch_shapes=[pltpu.VMEM((seq, Q_BLOCK), I32), pltpu.VMEM((8, Q_BLOCK), I32),
                        pltpu.VMEM((seq, Q_BLOCK), I32)],
        compiler_params=_params(("arbitrary", "arbitrary")),
        name="dsa_select",
    )(qit, wht, kix, *ffn_weights)


def _rel_bucket(rel):
    nb = REL_BUCKETS // 2
    max_exact = nb // 2
    ret = jnp.where(rel > 0, nb, 0)
    n = jnp.abs(rel)
    nf = jnp.maximum(n, 1).astype(F32)
    large = max_exact + (jnp.log(nf / max_exact) / math.log(REL_MAX_DIST / max_exact)
                         * (nb - max_exact)).astype(I32)
    large = jnp.minimum(large, nb - 1)
    return ret + jnp.where(n < max_exact, n, large)


NEAR_TILES = 3


def _dsa_attn_kernel(q_ref, kv_ref, mask_ref, rb_ref, wuv_ref, o_ref,
                     m_ref, l_ref, alpha_ref, acc_ref, corr_ref, s_ref, p_ref, madd_ref):
    b, qb = pl.program_id(0), pl.program_id(1)
    sub = KV_TILE // LANES
    far_bucket = REL_BUCKETS // 2 - 1
    half = DSA_HEADS * Q_BLOCK // 2

    def head_rows(h):
        return slice(h * Q_BLOCK, (h + 1) * Q_BLOCK)

    @pl.when((b == 0) & (qb == 0))
    def _():
        ti = lax.broadcasted_iota(I32, (Q_BLOCK, LANES), 0)
        si = lax.broadcasted_iota(I32, (Q_BLOCK, LANES), 1)
        for oi in range(NEAR_TILES):
            bucket = _rel_bucket((oi - (NEAR_TILES - 1)) * LANES + si - ti)
            for h in range(DSA_HEADS):
                tbl = jnp.zeros((Q_BLOCK, LANES), F32)
                for bk in range(REL_BUCKETS):
                    tbl = jnp.where(bucket == bk, rb_ref[bk, h], tbl)
                corr_ref[oi, head_rows(h), :] = (tbl - rb_ref[far_bucket, h]) * LOG2E

    m_ref[...] = jnp.full_like(m_ref, NEG_BIG)
    l_ref[...] = jnp.zeros_like(l_ref)
    acc_ref[...] = jnp.zeros_like(acc_ref)
    q_all = q_ref[0].reshape(DSA_HEADS * Q_BLOCK, DSA_KV_RANK)

    def key_step(kt, carry):
        kvt = kv_ref[pl.ds(pl.multiple_of(kt * KV_TILE, KV_TILE), KV_TILE), :]
        for part in range(2):
            rs = slice(part * half, (part + 1) * half)
            s_ref[rs, :] = lax.dot_general(q_all[rs], kvt, _NT, preferred_element_type=F32)
        for j in range(sub):
            d = kt * sub + j - qb

            @pl.when((d > -NEAR_TILES) & (d <= 0))
            def _(j=j, d=d):
                s_ref[:, j * LANES:(j + 1) * LANES] += corr_ref[d + NEAR_TILES - 1]

        for j in range(sub):
            madd_ref[:, j * LANES:(j + 1) * LANES] = mask_ref[0, 0, kt * sub + j].astype(F32)
        groups_per_head = Q_BLOCK // SM_ROWS
        for g in range(DSA_HEADS * groups_per_head):
            rs = slice(g * SM_ROWS, (g + 1) * SM_ROWS)
            qg = g % groups_per_head
            s = s_ref[rs, :] + madd_ref[qg * SM_ROWS:(qg + 1) * SM_ROWS, :]
            m_old = m_ref[rs, :]
            m_new = jnp.maximum(m_old, jnp.max(s, axis=1, keepdims=True))
            alpha = jnp.exp2(m_old - m_new)
            p = jnp.exp2(s - jnp.tile(m_new, (1, sub)))
            l_ref[rs, :] = alpha * l_ref[rs, :] + jnp.sum(p, axis=1, keepdims=True)
            alpha_ref[rs, :] = alpha
            p_ref[rs, :] = p.astype(BF16)
            m_ref[rs, :] = m_new
        for part in range(2):
            rs = slice(part * half, (part + 1) * half)
            pv = jnp.dot(p_ref[rs, :], kvt, preferred_element_type=F32)
            acc_ref[rs, :] = jnp.tile(alpha_ref[rs, :], (1, DSA_KV_RANK // LANES)) * acc_ref[rs, :] + pv
        return carry

    lax.fori_loop(0, qb // sub + 1, key_step, 0)

    dv = wuv_ref.shape[2]
    for h in range(DSA_HEADS):
        rs = head_rows(h)
        o = (acc_ref[rs, :] / jnp.tile(l_ref[rs, :], (1, DSA_KV_RANK // LANES))).astype(BF16)
        o_ref[:, h * dv:(h + 1) * dv] = jnp.dot(o, wuv_ref[h], preferred_element_type=F32).astype(BF16)


def _dsa_attn(q, kv, mask, rel_bias, w_uv, batch, seq):
    n = kv.shape[0]
    nqb = seq // Q_BLOCK
    dv = w_uv.shape[2]
    rows = DSA_HEADS * Q_BLOCK
    return pl.pallas_call(
        _dsa_attn_kernel,
        grid=(batch, nqb),
        in_specs=[
            pl.BlockSpec((1, DSA_HEADS, Q_BLOCK, DSA_KV_RANK), lambda b, qb: (b * nqb + qb, 0, 0, 0)),
            pl.BlockSpec((seq, DSA_KV_RANK), lambda b, qb: (b, 0)),
            pl.BlockSpec((1, 1) + mask.shape[2:], lambda b, qb: (b, qb, 0, 0, 0)),
            pl.BlockSpec(memory_space=pltpu.SMEM),
            pl.BlockSpec(w_uv.shape, lambda b, qb: (0, 0, 0)),
        ],
        out_specs=pl.BlockSpec((Q_BLOCK, DSA_HEADS * dv), lambda b, qb: (b * nqb + qb, 0)),
        out_shape=jax.ShapeDtypeStruct((n, DSA_HEADS * dv), BF16),
        scratch_shapes=[
            pltpu.VMEM((rows, LANES), F32),
            pltpu.VMEM((rows, LANES), F32),
            pltpu.VMEM((rows, LANES), F32),
            pltpu.VMEM((rows, DSA_KV_RANK), F32),
            pltpu.VMEM((NEAR_TILES, rows, LANES), F32),
            pltpu.VMEM((rows, KV_TILE), F32),
            pltpu.VMEM((rows, KV_TILE), BF16),
            pltpu.VMEM((Q_BLOCK, KV_TILE), F32),
        ],
        compiler_params=_params(("arbitrary", "arbitrary")),
        name="dsa_attn",
    )(q, kv, mask, rel_bias, w_uv)


def _pad_cols(w, width):
    return jnp.pad(w, ((0, 0), (0, width - w.shape[1])))


def kernel(x, ln_mix_g, ln_ffn_g, w_ffn_gate, w_ffn_up, w_ffn_down, rel_bias, ev_w_in, ev_w_out, sgu_ln_g, sgu_ln_b, sgu_w_s, sgu_b_s, od_w_in, od_w_out, hgrn_lb, hgrn_norm_g, dsa_cq_g, dsa_ckv_g, dsa_w_uq, dsa_qnorm_g, dsa_w_qidx, dsa_w_uv):
    batch, seq, d = x.shape
    n = batch * seq
    depth = ln_mix_g.shape[0]
    ksel = min(TOPK_MAX, seq // 4)
    tm = min(PROJ_ROWS, n)
    xf = x.reshape(n, d)
    ffn_f32 = (w_ffn_gate, w_ffn_up, w_ffn_down)
    ev_in_all, od_in_all = ev_w_in.astype(BF16), od_w_in.astype(BF16)
    for layer in range(depth):
        j = layer // 2
        if layer % 2 == 0:
            z, *ffn_w = _norm_matmul(xf, ln_mix_g[layer], ev_in_all, j, ev_in_all.shape[2], tm=tm,
                                     tn=PROJ_COLS, out_dtype=BF16, cast=[(w, layer) for w in ffn_f32])
            a1 = _retention(z, batch, seq)
            a2 = _sgu(z, sgu_ln_g[j], sgu_ln_b[j], sgu_w_s[j], sgu_b_s[j], rows=256)
            w_out = ev_w_out[j]
        else:
            w_in = od_w_in[j]
            gw = d // 2
            c = 4 * gw
            c_kidx = c + DSA_Q_RANK + DSA_KV_RANK
            w_dsa = jnp.concatenate([
                w_in[:, c:c_kidx],
                _pad_cols(w_in[:, c_kidx:c_kidx + IDX_DIM], LANES),
                _pad_cols(w_in[:, c_kidx + IDX_DIM:], LANES),
            ], axis=1).astype(BF16)
            a1, zd = _inproj_hgrn(xf, ln_mix_g[layer], od_in_all, j, w_dsa, hgrn_lb, hgrn_norm_g[j],
                                  batch, seq, layer)
            w_qit = jnp.pad(dsa_w_qidx[j].T.reshape(IDX_HEADS, IDX_DIM, DSA_Q_RANK),
                            ((0, 0), (0, LANES - IDX_DIM), (0, 0))).reshape(IDX_HEADS * LANES, DSA_Q_RANK)
            q, qit, kv, kix, wht = _dsa_prep(zd, dsa_cq_g[j], dsa_ckv_g[j], dsa_w_uq[j].astype(BF16),
                                             dsa_qnorm_g[j], w_qit.astype(BF16), tm=256)
            mask, *ffn_w = _dsa_select(qit, wht, kix, ffn_f32, layer, batch, seq, ksel)
            a2 = _dsa_attn(q, kv, mask, rel_bias, dsa_w_uv[j].astype(BF16), batch, seq)
            w_out = od_w_out[j]
        xf = _outproj(a1, a2, w_out.astype(BF16), xf, tm=min(OUT_ROWS, n), tn=OUT_COLS)
        xf = _ffn(xf, ln_ffn_g[layer], *(w[None] for w in ffn_w), 0, tm=min(FFN_ROWS, n), tf=FFN_COLS)
    return xf.reshape(batch, seq, d)
```

```python
import functools
import math

import jax
import jax.numpy as jnp
from jax import lax
from jax.experimental import pallas as pl
from jax.experimental.pallas import tpu as pltpu

F32 = jnp.float32
BF16 = jnp.bfloat16
I32 = jnp.int32

EPS = 1e-6
CHUNK = 64
LANES = 128
ROPE_BASE = 10000.0
RET_HEADS = 4
SGU_WINDOW = 128
SGU_GROUPS = 4
HG_HEADS = 8
DSA_HEADS = 8
DSA_Q_RANK = 384
DSA_KV_RANK = 256
IDX_HEADS = 16
IDX_DIM = 64
TOPK_MAX = 256
Q_BLOCK = 128
KV_TILE = 512
SM_ROWS = 64
LOG2E = math.log2(math.e)
REL_BUCKETS = 32
REL_MAX_DIST = 256
NEG_BIG = -1e30
INT_MIN = -(2 ** 31)
SCAN_TILES = 2
PLANE_ROWS = 256

RET_BLOCK = 256
RET_HEADS_PER_STEP = 4
HG_BLOCK = 256
HG_FINE = 4
VMEM_LIMIT = 48 * 1024 * 1024
PROJ_ROWS = 1024
PROJ_COLS = 1024
OUT_ROWS = 512
OUT_COLS = 2048
FFN_ROWS = 1024
FFN_VMEM_LIMIT = 58 * 1024 * 1024
SIDE_VMEM_LIMIT = 56 * 1024 * 1024
FFN_COLS = 512

_NT = (((1,), (1,)), ((), ()))
_TN = (((0,), (0,)), ((), ()))


def _params(semantics, vmem_limit=VMEM_LIMIT):
    return pltpu.CompilerParams(dimension_semantics=semantics, vmem_limit_bytes=vmem_limit)


def _silu(x):
    return x * jax.nn.sigmoid(x)


def _rms(x):
    return x * lax.rsqrt(jnp.mean(x * x, axis=-1, keepdims=True) + EPS)


def _cast_blocks(rows, steps):
    units = rows // 16
    blocks = max(k for k in range(1, min(units, steps) + 1) if units % k == 0)
    return rows // blocks, blocks


def _cast_specs(weights, step_of, steps):
    ins, outs, shapes = [], [], []
    for w, layer in weights:
        rows, blocks = _cast_blocks(w.shape[1], steps)

        def index(*ids, blocks=blocks):
            return (jnp.minimum(step_of(*ids), blocks - 1), 0)

        ins.append(pl.BlockSpec((None, rows, w.shape[2]), lambda *ids, layer=layer, index=index: (layer,) + index(*ids)))
        outs.append(pl.BlockSpec((rows, w.shape[2]), index))
        shapes.append(jax.ShapeDtypeStruct(w.shape[1:], BF16))
    return ins, outs, shapes


def _norm_matmul_kernel(x_ref, g_ref, w_ref, *rest):
    ncast = (len(rest) - 2) // 2
    o_ref, h_ref = rest[ncast], rest[-1]
    for src, dst in zip(rest[:ncast], rest[ncast + 1:-1]):
        dst[...] = src[...].astype(BF16)

    @pl.when(pl.program_id(1) == 0)
    def _():
        h_ref[...] = (_rms(x_ref[...]) * g_ref[...]).astype(BF16)

    o_ref[...] = jnp.dot(h_ref[...], w_ref[...], preferred_element_type=F32).astype(o_ref.dtype)


def _norm_matmul(x, g, w, layer, nout, *, tm, tn, out_dtype=F32, cast=()):
    n, d = x.shape
    ncols = nout // tn
    c_in, c_out, c_shape = _cast_specs(cast, lambda i, j: i * ncols + j, (n // tm) * ncols)
    return pl.pallas_call(
        _norm_matmul_kernel,
        grid=(n // tm, ncols),
        in_specs=[
            pl.BlockSpec((tm, d), lambda i, j: (i, 0)),
            pl.BlockSpec((1, d), lambda i, j: (0, 0)),
            pl.BlockSpec((None, d, tn), lambda i, j: (layer, 0, j)),
            *c_in,
        ],
        out_specs=[pl.BlockSpec((tm, tn), lambda i, j: (i, j)), *c_out],
        out_shape=[jax.ShapeDtypeStruct((n, nout), out_dtype), *c_shape],
        scratch_shapes=[pltpu.VMEM((tm, d), BF16)],
        compiler_params=_params(("arbitrary", "arbitrary"), SIDE_VMEM_LIMIT),
        name="norm_matmul",
    )(x, g.reshape(1, d), w, *(w_c for w_c, _ in cast))


def _outproj_kernel(a1_ref, a2_ref, w1_ref, w2_ref, r_ref, o_ref):
    acc = jnp.dot(a1_ref[...], w1_ref[...], preferred_element_type=F32)
    acc += jnp.dot(a2_ref[...], w2_ref[...], preferred_element_type=F32)
    o_ref[...] = r_ref[...] + acc


def _outproj(a1, a2, w, res, *, tm, tn):
    n, half = a1.shape
    d = w.shape[1]
    return pl.pallas_call(
        _outproj_kernel,
        grid=(n // tm, d // tn),
        in_specs=[
            pl.BlockSpec((tm, half), lambda i, j: (i, 0)),
            pl.BlockSpec((tm, half), lambda i, j: (i, 0)),
            pl.BlockSpec((half, tn), lambda i, j: (0, j)),
            pl.BlockSpec((half, tn), lambda i, j: (1, j)),
            pl.BlockSpec((tm, tn), lambda i, j: (i, j)),
        ],
        out_specs=pl.BlockSpec((tm, tn), lambda i, j: (i, j)),
        out_shape=jax.ShapeDtypeStruct((n, d), F32),
        compiler_params=_params(("arbitrary", "arbitrary")),
        name="outproj",
    )(a1, a2, w, w, res)


def _ffn_kernel(x_ref, g_ref, wg_ref, wu_ref, wd_ref, o_ref, h_ref):
    @pl.when(pl.program_id(1) == 0)
    def _():
        x = x_ref[...]
        h_ref[...] = (_rms(x) * g_ref[...]).astype(BF16)
        o_ref[...] = x

    h = h_ref[...]
    a = jnp.dot(h, wg_ref[...], preferred_element_type=F32)
    u = jnp.dot(h, wu_ref[...], preferred_element_type=F32)
    act = (_silu(a) * u).astype(BF16)
    o_ref[...] += jnp.dot(act, wd_ref[...], preferred_element_type=F32)


def _ffn(x, g, wg, wu, wd, layer, *, tm, tf):
    n, d = x.shape
    dff = wg.shape[2]
    return pl.pallas_call(
        _ffn_kernel,
        grid=(n // tm, dff // tf),
        in_specs=[
            pl.BlockSpec((tm, d), lambda i, f: (i, 0)),
            pl.BlockSpec((1, d), lambda i, f: (0, 0)),
            pl.BlockSpec((None, d, tf), lambda i, f: (layer, 0, f)),
            pl.BlockSpec((None, d, tf), lambda i, f: (layer, 0, f)),
            pl.BlockSpec((None, tf, d), lambda i, f: (layer, f, 0)),
        ],
        out_specs=pl.BlockSpec((tm, d), lambda i, f: (i, 0)),
        out_shape=jax.ShapeDtypeStruct((n, d), F32),
        scratch_shapes=[pltpu.VMEM((tm, d), BF16)],
        compiler_params=_params(("arbitrary", "arbitrary"), FFN_VMEM_LIMIT),
        name="ffn",
    )(x, g.reshape(1, d), wg, wu, wd)


def _retention_kernel(q_ref, k_ref, v_ref, g_ref, cos_ref, sin_ref, d_ref, xi_ref, zeta_ref,
                      gl_ref, o_ref, state_ref):
    @pl.when(pl.program_id(2) == 0)
    def _():
        state_ref[...] = jnp.zeros_like(state_ref)

    cos = cos_ref[...]
    sin = sin_ref[...]
    half = cos.shape[1]
    dk = 2 * half

    def rot(t):
        t1, t2 = t[:, :half], t[:, half:]
        return jnp.concatenate([t1 * cos - t2 * sin, t1 * sin + t2 * cos], axis=1)

    for i in range(RET_HEADS_PER_STEP):
        cs = slice(i * dk, (i + 1) * dk)
        q = rot(q_ref[:, cs].astype(F32))
        k = rot(k_ref[:, cs].astype(F32)) * (dk ** -0.5)
        qb = q.astype(BF16)
        vb = v_ref[:, cs].astype(BF16)
        scores = lax.dot_general(qb, k.astype(BF16), _NT, preferred_element_type=F32) * d_ref[i]
        intra = jnp.dot(scores.astype(BF16), vb, preferred_element_type=F32)
        state = state_ref[i]
        cross = jnp.dot(qb, state.astype(BF16), preferred_element_type=F32) * xi_ref[i]
        kz = (k * zeta_ref[i]).astype(BF16)
        state_ref[i] = state * gl_ref[i] + lax.dot_general(kz, vb, _TN, preferred_element_type=F32)
        o_ref[:, cs] = (_rms(intra + cross) * _silu(g_ref[:, cs].astype(F32))).astype(BF16)


def _retention_tables(seq, dk):
    blk = RET_BLOCK
    pos = jnp.arange(seq, dtype=F32)
    inv = ROPE_BASE ** (-jnp.arange(0, dk, 2, dtype=F32) / dk)
    ang = pos[:, None] * inv[None, :]
    log_gamma = jnp.log(1.0 - 2.0 ** (-5.0 - jnp.arange(RET_HEADS, dtype=F32)))
    i = jnp.arange(blk)
    same = (i[:, None] // CHUNK) == (i[None, :] // CHUNK)
    earlier = (i[None, :] // CHUNK) < (i[:, None] // CHUNK)
    diff = (i[:, None] - i[None, :]).astype(F32)
    dist = jnp.where(same, jnp.abs(diff), diff)
    decay = jnp.where((same | earlier)[None], jnp.exp(log_gamma[:, None, None] * dist[None]), 0.0)
    p = jnp.arange(blk, dtype=F32)
    wide = (RET_HEADS, blk, dk)
    xi = jnp.broadcast_to(jnp.exp(log_gamma[:, None] * (p + 1.0))[:, :, None], wide)
    zeta = jnp.broadcast_to(jnp.exp(log_gamma[:, None] * (blk - 1.0 - p))[:, :, None], wide)
    g_blk = jnp.broadcast_to(jnp.exp(log_gamma * blk)[:, None, None], (RET_HEADS, 1, dk))
    return jnp.cos(ang), jnp.sin(ang), decay, xi, zeta, g_blk


def _retention(z, batch, seq):
    n = z.shape[0]
    gw = z.shape[1] // 6
    dk = gw // RET_HEADS
    blk = RET_BLOCK
    nblk = seq // blk
    cos, sin, decay, xi, zeta, g_blk = _retention_tables(seq, dk)

    hp = RET_HEADS_PER_STEP
    groups = RET_HEADS // hp

    def zspec(part):
        return pl.BlockSpec((blk, hp * dk), lambda b, h, c: (b * nblk + c, part * groups + h))

    def hspec(rows, cols):
        return pl.BlockSpec((hp, rows, cols), lambda b, h, c: (h, 0, 0))

    return pl.pallas_call(
        _retention_kernel,
        grid=(batch, groups, nblk),
        in_specs=[
            zspec(0), zspec(1), zspec(2), zspec(3),
            pl.BlockSpec((blk, dk // 2), lambda b, h, c: (c, 0)),
            pl.BlockSpec((blk, dk // 2), lambda b, h, c: (c, 0)),
            hspec(blk, blk), hspec(blk, dk), hspec(blk, dk), hspec(1, dk),
        ],
        out_specs=pl.BlockSpec((blk, hp * dk), lambda b, h, c: (b * nblk + c, h)),
        out_shape=jax.ShapeDtypeStruct((n, gw), BF16),
        scratch_shapes=[pltpu.VMEM((hp, dk, dk), F32)],
        compiler_params=_params(("arbitrary", "arbitrary", "arbitrary")),
        name="retention",
    )(z, z, z, z, cos, sin, decay, xi, zeta, g_blk)


def _gelu(x):
    return 0.5 * x * (1.0 + lax.erf(x * math.sqrt(0.5)))


def _sgu_kernel(u_ref, v_ref, lng_ref, lnb_ref, w_ref, b_ref, o_ref):
    rows, width = v_ref.shape
    dg = width // SGU_GROUPS
    v = _gelu(v_ref[...].astype(F32))
    mu = jnp.mean(v, axis=-1, keepdims=True)
    var = jnp.mean(jnp.square(v - mu), axis=-1, keepdims=True)
    vn = ((v - mu) * lax.rsqrt(var + EPS) * lng_ref[...] + lnb_ref[...]).astype(BF16)
    u = _gelu(u_ref[...].astype(F32))
    ri = lax.broadcasted_iota(I32, (SGU_WINDOW, SGU_WINDOW), 0) // CHUNK
    ci = lax.broadcasted_iota(I32, (SGU_WINDOW, SGU_WINDOW), 1) // CHUNK
    allowed = ci <= ri
    for g in range(SGU_GROUPS):
        wg = jnp.where(allowed, w_ref[g], 0.0).astype(BF16)
        bias = b_ref[g]
        for w in range(rows // SGU_WINDOW):
            rs = slice(w * SGU_WINDOW, (w + 1) * SGU_WINDOW)
            cs = slice(g * dg, (g + 1) * dg)
            mixed = jnp.dot(wg, vn[rs, cs], preferred_element_type=F32) + bias
            o_ref[rs, cs] = (u[rs, cs] * mixed).astype(BF16)


def _sgu(z, ln_g, ln_b, w_s, b_s, *, rows):
    n = z.shape[0]
    gw = z.shape[1] // 6
    return pl.pallas_call(
        _sgu_kernel,
        grid=(n // rows,),
        in_specs=[
            pl.BlockSpec((rows, gw), lambda i: (i, 4)),
            pl.BlockSpec((rows, gw), lambda i: (i, 5)),
            pl.BlockSpec((1, gw), lambda i: (0, 0)),
            pl.BlockSpec((1, gw), lambda i: (0, 0)),
            pl.BlockSpec((SGU_GROUPS, SGU_WINDOW, SGU_WINDOW), lambda i: (0, 0, 0)),
            pl.BlockSpec((SGU_GROUPS, SGU_WINDOW, 1), lambda i: (0, 0, 0)),
        ],
        out_specs=pl.BlockSpec((rows, gw), lambda i: (i, 0)),
        out_shape=jax.ShapeDtypeStruct((n, gw), BF16),
        compiler_params=_params(("arbitrary",)),
        name="sgu",
    )(z, z, ln_g.reshape(1, gw), ln_b.reshape(1, gw), w_s, b_s.reshape(SGU_GROUPS, SGU_WINDOW, 1))


def _hgrn_head(q, f_logits, v, g, lb, ng, tri, st_ref, sh_ref):
    rows, dk = q.shape
    f = lb + (1.0 - lb) * jax.nn.sigmoid(f_logits)
    lf = jnp.log(f)
    kk = 1.0 - f
    qa = _silu(q)

    bcum = None
    rest = lf
    for _ in range(3):
        term = rest.astype(BF16)
        part = jnp.dot(tri, term, preferred_element_type=F32)
        bcum = part if bcum is None else bcum + part
        rest = rest - term.astype(F32)

    row = lax.broadcasted_iota(I32, (rows, dk), 0)
    ti = lax.broadcasted_iota(I32, (rows, rows), 0)
    si = lax.broadcasted_iota(I32, (rows, rows), 1)
    attn = jnp.zeros((rows, rows), F32)
    hs = rows // 2
    while hs >= HG_FINE:
        bs = 2 * hs
        parts = [jnp.broadcast_to(bcum[b * bs + hs - 1:b * bs + hs, :], (bs, dk))
                 for b in range(rows // bs)]
        anchor = parts[0] if len(parts) == 1 else jnp.concatenate(parts, axis=0)
        upper = (row & (bs - 1)) >= hs
        fac = jnp.exp(-jnp.abs(bcum - anchor))
        qt = jnp.where(upper, qa * fac, 0.0)
        kt = jnp.where(upper, 0.0, kk * fac)
        a = lax.dot_general(qt.astype(BF16), kt.astype(BF16), _NT, preferred_element_type=F32)
        if bs < rows:
            a = jnp.where((ti & -bs) == (si & -bs), a, 0.0)
        attn = attn + a
        hs //= 2

    vb = v.astype(BF16)
    near = qa * kk
    intra = jnp.sum(near, axis=1, keepdims=True) * v
    pad = jnp.zeros((HG_FINE, dk), F32)
    for idx, val in enumerate((kk, bcum, v)):
        sh_ref[idx, 0:HG_FINE, :] = pad
        sh_ref[idx, HG_FINE:, :] = val
    for delta in range(1, HG_FINE):
        back = slice(HG_FINE - delta, HG_FINE - delta + rows)
        prod = qa * sh_ref[0, back, :] * jnp.exp(jnp.minimum(bcum - sh_ref[1, back, :], 0.0))
        prod = jnp.where((row & (HG_FINE - 1)) >= delta, prod, 0.0)
        intra = intra + jnp.sum(prod, axis=1, keepdims=True) * sh_ref[2, back, :]
    intra = intra + jnp.dot(attn.astype(BF16), vb, preferred_element_type=F32)
    st = st_ref[...]
    cross = lax.dot_general((qa * jnp.exp(bcum)).astype(BF16), st.astype(BF16), _NT,
                            preferred_element_type=F32)
    blast = bcum[rows - 1:rows, :]
    kb = (kk * jnp.exp(blast - bcum)).astype(BF16)
    st_ref[...] = st * jnp.exp(blast) + lax.dot_general(vb, kb, _TN, preferred_element_type=F32)
    return _rms(intra + cross) * ng * _silu(g)


def _hgrn_lower_bound(lb_raw, layer):
    e = jnp.exp(lb_raw - jnp.max(lb_raw, axis=0, keepdims=True))
    soft = e / jnp.sum(e, axis=0, keepdims=True)
    return jnp.sum(soft[1:layer + 1], axis=0, keepdims=True)


def _inproj_hgrn_kernel(x_ref, g_ref, w_ref, ws_ref, lb_ref, ng_ref, tri_ref, o_ref, zd_ref,
                        z_ref, st_ref, sh_ref, *, layer):
    c = pl.program_id(1)
    gw = o_ref.shape[1]
    dk = gw // HG_HEADS

    @pl.when((pl.program_id(0) == 0) & (c == 0))
    def _():
        z_ref[...] = jnp.zeros_like(z_ref)

    @pl.when(c <= 1)
    def _():
        st_ref[...] = jnp.zeros_like(st_ref)

    h = (_rms(x_ref[...]) * g_ref[...]).astype(BF16)
    z_ref[c % 2] = jnp.dot(h, w_ref[...], preferred_element_type=F32)
    zd_ref[...] = jnp.dot(h, ws_ref[...], preferred_element_type=F32)

    prev = (c + 1) % 2
    lb = _hgrn_lower_bound(lb_ref[...], layer)
    tri = tri_ref[...]
    for hd in range(HG_HEADS):
        cols = [slice(part * gw + hd * dk, part * gw + (hd + 1) * dk) for part in range(4)]
        hs = slice(hd * dk, (hd + 1) * dk)
        out = _hgrn_head(z_ref[prev, :, cols[0]], z_ref[prev, :, cols[1]], z_ref[prev, :, cols[2]],
                         z_ref[prev, :, cols[3]], lb[:, hs], ng_ref[:, hs], tri, st_ref.at[hd], sh_ref.at[hd])
        o_ref[:, hs] = out.astype(BF16)


def _inproj_hgrn(x, g, w, layer_idx, w_side, lb_raw, norm_g, batch, seq, layer):
    n, d = x.shape
    gw = norm_g.shape[0]
    dk = gw // HG_HEADS
    blk = HG_BLOCK
    nblk = seq // blk
    ns = w_side.shape[1]
    depth = lb_raw.shape[0]
    const = lambda b, c: (0, 0)
    return pl.pallas_call(
        functools.partial(_inproj_hgrn_kernel, layer=layer),
        grid=(batch, nblk + 1),
        in_specs=[
            pl.BlockSpec((blk, d), lambda b, c: (b * nblk + jnp.minimum(c, nblk - 1), 0)),
            pl.BlockSpec((1, d), const),
            pl.BlockSpec((None, d, 4 * gw), lambda b, c: (layer_idx, 0, 0), pipeline_mode=pl.Buffered(1)),
            pl.BlockSpec((d, ns), const, pipeline_mode=pl.Buffered(1)),
            pl.BlockSpec((depth, gw), const),
            pl.BlockSpec((1, gw), const),
            pl.BlockSpec((blk, blk), const),
        ],
        out_specs=[
            pl.BlockSpec((blk, gw), lambda b, c: (b * nblk + jnp.maximum(c - 1, 0), 0)),
            pl.BlockSpec((blk, ns), lambda b, c: (b * nblk + jnp.minimum(c, nblk - 1), 0)),
        ],
        out_shape=[jax.ShapeDtypeStruct((n, gw), BF16), jax.ShapeDtypeStruct((n, ns), F32)],
        scratch_shapes=[
            pltpu.VMEM((2, blk, 4 * gw), F32),
            pltpu.VMEM((HG_HEADS, dk, dk), F32),
            pltpu.VMEM((HG_HEADS, 3, blk + HG_FINE, dk), F32),
        ],
        compiler_params=_params(("arbitrary", "arbitrary"), SIDE_VMEM_LIMIT),
        name="inproj_hgrn",
    )(x, g.reshape(1, d), w, w_side, lb_raw, norm_g.reshape(1, gw), jnp.tril(jnp.ones((blk, blk), BF16)))


def _dsa_prep_kernel(zd_ref, cqg_ref, ckvg_ref, wuq_ref, qng_ref, wqit_ref,
                     q_ref, qit_ref, kv_ref, kix_ref, wht_ref):
    zd = zd_ref[...]
    cq = (_rms(zd[:, :DSA_Q_RANK]) * cqg_ref[...]).astype(BF16)
    qf = jnp.dot(cq, wuq_ref[...], preferred_element_type=F32)
    for i in range(q_ref.shape[0]):
        rs = slice(i * Q_BLOCK, (i + 1) * Q_BLOCK)
        for h in range(DSA_HEADS):
            cs = slice(h * DSA_KV_RANK, (h + 1) * DSA_KV_RANK)
            q_ref[i, h] = (_rms(qf[rs, cs]) * qng_ref[...] * (DSA_KV_RANK ** -0.5 * LOG2E)).astype(BF16)
    qit = lax.dot_general(wqit_ref[...], cq, _NT, preferred_element_type=F32)
    qit = (qit * (IDX_DIM ** -0.5)).astype(BF16)
    for i in range(q_ref.shape[0]):
        for h in range(IDX_HEADS):
            c = (i * IDX_HEADS + h) * Q_BLOCK
            qit_ref[:, c:c + Q_BLOCK] = qit[h * LANES:(h + 1) * LANES, i * Q_BLOCK:(i + 1) * Q_BLOCK]
    c0 = DSA_Q_RANK
    c1 = c0 + DSA_KV_RANK
    kv_ref[...] = (_rms(zd[:, c0:c1]) * ckvg_ref[...]).astype(BF16)
    kix_ref[...] = zd[:, c1:c1 + LANES].astype(BF16)
    wht = jnp.transpose(zd[:, c1 + LANES:c1 + 2 * LANES] * (IDX_HEADS ** -0.5))
    wht_ref[...] = wht[:IDX_HEADS, :]


def _dsa_prep(zd, cq_g, ckv_g, w_uq, qn_g, w_qit, *, tm):
    n, wd = zd.shape
    dq = w_uq.shape[1]
    dqi = w_qit.shape[0]
    full = lambda i: (0, 0)
    rows = lambda i: (i, 0)
    cols = lambda i: (0, i)
    return pl.pallas_call(
        _dsa_prep_kernel,
        grid=(n // tm,),
        in_specs=[
            pl.BlockSpec((tm, wd), rows),
            pl.BlockSpec((1, DSA_Q_RANK), full),
            pl.BlockSpec((1, DSA_KV_RANK), full),
            pl.BlockSpec((DSA_Q_RANK, dq), full),
            pl.BlockSpec((1, DSA_KV_RANK), full),
            pl.BlockSpec((dqi, DSA_Q_RANK), full),
        ],
        out_specs=[
            pl.BlockSpec((tm // Q_BLOCK, DSA_HEADS, Q_BLOCK, DSA_KV_RANK), lambda i: (i, 0, 0, 0)),
            pl.BlockSpec((LANES, IDX_HEADS * tm), cols),
            pl.BlockSpec((tm, DSA_KV_RANK), rows),
            pl.BlockSpec((tm, LANES), rows),
            pl.BlockSpec((IDX_HEADS, tm), cols),
        ],
        out_shape=[
            jax.ShapeDtypeStruct((n // Q_BLOCK, DSA_HEADS, Q_BLOCK, DSA_KV_RANK), BF16),
            jax.ShapeDtypeStruct((LANES, IDX_HEADS * n), BF16),
            jax.ShapeDtypeStruct((n, DSA_KV_RANK), BF16),
            jax.ShapeDtypeStruct((n, LANES), BF16),
            jax.ShapeDtypeStruct((IDX_HEADS, n), F32),
        ],
        compiler_params=_params(("arbitrary",)),
        name="dsa_prep",
    )(zd, cq_g.reshape(1, -1), ckv_g.reshape(1, -1), w_uq, qn_g.reshape(1, -1), w_qit)


def _bit_planes(v):
    v = list(v)

    def swap(lo, hi, j, m):
        return (lo & ~m) | ((hi >> j) & m), (hi & m) | ((lo << j) & ~m)

    for j, m in ((16, 0x0000FFFF), (8, 0x00FF00FF), (4, 0x0F0F0F0F), (2, 0x33333333), (1, 0x55555555)):
        for k in range(len(v)):
            if k & j == 0:
                v[k], v[k + j] = swap(v[k], v[k + j], j, m)
    return v


def _dsa_select_kernel(qit_ref, wht_ref, kix_ref, wg_in, wu_in, wd_in, m_ref, wg_out, wu_out, wd_out,
                       key_ref, jc_ref, plane_ref, *, ksel, idx_bits):
    for src, dst in ((wg_in, wg_out), (wu_in, wu_out), (wd_in, wd_out)):
        dst[...] = src[...].astype(BF16)

    qb = pl.program_id(1)
    ntile = qb + 1
    ntile_all = m_ref.shape[2]
    rowi = lax.broadcasted_iota(I32, (LANES, Q_BLOCK), 0)
    coli = lax.broadcasted_iota(I32, (LANES, Q_BLOCK), 1)
    q_chunk = (qb * Q_BLOCK + coli) // CHUNK
    pairs = IDX_HEADS // 2
    w_pair = [jnp.concatenate([wht_ref[2 * p:2 * p + 1, :], wht_ref[2 * p + 1:2 * p + 2, :]], axis=1)
              for p in range(pairs)]

    def tile_rows(j):
        return pl.ds(pl.multiple_of(j * LANES, LANES), LANES)

    def admissible(j):
        return ((j * LANES + rowi) // CHUNK) <= q_chunk

    def score_tile(jj, carry):
        words = []
        for t in range(2):
            j = 2 * jj + t
            kt = kix_ref[tile_rows(j), :]
            sc = None
            for p in range(pairs):
                s2 = jnp.dot(kt, qit_ref[:, 2 * p * Q_BLOCK:2 * (p + 1) * Q_BLOCK],
                             preferred_element_type=F32)
                c2 = w_pair[p] * jnp.maximum(s2, 0.0)
                c = c2[:, :Q_BLOCK] + c2[:, Q_BLOCK:]
                sc = c if sc is None else sc + c
            bits = pltpu.bitcast(sc, I32)
            key = bits ^ ((bits >> 31) & 0x7FFFFFFF)
            key = jnp.where(admissible(j), key, INT_MIN)
            key_ref[tile_rows(j), :] = key
            words += [key[8 * k:8 * (k + 1), :] for k in range(LANES // 8)]
        for i, p in enumerate(_bit_planes(words)):
            plane_ref[pl.ds(pl.multiple_of(jj * PLANE_ROWS + 8 * i, 8), 8), :] = p
        return carry

    @pl.when(qb == 0)
    def _():
        key_ref[...] = jnp.full(key_ref.shape, INT_MIN, I32)
        plane_row = lax.broadcasted_iota(I32, plane_ref.shape, 0) & (PLANE_ROWS - 1)
        plane_ref[...] = jnp.where(plane_row < 8, -1, 0)

    lax.fori_loop(0, (ntile + 1) // 2, score_tile, 0)

    def count(pred_fn):
        def body(jj, acc):
            for t in range(SCAN_TILES):
                j = SCAN_TILES * jj + t
                acc = acc + pred_fn(j, key_ref[tile_rows(j), :]).astype(I32)
            return acc
        trips = (ntile + SCAN_TILES - 1) // SCAN_TILES
        acc = lax.fori_loop(0, trips, body, jnp.zeros((LANES, Q_BLOCK), I32))
        return jnp.sum(acc, axis=0, keepdims=True)

    nword = plane_ref.shape[0] // PLANE_ROWS

    def plane(v, i):
        return plane_ref[pl.ds(pl.multiple_of(v * PLANE_ROWS + 8 * i, 8), 8), :]

    def total(words):
        acc = lax.population_count(words[0])
        for w in words[1:]:
            acc = acc + lax.population_count(w)
        return jnp.sum(acc, axis=0, keepdims=True)

    def radix_pair(t, c):
        alive, above, t_u = c
        i = 2 * t
        flip = jnp.where(t == 0, -1, 0)
        set1 = [a & (plane(v, i) ^ flip) for v, a in enumerate(alive)]
        clr1 = [a ^ s for a, s in zip(alive, set1)]
        p2 = [plane(v, i + 1) for v in range(nword)]
        set1_set2 = [s & p for s, p in zip(set1, p2)]
        clr1_set2 = [s & p for s, p in zip(clr1, p2)]
        n1, n11, n01 = total(set1), total(set1_set2), total(clr1_set2)
        take1 = (above + n1) >= ksel
        above = jnp.where(take1, above, above + n1)
        n2 = jnp.where(take1, n11, n01)
        take2 = (above + n2) >= ksel
        above = jnp.where(take2, above, above + n2)
        alive = tuple(
            jnp.where(take1, jnp.where(take2, ss, s ^ ss), jnp.where(take2, cs, c0 ^ cs))
            for s, c0, ss, cs in zip(set1, clr1, set1_set2, clr1_set2))
        t_u = (t_u | jnp.where(take1, jnp.left_shift(jnp.int32(1), 31 - i), 0)
               | jnp.where(take2, jnp.left_shift(jnp.int32(1), 30 - i), 0))
        return alive, above, t_u

    start = (tuple(jnp.full((8, Q_BLOCK), -1, I32) for _ in range(nword)),
             jnp.zeros((1, Q_BLOCK), I32), jnp.zeros((1, Q_BLOCK), I32))
    alive, above, t_u = lax.fori_loop(0, 16, radix_pair, start)
    thr = t_u ^ INT_MIN
    ties = lax.population_count(alive[0])
    for a in alive[1:]:
        ties = ties + lax.population_count(a)
    cnt_t = above + jnp.sum(ties, axis=0, keepdims=True)
    tied = jnp.max(jnp.where((cnt_t > ksel) & (thr > INT_MIN), 1.0, 0.0))

    jc_ref[...] = jnp.full(jc_ref.shape, 2 ** 31 - 1, I32)

    @pl.when(tied > 0.0)
    def _():
        need = ksel - count(lambda j, k: k > thr)

        def index_bit(i, j_c):
            cand = j_c | jnp.left_shift(jnp.int32(1), idx_bits - 1 - i)
            cnt = count(lambda j, k: (k == thr) & ((j * LANES + rowi) < cand))
            return jnp.where(cnt < need, cand, j_c)

        j_c = lax.fori_loop(0, idx_bits, index_bit, jnp.zeros((1, Q_BLOCK), I32))
        jc_ref[...] = jnp.broadcast_to(j_c, jc_ref.shape)

    j_c = jc_ref[0:1, :]
    eye = (rowi == coli).astype(BF16)

    group = KV_TILE // LANES

    floor = jnp.maximum(thr, INT_MIN + 1)

    def write_group(with_ties):
        def body(g, carry):
            for t in range(group):
                j = g * group + t
                k = key_ref[tile_rows(j), :]
                if with_ties:
                    sel = ((k > thr) | ((k == thr) & ((j * LANES + rowi) <= j_c))) & admissible(j)
                else:
                    sel = k >= floor
                sel = jnp.where(sel, 1.0, 0.0).astype(BF16)
                sel_t = lax.dot_general(eye, sel, _NT, preferred_element_type=F32)
                m_ref[0, 0, j] = ((sel_t - 1.0) * -NEG_BIG).astype(BF16)
            return carry
        return body

    ngroup = (ntile + group - 1) // group

    @pl.when(tied > 0.0)
    def _():
        lax.fori_loop(0, ngroup, write_group(True), 0)

    @pl.when(tied <= 0.0)
    def _():
        lax.fori_loop(0, ngroup, write_group(False), 0)

    def blank_tile(j, carry):
        m_ref[0, 0, j] = jnp.full((Q_BLOCK, LANES), NEG_BIG, BF16)
        return carry

    lax.fori_loop(ngroup * group, ntile_all, blank_tile, 0)


def _dsa_select(qit, wht, kix, ffn_weights, layer, batch, seq, ksel):
    nqb = seq // Q_BLOCK
    nkt = seq // LANES
    w_in, w_out, w_shape = _cast_specs([(w, layer) for w in ffn_weights], lambda b, q: b * nqb + q,
                                       batch * nqb)
    return pl.pallas_call(
        functools.partial(_dsa_select_kernel, ksel=ksel, idx_bits=int(math.log2(seq))),
        grid=(batch, nqb),
        in_specs=[
            pl.BlockSpec((LANES, IDX_HEADS * Q_BLOCK), lambda b, q: (0, b * nqb + q)),
            pl.BlockSpec((IDX_HEADS, Q_BLOCK), lambda b, q: (0, b * nqb + q)),
            pl.BlockSpec((seq, LANES), lambda b, q: (b, 0)),
            *w_in,
        ],
        out_specs=[pl.BlockSpec((1, 1, nkt, Q_BLOCK, LANES), lambda b, q: (b, q, 0, 0, 0)), *w_out],
        out_shape=[jax.ShapeDtypeStruct((batch, nqb, nkt, Q_BLOCK, LANES), BF16), *w_shape],
        scratch_shapes=[pltpu.VMEM((seq, Q_BLOCK), I32), pltpu.VMEM((8, Q_BLOCK), I32),
                        pltpu.VMEM((seq, Q_BLOCK), I32)],
        compiler_params=_params(("arbitrary", "arbitrary")),
        name="dsa_select",
    )(qit, wht, kix, *ffn_weights)


def _rel_bucket(rel):
    nb = REL_BUCKETS // 2
    max_exact = nb // 2
    ret = jnp.where(rel > 0, nb, 0)
    n = jnp.abs(rel)
    nf = jnp.maximum(n, 1).astype(F32)
    large = max_exact + (jnp.log(nf / max_exact) / math.log(REL_MAX_DIST / max_exact)
                         * (nb - max_exact)).astype(I32)
    large = jnp.minimum(large, nb - 1)
    return ret + jnp.where(n < max_exact, n, large)


NEAR_TILES = 3


def _dsa_attn_kernel(q_ref, kv_ref, mask_ref, rb_ref, wuv_ref, o_ref,
                     m_ref, l_ref, alpha_ref, acc_ref, corr_ref, s_ref, p_ref, madd_ref):
    b, qb = pl.program_id(0), pl.program_id(1)
    sub = KV_TILE // LANES
    far_bucket = REL_BUCKETS // 2 - 1
    half = DSA_HEADS * Q_BLOCK // 2

    def head_rows(h):
        return slice(h * Q_BLOCK, (h + 1) * Q_BLOCK)

    @pl.when((b == 0) & (qb == 0))
    def _():
        ti = lax.broadcasted_iota(I32, (Q_BLOCK, LANES), 0)
        si = lax.broadcasted_iota(I32, (Q_BLOCK, LANES), 1)
        for oi in range(NEAR_TILES):
            bucket = _rel_bucket((oi - (NEAR_TILES - 1)) * LANES + si - ti)
            for h in range(DSA_HEADS):
                tbl = jnp.zeros((Q_BLOCK, LANES), F32)
                for bk in range(REL_BUCKETS):
                    tbl = jnp.where(bucket == bk, rb_ref[bk, h], tbl)
                corr_ref[oi, head_rows(h), :] = (tbl - rb_ref[far_bucket, h]) * LOG2E

    m_ref[...] = jnp.full_like(m_ref, NEG_BIG)
    l_ref[...] = jnp.zeros_like(l_ref)
    acc_ref[...] = jnp.zeros_like(acc_ref)
    q_all = q_ref[0].reshape(DSA_HEADS * Q_BLOCK, DSA_KV_RANK)

    def key_step(kt, carry):
        kvt = kv_ref[pl.ds(pl.multiple_of(kt * KV_TILE, KV_TILE), KV_TILE), :]
        for part in range(2):
            rs = slice(part * half, (part + 1) * half)
            s_ref[rs, :] = lax.dot_general(q_all[rs], kvt, _NT, preferred_element_type=F32)
        for j in range(sub):
            d = kt * sub + j - qb

            @pl.when((d > -NEAR_TILES) & (d <= 0))
            def _(j=j, d=d):
                s_ref[:, j * LANES:(j + 1) * LANES] += corr_ref[d + NEAR_TILES - 1]

        for j in range(sub):
            madd_ref[:, j * LANES:(j + 1) * LANES] = mask_ref[0, 0, kt * sub + j].astype(F32)
        groups_per_head = Q_BLOCK // SM_ROWS
        for g in range(DSA_HEADS * groups_per_head):
            rs = slice(g * SM_ROWS, (g + 1) * SM_ROWS)
            qg = g % groups_per_head
            s = s_ref[rs, :] + madd_ref[qg * SM_ROWS:(qg + 1) * SM_ROWS, :]
            m_old = m_ref[rs, :]
            m_new = jnp.maximum(m_old, jnp.max(s, axis=1, keepdims=True))
            alpha = jnp.exp2(m_old - m_new)
            p = jnp.exp2(s - jnp.tile(m_new, (1, sub)))
            l_ref[rs, :] = alpha * l_ref[rs, :] + jnp.sum(p, axis=1, keepdims=True)
            alpha_ref[rs, :] = alpha
            p_ref[rs, :] = p.astype(BF16)
            m_ref[rs, :] = m_new
        for part in range(2):
            rs = slice(part * half, (part + 1) * half)
            pv = jnp.dot(p_ref[rs, :], kvt, preferred_element_type=F32)
            acc_ref[rs, :] = jnp.tile(alpha_ref[rs, :], (1, DSA_KV_RANK // LANES)) * acc_ref[rs, :] + pv
        return carry

    lax.fori_loop(0, qb // sub + 1, key_step, 0)

    dv = wuv_ref.shape[2]
    for h in range(DSA_HEADS):
        rs = head_rows(h)
        o = (acc_ref[rs, :] / jnp.tile(l_ref[rs, :], (1, DSA_KV_RANK // LANES))).astype(BF16)
        o_ref[:, h * dv:(h + 1) * dv] = jnp.dot(o, wuv_ref[h], preferred_element_type=F32).astype(BF16)


def _dsa_attn(q, kv, mask, rel_bias, w_uv, batch, seq):
    n = kv.shape[0]
    nqb = seq // Q_BLOCK
    dv = w_uv.shape[2]
    rows = DSA_HEADS * Q_BLOCK
    return pl.pallas_call(
        _dsa_attn_kernel,
        grid=(batch, nqb),
        in_specs=[
            pl.BlockSpec((1, DSA_HEADS, Q_BLOCK, DSA_KV_RANK), lambda b, qb: (b * nqb + qb, 0, 0, 0)),
            pl.BlockSpec((seq, DSA_KV_RANK), lambda b, qb: (b, 0)),
            pl.BlockSpec((1, 1) + mask.shape[2:], lambda b, qb: (b, qb, 0, 0, 0)),
            pl.BlockSpec(memory_space=pltpu.SMEM),
            pl.BlockSpec(w_uv.shape, lambda b, qb: (0, 0, 0)),
        ],
        out_specs=pl.BlockSpec((Q_BLOCK, DSA_HEADS * dv), lambda b, qb: (b * nqb + qb, 0)),
        out_shape=jax.ShapeDtypeStruct((n, DSA_HEADS * dv), BF16),
        scratch_shapes=[
            pltpu.VMEM((rows, LANES), F32),
            pltpu.VMEM((rows, LANES), F32),
            pltpu.VMEM((rows, LANES), F32),
            pltpu.VMEM((rows, DSA_KV_RANK), F32),
            pltpu.VMEM((NEAR_TILES, rows, LANES), F32),
            pltpu.VMEM((rows, KV_TILE), F32),
            pltpu.VMEM((rows, KV_TILE), BF16),
            pltpu.VMEM((Q_BLOCK, KV_TILE), F32),
        ],
        compiler_params=_params(("arbitrary", "arbitrary")),
        name="dsa_attn",
    )(q, kv, mask, rel_bias, w_uv)


def _pad_cols(w, width):
    return jnp.pad(w, ((0, 0), (0, width - w.shape[1])))


def kernel(x, ln_mix_g, ln_ffn_g, w_ffn_gate, w_ffn_up, w_ffn_down, rel_bias, ev_w_in, ev_w_out, sgu_ln_g, sgu_ln_b, sgu_w_s, sgu_b_s, od_w_in, od_w_out, hgrn_lb, hgrn_norm_g, dsa_cq_g, dsa_ckv_g, dsa_w_uq, dsa_qnorm_g, dsa_w_qidx, dsa_w_uv):
    batch, seq, d = x.shape
    n = batch * seq
    depth = ln_mix_g.shape[0]
    ksel = min(TOPK_MAX, seq // 4)
    tm = min(PROJ_ROWS, n)
    xf = x.reshape(n, d)
    ffn_f32 = (w_ffn_gate, w_ffn_up, w_ffn_down)
    ev_in_all, od_in_all = ev_w_in.astype(BF16), od_w_in.astype(BF16)
    for layer in range(depth):
        j = layer // 2
        if layer % 2 == 0:
            z, *ffn_w = _norm_matmul(xf, ln_mix_g[layer], ev_in_all, j, ev_in_all.shape[2], tm=tm,
                                     tn=PROJ_COLS, out_dtype=BF16, cast=[(w, layer) for w in ffn_f32])
            a1 = _retention(z, batch, seq)
            a2 = _sgu(z, sgu_ln_g[j], sgu_ln_b[j], sgu_w_s[j], sgu_b_s[j], rows=256)
            w_out = ev_w_out[j]
        else:
            w_in = od_w_in[j]
            gw = d // 2
            c = 4 * gw
            c_kidx = c + DSA_Q_RANK + DSA_KV_RANK
            w_dsa = jnp.concatenate([
                w_in[:, c:c_kidx],
                _pad_cols(w_in[:, c_kidx:c_kidx + IDX_DIM], LANES),
                _pad_cols(w_in[:, c_kidx + IDX_DIM:], LANES),
            ], axis=1).astype(BF16)
            a1, zd = _inproj_hgrn(xf, ln_mix_g[layer], od_in_all, j, w_dsa, hgrn_lb, hgrn_norm_g[j],
                                  batch, seq, layer)
            w_qit = jnp.pad(dsa_w_qidx[j].T.reshape(IDX_HEADS, IDX_DIM, DSA_Q_RANK),
                            ((0, 0), (0, LANES - IDX_DIM), (0, 0))).reshape(IDX_HEADS * LANES, DSA_Q_RANK)
            q, qit, kv, kix, wht = _dsa_prep(zd, dsa_cq_g[j], dsa_ckv_g[j], dsa_w_uq[j].astype(BF16),
                                             dsa_qnorm_g[j], w_qit.astype(BF16), tm=256)
            mask, *ffn_w = _dsa_select(qit, wht, kix, ffn_f32, layer, batch, seq, ksel)
            a2 = _dsa_attn(q, kv, mask, rel_bias, dsa_w_uv[j].astype(BF16), batch, seq)
            w_out = od_w_out[j]
        xf = _outproj(a1, a2, w_out.astype(BF16), xf, tm=min(OUT_ROWS, n), tn=OUT_COLS)
        xf = _ffn(xf, ln_ffn_g[layer], *(w[None] for w in ffn_w), 0, tm=min(FFN_ROWS, n), tf=FFN_COLS)
    return xf.reshape(batch, seq, d)
```

```python
import functools
import math

import jax
import jax.numpy as jnp
from jax import lax
from jax.experimental import pallas as pl
from jax.experimental.pallas import tpu as pltpu

F32 = jnp.float32
BF16 = jnp.bfloat16
I32 = jnp.int32

EPS = 1e-6
CHUNK = 64
LANES = 128
ROPE_BASE = 10000.0
RET_HEADS = 4
SGU_WINDOW = 128
SGU_GROUPS = 4
HG_HEADS = 8
DSA_HEADS = 8
DSA_Q_RANK = 384
DSA_KV_RANK = 256
IDX_HEADS = 16
IDX_DIM = 64
TOPK_MAX = 256
Q_BLOCK = 128
KV_TILE = 512
SM_ROWS = 64
LOG2E = math.log2(math.e)
REL_BUCKETS = 32
REL_MAX_DIST = 256
NEG_BIG = -1e30
INT_MIN = -(2 ** 31)
SCAN_TILES = 2
PLANE_ROWS = 256

RET_BLOCK = 256
RET_HEADS_PER_STEP = 4
HG_BLOCK = 256
HG_FINE = 4
VMEM_LIMIT = 48 * 1024 * 1024
PROJ_ROWS = 1024
PROJ_COLS = 1024
OUT_ROWS = 512
OUT_COLS = 2048
FFN_ROWS = 1024
FFN_VMEM_LIMIT = 58 * 1024 * 1024
SIDE_VMEM_LIMIT = 56 * 1024 * 1024
FFN_COLS = 512
SGU_ROWS = 512
PREP_ROWS = 512

_NT = (((1,), (1,)), ((), ()))
_TN = (((0,), (0,)), ((), ()))


def _params(semantics, vmem_limit=VMEM_LIMIT):
    return pltpu.CompilerParams(dimension_semantics=semantics, vmem_limit_bytes=vmem_limit)


def _silu(x):
    return x * jax.nn.sigmoid(x)


def _rms(x):
    return x * lax.rsqrt(jnp.mean(x * x, axis=-1, keepdims=True) + EPS)


def _cast_blocks(rows, steps):
    units = rows // 16
    blocks = max(k for k in range(1, min(units, steps) + 1) if units % k == 0)
    return rows // blocks, blocks


def _cast_specs(weights, step_of, steps):
    ins, outs, shapes = [], [], []
    for w, layer in weights:
        rows, blocks = _cast_blocks(w.shape[1], steps)

        def index(*ids, blocks=blocks):
            return (jnp.minimum(step_of(*ids), blocks - 1), 0)

        ins.append(pl.BlockSpec((None, rows, w.shape[2]), lambda *ids, layer=layer, index=index: (layer,) + index(*ids)))
        outs.append(pl.BlockSpec((rows, w.shape[2]), index))
        shapes.append(jax.ShapeDtypeStruct(w.shape[1:], BF16))
    return ins, outs, shapes


def _norm_matmul_kernel(x_ref, g_ref, w_ref, *rest):
    ncast = (len(rest) - 2) // 2
    o_ref, h_ref = rest[ncast], rest[-1]
    for src, dst in zip(rest[:ncast], rest[ncast + 1:-1]):
        dst[...] = src[...].astype(BF16)

    @pl.when(pl.program_id(1) == 0)
    def _():
        h_ref[...] = (_rms(x_ref[...]) * g_ref[...]).astype(BF16)

    o_ref[...] = jnp.dot(h_ref[...], w_ref[...], preferred_element_type=F32).astype(o_ref.dtype)


def _norm_matmul(x, g, w, layer, nout, *, tm, tn, out_dtype=F32, cast=()):
    n, d = x.shape
    ncols = nout // tn
    c_in, c_out, c_shape = _cast_specs(cast, lambda i, j: i * ncols + j, (n // tm) * ncols)
    return pl.pallas_call(
        _norm_matmul_kernel,
        grid=(n // tm, ncols),
        in_specs=[
            pl.BlockSpec((tm, d), lambda i, j: (i, 0)),
            pl.BlockSpec((1, d), lambda i, j: (0, 0)),
            pl.BlockSpec((None, d, tn), lambda i, j: (layer, 0, j)),
            *c_in,
        ],
        out_specs=[pl.BlockSpec((tm, tn), lambda i, j: (i, j)), *c_out],
        out_shape=[jax.ShapeDtypeStruct((n, nout), out_dtype), *c_shape],
        scratch_shapes=[pltpu.VMEM((tm, d), BF16)],
        compiler_params=_params(("arbitrary", "arbitrary"), SIDE_VMEM_LIMIT),
        name="norm_matmul",
    )(x, g.reshape(1, d), w, *(w_c for w_c, _ in cast))


def _outproj_kernel(a1_ref, a2_ref, w1_ref, w2_ref, r_ref, o_ref):
    acc = jnp.dot(a1_ref[...], w1_ref[...], preferred_element_type=F32)
    acc += jnp.dot(a2_ref[...], w2_ref[...], preferred_element_type=F32)
    o_ref[...] = r_ref[...] + acc


def _outproj(a1, a2, w, res, *, tm, tn):
    n, half = a1.shape
    d = w.shape[1]
    return pl.pallas_call(
        _outproj_kernel,
        grid=(n // tm, d // tn),
        in_specs=[
            pl.BlockSpec((tm, half), lambda i, j: (i, 0)),
            pl.BlockSpec((tm, half), lambda i, j: (i, 0)),
            pl.BlockSpec((half, tn), lambda i, j: (0, j)),
            pl.BlockSpec((half, tn), lambda i, j: (1, j)),
            pl.BlockSpec((tm, tn), lambda i, j: (i, j)),
        ],
        out_specs=pl.BlockSpec((tm, tn), lambda i, j: (i, j)),
        out_shape=jax.ShapeDtypeStruct((n, d), F32),
        compiler_params=_params(("arbitrary", "arbitrary")),
        name="outproj",
    )(a1, a2, w, w, res)


def _ffn_kernel(x_ref, g_ref, wg_ref, wu_ref, wd_ref, o_ref, h_ref):
    @pl.when(pl.program_id(1) == 0)
    def _():
        x = x_ref[...]
        h_ref[...] = (_rms(x) * g_ref[...]).astype(BF16)
        o_ref[...] = x

    h = h_ref[...]
    a = jnp.dot(h, wg_ref[...], preferred_element_type=F32)
    u = jnp.dot(h, wu_ref[...], preferred_element_type=F32)
    act = (_silu(a) * u).astype(BF16)
    o_ref[...] += jnp.dot(act, wd_ref[...], preferred_element_type=F32)


def _ffn(x, g, wg, wu, wd, layer, *, tm, tf):
    n, d = x.shape
    dff = wg.shape[2]
    return pl.pallas_call(
        _ffn_kernel,
        grid=(n // tm, dff // tf),
        in_specs=[
            pl.BlockSpec((tm, d), lambda i, f: (i, 0)),
            pl.BlockSpec((1, d), lambda i, f: (0, 0)),
            pl.BlockSpec((None, d, tf), lambda i, f: (layer, 0, f)),
            pl.BlockSpec((None, d, tf), lambda i, f: (layer, 0, f)),
            pl.BlockSpec((None, tf, d), lambda i, f: (layer, f, 0)),
        ],
        out_specs=pl.BlockSpec((tm, d), lambda i, f: (i, 0)),
        out_shape=jax.ShapeDtypeStruct((n, d), F32),
        scratch_shapes=[pltpu.VMEM((tm, d), BF16)],
        compiler_params=_params(("arbitrary", "arbitrary"), FFN_VMEM_LIMIT),
        name="ffn",
    )(x, g.reshape(1, d), wg, wu, wd)


def _retention_kernel(q_ref, k_ref, v_ref, g_ref, cos_ref, sin_ref, d_ref, xi_ref, zeta_ref,
                      gl_ref, o_ref, state_ref):
    @pl.when(pl.program_id(2) == 0)
    def _():
        state_ref[...] = jnp.zeros_like(state_ref)

    cos = cos_ref[...]
    sin = sin_ref[...]
    half = cos.shape[1]
    dk = 2 * half

    def rot(t):
        t1, t2 = t[:, :half], t[:, half:]
        return jnp.concatenate([t1 * cos - t2 * sin, t1 * sin + t2 * cos], axis=1)

    for i in range(RET_HEADS_PER_STEP):
        cs = slice(i * dk, (i + 1) * dk)
        q = rot(q_ref[:, cs].astype(F32))
        k = rot(k_ref[:, cs].astype(F32)) * (dk ** -0.5)
        qb = q.astype(BF16)
        vb = v_ref[:, cs].astype(BF16)
        scores = lax.dot_general(qb, k.astype(BF16), _NT, preferred_element_type=F32) * d_ref[i]
        intra = jnp.dot(scores.astype(BF16), vb, preferred_element_type=F32)
        state = state_ref[i]
        cross = jnp.dot(qb, state.astype(BF16), preferred_element_type=F32) * xi_ref[i]
        kz = (k * zeta_ref[i]).astype(BF16)
        state_ref[i] = state * gl_ref[i] + lax.dot_general(kz, vb, _TN, preferred_element_type=F32)
        o_ref[:, cs] = (_rms(intra + cross) * _silu(g_ref[:, cs].astype(F32))).astype(BF16)


def _retention_tables(seq, dk):
    blk = RET_BLOCK
    pos = jnp.arange(seq, dtype=F32)
    inv = ROPE_BASE ** (-jnp.arange(0, dk, 2, dtype=F32) / dk)
    ang = pos[:, None] * inv[None, :]
    log_gamma = jnp.log(1.0 - 2.0 ** (-5.0 - jnp.arange(RET_HEADS, dtype=F32)))
    i = jnp.arange(blk)
    same = (i[:, None] // CHUNK) == (i[None, :] // CHUNK)
    earlier = (i[None, :] // CHUNK) < (i[:, None] // CHUNK)
    diff = (i[:, None] - i[None, :]).astype(F32)
    dist = jnp.where(same, jnp.abs(diff), diff)
    decay = jnp.where((same | earlier)[None], jnp.exp(log_gamma[:, None, None] * dist[None]), 0.0)
    p = jnp.arange(blk, dtype=F32)
    wide = (RET_HEADS, blk, dk)
    xi = jnp.broadcast_to(jnp.exp(log_gamma[:, None] * (p + 1.0))[:, :, None], wide)
    zeta = jnp.broadcast_to(jnp.exp(log_gamma[:, None] * (blk - 1.0 - p))[:, :, None], wide)
    g_blk = jnp.broadcast_to(jnp.exp(log_gamma * blk)[:, None, None], (RET_HEADS, 1, dk))
    return jnp.cos(ang), jnp.sin(ang), decay, xi, zeta, g_blk


def _retention(z, batch, seq):
    n = z.shape[0]
    gw = z.shape[1] // 6
    dk = gw // RET_HEADS
    blk = RET_BLOCK
    nblk = seq // blk
    cos, sin, decay, xi, zeta, g_blk = _retention_tables(seq, dk)

    hp = RET_HEADS_PER_STEP
    groups = RET_HEADS // hp

    def zspec(part):
        return pl.BlockSpec((blk, hp * dk), lambda b, h, c: (b * nblk + c, part * groups + h))

    def hspec(rows, cols):
        return pl.BlockSpec((hp, rows, cols), lambda b, h, c: (h, 0, 0))

    return pl.pallas_call(
        _retention_kernel,
        grid=(batch, groups, nblk),
        in_specs=[
            zspec(0), zspec(1), zspec(2), zspec(3),
            pl.BlockSpec((blk, dk // 2), lambda b, h, c: (c, 0)),
            pl.BlockSpec((blk, dk // 2), lambda b, h, c: (c, 0)),
            hspec(blk, blk), hspec(blk, dk), hspec(blk, dk), hspec(1, dk),
        ],
        out_specs=pl.BlockSpec((blk, hp * dk), lambda b, h, c: (b * nblk + c, h)),
        out_shape=jax.ShapeDtypeStruct((n, gw), BF16),
        scratch_shapes=[pltpu.VMEM((hp, dk, dk), F32)],
        compiler_params=_params(("arbitrary", "arbitrary", "arbitrary")),
        name="retention",
    )(z, z, z, z, cos, sin, decay, xi, zeta, g_blk)


def _gelu(x):
    return 0.5 * x * (1.0 + lax.erf(x * math.sqrt(0.5)))


def _sgu_kernel(u_ref, v_ref, lng_ref, lnb_ref, w_ref, b_ref, o_ref):
    rows, width = v_ref.shape
    dg = width // SGU_GROUPS
    v = _gelu(v_ref[...].astype(F32))
    mu = jnp.mean(v, axis=-1, keepdims=True)
    var = jnp.mean(jnp.square(v - mu), axis=-1, keepdims=True)
    vn = ((v - mu) * lax.rsqrt(var + EPS) * lng_ref[...] + lnb_ref[...]).astype(BF16)
    u = _gelu(u_ref[...].astype(F32))
    ri = lax.broadcasted_iota(I32, (SGU_WINDOW, SGU_WINDOW), 0) // CHUNK
    ci = lax.broadcasted_iota(I32, (SGU_WINDOW, SGU_WINDOW), 1) // CHUNK
    allowed = ci <= ri
    for g in range(SGU_GROUPS):
        wg = jnp.where(allowed, w_ref[g], 0.0).astype(BF16)
        bias = b_ref[g]
        for w in range(rows // SGU_WINDOW):
            rs = slice(w * SGU_WINDOW, (w + 1) * SGU_WINDOW)
            cs = slice(g * dg, (g + 1) * dg)
            mixed = jnp.dot(wg, vn[rs, cs], preferred_element_type=F32) + bias
            o_ref[rs, cs] = (u[rs, cs] * mixed).astype(BF16)


def _sgu(z, ln_g, ln_b, w_s, b_s, *, rows):
    n = z.shape[0]
    gw = z.shape[1] // 6
    return pl.pallas_call(
        _sgu_kernel,
        grid=(n // rows,),
        in_specs=[
            pl.BlockSpec((rows, gw), lambda i: (i, 4)),
            pl.BlockSpec((rows, gw), lambda i: (i, 5)),
            pl.BlockSpec((1, gw), lambda i: (0, 0)),
            pl.BlockSpec((1, gw), lambda i: (0, 0)),
            pl.BlockSpec((SGU_GROUPS, SGU_WINDOW, SGU_WINDOW), lambda i: (0, 0, 0)),
            pl.BlockSpec((SGU_GROUPS, SGU_WINDOW, 1), lambda i: (0, 0, 0)),
        ],
        out_specs=pl.BlockSpec((rows, gw), lambda i: (i, 0)),
        out_shape=jax.ShapeDtypeStruct((n, gw), BF16),
        compiler_params=_params(("arbitrary",)),
        name="sgu",
    )(z, z, ln_g.reshape(1, gw), ln_b.reshape(1, gw), w_s, b_s.reshape(SGU_GROUPS, SGU_WINDOW, 1))


def _hgrn_head(q, f_logits, v, g, lb, ng, tri, st_ref, sh_ref):
    rows, dk = q.shape
    f = lb + (1.0 - lb) * jax.nn.sigmoid(f_logits)
    lf = jnp.log(f)
    kk = 1.0 - f
    qa = _silu(q)

    bcum = None
    rest = lf
    for _ in range(3):
        term = rest.astype(BF16)
        part = jnp.dot(tri, term, preferred_element_type=F32)
        bcum = part if bcum is None else bcum + part
        rest = rest - term.astype(F32)

    row = lax.broadcasted_iota(I32, (rows, dk), 0)
    ti = lax.broadcasted_iota(I32, (rows, rows), 0)
    si = lax.broadcasted_iota(I32, (rows, rows), 1)
    attn = jnp.zeros((rows, rows), F32)
    hs = rows // 2
    while hs >= HG_FINE:
        bs = 2 * hs
        parts = [jnp.broadcast_to(bcum[b * bs + hs - 1:b * bs + hs, :], (bs, dk))
                 for b in range(rows // bs)]
        anchor = parts[0] if len(parts) == 1 else jnp.concatenate(parts, axis=0)
        upper = (row & (bs - 1)) >= hs
        fac = jnp.exp(-jnp.abs(bcum - anchor))
        qt = jnp.where(upper, qa * fac, 0.0)
        kt = jnp.where(upper, 0.0, kk * fac)
        a = lax.dot_general(qt.astype(BF16), kt.astype(BF16), _NT, preferred_element_type=F32)
        if bs < rows:
            a = jnp.where((ti & -bs) == (si & -bs), a, 0.0)
        attn = attn + a
        hs //= 2

    vb = v.astype(BF16)
    near = qa * kk
    intra = jnp.sum(near, axis=1, keepdims=True) * v
    pad = jnp.zeros((HG_FINE, dk), F32)
    for idx, val in enumerate((kk, bcum, v)):
        sh_ref[idx, 0:HG_FINE, :] = pad
        sh_ref[idx, HG_FINE:, :] = val
    for delta in range(1, HG_FINE):
        back = slice(HG_FINE - delta, HG_FINE - delta + rows)
        prod = qa * sh_ref[0, back, :] * jnp.exp(jnp.minimum(bcum - sh_ref[1, back, :], 0.0))
        prod = jnp.where((row & (HG_FINE - 1)) >= delta, prod, 0.0)
        intra = intra + jnp.sum(prod, axis=1, keepdims=True) * sh_ref[2, back, :]
    intra = intra + jnp.dot(attn.astype(BF16), vb, preferred_element_type=F32)
    st = st_ref[...]
    cross = lax.dot_general((qa * jnp.exp(bcum)).astype(BF16), st.astype(BF16), _NT,
                            preferred_element_type=F32)
    blast = bcum[rows - 1:rows, :]
    kb = (kk * jnp.exp(blast - bcum)).astype(BF16)
    st_ref[...] = st * jnp.exp(blast) + lax.dot_general(vb, kb, _TN, preferred_element_type=F32)
    return _rms(intra + cross) * ng * _silu(g)


def _hgrn_lower_bound(lb_raw, layer):
    e = jnp.exp(lb_raw - jnp.max(lb_raw, axis=0, keepdims=True))
    soft = e / jnp.sum(e, axis=0, keepdims=True)
    return jnp.sum(soft[1:layer + 1], axis=0, keepdims=True)


def _inproj_hgrn_kernel(x_ref, g_ref, w_ref, ws_ref, lb_ref, ng_ref, tri_ref, o_ref, zd_ref,
                        z_ref, st_ref, sh_ref, *, layer):
    c = pl.program_id(1)
    gw = o_ref.shape[1]
    dk = gw // HG_HEADS

    @pl.when((pl.program_id(0) == 0) & (c == 0))
    def _():
        z_ref[...] = jnp.zeros_like(z_ref)

    @pl.when(c <= 1)
    def _():
        st_ref[...] = jnp.zeros_like(st_ref)

    h = (_rms(x_ref[...]) * g_ref[...]).astype(BF16)
    z_ref[c % 2] = jnp.dot(h, w_ref[...], preferred_element_type=F32)
    zd_ref[...] = jnp.dot(h, ws_ref[...], preferred_element_type=F32)

    prev = (c + 1) % 2
    lb = _hgrn_lower_bound(lb_ref[...], layer)
    tri = tri_ref[...]
    for hd in range(HG_HEADS):
        cols = [slice(part * gw + hd * dk, part * gw + (hd + 1) * dk) for part in range(4)]
        hs = slice(hd * dk, (hd + 1) * dk)
        out = _hgrn_head(z_ref[prev, :, cols[0]], z_ref[prev, :, cols[1]], z_ref[prev, :, cols[2]],
                         z_ref[prev, :, cols[3]], lb[:, hs], ng_ref[:, hs], tri, st_ref.at[hd], sh_ref.at[hd])
        o_ref[:, hs] = out.astype(BF16)


def _inproj_hgrn(x, g, w, layer_idx, w_side, lb_raw, norm_g, batch, seq, layer):
    n, d = x.shape
    gw = norm_g.shape[0]
    dk = gw // HG_HEADS
    blk = HG_BLOCK
    nblk = seq // blk
    ns = w_side.shape[1]
    depth = lb_raw.shape[0]
    const = lambda b, c: (0, 0)
    return pl.pallas_call(
        functools.partial(_inproj_hgrn_kernel, layer=layer),
        grid=(batch, nblk + 1),
        in_specs=[
            pl.BlockSpec((blk, d), lambda b, c: (b * nblk + jnp.minimum(c, nblk - 1), 0)),
            pl.BlockSpec((1, d), const),
            pl.BlockSpec((None, d, 4 * gw), lambda b, c: (layer_idx, 0, 0), pipeline_mode=pl.Buffered(1)),
            pl.BlockSpec((d, ns), const, pipeline_mode=pl.Buffered(1)),
            pl.BlockSpec((depth, gw), const),
            pl.BlockSpec((1, gw), const),
            pl.BlockSpec((blk, blk), const),
        ],
        out_specs=[
            pl.BlockSpec((blk, gw), lambda b, c: (b * nblk + jnp.maximum(c - 1, 0), 0)),
            pl.BlockSpec((blk, ns), lambda b, c: (b * nblk + jnp.minimum(c, nblk - 1), 0)),
        ],
        out_shape=[jax.ShapeDtypeStruct((n, gw), BF16), jax.ShapeDtypeStruct((n, ns), F32)],
        scratch_shapes=[
            pltpu.VMEM((2, blk, 4 * gw), F32),
            pltpu.VMEM((HG_HEADS, dk, dk), F32),
            pltpu.VMEM((HG_HEADS, 3, blk + HG_FINE, dk), F32),
        ],
        compiler_params=_params(("arbitrary", "arbitrary"), SIDE_VMEM_LIMIT),
        name="inproj_hgrn",
    )(x, g.reshape(1, d), w, w_side, lb_raw, norm_g.reshape(1, gw), jnp.tril(jnp.ones((blk, blk), BF16)))


def _dsa_prep_kernel(zd_ref, cqg_ref, ckvg_ref, wuq_ref, qng_ref, wqit_ref,
                     q_ref, qit_ref, kv_ref, kix_ref, wht_ref):
    zd = zd_ref[...]
    cq = (_rms(zd[:, :DSA_Q_RANK]) * cqg_ref[...]).astype(BF16)
    qf = jnp.dot(cq, wuq_ref[...], preferred_element_type=F32)
    for i in range(q_ref.shape[0]):
        rs = slice(i * Q_BLOCK, (i + 1) * Q_BLOCK)
        for h in range(DSA_HEADS):
            cs = slice(h * DSA_KV_RANK, (h + 1) * DSA_KV_RANK)
            q_ref[i, h] = (_rms(qf[rs, cs]) * qng_ref[...] * (DSA_KV_RANK ** -0.5 * LOG2E)).astype(BF16)
    qit = lax.dot_general(wqit_ref[...], cq, _NT, preferred_element_type=F32)
    qit = (qit * (IDX_DIM ** -0.5)).astype(BF16)
    for i in range(q_ref.shape[0]):
        for h in range(IDX_HEADS):
            c = (i * IDX_HEADS + h) * Q_BLOCK
            qit_ref[:, c:c + Q_BLOCK] = qit[h * LANES:(h + 1) * LANES, i * Q_BLOCK:(i + 1) * Q_BLOCK]
    c0 = DSA_Q_RANK
    c1 = c0 + DSA_KV_RANK
    kv_ref[...] = (_rms(zd[:, c0:c1]) * ckvg_ref[...]).astype(BF16)
    kix_ref[...] = zd[:, c1:c1 + LANES].astype(BF16)
    wht = jnp.transpose(zd[:, c1 + LANES:c1 + 2 * LANES] * (IDX_HEADS ** -0.5))
    wht_ref[...] = wht[:IDX_HEADS, :]


def _dsa_prep(zd, cq_g, ckv_g, w_uq, qn_g, w_qit, *, tm):
    n, wd = zd.shape
    dq = w_uq.shape[1]
    dqi = w_qit.shape[0]
    full = lambda i: (0, 0)
    rows = lambda i: (i, 0)
    cols = lambda i: (0, i)
    return pl.pallas_call(
        _dsa_prep_kernel,
        grid=(n // tm,),
        in_specs=[
            pl.BlockSpec((tm, wd), rows),
            pl.BlockSpec((1, DSA_Q_RANK), full),
            pl.BlockSpec((1, DSA_KV_RANK), full),
            pl.BlockSpec((DSA_Q_RANK, dq), full),
            pl.BlockSpec((1, DSA_KV_RANK), full),
            pl.BlockSpec((dqi, DSA_Q_RANK), full),
        ],
        out_specs=[
            pl.BlockSpec((tm // Q_BLOCK, DSA_HEADS, Q_BLOCK, DSA_KV_RANK), lambda i: (i, 0, 0, 0)),
            pl.BlockSpec((LANES, IDX_HEADS * tm), cols),
            pl.BlockSpec((tm, DSA_KV_RANK), rows),
            pl.BlockSpec((tm, LANES), rows),
            pl.BlockSpec((IDX_HEADS, tm), cols),
        ],
        out_shape=[
            jax.ShapeDtypeStruct((n // Q_BLOCK, DSA_HEADS, Q_BLOCK, DSA_KV_RANK), BF16),
            jax.ShapeDtypeStruct((LANES, IDX_HEADS * n), BF16),
            jax.ShapeDtypeStruct((n, DSA_KV_RANK), BF16),
            jax.ShapeDtypeStruct((n, LANES), BF16),
            jax.ShapeDtypeStruct((IDX_HEADS, n), F32),
        ],
        compiler_params=_params(("arbitrary",)),
        name="dsa_prep",
    )(zd, cq_g.reshape(1, -1), ckv_g.reshape(1, -1), w_uq, qn_g.reshape(1, -1), w_qit)


def _bit_planes(v):
    v = list(v)

    def swap(lo, hi, j, m):
        return (lo & ~m) | ((hi >> j) & m), (hi & m) | ((lo << j) & ~m)

    for j, m in ((16, 0x0000FFFF), (8, 0x00FF00FF), (4, 0x0F0F0F0F), (2, 0x33333333), (1, 0x55555555)):
        for k in range(len(v)):
            if k & j == 0:
                v[k], v[k + j] = swap(v[k], v[k + j], j, m)
    return v


def _dsa_select_kernel(qit_ref, wht_ref, kix_ref, wg_in, wu_in, wd_in, m_ref, wg_out, wu_out, wd_out,
                       key_ref, jc_ref, plane_ref, *, ksel, idx_bits):
    for src, dst in ((wg_in, wg_out), (wu_in, wu_out), (wd_in, wd_out)):
        dst[...] = src[...].astype(BF16)

    qb = pl.program_id(1)
    ntile = qb + 1
    ntile_all = m_ref.shape[2]
    rowi = lax.broadcasted_iota(I32, (LANES, Q_BLOCK), 0)
    coli = lax.broadcasted_iota(I32, (LANES, Q_BLOCK), 1)
    q_chunk = (qb * Q_BLOCK + coli) // CHUNK
    pairs = IDX_HEADS // 2
    w_pair = [jnp.concatenate([wht_ref[2 * p:2 * p + 1, :], wht_ref[2 * p + 1:2 * p + 2, :]], axis=1)
              for p in range(pairs)]

    def tile_rows(j):
        return pl.ds(pl.multiple_of(j * LANES, LANES), LANES)

    def admissible(j):
        return ((j * LANES + rowi) // CHUNK) <= q_chunk

    def score_tile(jj, carry):
        words = []
        for t in range(2):
            j = 2 * jj + t
            kt = kix_ref[tile_rows(j), :]
            sc = None
            for p in range(pairs):
                s2 = jnp.dot(kt, qit_ref[:, 2 * p * Q_BLOCK:2 * (p + 1) * Q_BLOCK],
                             preferred_element_type=F32)
                c2 = w_pair[p] * jnp.maximum(s2, 0.0)
                c = c2[:, :Q_BLOCK] + c2[:, Q_BLOCK:]
                sc = c if sc is None else sc + c
            bits = pltpu.bitcast(sc, I32)
            key = bits ^ ((bits >> 31) & 0x7FFFFFFF)
            key = jnp.where(admissible(j), key, INT_MIN)
            key_ref[tile_rows(j), :] = key
            words += [key[8 * k:8 * (k + 1), :] for k in range(LANES // 8)]
        for i, p in enumerate(_bit_planes(words)):
            plane_ref[pl.ds(pl.multiple_of(jj * PLANE_ROWS + 8 * i, 8), 8), :] = p
        return carry

    @pl.when(qb == 0)
    def _():
        key_ref[...] = jnp.full(key_ref.shape, INT_MIN, I32)
        plane_row = lax.broadcasted_iota(I32, plane_ref.shape, 0) & (PLANE_ROWS - 1)
        plane_ref[...] = jnp.where(plane_row < 8, -1, 0)

    lax.fori_loop(0, (ntile + 1) // 2, score_tile, 0)

    def count(pred_fn):
        def body(jj, acc):
            for t in range(SCAN_TILES):
                j = SCAN_TILES * jj + t
                acc = acc + pred_fn(j, key_ref[tile_rows(j), :]).astype(I32)
            return acc
        trips = (ntile + SCAN_TILES - 1) // SCAN_TILES
        acc = lax.fori_loop(0, trips, body, jnp.zeros((LANES, Q_BLOCK), I32))
        return jnp.sum(acc, axis=0, keepdims=True)

    nword = plane_ref.shape[0] // PLANE_ROWS

    def plane(v, i):
        return plane_ref[pl.ds(pl.multiple_of(v * PLANE_ROWS + 8 * i, 8), 8), :]

    def total(words):
        acc = lax.population_count(words[0])
        for w in words[1:]:
            acc = acc + lax.population_count(w)
        return jnp.sum(acc, axis=0, keepdims=True)

    def radix_pair(t, c):
        alive, above, t_u = c
        i = 2 * t
        flip = jnp.where(t == 0, -1, 0)
        set1 = [a & (plane(v, i) ^ flip) for v, a in enumerate(alive)]
        clr1 = [a ^ s for a, s in zip(alive, set1)]
        p2 = [plane(v, i + 1) for v in range(nword)]
        set1_set2 = [s & p for s, p in zip(set1, p2)]
        clr1_set2 = [s & p for s, p in zip(clr1, p2)]
        n1, n11, n01 = total(set1), total(set1_set2), total(clr1_set2)
        take1 = (above + n1) >= ksel
        above = jnp.where(take1, above, above + n1)
        n2 = jnp.where(take1, n11, n01)
        take2 = (above + n2) >= ksel
        above = jnp.where(take2, above, above + n2)
        alive = tuple(
            jnp.where(take1, jnp.where(take2, ss, s ^ ss), jnp.where(take2, cs, c0 ^ cs))
            for s, c0, ss, cs in zip(set1, clr1, set1_set2, clr1_set2))
        t_u = (t_u | jnp.where(take1, jnp.left_shift(jnp.int32(1), 31 - i), 0)
               | jnp.where(take2, jnp.left_shift(jnp.int32(1), 30 - i), 0))
        return alive, above, t_u

    start = (tuple(jnp.full((8, Q_BLOCK), -1, I32) for _ in range(nword)),
             jnp.zeros((1, Q_BLOCK), I32), jnp.zeros((1, Q_BLOCK), I32))
    alive, above, t_u = lax.fori_loop(0, 16, radix_pair, start)
    thr = t_u ^ INT_MIN
    ties = lax.population_count(alive[0])
    for a in alive[1:]:
        ties = ties + lax.population_count(a)
    cnt_t = above + jnp.sum(ties, axis=0, keepdims=True)
    tied = jnp.max(jnp.where((cnt_t > ksel) & (thr > INT_MIN), 1.0, 0.0))

    jc_ref[...] = jnp.full(jc_ref.shape, 2 ** 31 - 1, I32)

    @pl.when(tied > 0.0)
    def _():
        need = ksel - count(lambda j, k: k > thr)

        def index_bit(i, j_c):
            cand = j_c | jnp.left_shift(jnp.int32(1), idx_bits - 1 - i)
            cnt = count(lambda j, k: (k == thr) & ((j * LANES + rowi) < cand))
            return jnp.where(cnt < need, cand, j_c)

        j_c = lax.fori_loop(0, idx_bits, index_bit, jnp.zeros((1, Q_BLOCK), I32))
        jc_ref[...] = jnp.broadcast_to(j_c, jc_ref.shape)

    j_c = jc_ref[0:1, :]
    eye = (rowi == coli).astype(BF16)

    group = KV_TILE // LANES

    floor = jnp.maximum(thr, INT_MIN + 1)

    def write_group(with_ties):
        def body(g, carry):
            for t in range(group):
                j = g * group + t
                k = key_ref[tile_rows(j), :]
                if with_ties:
                    sel = ((k > thr) | ((k == thr) & ((j * LANES + rowi) <= j_c))) & admissible(j)
                else:
                    sel = k >= floor
                sel = jnp.where(sel, 1.0, 0.0).astype(BF16)
                sel_t = lax.dot_general(eye, sel, _NT, preferred_element_type=F32)
                m_ref[0, 0, j] = ((sel_t - 1.0) * -NEG_BIG).astype(BF16)
            return carry
        return body

    ngroup = (ntile + group - 1) // group

    @pl.when(tied > 0.0)
    def _():
        lax.fori_loop(0, ngroup, write_group(True), 0)

    @pl.when(tied <= 0.0)
    def _():
        lax.fori_loop(0, ngroup, write_group(False), 0)

    def blank_tile(j, carry):
        m_ref[0, 0, j] = jnp.full((Q_BLOCK, LANES), NEG_BIG, BF16)
        return carry

    lax.fori_loop(ngroup * group, ntile_all, blank_tile, 0)


def _dsa_select(qit, wht, kix, ffn_weights, layer, batch, seq, ksel):
    nqb = seq // Q_BLOCK
    nkt = seq // LANES
    w_in, w_out, w_shape = _cast_specs([(w, layer) for w in ffn_weights], lambda b, q: b * nqb + q,
                                       batch * nqb)
    return pl.pallas_call(
        functools.partial(_dsa_select_kernel, ksel=ksel, idx_bits=int(math.log2(seq))),
        grid=(batch, nqb),
        in_specs=[
            pl.BlockSpec((LANES, IDX_HEADS * Q_BLOCK), lambda b, q: (0, b * nqb + q)),
            pl.BlockSpec((IDX_HEADS, Q_BLOCK), lambda b, q: (0, b * nqb + q)),
            pl.BlockSpec((seq, LANES), lambda b, q: (b, 0)),
            *w_in,
        ],
        out_specs=[pl.BlockSpec((1, 1, nkt, Q_BLOCK, LANES), lambda b, q: (b, q, 0, 0, 0)), *w_out],
        out_shape=[jax.ShapeDtypeStruct((batch, nqb, nkt, Q_BLOCK, LANES), BF16), *w_shape],
        scratch_shapes=[pltpu.VMEM((seq, Q_BLOCK), I32), pltpu.VMEM((8, Q_BLOCK), I32),
                        pltpu.VMEM((seq, Q_BLOCK), I32)],
        compiler_params=_params(("arbitrary", "arbitrary")),
        name="dsa_select",
    )(qit, wht, kix, *ffn_weights)


def _rel_bucket(rel):
    nb = REL_BUCKETS // 2
    max_exact = nb // 2
    ret = jnp.where(rel > 0, nb, 0)
    n = jnp.abs(rel)
    nf = jnp.maximum(n, 1).astype(F32)
    large = max_exact + (jnp.log(nf / max_exact) / math.log(REL_MAX_DIST / max_exact)
                         * (nb - max_exact)).astype(I32)
    large = jnp.minimum(large, nb - 1)
    return ret + jnp.where(n < max_exact, n, large)


NEAR_TILES = 3


def _dsa_attn_kernel(q_ref, kv_ref, mask_ref, rb_ref, wuv_ref, o_ref,
                     m_ref, l_ref, alpha_ref, acc_ref, corr_ref, s_ref, p_ref, madd_ref):
    b, qb = pl.program_id(0), pl.program_id(1)
    sub = KV_TILE // LANES
    far_bucket = REL_BUCKETS // 2 - 1
    half = DSA_HEADS * Q_BLOCK // 2

    def head_rows(h):
        return slice(h * Q_BLOCK, (h + 1) * Q_BLOCK)

    @pl.when((b == 0) & (qb == 0))
    def _():
        ti = lax.broadcasted_iota(I32, (Q_BLOCK, LANES), 0)
        si = lax.broadcasted_iota(I32, (Q_BLOCK, LANES), 1)
        for oi in range(NEAR_TILES):
            bucket = _rel_bucket((oi - (NEAR_TILES - 1)) * LANES + si - ti)
            for h in range(DSA_HEADS):
                tbl = jnp.zeros((Q_BLOCK, LANES), F32)
                for bk in range(REL_BUCKETS):
                    tbl = jnp.where(bucket == bk, rb_ref[bk, h], tbl)
                corr_ref[oi, head_rows(h), :] = (tbl - rb_ref[far_bucket, h]) * LOG2E

    m_ref[...] = jnp.full_like(m_ref, NEG_BIG)
    l_ref[...] = jnp.zeros_like(l_ref)
    acc_ref[...] = jnp.zeros_like(acc_ref)
    q_all = q_ref[0].reshape(DSA_HEADS * Q_BLOCK, DSA_KV_RANK)

    def key_step(kt, carry):
        kvt = kv_ref[pl.ds(pl.multiple_of(kt * KV_TILE, KV_TILE), KV_TILE), :]
        for part in range(2):
            rs = slice(part * half, (part + 1) * half)
            s_ref[rs, :] = lax.dot_general(q_all[rs], kvt, _NT, preferred_element_type=F32)
        for j in range(sub):
            d = kt * sub + j - qb

            @pl.when((d > -NEAR_TILES) & (d <= 0))
            def _(j=j, d=d):
                s_ref[:, j * LANES:(j + 1) * LANES] += corr_ref[d + NEAR_TILES - 1]

        for j in range(sub):
            madd_ref[:, j * LANES:(j + 1) * LANES] = mask_ref[0, 0, kt * sub + j].astype(F32)
        groups_per_head = Q_BLOCK // SM_ROWS
        for g in range(DSA_HEADS * groups_per_head):
            rs = slice(g * SM_ROWS, (g + 1) * SM_ROWS)
            qg = g % groups_per_head
            s = s_ref[rs, :] + madd_ref[qg * SM_ROWS:(qg + 1) * SM_ROWS, :]
            m_old = m_ref[rs, :]
            m_new = jnp.maximum(m_old, jnp.max(s, axis=1, keepdims=True))
            alpha = jnp.exp2(m_old - m_new)
            p = jnp.exp2(s - jnp.tile(m_new, (1, sub)))
            l_ref[rs, :] = alpha * l_ref[rs, :] + jnp.sum(p, axis=1, keepdims=True)
            alpha_ref[rs, :] = alpha
            p_ref[rs, :] = p.astype(BF16)
            m_ref[rs, :] = m_new
        for part in range(2):
            rs = slice(part * half, (part + 1) * half)
            pv = jnp.dot(p_ref[rs, :], kvt, preferred_element_type=F32)
            acc_ref[rs, :] = jnp.tile(alpha_ref[rs, :], (1, DSA_KV_RANK // LANES)) * acc_ref[rs, :] + pv
        return carry

    lax.fori_loop(0, qb // sub + 1, key_step, 0)

    dv = wuv_ref.shape[2]
    for h in range(DSA_HEADS):
        rs = head_rows(h)
        o = (acc_ref[rs, :] / jnp.tile(l_ref[rs, :], (1, DSA_KV_RANK // LANES))).astype(BF16)
        o_ref[:, h * dv:(h + 1) * dv] = jnp.dot(o, wuv_ref[h], preferred_element_type=F32).astype(BF16)


def _dsa_attn(q, kv, mask, rel_bias, w_uv, batch, seq):
    n = kv.shape[0]
    nqb = seq // Q_BLOCK
    dv = w_uv.shape[2]
    rows = DSA_HEADS * Q_BLOCK
    return pl.pallas_call(
        _dsa_attn_kernel,
        grid=(batch, nqb),
        in_specs=[
            pl.BlockSpec((1, DSA_HEADS, Q_BLOCK, DSA_KV_RANK), lambda b, qb: (b * nqb + qb, 0, 0, 0)),
            pl.BlockSpec((seq, DSA_KV_RANK), lambda b, qb: (b, 0)),
            pl.BlockSpec((1, 1) + mask.shape[2:], lambda b, qb: (b, qb, 0, 0, 0)),
            pl.BlockSpec(memory_space=pltpu.SMEM),
            pl.BlockSpec(w_uv.shape, lambda b, qb: (0, 0, 0)),
        ],
        out_specs=pl.BlockSpec((Q_BLOCK, DSA_HEADS * dv), lambda b, qb: (b * nqb + qb, 0)),
        out_shape=jax.ShapeDtypeStruct((n, DSA_HEADS * dv), BF16),
        scratch_shapes=[
            pltpu.VMEM((rows, LANES), F32),
            pltpu.VMEM((rows, LANES), F32),
            pltpu.VMEM((rows, LANES), F32),
            pltpu.VMEM((rows, DSA_KV_RANK), F32),
            pltpu.VMEM((NEAR_TILES, rows, LANES), F32),
            pltpu.VMEM((rows, KV_TILE), F32),
            pltpu.VMEM((rows, KV_TILE), BF16),
            pltpu.VMEM((Q_BLOCK, KV_TILE), F32),
        ],
        compiler_params=_params(("arbitrary", "arbitrary")),
        name="dsa_attn",
    )(q, kv, mask, rel_bias, w_uv)


def _pad_cols(w, width):
    return jnp.pad(w, ((0, 0), (0, width - w.shape[1])))


def kernel(x, ln_mix_g, ln_ffn_g, w_ffn_gate, w_ffn_up, w_ffn_down, rel_bias, ev_w_in, ev_w_out, sgu_ln_g, sgu_ln_b, sgu_w_s, sgu_b_s, od_w_in, od_w_out, hgrn_lb, hgrn_norm_g, dsa_cq_g, dsa_ckv_g, dsa_w_uq, dsa_qnorm_g, dsa_w_qidx, dsa_w_uv):
    batch, seq, d = x.shape
    n = batch * seq
    depth = ln_mix_g.shape[0]
    ksel = min(TOPK_MAX, seq // 4)
    tm = min(PROJ_ROWS, n)
    xf = x.reshape(n, d)
    ffn_f32 = (w_ffn_gate, w_ffn_up, w_ffn_down)
    ev_in_all, od_in_all = ev_w_in.astype(BF16), od_w_in.astype(BF16)
    for layer in range(depth):
        j = layer // 2
        if layer % 2 == 0:
            z, *ffn_w = _norm_matmul(xf, ln_mix_g[layer], ev_in_all, j, ev_in_all.shape[2], tm=tm,
                                     tn=PROJ_COLS, out_dtype=BF16, cast=[(w, layer) for w in ffn_f32])
            a1 = _retention(z, batch, seq)
            a2 = _sgu(z, sgu_ln_g[j], sgu_ln_b[j], sgu_w_s[j], sgu_b_s[j], rows=min(SGU_ROWS, n))
            w_out = ev_w_out[j]
        else:
            w_in = od_w_in[j]
            gw = d // 2
            c = 4 * gw
            c_kidx = c + DSA_Q_RANK + DSA_KV_RANK
            w_dsa = jnp.concatenate([
                w_in[:, c:c_kidx],
                _pad_cols(w_in[:, c_kidx:c_kidx + IDX_DIM], LANES),
                _pad_cols(w_in[:, c_kidx + IDX_DIM:], LANES),
            ], axis=1).astype(BF16)
            a1, zd = _inproj_hgrn(xf, ln_mix_g[layer], od_in_all, j, w_dsa, hgrn_lb, hgrn_norm_g[j],
                                  batch, seq, layer)
            w_qit = jnp.pad(dsa_w_qidx[j].T.reshape(IDX_HEADS, IDX_DIM, DSA_Q_RANK),
                            ((0, 0), (0, LANES - IDX_DIM), (0, 0))).reshape(IDX_HEADS * LANES, DSA_Q_RANK)
            q, qit, kv, kix, wht = _dsa_prep(zd, dsa_cq_g[j], dsa_ckv_g[j], dsa_w_uq[j].astype(BF16),
                                             dsa_qnorm_g[j], w_qit.astype(BF16), tm=min(PREP_ROWS, n))
            mask, *ffn_w = _dsa_select(qit, wht, kix, ffn_f32, layer, batch, seq, ksel)
            a2 = _dsa_attn(q, kv, mask, rel_bias, dsa_w_uv[j].astype(BF16), batch, seq)
            w_out = od_w_out[j]
        xf = _outproj(a1, a2, w_out.astype(BF16), xf, tm=min(OUT_ROWS, n), tn=OUT_COLS)
        xf = _ffn(xf, ln_ffn_g[layer], *(w[None] for w in ffn_w), 0, tm=min(FFN_ROWS, n), tf=FFN_COLS)
    return xf.reshape(batch, seq, d)
```

```python
import functools
import math

import jax
import jax.numpy as jnp
from jax import lax
from jax.experimental import pallas as pl
from jax.experimental.pallas import tpu as pltpu

F32 = jnp.float32
BF16 = jnp.bfloat16
I32 = jnp.int32

EPS = 1e-6
CHUNK = 64
LANES = 128
ROPE_BASE = 10000.0
RET_HEADS = 4
SGU_WINDOW = 128
SGU_GROUPS = 4
HG_HEADS = 8
DSA_HEADS = 8
DSA_Q_RANK = 384
DSA_KV_RANK = 256
IDX_HEADS = 16
IDX_DIM = 64
TOPK_MAX = 256
Q_BLOCK = 128
KV_TILE = 512
SM_ROWS = 64
LOG2E = math.log2(math.e)
REL_BUCKETS = 32
REL_MAX_DIST = 256
NEG_BIG = -1e30
INT_MIN = -(2 ** 31)
SCAN_TILES = 2
PLANE_ROWS = 256

RET_BLOCK = 256
RET_HEADS_PER_STEP = 4
HG_BLOCK = 256
HG_FINE = 4
VMEM_LIMIT = 48 * 1024 * 1024
PROJ_ROWS = 1024
PROJ_COLS = 1024
OUT_ROWS = 512
OUT_COLS = 2048
FFN_ROWS = 1024
FFN_VMEM_LIMIT = 58 * 1024 * 1024
SIDE_VMEM_LIMIT = 56 * 1024 * 1024
FFN_COLS = 512
SGU_ROWS = 512
PREP_ROWS = 512

_NT = (((1,), (1,)), ((), ()))
_TN = (((0,), (0,)), ((), ()))


def _params(semantics, vmem_limit=VMEM_LIMIT):
    return pltpu.CompilerParams(dimension_semantics=semantics, vmem_limit_bytes=vmem_limit)


def _silu(x):
    return x * jax.nn.sigmoid(x)


def _rms(x):
    return x * lax.rsqrt(jnp.mean(x * x, axis=-1, keepdims=True) + EPS)


def _cast_blocks(rows, steps):
    units = rows // 16
    blocks = max(k for k in range(1, min(units, steps) + 1) if units % k == 0)
    return rows // blocks, blocks


def _cast_specs(weights, step_of, steps):
    ins, outs, shapes = [], [], []
    for w, layer in weights:
        rows, blocks = _cast_blocks(w.shape[1], steps)

        def index(*ids, blocks=blocks):
            return (jnp.minimum(step_of(*ids), blocks - 1), 0)

        ins.append(pl.BlockSpec((None, rows, w.shape[2]), lambda *ids, layer=layer, index=index: (layer,) + index(*ids)))
        outs.append(pl.BlockSpec((rows, w.shape[2]), index))
        shapes.append(jax.ShapeDtypeStruct(w.shape[1:], BF16))
    return ins, outs, shapes


def _norm_matmul_kernel(x_ref, g_ref, w_ref, *rest):
    ncast = (len(rest) - 2) // 2
    o_ref, h_ref = rest[ncast], rest[-1]
    for src, dst in zip(rest[:ncast], rest[ncast + 1:-1]):
        dst[...] = src[...].astype(BF16)

    @pl.when(pl.program_id(1) == 0)
    def _():
        h_ref[...] = (_rms(x_ref[...]) * g_ref[...]).astype(BF16)

    o_ref[...] = jnp.dot(h_ref[...], w_ref[...], preferred_element_type=F32).astype(o_ref.dtype)


def _norm_matmul(x, g, w, layer, nout, *, tm, tn, out_dtype=F32, cast=()):
    n, d = x.shape
    ncols = nout // tn
    c_in, c_out, c_shape = _cast_specs(cast, lambda i, j: i * ncols + j, (n // tm) * ncols)
    return pl.pallas_call(
        _norm_matmul_kernel,
        grid=(n // tm, ncols),
        in_specs=[
            pl.BlockSpec((tm, d), lambda i, j: (i, 0)),
            pl.BlockSpec((1, d), lambda i, j: (0, 0)),
            pl.BlockSpec((None, d, tn), lambda i, j: (layer, 0, j)),
            *c_in,
        ],
        out_specs=[pl.BlockSpec((tm, tn), lambda i, j: (i, j)), *c_out],
        out_shape=[jax.ShapeDtypeStruct((n, nout), out_dtype), *c_shape],
        scratch_shapes=[pltpu.VMEM((tm, d), BF16)],
        compiler_params=_params(("arbitrary", "arbitrary"), SIDE_VMEM_LIMIT),
        name="norm_matmul",
    )(x, g.reshape(1, d), w, *(w_c for w_c, _ in cast))


def _outproj_kernel(a1_ref, a2_ref, w1_ref, w2_ref, r_ref, o_ref):
    acc = jnp.dot(a1_ref[...], w1_ref[...], preferred_element_type=F32)
    acc += jnp.dot(a2_ref[...], w2_ref[...], preferred_element_type=F32)
    o_ref[...] = r_ref[...] + acc


def _outproj(a1, a2, w, res, *, tm, tn):
    n, half = a1.shape
    d = w.shape[1]
    return pl.pallas_call(
        _outproj_kernel,
        grid=(n // tm, d // tn),
        in_specs=[
            pl.BlockSpec((tm, half), lambda i, j: (i, 0)),
            pl.BlockSpec((tm, half), lambda i, j: (i, 0)),
            pl.BlockSpec((half, tn), lambda i, j: (0, j)),
            pl.BlockSpec((half, tn), lambda i, j: (1, j)),
            pl.BlockSpec((tm, tn), lambda i, j: (i, j)),
        ],
        out_specs=pl.BlockSpec((tm, tn), lambda i, j: (i, j)),
        out_shape=jax.ShapeDtypeStruct((n, d), F32),
        compiler_params=_params(("arbitrary", "arbitrary")),
        name="outproj",
    )(a1, a2, w, w, res)


def _ffn_kernel(x_ref, g_ref, wg_ref, wu_ref, wd_ref, o_ref, h_ref):
    @pl.when(pl.program_id(1) == 0)
    def _():
        x = x_ref[...]
        h_ref[...] = (_rms(x) * g_ref[...]).astype(BF16)
        o_ref[...] = x

    h = h_ref[...]
    a = jnp.dot(h, wg_ref[...], preferred_element_type=F32)
    u = jnp.dot(h, wu_ref[...], preferred_element_type=F32)
    act = (_silu(a) * u).astype(BF16)
    o_ref[...] += jnp.dot(act, wd_ref[...], preferred_element_type=F32)


def _ffn(x, g, wg, wu, wd, layer, *, tm, tf):
    n, d = x.shape
    dff = wg.shape[2]
    return pl.pallas_call(
        _ffn_kernel,
        grid=(n // tm, dff // tf),
        in_specs=[
            pl.BlockSpec((tm, d), lambda i, f: (i, 0)),
            pl.BlockSpec((1, d), lambda i, f: (0, 0)),
            pl.BlockSpec((None, d, tf), lambda i, f: (layer, 0, f)),
            pl.BlockSpec((None, d, tf), lambda i, f: (layer, 0, f)),
            pl.BlockSpec((None, tf, d), lambda i, f: (layer, f, 0)),
        ],
        out_specs=pl.BlockSpec((tm, d), lambda i, f: (i, 0)),
        out_shape=jax.ShapeDtypeStruct((n, d), F32),
        scratch_shapes=[pltpu.VMEM((tm, d), BF16)],
        compiler_params=_params(("arbitrary", "arbitrary"), FFN_VMEM_LIMIT),
        name="ffn",
    )(x, g.reshape(1, d), wg, wu, wd)


def _retention_kernel(q_ref, k_ref, v_ref, g_ref, cos_ref, sin_ref, d_ref, xi_ref, zeta_ref,
                      gl_ref, o_ref, state_ref):
    @pl.when(pl.program_id(2) == 0)
    def _():
        state_ref[...] = jnp.zeros_like(state_ref)

    cos = cos_ref[...]
    sin = sin_ref[...]
    half = cos.shape[1]
    dk = 2 * half

    def rot(t):
        t1, t2 = t[:, :half], t[:, half:]
        return jnp.concatenate([t1 * cos - t2 * sin, t1 * sin + t2 * cos], axis=1)

    for i in range(RET_HEADS_PER_STEP):
        cs = slice(i * dk, (i + 1) * dk)
        q = rot(q_ref[:, cs].astype(F32))
        k = rot(k_ref[:, cs].astype(F32)) * (dk ** -0.5)
        qb = q.astype(BF16)
        vb = v_ref[:, cs].astype(BF16)
        scores = lax.dot_general(qb, k.astype(BF16), _NT, preferred_element_type=F32) * d_ref[i]
        intra = jnp.dot(scores.astype(BF16), vb, preferred_element_type=F32)
        state = state_ref[i]
        cross = jnp.dot(qb, state.astype(BF16), preferred_element_type=F32) * xi_ref[i]
        kz = (k * zeta_ref[i]).astype(BF16)
        state_ref[i] = state * gl_ref[i] + lax.dot_general(kz, vb, _TN, preferred_element_type=F32)
        o_ref[:, cs] = (_rms(intra + cross) * _silu(g_ref[:, cs].astype(F32))).astype(BF16)


def _retention_tables(seq, dk):
    blk = RET_BLOCK
    pos = jnp.arange(seq, dtype=F32)
    inv = ROPE_BASE ** (-jnp.arange(0, dk, 2, dtype=F32) / dk)
    ang = pos[:, None] * inv[None, :]
    log_gamma = jnp.log(1.0 - 2.0 ** (-5.0 - jnp.arange(RET_HEADS, dtype=F32)))
    i = jnp.arange(blk)
    same = (i[:, None] // CHUNK) == (i[None, :] // CHUNK)
    earlier = (i[None, :] // CHUNK) < (i[:, None] // CHUNK)
    diff = (i[:, None] - i[None, :]).astype(F32)
    dist = jnp.where(same, jnp.abs(diff), diff)
    decay = jnp.where((same | earlier)[None], jnp.exp(log_gamma[:, None, None] * dist[None]), 0.0)
    p = jnp.arange(blk, dtype=F32)
    wide = (RET_HEADS, blk, dk)
    xi = jnp.broadcast_to(jnp.exp(log_gamma[:, None] * (p + 1.0))[:, :, None], wide)
    zeta = jnp.broadcast_to(jnp.exp(log_gamma[:, None] * (blk - 1.0 - p))[:, :, None], wide)
    g_blk = jnp.broadcast_to(jnp.exp(log_gamma * blk)[:, None, None], (RET_HEADS, 1, dk))
    return jnp.cos(ang), jnp.sin(ang), decay, xi, zeta, g_blk


def _retention(z, batch, seq):
    n = z.shape[0]
    gw = z.shape[1] // 6
    dk = gw // RET_HEADS
    blk = RET_BLOCK
    nblk = seq // blk
    cos, sin, decay, xi, zeta, g_blk = _retention_tables(seq, dk)

    hp = RET_HEADS_PER_STEP
    groups = RET_HEADS // hp

    def zspec(part):
        return pl.BlockSpec((blk, hp * dk), lambda b, h, c: (b * nblk + c, part * groups + h))

    def hspec(rows, cols):
        return pl.BlockSpec((hp, rows, cols), lambda b, h, c: (h, 0, 0))

    return pl.pallas_call(
        _retention_kernel,
        grid=(batch, groups, nblk),
        in_specs=[
            zspec(0), zspec(1), zspec(2), zspec(3),
            pl.BlockSpec((blk, dk // 2), lambda b, h, c: (c, 0)),
            pl.BlockSpec((blk, dk // 2), lambda b, h, c: (c, 0)),
            hspec(blk, blk), hspec(blk, dk), hspec(blk, dk), hspec(1, dk),
        ],
        out_specs=pl.BlockSpec((blk, hp * dk), lambda b, h, c: (b * nblk + c, h)),
        out_shape=jax.ShapeDtypeStruct((n, gw), BF16),
        scratch_shapes=[pltpu.VMEM((hp, dk, dk), F32)],
        compiler_params=_params(("arbitrary", "arbitrary", "arbitrary")),
        name="retention",
    )(z, z, z, z, cos, sin, decay, xi, zeta, g_blk)


def _gelu(x):
    return 0.5 * x * (1.0 + lax.erf(x * math.sqrt(0.5)))


def _sgu_kernel(u_ref, v_ref, lng_ref, lnb_ref, w_ref, b_ref, o_ref):
    rows, width = v_ref.shape
    dg = width // SGU_GROUPS
    v = _gelu(v_ref[...].astype(F32))
    mu = jnp.mean(v, axis=-1, keepdims=True)
    var = jnp.mean(jnp.square(v - mu), axis=-1, keepdims=True)
    vn = ((v - mu) * lax.rsqrt(var + EPS) * lng_ref[...] + lnb_ref[...]).astype(BF16)
    u = _gelu(u_ref[...].astype(F32))
    ri = lax.broadcasted_iota(I32, (SGU_WINDOW, SGU_WINDOW), 0) // CHUNK
    ci = lax.broadcasted_iota(I32, (SGU_WINDOW, SGU_WINDOW), 1) // CHUNK
    allowed = ci <= ri
    for g in range(SGU_GROUPS):
        wg = jnp.where(allowed, w_ref[g], 0.0).astype(BF16)
        bias = b_ref[g]
        for w in range(rows // SGU_WINDOW):
            rs = slice(w * SGU_WINDOW, (w + 1) * SGU_WINDOW)
            cs = slice(g * dg, (g + 1) * dg)
            mixed = jnp.dot(wg, vn[rs, cs], preferred_element_type=F32) + bias
            o_ref[rs, cs] = (u[rs, cs] * mixed).astype(BF16)


def _sgu(z, ln_g, ln_b, w_s, b_s, *, rows):
    n = z.shape[0]
    gw = z.shape[1] // 6
    return pl.pallas_call(
        _sgu_kernel,
        grid=(n // rows,),
        in_specs=[
            pl.BlockSpec((rows, gw), lambda i: (i, 4)),
            pl.BlockSpec((rows, gw), lambda i: (i, 5)),
            pl.BlockSpec((1, gw), lambda i: (0, 0)),
            pl.BlockSpec((1, gw), lambda i: (0, 0)),
            pl.BlockSpec((SGU_GROUPS, SGU_WINDOW, SGU_WINDOW), lambda i: (0, 0, 0)),
            pl.BlockSpec((SGU_GROUPS, SGU_WINDOW, 1), lambda i: (0, 0, 0)),
        ],
        out_specs=pl.BlockSpec((rows, gw), lambda i: (i, 0)),
        out_shape=jax.ShapeDtypeStruct((n, gw), BF16),
        compiler_params=_params(("arbitrary",)),
        name="sgu",
    )(z, z, ln_g.reshape(1, gw), ln_b.reshape(1, gw), w_s, b_s.reshape(SGU_GROUPS, SGU_WINDOW, 1))


def _hgrn_head(q, f_logits, v, g, lb, ng, tri, st_ref, sh_ref):
    rows, dk = q.shape
    f = lb + (1.0 - lb) * jax.nn.sigmoid(f_logits)
    lf = jnp.log(f)
    kk = 1.0 - f
    qa = _silu(q)

    bcum = None
    rest = lf
    for _ in range(3):
        term = rest.astype(BF16)
        part = jnp.dot(tri, term, preferred_element_type=F32)
        bcum = part if bcum is None else bcum + part
        rest = rest - term.astype(F32)

    row = lax.broadcasted_iota(I32, (rows, dk), 0)
    ti = lax.broadcasted_iota(I32, (rows, rows), 0)
    si = lax.broadcasted_iota(I32, (rows, rows), 1)
    attn = jnp.zeros((rows, rows), F32)
    hs = rows // 2
    while hs >= HG_FINE:
        bs = 2 * hs
        parts = [jnp.broadcast_to(bcum[b * bs + hs - 1:b * bs + hs, :], (bs, dk))
                 for b in range(rows // bs)]
        anchor = parts[0] if len(parts) == 1 else jnp.concatenate(parts, axis=0)
        upper = (row & (bs - 1)) >= hs
        fac = jnp.exp(-jnp.abs(bcum - anchor))
        qt = jnp.where(upper, qa * fac, 0.0)
        kt = jnp.where(upper, 0.0, kk * fac)
        a = lax.dot_general(qt.astype(BF16), kt.astype(BF16), _NT, preferred_element_type=F32)
        if bs < rows:
            a = jnp.where((ti & -bs) == (si & -bs), a, 0.0)
        attn = attn + a
        hs //= 2

    vb = v.astype(BF16)
    near = qa * kk
    intra = jnp.sum(near, axis=1, keepdims=True) * v
    pad = jnp.zeros((HG_FINE, dk), F32)
    for idx, val in enumerate((kk, bcum, v)):
        sh_ref[idx, 0:HG_FINE, :] = pad
        sh_ref[idx, HG_FINE:, :] = val
    for delta in range(1, HG_FINE):
        back = slice(HG_FINE - delta, HG_FINE - delta + rows)
        prod = qa * sh_ref[0, back, :] * jnp.exp(jnp.minimum(bcum - sh_ref[1, back, :], 0.0))
        prod = jnp.where((row & (HG_FINE - 1)) >= delta, prod, 0.0)
        intra = intra + jnp.sum(prod, axis=1, keepdims=True) * sh_ref[2, back, :]
    intra = intra + jnp.dot(attn.astype(BF16), vb, preferred_element_type=F32)
    st = st_ref[...]
    cross = lax.dot_general((qa * jnp.exp(bcum)).astype(BF16), st.astype(BF16), _NT,
                            preferred_element_type=F32)
    blast = bcum[rows - 1:rows, :]
    kb = (kk * jnp.exp(blast - bcum)).astype(BF16)
    st_ref[...] = st * jnp.exp(blast) + lax.dot_general(vb, kb, _TN, preferred_element_type=F32)
    return _rms(intra + cross) * ng * _silu(g)


def _hgrn_lower_bound(lb_raw, layer):
    e = jnp.exp(lb_raw - jnp.max(lb_raw, axis=0, keepdims=True))
    soft = e / jnp.sum(e, axis=0, keepdims=True)
    return jnp.sum(soft[1:layer + 1], axis=0, keepdims=True)


def _inproj_hgrn_kernel(x_ref, g_ref, w_ref, ws_ref, lb_ref, ng_ref, tri_ref, o_ref, zd_ref,
                        z_ref, st_ref, sh_ref, *, layer):
    c = pl.program_id(1)
    gw = o_ref.shape[1]
    dk = gw // HG_HEADS

    @pl.when((pl.program_id(0) == 0) & (c == 0))
    def _():
        z_ref[...] = jnp.zeros_like(z_ref)

    @pl.when(c <= 1)
    def _():
        st_ref[...] = jnp.zeros_like(st_ref)

    h = (_rms(x_ref[...]) * g_ref[...]).astype(BF16)
    z_ref[c % 2] = jnp.dot(h, w_ref[...], preferred_element_type=F32)
    zd_ref[...] = jnp.dot(h, ws_ref[...], preferred_element_type=F32)

    prev = (c + 1) % 2
    lb = _hgrn_lower_bound(lb_ref[...], layer)
    tri = tri_ref[...]
    for hd in range(HG_HEADS):
        cols = [slice(part * gw + hd * dk, part * gw + (hd + 1) * dk) for part in range(4)]
        hs = slice(hd * dk, (hd + 1) * dk)
        out = _hgrn_head(z_ref[prev, :, cols[0]], z_ref[prev, :, cols[1]], z_ref[prev, :, cols[2]],
                         z_ref[prev, :, cols[3]], lb[:, hs], ng_ref[:, hs], tri, st_ref.at[hd], sh_ref.at[hd])
        o_ref[:, hs] = out.astype(BF16)


def _inproj_hgrn(x, g, w, layer_idx, w_side, lb_raw, norm_g, batch, seq, layer):
    n, d = x.shape
    gw = norm_g.shape[0]
    dk = gw // HG_HEADS
    blk = HG_BLOCK
    nblk = seq // blk
    ns = w_side.shape[1]
    depth = lb_raw.shape[0]
    const = lambda b, c: (0, 0)
    return pl.pallas_call(
        functools.partial(_inproj_hgrn_kernel, layer=layer),
        grid=(batch, nblk + 1),
        in_specs=[
            pl.BlockSpec((blk, d), lambda b, c: (b * nblk + jnp.minimum(c, nblk - 1), 0)),
            pl.BlockSpec((1, d), const),
            pl.BlockSpec((None, d, 4 * gw), lambda b, c: (layer_idx, 0, 0), pipeline_mode=pl.Buffered(1)),
            pl.BlockSpec((d, ns), const, pipeline_mode=pl.Buffered(1)),
            pl.BlockSpec((depth, gw), const),
            pl.BlockSpec((1, gw), const),
            pl.BlockSpec((blk, blk), const),
        ],
        out_specs=[
            pl.BlockSpec((blk, gw), lambda b, c: (b * nblk + jnp.maximum(c - 1, 0), 0)),
            pl.BlockSpec((blk, ns), lambda b, c: (b * nblk + jnp.minimum(c, nblk - 1), 0)),
        ],
        out_shape=[jax.ShapeDtypeStruct((n, gw), BF16), jax.ShapeDtypeStruct((n, ns), F32)],
        scratch_shapes=[
            pltpu.VMEM((2, blk, 4 * gw), F32),
            pltpu.VMEM((HG_HEADS, dk, dk), F32),
            pltpu.VMEM((HG_HEADS, 3, blk + HG_FINE, dk), F32),
        ],
        compiler_params=_params(("arbitrary", "arbitrary"), SIDE_VMEM_LIMIT),
        name="inproj_hgrn",
    )(x, g.reshape(1, d), w, w_side, lb_raw, norm_g.reshape(1, gw), jnp.tril(jnp.ones((blk, blk), BF16)))


def _dsa_prep_kernel(zd_ref, cqg_ref, ckvg_ref, wuq_ref, qng_ref, wqit_ref,
                     q_ref, qit_ref, kv_ref, kix_ref, wht_ref):
    zd = zd_ref[...]
    cq = (_rms(zd[:, :DSA_Q_RANK]) * cqg_ref[...]).astype(BF16)
    qf = jnp.dot(cq, wuq_ref[...], preferred_element_type=F32)
    for i in range(q_ref.shape[0]):
        rs = slice(i * Q_BLOCK, (i + 1) * Q_BLOCK)
        for h in range(DSA_HEADS):
            cs = slice(h * DSA_KV_RANK, (h + 1) * DSA_KV_RANK)
            q_ref[i, h] = (_rms(qf[rs, cs]) * qng_ref[...] * (DSA_KV_RANK ** -0.5 * LOG2E)).astype(BF16)
    qit = lax.dot_general(wqit_ref[...], cq, _NT, preferred_element_type=F32)
    qit = (qit * (IDX_DIM ** -0.5)).astype(BF16)
    for i in range(q_ref.shape[0]):
        for h in range(IDX_HEADS):
            c = (i * IDX_HEADS + h) * Q_BLOCK
            qit_ref[:, c:c + Q_BLOCK] = qit[h * LANES:(h + 1) * LANES, i * Q_BLOCK:(i + 1) * Q_BLOCK]
    c0 = DSA_Q_RANK
    c1 = c0 + DSA_KV_RANK
    kv_ref[...] = (_rms(zd[:, c0:c1]) * ckvg_ref[...]).astype(BF16)
    kix_ref[...] = zd[:, c1:c1 + LANES].astype(BF16)
    wht = jnp.transpose(zd[:, c1 + LANES:c1 + 2 * LANES] * (IDX_HEADS ** -0.5))
    wht_ref[...] = wht[:IDX_HEADS, :]


def _dsa_prep(zd, cq_g, ckv_g, w_uq, qn_g, w_qit, *, tm):
    n, wd = zd.shape
    dq = w_uq.shape[1]
    dqi = w_qit.shape[0]
    full = lambda i: (0, 0)
    rows = lambda i: (i, 0)
    cols = lambda i: (0, i)
    return pl.pallas_call(
        _dsa_prep_kernel,
        grid=(n // tm,),
        in_specs=[
            pl.BlockSpec((tm, wd), rows),
            pl.BlockSpec((1, DSA_Q_RANK), full),
            pl.BlockSpec((1, DSA_KV_RANK), full),
            pl.BlockSpec((DSA_Q_RANK, dq), full),
            pl.BlockSpec((1, DSA_KV_RANK), full),
            pl.BlockSpec((dqi, DSA_Q_RANK), full),
        ],
        out_specs=[
            pl.BlockSpec((tm // Q_BLOCK, DSA_HEADS, Q_BLOCK, DSA_KV_RANK), lambda i: (i, 0, 0, 0)),
            pl.BlockSpec((LANES, IDX_HEADS * tm), cols),
            pl.BlockSpec((tm, DSA_KV_RANK), rows),
            pl.BlockSpec((tm, LANES), rows),
            pl.BlockSpec((IDX_HEADS, tm), cols),
        ],
        out_shape=[
            jax.ShapeDtypeStruct((n // Q_BLOCK, DSA_HEADS, Q_BLOCK, DSA_KV_RANK), BF16),
            jax.ShapeDtypeStruct((LANES, IDX_HEADS * n), BF16),
            jax.ShapeDtypeStruct((n, DSA_KV_RANK), BF16),
            jax.ShapeDtypeStruct((n, LANES), BF16),
            jax.ShapeDtypeStruct((IDX_HEADS, n), F32),
        ],
        compiler_params=_params(("arbitrary",)),
        name="dsa_prep",
    )(zd, cq_g.reshape(1, -1), ckv_g.reshape(1, -1), w_uq, qn_g.reshape(1, -1), w_qit)


def _bit_planes(v):
    v = list(v)

    def swap(lo, hi, j, m):
        return (lo & ~m) | ((hi >> j) & m), (hi & m) | ((lo << j) & ~m)

    for j, m in ((16, 0x0000FFFF), (8, 0x00FF00FF), (4, 0x0F0F0F0F), (2, 0x33333333), (1, 0x55555555)):
        for k in range(len(v)):
            if k & j == 0:
                v[k], v[k + j] = swap(v[k], v[k + j], j, m)
    return v


def _dsa_select_kernel(qit_ref, wht_ref, kix_ref, wg_in, wu_in, wd_in, m_ref, wg_out, wu_out, wd_out,
                       key_ref, jc_ref, plane_ref, *, ksel, idx_bits):
    for src, dst in ((wg_in, wg_out), (wu_in, wu_out), (wd_in, wd_out)):
        dst[...] = src[...].astype(BF16)

    qb = pl.program_id(1)
    ntile = qb + 1
    ntile_all = m_ref.shape[2]
    rowi = lax.broadcasted_iota(I32, (LANES, Q_BLOCK), 0)
    coli = lax.broadcasted_iota(I32, (LANES, Q_BLOCK), 1)
    q_chunk = (qb * Q_BLOCK + coli) // CHUNK
    pairs = IDX_HEADS // 2
    w_pair = [jnp.concatenate([wht_ref[2 * p:2 * p + 1, :], wht_ref[2 * p + 1:2 * p + 2, :]], axis=1)
              for p in range(pairs)]

    def tile_rows(j):
        return pl.ds(pl.multiple_of(j * LANES, LANES), LANES)

    def admissible(j):
        return ((j * LANES + rowi) // CHUNK) <= q_chunk

    def score_tile(jj, carry):
        words = []
        for t in range(2):
            j = 2 * jj + t
            kt = kix_ref[tile_rows(j), :]
            sc = None
            for p in range(pairs):
                s2 = jnp.dot(kt, qit_ref[:, 2 * p * Q_BLOCK:2 * (p + 1) * Q_BLOCK],
                             preferred_element_type=F32)
                c2 = w_pair[p] * jnp.maximum(s2, 0.0)
                c = c2[:, :Q_BLOCK] + c2[:, Q_BLOCK:]
                sc = c if sc is None else sc + c
            bits = pltpu.bitcast(sc, I32)
            key = bits ^ ((bits >> 31) & 0x7FFFFFFF)
            key = jnp.where(key == -1, 0, key)
            key = jnp.where(admissible(j), key, INT_MIN)
            key_ref[tile_rows(j), :] = key
            words += [key[8 * k:8 * (k + 1), :] for k in range(LANES // 8)]
        for i, p in enumerate(_bit_planes(words)):
            plane_ref[pl.ds(pl.multiple_of(jj * PLANE_ROWS + 8 * i, 8), 8), :] = p
        return carry

    @pl.when(qb == 0)
    def _():
        key_ref[...] = jnp.full(key_ref.shape, INT_MIN, I32)
        plane_row = lax.broadcasted_iota(I32, plane_ref.shape, 0) & (PLANE_ROWS - 1)
        plane_ref[...] = jnp.where(plane_row < 8, -1, 0)

    lax.fori_loop(0, (ntile + 1) // 2, score_tile, 0)

    def count(pred_fn):
        def body(jj, acc):
            for t in range(SCAN_TILES):
                j = SCAN_TILES * jj + t
                acc = acc + pred_fn(j, key_ref[tile_rows(j), :]).astype(I32)
            return acc
        trips = (ntile + SCAN_TILES - 1) // SCAN_TILES
        acc = lax.fori_loop(0, trips, body, jnp.zeros((LANES, Q_BLOCK), I32))
        return jnp.sum(acc, axis=0, keepdims=True)

    nword = plane_ref.shape[0] // PLANE_ROWS

    def plane(v, i):
        return plane_ref[pl.ds(pl.multiple_of(v * PLANE_ROWS + 8 * i, 8), 8), :]

    def total(words):
        acc = lax.population_count(words[0])
        for w in words[1:]:
            acc = acc + lax.population_count(w)
        return jnp.sum(acc, axis=0, keepdims=True)

    def radix_pair(t, c):
        alive, above, t_u = c
        i = 2 * t
        flip = jnp.where(t == 0, -1, 0)
        set1 = [a & (plane(v, i) ^ flip) for v, a in enumerate(alive)]
        clr1 = [a ^ s for a, s in zip(alive, set1)]
        p2 = [plane(v, i + 1) for v in range(nword)]
        set1_set2 = [s & p for s, p in zip(set1, p2)]
        clr1_set2 = [s & p for s, p in zip(clr1, p2)]
        n1, n11, n01 = total(set1), total(set1_set2), total(clr1_set2)
        take1 = (above + n1) >= ksel
        above = jnp.where(take1, above, above + n1)
        n2 = jnp.where(take1, n11, n01)
        take2 = (above + n2) >= ksel
        above = jnp.where(take2, above, above + n2)
        alive = tuple(
            jnp.where(take1, jnp.where(take2, ss, s ^ ss), jnp.where(take2, cs, c0 ^ cs))
            for s, c0, ss, cs in zip(set1, clr1, set1_set2, clr1_set2))
        t_u = (t_u | jnp.where(take1, jnp.left_shift(jnp.int32(1), 31 - i), 0)
               | jnp.where(take2, jnp.left_shift(jnp.int32(1), 30 - i), 0))
        return alive, above, t_u

    start = (tuple(jnp.full((8, Q_BLOCK), -1, I32) for _ in range(nword)),
             jnp.zeros((1, Q_BLOCK), I32), jnp.zeros((1, Q_BLOCK), I32))
    alive, above, t_u = lax.fori_loop(0, 16, radix_pair, start)
    thr = t_u ^ INT_MIN
    ties = lax.population_count(alive[0])
    for a in alive[1:]:
        ties = ties + lax.population_count(a)
    cnt_t = above + jnp.sum(ties, axis=0, keepdims=True)
    tied = jnp.max(jnp.where((cnt_t > ksel) & (thr > INT_MIN), 1.0, 0.0))

    jc_ref[...] = jnp.full(jc_ref.shape, 2 ** 31 - 1, I32)

    @pl.when(tied > 0.0)
    def _():
        need = ksel - count(lambda j, k: k > thr)

        def index_bit(i, j_c):
            cand = j_c | jnp.left_shift(jnp.int32(1), idx_bits - 1 - i)
            cnt = count(lambda j, k: (k == thr) & ((j * LANES + rowi) < cand))
            return jnp.where(cnt < need, cand, j_c)

        j_c = lax.fori_loop(0, idx_bits, index_bit, jnp.zeros((1, Q_BLOCK), I32))
        jc_ref[...] = jnp.broadcast_to(j_c, jc_ref.shape)

    j_c = jc_ref[0:1, :]
    eye = (rowi == coli).astype(BF16)

    group = KV_TILE // LANES

    floor = jnp.maximum(thr, INT_MIN + 1)

    def write_group(with_ties):
        def body(g, carry):
            for t in range(group):
                j = g * group + t
                k = key_ref[tile_rows(j), :]
                if with_ties:
                    sel = ((k > thr) | ((k == thr) & ((j * LANES + rowi) <= j_c))) & admissible(j)
                else:
                    sel = k >= floor
                sel = jnp.where(sel, 1.0, 0.0).astype(BF16)
                sel_t = lax.dot_general(eye, sel, _NT, preferred_element_type=F32)
                m_ref[0, 0, j] = ((sel_t - 1.0) * -NEG_BIG).astype(BF16)
            return carry
        return body

    ngroup = (ntile + group - 1) // group

    @pl.when(tied > 0.0)
    def _():
        lax.fori_loop(0, ngroup, write_group(True), 0)

    @pl.when(tied <= 0.0)
    def _():
        lax.fori_loop(0, ngroup, write_group(False), 0)

    def blank_tile(j, carry):
        m_ref[0, 0, j] = jnp.full((Q_BLOCK, LANES), NEG_BIG, BF16)
        return carry

    lax.fori_loop(ngroup * group, ntile_all, blank_tile, 0)


def _dsa_select(qit, wht, kix, ffn_weights, layer, batch, seq, ksel):
    nqb = seq // Q_BLOCK
    nkt = seq // LANES
    w_in, w_out, w_shape = _cast_specs([(w, layer) for w in ffn_weights], lambda b, q: b * nqb + q,
                                       batch * nqb)
    return pl.pallas_call(
        functools.partial(_dsa_select_kernel, ksel=ksel, idx_bits=int(math.log2(seq))),
        grid=(batch, nqb),
        in_specs=[
            pl.BlockSpec((LANES, IDX_HEADS * Q_BLOCK), lambda b, q: (0, b * nqb + q)),
            pl.BlockSpec((IDX_HEADS, Q_BLOCK), lambda b, q: (0, b * nqb + q)),
            pl.BlockSpec((seq, LANES), lambda b, q: (b, 0)),
            *w_in,
        ],
        out_specs=[pl.BlockSpec((1, 1, nkt, Q_BLOCK, LANES), lambda b, q: (b, q, 0, 0, 0)), *w_out],
        out_shape=[jax.ShapeDtypeStruct((batch, nqb, nkt, Q_BLOCK, LANES), BF16), *w_shape],
        scratch_shapes=[pltpu.VMEM((seq, Q_BLOCK), I32), pltpu.VMEM((8, Q_BLOCK), I32),
                        pltpu.VMEM((seq, Q_BLOCK), I32)],
        compiler_params=_params(("arbitrary", "arbitrary")),
        name="dsa_select",
    )(qit, wht, kix, *ffn_weights)


def _rel_bucket(rel):
    nb = REL_BUCKETS // 2
    max_exact = nb // 2
    ret = jnp.where(rel > 0, nb, 0)
    n = jnp.abs(rel)
    nf = jnp.maximum(n, 1).astype(F32)
    large = max_exact + (jnp.log(nf / max_exact) / math.log(REL_MAX_DIST / max_exact)
                         * (nb - max_exact)).astype(I32)
    large = jnp.minimum(large, nb - 1)
    return ret + jnp.where(n < max_exact, n, large)


NEAR_TILES = 3


def _dsa_attn_kernel(q_ref, kv_ref, mask_ref, rb_ref, wuv_ref, o_ref,
                     m_ref, l_ref, alpha_ref, acc_ref, corr_ref, s_ref, p_ref, madd_ref):
    b, qb = pl.program_id(0), pl.program_id(1)
    sub = KV_TILE // LANES
    far_bucket = REL_BUCKETS // 2 - 1
    half = DSA_HEADS * Q_BLOCK // 2

    def head_rows(h):
        return slice(h * Q_BLOCK, (h + 1) * Q_BLOCK)

    @pl.when((b == 0) & (qb == 0))
    def _():
        ti = lax.broadcasted_iota(I32, (Q_BLOCK, LANES), 0)
        si = lax.broadcasted_iota(I32, (Q_BLOCK, LANES), 1)
        for oi in range(NEAR_TILES):
            bucket = _rel_bucket((oi - (NEAR_TILES - 1)) * LANES + si - ti)
            for h in range(DSA_HEADS):
                tbl = jnp.zeros((Q_BLOCK, LANES), F32)
                for bk in range(REL_BUCKETS):
                    tbl = jnp.where(bucket == bk, rb_ref[bk, h], tbl)
                corr_ref[oi, head_rows(h), :] = (tbl - rb_ref[far_bucket, h]) * LOG2E

    m_ref[...] = jnp.full_like(m_ref, NEG_BIG)
    l_ref[...] = jnp.zeros_like(l_ref)
    acc_ref[...] = jnp.zeros_like(acc_ref)
    q_all = q_ref[0].reshape(DSA_HEADS * Q_BLOCK, DSA_KV_RANK)

    def key_step(kt, carry):
        kvt = kv_ref[pl.ds(pl.multiple_of(kt * KV_TILE, KV_TILE), KV_TILE), :]
        for part in range(2):
            rs = slice(part * half, (part + 1) * half)
            s_ref[rs, :] = lax.dot_general(q_all[rs], kvt, _NT, preferred_element_type=F32)
        for j in range(sub):
            d = kt * sub + j - qb

            @pl.when((d > -NEAR_TILES) & (d <= 0))
            def _(j=j, d=d):
                s_ref[:, j * LANES:(j + 1) * LANES] += corr_ref[d + NEAR_TILES - 1]

        for j in range(sub):
            madd_ref[:, j * LANES:(j + 1) * LANES] = mask_ref[0, 0, kt * sub + j].astype(F32)
        groups_per_head = Q_BLOCK // SM_ROWS
        for g in range(DSA_HEADS * groups_per_head):
            rs = slice(g * SM_ROWS, (g + 1) * SM_ROWS)
            qg = g % groups_per_head
            s = s_ref[rs, :] + madd_ref[qg * SM_ROWS:(qg + 1) * SM_ROWS, :]
            m_old = m_ref[rs, :]
            m_new = jnp.maximum(m_old, jnp.max(s, axis=1, keepdims=True))
            alpha = jnp.exp2(m_old - m_new)
            p = jnp.exp2(s - jnp.tile(m_new, (1, sub)))
            l_ref[rs, :] = alpha * l_ref[rs, :] + jnp.sum(p, axis=1, keepdims=True)
            alpha_ref[rs, :] = alpha
            p_ref[rs, :] = p.astype(BF16)
            m_ref[rs, :] = m_new
        for part in range(2):
            rs = slice(part * half, (part + 1) * half)
            pv = jnp.dot(p_ref[rs, :], kvt, preferred_element_type=F32)
            acc_ref[rs, :] = jnp.tile(alpha_ref[rs, :], (1, DSA_KV_RANK // LANES)) * acc_ref[rs, :] + pv
        return carry

    lax.fori_loop(0, qb // sub + 1, key_step, 0)

    dv = wuv_ref.shape[2]
    for h in range(DSA_HEADS):
        rs = head_rows(h)
        o = (acc_ref[rs, :] / jnp.tile(l_ref[rs, :], (1, DSA_KV_RANK // LANES))).astype(BF16)
        o_ref[:, h * dv:(h + 1) * dv] = jnp.dot(o, wuv_ref[h], preferred_element_type=F32).astype(BF16)


def _dsa_attn(q, kv, mask, rel_bias, w_uv, batch, seq):
    n = kv.shape[0]
    nqb = seq // Q_BLOCK
    dv = w_uv.shape[2]
    rows = DSA_HEADS * Q_BLOCK
    return pl.pallas_call(
        _dsa_attn_kernel,
        grid=(batch, nqb),
        in_specs=[
            pl.BlockSpec((1, DSA_HEADS, Q_BLOCK, DSA_KV_RANK), lambda b, qb: (b * nqb + qb, 0, 0, 0)),
            pl.BlockSpec((seq, DSA_KV_RANK), lambda b, qb: (b, 0)),
            pl.BlockSpec((1, 1) + mask.shape[2:], lambda b, qb: (b, qb, 0, 0, 0)),
            pl.BlockSpec(memory_space=pltpu.SMEM),
            pl.BlockSpec(w_uv.shape, lambda b, qb: (0, 0, 0)),
        ],
        out_specs=pl.BlockSpec((Q_BLOCK, DSA_HEADS * dv), lambda b, qb: (b * nqb + qb, 0)),
        out_shape=jax.ShapeDtypeStruct((n, DSA_HEADS * dv), BF16),
        scratch_shapes=[
            pltpu.VMEM((rows, LANES), F32),
            pltpu.VMEM((rows, LANES), F32),
            pltpu.VMEM((rows, LANES), F32),
            pltpu.VMEM((rows, DSA_KV_RANK), F32),
            pltpu.VMEM((NEAR_TILES, rows, LANES), F32),
            pltpu.VMEM((rows, KV_TILE), F32),
            pltpu.VMEM((rows, KV_TILE), BF16),
            pltpu.VMEM((Q_BLOCK, KV_TILE), F32),
        ],
        compiler_params=_params(("arbitrary", "arbitrary")),
        name="dsa_attn",
    )(q, kv, mask, rel_bias, w_uv)


def _pad_cols(w, width):
    return jnp.pad(w, ((0, 0), (0, width - w.shape[1])))


def kernel(x, ln_mix_g, ln_ffn_g, w_ffn_gate, w_ffn_up, w_ffn_down, rel_bias, ev_w_in, ev_w_out, sgu_ln_g, sgu_ln_b, sgu_w_s, sgu_b_s, od_w_in, od_w_out, hgrn_lb, hgrn_norm_g, dsa_cq_g, dsa_ckv_g, dsa_w_uq, dsa_qnorm_g, dsa_w_qidx, dsa_w_uv):
    batch, seq, d = x.shape
    n = batch * seq
    depth = ln_mix_g.shape[0]
    ksel = min(TOPK_MAX, seq // 4)
    tm = min(PROJ_ROWS, n)
    xf = x.reshape(n, d)
    ffn_f32 = (w_ffn_gate, w_ffn_up, w_ffn_down)
    ev_in_all, od_in_all = ev_w_in.astype(BF16), od_w_in.astype(BF16)
    for layer in range(depth):
        j = layer // 2
        if layer % 2 == 0:
            z, *ffn_w = _norm_matmul(xf, ln_mix_g[layer], ev_in_all, j, ev_in_all.shape[2], tm=tm,
                                     tn=PROJ_COLS, out_dtype=BF16, cast=[(w, layer) for w in ffn_f32])
            a1 = _retention(z, batch, seq)
            a2 = _sgu(z, sgu_ln_g[j], sgu_ln_b[j], sgu_w_s[j], sgu_b_s[j], rows=min(SGU_ROWS, n))
            w_out = ev_w_out[j]
        else:
            w_in = od_w_in[j]
            gw = d // 2
            c = 4 * gw
            c_kidx = c + DSA_Q_RANK + DSA_KV_RANK
            w_dsa = jnp.concatenate([
                w_in[:, c:c_kidx],
                _pad_cols(w_in[:, c_kidx:c_kidx + IDX_DIM], LANES),
                _pad_cols(w_in[:, c_kidx + IDX_DIM:], LANES),
            ], axis=1).astype(BF16)
            a1, zd = _inproj_hgrn(xf, ln_mix_g[layer], od_in_all, j, w_dsa, hgrn_lb, hgrn_norm_g[j],
                                  batch, seq, layer)
            w_qit = jnp.pad(dsa_w_qidx[j].T.reshape(IDX_HEADS, IDX_DIM, DSA_Q_RANK),
                            ((0, 0), (0, LANES - IDX_DIM), (0, 0))).reshape(IDX_HEADS * LANES, DSA_Q_RANK)
            q, qit, kv, kix, wht = _dsa_prep(zd, dsa_cq_g[j], dsa_ckv_g[j], dsa_w_uq[j].astype(BF16),
                                             dsa_qnorm_g[j], w_qit.astype(BF16), tm=min(PREP_ROWS, n))
            mask, *ffn_w = _dsa_select(qit, wht, kix, ffn_f32, layer, batch, seq, ksel)
            a2 = _dsa_attn(q, kv, mask, rel_bias, dsa_w_uv[j].astype(BF16), batch, seq)
            w_out = od_w_out[j]
        xf = _outproj(a1, a2, w_out.astype(BF16), xf, tm=min(OUT_ROWS, n), tn=OUT_COLS)
        xf = _ffn(xf, ln_ffn_g[layer], *(w[None] for w in ffn_w), 0, tm=min(FFN_ROWS, n), tf=FFN_COLS)
    return xf.reshape(batch, seq, d)
```

```python
import functools
import math

import jax
import jax.numpy as jnp
from jax import lax
from jax.experimental import pallas as pl
from jax.experimental.pallas import tpu as pltpu

F32 = jnp.float32
BF16 = jnp.bfloat16
I32 = jnp.int32

EPS = 1e-6
CHUNK = 64
LANES = 128
ROPE_BASE = 10000.0
RET_HEADS = 4
SGU_WINDOW = 128
SGU_GROUPS = 4
HG_HEADS = 8
DSA_HEADS = 8
DSA_Q_RANK = 384
DSA_KV_RANK = 256
IDX_HEADS = 16
IDX_DIM = 64
TOPK_MAX = 256
Q_BLOCK = 128
KV_TILE = 512
SM_ROWS = 64
LOG2E = math.log2(math.e)
REL_BUCKETS = 32
REL_MAX_DIST = 256
NEG_BIG = -1e30
INT_MIN = -(2 ** 31)
SCAN_TILES = 2
PLANE_ROWS = 256

RET_BLOCK = 256
HG_BLOCK = 256
HG_FINE = 4
VMEM_LIMIT = 48 * 1024 * 1024
PROJ_ROWS = 1024
PROJ_COLS = 1024
OUT_ROWS = 512
OUT_COLS = 2048
FFN_ROWS = 1024
FFN_VMEM_LIMIT = 58 * 1024 * 1024
SIDE_VMEM_LIMIT = 56 * 1024 * 1024
FFN_COLS = 512
PREP_ROWS = 512

_NT = (((1,), (1,)), ((), ()))
_TN = (((0,), (0,)), ((), ()))


def _params(semantics, vmem_limit=VMEM_LIMIT):
    return pltpu.CompilerParams(dimension_semantics=semantics, vmem_limit_bytes=vmem_limit)


def _silu(x):
    return x * jax.nn.sigmoid(x)


def _rms(x):
    return x * lax.rsqrt(jnp.mean(x * x, axis=-1, keepdims=True) + EPS)


def _cast_blocks(rows, steps):
    units = rows // 16
    blocks = max(k for k in range(1, min(units, steps) + 1) if units % k == 0)
    return rows // blocks, blocks


def _cast_specs(weights, step_of, steps):
    ins, outs, shapes = [], [], []
    for w, layer in weights:
        rows, blocks = _cast_blocks(w.shape[1], steps)

        def index(*ids, blocks=blocks):
            return (jnp.minimum(step_of(*ids), blocks - 1), 0)

        ins.append(pl.BlockSpec((None, rows, w.shape[2]), lambda *ids, layer=layer, index=index: (layer,) + index(*ids)))
        outs.append(pl.BlockSpec((rows, w.shape[2]), index))
        shapes.append(jax.ShapeDtypeStruct(w.shape[1:], BF16))
    return ins, outs, shapes


def _norm_matmul_kernel(x_ref, g_ref, w_ref, *rest):
    ncast = (len(rest) - 2) // 2
    o_ref, h_ref = rest[ncast], rest[-1]
    for src, dst in zip(rest[:ncast], rest[ncast + 1:-1]):
        dst[...] = src[...].astype(BF16)

    @pl.when(pl.program_id(1) == 0)
    def _():
        h_ref[...] = (_rms(x_ref[...]) * g_ref[...]).astype(BF16)

    o_ref[...] = jnp.dot(h_ref[...], w_ref[...], preferred_element_type=F32).astype(o_ref.dtype)


def _norm_matmul(x, g, w, layer, nout, *, tm, tn, out_dtype=F32, cast=()):
    n, d = x.shape
    ncols = nout // tn
    c_in, c_out, c_shape = _cast_specs(cast, lambda i, j: i * ncols + j, (n // tm) * ncols)
    return pl.pallas_call(
        _norm_matmul_kernel,
        grid=(n // tm, ncols),
        in_specs=[
            pl.BlockSpec((tm, d), lambda i, j: (i, 0)),
            pl.BlockSpec((1, d), lambda i, j: (0, 0)),
            pl.BlockSpec((None, d, tn), lambda i, j: (layer, 0, j)),
            *c_in,
        ],
        out_specs=[pl.BlockSpec((tm, tn), lambda i, j: (i, j)), *c_out],
        out_shape=[jax.ShapeDtypeStruct((n, nout), out_dtype), *c_shape],
        scratch_shapes=[pltpu.VMEM((tm, d), BF16)],
        compiler_params=_params(("arbitrary", "arbitrary"), SIDE_VMEM_LIMIT),
        name="norm_matmul",
    )(x, g.reshape(1, d), w, *(w_c for w_c, _ in cast))


def _outproj_kernel(a1_ref, a2_ref, w1_ref, w2_ref, r_ref, o_ref):
    acc = jnp.dot(a1_ref[...], w1_ref[...], preferred_element_type=F32)
    acc += jnp.dot(a2_ref[...], w2_ref[...], preferred_element_type=F32)
    o_ref[...] = r_ref[...] + acc


def _outproj(a1, a2, w, res, *, tm, tn):
    n, half = a1.shape
    d = w.shape[1]
    return pl.pallas_call(
        _outproj_kernel,
        grid=(n // tm, d // tn),
        in_specs=[
            pl.BlockSpec((tm, half), lambda i, j: (i, 0)),
            pl.BlockSpec((tm, half), lambda i, j: (i, 0)),
            pl.BlockSpec((half, tn), lambda i, j: (0, j)),
            pl.BlockSpec((half, tn), lambda i, j: (1, j)),
            pl.BlockSpec((tm, tn), lambda i, j: (i, j)),
        ],
        out_specs=pl.BlockSpec((tm, tn), lambda i, j: (i, j)),
        out_shape=jax.ShapeDtypeStruct((n, d), F32),
        compiler_params=_params(("arbitrary", "arbitrary")),
        name="outproj",
    )(a1, a2, w, w, res)


def _ffn_kernel(x_ref, g_ref, wg_ref, wu_ref, wd_ref, o_ref, h_ref):
    @pl.when(pl.program_id(1) == 0)
    def _():
        x = x_ref[...]
        h_ref[...] = (_rms(x) * g_ref[...]).astype(BF16)
        o_ref[...] = x

    h = h_ref[...]
    a = jnp.dot(h, wg_ref[...], preferred_element_type=F32)
    u = jnp.dot(h, wu_ref[...], preferred_element_type=F32)
    act = (_silu(a) * u).astype(BF16)
    o_ref[...] += jnp.dot(act, wd_ref[...], preferred_element_type=F32)


def _ffn(x, g, wg, wu, wd, layer, *, tm, tf):
    n, d = x.shape
    dff = wg.shape[2]
    return pl.pallas_call(
        _ffn_kernel,
        grid=(n // tm, dff // tf),
        in_specs=[
            pl.BlockSpec((tm, d), lambda i, f: (i, 0)),
            pl.BlockSpec((1, d), lambda i, f: (0, 0)),
            pl.BlockSpec((None, d, tf), lambda i, f: (layer, 0, f)),
            pl.BlockSpec((None, d, tf), lambda i, f: (layer, 0, f)),
            pl.BlockSpec((None, tf, d), lambda i, f: (layer, f, 0)),
        ],
        out_specs=pl.BlockSpec((tm, d), lambda i, f: (i, 0)),
        out_shape=jax.ShapeDtypeStruct((n, d), F32),
        scratch_shapes=[pltpu.VMEM((tm, d), BF16)],
        compiler_params=_params(("arbitrary", "arbitrary"), FFN_VMEM_LIMIT),
        name="ffn",
    )(x, g.reshape(1, d), wg, wu, wd)


def _retention_tables(seq, dk):
    blk = RET_BLOCK
    pos = jnp.arange(seq, dtype=F32)
    inv = ROPE_BASE ** (-jnp.arange(0, dk, 2, dtype=F32) / dk)
    ang = pos[:, None] * inv[None, :]
    log_gamma = jnp.log(1.0 - 2.0 ** (-5.0 - jnp.arange(RET_HEADS, dtype=F32)))
    i = jnp.arange(blk)
    same = (i[:, None] // CHUNK) == (i[None, :] // CHUNK)
    earlier = (i[None, :] // CHUNK) < (i[:, None] // CHUNK)
    diff = (i[:, None] - i[None, :]).astype(F32)
    dist = jnp.where(same, jnp.abs(diff), diff)
    decay = jnp.where((same | earlier)[None], jnp.exp(log_gamma[:, None, None] * dist[None]), 0.0)
    p = jnp.arange(blk, dtype=F32)
    wide = (RET_HEADS, blk, dk)
    xi = jnp.broadcast_to(jnp.exp(log_gamma[:, None] * (p + 1.0))[:, :, None], wide)
    zeta = jnp.broadcast_to(jnp.exp(log_gamma[:, None] * (blk - 1.0 - p))[:, :, None], wide)
    g_blk = jnp.broadcast_to(jnp.exp(log_gamma * blk)[:, None, None], (RET_HEADS, 1, dk))
    return jnp.cos(ang), jnp.sin(ang), decay, xi, zeta, g_blk


def _gelu(x):
    return 0.5 * x * (1.0 + lax.erf(x * math.sqrt(0.5)))


def _sgu_kernel(u_ref, v_ref, lng_ref, lnb_ref, w_ref, b_ref, o_ref):
    rows, width = v_ref.shape
    dg = width // SGU_GROUPS
    v = _gelu(v_ref[...].astype(F32))
    mu = jnp.mean(v, axis=-1, keepdims=True)
    var = jnp.mean(jnp.square(v - mu), axis=-1, keepdims=True)
    vn = ((v - mu) * lax.rsqrt(var + EPS) * lng_ref[...] + lnb_ref[...]).astype(BF16)
    u = _gelu(u_ref[...].astype(F32))
    ri = lax.broadcasted_iota(I32, (SGU_WINDOW, SGU_WINDOW), 0) // CHUNK
    ci = lax.broadcasted_iota(I32, (SGU_WINDOW, SGU_WINDOW), 1) // CHUNK
    allowed = ci <= ri
    for g in range(SGU_GROUPS):
        wg = jnp.where(allowed, w_ref[g], 0.0).astype(BF16)
        bias = b_ref[g]
        for w in range(rows // SGU_WINDOW):
            rs = slice(w * SGU_WINDOW, (w + 1) * SGU_WINDOW)
            cs = slice(g * dg, (g + 1) * dg)
            mixed = jnp.dot(wg, vn[rs, cs], preferred_element_type=F32) + bias
            o_ref[rs, cs] = (u[rs, cs] * mixed).astype(BF16)


def _even_mixers_kernel(q_ref, k_ref, v_ref, g_ref, u_ref, vs_ref, cos_ref, sin_ref, d_ref, xi_ref,
                        zeta_ref, gl_ref, lng_ref, lnb_ref, ws_ref, bs_ref, oret_ref, osgu_ref, state_ref):
    @pl.when(pl.program_id(1) == 0)
    def _():
        state_ref[...] = jnp.zeros_like(state_ref)

    cos = cos_ref[...]
    sin = sin_ref[...]
    half = cos.shape[1]
    dk = 2 * half

    def rot(t):
        t1, t2 = t[:, :half], t[:, half:]
        return jnp.concatenate([t1 * cos - t2 * sin, t1 * sin + t2 * cos], axis=1)

    for i in range(RET_HEADS):
        cs = slice(i * dk, (i + 1) * dk)
        q = rot(q_ref[:, cs].astype(F32))
        k = rot(k_ref[:, cs].astype(F32)) * (dk ** -0.5)
        qb = q.astype(BF16)
        vb = v_ref[:, cs].astype(BF16)
        scores = lax.dot_general(qb, k.astype(BF16), _NT, preferred_element_type=F32) * d_ref[i]
        intra = jnp.dot(scores.astype(BF16), vb, preferred_element_type=F32)
        state = state_ref[i]
        cross = jnp.dot(qb, state.astype(BF16), preferred_element_type=F32) * xi_ref[i]
        kz = (k * zeta_ref[i]).astype(BF16)
        state_ref[i] = state * gl_ref[i] + lax.dot_general(kz, vb, _TN, preferred_element_type=F32)
        oret_ref[:, cs] = (_rms(intra + cross) * _silu(g_ref[:, cs].astype(F32))).astype(BF16)
    _sgu_kernel(u_ref, vs_ref, lng_ref, lnb_ref, ws_ref, bs_ref, osgu_ref)


def _even_mixers(z, ln_g, ln_b, w_s, b_s, batch, seq):
    n = z.shape[0]
    gw = z.shape[1] // 6
    dk = gw // RET_HEADS
    blk = RET_BLOCK
    nblk = seq // blk
    cos, sin, decay, xi, zeta, g_blk = _retention_tables(seq, dk)
    zspec = lambda part: pl.BlockSpec((blk, gw), lambda b, c: (b * nblk + c, part))
    const2 = lambda b, c: (0, 0)
    const3 = lambda b, c: (0, 0, 0)
    out = pl.BlockSpec((blk, gw), lambda b, c: (b * nblk + c, 0))
    return pl.pallas_call(
        _even_mixers_kernel,
        grid=(batch, nblk),
        in_specs=[
            zspec(0), zspec(1), zspec(2), zspec(3), zspec(4), zspec(5),
            pl.BlockSpec((blk, dk // 2), lambda b, c: (c, 0)),
            pl.BlockSpec((blk, dk // 2), lambda b, c: (c, 0)),
            pl.BlockSpec(decay.shape, const3), pl.BlockSpec(xi.shape, const3),
            pl.BlockSpec(zeta.shape, const3), pl.BlockSpec(g_blk.shape, const3),
            pl.BlockSpec((1, gw), const2), pl.BlockSpec((1, gw), const2),
            pl.BlockSpec((SGU_GROUPS, SGU_WINDOW, SGU_WINDOW), const3),
            pl.BlockSpec((SGU_GROUPS, SGU_WINDOW, 1), const3),
        ],
        out_specs=[out, out],
        out_shape=[jax.ShapeDtypeStruct((n, gw), BF16)] * 2,
        scratch_shapes=[pltpu.VMEM((RET_HEADS, dk, dk), F32)],
        compiler_params=_params(("arbitrary", "arbitrary")),
        name="even_mixers",
    )(z, z, z, z, z, z, cos, sin, decay, xi, zeta, g_blk, ln_g.reshape(1, gw), ln_b.reshape(1, gw),
      w_s, b_s.reshape(SGU_GROUPS, SGU_WINDOW, 1))


def _hgrn_head(q, f_logits, v, g, lb, ng, tri, st_ref, sh_ref):
    rows, dk = q.shape
    f = lb + (1.0 - lb) * jax.nn.sigmoid(f_logits)
    lf = jnp.log(f)
    kk = 1.0 - f
    qa = _silu(q)

    bcum = None
    rest = lf
    for _ in range(3):
        term = rest.astype(BF16)
        part = jnp.dot(tri, term, preferred_element_type=F32)
        bcum = part if bcum is None else bcum + part
        rest = rest - term.astype(F32)

    row = lax.broadcasted_iota(I32, (rows, dk), 0)
    ti = lax.broadcasted_iota(I32, (rows, rows), 0)
    si = lax.broadcasted_iota(I32, (rows, rows), 1)
    attn = jnp.zeros((rows, rows), F32)
    hs = rows // 2
    while hs >= HG_FINE:
        bs = 2 * hs
        parts = [jnp.broadcast_to(bcum[b * bs + hs - 1:b * bs + hs, :], (bs, dk))
                 for b in range(rows // bs)]
        anchor = parts[0] if len(parts) == 1 else jnp.concatenate(parts, axis=0)
        upper = (row & (bs - 1)) >= hs
        fac = jnp.exp(-jnp.abs(bcum - anchor))
        qt = jnp.where(upper, qa * fac, 0.0)
        kt = jnp.where(upper, 0.0, kk * fac)
        a = lax.dot_general(qt.astype(BF16), kt.astype(BF16), _NT, preferred_element_type=F32)
        if bs < rows:
            a = jnp.where((ti & -bs) == (si & -bs), a, 0.0)
        attn = attn + a
        hs //= 2

    vb = v.astype(BF16)
    near = qa * kk
    intra = jnp.sum(near, axis=1, keepdims=True) * v
    pad = jnp.zeros((HG_FINE, dk), F32)
    for idx, val in enumerate((kk, bcum, v)):
        sh_ref[idx, 0:HG_FINE, :] = pad
        sh_ref[idx, HG_FINE:, :] = val
    for delta in range(1, HG_FINE):
        back = slice(HG_FINE - delta, HG_FINE - delta + rows)
        prod = qa * sh_ref[0, back, :] * jnp.exp(jnp.minimum(bcum - sh_ref[1, back, :], 0.0))
        prod = jnp.where((row & (HG_FINE - 1)) >= delta, prod, 0.0)
        intra = intra + jnp.sum(prod, axis=1, keepdims=True) * sh_ref[2, back, :]
    intra = intra + jnp.dot(attn.astype(BF16), vb, preferred_element_type=F32)
    st = st_ref[...]
    cross = lax.dot_general((qa * jnp.exp(bcum)).astype(BF16), st.astype(BF16), _NT,
                            preferred_element_type=F32)
    blast = bcum[rows - 1:rows, :]
    kb = (kk * jnp.exp(blast - bcum)).astype(BF16)
    st_ref[...] = st * jnp.exp(blast) + lax.dot_general(vb, kb, _TN, preferred_element_type=F32)
    return _rms(intra + cross) * ng * _silu(g)


def _hgrn_lower_bound(lb_raw, layer):
    e = jnp.exp(lb_raw - jnp.max(lb_raw, axis=0, keepdims=True))
    soft = e / jnp.sum(e, axis=0, keepdims=True)
    return jnp.sum(soft[1:layer + 1], axis=0, keepdims=True)


def _inproj_hgrn_kernel(x_ref, g_ref, w_ref, ws_ref, lb_ref, ng_ref, tri_ref, o_ref, zd_ref,
                        z_ref, st_ref, sh_ref, *, layer):
    c = pl.program_id(1)
    gw = o_ref.shape[1]
    dk = gw // HG_HEADS

    @pl.when((pl.program_id(0) == 0) & (c == 0))
    def _():
        z_ref[...] = jnp.zeros_like(z_ref)

    @pl.when(c <= 1)
    def _():
        st_ref[...] = jnp.zeros_like(st_ref)

    h = (_rms(x_ref[...]) * g_ref[...]).astype(BF16)
    z_ref[c % 2] = jnp.dot(h, w_ref[...], preferred_element_type=F32)
    zd_ref[...] = jnp.dot(h, ws_ref[...], preferred_element_type=F32)

    prev = (c + 1) % 2
    lb = _hgrn_lower_bound(lb_ref[...], layer)
    tri = tri_ref[...]
    for hd in range(HG_HEADS):
        cols = [slice(part * gw + hd * dk, part * gw + (hd + 1) * dk) for part in range(4)]
        hs = slice(hd * dk, (hd + 1) * dk)
        out = _hgrn_head(z_ref[prev, :, cols[0]], z_ref[prev, :, cols[1]], z_ref[prev, :, cols[2]],
                         z_ref[prev, :, cols[3]], lb[:, hs], ng_ref[:, hs], tri, st_ref.at[hd], sh_ref.at[hd])
        o_ref[:, hs] = out.astype(BF16)


def _inproj_hgrn(x, g, w, layer_idx, w_side, lb_raw, norm_g, batch, seq, layer):
    n, d = x.shape
    gw = norm_g.shape[0]
    dk = gw // HG_HEADS
    blk = HG_BLOCK
    nblk = seq // blk
    ns = w_side.shape[1]
    depth = lb_raw.shape[0]
    const = lambda b, c: (0, 0)
    return pl.pallas_call(
        functools.partial(_inproj_hgrn_kernel, layer=layer),
        grid=(batch, nblk + 1),
        in_specs=[
            pl.BlockSpec((blk, d), lambda b, c: (b * nblk + jnp.minimum(c, nblk - 1), 0)),
            pl.BlockSpec((1, d), const),
            pl.BlockSpec((None, d, 4 * gw), lambda b, c: (layer_idx, 0, 0), pipeline_mode=pl.Buffered(1)),
            pl.BlockSpec((d, ns), const, pipeline_mode=pl.Buffered(1)),
            pl.BlockSpec((depth, gw), const),
            pl.BlockSpec((1, gw), const),
            pl.BlockSpec((blk, blk), const),
        ],
        out_specs=[
            pl.BlockSpec((blk, gw), lambda b, c: (b * nblk + jnp.maximum(c - 1, 0), 0)),
            pl.BlockSpec((blk, ns), lambda b, c: (b * nblk + jnp.minimum(c, nblk - 1), 0)),
        ],
        out_shape=[jax.ShapeDtypeStruct((n, gw), BF16), jax.ShapeDtypeStruct((n, ns), F32)],
        scratch_shapes=[
            pltpu.VMEM((2, blk, 4 * gw), F32),
            pltpu.VMEM((HG_HEADS, dk, dk), F32),
            pltpu.VMEM((HG_HEADS, 3, blk + HG_FINE, dk), F32),
        ],
        compiler_params=_params(("arbitrary", "arbitrary"), SIDE_VMEM_LIMIT),
        name="inproj_hgrn",
    )(x, g.reshape(1, d), w, w_side, lb_raw, norm_g.reshape(1, gw), jnp.tril(jnp.ones((blk, blk), BF16)))


def _dsa_prep_kernel(zd_ref, cqg_ref, ckvg_ref, wuq_ref, qng_ref, wqit_ref,
                     q_ref, qit_ref, kv_ref, kix_ref, wht_ref):
    zd = zd_ref[...]
    cq = (_rms(zd[:, :DSA_Q_RANK]) * cqg_ref[...]).astype(BF16)
    qf = jnp.dot(cq, wuq_ref[...], preferred_element_type=F32)
    for i in range(q_ref.shape[0]):
        rs = slice(i * Q_BLOCK, (i + 1) * Q_BLOCK)
        for h in range(DSA_HEADS):
            cs = slice(h * DSA_KV_RANK, (h + 1) * DSA_KV_RANK)
            q_ref[i, h] = (_rms(qf[rs, cs]) * qng_ref[...] * (DSA_KV_RANK ** -0.5 * LOG2E)).astype(BF16)
    qit = lax.dot_general(wqit_ref[...], cq, _NT, preferred_element_type=F32)
    qit = (qit * (IDX_DIM ** -0.5)).astype(BF16)
    for i in range(q_ref.shape[0]):
        for h in range(IDX_HEADS):
            c = (i * IDX_HEADS + h) * Q_BLOCK
            qit_ref[:, c:c + Q_BLOCK] = qit[h * LANES:(h + 1) * LANES, i * Q_BLOCK:(i + 1) * Q_BLOCK]
    c0 = DSA_Q_RANK
    c1 = c0 + DSA_KV_RANK
    kv_ref[...] = (_rms(zd[:, c0:c1]) * ckvg_ref[...]).astype(BF16)
    kix_ref[...] = zd[:, c1:c1 + LANES].astype(BF16)
    wht = jnp.transpose(zd[:, c1 + LANES:c1 + 2 * LANES] * (IDX_HEADS ** -0.5))
    wht_ref[...] = wht[:IDX_HEADS, :]


def _dsa_prep(zd, cq_g, ckv_g, w_uq, qn_g, w_qit, *, tm):
    n, wd = zd.shape
    dq = w_uq.shape[1]
    dqi = w_qit.shape[0]
    full = lambda i: (0, 0)
    rows = lambda i: (i, 0)
    cols = lambda i: (0, i)
    return pl.pallas_call(
        _dsa_prep_kernel,
        grid=(n // tm,),
        in_specs=[
            pl.BlockSpec((tm, wd), rows),
            pl.BlockSpec((1, DSA_Q_RANK), full),
            pl.BlockSpec((1, DSA_KV_RANK), full),
            pl.BlockSpec((DSA_Q_RANK, dq), full),
            pl.BlockSpec((1, DSA_KV_RANK), full),
            pl.BlockSpec((dqi, DSA_Q_RANK), full),
        ],
        out_specs=[
            pl.BlockSpec((tm // Q_BLOCK, DSA_HEADS, Q_BLOCK, DSA_KV_RANK), lambda i: (i, 0, 0, 0)),
            pl.BlockSpec((LANES, IDX_HEADS * tm), cols),
            pl.BlockSpec((tm, DSA_KV_RANK), rows),
            pl.BlockSpec((tm, LANES), rows),
            pl.BlockSpec((IDX_HEADS, tm), cols),
        ],
        out_shape=[
            jax.ShapeDtypeStruct((n // Q_BLOCK, DSA_HEADS, Q_BLOCK, DSA_KV_RANK), BF16),
            jax.ShapeDtypeStruct((LANES, IDX_HEADS * n), BF16),
            jax.ShapeDtypeStruct((n, DSA_KV_RANK), BF16),
            jax.ShapeDtypeStruct((n, LANES), BF16),
            jax.ShapeDtypeStruct((IDX_HEADS, n), F32),
        ],
        compiler_params=_params(("arbitrary",)),
        name="dsa_prep",
    )(zd, cq_g.reshape(1, -1), ckv_g.reshape(1, -1), w_uq, qn_g.reshape(1, -1), w_qit)


def _bit_planes(v):
    v = list(v)

    def swap(lo, hi, j, m):
        return (lo & ~m) | ((hi >> j) & m), (hi & m) | ((lo << j) & ~m)

    for j, m in ((16, 0x0000FFFF), (8, 0x00FF00FF), (4, 0x0F0F0F0F), (2, 0x33333333), (1, 0x55555555)):
        for k in range(len(v)):
            if k & j == 0:
                v[k], v[k + j] = swap(v[k], v[k + j], j, m)
    return v


def _dsa_select_kernel(qit_ref, wht_ref, kix_ref, wg_in, wu_in, wd_in, m_ref, wg_out, wu_out, wd_out,
                       key_ref, jc_ref, plane_ref, *, ksel, idx_bits):
    for src, dst in ((wg_in, wg_out), (wu_in, wu_out), (wd_in, wd_out)):
        dst[...] = src[...].astype(BF16)

    qb = pl.program_id(1)
    ntile = qb + 1
    ntile_all = m_ref.shape[2]
    rowi = lax.broadcasted_iota(I32, (LANES, Q_BLOCK), 0)
    coli = lax.broadcasted_iota(I32, (LANES, Q_BLOCK), 1)
    q_chunk = (qb * Q_BLOCK + coli) // CHUNK
    pairs = IDX_HEADS // 2
    w_pair = [jnp.concatenate([wht_ref[2 * p:2 * p + 1, :], wht_ref[2 * p + 1:2 * p + 2, :]], axis=1)
              for p in range(pairs)]

    def tile_rows(j):
        return pl.ds(pl.multiple_of(j * LANES, LANES), LANES)

    def admissible(j):
        return ((j * LANES + rowi) // CHUNK) <= q_chunk

    def score_tile(jj, carry):
        words = []
        for t in range(2):
            j = 2 * jj + t
            kt = kix_ref[tile_rows(j), :]
            sc = None
            for p in range(pairs):
                s2 = jnp.dot(kt, qit_ref[:, 2 * p * Q_BLOCK:2 * (p + 1) * Q_BLOCK],
                             preferred_element_type=F32)
                c2 = w_pair[p] * jnp.maximum(s2, 0.0)
                c = c2[:, :Q_BLOCK] + c2[:, Q_BLOCK:]
                sc = c if sc is None else sc + c
            bits = pltpu.bitcast(sc, I32)
            key = bits ^ ((bits >> 31) & 0x7FFFFFFF)
            key = jnp.where(key == -1, 0, key)
            key = jnp.where(admissible(j), key, INT_MIN)
            key_ref[tile_rows(j), :] = key
            words += [key[8 * k:8 * (k + 1), :] for k in range(LANES // 8)]
        for i, p in enumerate(_bit_planes(words)):
            plane_ref[pl.ds(pl.multiple_of(jj * PLANE_ROWS + 8 * i, 8), 8), :] = p
        return carry

    @pl.when(qb == 0)
    def _():
        key_ref[...] = jnp.full(key_ref.shape, INT_MIN, I32)
        plane_row = lax.broadcasted_iota(I32, plane_ref.shape, 0) & (PLANE_ROWS - 1)
        plane_ref[...] = jnp.where(plane_row < 8, -1, 0)

    lax.fori_loop(0, (ntile + 1) // 2, score_tile, 0)

    def count(pred_fn):
        def body(jj, acc):
            for t in range(SCAN_TILES):
                j = SCAN_TILES * jj + t
                acc = acc + pred_fn(j, key_ref[tile_rows(j), :]).astype(I32)
            return acc
        trips = (ntile + SCAN_TILES - 1) // SCAN_TILES
        acc = lax.fori_loop(0, trips, body, jnp.zeros((LANES, Q_BLOCK), I32))
        return jnp.sum(acc, axis=0, keepdims=True)

    nword = plane_ref.shape[0] // PLANE_ROWS

    def plane(v, i):
        return plane_ref[pl.ds(pl.multiple_of(v * PLANE_ROWS + 8 * i, 8), 8), :]

    def total(words):
        acc = lax.population_count(words[0])
        for w in words[1:]:
            acc = acc + lax.population_count(w)
        return jnp.sum(acc, axis=0, keepdims=True)

    def radix_pair(t, c):
        alive, above, t_u = c
        i = 2 * t
        flip = jnp.where(t == 0, -1, 0)
        set1 = [a & (plane(v, i) ^ flip) for v, a in enumerate(alive)]
        clr1 = [a ^ s for a, s in zip(alive, set1)]
        p2 = [plane(v, i + 1) for v in range(nword)]
        set1_set2 = [s & p for s, p in zip(set1, p2)]
        clr1_set2 = [s & p for s, p in zip(clr1, p2)]
        n1, n11, n01 = total(set1), total(set1_set2), total(clr1_set2)
        take1 = (above + n1) >= ksel
        above = jnp.where(take1, above, above + n1)
        n2 = jnp.where(take1, n11, n01)
        take2 = (above + n2) >= ksel
        above = jnp.where(take2, above, above + n2)
        alive = tuple(
            jnp.where(take1, jnp.where(take2, ss, s ^ ss), jnp.where(take2, cs, c0 ^ cs))
            for s, c0, ss, cs in zip(set1, clr1, set1_set2, clr1_set2))
        t_u = (t_u | jnp.where(take1, jnp.left_shift(jnp.int32(1), 31 - i), 0)
               | jnp.where(take2, jnp.left_shift(jnp.int32(1), 30 - i), 0))
        return alive, above, t_u

    start = (tuple(jnp.full((8, Q_BLOCK), -1, I32) for _ in range(nword)),
             jnp.zeros((1, Q_BLOCK), I32), jnp.zeros((1, Q_BLOCK), I32))
    alive, above, t_u = lax.fori_loop(0, 16, radix_pair, start)
    thr = t_u ^ INT_MIN
    ties = lax.population_count(alive[0])
    for a in alive[1:]:
        ties = ties + lax.population_count(a)
    cnt_t = above + jnp.sum(ties, axis=0, keepdims=True)
    tied = jnp.max(jnp.where((cnt_t > ksel) & (thr > INT_MIN), 1.0, 0.0))

    jc_ref[...] = jnp.full(jc_ref.shape, 2 ** 31 - 1, I32)

    @pl.when(tied > 0.0)
    def _():
        need = ksel - count(lambda j, k: k > thr)

        def index_bit(i, j_c):
            cand = j_c | jnp.left_shift(jnp.int32(1), idx_bits - 1 - i)
            cnt = count(lambda j, k: (k == thr) & ((j * LANES + rowi) < cand))
            return jnp.where(cnt < need, cand, j_c)

        j_c = lax.fori_loop(0, idx_bits, index_bit, jnp.zeros((1, Q_BLOCK), I32))
        jc_ref[...] = jnp.broadcast_to(j_c, jc_ref.shape)

    j_c = jc_ref[0:1, :]
    eye = (rowi == coli).astype(BF16)

    group = KV_TILE // LANES

    floor = jnp.maximum(thr, INT_MIN + 1)

    def write_group(with_ties):
        def body(g, carry):
            for t in range(group):
                j = g * group + t
                k = key_ref[tile_rows(j), :]
                if with_ties:
                    sel = ((k > thr) | ((k == thr) & ((j * LANES + rowi) <= j_c))) & admissible(j)
                else:
                    sel = k >= floor
                sel = jnp.where(sel, 1.0, 0.0).astype(BF16)
                sel_t = lax.dot_general(eye, sel, _NT, preferred_element_type=F32)
                m_ref[0, 0, j] = ((sel_t - 1.0) * -NEG_BIG).astype(BF16)
            return carry
        return body

    ngroup = (ntile + group - 1) // group

    @pl.when(tied > 0.0)
    def _():
        lax.fori_loop(0, ngroup, write_group(True), 0)

    @pl.when(tied <= 0.0)
    def _():
        lax.fori_loop(0, ngroup, write_group(False), 0)

    def blank_tile(j, carry):
        m_ref[0, 0, j] = jnp.full((Q_BLOCK, LANES), NEG_BIG, BF16)
        return carry

    lax.fori_loop(ngroup * group, ntile_all, blank_tile, 0)


def _dsa_select(qit, wht, kix, ffn_weights, layer, batch, seq, ksel):
    nqb = seq // Q_BLOCK
    nkt = seq // LANES
    w_in, w_out, w_shape = _cast_specs([(w, layer) for w in ffn_weights], lambda b, q: b * nqb + q,
                                       batch * nqb)
    return pl.pallas_call(
        functools.partial(_dsa_select_kernel, ksel=ksel, idx_bits=int(math.log2(seq))),
        grid=(batch, nqb),
        in_specs=[
            pl.BlockSpec((LANES, IDX_HEADS * Q_BLOCK), lambda b, q: (0, b * nqb + q)),
            pl.BlockSpec((IDX_HEADS, Q_BLOCK), lambda b, q: (0, b * nqb + q)),
            pl.BlockSpec((seq, LANES), lambda b, q: (b, 0)),
            *w_in,
        ],
        out_specs=[pl.BlockSpec((1, 1, nkt, Q_BLOCK, LANES), lambda b, q: (b, q, 0, 0, 0)), *w_out],
        out_shape=[jax.ShapeDtypeStruct((batch, nqb, nkt, Q_BLOCK, LANES), BF16), *w_shape],
        scratch_shapes=[pltpu.VMEM((seq, Q_BLOCK), I32), pltpu.VMEM((8, Q_BLOCK), I32),
                        pltpu.VMEM((seq, Q_BLOCK), I32)],
        compiler_params=_params(("arbitrary", "arbitrary")),
        name="dsa_select",
    )(qit, wht, kix, *ffn_weights)


def _rel_bucket(rel):
    nb = REL_BUCKETS // 2
    max_exact = nb // 2
    ret = jnp.where(rel > 0, nb, 0)
    n = jnp.abs(rel)
    nf = jnp.maximum(n, 1).astype(F32)
    large = max_exact + (jnp.log(nf / max_exact) / math.log(REL_MAX_DIST / max_exact)
                         * (nb - max_exact)).astype(I32)
    large = jnp.minimum(large, nb - 1)
    return ret + jnp.where(n < max_exact, n, large)


NEAR_TILES = 3


def _dsa_attn_kernel(q_ref, kv_ref, mask_ref, rb_ref, wuv_ref, o_ref,
                     m_ref, l_ref, alpha_ref, acc_ref, corr_ref, s_ref, p_ref, madd_ref):
    b, qb = pl.program_id(0), pl.program_id(1)
    sub = KV_TILE // LANES
    far_bucket = REL_BUCKETS // 2 - 1
    half = DSA_HEADS * Q_BLOCK // 2

    def head_rows(h):
        return slice(h * Q_BLOCK, (h + 1) * Q_BLOCK)

    @pl.when((b == 0) & (qb == 0))
    def _():
        ti = lax.broadcasted_iota(I32, (Q_BLOCK, LANES), 0)
        si = lax.broadcasted_iota(I32, (Q_BLOCK, LANES), 1)
        for oi in range(NEAR_TILES):
            bucket = _rel_bucket((oi - (NEAR_TILES - 1)) * LANES + si - ti)
            for h in range(DSA_HEADS):
                tbl = jnp.zeros((Q_BLOCK, LANES), F32)
                for bk in range(REL_BUCKETS):
                    tbl = jnp.where(bucket == bk, rb_ref[bk, h], tbl)
                corr_ref[oi, head_rows(h), :] = (tbl - rb_ref[far_bucket, h]) * LOG2E

    m_ref[...] = jnp.full_like(m_ref, NEG_BIG)
    l_ref[...] = jnp.zeros_like(l_ref)
    acc_ref[...] = jnp.zeros_like(acc_ref)
    q_all = q_ref[0].reshape(DSA_HEADS * Q_BLOCK, DSA_KV_RANK)

    def key_step(kt, carry):
        kvt = kv_ref[pl.ds(pl.multiple_of(kt * KV_TILE, KV_TILE), KV_TILE), :]
        for part in range(2):
            rs = slice(part * half, (part + 1) * half)
            s_ref[rs, :] = lax.dot_general(q_all[rs], kvt, _NT, preferred_element_type=F32)
        for j in range(sub):
            d = kt * sub + j - qb

            @pl.when((d > -NEAR_TILES) & (d <= 0))
            def _(j=j, d=d):
                s_ref[:, j * LANES:(j + 1) * LANES] += corr_ref[d + NEAR_TILES - 1]

        for j in range(sub):
            madd_ref[:, j * LANES:(j + 1) * LANES] = mask_ref[0, 0, kt * sub + j].astype(F32)
        groups_per_head = Q_BLOCK // SM_ROWS
        for g in range(DSA_HEADS * groups_per_head):
            rs = slice(g * SM_ROWS, (g + 1) * SM_ROWS)
            qg = g % groups_per_head
            s = s_ref[rs, :] + madd_ref[qg * SM_ROWS:(qg + 1) * SM_ROWS, :]
            m_old = m_ref[rs, :]
            m_new = jnp.maximum(m_old, jnp.max(s, axis=1, keepdims=True))
            alpha = jnp.exp2(m_old - m_new)
            p = jnp.exp2(s - jnp.tile(m_new, (1, sub)))
            l_ref[rs, :] = alpha * l_ref[rs, :] + jnp.sum(p, axis=1, keepdims=True)
            alpha_ref[rs, :] = alpha
            p_ref[rs, :] = p.astype(BF16)
            m_ref[rs, :] = m_new
        for part in range(2):
            rs = slice(part * half, (part + 1) * half)
            pv = jnp.dot(p_ref[rs, :], kvt, preferred_element_type=F32)
            acc_ref[rs, :] = jnp.tile(alpha_ref[rs, :], (1, DSA_KV_RANK // LANES)) * acc_ref[rs, :] + pv
        return carry

    lax.fori_loop(0, qb // sub + 1, key_step, 0)

    dv = wuv_ref.shape[2]
    for h in range(DSA_HEADS):
        rs = head_rows(h)
        o = (acc_ref[rs, :] / jnp.tile(l_ref[rs, :], (1, DSA_KV_RANK // LANES))).astype(BF16)
        o_ref[:, h * dv:(h + 1) * dv] = jnp.dot(o, wuv_ref[h], preferred_element_type=F32).astype(BF16)


def _dsa_attn(q, kv, mask, rel_bias, w_uv, batch, seq):
    n = kv.shape[0]
    nqb = seq // Q_BLOCK
    dv = w_uv.shape[2]
    rows = DSA_HEADS * Q_BLOCK
    return pl.pallas_call(
        _dsa_attn_kernel,
        grid=(batch, nqb),
        in_specs=[
            pl.BlockSpec((1, DSA_HEADS, Q_BLOCK, DSA_KV_RANK), lambda b, qb: (b * nqb + qb, 0, 0, 0)),
            pl.BlockSpec((seq, DSA_KV_RANK), lambda b, qb: (b, 0)),
            pl.BlockSpec((1, 1) + mask.shape[2:], lambda b, qb: (b, qb, 0, 0, 0)),
            pl.BlockSpec(memory_space=pltpu.SMEM),
            pl.BlockSpec(w_uv.shape, lambda b, qb: (0, 0, 0)),
        ],
        out_specs=pl.BlockSpec((Q_BLOCK, DSA_HEADS * dv), lambda b, qb: (b * nqb + qb, 0)),
        out_shape=jax.ShapeDtypeStruct((n, DSA_HEADS * dv), BF16),
        scratch_shapes=[
            pltpu.VMEM((rows, LANES), F32),
            pltpu.VMEM((rows, LANES), F32),
            pltpu.VMEM((rows, LANES), F32),
            pltpu.VMEM((rows, DSA_KV_RANK), F32),
            pltpu.VMEM((NEAR_TILES, rows, LANES), F32),
            pltpu.VMEM((rows, KV_TILE), F32),
            pltpu.VMEM((rows, KV_TILE), BF16),
            pltpu.VMEM((Q_BLOCK, KV_TILE), F32),
        ],
        compiler_params=_params(("arbitrary", "arbitrary")),
        name="dsa_attn",
    )(q, kv, mask, rel_bias, w_uv)


def _pad_cols(w, width):
    return jnp.pad(w, ((0, 0), (0, width - w.shape[1])))


def kernel(x, ln_mix_g, ln_ffn_g, w_ffn_gate, w_ffn_up, w_ffn_down, rel_bias, ev_w_in, ev_w_out, sgu_ln_g, sgu_ln_b, sgu_w_s, sgu_b_s, od_w_in, od_w_out, hgrn_lb, hgrn_norm_g, dsa_cq_g, dsa_ckv_g, dsa_w_uq, dsa_qnorm_g, dsa_w_qidx, dsa_w_uv):
    batch, seq, d = x.shape
    n = batch * seq
    depth = ln_mix_g.shape[0]
    ksel = min(TOPK_MAX, seq // 4)
    tm = min(PROJ_ROWS, n)
    xf = x.reshape(n, d)
    ffn_f32 = (w_ffn_gate, w_ffn_up, w_ffn_down)
    ev_in_all, od_in_all = ev_w_in.astype(BF16), od_w_in.astype(BF16)
    for layer in range(depth):
        j = layer // 2
        if layer % 2 == 0:
            z, *ffn_w = _norm_matmul(xf, ln_mix_g[layer], ev_in_all, j, ev_in_all.shape[2], tm=tm,
                                     tn=PROJ_COLS, out_dtype=BF16, cast=[(w, layer) for w in ffn_f32])
            a1, a2 = _even_mixers(z, sgu_ln_g[j], sgu_ln_b[j], sgu_w_s[j], sgu_b_s[j], batch, seq)
            w_out = ev_w_out[j]
        else:
            w_in = od_w_in[j]
            gw = d // 2
            c = 4 * gw
            c_kidx = c + DSA_Q_RANK + DSA_KV_RANK
            w_dsa = jnp.concatenate([
                w_in[:, c:c_kidx],
                _pad_cols(w_in[:, c_kidx:c_kidx + IDX_DIM], LANES),
                _pad_cols(w_in[:, c_kidx + IDX_DIM:], LANES),
            ], axis=1).astype(BF16)
            a1, zd = _inproj_hgrn(xf, ln_mix_g[layer], od_in_all, j, w_dsa, hgrn_lb, hgrn_norm_g[j],
                                  batch, seq, layer)
            w_qit = jnp.pad(dsa_w_qidx[j].T.reshape(IDX_HEADS, IDX_DIM, DSA_Q_RANK),
                            ((0, 0), (0, LANES - IDX_DIM), (0, 0))).reshape(IDX_HEADS * LANES, DSA_Q_RANK)
            q, qit, kv, kix, wht = _dsa_prep(zd, dsa_cq_g[j], dsa_ckv_g[j], dsa_w_uq[j].astype(BF16),
                                             dsa_qnorm_g[j], w_qit.astype(BF16), tm=min(PREP_ROWS, n))
            mask, *ffn_w = _dsa_select(qit, wht, kix, ffn_f32, layer, batch, seq, ksel)
            a2 = _dsa_attn(q, kv, mask, rel_bias, dsa_w_uv[j].astype(BF16), batch, seq)
            w_out = od_w_out[j]
        xf = _outproj(a1, a2, w_out.astype(BF16), xf, tm=min(OUT_ROWS, n), tn=OUT_COLS)
        xf = _ffn(xf, ln_ffn_g[layer], *(w[None] for w in ffn_w), 0, tm=min(FFN_ROWS, n), tf=FFN_COLS)
    return xf.reshape(batch, seq, d)
```

```python
import functools
import math

import jax
import jax.numpy as jnp
from jax import lax
from jax.experimental import pallas as pl
from jax.experimental.pallas import tpu as pltpu

F32 = jnp.float32
BF16 = jnp.bfloat16
I32 = jnp.int32

EPS = 1e-6
CHUNK = 64
LANES = 128
ROPE_BASE = 10000.0
RET_HEADS = 4
SGU_WINDOW = 128
SGU_GROUPS = 4
HG_HEADS = 8
DSA_HEADS = 8
DSA_Q_RANK = 384
DSA_KV_RANK = 256
IDX_HEADS = 16
IDX_DIM = 64
TOPK_MAX = 256
Q_BLOCK = 128
KV_TILE = 512
SM_ROWS = 64
LOG2E = math.log2(math.e)
REL_BUCKETS = 32
REL_MAX_DIST = 256
NEG_BIG = -1e30
INT_MIN = -(2 ** 31)
SCAN_TILES = 2
PLANE_ROWS = 256

RET_BLOCK = 256
HG_BLOCK = 256
HG_FINE = 4
VMEM_LIMIT = 48 * 1024 * 1024
PROJ_ROWS = 1024
PROJ_COLS = 1024
OUT_ROWS = 512
OUT_COLS = 2048
FFN_ROWS = 1024
FFN_VMEM_LIMIT = 58 * 1024 * 1024
SIDE_VMEM_LIMIT = 56 * 1024 * 1024
FFN_COLS = 512
PREP_ROWS = 512

_NT = (((1,), (1,)), ((), ()))
_TN = (((0,), (0,)), ((), ()))


def _params(semantics, vmem_limit=VMEM_LIMIT):
    return pltpu.CompilerParams(dimension_semantics=semantics, vmem_limit_bytes=vmem_limit)


def _silu(x):
    return x * jax.nn.sigmoid(x)


def _rms(x):
    return x * lax.rsqrt(jnp.mean(x * x, axis=-1, keepdims=True) + EPS)


def _cast_blocks(rows, steps):
    units = rows // 16
    blocks = max(k for k in range(1, min(units, steps) + 1) if units % k == 0)
    return rows // blocks, blocks


def _cast_specs(weights, step_of, steps):
    ins, outs, shapes = [], [], []
    for w, layer in weights:
        rows, blocks = _cast_blocks(w.shape[1], steps)

        def index(*ids, blocks=blocks):
            return (jnp.minimum(step_of(*ids), blocks - 1), 0)

        ins.append(pl.BlockSpec((None, rows, w.shape[2]), lambda *ids, layer=layer, index=index: (layer,) + index(*ids)))
        outs.append(pl.BlockSpec((rows, w.shape[2]), index))
        shapes.append(jax.ShapeDtypeStruct(w.shape[1:], BF16))
    return ins, outs, shapes


def _norm_matmul_kernel(x_ref, g_ref, w_ref, *rest):
    ncast = (len(rest) - 2) // 2
    o_ref, h_ref = rest[ncast], rest[-1]
    for src, dst in zip(rest[:ncast], rest[ncast + 1:-1]):
        dst[...] = src[...].astype(BF16)

    @pl.when(pl.program_id(1) == 0)
    def _():
        h_ref[...] = (_rms(x_ref[...]) * g_ref[...]).astype(BF16)

    o_ref[...] = jnp.dot(h_ref[...], w_ref[...], preferred_element_type=F32).astype(o_ref.dtype)


def _norm_matmul(x, g, w, layer, nout, *, tm, tn, out_dtype=F32, cast=()):
    n, d = x.shape
    ncols = nout // tn
    c_in, c_out, c_shape = _cast_specs(cast, lambda i, j: i * ncols + j, (n // tm) * ncols)
    return pl.pallas_call(
        _norm_matmul_kernel,
        grid=(n // tm, ncols),
        in_specs=[
            pl.BlockSpec((tm, d), lambda i, j: (i, 0)),
            pl.BlockSpec((1, d), lambda i, j: (0, 0)),
            pl.BlockSpec((None, d, tn), lambda i, j: (layer, 0, j)),
            *c_in,
        ],
        out_specs=[pl.BlockSpec((tm, tn), lambda i, j: (i, j)), *c_out],
        out_shape=[jax.ShapeDtypeStruct((n, nout), out_dtype), *c_shape],
        scratch_shapes=[pltpu.VMEM((tm, d), BF16)],
        compiler_params=_params(("arbitrary", "arbitrary"), SIDE_VMEM_LIMIT),
        name="norm_matmul",
    )(x, g.reshape(1, d), w, *(w_c for w_c, _ in cast))


def _outproj_kernel(a1_ref, a2_ref, w1_ref, w2_ref, r_ref, o_ref):
    acc = jnp.dot(a1_ref[...], w1_ref[...], preferred_element_type=F32)
    acc += jnp.dot(a2_ref[...], w2_ref[...], preferred_element_type=F32)
    o_ref[...] = r_ref[...] + acc


def _outproj(a1, a2, w, res, *, tm, tn):
    n, half = a1.shape
    d = w.shape[1]
    return pl.pallas_call(
        _outproj_kernel,
        grid=(n // tm, d // tn),
        in_specs=[
            pl.BlockSpec((tm, half), lambda i, j: (i, 0)),
            pl.BlockSpec((tm, half), lambda i, j: (i, 0)),
            pl.BlockSpec((half, tn), lambda i, j: (0, j)),
            pl.BlockSpec((half, tn), lambda i, j: (1, j)),
            pl.BlockSpec((tm, tn), lambda i, j: (i, j)),
        ],
        out_specs=pl.BlockSpec((tm, tn), lambda i, j: (i, j)),
        out_shape=jax.ShapeDtypeStruct((n, d), F32),
        compiler_params=_params(("arbitrary", "arbitrary")),
        name="outproj",
    )(a1, a2, w, w, res)


def _ffn_kernel(x_ref, g_ref, wg_ref, wu_ref, wd_ref, o_ref, h_ref):
    @pl.when(pl.program_id(1) == 0)
    def _():
        x = x_ref[...]
        h_ref[...] = (_rms(x) * g_ref[...]).astype(BF16)
        o_ref[...] = x

    h = h_ref[...]
    a = jnp.dot(h, wg_ref[...], preferred_element_type=F32)
    u = jnp.dot(h, wu_ref[...], preferred_element_type=F32)
    act = (_silu(a) * u).astype(BF16)
    o_ref[...] += jnp.dot(act, wd_ref[...], preferred_element_type=F32)


def _ffn(x, g, wg, wu, wd, layer, *, tm, tf):
    n, d = x.shape
    dff = wg.shape[2]
    return pl.pallas_call(
        _ffn_kernel,
        grid=(n // tm, dff // tf),
        in_specs=[
            pl.BlockSpec((tm, d), lambda i, f: (i, 0)),
            pl.BlockSpec((1, d), lambda i, f: (0, 0)),
            pl.BlockSpec((None, d, tf), lambda i, f: (layer, 0, f)),
            pl.BlockSpec((None, d, tf), lambda i, f: (layer, 0, f)),
            pl.BlockSpec((None, tf, d), lambda i, f: (layer, f, 0)),
        ],
        out_specs=pl.BlockSpec((tm, d), lambda i, f: (i, 0)),
        out_shape=jax.ShapeDtypeStruct((n, d), F32),
        scratch_shapes=[pltpu.VMEM((tm, d), BF16)],
        compiler_params=_params(("arbitrary", "arbitrary"), FFN_VMEM_LIMIT),
        name="ffn",
    )(x, g.reshape(1, d), wg, wu, wd)


def _retention_tables(seq, dk):
    blk = RET_BLOCK
    pos = jnp.arange(seq, dtype=F32)
    inv = ROPE_BASE ** (-jnp.arange(0, dk, 2, dtype=F32) / dk)
    ang = pos[:, None] * inv[None, :]
    log_gamma = jnp.log(1.0 - 2.0 ** (-5.0 - jnp.arange(RET_HEADS, dtype=F32)))
    i = jnp.arange(blk)
    same = (i[:, None] // CHUNK) == (i[None, :] // CHUNK)
    earlier = (i[None, :] // CHUNK) < (i[:, None] // CHUNK)
    diff = (i[:, None] - i[None, :]).astype(F32)
    dist = jnp.where(same, jnp.abs(diff), diff)
    decay = jnp.where((same | earlier)[None], jnp.exp(log_gamma[:, None, None] * dist[None]), 0.0)
    p = jnp.arange(blk, dtype=F32)
    wide = (RET_HEADS, blk, dk)
    xi = jnp.broadcast_to(jnp.exp(log_gamma[:, None] * (p + 1.0))[:, :, None], wide)
    zeta = jnp.broadcast_to(jnp.exp(log_gamma[:, None] * (blk - 1.0 - p))[:, :, None], wide)
    g_blk = jnp.broadcast_to(jnp.exp(log_gamma * blk)[:, None, None], (RET_HEADS, 1, dk))
    return jnp.cos(ang), jnp.sin(ang), decay, xi, zeta, g_blk


def _gelu(x):
    return 0.5 * x * (1.0 + lax.erf(x * math.sqrt(0.5)))


def _sgu_kernel(u_ref, v_ref, lng_ref, lnb_ref, w_ref, b_ref, o_ref):
    rows, width = v_ref.shape
    dg = width // SGU_GROUPS
    v = _gelu(v_ref[...].astype(F32))
    mu = jnp.mean(v, axis=-1, keepdims=True)
    var = jnp.mean(jnp.square(v - mu), axis=-1, keepdims=True)
    vn = ((v - mu) * lax.rsqrt(var + EPS) * lng_ref[...] + lnb_ref[...]).astype(BF16)
    u = _gelu(u_ref[...].astype(F32))
    ri = lax.broadcasted_iota(I32, (SGU_WINDOW, SGU_WINDOW), 0) // CHUNK
    ci = lax.broadcasted_iota(I32, (SGU_WINDOW, SGU_WINDOW), 1) // CHUNK
    allowed = ci <= ri
    for g in range(SGU_GROUPS):
        wg = jnp.where(allowed, w_ref[g], 0.0).astype(BF16)
        bias = b_ref[g]
        for w in range(rows // SGU_WINDOW):
            rs = slice(w * SGU_WINDOW, (w + 1) * SGU_WINDOW)
            cs = slice(g * dg, (g + 1) * dg)
            mixed = jnp.dot(wg, vn[rs, cs], preferred_element_type=F32) + bias
            o_ref[rs, cs] = (u[rs, cs] * mixed).astype(BF16)


def _even_mixers_kernel(q_ref, k_ref, v_ref, g_ref, u_ref, vs_ref, cos_ref, sin_ref, d_ref, xi_ref,
                        zeta_ref, gl_ref, lng_ref, lnb_ref, ws_ref, bs_ref, oret_ref, osgu_ref, state_ref):
    @pl.when(pl.program_id(1) == 0)
    def _():
        state_ref[...] = jnp.zeros_like(state_ref)

    cos = cos_ref[...]
    sin = sin_ref[...]
    half = cos.shape[1]
    dk = 2 * half

    def rot(t):
        t1, t2 = t[:, :half], t[:, half:]
        return jnp.concatenate([t1 * cos - t2 * sin, t1 * sin + t2 * cos], axis=1)

    for i in range(RET_HEADS):
        cs = slice(i * dk, (i + 1) * dk)
        q = rot(q_ref[:, cs].astype(F32))
        k = rot(k_ref[:, cs].astype(F32)) * (dk ** -0.5)
        qb = q.astype(BF16)
        vb = v_ref[:, cs].astype(BF16)
        scores = lax.dot_general(qb, k.astype(BF16), _NT, preferred_element_type=F32) * d_ref[i]
        intra = jnp.dot(scores.astype(BF16), vb, preferred_element_type=F32)
        state = state_ref[i]
        cross = jnp.dot(qb, state.astype(BF16), preferred_element_type=F32) * xi_ref[i]
        kz = (k * zeta_ref[i]).astype(BF16)
        state_ref[i] = state * gl_ref[i] + lax.dot_general(kz, vb, _TN, preferred_element_type=F32)
        oret_ref[:, cs] = (_rms(intra + cross) * _silu(g_ref[:, cs].astype(F32))).astype(BF16)
    _sgu_kernel(u_ref, vs_ref, lng_ref, lnb_ref, ws_ref, bs_ref, osgu_ref)


def _even_mixers(z, ln_g, ln_b, w_s, b_s, batch, seq):
    n = z.shape[0]
    gw = z.shape[1] // 6
    dk = gw // RET_HEADS
    blk = RET_BLOCK
    nblk = seq // blk
    cos, sin, decay, xi, zeta, g_blk = _retention_tables(seq, dk)
    zspec = lambda part: pl.BlockSpec((blk, gw), lambda b, c: (b * nblk + c, part))
    const2 = lambda b, c: (0, 0)
    const3 = lambda b, c: (0, 0, 0)
    out = pl.BlockSpec((blk, gw), lambda b, c: (b * nblk + c, 0))
    return pl.pallas_call(
        _even_mixers_kernel,
        grid=(batch, nblk),
        in_specs=[
            zspec(0), zspec(1), zspec(2), zspec(3), zspec(4), zspec(5),
            pl.BlockSpec((blk, dk // 2), lambda b, c: (c, 0)),
            pl.BlockSpec((blk, dk // 2), lambda b, c: (c, 0)),
            pl.BlockSpec(decay.shape, const3), pl.BlockSpec(xi.shape, const3),
            pl.BlockSpec(zeta.shape, const3), pl.BlockSpec(g_blk.shape, const3),
            pl.BlockSpec((1, gw), const2), pl.BlockSpec((1, gw), const2),
            pl.BlockSpec((SGU_GROUPS, SGU_WINDOW, SGU_WINDOW), const3),
            pl.BlockSpec((SGU_GROUPS, SGU_WINDOW, 1), const3),
        ],
        out_specs=[out, out],
        out_shape=[jax.ShapeDtypeStruct((n, gw), BF16)] * 2,
        scratch_shapes=[pltpu.VMEM((RET_HEADS, dk, dk), F32)],
        compiler_params=_params(("arbitrary", "arbitrary")),
        name="even_mixers",
    )(z, z, z, z, z, z, cos, sin, decay, xi, zeta, g_blk, ln_g.reshape(1, gw), ln_b.reshape(1, gw),
      w_s, b_s.reshape(SGU_GROUPS, SGU_WINDOW, 1))


def _hgrn_head(q, f_logits, v, g, lb, ng, tri, st_ref, sh_ref):
    rows, dk = q.shape
    f = lb + (1.0 - lb) * jax.nn.sigmoid(f_logits)
    lf = jnp.log(f)
    kk = 1.0 - f
    qa = _silu(q)

    bcum = None
    rest = lf
    for _ in range(3):
        term = rest.astype(BF16)
        part = jnp.dot(tri, term, preferred_element_type=F32)
        bcum = part if bcum is None else bcum + part
        rest = rest - term.astype(F32)

    row = lax.broadcasted_iota(I32, (rows, dk), 0)
    ti = lax.broadcasted_iota(I32, (rows, rows), 0)
    si = lax.broadcasted_iota(I32, (rows, rows), 1)
    attn = jnp.zeros((rows, rows), F32)
    hs = rows // 2
    while hs >= HG_FINE:
        bs = 2 * hs
        parts = [jnp.broadcast_to(bcum[b * bs + hs - 1:b * bs + hs, :], (bs, dk))
                 for b in range(rows // bs)]
        anchor = parts[0] if len(parts) == 1 else jnp.concatenate(parts, axis=0)
        upper = (row & (bs - 1)) >= hs
        fac = jnp.exp(-jnp.abs(bcum - anchor))
        qt = jnp.where(upper, qa * fac, 0.0)
        kt = jnp.where(upper, 0.0, kk * fac)
        a = lax.dot_general(qt.astype(BF16), kt.astype(BF16), _NT, preferred_element_type=F32)
        if bs < rows:
            a = jnp.where((ti & -bs) == (si & -bs), a, 0.0)
        attn = attn + a
        hs //= 2

    vb = v.astype(BF16)
    near = qa * kk
    intra = jnp.sum(near, axis=1, keepdims=True) * v
    pad = jnp.zeros((HG_FINE, dk), F32)
    for idx, val in enumerate((kk, bcum, v)):
        sh_ref[idx, 0:HG_FINE, :] = pad
        sh_ref[idx, HG_FINE:, :] = val
    for delta in range(1, HG_FINE):
        back = slice(HG_FINE - delta, HG_FINE - delta + rows)
        prod = qa * sh_ref[0, back, :] * jnp.exp(jnp.minimum(bcum - sh_ref[1, back, :], 0.0))
        prod = jnp.where((row & (HG_FINE - 1)) >= delta, prod, 0.0)
        intra = intra + jnp.sum(prod, axis=1, keepdims=True) * sh_ref[2, back, :]
    intra = intra + jnp.dot(attn.astype(BF16), vb, preferred_element_type=F32)
    st = st_ref[...]
    cross = lax.dot_general((qa * jnp.exp(bcum)).astype(BF16), st.astype(BF16), _NT,
                            preferred_element_type=F32)
    blast = bcum[rows - 1:rows, :]
    kb = (kk * jnp.exp(blast - bcum)).astype(BF16)
    st_ref[...] = st * jnp.exp(blast) + lax.dot_general(vb, kb, _TN, preferred_element_type=F32)
    return _rms(intra + cross) * ng * _silu(g)


def _hgrn_lower_bound(lb_raw, layer):
    e = jnp.exp(lb_raw - jnp.max(lb_raw, axis=0, keepdims=True))
    soft = e / jnp.sum(e, axis=0, keepdims=True)
    return jnp.sum(soft[1:layer + 1], axis=0, keepdims=True)


def _inproj_hgrn_kernel(x_ref, g_ref, w_ref, ws_ref, lb_ref, ng_ref, tri_ref, o_ref, zd_ref,
                        z_ref, st_ref, sh_ref, *, layer):
    c = pl.program_id(1)
    gw = o_ref.shape[1]
    dk = gw // HG_HEADS

    last = pl.num_programs(1) - 1

    @pl.when(c == 1)
    def _():
        st_ref[...] = jnp.zeros_like(st_ref)

    def project():
        h = (_rms(x_ref[...]) * g_ref[...]).astype(BF16)
        z_ref[c % 2] = jnp.dot(h, w_ref[...], preferred_element_type=F32)
        zd_ref[...] = jnp.dot(h, ws_ref[...], preferred_element_type=F32)

    def recur():
        prev = (c + 1) % 2
        lb = _hgrn_lower_bound(lb_ref[...], layer)
        tri = tri_ref[...]
        for hd in range(HG_HEADS):
            cols = [slice(part * gw + hd * dk, part * gw + (hd + 1) * dk) for part in range(4)]
            hs = slice(hd * dk, (hd + 1) * dk)
            out = _hgrn_head(z_ref[prev, :, cols[0]], z_ref[prev, :, cols[1]], z_ref[prev, :, cols[2]],
                             z_ref[prev, :, cols[3]], lb[:, hs], ng_ref[:, hs], tri, st_ref.at[hd],
                             sh_ref.at[hd])
            o_ref[:, hs] = out.astype(BF16)

    @pl.when(c == 0)
    def _():
        project()

    @pl.when((c > 0) & (c < last))
    def _():
        project()
        recur()

    @pl.when(c == last)
    def _():
        recur()


def _inproj_hgrn(x, g, w, layer_idx, w_side, lb_raw, norm_g, batch, seq, layer):
    n, d = x.shape
    gw = norm_g.shape[0]
    dk = gw // HG_HEADS
    blk = HG_BLOCK
    nblk = seq // blk
    ns = w_side.shape[1]
    depth = lb_raw.shape[0]
    const = lambda b, c: (0, 0)
    return pl.pallas_call(
        functools.partial(_inproj_hgrn_kernel, layer=layer),
        grid=(batch, nblk + 1),
        in_specs=[
            pl.BlockSpec((blk, d), lambda b, c: (b * nblk + jnp.minimum(c, nblk - 1), 0)),
            pl.BlockSpec((1, d), const),
            pl.BlockSpec((None, d, 4 * gw), lambda b, c: (layer_idx, 0, 0), pipeline_mode=pl.Buffered(1)),
            pl.BlockSpec((d, ns), const, pipeline_mode=pl.Buffered(1)),
            pl.BlockSpec((depth, gw), const),
            pl.BlockSpec((1, gw), const),
            pl.BlockSpec((blk, blk), const),
        ],
        out_specs=[
            pl.BlockSpec((blk, gw), lambda b, c: (b * nblk + jnp.maximum(c - 1, 0), 0)),
            pl.BlockSpec((blk, ns), lambda b, c: (b * nblk + jnp.minimum(c, nblk - 1), 0)),
        ],
        out_shape=[jax.ShapeDtypeStruct((n, gw), BF16), jax.ShapeDtypeStruct((n, ns), F32)],
        scratch_shapes=[
            pltpu.VMEM((2, blk, 4 * gw), F32),
            pltpu.VMEM((HG_HEADS, dk, dk), F32),
            pltpu.VMEM((HG_HEADS, 3, blk + HG_FINE, dk), F32),
        ],
        compiler_params=_params(("arbitrary", "arbitrary"), SIDE_VMEM_LIMIT),
        name="inproj_hgrn",
    )(x, g.reshape(1, d), w, w_side, lb_raw, norm_g.reshape(1, gw), jnp.tril(jnp.ones((blk, blk), BF16)))


def _dsa_prep_kernel(zd_ref, cqg_ref, ckvg_ref, wuq_ref, qng_ref, wqit_ref,
                     q_ref, qit_ref, kv_ref, kix_ref, wht_ref):
    zd = zd_ref[...]
    cq = (_rms(zd[:, :DSA_Q_RANK]) * cqg_ref[...]).astype(BF16)
    qf = jnp.dot(cq, wuq_ref[...], preferred_element_type=F32)
    for i in range(q_ref.shape[0]):
        rs = slice(i * Q_BLOCK, (i + 1) * Q_BLOCK)
        for h in range(DSA_HEADS):
            cs = slice(h * DSA_KV_RANK, (h + 1) * DSA_KV_RANK)
            q_ref[i, h] = (_rms(qf[rs, cs]) * qng_ref[...] * (DSA_KV_RANK ** -0.5 * LOG2E)).astype(BF16)
    qit = lax.dot_general(wqit_ref[...], cq, _NT, preferred_element_type=F32)
    qit = (qit * (IDX_DIM ** -0.5)).astype(BF16)
    for i in range(q_ref.shape[0]):
        for h in range(IDX_HEADS):
            c = (i * IDX_HEADS + h) * Q_BLOCK
            qit_ref[:, c:c + Q_BLOCK] = qit[h * LANES:(h + 1) * LANES, i * Q_BLOCK:(i + 1) * Q_BLOCK]
    c0 = DSA_Q_RANK
    c1 = c0 + DSA_KV_RANK
    kv_ref[...] = (_rms(zd[:, c0:c1]) * ckvg_ref[...]).astype(BF16)
    kix_ref[...] = zd[:, c1:c1 + LANES].astype(BF16)
    wht = jnp.transpose(zd[:, c1 + LANES:c1 + 2 * LANES] * (IDX_HEADS ** -0.5))
    wht_ref[...] = wht[:IDX_HEADS, :]


def _dsa_prep(zd, cq_g, ckv_g, w_uq, qn_g, w_qit, *, tm):
    n, wd = zd.shape
    dq = w_uq.shape[1]
    dqi = w_qit.shape[0]
    full = lambda i: (0, 0)
    rows = lambda i: (i, 0)
    cols = lambda i: (0, i)
    return pl.pallas_call(
        _dsa_prep_kernel,
        grid=(n // tm,),
        in_specs=[
            pl.BlockSpec((tm, wd), rows),
            pl.BlockSpec((1, DSA_Q_RANK), full),
            pl.BlockSpec((1, DSA_KV_RANK), full),
            pl.BlockSpec((DSA_Q_RANK, dq), full),
            pl.BlockSpec((1, DSA_KV_RANK), full),
            pl.BlockSpec((dqi, DSA_Q_RANK), full),
        ],
        out_specs=[
            pl.BlockSpec((tm // Q_BLOCK, DSA_HEADS, Q_BLOCK, DSA_KV_RANK), lambda i: (i, 0, 0, 0)),
            pl.BlockSpec((LANES, IDX_HEADS * tm), cols),
            pl.BlockSpec((tm, DSA_KV_RANK), rows),
            pl.BlockSpec((tm, LANES), rows),
            pl.BlockSpec((IDX_HEADS, tm), cols),
        ],
        out_shape=[
            jax.ShapeDtypeStruct((n // Q_BLOCK, DSA_HEADS, Q_BLOCK, DSA_KV_RANK), BF16),
            jax.ShapeDtypeStruct((LANES, IDX_HEADS * n), BF16),
            jax.ShapeDtypeStruct((n, DSA_KV_RANK), BF16),
            jax.ShapeDtypeStruct((n, LANES), BF16),
            jax.ShapeDtypeStruct((IDX_HEADS, n), F32),
        ],
        compiler_params=_params(("arbitrary",)),
        name="dsa_prep",
    )(zd, cq_g.reshape(1, -1), ckv_g.reshape(1, -1), w_uq, qn_g.reshape(1, -1), w_qit)


def _bit_planes(v):
    v = list(v)

    def swap(lo, hi, j, m):
        return (lo & ~m) | ((hi >> j) & m), (hi & m) | ((lo << j) & ~m)

    for j, m in ((16, 0x0000FFFF), (8, 0x00FF00FF), (4, 0x0F0F0F0F), (2, 0x33333333), (1, 0x55555555)):
        for k in range(len(v)):
            if k & j == 0:
                v[k], v[k + j] = swap(v[k], v[k + j], j, m)
    return v


def _dsa_select_kernel(qit_ref, wht_ref, kix_ref, wg_in, wu_in, wd_in, m_ref, wg_out, wu_out, wd_out,
                       key_ref, jc_ref, plane_ref, *, ksel, idx_bits):
    for src, dst in ((wg_in, wg_out), (wu_in, wu_out), (wd_in, wd_out)):
        dst[...] = src[...].astype(BF16)

    qb = pl.program_id(1)
    ntile = qb + 1
    ntile_all = m_ref.shape[2]
    rowi = lax.broadcasted_iota(I32, (LANES, Q_BLOCK), 0)
    coli = lax.broadcasted_iota(I32, (LANES, Q_BLOCK), 1)
    q_chunk = (qb * Q_BLOCK + coli) // CHUNK
    pairs = IDX_HEADS // 2
    w_pair = [jnp.concatenate([wht_ref[2 * p:2 * p + 1, :], wht_ref[2 * p + 1:2 * p + 2, :]], axis=1)
              for p in range(pairs)]

    def tile_rows(j):
        return pl.ds(pl.multiple_of(j * LANES, LANES), LANES)

    def admissible(j):
        return ((j * LANES + rowi) // CHUNK) <= q_chunk

    def score_tile(jj, carry):
        words = []
        for t in range(2):
            j = 2 * jj + t
            kt = kix_ref[tile_rows(j), :]
            sc = None
            for p in range(pairs):
                s2 = jnp.dot(kt, qit_ref[:, 2 * p * Q_BLOCK:2 * (p + 1) * Q_BLOCK],
                             preferred_element_type=F32)
                c2 = w_pair[p] * jnp.maximum(s2, 0.0)
                c = c2[:, :Q_BLOCK] + c2[:, Q_BLOCK:]
                sc = c if sc is None else sc + c
            bits = pltpu.bitcast(sc, I32)
            key = bits ^ ((bits >> 31) & 0x7FFFFFFF)
            key = jnp.where(key == -1, 0, key)
            key = jnp.where(admissible(j), key, INT_MIN)
            key_ref[tile_rows(j), :] = key
            words += [key[8 * k:8 * (k + 1), :] for k in range(LANES // 8)]
        for i, p in enumerate(_bit_planes(words)):
            plane_ref[pl.ds(pl.multiple_of(jj * PLANE_ROWS + 8 * i, 8), 8), :] = p
        return carry

    @pl.when(qb == 0)
    def _():
        key_ref[...] = jnp.full(key_ref.shape, INT_MIN, I32)
        plane_row = lax.broadcasted_iota(I32, plane_ref.shape, 0) & (PLANE_ROWS - 1)
        plane_ref[...] = jnp.where(plane_row < 8, -1, 0)

    lax.fori_loop(0, (ntile + 1) // 2, score_tile, 0)

    def count(pred_fn):
        def body(jj, acc):
            for t in range(SCAN_TILES):
                j = SCAN_TILES * jj + t
                acc = acc + pred_fn(j, key_ref[tile_rows(j), :]).astype(I32)
            return acc
        trips = (ntile + SCAN_TILES - 1) // SCAN_TILES
        acc = lax.fori_loop(0, trips, body, jnp.zeros((LANES, Q_BLOCK), I32))
        return jnp.sum(acc, axis=0, keepdims=True)

    nword = plane_ref.shape[0] // PLANE_ROWS

    def plane(v, i):
        return plane_ref[pl.ds(pl.multiple_of(v * PLANE_ROWS + 8 * i, 8), 8), :]

    def total(words):
        acc = lax.population_count(words[0])
        for w in words[1:]:
            acc = acc + lax.population_count(w)
        return jnp.sum(acc, axis=0, keepdims=True)

    def radix_pair(t, c):
        alive, above, t_u = c
        i = 2 * t
        flip = jnp.where(t == 0, -1, 0)
        set1 = [a & (plane(v, i) ^ flip) for v, a in enumerate(alive)]
        clr1 = [a ^ s for a, s in zip(alive, set1)]
        p2 = [plane(v, i + 1) for v in range(nword)]
        set1_set2 = [s & p for s, p in zip(set1, p2)]
        clr1_set2 = [s & p for s, p in zip(clr1, p2)]
        n1, n11, n01 = total(set1), total(set1_set2), total(clr1_set2)
        take1 = (above + n1) >= ksel
        above = jnp.where(take1, above, above + n1)
        n2 = jnp.where(take1, n11, n01)
        take2 = (above + n2) >= ksel
        above = jnp.where(take2, above, above + n2)
        alive = tuple(
            jnp.where(take1, jnp.where(take2, ss, s ^ ss), jnp.where(take2, cs, c0 ^ cs))
            for s, c0, ss, cs in zip(set1, clr1, set1_set2, clr1_set2))
        t_u = (t_u | jnp.where(take1, jnp.left_shift(jnp.int32(1), 31 - i), 0)
               | jnp.where(take2, jnp.left_shift(jnp.int32(1), 30 - i), 0))
        return alive, above, t_u

    start = (tuple(jnp.full((8, Q_BLOCK), -1, I32) for _ in range(nword)),
             jnp.zeros((1, Q_BLOCK), I32), jnp.zeros((1, Q_BLOCK), I32))
    alive, above, t_u = lax.fori_loop(0, 16, radix_pair, start)
    thr = t_u ^ INT_MIN
    ties = lax.population_count(alive[0])
    for a in alive[1:]:
        ties = ties + lax.population_count(a)
    cnt_t = above + jnp.sum(ties, axis=0, keepdims=True)
    tied = jnp.max(jnp.where((cnt_t > ksel) & (thr > INT_MIN), 1.0, 0.0))

    jc_ref[...] = jnp.full(jc_ref.shape, 2 ** 31 - 1, I32)

    @pl.when(tied > 0.0)
    def _():
        need = ksel - count(lambda j, k: k > thr)

        def index_bit(i, j_c):
            cand = j_c | jnp.left_shift(jnp.int32(1), idx_bits - 1 - i)
            cnt = count(lambda j, k: (k == thr) & ((j * LANES + rowi) < cand))
            return jnp.where(cnt < need, cand, j_c)

        j_c = lax.fori_loop(0, idx_bits, index_bit, jnp.zeros((1, Q_BLOCK), I32))
        jc_ref[...] = jnp.broadcast_to(j_c, jc_ref.shape)

    j_c = jc_ref[0:1, :]
    eye = (rowi == coli).astype(BF16)

    group = KV_TILE // LANES

    floor = jnp.maximum(thr, INT_MIN + 1)

    def write_group(with_ties):
        def body(g, carry):
            for t in range(group):
                j = g * group + t
                k = key_ref[tile_rows(j), :]
                if with_ties:
                    sel = ((k > thr) | ((k == thr) & ((j * LANES + rowi) <= j_c))) & admissible(j)
                else:
                    sel = k >= floor
                sel = jnp.where(sel, 1.0, 0.0).astype(BF16)
                sel_t = lax.dot_general(eye, sel, _NT, preferred_element_type=F32)
                m_ref[0, 0, j] = ((sel_t - 1.0) * -NEG_BIG).astype(BF16)
            return carry
        return body

    ngroup = (ntile + group - 1) // group

    @pl.when(tied > 0.0)
    def _():
        lax.fori_loop(0, ngroup, write_group(True), 0)

    @pl.when(tied <= 0.0)
    def _():
        lax.fori_loop(0, ngroup, write_group(False), 0)

    def blank_tile(j, carry):
        m_ref[0, 0, j] = jnp.full((Q_BLOCK, LANES), NEG_BIG, BF16)
        return carry

    lax.fori_loop(ngroup * group, ntile_all, blank_tile, 0)


def _dsa_select(qit, wht, kix, ffn_weights, layer, batch, seq, ksel):
    nqb = seq // Q_BLOCK
    nkt = seq // LANES
    w_in, w_out, w_shape = _cast_specs([(w, layer) for w in ffn_weights], lambda b, q: b * nqb + q,
                                       batch * nqb)
    return pl.pallas_call(
        functools.partial(_dsa_select_kernel, ksel=ksel, idx_bits=int(math.log2(seq))),
        grid=(batch, nqb),
        in_specs=[
            pl.BlockSpec((LANES, IDX_HEADS * Q_BLOCK), lambda b, q: (0, b * nqb + q)),
            pl.BlockSpec((IDX_HEADS, Q_BLOCK), lambda b, q: (0, b * nqb + q)),
            pl.BlockSpec((seq, LANES), lambda b, q: (b, 0)),
            *w_in,
        ],
        out_specs=[pl.BlockSpec((1, 1, nkt, Q_BLOCK, LANES), lambda b, q: (b, q, 0, 0, 0)), *w_out],
        out_shape=[jax.ShapeDtypeStruct((batch, nqb, nkt, Q_BLOCK, LANES), BF16), *w_shape],
        scratch_shapes=[pltpu.VMEM((seq, Q_BLOCK), I32), pltpu.VMEM((8, Q_BLOCK), I32),
                        pltpu.VMEM((seq, Q_BLOCK), I32)],
        compiler_params=_params(("arbitrary", "arbitrary")),
        name="dsa_select",
    )(qit, wht, kix, *ffn_weights)


def _rel_bucket(rel):
    nb = REL_BUCKETS // 2
    max_exact = nb // 2
    ret = jnp.where(rel > 0, nb, 0)
    n = jnp.abs(rel)
    nf = jnp.maximum(n, 1).astype(F32)
    large = max_exact + (jnp.log(nf / max_exact) / math.log(REL_MAX_DIST / max_exact)
                         * (nb - max_exact)).astype(I32)
    large = jnp.minimum(large, nb - 1)
    return ret + jnp.where(n < max_exact, n, large)


NEAR_TILES = 3


def _dsa_attn_kernel(q_ref, kv_ref, mask_ref, rb_ref, wuv_ref, o_ref,
                     m_ref, l_ref, alpha_ref, acc_ref, corr_ref, s_ref, p_ref, madd_ref):
    b, qb = pl.program_id(0), pl.program_id(1)
    sub = KV_TILE // LANES
    far_bucket = REL_BUCKETS // 2 - 1
    half = DSA_HEADS * Q_BLOCK // 2

    def head_rows(h):
        return slice(h * Q_BLOCK, (h + 1) * Q_BLOCK)

    @pl.when((b == 0) & (qb == 0))
    def _():
        ti = lax.broadcasted_iota(I32, (Q_BLOCK, LANES), 0)
        si = lax.broadcasted_iota(I32, (Q_BLOCK, LANES), 1)
        for oi in range(NEAR_TILES):
            bucket = _rel_bucket((oi - (NEAR_TILES - 1)) * LANES + si - ti)
            for h in range(DSA_HEADS):
                tbl = jnp.zeros((Q_BLOCK, LANES), F32)
                for bk in range(REL_BUCKETS):
                    tbl = jnp.where(bucket == bk, rb_ref[bk, h], tbl)
                corr_ref[oi, head_rows(h), :] = (tbl - rb_ref[far_bucket, h]) * LOG2E

    m_ref[...] = jnp.full_like(m_ref, NEG_BIG)
    l_ref[...] = jnp.zeros_like(l_ref)
    acc_ref[...] = jnp.zeros_like(acc_ref)
    q_all = q_ref[0].reshape(DSA_HEADS * Q_BLOCK, DSA_KV_RANK)

    def key_step(kt, carry):
        kvt = kv_ref[pl.ds(pl.multiple_of(kt * KV_TILE, KV_TILE), KV_TILE), :]
        for part in range(2):
            rs = slice(part * half, (part + 1) * half)
            s_ref[rs, :] = lax.dot_general(q_all[rs], kvt, _NT, preferred_element_type=F32)
        for j in range(sub):
            d = kt * sub + j - qb

            @pl.when((d > -NEAR_TILES) & (d <= 0))
            def _(j=j, d=d):
                s_ref[:, j * LANES:(j + 1) * LANES] += corr_ref[d + NEAR_TILES - 1]

        for j in range(sub):
            madd_ref[:, j * LANES:(j + 1) * LANES] = mask_ref[0, 0, kt * sub + j].astype(F32)
        groups_per_head = Q_BLOCK // SM_ROWS
        for g in range(DSA_HEADS * groups_per_head):
            rs = slice(g * SM_ROWS, (g + 1) * SM_ROWS)
            qg = g % groups_per_head
            s = s_ref[rs, :] + madd_ref[qg * SM_ROWS:(qg + 1) * SM_ROWS, :]
            m_old = m_ref[rs, :]
            m_new = jnp.maximum(m_old, jnp.max(s, axis=1, keepdims=True))
            alpha = jnp.exp2(m_old - m_new)
            p = jnp.exp2(s - jnp.tile(m_new, (1, sub)))
            l_ref[rs, :] = alpha * l_ref[rs, :] + jnp.sum(p, axis=1, keepdims=True)
            alpha_ref[rs, :] = alpha
            p_ref[rs, :] = p.astype(BF16)
            m_ref[rs, :] = m_new
        for part in range(2):
            rs = slice(part * half, (part + 1) * half)
            pv = jnp.dot(p_ref[rs, :], kvt, preferred_element_type=F32)
            acc_ref[rs, :] = jnp.tile(alpha_ref[rs, :], (1, DSA_KV_RANK // LANES)) * acc_ref[rs, :] + pv
        return carry

    lax.fori_loop(0, qb // sub + 1, key_step, 0)

    dv = wuv_ref.shape[2]
    for h in range(DSA_HEADS):
        rs = head_rows(h)
        o = (acc_ref[rs, :] / jnp.tile(l_ref[rs, :], (1, DSA_KV_RANK // LANES))).astype(BF16)
        o_ref[:, h * dv:(h + 1) * dv] = jnp.dot(o, wuv_ref[h], preferred_element_type=F32).astype(BF16)


def _dsa_attn(q, kv, mask, rel_bias, w_uv, batch, seq):
    n = kv.shape[0]
    nqb = seq // Q_BLOCK
    dv = w_uv.shape[2]
    rows = DSA_HEADS * Q_BLOCK
    return pl.pallas_call(
        _dsa_attn_kernel,
        grid=(batch, nqb),
        in_specs=[
            pl.BlockSpec((1, DSA_HEADS, Q_BLOCK, DSA_KV_RANK), lambda b, qb: (b * nqb + qb, 0, 0, 0)),
            pl.BlockSpec((seq, DSA_KV_RANK), lambda b, qb: (b, 0)),
            pl.BlockSpec((1, 1) + mask.shape[2:], lambda b, qb: (b, qb, 0, 0, 0)),
            pl.BlockSpec(memory_space=pltpu.SMEM),
            pl.BlockSpec(w_uv.shape, lambda b, qb: (0, 0, 0)),
        ],
        out_specs=pl.BlockSpec((Q_BLOCK, DSA_HEADS * dv), lambda b, qb: (b * nqb + qb, 0)),
        out_shape=jax.ShapeDtypeStruct((n, DSA_HEADS * dv), BF16),
        scratch_shapes=[
            pltpu.VMEM((rows, LANES), F32),
            pltpu.VMEM((rows, LANES), F32),
            pltpu.VMEM((rows, LANES), F32),
            pltpu.VMEM((rows, DSA_KV_RANK), F32),
            pltpu.VMEM((NEAR_TILES, rows, LANES), F32),
            pltpu.VMEM((rows, KV_TILE), F32),
            pltpu.VMEM((rows, KV_TILE), BF16),
            pltpu.VMEM((Q_BLOCK, KV_TILE), F32),
        ],
        compiler_params=_params(("arbitrary", "arbitrary")),
        name="dsa_attn",
    )(q, kv, mask, rel_bias, w_uv)


def _pad_cols(w, width):
    return jnp.pad(w, ((0, 0), (0, width - w.shape[1])))


def kernel(x, ln_mix_g, ln_ffn_g, w_ffn_gate, w_ffn_up, w_ffn_down, rel_bias, ev_w_in, ev_w_out, sgu_ln_g, sgu_ln_b, sgu_w_s, sgu_b_s, od_w_in, od_w_out, hgrn_lb, hgrn_norm_g, dsa_cq_g, dsa_ckv_g, dsa_w_uq, dsa_qnorm_g, dsa_w_qidx, dsa_w_uv):
    batch, seq, d = x.shape
    n = batch * seq
    depth = ln_mix_g.shape[0]
    ksel = min(TOPK_MAX, seq // 4)
    tm = min(PROJ_ROWS, n)
    xf = x.reshape(n, d)
    ffn_f32 = (w_ffn_gate, w_ffn_up, w_ffn_down)
    ev_in_all, od_in_all = ev_w_in.astype(BF16), od_w_in.astype(BF16)
    for layer in range(depth):
        j = layer // 2
        if layer % 2 == 0:
            z, *ffn_w = _norm_matmul(xf, ln_mix_g[layer], ev_in_all, j, ev_in_all.shape[2], tm=tm,
                                     tn=PROJ_COLS, out_dtype=BF16, cast=[(w, layer) for w in ffn_f32])
            a1, a2 = _even_mixers(z, sgu_ln_g[j], sgu_ln_b[j], sgu_w_s[j], sgu_b_s[j], batch, seq)
            w_out = ev_w_out[j]
        else:
            w_in = od_w_in[j]
            gw = d // 2
            c = 4 * gw
            c_kidx = c + DSA_Q_RANK + DSA_KV_RANK
            w_dsa = jnp.concatenate([
                w_in[:, c:c_kidx],
                _pad_cols(w_in[:, c_kidx:c_kidx + IDX_DIM], LANES),
                _pad_cols(w_in[:, c_kidx + IDX_DIM:], LANES),
            ], axis=1).astype(BF16)
            a1, zd = _inproj_hgrn(xf, ln_mix_g[layer], od_in_all, j, w_dsa, hgrn_lb, hgrn_norm_g[j],
                                  batch, seq, layer)
            w_qit = jnp.pad(dsa_w_qidx[j].T.reshape(IDX_HEADS, IDX_DIM, DSA_Q_RANK),
                            ((0, 0), (0, LANES - IDX_DIM), (0, 0))).reshape(IDX_HEADS * LANES, DSA_Q_RANK)
            q, qit, kv, kix, wht = _dsa_prep(zd, dsa_cq_g[j], dsa_ckv_g[j], dsa_w_uq[j].astype(BF16),
                                             dsa_qnorm_g[j], w_qit.astype(BF16), tm=min(PREP_ROWS, n))
            mask, *ffn_w = _dsa_select(qit, wht, kix, ffn_f32, layer, batch, seq, ksel)
            a2 = _dsa_attn(q, kv, mask, rel_bias, dsa_w_uv[j].astype(BF16), batch, seq)
            w_out = od_w_out[j]
        xf = _outproj(a1, a2, w_out.astype(BF16), xf, tm=min(OUT_ROWS, n), tn=OUT_COLS)
        xf = _ffn(xf, ln_ffn_g[layer], *(w[None] for w in ffn_w), 0, tm=min(FFN_ROWS, n), tf=FFN_COLS)
    return xf.reshape(batch, seq, d)
```

```python
import functools
import math

import jax
import jax.numpy as jnp
from jax import lax
from jax.experimental import pallas as pl
from jax.experimental.pallas import tpu as pltpu

F32 = jnp.float32
BF16 = jnp.bfloat16
I32 = jnp.int32

EPS = 1e-6
CHUNK = 64
LANES = 128
ROPE_BASE = 10000.0
RET_HEADS = 4
SGU_WINDOW = 128
SGU_GROUPS = 4
HG_HEADS = 8
DSA_HEADS = 8
DSA_Q_RANK = 384
DSA_KV_RANK = 256
IDX_HEADS = 16
IDX_DIM = 64
TOPK_MAX = 256
Q_BLOCK = 128
KV_TILE = 512
SM_ROWS = 64
LOG2E = math.log2(math.e)
REL_BUCKETS = 32
REL_MAX_DIST = 256
NEG_BIG = -1e30
INT_MIN = -(2 ** 31)
SCAN_TILES = 2
PLANE_ROWS = 256

RET_BLOCK = 256
HG_BLOCK = 256
HG_FINE = 4
VMEM_LIMIT = 48 * 1024 * 1024
PROJ_ROWS = 1024
PROJ_COLS = 1024
OUT_ROWS = 512
OUT_COLS = 2048
FFN_ROWS = 1024
FFN_VMEM_LIMIT = 58 * 1024 * 1024
SIDE_VMEM_LIMIT = 56 * 1024 * 1024
FFN_COLS = 512
PREP_ROWS = 512

_NT = (((1,), (1,)), ((), ()))
_TN = (((0,), (0,)), ((), ()))


def _params(semantics, vmem_limit=VMEM_LIMIT):
    return pltpu.CompilerParams(dimension_semantics=semantics, vmem_limit_bytes=vmem_limit)


def _silu(x):
    return x * jax.nn.sigmoid(x)


def _rms(x):
    return x * lax.rsqrt(jnp.mean(x * x, axis=-1, keepdims=True) + EPS)


def _cast_blocks(rows, steps):
    units = rows // 16
    blocks = max(k for k in range(1, min(units, steps) + 1) if units % k == 0)
    return rows // blocks, blocks


def _cast_specs(weights, step_of, steps):
    ins, outs, shapes = [], [], []
    for w, layer in weights:
        rows, blocks = _cast_blocks(w.shape[1], steps)

        def index(*ids, blocks=blocks):
            return (jnp.minimum(step_of(*ids), blocks - 1), 0)

        ins.append(pl.BlockSpec((None, rows, w.shape[2]), lambda *ids, layer=layer, index=index: (layer,) + index(*ids)))
        outs.append(pl.BlockSpec((rows, w.shape[2]), index))
        shapes.append(jax.ShapeDtypeStruct(w.shape[1:], BF16))
    return ins, outs, shapes


def _norm_matmul_kernel(x_ref, g_ref, w_ref, *rest):
    ncast = (len(rest) - 2) // 2
    o_ref, h_ref = rest[ncast], rest[-1]
    for src, dst in zip(rest[:ncast], rest[ncast + 1:-1]):
        dst[...] = src[...].astype(BF16)

    @pl.when(pl.program_id(1) == 0)
    def _():
        h_ref[...] = (_rms(x_ref[...]) * g_ref[...]).astype(BF16)

    o_ref[...] = jnp.dot(h_ref[...], w_ref[...], preferred_element_type=F32).astype(o_ref.dtype)


def _norm_matmul(x, g, w, layer, nout, *, tm, tn, out_dtype=F32, cast=()):
    n, d = x.shape
    ncols = nout // tn
    c_in, c_out, c_shape = _cast_specs(cast, lambda i, j: i * ncols + j, (n // tm) * ncols)
    return pl.pallas_call(
        _norm_matmul_kernel,
        grid=(n // tm, ncols),
        in_specs=[
            pl.BlockSpec((tm, d), lambda i, j: (i, 0)),
            pl.BlockSpec((1, d), lambda i, j: (0, 0)),
            pl.BlockSpec((None, d, tn), lambda i, j: (layer, 0, j)),
            *c_in,
        ],
        out_specs=[pl.BlockSpec((tm, tn), lambda i, j: (i, j)), *c_out],
        out_shape=[jax.ShapeDtypeStruct((n, nout), out_dtype), *c_shape],
        scratch_shapes=[pltpu.VMEM((tm, d), BF16)],
        compiler_params=_params(("arbitrary", "arbitrary"), SIDE_VMEM_LIMIT),
        name="norm_matmul",
    )(x, g.reshape(1, d), w, *(w_c for w_c, _ in cast))


def _outproj_kernel(a1_ref, a2_ref, w1_ref, w2_ref, r_ref, o_ref):
    acc = jnp.dot(a1_ref[...], w1_ref[...], preferred_element_type=F32)
    acc += jnp.dot(a2_ref[...], w2_ref[...], preferred_element_type=F32)
    o_ref[...] = r_ref[...] + acc


def _outproj(a1, a2, w, res, *, tm, tn):
    n, half = a1.shape
    d = w.shape[1]
    return pl.pallas_call(
        _outproj_kernel,
        grid=(n // tm, d // tn),
        in_specs=[
            pl.BlockSpec((tm, half), lambda i, j: (i, 0)),
            pl.BlockSpec((tm, half), lambda i, j: (i, 0)),
            pl.BlockSpec((half, tn), lambda i, j: (0, j)),
            pl.BlockSpec((half, tn), lambda i, j: (1, j)),
            pl.BlockSpec((tm, tn), lambda i, j: (i, j)),
        ],
        out_specs=pl.BlockSpec((tm, tn), lambda i, j: (i, j)),
        out_shape=jax.ShapeDtypeStruct((n, d), F32),
        compiler_params=_params(("arbitrary", "arbitrary")),
        name="outproj",
    )(a1, a2, w, w, res)


def _ffn_kernel(x_ref, g_ref, wg_ref, wu_ref, wd_ref, o_ref, h_ref):
    first = pl.program_id(1) == 0

    @pl.when(first)
    def _():
        h_ref[...] = (_rms(x_ref[...]) * g_ref[...]).astype(BF16)

    def down_projection():
        h = h_ref[...]
        a = jnp.dot(h, wg_ref[...], preferred_element_type=F32)
        u = jnp.dot(h, wu_ref[...], preferred_element_type=F32)
        act = (_silu(a) * u).astype(BF16)
        return jnp.dot(act, wd_ref[...], preferred_element_type=F32)

    @pl.when(first)
    def _():
        o_ref[...] = x_ref[...] + down_projection()

    @pl.when(jnp.logical_not(first))
    def _():
        o_ref[...] += down_projection()


def _ffn(x, g, wg, wu, wd, layer, *, tm, tf):
    n, d = x.shape
    dff = wg.shape[2]
    return pl.pallas_call(
        _ffn_kernel,
        grid=(n // tm, dff // tf),
        in_specs=[
            pl.BlockSpec((tm, d), lambda i, f: (i, 0)),
            pl.BlockSpec((1, d), lambda i, f: (0, 0)),
            pl.BlockSpec((None, d, tf), lambda i, f: (layer, 0, f)),
            pl.BlockSpec((None, d, tf), lambda i, f: (layer, 0, f)),
            pl.BlockSpec((None, tf, d), lambda i, f: (layer, f, 0)),
        ],
        out_specs=pl.BlockSpec((tm, d), lambda i, f: (i, 0)),
        out_shape=jax.ShapeDtypeStruct((n, d), F32),
        scratch_shapes=[pltpu.VMEM((tm, d), BF16)],
        compiler_params=_params(("arbitrary", "arbitrary"), FFN_VMEM_LIMIT),
        name="ffn",
    )(x, g.reshape(1, d), wg, wu, wd)


def _retention_tables(seq, dk):
    blk = RET_BLOCK
    pos = jnp.arange(seq, dtype=F32)
    inv = ROPE_BASE ** (-jnp.arange(0, dk, 2, dtype=F32) / dk)
    ang = pos[:, None] * inv[None, :]
    log_gamma = jnp.log(1.0 - 2.0 ** (-5.0 - jnp.arange(RET_HEADS, dtype=F32)))
    i = jnp.arange(blk)
    same = (i[:, None] // CHUNK) == (i[None, :] // CHUNK)
    earlier = (i[None, :] // CHUNK) < (i[:, None] // CHUNK)
    diff = (i[:, None] - i[None, :]).astype(F32)
    dist = jnp.where(same, jnp.abs(diff), diff)
    decay = jnp.where((same | earlier)[None], jnp.exp(log_gamma[:, None, None] * dist[None]), 0.0)
    p = jnp.arange(blk, dtype=F32)
    wide = (RET_HEADS, blk, dk)
    xi = jnp.broadcast_to(jnp.exp(log_gamma[:, None] * (p + 1.0))[:, :, None], wide)
    zeta = jnp.broadcast_to(jnp.exp(log_gamma[:, None] * (blk - 1.0 - p))[:, :, None], wide)
    g_blk = jnp.broadcast_to(jnp.exp(log_gamma * blk)[:, None, None], (RET_HEADS, 1, dk))
    return jnp.cos(ang), jnp.sin(ang), decay, xi, zeta, g_blk


def _gelu(x):
    return 0.5 * x * (1.0 + lax.erf(x * math.sqrt(0.5)))


def _sgu_kernel(u_ref, v_ref, lng_ref, lnb_ref, w_ref, b_ref, o_ref):
    rows, width = v_ref.shape
    dg = width // SGU_GROUPS
    v = _gelu(v_ref[...].astype(F32))
    mu = jnp.mean(v, axis=-1, keepdims=True)
    var = jnp.mean(jnp.square(v - mu), axis=-1, keepdims=True)
    vn = ((v - mu) * lax.rsqrt(var + EPS) * lng_ref[...] + lnb_ref[...]).astype(BF16)
    u = _gelu(u_ref[...].astype(F32))
    ri = lax.broadcasted_iota(I32, (SGU_WINDOW, SGU_WINDOW), 0) // CHUNK
    ci = lax.broadcasted_iota(I32, (SGU_WINDOW, SGU_WINDOW), 1) // CHUNK
    allowed = ci <= ri
    for g in range(SGU_GROUPS):
        wg = jnp.where(allowed, w_ref[g], 0.0).astype(BF16)
        bias = b_ref[g]
        for w in range(rows // SGU_WINDOW):
            rs = slice(w * SGU_WINDOW, (w + 1) * SGU_WINDOW)
            cs = slice(g * dg, (g + 1) * dg)
            mixed = jnp.dot(wg, vn[rs, cs], preferred_element_type=F32) + bias
            o_ref[rs, cs] = (u[rs, cs] * mixed).astype(BF16)


def _even_mixers_kernel(q_ref, k_ref, v_ref, g_ref, u_ref, vs_ref, cos_ref, sin_ref, d_ref, xi_ref,
                        zeta_ref, gl_ref, lng_ref, lnb_ref, ws_ref, bs_ref, oret_ref, osgu_ref, state_ref):
    @pl.when(pl.program_id(1) == 0)
    def _():
        state_ref[...] = jnp.zeros_like(state_ref)

    cos = cos_ref[...]
    sin = sin_ref[...]
    half = cos.shape[1]
    dk = 2 * half

    def rot(t):
        t1, t2 = t[:, :half], t[:, half:]
        return jnp.concatenate([t1 * cos - t2 * sin, t1 * sin + t2 * cos], axis=1)

    for i in range(RET_HEADS):
        cs = slice(i * dk, (i + 1) * dk)
        q = rot(q_ref[:, cs].astype(F32))
        k = rot(k_ref[:, cs].astype(F32)) * (dk ** -0.5)
        qb = q.astype(BF16)
        vb = v_ref[:, cs].astype(BF16)
        scores = lax.dot_general(qb, k.astype(BF16), _NT, preferred_element_type=F32) * d_ref[i]
        intra = jnp.dot(scores.astype(BF16), vb, preferred_element_type=F32)
        state = state_ref[i]
        cross = jnp.dot(qb, state.astype(BF16), preferred_element_type=F32) * xi_ref[i]
        kz = (k * zeta_ref[i]).astype(BF16)
        state_ref[i] = state * gl_ref[i] + lax.dot_general(kz, vb, _TN, preferred_element_type=F32)
        oret_ref[:, cs] = (_rms(intra + cross) * _silu(g_ref[:, cs].astype(F32))).astype(BF16)
    _sgu_kernel(u_ref, vs_ref, lng_ref, lnb_ref, ws_ref, bs_ref, osgu_ref)


def _even_mixers(z, ln_g, ln_b, w_s, b_s, batch, seq):
    n = z.shape[0]
    gw = z.shape[1] // 6
    dk = gw // RET_HEADS
    blk = RET_BLOCK
    nblk = seq // blk
    cos, sin, decay, xi, zeta, g_blk = _retention_tables(seq, dk)
    zspec = lambda part: pl.BlockSpec((blk, gw), lambda b, c: (b * nblk + c, part))
    const2 = lambda b, c: (0, 0)
    const3 = lambda b, c: (0, 0, 0)
    out = pl.BlockSpec((blk, gw), lambda b, c: (b * nblk + c, 0))
    return pl.pallas_call(
        _even_mixers_kernel,
        grid=(batch, nblk),
        in_specs=[
            zspec(0), zspec(1), zspec(2), zspec(3), zspec(4), zspec(5),
            pl.BlockSpec((blk, dk // 2), lambda b, c: (c, 0)),
            pl.BlockSpec((blk, dk // 2), lambda b, c: (c, 0)),
            pl.BlockSpec(decay.shape, const3), pl.BlockSpec(xi.shape, const3),
            pl.BlockSpec(zeta.shape, const3), pl.BlockSpec(g_blk.shape, const3),
            pl.BlockSpec((1, gw), const2), pl.BlockSpec((1, gw), const2),
            pl.BlockSpec((SGU_GROUPS, SGU_WINDOW, SGU_WINDOW), const3),
            pl.BlockSpec((SGU_GROUPS, SGU_WINDOW, 1), const3),
        ],
        out_specs=[out, out],
        out_shape=[jax.ShapeDtypeStruct((n, gw), BF16)] * 2,
        scratch_shapes=[pltpu.VMEM((RET_HEADS, dk, dk), F32)],
        compiler_params=_params(("arbitrary", "arbitrary")),
        name="even_mixers",
    )(z, z, z, z, z, z, cos, sin, decay, xi, zeta, g_blk, ln_g.reshape(1, gw), ln_b.reshape(1, gw),
      w_s, b_s.reshape(SGU_GROUPS, SGU_WINDOW, 1))


def _hgrn_head(q, f_logits, v, g, lb, ng, tri, st_ref, sh_ref):
    rows, dk = q.shape
    f = lb + (1.0 - lb) * jax.nn.sigmoid(f_logits)
    lf = jnp.log(f)
    kk = 1.0 - f
    qa = _silu(q)

    bcum = None
    rest = lf
    for _ in range(3):
        term = rest.astype(BF16)
        part = jnp.dot(tri, term, preferred_element_type=F32)
        bcum = part if bcum is None else bcum + part
        rest = rest - term.astype(F32)

    row = lax.broadcasted_iota(I32, (rows, dk), 0)
    ti = lax.broadcasted_iota(I32, (rows, rows), 0)
    si = lax.broadcasted_iota(I32, (rows, rows), 1)
    attn = jnp.zeros((rows, rows), F32)
    hs = rows // 2
    while hs >= HG_FINE:
        bs = 2 * hs
        parts = [jnp.broadcast_to(bcum[b * bs + hs - 1:b * bs + hs, :], (bs, dk))
                 for b in range(rows // bs)]
        anchor = parts[0] if len(parts) == 1 else jnp.concatenate(parts, axis=0)
        upper = (row & (bs - 1)) >= hs
        fac = jnp.exp(-jnp.abs(bcum - anchor))
        qt = jnp.where(upper, qa * fac, 0.0)
        kt = jnp.where(upper, 0.0, kk * fac)
        a = lax.dot_general(qt.astype(BF16), kt.astype(BF16), _NT, preferred_element_type=F32)
        if bs < rows:
            a = jnp.where((ti & -bs) == (si & -bs), a, 0.0)
        attn = attn + a
        hs //= 2

    vb = v.astype(BF16)
    near = qa * kk
    intra = jnp.sum(near, axis=1, keepdims=True) * v
    pad = jnp.zeros((HG_FINE, dk), F32)
    for idx, val in enumerate((kk, bcum, v)):
        sh_ref[idx, 0:HG_FINE, :] = pad
        sh_ref[idx, HG_FINE:, :] = val
    for delta in range(1, HG_FINE):
        back = slice(HG_FINE - delta, HG_FINE - delta + rows)
        prod = qa * sh_ref[0, back, :] * jnp.exp(jnp.minimum(bcum - sh_ref[1, back, :], 0.0))
        prod = jnp.where((row & (HG_FINE - 1)) >= delta, prod, 0.0)
        intra = intra + jnp.sum(prod, axis=1, keepdims=True) * sh_ref[2, back, :]
    intra = intra + jnp.dot(attn.astype(BF16), vb, preferred_element_type=F32)
    st = st_ref[...]
    cross = lax.dot_general((qa * jnp.exp(bcum)).astype(BF16), st.astype(BF16), _NT,
                            preferred_element_type=F32)
    blast = bcum[rows - 1:rows, :]
    kb = (kk * jnp.exp(blast - bcum)).astype(BF16)
    st_ref[...] = st * jnp.exp(blast) + lax.dot_general(vb, kb, _TN, preferred_element_type=F32)
    return _rms(intra + cross) * ng * _silu(g)


def _hgrn_lower_bound(lb_raw, layer):
    e = jnp.exp(lb_raw - jnp.max(lb_raw, axis=0, keepdims=True))
    soft = e / jnp.sum(e, axis=0, keepdims=True)
    return jnp.sum(soft[1:layer + 1], axis=0, keepdims=True)


def _inproj_hgrn_kernel(x_ref, g_ref, w_ref, ws_ref, lb_ref, ng_ref, tri_ref, o_ref, zd_ref,
                        z_ref, st_ref, sh_ref, *, layer):
    c = pl.program_id(1)
    gw = o_ref.shape[1]
    dk = gw // HG_HEADS

    last = pl.num_programs(1) - 1

    @pl.when(c == 1)
    def _():
        st_ref[...] = jnp.zeros_like(st_ref)

    def project():
        h = (_rms(x_ref[...]) * g_ref[...]).astype(BF16)
        z_ref[c % 2] = jnp.dot(h, w_ref[...], preferred_element_type=F32)
        zd_ref[...] = jnp.dot(h, ws_ref[...], preferred_element_type=F32)

    def recur():
        prev = (c + 1) % 2
        lb = _hgrn_lower_bound(lb_ref[...], layer)
        tri = tri_ref[...]
        for hd in range(HG_HEADS):
            cols = [slice(part * gw + hd * dk, part * gw + (hd + 1) * dk) for part in range(4)]
            hs = slice(hd * dk, (hd + 1) * dk)
            out = _hgrn_head(z_ref[prev, :, cols[0]], z_ref[prev, :, cols[1]], z_ref[prev, :, cols[2]],
                             z_ref[prev, :, cols[3]], lb[:, hs], ng_ref[:, hs], tri, st_ref.at[hd],
                             sh_ref.at[hd])
            o_ref[:, hs] = out.astype(BF16)

    @pl.when(c == 0)
    def _():
        project()

    @pl.when((c > 0) & (c < last))
    def _():
        project()
        recur()

    @pl.when(c == last)
    def _():
        recur()


def _inproj_hgrn(x, g, w, layer_idx, w_side, lb_raw, norm_g, batch, seq, layer):
    n, d = x.shape
    gw = norm_g.shape[0]
    dk = gw // HG_HEADS
    blk = HG_BLOCK
    nblk = seq // blk
    ns = w_side.shape[1]
    depth = lb_raw.shape[0]
    const = lambda b, c: (0, 0)
    return pl.pallas_call(
        functools.partial(_inproj_hgrn_kernel, layer=layer),
        grid=(batch, nblk + 1),
        in_specs=[
            pl.BlockSpec((blk, d), lambda b, c: (b * nblk + jnp.minimum(c, nblk - 1), 0)),
            pl.BlockSpec((1, d), const),
            pl.BlockSpec((None, d, 4 * gw), lambda b, c: (layer_idx, 0, 0), pipeline_mode=pl.Buffered(1)),
            pl.BlockSpec((d, ns), const, pipeline_mode=pl.Buffered(1)),
            pl.BlockSpec((depth, gw), const),
            pl.BlockSpec((1, gw), const),
            pl.BlockSpec((blk, blk), const),
        ],
        out_specs=[
            pl.BlockSpec((blk, gw), lambda b, c: (b * nblk + jnp.maximum(c - 1, 0), 0)),
            pl.BlockSpec((blk, ns), lambda b, c: (b * nblk + jnp.minimum(c, nblk - 1), 0)),
        ],
        out_shape=[jax.ShapeDtypeStruct((n, gw), BF16), jax.ShapeDtypeStruct((n, ns), F32)],
        scratch_shapes=[
            pltpu.VMEM((2, blk, 4 * gw), F32),
            pltpu.VMEM((HG_HEADS, dk, dk), F32),
            pltpu.VMEM((HG_HEADS, 3, blk + HG_FINE, dk), F32),
        ],
        compiler_params=_params(("arbitrary", "arbitrary"), SIDE_VMEM_LIMIT),
        name="inproj_hgrn",
    )(x, g.reshape(1, d), w, w_side, lb_raw, norm_g.reshape(1, gw), jnp.tril(jnp.ones((blk, blk), BF16)))


def _dsa_prep_kernel(zd_ref, cqg_ref, ckvg_ref, wuq_ref, qng_ref, wqit_ref,
                     q_ref, qit_ref, kv_ref, kix_ref, wht_ref):
    zd = zd_ref[...]
    cq = (_rms(zd[:, :DSA_Q_RANK]) * cqg_ref[...]).astype(BF16)
    qf = jnp.dot(cq, wuq_ref[...], preferred_element_type=F32)
    for i in range(q_ref.shape[0]):
        rs = slice(i * Q_BLOCK, (i + 1) * Q_BLOCK)
        for h in range(DSA_HEADS):
            cs = slice(h * DSA_KV_RANK, (h + 1) * DSA_KV_RANK)
            q_ref[i, h] = (_rms(qf[rs, cs]) * qng_ref[...] * (DSA_KV_RANK ** -0.5 * LOG2E)).astype(BF16)
    qit = lax.dot_general(wqit_ref[...], cq, _NT, preferred_element_type=F32)
    qit = (qit * (IDX_DIM ** -0.5)).astype(BF16)
    for i in range(q_ref.shape[0]):
        for h in range(IDX_HEADS):
            c = (i * IDX_HEADS + h) * Q_BLOCK
            qit_ref[:, c:c + Q_BLOCK] = qit[h * LANES:(h + 1) * LANES, i * Q_BLOCK:(i + 1) * Q_BLOCK]
    c0 = DSA_Q_RANK
    c1 = c0 + DSA_KV_RANK
    kv_ref[...] = (_rms(zd[:, c0:c1]) * ckvg_ref[...]).astype(BF16)
    kix_ref[...] = zd[:, c1:c1 + LANES].astype(BF16)
    wht = jnp.transpose(zd[:, c1 + LANES:c1 + 2 * LANES] * (IDX_HEADS ** -0.5))
    wht_ref[...] = wht[:IDX_HEADS, :]


def _dsa_prep(zd, cq_g, ckv_g, w_uq, qn_g, w_qit, *, tm):
    n, wd = zd.shape
    dq = w_uq.shape[1]
    dqi = w_qit.shape[0]
    full = lambda i: (0, 0)
    rows = lambda i: (i, 0)
    cols = lambda i: (0, i)
    return pl.pallas_call(
        _dsa_prep_kernel,
        grid=(n // tm,),
        in_specs=[
            pl.BlockSpec((tm, wd), rows),
            pl.BlockSpec((1, DSA_Q_RANK), full),
            pl.BlockSpec((1, DSA_KV_RANK), full),
            pl.BlockSpec((DSA_Q_RANK, dq), full),
            pl.BlockSpec((1, DSA_KV_RANK), full),
            pl.BlockSpec((dqi, DSA_Q_RANK), full),
        ],
        out_specs=[
            pl.BlockSpec((tm // Q_BLOCK, DSA_HEADS, Q_BLOCK, DSA_KV_RANK), lambda i: (i, 0, 0, 0)),
            pl.BlockSpec((LANES, IDX_HEADS * tm), cols),
            pl.BlockSpec((tm, DSA_KV_RANK), rows),
            pl.BlockSpec((tm, LANES), rows),
            pl.BlockSpec((IDX_HEADS, tm), cols),
        ],
        out_shape=[
            jax.ShapeDtypeStruct((n // Q_BLOCK, DSA_HEADS, Q_BLOCK, DSA_KV_RANK), BF16),
            jax.ShapeDtypeStruct((LANES, IDX_HEADS * n), BF16),
            jax.ShapeDtypeStruct((n, DSA_KV_RANK), BF16),
            jax.ShapeDtypeStruct((n, LANES), BF16),
            jax.ShapeDtypeStruct((IDX_HEADS, n), F32),
        ],
        compiler_params=_params(("arbitrary",)),
        name="dsa_prep",
    )(zd, cq_g.reshape(1, -1), ckv_g.reshape(1, -1), w_uq, qn_g.reshape(1, -1), w_qit)


def _bit_planes(v):
    v = list(v)

    def swap(lo, hi, j, m):
        return (lo & ~m) | ((hi >> j) & m), (hi & m) | ((lo << j) & ~m)

    for j, m in ((16, 0x0000FFFF), (8, 0x00FF00FF), (4, 0x0F0F0F0F), (2, 0x33333333), (1, 0x55555555)):
        for k in range(len(v)):
            if k & j == 0:
                v[k], v[k + j] = swap(v[k], v[k + j], j, m)
    return v


def _dsa_select_kernel(qit_ref, wht_ref, kix_ref, wg_in, wu_in, wd_in, m_ref, wg_out, wu_out, wd_out,
                       key_ref, jc_ref, plane_ref, *, ksel, idx_bits):
    for src, dst in ((wg_in, wg_out), (wu_in, wu_out), (wd_in, wd_out)):
        dst[...] = src[...].astype(BF16)

    qb = pl.program_id(1)
    ntile = qb + 1
    ntile_all = m_ref.shape[2]
    rowi = lax.broadcasted_iota(I32, (LANES, Q_BLOCK), 0)
    coli = lax.broadcasted_iota(I32, (LANES, Q_BLOCK), 1)
    q_chunk = (qb * Q_BLOCK + coli) // CHUNK
    pairs = IDX_HEADS // 2
    w_pair = [jnp.concatenate([wht_ref[2 * p:2 * p + 1, :], wht_ref[2 * p + 1:2 * p + 2, :]], axis=1)
              for p in range(pairs)]

    def tile_rows(j):
        return pl.ds(pl.multiple_of(j * LANES, LANES), LANES)

    def admissible(j):
        return ((j * LANES + rowi) // CHUNK) <= q_chunk

    def score_tile(jj, carry):
        words = []
        for t in range(2):
            j = 2 * jj + t
            kt = kix_ref[tile_rows(j), :]
            sc = None
            for p in range(pairs):
                s2 = jnp.dot(kt, qit_ref[:, 2 * p * Q_BLOCK:2 * (p + 1) * Q_BLOCK],
                             preferred_element_type=F32)
                c2 = w_pair[p] * jnp.maximum(s2, 0.0)
                c = c2[:, :Q_BLOCK] + c2[:, Q_BLOCK:]
                sc = c if sc is None else sc + c
            bits = pltpu.bitcast(sc, I32)
            key = bits ^ ((bits >> 31) & 0x7FFFFFFF)
            key = jnp.where(key == -1, 0, key)
            key = jnp.where(admissible(j), key, INT_MIN)
            key_ref[tile_rows(j), :] = key
            words += [key[8 * k:8 * (k + 1), :] for k in range(LANES // 8)]
        for i, p in enumerate(_bit_planes(words)):
            plane_ref[pl.ds(pl.multiple_of(jj * PLANE_ROWS + 8 * i, 8), 8), :] = p
        return carry

    @pl.when(qb == 0)
    def _():
        key_ref[...] = jnp.full(key_ref.shape, INT_MIN, I32)
        plane_row = lax.broadcasted_iota(I32, plane_ref.shape, 0) & (PLANE_ROWS - 1)
        plane_ref[...] = jnp.where(plane_row < 8, -1, 0)

    lax.fori_loop(0, (ntile + 1) // 2, score_tile, 0)

    def count(pred_fn):
        def body(jj, acc):
            for t in range(SCAN_TILES):
                j = SCAN_TILES * jj + t
                acc = acc + pred_fn(j, key_ref[tile_rows(j), :]).astype(I32)
            return acc
        trips = (ntile + SCAN_TILES - 1) // SCAN_TILES
        acc = lax.fori_loop(0, trips, body, jnp.zeros((LANES, Q_BLOCK), I32))
        return jnp.sum(acc, axis=0, keepdims=True)

    nword = plane_ref.shape[0] // PLANE_ROWS

    def plane(v, i):
        return plane_ref[pl.ds(pl.multiple_of(v * PLANE_ROWS + 8 * i, 8), 8), :]

    def total(words):
        acc = lax.population_count(words[0])
        for w in words[1:]:
            acc = acc + lax.population_count(w)
        return jnp.sum(acc, axis=0, keepdims=True)

    def radix_pair(t, c):
        alive, above, t_u = c
        i = 2 * t
        flip = jnp.where(t == 0, -1, 0)
        set1 = [a & (plane(v, i) ^ flip) for v, a in enumerate(alive)]
        clr1 = [a ^ s for a, s in zip(alive, set1)]
        p2 = [plane(v, i + 1) for v in range(nword)]
        set1_set2 = [s & p for s, p in zip(set1, p2)]
        clr1_set2 = [s & p for s, p in zip(clr1, p2)]
        n1, n11, n01 = total(set1), total(set1_set2), total(clr1_set2)
        take1 = (above + n1) >= ksel
        above = jnp.where(take1, above, above + n1)
        n2 = jnp.where(take1, n11, n01)
        take2 = (above + n2) >= ksel
        above = jnp.where(take2, above, above + n2)
        alive = tuple(
            jnp.where(take1, jnp.where(take2, ss, s ^ ss), jnp.where(take2, cs, c0 ^ cs))
            for s, c0, ss, cs in zip(set1, clr1, set1_set2, clr1_set2))
        t_u = (t_u | jnp.where(take1, jnp.left_shift(jnp.int32(1), 31 - i), 0)
               | jnp.where(take2, jnp.left_shift(jnp.int32(1), 30 - i), 0))
        return alive, above, t_u

    start = (tuple(jnp.full((8, Q_BLOCK), -1, I32) for _ in range(nword)),
             jnp.zeros((1, Q_BLOCK), I32), jnp.zeros((1, Q_BLOCK), I32))
    alive, above, t_u = lax.fori_loop(0, 16, radix_pair, start)
    thr = t_u ^ INT_MIN
    ties = lax.population_count(alive[0])
    for a in alive[1:]:
        ties = ties + lax.population_count(a)
    cnt_t = above + jnp.sum(ties, axis=0, keepdims=True)
    tied = jnp.max(jnp.where((cnt_t > ksel) & (thr > INT_MIN), 1.0, 0.0))

    jc_ref[...] = jnp.full(jc_ref.shape, 2 ** 31 - 1, I32)

    @pl.when(tied > 0.0)
    def _():
        need = ksel - count(lambda j, k: k > thr)

        def index_bit(i, j_c):
            cand = j_c | jnp.left_shift(jnp.int32(1), idx_bits - 1 - i)
            cnt = count(lambda j, k: (k == thr) & ((j * LANES + rowi) < cand))
            return jnp.where(cnt < need, cand, j_c)

        j_c = lax.fori_loop(0, idx_bits, index_bit, jnp.zeros((1, Q_BLOCK), I32))
        jc_ref[...] = jnp.broadcast_to(j_c, jc_ref.shape)

    j_c = jc_ref[0:1, :]
    eye = (rowi == coli).astype(BF16)

    group = KV_TILE // LANES

    floor = jnp.maximum(thr, INT_MIN + 1)

    def write_group(with_ties):
        def body(g, carry):
            for t in range(group):
                j = g * group + t
                k = key_ref[tile_rows(j), :]
                if with_ties:
                    sel = ((k > thr) | ((k == thr) & ((j * LANES + rowi) <= j_c))) & admissible(j)
                else:
                    sel = k >= floor
                sel = jnp.where(sel, 1.0, 0.0).astype(BF16)
                sel_t = lax.dot_general(eye, sel, _NT, preferred_element_type=F32)
                m_ref[0, 0, j] = ((sel_t - 1.0) * -NEG_BIG).astype(BF16)
            return carry
        return body

    ngroup = (ntile + group - 1) // group

    @pl.when(tied > 0.0)
    def _():
        lax.fori_loop(0, ngroup, write_group(True), 0)

    @pl.when(tied <= 0.0)
    def _():
        lax.fori_loop(0, ngroup, write_group(False), 0)

    def blank_tile(j, carry):
        m_ref[0, 0, j] = jnp.full((Q_BLOCK, LANES), NEG_BIG, BF16)
        return carry

    lax.fori_loop(ngroup * group, ntile_all, blank_tile, 0)


def _dsa_select(qit, wht, kix, ffn_weights, layer, batch, seq, ksel):
    nqb = seq // Q_BLOCK
    nkt = seq // LANES
    w_in, w_out, w_shape = _cast_specs([(w, layer) for w in ffn_weights], lambda b, q: b * nqb + q,
                                       batch * nqb)
    return pl.pallas_call(
        functools.partial(_dsa_select_kernel, ksel=ksel, idx_bits=int(math.log2(seq))),
        grid=(batch, nqb),
        in_specs=[
            pl.BlockSpec((LANES, IDX_HEADS * Q_BLOCK), lambda b, q: (0, b * nqb + q)),
            pl.BlockSpec((IDX_HEADS, Q_BLOCK), lambda b, q: (0, b * nqb + q)),
            pl.BlockSpec((seq, LANES), lambda b, q: (b, 0)),
            *w_in,
        ],
        out_specs=[pl.BlockSpec((1, 1, nkt, Q_BLOCK, LANES), lambda b, q: (b, q, 0, 0, 0)), *w_out],
        out_shape=[jax.ShapeDtypeStruct((batch, nqb, nkt, Q_BLOCK, LANES), BF16), *w_shape],
        scratch_shapes=[pltpu.VMEM((seq, Q_BLOCK), I32), pltpu.VMEM((8, Q_BLOCK), I32),
                        pltpu.VMEM((seq, Q_BLOCK), I32)],
        compiler_params=_params(("arbitrary", "arbitrary")),
        name="dsa_select",
    )(qit, wht, kix, *ffn_weights)


def _rel_bucket(rel):
    nb = REL_BUCKETS // 2
    max_exact = nb // 2
    ret = jnp.where(rel > 0, nb, 0)
    n = jnp.abs(rel)
    nf = jnp.maximum(n, 1).astype(F32)
    large = max_exact + (jnp.log(nf / max_exact) / math.log(REL_MAX_DIST / max_exact)
                         * (nb - max_exact)).astype(I32)
    large = jnp.minimum(large, nb - 1)
    return ret + jnp.where(n < max_exact, n, large)


NEAR_TILES = 3


def _dsa_attn_kernel(q_ref, kv_ref, mask_ref, rb_ref, wuv_ref, o_ref,
                     m_ref, l_ref, alpha_ref, acc_ref, corr_ref, s_ref, p_ref, madd_ref):
    b, qb = pl.program_id(0), pl.program_id(1)
    sub = KV_TILE // LANES
    far_bucket = REL_BUCKETS // 2 - 1
    half = DSA_HEADS * Q_BLOCK // 2

    def head_rows(h):
        return slice(h * Q_BLOCK, (h + 1) * Q_BLOCK)

    @pl.when((b == 0) & (qb == 0))
    def _():
        ti = lax.broadcasted_iota(I32, (Q_BLOCK, LANES), 0)
        si = lax.broadcasted_iota(I32, (Q_BLOCK, LANES), 1)
        for oi in range(NEAR_TILES):
            bucket = _rel_bucket((oi - (NEAR_TILES - 1)) * LANES + si - ti)
            for h in range(DSA_HEADS):
                tbl = jnp.zeros((Q_BLOCK, LANES), F32)
                for bk in range(REL_BUCKETS):
                    tbl = jnp.where(bucket == bk, rb_ref[bk, h], tbl)
                corr_ref[oi, head_rows(h), :] = (tbl - rb_ref[far_bucket, h]) * LOG2E

    m_ref[...] = jnp.full_like(m_ref, NEG_BIG)
    l_ref[...] = jnp.zeros_like(l_ref)
    acc_ref[...] = jnp.zeros_like(acc_ref)
    q_all = q_ref[0].reshape(DSA_HEADS * Q_BLOCK, DSA_KV_RANK)

    def key_step(kt, carry):
        kvt = kv_ref[pl.ds(pl.multiple_of(kt * KV_TILE, KV_TILE), KV_TILE), :]
        for part in range(2):
            rs = slice(part * half, (part + 1) * half)
            s_ref[rs, :] = lax.dot_general(q_all[rs], kvt, _NT, preferred_element_type=F32)
        for j in range(sub):
            d = kt * sub + j - qb

            @pl.when((d > -NEAR_TILES) & (d <= 0))
            def _(j=j, d=d):
                s_ref[:, j * LANES:(j + 1) * LANES] += corr_ref[d + NEAR_TILES - 1]

        for j in range(sub):
            madd_ref[:, j * LANES:(j + 1) * LANES] = mask_ref[0, 0, kt * sub + j].astype(F32)
        groups_per_head = Q_BLOCK // SM_ROWS
        for g in range(DSA_HEADS * groups_per_head):
            rs = slice(g * SM_ROWS, (g + 1) * SM_ROWS)
            qg = g % groups_per_head
            s = s_ref[rs, :] + madd_ref[qg * SM_ROWS:(qg + 1) * SM_ROWS, :]
            m_old = m_ref[rs, :]
            m_new = jnp.maximum(m_old, jnp.max(s, axis=1, keepdims=True))
            alpha = jnp.exp2(m_old - m_new)
            p = jnp.exp2(s - jnp.tile(m_new, (1, sub)))
            l_ref[rs, :] = alpha * l_ref[rs, :] + jnp.sum(p, axis=1, keepdims=True)
            alpha_ref[rs, :] = alpha
            p_ref[rs, :] = p.astype(BF16)
            m_ref[rs, :] = m_new
        for part in range(2):
            rs = slice(part * half, (part + 1) * half)
            pv = jnp.dot(p_ref[rs, :], kvt, preferred_element_type=F32)
            acc_ref[rs, :] = jnp.tile(alpha_ref[rs, :], (1, DSA_KV_RANK // LANES)) * acc_ref[rs, :] + pv
        return carry

    lax.fori_loop(0, qb // sub + 1, key_step, 0)

    dv = wuv_ref.shape[2]
    for h in range(DSA_HEADS):
        rs = head_rows(h)
        o = (acc_ref[rs, :] / jnp.tile(l_ref[rs, :], (1, DSA_KV_RANK // LANES))).astype(BF16)
        o_ref[:, h * dv:(h + 1) * dv] = jnp.dot(o, wuv_ref[h], preferred_element_type=F32).astype(BF16)


def _dsa_attn(q, kv, mask, rel_bias, w_uv, batch, seq):
    n = kv.shape[0]
    nqb = seq // Q_BLOCK
    dv = w_uv.shape[2]
    rows = DSA_HEADS * Q_BLOCK
    return pl.pallas_call(
        _dsa_attn_kernel,
        grid=(batch, nqb),
        in_specs=[
            pl.BlockSpec((1, DSA_HEADS, Q_BLOCK, DSA_KV_RANK), lambda b, qb: (b * nqb + qb, 0, 0, 0)),
            pl.BlockSpec((seq, DSA_KV_RANK), lambda b, qb: (b, 0)),
            pl.BlockSpec((1, 1) + mask.shape[2:], lambda b, qb: (b, qb, 0, 0, 0)),
            pl.BlockSpec(memory_space=pltpu.SMEM),
            pl.BlockSpec(w_uv.shape, lambda b, qb: (0, 0, 0)),
        ],
        out_specs=pl.BlockSpec((Q_BLOCK, DSA_HEADS * dv), lambda b, qb: (b * nqb + qb, 0)),
        out_shape=jax.ShapeDtypeStruct((n, DSA_HEADS * dv), BF16),
        scratch_shapes=[
            pltpu.VMEM((rows, LANES), F32),
            pltpu.VMEM((rows, LANES), F32),
            pltpu.VMEM((rows, LANES), F32),
            pltpu.VMEM((rows, DSA_KV_RANK), F32),
            pltpu.VMEM((NEAR_TILES, rows, LANES), F32),
            pltpu.VMEM((rows, KV_TILE), F32),
            pltpu.VMEM((rows, KV_TILE), BF16),
            pltpu.VMEM((Q_BLOCK, KV_TILE), F32),
        ],
        compiler_params=_params(("arbitrary", "arbitrary")),
        name="dsa_attn",
    )(q, kv, mask, rel_bias, w_uv)


def _pad_cols(w, width):
    return jnp.pad(w, ((0, 0), (0, width - w.shape[1])))


def kernel(x, ln_mix_g, ln_ffn_g, w_ffn_gate, w_ffn_up, w_ffn_down, rel_bias, ev_w_in, ev_w_out, sgu_ln_g, sgu_ln_b, sgu_w_s, sgu_b_s, od_w_in, od_w_out, hgrn_lb, hgrn_norm_g, dsa_cq_g, dsa_ckv_g, dsa_w_uq, dsa_qnorm_g, dsa_w_qidx, dsa_w_uv):
    batch, seq, d = x.shape
    n = batch * seq
    depth = ln_mix_g.shape[0]
    ksel = min(TOPK_MAX, seq // 4)
    tm = min(PROJ_ROWS, n)
    xf = x.reshape(n, d)
    ffn_f32 = (w_ffn_gate, w_ffn_up, w_ffn_down)
    ev_in_all, od_in_all = ev_w_in.astype(BF16), od_w_in.astype(BF16)
    for layer in range(depth):
        j = layer // 2
        if layer % 2 == 0:
            z, *ffn_w = _norm_matmul(xf, ln_mix_g[layer], ev_in_all, j, ev_in_all.shape[2], tm=tm,
                                     tn=PROJ_COLS, out_dtype=BF16, cast=[(w, layer) for w in ffn_f32])
            a1, a2 = _even_mixers(z, sgu_ln_g[j], sgu_ln_b[j], sgu_w_s[j], sgu_b_s[j], batch, seq)
            w_out = ev_w_out[j]
        else:
            w_in = od_w_in[j]
            gw = d // 2
            c = 4 * gw
            c_kidx = c + DSA_Q_RANK + DSA_KV_RANK
            w_dsa = jnp.concatenate([
                w_in[:, c:c_kidx],
                _pad_cols(w_in[:, c_kidx:c_kidx + IDX_DIM], LANES),
                _pad_cols(w_in[:, c_kidx + IDX_DIM:], LANES),
            ], axis=1).astype(BF16)
            a1, zd = _inproj_hgrn(xf, ln_mix_g[layer], od_in_all, j, w_dsa, hgrn_lb, hgrn_norm_g[j],
                                  batch, seq, layer)
            w_qit = jnp.pad(dsa_w_qidx[j].T.reshape(IDX_HEADS, IDX_DIM, DSA_Q_RANK),
                            ((0, 0), (0, LANES - IDX_DIM), (0, 0))).reshape(IDX_HEADS * LANES, DSA_Q_RANK)
            q, qit, kv, kix, wht = _dsa_prep(zd, dsa_cq_g[j], dsa_ckv_g[j], dsa_w_uq[j].astype(BF16),
                                             dsa_qnorm_g[j], w_qit.astype(BF16), tm=min(PREP_ROWS, n))
            mask, *ffn_w = _dsa_select(qit, wht, kix, ffn_f32, layer, batch, seq, ksel)
            a2 = _dsa_attn(q, kv, mask, rel_bias, dsa_w_uv[j].astype(BF16), batch, seq)
            w_out = od_w_out[j]
        xf = _outproj(a1, a2, w_out.astype(BF16), xf, tm=min(OUT_ROWS, n), tn=OUT_COLS)
        xf = _ffn(xf, ln_ffn_g[layer], *(w[None] for w in ffn_w), 0, tm=min(FFN_ROWS, n), tf=FFN_COLS)
    return xf.reshape(batch, seq, d)
```
